```python
import jax, jax.numpy as jnp
from jax import lax
import numpy as np

D_MODEL = 1024
BATCH = 16
SEQ = 4096
DEPTH = 1

PLE_DIM = 256
N_HEADS = 16
N_KV_HEADS = 2
HEAD_DIM = 64
GROUP = N_HEADS // N_KV_HEADS
ATTN_WIDTH = N_HEADS * HEAD_DIM
KV_WIDTH = N_KV_HEADS * HEAD_DIM
CONV_WIDTH = D_MODEL
CONV_KERNEL = 31
WINDOW = 128
BLOCK = 128
ROPE_DIM = HEAD_DIM // 4
ROPE_THETA = 500000.0
N_BRANCH = 2
EPS = 1e-6
MAX_POS_OFFSET = 1024

COL_SIZES = (CONV_WIDTH, CONV_WIDTH, CONV_WIDTH,
             ATTN_WIDTH, KV_WIDTH, KV_WIDTH, ATTN_WIDTH,
             D_MODEL, D_MODEL)
IN_WIDTH = sum(COL_SIZES)

kernel_name = "hybrid_conformer_conv_swa_sink_gated_merge"


def rmsnorm(x, g):
    xf = x.astype(jnp.float32)
    y = xf * lax.rsqrt(jnp.mean(xf * xf, axis=-1, keepdims=True) + EPS)
    return (y * g.astype(jnp.float32)).astype(x.dtype)


def layernorm(x, g, b):
    xf = x.astype(jnp.float32)
    mu = jnp.mean(xf, axis=-1, keepdims=True)
    var = jnp.mean(jnp.square(xf - mu), axis=-1, keepdims=True)
    y = (xf - mu) * lax.rsqrt(var + EPS)
    return (y * g.astype(jnp.float32) + b.astype(jnp.float32)).astype(x.dtype)


def split_columns(z):
    outs, start = [], 0
    for size in COL_SIZES:
        outs.append(z[..., start:start + size])
        start += size
    return outs


def conformer_conv(val, glu_gate, w_dw, b_dw, ln_g, ln_b, w_pw):
    u = val * jax.nn.sigmoid(glu_gate)
    c = lax.conv_general_dilated(
        u, w_dw[:, None, :].astype(u.dtype), window_strides=(1,),
        padding=[(CONV_KERNEL - 1, 0)],
        dimension_numbers=('NWC', 'WIO', 'NWC'),
        feature_group_count=CONV_WIDTH) + b_dw
    c = jax.nn.silu(layernorm(c, ln_g, ln_b))
    return c @ w_pw


def rope_tables(positions, dtype):
    inv = jnp.power(ROPE_THETA, -jnp.arange(0, ROPE_DIM, 2, dtype=jnp.float32) / ROPE_DIM)
    ang = positions.astype(jnp.float32)[..., None] * inv
    return jnp.cos(ang)[:, :, None, :].astype(dtype), jnp.sin(ang)[:, :, None, :].astype(dtype)


def partial_rope(t, cos, sin):
    half = ROPE_DIM // 2
    t1 = t[..., :half]
    t2 = t[..., half:ROPE_DIM]
    return jnp.concatenate([t1 * cos - t2 * sin, t2 * cos + t1 * sin, t[..., ROPE_DIM:]], axis=-1)


def sliding_window_sink_attention(q, k, v, sinks, positions):
    B, S = q.shape[0], q.shape[1]
    nb = S // BLOCK
    q = q.reshape(B, S, N_HEADS, HEAD_DIM)
    k = k.reshape(B, S, N_KV_HEADS, HEAD_DIM)
    v = v.reshape(B, S, N_KV_HEADS, HEAD_DIM)
    cos, sin = rope_tables(positions, q.dtype)
    q = partial_rope(q, cos, sin)
    k = partial_rope(k, cos, sin)
    q = q.reshape(B, nb, BLOCK, N_KV_HEADS, GROUP, HEAD_DIM)
    k = k.reshape(B, nb, BLOCK, N_KV_HEADS, HEAD_DIM)
    v = v.reshape(B, nb, BLOCK, N_KV_HEADS, HEAD_DIM)
    pad = ((0, 0), (1, 0), (0, 0), (0, 0), (0, 0))
    kb = jnp.concatenate([jnp.pad(k[:, :-1], pad), k], axis=2)
    vb = jnp.concatenate([jnp.pad(v[:, :-1], pad), v], axis=2)
    s = jnp.einsum('bnqkgd,bnskd->bnkgqs', q, kb).astype(jnp.float32) * (HEAD_DIM ** -0.5)
    qi = jnp.arange(BLOCK)[:, None]
    sj = jnp.arange(2 * BLOCK)[None, :]
    band = (sj <= qi + BLOCK) & (sj > qi + BLOCK - WINDOW)
    blk = jnp.arange(nb)[:, None, None]
    mask = band[None] & ((blk > 0) | (sj[None] >= BLOCK))
    s = jnp.where(mask[None, :, None, None], s, -jnp.inf)
    sink = jnp.broadcast_to(sinks.astype(jnp.float32).reshape(1, 1, N_KV_HEADS, GROUP, 1, 1),
                            s.shape[:-1] + (1,))
    probs = jax.nn.softmax(jnp.concatenate([s, sink], axis=-1), axis=-1)[..., :-1]
    o = jnp.einsum('bnkgqs,bnskd->bnqkgd', probs.astype(vb.dtype), vb)
    return o.reshape(B, S, ATTN_WIDTH)


def _fwd_setup_inputs(seed: int = 0) -> dict:
    key = jax.random.key(seed)
    ks = jax.random.split(key, 20)
    f32 = jnp.float32
    x = jax.random.normal(ks[0], (BATCH, SEQ, D_MODEL), f32)
    p = jax.random.normal(ks[1], (DEPTH, BATCH, SEQ, PLE_DIM), f32)
    offsets = jax.random.randint(ks[2], (BATCH, 1), 0, MAX_POS_OFFSET, dtype=jnp.int32)
    positions = offsets + jnp.arange(SEQ, dtype=jnp.int32)[None, :]
    w_in = jax.random.normal(ks[3], (DEPTH, D_MODEL, IN_WIDTH), f32) * D_MODEL ** -0.5
    ln_pre = 1.0 + 0.05 * jax.random.normal(ks[4], (DEPTH, D_MODEL), f32)
    ln_post = 1.0 + 0.05 * jax.random.normal(ks[5], (DEPTH, D_MODEL), f32)
    w_dw = jax.random.normal(ks[6], (DEPTH, CONV_KERNEL, CONV_WIDTH), f32) * CONV_KERNEL ** -0.5
    b_dw = 0.02 * jax.random.normal(ks[7], (DEPTH, CONV_WIDTH), f32)
    conv_ln_g = 1.0 + 0.05 * jax.random.normal(ks[8], (DEPTH, CONV_WIDTH), f32)
    conv_ln_b = 0.02 * jax.random.normal(ks[9], (DEPTH, CONV_WIDTH), f32)
    w_pw = jax.random.normal(ks[10], (DEPTH, CONV_WIDTH, CONV_WIDTH), f32) * CONV_WIDTH ** -0.5
    sinks = 0.5 * jax.random.normal(ks[11], (DEPTH, N_HEADS), f32)
    w_br_conv = jax.random.normal(ks[12], (DEPTH, CONV_WIDTH, D_MODEL), f32) * CONV_WIDTH ** -0.5
    w_br_attn = jax.random.normal(ks[13], (DEPTH, ATTN_WIDTH, D_MODEL), f32) * ATTN_WIDTH ** -0.5
    w_out = jax.random.normal(ks[14], (DEPTH, D_MODEL, D_MODEL), f32) * D_MODEL ** -0.5
    w_ple_gate = jax.random.normal(ks[15], (DEPTH, D_MODEL, D_MODEL), f32) * D_MODEL ** -0.5
    w_ple_proj = jax.random.normal(ks[16], (DEPTH, PLE_DIM, D_MODEL), f32) * PLE_DIM ** -0.5
    return {"x": x, "p": p, "positions": positions, "w_in": w_in, "ln_pre": ln_pre,
            "ln_post": ln_post, "w_dw": w_dw, "b_dw": b_dw, "conv_ln_g": conv_ln_g,
            "conv_ln_b": conv_ln_b, "w_pw": w_pw, "sinks": sinks, "w_br_conv": w_br_conv,
            "w_br_attn": w_br_attn, "w_out": w_out, "w_ple_gate": w_ple_gate,
            "w_ple_proj": w_ple_proj}


def _fwd_reference(x, p, positions, w_in, ln_pre, ln_post, w_dw, b_dw, conv_ln_g, conv_ln_b,
              w_pw, sinks, w_br_conv, w_br_attn, w_out, w_ple_gate, w_ple_proj):
    for i in range(DEPTH):
        h = rmsnorm(x, ln_pre[i])
        z = h @ w_in[i]
        (c_val, c_glu, c_gate, q, k, v, a_gate, g_conv, g_attn) = split_columns(z)
        ya = conformer_conv(c_val, c_glu, w_dw[i], b_dw[i], conv_ln_g[i], conv_ln_b[i], w_pw[i])
        ya = (ya * jax.nn.silu(c_gate)) @ w_br_conv[i]
        yb = sliding_window_sink_attention(q, k, v, sinks[i], positions)
        yb = (yb * jax.nn.silu(a_gate)) @ w_br_attn[i]
        m = jax.nn.sigmoid(g_conv) * ya + jax.nn.sigmoid(g_attn) * yb
        x = x + rmsnorm(m @ w_out[i], ln_post[i])
        x = x + jax.nn.sigmoid(x @ w_ple_gate[i]) * (p[i] @ w_ple_proj[i])
    return x


import jax as _jax
import jax.numpy as _jnp

TWIN_FORMAT = 'train_step'
FWD_PARAMS = ['x', 'p', 'positions', 'w_in', 'ln_pre', 'ln_post', 'w_dw', 'b_dw', 'conv_ln_g', 'conv_ln_b', 'w_pw', 'sinks', 'w_br_conv', 'w_br_attn', 'w_out', 'w_ple_gate', 'w_ple_proj']
TWIN_WEIGHTS = ['w_in', 'ln_pre', 'ln_post', 'w_dw', 'b_dw', 'conv_ln_g', 'conv_ln_b', 'w_pw', 'sinks', 'w_br_conv', 'w_br_attn', 'w_out', 'w_ple_gate', 'w_ple_proj']
TWIN_DIFF_INPUT = 'x'
TWIN_INPUTS = ['x', 'p', 'positions', 'w_in', 'ln_pre', 'ln_post', 'w_dw', 'b_dw', 'conv_ln_g', 'conv_ln_b', 'w_pw', 'sinks', 'w_br_conv', 'w_br_attn', 'w_out', 'w_ple_gate', 'w_ple_proj', 'loss_target', 'm_w_in', 'm_ln_pre', 'm_ln_post', 'm_w_dw', 'm_b_dw', 'm_conv_ln_g', 'm_conv_ln_b', 'm_w_pw', 'm_sinks', 'm_w_br_conv', 'm_w_br_attn', 'm_w_out', 'm_w_ple_gate', 'm_w_ple_proj', 'v_w_in', 'v_ln_pre', 'v_ln_post', 'v_w_dw', 'v_b_dw', 'v_conv_ln_g', 'v_conv_ln_b', 'v_w_pw', 'v_sinks', 'v_w_br_conv', 'v_w_br_attn', 'v_w_out', 'v_w_ple_gate', 'v_w_ple_proj']
TWIN_OUTPUTS = ['loss', 'grad_x', 'grad_w_in', 'grad_ln_pre', 'grad_ln_post', 'grad_w_dw', 'grad_b_dw', 'grad_conv_ln_g', 'grad_conv_ln_b', 'grad_w_pw', 'grad_sinks', 'grad_w_br_conv', 'grad_w_br_attn', 'grad_w_out', 'grad_w_ple_gate', 'grad_w_ple_proj', 'delta_w_in', 'delta_ln_pre', 'delta_ln_post', 'delta_w_dw', 'delta_b_dw', 'delta_conv_ln_g', 'delta_conv_ln_b', 'delta_w_pw', 'delta_sinks', 'delta_w_br_conv', 'delta_w_br_attn', 'delta_w_out', 'delta_w_ple_gate', 'delta_w_ple_proj', 'new_m_w_in', 'new_m_ln_pre', 'new_m_ln_post', 'new_m_w_dw', 'new_m_b_dw', 'new_m_conv_ln_g', 'new_m_conv_ln_b', 'new_m_w_pw', 'new_m_sinks', 'new_m_w_br_conv', 'new_m_w_br_attn', 'new_m_w_out', 'new_m_w_ple_gate', 'new_m_w_ple_proj', 'new_v_w_in', 'new_v_ln_pre', 'new_v_ln_post', 'new_v_w_dw', 'new_v_b_dw', 'new_v_conv_ln_g', 'new_v_conv_ln_b', 'new_v_w_pw', 'new_v_sinks', 'new_v_w_br_conv', 'new_v_w_br_attn', 'new_v_w_out', 'new_v_w_ple_gate', 'new_v_w_ple_proj']
TWIN_LEAF_KINDS = {'loss': 'loss', 'grad_x': 'grad_x', 'grad_w_in': 'grad_w', 'grad_ln_pre': 'grad_w', 'grad_ln_post': 'grad_w', 'grad_w_dw': 'grad_w', 'grad_b_dw': 'grad_w', 'grad_conv_ln_g': 'grad_w', 'grad_conv_ln_b': 'grad_w', 'grad_w_pw': 'grad_w', 'grad_sinks': 'grad_w', 'grad_w_br_conv': 'grad_w', 'grad_w_br_attn': 'grad_w', 'grad_w_out': 'grad_w', 'grad_w_ple_gate': 'grad_w', 'grad_w_ple_proj': 'grad_w', 'delta_w_in': 'delta_w', 'delta_ln_pre': 'delta_w', 'delta_ln_post': 'delta_w', 'delta_w_dw': 'delta_w', 'delta_b_dw': 'delta_w', 'delta_conv_ln_g': 'delta_w', 'delta_conv_ln_b': 'delta_w', 'delta_w_pw': 'delta_w', 'delta_sinks': 'delta_w', 'delta_w_br_conv': 'delta_w', 'delta_w_br_attn': 'delta_w', 'delta_w_out': 'delta_w', 'delta_w_ple_gate': 'delta_w', 'delta_w_ple_proj': 'delta_w', 'new_m_w_in': 'new_m', 'new_m_ln_pre': 'new_m', 'new_m_ln_post': 'new_m', 'new_m_w_dw': 'new_m', 'new_m_b_dw': 'new_m', 'new_m_conv_ln_g': 'new_m', 'new_m_conv_ln_b': 'new_m', 'new_m_w_pw': 'new_m', 'new_m_sinks': 'new_m', 'new_m_w_br_conv': 'new_m', 'new_m_w_br_attn': 'new_m', 'new_m_w_out': 'new_m', 'new_m_w_ple_gate': 'new_m', 'new_m_w_ple_proj': 'new_m', 'new_v_w_in': 'new_v', 'new_v_ln_pre': 'new_v', 'new_v_ln_post': 'new_v', 'new_v_w_dw': 'new_v', 'new_v_b_dw': 'new_v', 'new_v_conv_ln_g': 'new_v', 'new_v_conv_ln_b': 'new_v', 'new_v_w_pw': 'new_v', 'new_v_sinks': 'new_v', 'new_v_w_br_conv': 'new_v', 'new_v_w_br_attn': 'new_v', 'new_v_w_out': 'new_v', 'new_v_w_ple_gate': 'new_v', 'new_v_w_ple_proj': 'new_v'}


def _forward(args):
    return _fwd_reference(*[args[k] for k in FWD_PARAMS])


def _output_shape():
    out = _jax.eval_shape(lambda: _forward(_fwd_setup_inputs(0)))
    return out.shape, out.dtype

N_MICROBATCH = 1
ADAM_LR = 0.001
ADAM_B1 = 0.9
ADAM_B2 = 0.999
ADAM_EPS = 1e-08
ADAM_WD = 0.01
ADAM_STEP = 10
PER_EXAMPLE_BATCH_AXIS = {'x': 0, 'p': 1, 'positions': 0, 'loss_target': 0}
SHARED_INPUTS = []
_WEIGHT_DTYPES = {'w_in': _jnp.float32, 'ln_pre': _jnp.float32, 'ln_post': _jnp.float32, 'w_dw': _jnp.float32, 'b_dw': _jnp.float32, 'conv_ln_g': _jnp.float32, 'conv_ln_b': _jnp.float32, 'w_pw': _jnp.float32, 'sinks': _jnp.float32, 'w_br_conv': _jnp.float32, 'w_br_attn': _jnp.float32, 'w_out': _jnp.float32, 'w_ple_gate': _jnp.float32, 'w_ple_proj': _jnp.float32}
MOMENT_SCALE = {'w_in': 2.458777e-01, 'ln_pre': 6.622606e-01, 'ln_post': 6.521595e+01, 'w_dw': 4.228865e-01, 'b_dw': 2.024845e+00, 'conv_ln_g': 8.536594e-01, 'conv_ln_b': 1.233241e+00, 'w_pw': 5.527938e-01, 'sinks': 7.727743e-02, 'w_br_conv': 5.411086e-01, 'w_br_attn': 1.253103e-01, 'w_out': 5.420549e-01, 'w_ple_gate': 3.473425e-01, 'w_ple_proj': 7.435581e-01}


def _to_microbatches(a, axis):
    t = _jnp.moveaxis(a, axis, 0)
    t = t.reshape((N_MICROBATCH, t.shape[0] // N_MICROBATCH) + t.shape[1:])
    return _jnp.moveaxis(t, 1, axis + 1)


def setup_inputs(seed: int = 0) -> dict:
    inp = _fwd_setup_inputs(seed)
    key = _jax.random.fold_in(_jax.random.key(seed), 7919)
    shape, _ = _output_shape()
    out = dict(inp)
    out["loss_target"] = _jax.random.normal(_jax.random.fold_in(key, 0), shape, _jnp.float32)
    for i, name in enumerate(TWIN_WEIGHTS):
        w = inp[name].astype(_jnp.float32)
        if MOMENT_SCALE is None:
            s = _jnp.sqrt(_jnp.mean(_jnp.square(w)) + 1e-30)
        else:
            s = MOMENT_SCALE[name]
        km, kv = _jax.random.split(_jax.random.fold_in(key, i + 1))
        out[name] = w
        out["m_" + name] = s * _jax.random.normal(km, w.shape, _jnp.float32)
        out["v_" + name] = (s * s) * _jax.random.uniform(kv, w.shape, _jnp.float32, 0.5, 1.5)
    if N_MICROBATCH > 1:
        for name, axis in PER_EXAMPLE_BATCH_AXIS.items():
            out[name] = _to_microbatches(out[name], axis)
    return {'x': out['x'], 'p': out['p'], 'positions': out['positions'], 'w_in': out['w_in'], 'ln_pre': out['ln_pre'], 'ln_post': out['ln_post'], 'w_dw': out['w_dw'], 'b_dw': out['b_dw'], 'conv_ln_g': out['conv_ln_g'], 'conv_ln_b': out['conv_ln_b'], 'w_pw': out['w_pw'], 'sinks': out['sinks'], 'w_br_conv': out['w_br_conv'], 'w_br_attn': out['w_br_attn'], 'w_out': out['w_out'], 'w_ple_gate': out['w_ple_gate'], 'w_ple_proj': out['w_ple_proj'], 'loss_target': out['loss_target'], 'm_w_in': out['m_w_in'], 'm_ln_pre': out['m_ln_pre'], 'm_ln_post': out['m_ln_post'], 'm_w_dw': out['m_w_dw'], 'm_b_dw': out['m_b_dw'], 'm_conv_ln_g': out['m_conv_ln_g'], 'm_conv_ln_b': out['m_conv_ln_b'], 'm_w_pw': out['m_w_pw'], 'm_sinks': out['m_sinks'], 'm_w_br_conv': out['m_w_br_conv'], 'm_w_br_attn': out['m_w_br_attn'], 'm_w_out': out['m_w_out'], 'm_w_ple_gate': out['m_w_ple_gate'], 'm_w_ple_proj': out['m_w_ple_proj'], 'v_w_in': out['v_w_in'], 'v_ln_pre': out['v_ln_pre'], 'v_ln_post': out['v_ln_post'], 'v_w_dw': out['v_w_dw'], 'v_b_dw': out['v_b_dw'], 'v_conv_ln_g': out['v_conv_ln_g'], 'v_conv_ln_b': out['v_conv_ln_b'], 'v_w_pw': out['v_w_pw'], 'v_sinks': out['v_sinks'], 'v_w_br_conv': out['v_w_br_conv'], 'v_w_br_attn': out['v_w_br_attn'], 'v_w_out': out['v_w_out'], 'v_w_ple_gate': out['v_w_ple_gate'], 'v_w_ple_proj': out['v_w_ple_proj']}


def _loss(weights, diff, rest, loss_target):
    with _jax.named_scope("forward"):
        args = {**rest, TWIN_DIFF_INPUT: diff, **{k: w.astype(_WEIGHT_DTYPES[k]) for k, w in weights.items()}}
        y = _forward(args)
    with _jax.named_scope("loss_head"):
        err = _jnp.square(y.astype(_jnp.float32) - loss_target)
        return 0.5 * _jnp.sum(_jnp.mean(err, axis=-1)) if err.ndim else 0.5 * err


def _adamw(w, g, m, v):
    m = ADAM_B1 * m + (1.0 - ADAM_B1) * g
    v = ADAM_B2 * v + (1.0 - ADAM_B2) * _jnp.square(g)
    m_hat = m / (1.0 - ADAM_B1 ** ADAM_STEP)
    v_hat = v / (1.0 - ADAM_B2 ** ADAM_STEP)
    delta = -ADAM_LR * (m_hat / (_jnp.sqrt(v_hat) + ADAM_EPS) + ADAM_WD * w)
    return delta, m, v


def reference(x, p, positions, w_in, ln_pre, ln_post, w_dw, b_dw, conv_ln_g, conv_ln_b, w_pw, sinks, w_br_conv, w_br_attn, w_out, w_ple_gate, w_ple_proj, loss_target, m_w_in, m_ln_pre, m_ln_post, m_w_dw, m_b_dw, m_conv_ln_g, m_conv_ln_b, m_w_pw, m_sinks, m_w_br_conv, m_w_br_attn, m_w_out, m_w_ple_gate, m_w_ple_proj, v_w_in, v_ln_pre, v_ln_post, v_w_dw, v_b_dw, v_conv_ln_g, v_conv_ln_b, v_w_pw, v_sinks, v_w_br_conv, v_w_br_attn, v_w_out, v_w_ple_gate, v_w_ple_proj):
    given = dict(x=x, p=p, positions=positions, w_in=w_in, ln_pre=ln_pre, ln_post=ln_post, w_dw=w_dw, b_dw=b_dw, conv_ln_g=conv_ln_g, conv_ln_b=conv_ln_b, w_pw=w_pw, sinks=sinks, w_br_conv=w_br_conv, w_br_attn=w_br_attn, w_out=w_out, w_ple_gate=w_ple_gate, w_ple_proj=w_ple_proj, loss_target=loss_target, m_w_in=m_w_in, m_ln_pre=m_ln_pre, m_ln_post=m_ln_post, m_w_dw=m_w_dw, m_b_dw=m_b_dw, m_conv_ln_g=m_conv_ln_g, m_conv_ln_b=m_conv_ln_b, m_w_pw=m_w_pw, m_sinks=m_sinks, m_w_br_conv=m_w_br_conv, m_w_br_attn=m_w_br_attn, m_w_out=m_w_out, m_w_ple_gate=m_w_ple_gate, m_w_ple_proj=m_w_ple_proj, v_w_in=v_w_in, v_ln_pre=v_ln_pre, v_ln_post=v_ln_post, v_w_dw=v_w_dw, v_b_dw=v_b_dw, v_conv_ln_g=v_conv_ln_g, v_conv_ln_b=v_conv_ln_b, v_w_pw=v_w_pw, v_sinks=v_sinks, v_w_br_conv=v_w_br_conv, v_w_br_attn=v_w_br_attn, v_w_out=v_w_out, v_w_ple_gate=v_w_ple_gate, v_w_ple_proj=v_w_ple_proj)
    weights = {n: given[n] for n in TWIN_WEIGHTS}
    shared = {n: given[n] for n in SHARED_INPUTS}
    per_example = {n: given[n] for n in ['x', 'p', 'positions']}
    grad_fn = _jax.value_and_grad(_loss, argnums=(0, 1))

    def one_microbatch(ex, loss_target):
        ex = dict(ex)
        diff = ex.pop(TWIN_DIFF_INPUT)
        return grad_fn(weights, diff, {**shared, **ex}, loss_target)

    if N_MICROBATCH == 1:
        loss, (grad_w, grad_x) = one_microbatch(per_example, given["loss_target"])
    else:
        def body(carry, xs):
            loss_sum, grad_sum = carry
            l_k, (gw_k, gx_k) = one_microbatch(xs[0], xs[1])
            with _jax.named_scope("update"):
                return (loss_sum + l_k, _jax.tree.map(_jnp.add, grad_sum, gw_k)), gx_k

        init = (_jnp.zeros((), _jnp.float32), _jax.tree.map(_jnp.zeros_like, weights))
        (loss, grad_w), grad_x = _jax.lax.scan(body, init, (per_example, given["loss_target"]))
    with _jax.named_scope("update"):
        delta_w, new_m, new_v = {}, {}, {}
        for n in TWIN_WEIGHTS:
            delta_w[n], new_m[n], new_v[n] = _adamw(weights[n], grad_w[n], given["m_" + n], given["v_" + n])
    return (loss, grad_x, *[grad_w[n] for n in TWIN_WEIGHTS], *[delta_w[n] for n in TWIN_WEIGHTS],
            *[new_m[n] for n in TWIN_WEIGHTS], *[new_v[n] for n in TWIN_WEIGHTS])
```

```python
import functools

import jax
import jax.numpy as jnp
import numpy as np
from jax import lax
from jax.experimental import pallas as pl
from jax.experimental.pallas import tpu as pltpu

F32 = jnp.float32
BF16 = jnp.bfloat16

D = 1024
PLE = 256
N_HEADS = 16
HEAD_DIM = 64
BLOCK = 128
CONV_K = 31
ROPE_DIM = 16
ROPE_THETA = 500000.0
EPS = 1e-6
IN_WIDTH = 7424
N_SHARDS = 4

ADAM_LR = 0.001
ADAM_B1 = 0.9
ADAM_B2 = 0.999
ADAM_EPS = 1e-08
ADAM_WD = 0.01
ADAM_STEP = 10

SQ_NAMES = ("w_pw", "w_br_conv", "w_br_attn", "w_out", "w_ple_gate")
WT0 = 5 * D
WPP0 = WT0 + IN_WIDTH
WALL_ROWS = WPP0 + PLE
WIN_SHARD = IN_WIDTH // N_SHARDS
SQ_SHARD = D // N_SHARDS
WPP_SHARD = PLE * PLE // D
PACK_ROWS = WIN_SHARD + 5 * SQ_SHARD + WPP_SHARD
HALF_ROWS = PACK_ROWS // 2
VMEM_LIMIT = 56 * 1024 * 1024
MESH = pl.DeviceIdType.MESH
TILE_PROJ = 1024
TILE_TOKEN = 256
TILE_ATTN = 512


def _zp_row(o):
    if o < 4096:
        return o
    if o < 4352:
        return o + 3072
    return o - 256


def _pieces(s):
    out = []
    for a, n in ((0, 384), (384, 256), (640, 1216)):
        out.append((a, n, WT0 + _zp_row(WIN_SHARD * s + a)))
    for k in range(5):
        out.append((WIN_SHARD + SQ_SHARD * k, SQ_SHARD, D * k + SQ_SHARD * s))
    out.append((WIN_SHARD + 5 * SQ_SHARD, WPP_SHARD, WPP0 + WPP_SHARD * s))
    return out


N_PIECES = len(_pieces(0))


def _sel(s, vals):
    r = jnp.int32(vals[0])
    for i in range(1, len(vals)):
        r = jnp.where(s == i, jnp.int32(vals[i]), r)
    return r


def _sig(x):
    return 1.0 / (1.0 + jnp.exp(-x))


def _mm(a, b):
    return lax.dot_general(a, b, (((1,), (0,)), ((), ())), preferred_element_type=F32)


def _mm_nt(a, b):
    return lax.dot_general(a, b, (((1,), (1,)), ((), ())), preferred_element_type=F32)


def _mm_tn(a, b):
    return lax.dot_general(a, b, (((0,), (0,)), ((), ())), preferred_element_type=F32)


def _params(sem=None):
    return pltpu.CompilerParams(dimension_semantics=sem, vmem_limit_bytes=VMEM_LIMIT)


def _flush(acc_ref, out_ref, sem):
    cp = pltpu.make_async_copy(acc_ref, out_ref, sem)
    cp.start()
    cp.wait()


def _coords():
    return lax.axis_index("x"), lax.axis_index("y"), lax.axis_index("c")


def _chip_peers(x, y):
    return [(1 - x, y), (x, 1 - y), (1 - x, 1 - y)]


def _gather_weights(pack, wdw_shard):
    tables = [[_pieces(s)[p][2] for s in range(N_SHARDS)] for p in range(N_PIECES)]
    src_rows = [(_pieces(0)[p][0], _pieces(0)[p][1]) for p in range(N_PIECES)]

    def body(pack_ref, wdw_ref, wall_ref, wdwall_ref, send_sems, recv_sems, loc_sems):
        x, y, c = _coords()
        s_me = 2 * x + y
        peers = _chip_peers(x, y)

        def dst_row(p, s):
            return pl.multiple_of(_sel(s, tables[p]), 32)

        def rcopy(src, dst, k, dev):
            return pltpu.make_async_remote_copy(
                src_ref=src, dst_ref=dst, send_sem=send_sems.at[k], recv_sem=recv_sems.at[k],
                device_id=dev, device_id_type=MESH)

        def half_bytes(k):
            rows = wall_ref.at[pl.ds(0, HALF_ROWS)]
            return rcopy(rows, rows, k, (x, y, c))

        for p in range(N_PIECES):
            a, n = src_rows[p]
            pltpu.make_async_copy(pack_ref.at[pl.ds(a, n)], wall_ref.at[pl.ds(dst_row(p, s_me), n)],
                                  loc_sems.at[0]).start()
        own_wdw = pltpu.make_async_copy(wdw_ref, wdwall_ref.at[s_me], loc_sems.at[1])
        own_wdw.start()
        wdw_sends = []
        for k, (px, py) in enumerate(peers):
            for p in range(N_PIECES):
                a, n = src_rows[p]
                h = n // 2
                rcopy(pack_ref.at[pl.ds(pl.multiple_of(a + c * h, 32), h)],
                      wall_ref.at[pl.ds(pl.multiple_of(dst_row(p, s_me) + c * h, 32), h)], k,
                      (px, py, c)).start()
            cp = rcopy(wdw_ref, wdwall_ref.at[s_me], 6 + k, (px, py, c))
            cp.start()
            wdw_sends.append(cp)
        for k, (px, py) in enumerate(peers):
            s_p = 2 * px + py
            half_bytes(k).wait_recv()
            for p in range(N_PIECES):
                n = src_rows[p][1]
                h = n // 2
                rows = wall_ref.at[pl.ds(pl.multiple_of(dst_row(p, s_p) + c * h, 32), h)]
                rcopy(rows, rows, 3 + k, (x, y, 1 - c)).start()
        for k in range(3):
            half_bytes(3 + k).wait_recv()
        for k in range(3):
            wdw_sends[k].wait_recv()
        for k in range(6):
            half_bytes(k).wait_send()
        for k in range(3):
            wdw_sends[k].wait_send()
        pltpu.make_async_copy(pack_ref, wall_ref.at[pl.ds(0, PACK_ROWS)], loc_sems.at[0]).wait()
        own_wdw.wait()

    any_spec = pl.BlockSpec(memory_space=pl.ANY)
    return pl.pallas_call(
        body, name="gather_weights",
        out_shape=(jax.ShapeDtypeStruct((WALL_ROWS, D), BF16),
                   jax.ShapeDtypeStruct((N_SHARDS, 32, PLE), F32)),
        in_specs=[any_spec, any_spec], out_specs=(any_spec, any_spec),
        scratch_shapes=[pltpu.SemaphoreType.DMA((9,)), pltpu.SemaphoreType.DMA((9,)),
                        pltpu.SemaphoreType.DMA((2,))],
    )(pack, wdw_shard)


def _exchange_halves(gpack):
    def body(g_ref, r1_ref, send_sem, recv_sem):
        x, y, c = _coords()
        cp = pltpu.make_async_remote_copy(
            src_ref=g_ref.at[:, pl.ds(pl.multiple_of((1 - c) * HALF_ROWS, 32), HALF_ROWS), :],
            dst_ref=r1_ref, send_sem=send_sem, recv_sem=recv_sem,
            device_id=(x, y, 1 - c), device_id_type=MESH)
        cp.start()
        cp.wait()

    any_spec = pl.BlockSpec(memory_space=pl.ANY)
    return pl.pallas_call(
        body, name="exchange_halves",
        out_shape=jax.ShapeDtypeStruct((N_SHARDS, HALF_ROWS, D), F32),
        in_specs=[any_spec], out_specs=any_spec,
        scratch_shapes=[pltpu.SemaphoreType.DMA, pltpu.SemaphoreType.DMA],
    )(gpack)


RT = 320


def _chip_sum(cidx, gpack, r1):
    def body(c_ref, g_ref, r_ref, o_ref):
        o_ref[...] = (g_ref[...] + r_ref[...]).astype(BF16)

    nt = HALF_ROWS // RT
    return pl.pallas_call(
        body, name="chip_sum",
        grid_spec=pltpu.PrefetchScalarGridSpec(
            num_scalar_prefetch=1, grid=(N_SHARDS, nt),
            in_specs=[pl.BlockSpec((1, RT, D), lambda s, t, c: (s, c[0] * nt + t, 0)),
                      pl.BlockSpec((1, RT, D), lambda s, t, c: (s, t, 0))],
            out_specs=pl.BlockSpec((1, RT, D), lambda s, t, c: (s, t, 0))),
        out_shape=jax.ShapeDtypeStruct((N_SHARDS, HALF_ROWS, D), BF16),
        compiler_params=_params(("arbitrary", "arbitrary")),
    )(cidx, gpack, r1)


def _send_chip_sums(cs):
    def body(cs_ref, r2_ref, send_sems, recv_sems):
        x, y, c = _coords()
        cps = []
        for k, (px, py) in enumerate(_chip_peers(x, y)):
            cp = pltpu.make_async_remote_copy(
                src_ref=cs_ref.at[2 * px + py], dst_ref=r2_ref.at[k],
                send_sem=send_sems.at[k], recv_sem=recv_sems.at[k],
                device_id=(px, py, c), device_id_type=MESH)
            cp.start()
            cps.append(cp)
        for cp in cps:
            cp.wait_recv()
        for cp in cps:
            cp.wait_send()

    any_spec = pl.BlockSpec(memory_space=pl.ANY)
    return pl.pallas_call(
        body, name="send_chip_sums",
        out_shape=jax.ShapeDtypeStruct((3, HALF_ROWS, D), BF16),
        in_specs=[any_spec], out_specs=any_spec,
        scratch_shapes=[pltpu.SemaphoreType.DMA((3,)), pltpu.SemaphoreType.DMA((3,))],
    )(cs)


def _final_half(sc, gpack, r1, r2):
    def body(sc_ref, g_ref, r_ref, p_ref, o_ref):
        acc = g_ref[0] + r_ref[0]
        for k in range(3):
            acc = acc + p_ref[k].astype(F32)
        o_ref[...] = acc

    nt = HALF_ROWS // RT
    return pl.pallas_call(
        body, name="final_half",
        grid_spec=pltpu.PrefetchScalarGridSpec(
            num_scalar_prefetch=1, grid=(nt,),
            in_specs=[pl.BlockSpec((1, RT, D), lambda t, sc: (sc[0], sc[1] * nt + t, 0)),
                      pl.BlockSpec((1, RT, D), lambda t, sc: (sc[0], t, 0)),
                      pl.BlockSpec((3, RT, D), lambda t, sc: (0, t, 0))],
            out_specs=pl.BlockSpec((RT, D), lambda t, sc: (t, 0))),
        out_shape=jax.ShapeDtypeStruct((HALF_ROWS, D), F32),
        compiler_params=_params(("arbitrary",)),
    )(sc, gpack, r1, r2)


def _join_halves(fh):
    def body(f_ref, o_ref, send_sem, recv_sem, loc_sem):
        x, y, c = _coords()
        mine = o_ref.at[pl.ds(pl.multiple_of(c * HALF_ROWS, 32), HALF_ROWS)]
        loc = pltpu.make_async_copy(f_ref, mine, loc_sem)
        loc.start()
        cp = pltpu.make_async_remote_copy(
            src_ref=f_ref, dst_ref=mine, send_sem=send_sem, recv_sem=recv_sem,
            device_id=(x, y, 1 - c), device_id_type=MESH)
        cp.start()
        cp.wait()
        loc.wait()

    any_spec = pl.BlockSpec(memory_space=pl.ANY)
    return pl.pallas_call(
        body, name="join_halves",
        out_shape=jax.ShapeDtypeStruct((PACK_ROWS, D), F32),
        in_specs=[any_spec], out_specs=any_spec,
        scratch_shapes=[pltpu.SemaphoreType.DMA, pltpu.SemaphoreType.DMA, pltpu.SemaphoreType.DMA],
    )(fh)


VEC_ROWS = 40


def _all_reduce_small(vec):
    def body(v_ref, o_ref, buf, send_sems, recv_sems):
        x, y, c = _coords()
        me = 4 * x + 2 * y + c
        buf[me] = v_ref[...]
        cps = []
        for r in range(1, 8):
            dx, dy, dc = (r >> 2) & 1, (r >> 1) & 1, r & 1
            peer = (1 - x if dx else x, 1 - y if dy else y, 1 - c if dc else c)
            cp = pltpu.make_async_remote_copy(
                src_ref=v_ref, dst_ref=buf.at[me], send_sem=send_sems.at[r - 1],
                recv_sem=recv_sems.at[r - 1], device_id=peer, device_id_type=MESH)
            cp.start()
            cps.append(cp)
        for cp in cps:
            cp.wait_recv()
        for cp in cps:
            cp.wait_send()
        acc = buf[0]
        for d in range(1, 8):
            acc = acc + buf[d]
        o_ref[...] = acc

    vm = pl.BlockSpec(memory_space=pltpu.VMEM)
    return pl.pallas_call(
        body, name="all_reduce_small",
        out_shape=jax.ShapeDtypeStruct((VEC_ROWS, D), F32),
        in_specs=[vm], out_specs=vm,
        scratch_shapes=[pltpu.VMEM((8, VEC_ROWS, D), F32), pltpu.SemaphoreType.DMA((7,)),
                        pltpu.SemaphoreType.DMA((7,))],
    )(vec)


def _inproj(x, ln_pre, wall, tm):
    T = x.shape[0]

    def body(x_ref, g_ref, w_ref, z_ref, h_ref, hs):
        @pl.when(pl.program_id(1) == 0)
        def _():
            xv = x_ref[...]
            r = lax.rsqrt(jnp.mean(xv * xv, axis=-1, keepdims=True) + EPS)
            h = (xv * r * g_ref[...]).astype(BF16)
            hs[...] = h
            h_ref[...] = h

        z_ref[...] = _mm_nt(hs[...], w_ref[...]).astype(BF16)

    return pl.pallas_call(
        body, name="inproj", grid=(T // tm, 7),
        in_specs=[pl.BlockSpec((tm, D), lambda i, j: (i, 0)),
                  pl.BlockSpec((1, D), lambda i, j: (0, 0)),
                  pl.BlockSpec((D, D), lambda i, j: (5 + j, 0))],
        out_specs=(pl.BlockSpec((tm, D), lambda i, j: (i, j)),
                   pl.BlockSpec((tm, D), lambda i, j: (i, 0))),
        out_shape=(jax.ShapeDtypeStruct((T, 7 * D), BF16), jax.ShapeDtypeStruct((T, D), BF16)),
        scratch_shapes=[pltpu.VMEM((tm, D), BF16)],
        compiler_params=_params(("arbitrary", "arbitrary")),
    )(x, ln_pre, wall)


def _kvproj(h, wall, tm):
    T = h.shape[0]

    def body(h_ref, w_ref, o_ref):
        o_ref[...] = _mm_nt(h_ref[...], w_ref[...]).astype(BF16)

    return pl.pallas_call(
        body, name="kvproj", grid=(T // tm,),
        in_specs=[pl.BlockSpec((tm, D), lambda i: (i, 0)),
                  pl.BlockSpec((2 * BLOCK, D), lambda i: ((WT0 + 7 * D) // (2 * BLOCK), 0))],
        out_specs=pl.BlockSpec((tm, 2 * BLOCK), lambda i: (i, 0)),
        out_shape=jax.ShapeDtypeStruct((T, 2 * BLOCK), BF16),
        compiler_params=_params(("arbitrary",)),
    )(h, wall)


HALO = 32
CONV_RC = 64
CONV_LC = 256


def _conv_taps(w_ref, src, r0, lane0, offset_of_tap):
    slab = src[pl.ds(r0, CONV_RC + HALO + 8), pl.ds(lane0, CONV_LC)]
    acc = jnp.zeros((CONV_RC, CONV_LC), F32)
    for b in range(8):
        taps = [k for k in range(CONV_K) if offset_of_tap(k) % 8 == b]
        if not taps:
            continue
        sh = slab[b:b + CONV_RC + HALO]
        for k in taps:
            a8 = offset_of_tap(k) - b
            acc = acc + w_ref[k:k + 1, pl.ds(lane0, CONV_LC)] * sh[a8:a8 + CONV_RC]
    return acc


def _conv_fwd(z, wdw, b_dw, ln_g, ln_b, wall, S, tm):
    T = z.shape[0]
    nt = S // tm
    hb = tm // HALO

    def body(cv_ref, cg_ref, cgate_ref, hcv_ref, hcg_ref, wdw_ref, bdw_ref, lng_ref, lnb_ref, wpw_ref,
             wbrc_ref, ya_ref, y_ref, rstd_ref, pw_ref, ubuf, cbuf):
        t = pl.program_id(1)
        ubuf[HALO:HALO + tm, :] = cv_ref[...].astype(F32) * _sig(cg_ref[...].astype(F32))
        hu = hcv_ref[...].astype(F32) * _sig(hcg_ref[...].astype(F32))
        ubuf[0:HALO, :] = jnp.where(t > 0, hu, 0.0)
        ubuf[HALO + tm:HALO + tm + 8, :] = jnp.zeros((8, D), F32)

        def chunk(ci, carry):
            r0 = pl.multiple_of(ci * CONV_RC, CONV_RC)
            for lg in range(D // CONV_LC):
                acc = _conv_taps(wdw_ref, ubuf, r0, lg * CONV_LC, lambda k: HALO - (CONV_K - 1) + k)
                cbuf[pl.ds(r0, CONV_RC), pl.ds(lg * CONV_LC, CONV_LC)] = acc
            return carry

        lax.fori_loop(0, tm // CONV_RC, chunk, 0)
        cc = cbuf[...] + bdw_ref[...]
        mu = jnp.mean(cc, axis=-1, keepdims=True)
        dd = cc - mu
        rstd = lax.rsqrt(jnp.mean(dd * dd, axis=-1, keepdims=True) + EPS)
        yn = dd * rstd
        y_ref[...] = yn.astype(BF16)
        rstd_ref[...] = rstd
        n = yn * lng_ref[...] + lnb_ref[...]
        s = n * _sig(n)
        pw = _mm(s.astype(BF16), wpw_ref[...])
        pw_ref[...] = pw.astype(BF16)
        gt = cgate_ref[...].astype(F32)
        ya_in = pw * (gt * _sig(gt))
        ya_ref[...] = _mm(ya_in.astype(BF16), wbrc_ref[...]).astype(BF16)

    def row(b, t):
        return b * nt + t

    def halo(b, t):
        return jnp.maximum(row(b, t) * hb - 1, 0)

    vec = pl.BlockSpec((1, D), lambda b, t: (0, 0))
    tile = lambda j: pl.BlockSpec((tm, D), lambda b, t: (row(b, t), j))
    out_tile = pl.BlockSpec((tm, D), lambda b, t: (row(b, t), 0))
    return pl.pallas_call(
        body, name="conv_fwd", grid=(T // S, nt),
        in_specs=[tile(0), tile(1), tile(2),
                  pl.BlockSpec((HALO, D), lambda b, t: (halo(b, t), 0)),
                  pl.BlockSpec((HALO, D), lambda b, t: (halo(b, t), 1)),
                  pl.BlockSpec((32, D), lambda b, t: (0, 0)), vec, vec, vec,
                  pl.BlockSpec((D, D), lambda b, t: (0, 0)),
                  pl.BlockSpec((D, D), lambda b, t: (1, 0))],
        out_specs=(out_tile, out_tile, pl.BlockSpec((tm, 1), lambda b, t: (row(b, t), 0)), out_tile),
        out_shape=(jax.ShapeDtypeStruct((T, D), BF16), jax.ShapeDtypeStruct((T, D), BF16),
                   jax.ShapeDtypeStruct((T, 1), F32), jax.ShapeDtypeStruct((T, D), BF16)),
        scratch_shapes=[pltpu.VMEM((tm + HALO + 8, D), F32), pltpu.VMEM((tm, D), F32)],
        compiler_params=_params(("arbitrary", "arbitrary")),
    )(z, z, z, z, z, wdw, b_dw, ln_g, ln_b, wall, wall)


def _rope(tv, cos, sin):
    lane = lax.broadcasted_iota(jnp.int32, tv.shape, 1) & (HEAD_DIM - 1)
    swapped = jnp.where(lane < ROPE_DIM // 2, pltpu.roll(tv, 2 * HEAD_DIM - ROPE_DIM // 2, 1),
                        pltpu.roll(tv, ROPE_DIM // 2, 1))
    return tv * cos + swapped * sin


def _kv_variants(kv):
    lane = lax.broadcasted_iota(jnp.int32, kv.shape, 1)
    lo = lane < HEAD_DIM
    sw = pltpu.roll(kv, HEAD_DIM, 1)
    z = jnp.zeros_like(kv)
    g0 = (jnp.where(lo, kv, z).astype(BF16), jnp.where(lo, z, sw).astype(BF16))
    g1 = (jnp.where(lo, sw, z).astype(BF16), jnp.where(lo, z, kv).astype(BF16))
    return (g0, g1)


def _band_mask(nq):
    qi = lax.broadcasted_iota(jnp.int32, (nq * BLOCK, 2 * BLOCK), 0) & (BLOCK - 1)
    sj = lax.broadcasted_iota(jnp.int32, (nq * BLOCK, 2 * BLOCK), 1)
    return (sj <= qi + BLOCK) & (sj > qi), sj


def _sink_col(sink_ref, g, e):
    return jnp.concatenate(
        [jnp.full((BLOCK, 1), sink_ref[8 * g + 2 * j + e], F32) for j in range(4)], axis=0)


def _softmax_sink(s, valid, sk):
    s = jnp.where(valid, s, -1e30)
    m = jnp.maximum(jnp.max(s, axis=-1, keepdims=True), sk)
    p = jnp.exp(s - m)
    ps = jnp.exp(sk - m)
    inv = 1.0 / (jnp.sum(p, axis=-1, keepdims=True) + ps)
    return p * inv, ps * inv


def _attn_fwd(z, zkv, cos_t, sin_t, sinks, S, tq):
    T = z.shape[0]
    nt = S // tq
    nq = tq // BLOCK

    def body(sink_ref, q_ref, kv_ref, hkv_ref, cos_ref, sin_ref, hcos_ref, hsin_ref, o_ref):
        t = pl.program_id(1)
        cos = cos_ref[...]
        sin = sin_ref[...]
        kv = jnp.concatenate([hkv_ref[...], kv_ref[...]], axis=0).astype(F32)
        cos_k = jnp.concatenate([hcos_ref[...], cos], axis=0)
        sin_k = jnp.concatenate([hsin_ref[...], sin], axis=0)
        kx = _kv_variants(_rope(kv[:, :BLOCK], cos_k, sin_k))
        vx = _kv_variants(kv[:, BLOCK:])
        band, sj = _band_mask(4)
        qs = [(_rope(q_ref[:, 128 * hp:128 * hp + 128].astype(F32), cos, sin) * 0.125).astype(BF16)
              for hp in range(8)]
        for n in range(nq):
            first = (t == 0) & (n == 0)
            valid = band & (jnp.logical_not(first) | (sj >= BLOCK))
            r0 = n * BLOCK
            for g in range(2):
                lhs = jnp.concatenate([qs[4 * g + j][r0:r0 + BLOCK] for j in range(4)], axis=0)
                acc = jnp.zeros((4 * BLOCK, BLOCK), F32)
                for e in range(2):
                    s = _mm_nt(lhs, kx[g][e][r0:r0 + 2 * BLOCK])
                    p, _ = _softmax_sink(s, valid, _sink_col(sink_ref, g, e))
                    acc = acc + _mm(p.astype(BF16), vx[g][e][r0:r0 + 2 * BLOCK])
                for j in range(4):
                    o_ref[r0:r0 + BLOCK, 128 * (4 * g + j):128 * (4 * g + j) + 128] = (
                        acc[j * BLOCK:(j + 1) * BLOCK].astype(BF16))

    def row(b, t):
        return b * nt + t

    def halo(b, t):
        return jnp.maximum(row(b, t) * nq - 1, 0)

    return pl.pallas_call(
        body, name="attn_fwd", grid=(T // S, nt),
        in_specs=[pl.BlockSpec(memory_space=pltpu.SMEM),
                  pl.BlockSpec((tq, D), lambda b, t: (row(b, t), 3)),
                  pl.BlockSpec((tq, 2 * BLOCK), lambda b, t: (row(b, t), 0)),
                  pl.BlockSpec((BLOCK, 2 * BLOCK), lambda b, t: (halo(b, t), 0)),
                  pl.BlockSpec((tq, BLOCK), lambda b, t: (row(b, t), 0)),
                  pl.BlockSpec((tq, BLOCK), lambda b, t: (row(b, t), 0)),
                  pl.BlockSpec((BLOCK, BLOCK), lambda b, t: (halo(b, t), 0)),
                  pl.BlockSpec((BLOCK, BLOCK), lambda b, t: (halo(b, t), 0))],
        out_specs=pl.BlockSpec((tq, D), lambda b, t: (row(b, t), 0)),
        out_shape=jax.ShapeDtypeStruct((T, D), BF16),
        compiler_params=_params(("arbitrary", "arbitrary")),
    )(sinks, z, zkv, zkv, cos_t, sin_t, cos_t, sin_t)


def _tail_a(x, tgt, p, o, ya, z, ln_post, wall, wppt, tm):
    T = x.shape[0]
    last = T // tm - 1

    def body(x_ref, tgt_ref, p_ref, o_ref, ya_ref, ag_ref, gc_ref, ga_ref, lnp_ref, wbra_ref, wout_ref,
             wpg_ref, wppt_ref, loss_ref, dx1_ref, dm_ref, yb_ref, glnp_ref, gwout_ref, gwpg_ref, gwpp_ref,
             acc_out, acc_pg, sem):
        i = pl.program_id(0)

        @pl.when(i == 0)
        def _():
            acc_out[...] = jnp.zeros_like(acc_out)
            acc_pg[...] = jnp.zeros_like(acc_pg)
            gwpp_ref[...] = jnp.zeros_like(gwpp_ref)
            glnp_ref[...] = jnp.zeros_like(glnp_ref)
            loss_ref[...] = jnp.zeros_like(loss_ref)

        ag = ag_ref[...].astype(F32)
        yb_in = (o_ref[...].astype(F32) * (ag * _sig(ag))).astype(BF16)
        yb = _mm(yb_in, wbra_ref[...])
        yb_ref[...] = yb.astype(BF16)
        m = (_sig(gc_ref[...].astype(F32)) * ya_ref[...].astype(F32)
             + _sig(ga_ref[...].astype(F32)) * yb).astype(BF16)
        mo = _mm(m, wout_ref[...])
        r2 = lax.rsqrt(jnp.mean(mo * mo, axis=-1, keepdims=True) + EPS)
        nrm = mo * r2
        g_post = lnp_ref[...]
        x1 = x_ref[...] + nrm * g_post
        x1b = x1.astype(BF16)
        gate = _sig(_mm(x1b, wpg_ref[...]))
        pb = p_ref[...].astype(BF16)
        pp = _mm_nt(pb, wppt_ref[...])
        err = x1 + gate * pp - tgt_ref[...]
        loss_ref[...] += 0.5 * jnp.sum(jnp.sum(err * err, axis=-1, keepdims=True) * (1.0 / D),
                                       axis=0, keepdims=True)
        dx2 = err * (1.0 / D)
        dgp = (dx2 * pp * gate * (1.0 - gate)).astype(BF16)
        dpp = (dx2 * gate).astype(BF16)
        dx1 = dx2 + _mm_nt(dgp, wpg_ref[...])
        dx1_ref[...] = dx1
        acc_pg[...] += _mm_tn(x1b, dgp)
        gwpp_ref[...] += _mm_tn(dpp, pb)
        glnp_ref[...] += jnp.sum(dx1 * nrm, axis=0, keepdims=True)
        a = dx1 * g_post
        dmo = (r2 * (a - nrm * jnp.mean(a * nrm, axis=-1, keepdims=True))).astype(BF16)
        dm_ref[...] = _mm_nt(dmo, wout_ref[...]).astype(BF16)
        acc_out[...] += _mm_tn(m, dmo)

        @pl.when(i == last)
        def _():
            _flush(acc_out, gwout_ref, sem.at[0])
            _flush(acc_pg, gwpg_ref, sem.at[1])

    tile = pl.BlockSpec((tm, D), lambda i: (i, 0))
    ztile = lambda j: pl.BlockSpec((tm, D), lambda i: (i, j))
    wsq = lambda k: pl.BlockSpec((D, D), lambda i: (k, 0))
    const = lambda shp: pl.BlockSpec(shp, lambda i: (0, 0))
    any_spec = pl.BlockSpec(memory_space=pl.ANY)
    return pl.pallas_call(
        body, name="tail_a", grid=(T // tm,),
        in_specs=[tile, tile, pl.BlockSpec((tm, PLE), lambda i: (i, 0)), tile, tile, ztile(4), ztile(5),
                  ztile(6), const((1, D)), wsq(2), wsq(3), wsq(4), const((D, PLE))],
        out_specs=(const((1, 1)), tile, tile, tile, const((1, D)), any_spec, any_spec, const((D, PLE))),
        out_shape=(jax.ShapeDtypeStruct((1, 1), F32), jax.ShapeDtypeStruct((T, D), F32),
                   jax.ShapeDtypeStruct((T, D), BF16), jax.ShapeDtypeStruct((T, D), BF16),
                   jax.ShapeDtypeStruct((1, D), F32), jax.ShapeDtypeStruct((D, D), F32),
                   jax.ShapeDtypeStruct((D, D), F32), jax.ShapeDtypeStruct((D, PLE), F32)),
        scratch_shapes=[pltpu.VMEM((D, D), F32), pltpu.VMEM((D, D), F32), pltpu.SemaphoreType.DMA((2,))],
        compiler_params=_params(("arbitrary",)),
    )(x, tgt, p, o, ya, z, z, z, ln_post, wall, wall, wall, wppt)


def _dsilu(v, sg):
    return sg * (1.0 + v * (1.0 - sg))


def _tail_b(dm, ya, yb, o, z, pw, y, rstd, ln_g, ln_b, wall, tm):
    T = dm.shape[0]
    last = T // tm - 1

    def body(dm_ref, ya_ref, yb_ref, o_ref, ag_ref, gc_ref, ga_ref, cgate_ref, pw_ref, y_ref, rstd_ref,
             lng_ref, lnb_ref, wpw_ref, wbrc_ref, wbra_ref, dg_ref, do_ref, dcg_ref, dc_ref, gvec_ref,
             gbra_ref, gbrc_ref, gpw_ref, acc_bra, acc_brc, acc_pw, sem):
        i = pl.program_id(0)

        @pl.when(i == 0)
        def _():
            acc_bra[...] = jnp.zeros_like(acc_bra)
            acc_brc[...] = jnp.zeros_like(acc_brc)
            acc_pw[...] = jnp.zeros_like(acc_pw)
            gvec_ref[...] = jnp.zeros_like(gvec_ref)

        dm_v = dm_ref[...].astype(F32)
        sgc = _sig(gc_ref[...].astype(F32))
        sga = _sig(ga_ref[...].astype(F32))
        dya = (dm_v * sgc).astype(BF16)
        dyb = (dm_v * sga).astype(BF16)
        dg_ref[:, D:2 * D] = (dm_v * ya_ref[...].astype(F32) * sgc * (1.0 - sgc)).astype(BF16)
        dg_ref[:, 2 * D:3 * D] = (dm_v * yb_ref[...].astype(F32) * sga * (1.0 - sga)).astype(BF16)
        ag = ag_ref[...].astype(F32)
        sag = _sig(ag)
        sa = ag * sag
        ov = o_ref[...].astype(F32)
        dyb_in = _mm_nt(dyb, wbra_ref[...])
        acc_bra[...] += _mm_tn((ov * sa).astype(BF16), dyb)
        do_ref[...] = (dyb_in * sa).astype(BF16)
        dg_ref[:, 0:D] = (dyb_in * ov * _dsilu(ag, sag)).astype(BF16)
        gt = cgate_ref[...].astype(F32)
        sgt = _sig(gt)
        sgate = gt * sgt
        pw = pw_ref[...].astype(F32)
        dya_in = _mm_nt(dya, wbrc_ref[...])
        acc_brc[...] += _mm_tn((pw * sgate).astype(BF16), dya)
        dpw = (dya_in * sgate).astype(BF16)
        dcg_ref[...] = (dya_in * pw * _dsilu(gt, sgt)).astype(BF16)
        yn = y_ref[...].astype(F32)
        g = lng_ref[...]
        n = yn * g + lnb_ref[...]
        sn = _sig(n)
        acc_pw[...] += _mm_tn((n * sn).astype(BF16), dpw)
        dn = _mm_nt(dpw, wpw_ref[...]) * _dsilu(n, sn)
        gvec_ref[0:1, :] += jnp.sum(dn * yn, axis=0, keepdims=True)
        gvec_ref[1:2, :] += jnp.sum(dn, axis=0, keepdims=True)
        dy = dn * g
        dc = rstd_ref[...] * (dy - jnp.mean(dy, axis=-1, keepdims=True)
                              - yn * jnp.mean(dy * yn, axis=-1, keepdims=True))
        gvec_ref[2:3, :] += jnp.sum(dc, axis=0, keepdims=True)
        dc_ref[...] = dc.astype(BF16)

        @pl.when(i == last)
        def _():
            _flush(acc_bra, gbra_ref, sem.at[0])
            _flush(acc_brc, gbrc_ref, sem.at[1])
            _flush(acc_pw, gpw_ref, sem.at[2])

    tile = pl.BlockSpec((tm, D), lambda i: (i, 0))
    ztile = lambda j: pl.BlockSpec((tm, D), lambda i: (i, j))
    wsq = lambda k: pl.BlockSpec((D, D), lambda i: (k, 0))
    const = lambda shp: pl.BlockSpec(shp, lambda i: (0, 0))
    any_spec = pl.BlockSpec(memory_space=pl.ANY)
    sq = jax.ShapeDtypeStruct((D, D), F32)
    return pl.pallas_call(
        body, name="tail_b", grid=(T // tm,),
        in_specs=[tile, tile, tile, tile, ztile(4), ztile(5), ztile(6), ztile(2), tile, tile,
                  pl.BlockSpec((tm, 1), lambda i: (i, 0)), const((1, D)), const((1, D)), wsq(0), wsq(1),
                  wsq(2)],
        out_specs=(pl.BlockSpec((tm, 3 * D), lambda i: (i, 0)), tile, tile, tile, const((8, D)),
                   any_spec, any_spec, any_spec),
        out_shape=(jax.ShapeDtypeStruct((T, 3 * D), BF16), jax.ShapeDtypeStruct((T, D), BF16),
                   jax.ShapeDtypeStruct((T, D), BF16), jax.ShapeDtypeStruct((T, D), BF16),
                   jax.ShapeDtypeStruct((8, D), F32), sq, sq, sq),
        scratch_shapes=[pltpu.VMEM((D, D), F32), pltpu.VMEM((D, D), F32), pltpu.VMEM((D, D), F32),
                        pltpu.SemaphoreType.DMA((3,))],
        compiler_params=_params(("arbitrary",)),
    )(dm, ya, yb, o, z, z, z, z, pw, y, rstd, ln_g, ln_b, wall, wall, wall)


def _conv_bwd(dc, z, wdw, S, tm):
    T = dc.shape[0]
    nt = S // tm
    hb = tm // HALO
    nrows = T // HALO

    def body(dc_ref, hdc_ref, cv_ref, cg_ref, hcv_ref, hcg_ref, wdw_ref, dz_ref, gw_ref, ubuf, dcbuf, dubuf,
             dwacc):
        b = pl.program_id(0)
        t = pl.program_id(1)

        @pl.when((b == 0) & (t == 0))
        def _():
            dwacc[...] = jnp.zeros_like(dwacc)

        cv = cv_ref[...].astype(F32)
        sg = _sig(cg_ref[...].astype(F32))
        ubuf[HALO:HALO + tm, :] = cv * sg
        hu = hcv_ref[...].astype(F32) * _sig(hcg_ref[...].astype(F32))
        ubuf[0:HALO, :] = jnp.where(t > 0, hu, 0.0)
        ubuf[HALO + tm:HALO + tm + 8, :] = jnp.zeros((8, D), F32)
        dcbuf[0:tm, :] = dc_ref[...].astype(F32)
        dcbuf[tm:tm + HALO, :] = jnp.where(t < nt - 1, hdc_ref[...].astype(F32), 0.0)
        dcbuf[tm + HALO:tm + HALO + 8, :] = jnp.zeros((8, D), F32)

        def chunk(ci, carry):
            r0 = pl.multiple_of(ci * CONV_RC, CONV_RC)
            for lg in range(D // CONV_LC):
                l0 = lg * CONV_LC
                dubuf[pl.ds(r0, CONV_RC), pl.ds(l0, CONV_LC)] = _conv_taps(
                    wdw_ref, dcbuf, r0, l0, lambda k: CONV_K - 1 - k)
                dcc = dcbuf[pl.ds(r0, CONV_RC), pl.ds(l0, CONV_LC)]
                slab = ubuf[pl.ds(r0, CONV_RC + HALO + 8), pl.ds(l0, CONV_LC)]
                for bb in range(8):
                    taps = [k for k in range(CONV_K) if (HALO - (CONV_K - 1) + k) % 8 == bb]
                    if not taps:
                        continue
                    sh = slab[bb:bb + CONV_RC + HALO]
                    for k in taps:
                        a8 = HALO - (CONV_K - 1) + k - bb
                        prod = dcc * sh[a8:a8 + CONV_RC]
                        part = prod[0:8]
                        for q in range(1, CONV_RC // 8):
                            part = part + prod[8 * q:8 * q + 8]
                        dwacc[8 * k:8 * k + 8, pl.ds(l0, CONV_LC)] += part
            return carry

        lax.fori_loop(0, tm // CONV_RC, chunk, 0)
        du = dubuf[...]
        dz_ref[:, 0:D] = (du * sg).astype(BF16)
        dz_ref[:, D:2 * D] = (du * cv * sg * (1.0 - sg)).astype(BF16)

        @pl.when((b == pl.num_programs(0) - 1) & (t == nt - 1))
        def _():
            for k in range(32):
                gw_ref[k:k + 1, :] = jnp.sum(dwacc[8 * k:8 * k + 8, :], axis=0, keepdims=True)

    def row(b, t):
        return b * nt + t

    def prev_halo(b, t):
        return jnp.maximum(row(b, t) * hb - 1, 0)

    def next_halo(b, t):
        return jnp.minimum((row(b, t) + 1) * hb, nrows - 1)

    return pl.pallas_call(
        body, name="conv_bwd", grid=(T // S, nt),
        in_specs=[pl.BlockSpec((tm, D), lambda b, t: (row(b, t), 0)),
                  pl.BlockSpec((HALO, D), lambda b, t: (next_halo(b, t), 0)),
                  pl.BlockSpec((tm, D), lambda b, t: (row(b, t), 0)),
                  pl.BlockSpec((tm, D), lambda b, t: (row(b, t), 1)),
                  pl.BlockSpec((HALO, D), lambda b, t: (prev_halo(b, t), 0)),
                  pl.BlockSpec((HALO, D), lambda b, t: (prev_halo(b, t), 1)),
                  pl.BlockSpec((32, D), lambda b, t: (0, 0))],
        out_specs=(pl.BlockSpec((tm, 2 * D), lambda b, t: (row(b, t), 0)),
                   pl.BlockSpec((32, D), lambda b, t: (0, 0))),
        out_shape=(jax.ShapeDtypeStruct((T, 2 * D), BF16), jax.ShapeDtypeStruct((32, D), F32)),
        scratch_shapes=[pltpu.VMEM((tm + HALO + 8, D), F32), pltpu.VMEM((tm + HALO + 8, D), F32),
                        pltpu.VMEM((tm, D), F32), pltpu.VMEM((8 * 32, D), F32)],
        compiler_params=_params(("arbitrary", "arbitrary")),
    )(dc, dc, z, z, z, z, wdw)


def _attn_bwd(z, zkv, o, do, cos_t, sin_t, sinks, S, tq):
    T = z.shape[0]
    nt = S // tq
    nq = tq // BLOCK

    def body(sink_ref, q_ref, kv_ref, hkv_ref, o_ref, do_ref, cos_ref, sin_ref, hcos_ref, hsin_ref,
             dq_ref, dkv_ref, gs_ref, carry, dkacc, dvacc):
        b = pl.program_id(0)
        tt = pl.program_id(1)
        t = nt - 1 - tt

        @pl.when((b == 0) & (tt == 0))
        def _():
            gs_ref[...] = jnp.zeros_like(gs_ref)

        @pl.when(tt == 0)
        def _():
            carry[...] = jnp.zeros_like(carry)

        cos = cos_ref[...]
        sin = sin_ref[...]
        kv = jnp.concatenate([hkv_ref[...], kv_ref[...]], axis=0).astype(F32)
        cos_k = jnp.concatenate([hcos_ref[...], cos], axis=0)
        sin_k = jnp.concatenate([hsin_ref[...], sin], axis=0)
        kx = _kv_variants(_rope(kv[:, :BLOCK], cos_k, sin_k))
        vx = _kv_variants(kv[:, BLOCK:])
        band, sj = _band_mask(4)
        lane = lax.broadcasted_iota(jnp.int32, (4 * BLOCK, BLOCK), 1)
        lo = lane < HEAD_DIM
        lo_k = lax.broadcasted_iota(jnp.int32, (2 * BLOCK, BLOCK), 1) < HEAD_DIM
        qs = [(_rope(q_ref[:, 128 * hp:128 * hp + 128].astype(F32), cos, sin) * 0.125).astype(BF16)
              for hp in range(8)]
        dkacc[...] = jnp.zeros_like(dkacc)
        dvacc[...] = jnp.zeros_like(dvacc)
        gsum = jnp.zeros((1, BLOCK), F32)
        hlane = lax.broadcasted_iota(jnp.int32, (1, BLOCK), 1)
        for n in range(nq):
            first = (t == 0) & (n == 0)
            valid = band & (jnp.logical_not(first) | (sj >= BLOCK))
            r0 = n * BLOCK
            for g in range(2):
                cols = [slice(128 * (4 * g + j), 128 * (4 * g + j) + 128) for j in range(4)]
                lhs = jnp.concatenate([qs[4 * g + j][r0:r0 + BLOCK] for j in range(4)], axis=0)
                dov = jnp.concatenate([do_ref[r0:r0 + BLOCK, cs] for cs in cols], axis=0)
                prod = dov.astype(F32) * jnp.concatenate(
                    [o_ref[r0:r0 + BLOCK, cs] for cs in cols], axis=0).astype(F32)
                dq = jnp.zeros((4 * BLOCK, BLOCK), F32)
                dk_e = []
                dv_e = []
                for e in range(2):
                    kw = kx[g][e][r0:r0 + 2 * BLOCK]
                    vw = vx[g][e][r0:r0 + 2 * BLOCK]
                    s = _mm_nt(lhs, kw)
                    p, psink = _softmax_sink(s, valid, _sink_col(sink_ref, g, e))
                    delta = jnp.sum(jnp.where(lo if e == 0 else jnp.logical_not(lo), prod, 0.0),
                                    axis=-1, keepdims=True)
                    ds = (p * (_mm_nt(dov, vw) - delta)).astype(BF16)
                    dq = dq + _mm(ds, kw)
                    dk_e.append(_mm_tn(ds, lhs))
                    dv_e.append(_mm_tn(p.astype(BF16), dov))
                    gs = -psink * delta
                    for j in range(4):
                        tot = jnp.sum(gs[j * BLOCK:(j + 1) * BLOCK], axis=0, keepdims=True)
                        gsum = gsum + jnp.where(hlane == 8 * g + 2 * j + e, tot, 0.0)
                for acc, parts in ((dkacc, dk_e), (dvacc, dv_e)):
                    if g == 0:
                        both = jnp.where(lo_k, parts[0] + pltpu.roll(parts[1], HEAD_DIM, 1), 0.0)
                    else:
                        both = jnp.where(lo_k, 0.0, parts[1] + pltpu.roll(parts[0], HEAD_DIM, 1))
                    acc[r0:r0 + 2 * BLOCK, :] += both
                for j in range(4):
                    dqj = _rope(dq[j * BLOCK:(j + 1) * BLOCK] * 0.125, cos[r0:r0 + BLOCK],
                                -sin[r0:r0 + BLOCK])
                    dq_ref[r0:r0 + BLOCK, cols[j]] = dqj.astype(BF16)
        gs_ref[0:1, :] += gsum
        dk_all = dkacc[...]
        dv_all = dvacc[...]
        last_rows = slice(tq, tq + BLOCK)
        dk_last = dk_all[last_rows] + carry[0:BLOCK, :]
        dv_last = dv_all[last_rows] + carry[BLOCK:2 * BLOCK, :]
        carry[0:BLOCK, :] = dk_all[0:BLOCK]
        carry[BLOCK:2 * BLOCK, :] = dv_all[0:BLOCK]
        if nq > 1:
            dk_t = jnp.concatenate([dk_all[BLOCK:tq], dk_last], axis=0)
            dv_t = jnp.concatenate([dv_all[BLOCK:tq], dv_last], axis=0)
        else:
            dk_t, dv_t = dk_last, dv_last
        dkv_ref[:, 0:BLOCK] = _rope(dk_t, cos, -sin).astype(BF16)
        dkv_ref[:, BLOCK:2 * BLOCK] = dv_t.astype(BF16)

    def row(b, tt):
        return b * nt + (nt - 1 - tt)

    def halo(b, tt):
        return jnp.maximum(row(b, tt) * nq - 1, 0)

    tile = pl.BlockSpec((tq, D), lambda b, tt: (row(b, tt), 0))
    return pl.pallas_call(
        body, name="attn_bwd", grid=(T // S, nt),
        in_specs=[pl.BlockSpec(memory_space=pltpu.SMEM),
                  pl.BlockSpec((tq, D), lambda b, tt: (row(b, tt), 3)),
                  pl.BlockSpec((tq, 2 * BLOCK), lambda b, tt: (row(b, tt), 0)),
                  pl.BlockSpec((BLOCK, 2 * BLOCK), lambda b, tt: (halo(b, tt), 0)),
                  tile, tile,
                  pl.BlockSpec((tq, BLOCK), lambda b, tt: (row(b, tt), 0)),
                  pl.BlockSpec((tq, BLOCK), lambda b, tt: (row(b, tt), 0)),
                  pl.BlockSpec((BLOCK, BLOCK), lambda b, tt: (halo(b, tt), 0)),
                  pl.BlockSpec((BLOCK, BLOCK), lambda b, tt: (halo(b, tt), 0))],
        out_specs=(tile, pl.BlockSpec((tq, 2 * BLOCK), lambda b, tt: (row(b, tt), 0)),
                   pl.BlockSpec((8, BLOCK), lambda b, tt: (0, 0))),
        out_shape=(jax.ShapeDtypeStruct((T, D), BF16), jax.ShapeDtypeStruct((T, 2 * BLOCK), BF16),
                   jax.ShapeDtypeStruct((8, BLOCK), F32)),
        scratch_shapes=[pltpu.VMEM((2 * BLOCK, BLOCK), F32), pltpu.VMEM((tq + BLOCK, BLOCK), F32),
                        pltpu.VMEM((tq + BLOCK, BLOCK), F32)],
        compiler_params=_params(("arbitrary", "arbitrary")),
    )(sinks, z, zkv, zkv, o, do, cos_t, sin_t, cos_t, sin_t)


def _dh(dz_cvg, dz_cgate, dz_q, dz_gates, dz_kv, wall, x, dx1, ln_pre, tm):
    T = x.shape[0]

    def body(cvg_ref, cgate_ref, q_ref, gates_ref, kv_ref, w_ref, wkv_ref, x_ref, dx1_ref, g_ref, gx_ref,
             glp_ref, acc):
        i = pl.program_id(0)
        k = pl.program_id(1)

        @pl.when((i == 0) & (k == 0))
        def _():
            glp_ref[...] = jnp.zeros_like(glp_ref)

        @pl.when(k == 0)
        def _():
            acc[...] = jnp.zeros_like(acc)

        for lo, hi, ref in ((0, 2, cvg_ref), (2, 3, cgate_ref), (3, 4, q_ref), (4, 7, gates_ref)):
            @pl.when((k >= lo) & (k < hi))
            def _(ref=ref):
                acc[...] += _mm(ref[...], w_ref[...])

        @pl.when(k == 7)
        def _():
            dh = acc[...] + _mm(kv_ref[...], wkv_ref[...])
            xv = x_ref[...]
            r = lax.rsqrt(jnp.mean(xv * xv, axis=-1, keepdims=True) + EPS)
            xr = xv * r
            glp_ref[...] += jnp.sum(dh * xr, axis=0, keepdims=True)
            a = dh * g_ref[...]
            gx_ref[...] = dx1_ref[...] + r * (a - xr * jnp.mean(a * xr, axis=-1, keepdims=True))

    def grp(lo, n):
        return pl.BlockSpec((tm, D), lambda i, k: (i, jnp.clip(k - lo, 0, n - 1)))

    tile = pl.BlockSpec((tm, D), lambda i, k: (i, 0))
    return pl.pallas_call(
        body, name="dh", grid=(T // tm, 8),
        in_specs=[grp(0, 2), grp(2, 1), grp(3, 1), grp(4, 3),
                  pl.BlockSpec((tm, 2 * BLOCK), lambda i, k: (i, 0)),
                  pl.BlockSpec((D, D), lambda i, k: (5 + jnp.minimum(k, 6), 0)),
                  pl.BlockSpec((2 * BLOCK, D), lambda i, k: ((WT0 + 7 * D) // (2 * BLOCK), 0)),
                  tile, tile, pl.BlockSpec((1, D), lambda i, k: (0, 0))],
        out_specs=(tile, pl.BlockSpec((1, D), lambda i, k: (0, 0))),
        out_shape=(jax.ShapeDtypeStruct((T, D), F32), jax.ShapeDtypeStruct((1, D), F32)),
        scratch_shapes=[pltpu.VMEM((tm, D), F32)],
        compiler_params=_params(("arbitrary", "arbitrary")),
    )(dz_cvg, dz_cgate, dz_q, dz_gates, dz_kv, wall, wall, x, dx1, ln_pre)


def _gwt(dz, h, tt, name):
    T, n = dz.shape
    bw = min(n, D)
    last = T // tt - 1

    def body(dz_ref, h_ref, o_ref, acc):
        t = pl.program_id(1)

        @pl.when(t == 0)
        def _():
            acc[...] = jnp.zeros_like(acc)

        acc[...] += _mm_tn(dz_ref[...], h_ref[...])

        @pl.when(t == last)
        def _():
            o_ref[...] = acc[...]

    return pl.pallas_call(
        body, name=name, grid=(n // bw, T // tt),
        in_specs=[pl.BlockSpec((tt, bw), lambda j, t: (t, j)), pl.BlockSpec((tt, D), lambda j, t: (t, 0))],
        out_specs=pl.BlockSpec((bw, D), lambda j, t: (j, 0)),
        out_shape=jax.ShapeDtypeStruct((n, D), F32),
        scratch_shapes=[pltpu.VMEM((bw, D), F32)],
        compiler_params=_params(("arbitrary", "arbitrary")),
    )(dz, h)


_BC1 = 1.0 - ADAM_B1 ** ADAM_STEP
_BC2 = 1.0 - ADAM_B2 ** ADAM_STEP


def _adamw_math(w, g, m, v):
    m = ADAM_B1 * m + (1.0 - ADAM_B1) * g
    v = ADAM_B2 * v + (1.0 - ADAM_B2) * (g * g)
    delta = -ADAM_LR * ((m / _BC1) / (jnp.sqrt(v / _BC2) + ADAM_EPS) + ADAM_WD * w)
    return delta, m, v


def _adamw_rows(g, w, m, v, rows, name):
    R, C = w.shape

    def body(g_ref, w_ref, m_ref, v_ref, d_ref, nm_ref, nv_ref):
        d, nm, nv = _adamw_math(w_ref[...], g_ref[...], m_ref[...], v_ref[...])
        d_ref[...] = d
        nm_ref[...] = nm
        nv_ref[...] = nv

    spec = pl.BlockSpec((rows, C), lambda i: (i, 0))
    shp = jax.ShapeDtypeStruct((R, C), F32)
    return pl.pallas_call(
        body, name=name, grid=(R // rows,), in_specs=[spec] * 4, out_specs=(spec,) * 3,
        out_shape=(shp,) * 3, compiler_params=_params(("arbitrary",)),
    )(g, w, m, v)


def _adamw_square(gfin, ws, ms, vs):
    rb = 64
    nb = SQ_SHARD // rb

    def body(*refs):
        g_refs = refs[0:5]
        w_refs, m_refs, v_refs = refs[5:10], refs[10:15], refs[15:20]
        outs = refs[20:]
        for k in range(5):
            gk = g_refs[k][...]
            d, nm, nv = _adamw_math(w_refs[k][...], gk, m_refs[k][...], v_refs[k][...])
            outs[4 * k][...] = gk
            outs[4 * k + 1][...] = d
            outs[4 * k + 2][...] = nm
            outs[4 * k + 3][...] = nv

    spec = pl.BlockSpec((rb, D), lambda i: (i, 0))
    gspecs = [pl.BlockSpec((rb, D), lambda i, k=k: ((WIN_SHARD + SQ_SHARD * k) // rb + i, 0))
              for k in range(5)]
    shp = jax.ShapeDtypeStruct((SQ_SHARD, D), F32)
    res = pl.pallas_call(
        body, name="adamw_square", grid=(nb,), in_specs=gspecs + [spec] * 15, out_specs=(spec,) * 20,
        out_shape=(shp,) * 20, compiler_params=_params(("arbitrary",)),
    )(*([gfin] * 5), *ws, *ms, *vs)
    return [tuple(res[4 * k:4 * k + 4]) for k in range(5)]


def _adamw_small(gs, ws, ms, vs):
    n = len(gs)

    def body(*refs):
        outs = refs[4 * n:]
        for k in range(n):
            d, nm, nv = _adamw_math(refs[n + k][...], refs[k][...], refs[2 * n + k][...],
                                    refs[3 * n + k][...])
            outs[3 * k][...] = d
            outs[3 * k + 1][...] = nm
            outs[3 * k + 2][...] = nv

    vm = pl.BlockSpec(memory_space=pltpu.VMEM)
    shapes = []
    for w in ws:
        shapes += [jax.ShapeDtypeStruct(w.shape, F32)] * 3
    res = pl.pallas_call(
        body, name="adamw_small", in_specs=[vm] * (4 * n), out_specs=(vm,) * (3 * n),
        out_shape=tuple(shapes),
    )(*gs, *ws, *ms, *vs)
    return [tuple(res[3 * k:3 * k + 3]) for k in range(n)]


def _rope_tables(positions):
    inv = jnp.power(ROPE_THETA, -jnp.arange(0, ROPE_DIM, 2, dtype=F32) / ROPE_DIM)
    ang = positions.astype(F32).reshape(-1, 1) * inv
    c8, s8 = jnp.cos(ang), jnp.sin(ang)
    n = ang.shape[0]
    pad1 = jnp.ones((n, HEAD_DIM - ROPE_DIM), F32)
    pad0 = jnp.zeros((n, HEAD_DIM - ROPE_DIM), F32)
    cos_h = jnp.concatenate([c8, c8, pad1], axis=1)
    sin_h = jnp.concatenate([-s8, s8, pad0], axis=1)
    return jnp.concatenate([cos_h, cos_h], axis=1), jnp.concatenate([sin_h, sin_h], axis=1)


def _f32_to_bf16_rows(a):
    return lax.bitcast_convert_type(a, BF16).reshape(a.shape[0], 2 * a.shape[1])


def kernel(x, p, positions, w_in, ln_pre, ln_post, w_dw, b_dw, conv_ln_g, conv_ln_b, w_pw, sinks, w_br_conv, w_br_attn, w_out, w_ple_gate, w_ple_proj, loss_target, m_w_in, m_ln_pre, m_ln_post, m_w_dw, m_b_dw, m_conv_ln_g, m_conv_ln_b, m_w_pw, m_sinks, m_w_br_conv, m_w_br_attn, m_w_out, m_w_ple_gate, m_w_ple_proj, v_w_in, v_ln_pre, v_ln_post, v_w_dw, v_b_dw, v_conv_ln_g, v_conv_ln_b, v_w_pw, v_sinks, v_w_br_conv, v_w_br_attn, v_w_out, v_w_ple_gate, v_w_ple_proj):
    nb, S, _ = x.shape
    T = nb * S
    xc = lax.axis_index("x")
    yc = lax.axis_index("y")
    cc = lax.axis_index("c")
    shard = 2 * xc + yc

    sq_w = (w_pw, w_br_conv, w_br_attn, w_out, w_ple_gate)
    pack = jnp.concatenate(
        [w_in[0].T] + [w[0] for w in sq_w] + [w_ple_proj[0].T.reshape(WPP_SHARD, D)], axis=0).astype(BF16)
    wdw_shard = jnp.pad(w_dw[0], ((0, 1), (0, 0)))
    wall, wdw_all = _gather_weights(pack, wdw_shard)
    wdw = jnp.concatenate([wdw_all[s] for s in range(N_SHARDS)], axis=1)
    wppt = wall[WPP0:WALL_ROWS].reshape(D, PLE)

    x2 = x.reshape(T, D)
    tgt = loss_target.reshape(T, D)
    p2 = p.reshape(T, PLE)
    cos_t, sin_t = _rope_tables(positions)
    sinks1 = sinks.reshape(N_HEADS)

    tm_big = min(TILE_PROJ, T)
    tm = min(TILE_TOKEN, S)
    tq = min(TILE_ATTN, S)

    z, h = _inproj(x2, ln_pre, wall, tm_big)
    zkv = _kvproj(h, wall, tm_big)
    ya, y, rstd, pw = _conv_fwd(z, wdw, b_dw, conv_ln_g, conv_ln_b, wall, S, tm)
    o = _attn_fwd(z, zkv, cos_t, sin_t, sinks1, S, tq)
    loss_p, dx1, dm, yb, g_ln_post, gw_out, gw_pg, gw_ppt = _tail_a(
        x2, tgt, p2, o, ya, z, ln_post, wall, wppt, tm)

    dgates, do, dcg, dc, gvec, gw_bra, gw_brc, gw_pw = _tail_b(
        dm, ya, yb, o, z, pw, y, rstd, conv_ln_g, conv_ln_b, wall, tm)
    dz_cvg, g_wdw = _conv_bwd(dc, z, wdw, S, tm)
    dq, dkv, g_sinks = _attn_bwd(z, zkv, o, do, cos_t, sin_t, sinks1, S, tq)
    gx, g_ln_pre = _dh(dz_cvg, dcg, dq, dgates, dkv, wall, x2, dx1, ln_pre, min(512, T))
    gwt = jnp.concatenate(
        [_gwt(dz_cvg, h, tm_big, "gwt_cvg"), _gwt(dcg, h, tm_big, "gwt_cgate"), _gwt(dq, h, tm_big, "gwt_q"),
         _gwt(dgates, h, tm_big, "gwt_gates"), _gwt(dkv, h, tm_big, "gwt_kv")], axis=0)

    gwall = jnp.concatenate([gw_pw, gw_brc, gw_bra, gw_out, gw_pg, gwt, gw_ppt.reshape(PLE, D)], axis=0)
    gpack = jnp.stack([jnp.concatenate([gwall[r:r + n] for (_, n, r) in _pieces(s)], axis=0)
                       for s in range(N_SHARDS)])
    cidx = jnp.reshape(cc, (1,)).astype(jnp.int32)
    scidx = jnp.stack([shard, cc]).astype(jnp.int32)
    r1 = _exchange_halves(gpack)
    cs = _chip_sum(cidx, gpack, r1)
    r2 = _send_chip_sums(cs)
    fh = _final_half(scidx, gpack, r1, r2)
    gfin = _join_halves(fh)

    row37 = jnp.concatenate([g_sinks[0:1, 0:N_HEADS], loss_p, jnp.zeros((1, D - N_HEADS - 1), F32)], axis=1)
    vec = jnp.concatenate([g_wdw, g_ln_pre, g_ln_post, gvec[2:3], gvec[0:1], gvec[1:2], row37,
                           jnp.zeros((VEC_ROWS - 38, D), F32)], axis=0)
    tot = _all_reduce_small(vec)

    g_w_in = gfin[0:WIN_SHARD].T[None]
    d_w_in, nm_w_in, nv_w_in = _adamw_rows(g_w_in[0], w_in[0], m_w_in[0], v_w_in[0], 128, "adamw_w_in")
    sq_m = (m_w_pw, m_w_br_conv, m_w_br_attn, m_w_out, m_w_ple_gate)
    sq_v = (v_w_pw, v_w_br_conv, v_w_br_attn, v_w_out, v_w_ple_gate)
    sq_res = _adamw_square(gfin, [w[0] for w in sq_w], [m[0] for m in sq_m], [v[0] for v in sq_v])
    g_wpp = gfin[WIN_SHARD + 5 * SQ_SHARD:PACK_ROWS].reshape(PLE, PLE).T
    g_dw_all = tot[0:CONV_K]
    g_dw = lax.dynamic_slice_in_dim(g_dw_all, shard * PLE, PLE, axis=1)
    small_g = [g_wpp, g_dw, tot[32:33], tot[33:34], tot[34:35], tot[35:36], tot[36:37],
               tot[37:38, 0:N_HEADS]]
    small_w = [w_ple_proj[0], w_dw[0], ln_pre, ln_post, b_dw, conv_ln_g, conv_ln_b, sinks]
    small_m = [m_w_ple_proj[0], m_w_dw[0], m_ln_pre, m_ln_post, m_b_dw, m_conv_ln_g, m_conv_ln_b, m_sinks]
    small_v = [v_w_ple_proj[0], v_w_dw[0], v_ln_pre, v_ln_post, v_b_dw, v_conv_ln_g, v_conv_ln_b, v_sinks]
    small = _adamw_small(small_g, small_w, small_m, small_v)

    loss = tot[37, N_HEADS]
    grads = [g_w_in, small_g[2], small_g[3], g_dw[None], small_g[4], small_g[5], small_g[6],
             sq_res[0][0][None], small_g[7], sq_res[1][0][None], sq_res[2][0][None], sq_res[3][0][None],
             sq_res[4][0][None], g_wpp[None]]

    def triple(i):
        w_in_t = (d_w_in[None], nm_w_in[None], nv_w_in[None])
        sq = lambda k: tuple(a[None] for a in sq_res[k][1:4])
        sm = lambda k, lead: tuple(a[None] if lead else a for a in small[k])
        return [w_in_t[i], sm(2, False)[i], sm(3, False)[i], sm(1, True)[i], sm(4, False)[i],
                sm(5, False)[i], sm(6, False)[i], sq(0)[i], sm(7, False)[i], sq(1)[i], sq(2)[i], sq(3)[i],
                sq(4)[i], sm(0, True)[i]]

    return (loss, gx.reshape(nb, S, D), *grads, *triple(0), *triple(1), *triple(2))
```

```python
import functools

import jax
import jax.numpy as jnp
import numpy as np
from jax import lax
from jax.experimental import pallas as pl
from jax.experimental.pallas import tpu as pltpu

F32 = jnp.float32
BF16 = jnp.bfloat16

D = 1024
PLE = 256
N_HEADS = 16
HEAD_DIM = 64
BLOCK = 128
CONV_K = 31
ROPE_DIM = 16
ROPE_THETA = 500000.0
EPS = 1e-6
IN_WIDTH = 7424
N_SHARDS = 4

ADAM_LR = 0.001
ADAM_B1 = 0.9
ADAM_B2 = 0.999
ADAM_EPS = 1e-08
ADAM_WD = 0.01
ADAM_STEP = 10

SQ_NAMES = ("w_pw", "w_br_conv", "w_br_attn", "w_out", "w_ple_gate")
WT0 = 5 * D
WPP0 = WT0 + IN_WIDTH
WALL_ROWS = WPP0 + PLE
WIN_SHARD = IN_WIDTH // N_SHARDS
SQ_SHARD = D // N_SHARDS
WPP_SHARD = PLE * PLE // D
PACK_ROWS = WIN_SHARD + 5 * SQ_SHARD + WPP_SHARD
HALF_ROWS = PACK_ROWS // 2
VMEM_LIMIT = 56 * 1024 * 1024
MESH = pl.DeviceIdType.MESH
TILE_PROJ = 1024
TILE_TOKEN = 256
TILE_ATTN = 512


def _zp_row(o):
    if o < 4096:
        return o
    if o < 4352:
        return o + 3072
    return o - 256


def _pieces(s):
    out = []
    for a, n in ((0, 384), (384, 256), (640, 1216)):
        out.append((a, n, WT0 + _zp_row(WIN_SHARD * s + a)))
    for k in range(5):
        out.append((WIN_SHARD + SQ_SHARD * k, SQ_SHARD, D * k + SQ_SHARD * s))
    out.append((WIN_SHARD + 5 * SQ_SHARD, WPP_SHARD, WPP0 + WPP_SHARD * s))
    return out


N_PIECES = len(_pieces(0))


def _sel(s, vals):
    r = jnp.int32(vals[0])
    for i in range(1, len(vals)):
        r = jnp.where(s == i, jnp.int32(vals[i]), r)
    return r


def _sig(x):
    return 1.0 / (1.0 + jnp.exp(-x))


def _mm(a, b):
    return lax.dot_general(a, b, (((1,), (0,)), ((), ())), preferred_element_type=F32)


def _mm_nt(a, b):
    return lax.dot_general(a, b, (((1,), (1,)), ((), ())), preferred_element_type=F32)


def _mm_tn(a, b):
    return lax.dot_general(a, b, (((0,), (0,)), ((), ())), preferred_element_type=F32)


def _params(sem=None):
    return pltpu.CompilerParams(dimension_semantics=sem, vmem_limit_bytes=VMEM_LIMIT)


def _flush(acc_ref, out_ref, sem):
    cp = pltpu.make_async_copy(acc_ref, out_ref, sem)
    cp.start()
    cp.wait()


def _coords():
    return lax.axis_index("x"), lax.axis_index("y"), lax.axis_index("c")


def _chip_peers(x, y):
    return [(1 - x, y), (x, 1 - y), (1 - x, 1 - y)]


def _gather_weights(pack, wdw_shard):
    tables = [[_pieces(s)[p][2] for s in range(N_SHARDS)] for p in range(N_PIECES)]
    src_rows = [(_pieces(0)[p][0], _pieces(0)[p][1]) for p in range(N_PIECES)]

    def body(pack_ref, wdw_ref, wall_ref, wdwall_ref, stage, send_sems, recv_sems, loc_sems):
        x, y, c = _coords()
        s_me = 2 * x + y
        peers = _chip_peers(x, y)

        def dst_row(p, s):
            return pl.multiple_of(_sel(s, tables[p]), 32)

        def rcopy(src, dst, k, dev):
            return pltpu.make_async_remote_copy(
                src_ref=src, dst_ref=dst, send_sem=send_sems.at[k], recv_sem=recv_sems.at[k],
                device_id=dev, device_id_type=MESH)

        def half_bytes(k):
            rows = wall_ref.at[pl.ds(0, HALF_ROWS)]
            return rcopy(rows, rows, k, (x, y, c))

        own_wdw = pltpu.make_async_copy(wdw_ref, wdwall_ref.at[s_me], loc_sems.at[1])
        own_wdw.start()
        wdw_sends = []
        for k, (px, py) in enumerate(peers):
            for p in range(N_PIECES):
                a, n = src_rows[p]
                h = n // 2
                rcopy(pack_ref.at[pl.ds(pl.multiple_of(a + c * h, 32), h)],
                      wall_ref.at[pl.ds(pl.multiple_of(dst_row(p, s_me) + c * h, 32), h)], k,
                      (px, py, c)).start()
            cp = rcopy(wdw_ref, wdwall_ref.at[s_me], 6 + k, (px, py, c))
            cp.start()
            wdw_sends.append(cp)
        for p in range(N_PIECES):
            a, n = src_rows[p]
            for src, dst in ((pack_ref.at[pl.ds(a, n)], stage.at[pl.ds(0, n)]),
                             (stage.at[pl.ds(0, n)], wall_ref.at[pl.ds(dst_row(p, s_me), n)])):
                cp = pltpu.make_async_copy(src, dst, loc_sems.at[0])
                cp.start()
                cp.wait()
        for k, (px, py) in enumerate(peers):
            s_p = 2 * px + py
            half_bytes(k).wait_recv()
            for p in range(N_PIECES):
                n = src_rows[p][1]
                h = n // 2
                rows = wall_ref.at[pl.ds(pl.multiple_of(dst_row(p, s_p) + c * h, 32), h)]
                rcopy(rows, rows, 3 + k, (x, y, 1 - c)).start()
        for k in range(3):
            half_bytes(3 + k).wait_recv()
        for k in range(3):
            wdw_sends[k].wait_recv()
        for k in range(6):
            half_bytes(k).wait_send()
        for k in range(3):
            wdw_sends[k].wait_send()
        own_wdw.wait()

    any_spec = pl.BlockSpec(memory_space=pl.ANY)
    return pl.pallas_call(
        body, name="gather_weights",
        out_shape=(jax.ShapeDtypeStruct((WALL_ROWS, D), BF16),
                   jax.ShapeDtypeStruct((N_SHARDS, 32, PLE), F32)),
        in_specs=[any_spec, any_spec], out_specs=(any_spec, any_spec),
        scratch_shapes=[pltpu.VMEM((max(n for _, n in src_rows), D), BF16),
                        pltpu.SemaphoreType.DMA((9,)), pltpu.SemaphoreType.DMA((9,)),
                        pltpu.SemaphoreType.DMA((2,))],
    )(pack, wdw_shard)


def _exchange_halves(gpack):
    def body(g_ref, r1_ref, send_sem, recv_sem):
        x, y, c = _coords()
        cp = pltpu.make_async_remote_copy(
            src_ref=g_ref.at[:, pl.ds(pl.multiple_of((1 - c) * HALF_ROWS, 32), HALF_ROWS), :],
            dst_ref=r1_ref, send_sem=send_sem, recv_sem=recv_sem,
            device_id=(x, y, 1 - c), device_id_type=MESH)
        cp.start()
        cp.wait()

    any_spec = pl.BlockSpec(memory_space=pl.ANY)
    return pl.pallas_call(
        body, name="exchange_halves",
        out_shape=jax.ShapeDtypeStruct((N_SHARDS, HALF_ROWS, D), F32),
        in_specs=[any_spec], out_specs=any_spec,
        scratch_shapes=[pltpu.SemaphoreType.DMA, pltpu.SemaphoreType.DMA],
    )(gpack)


RT = 320


def _chip_sum(cidx, gpack, r1):
    def body(c_ref, g_ref, r_ref, o_ref):
        o_ref[...] = (g_ref[...] + r_ref[...]).astype(BF16)

    nt = HALF_ROWS // RT
    return pl.pallas_call(
        body, name="chip_sum",
        grid_spec=pltpu.PrefetchScalarGridSpec(
            num_scalar_prefetch=1, grid=(N_SHARDS, nt),
            in_specs=[pl.BlockSpec((1, RT, D), lambda s, t, c: (s, c[0] * nt + t, 0)),
                      pl.BlockSpec((1, RT, D), lambda s, t, c: (s, t, 0))],
            out_specs=pl.BlockSpec((1, RT, D), lambda s, t, c: (s, t, 0))),
        out_shape=jax.ShapeDtypeStruct((N_SHARDS, HALF_ROWS, D), BF16),
        compiler_params=_params(("arbitrary", "arbitrary")),
    )(cidx, gpack, r1)


def _send_chip_sums(cs):
    def body(cs_ref, r2_ref, send_sems, recv_sems):
        x, y, c = _coords()
        cps = []
        for k, (px, py) in enumerate(_chip_peers(x, y)):
            cp = pltpu.make_async_remote_copy(
                src_ref=cs_ref.at[2 * px + py], dst_ref=r2_ref.at[k],
                send_sem=send_sems.at[k], recv_sem=recv_sems.at[k],
                device_id=(px, py, c), device_id_type=MESH)
            cp.start()
            cps.append(cp)
        for cp in cps:
            cp.wait_recv()
        for cp in cps:
            cp.wait_send()

    any_spec = pl.BlockSpec(memory_space=pl.ANY)
    return pl.pallas_call(
        body, name="send_chip_sums",
        out_shape=jax.ShapeDtypeStruct((3, HALF_ROWS, D), BF16),
        in_specs=[any_spec], out_specs=any_spec,
        scratch_shapes=[pltpu.SemaphoreType.DMA((3,)), pltpu.SemaphoreType.DMA((3,))],
    )(cs)


def _final_half(sc, gpack, r1, r2):
    def body(sc_ref, g_ref, r_ref, p_ref, o_ref):
        acc = g_ref[0] + r_ref[0]
        for k in range(3):
            acc = acc + p_ref[k].astype(F32)
        o_ref[...] = acc

    nt = HALF_ROWS // RT
    return pl.pallas_call(
        body, name="final_half",
        grid_spec=pltpu.PrefetchScalarGridSpec(
            num_scalar_prefetch=1, grid=(nt,),
            in_specs=[pl.BlockSpec((1, RT, D), lambda t, sc: (sc[0], sc[1] * nt + t, 0)),
                      pl.BlockSpec((1, RT, D), lambda t, sc: (sc[0], t, 0)),
                      pl.BlockSpec((3, RT, D), lambda t, sc: (0, t, 0))],
            out_specs=pl.BlockSpec((RT, D), lambda t, sc: (sc[1] * nt + t, 0))),
        out_shape=jax.ShapeDtypeStruct((PACK_ROWS, D), F32),
        compiler_params=_params(("arbitrary",)),
    )(sc, gpack, r1, r2)


def _swap_halves(fh):
    def body(f_ref, o_ref, send_sem, recv_sem):
        x, y, c = _coords()
        cp = pltpu.make_async_remote_copy(
            src_ref=f_ref.at[pl.ds(pl.multiple_of(c * HALF_ROWS, 32), HALF_ROWS)], dst_ref=o_ref,
            send_sem=send_sem, recv_sem=recv_sem, device_id=(x, y, 1 - c), device_id_type=MESH)
        cp.start()
        cp.wait()

    any_spec = pl.BlockSpec(memory_space=pl.ANY)
    return pl.pallas_call(
        body, name="swap_halves",
        out_shape=jax.ShapeDtypeStruct((HALF_ROWS, D), F32),
        in_specs=[any_spec], out_specs=any_spec,
        scratch_shapes=[pltpu.SemaphoreType.DMA, pltpu.SemaphoreType.DMA],
    )(fh)


VEC_ROWS = 40


def _all_reduce_small(vec):
    def body(v_ref, o_ref, buf, send_sems, recv_sems):
        x, y, c = _coords()
        me = 4 * x + 2 * y + c
        buf[me] = v_ref[...]
        cps = []
        for r in range(1, 8):
            dx, dy, dc = (r >> 2) & 1, (r >> 1) & 1, r & 1
            peer = (1 - x if dx else x, 1 - y if dy else y, 1 - c if dc else c)
            cp = pltpu.make_async_remote_copy(
                src_ref=v_ref, dst_ref=buf.at[me], send_sem=send_sems.at[r - 1],
                recv_sem=recv_sems.at[r - 1], device_id=peer, device_id_type=MESH)
            cp.start()
            cps.append(cp)
        for cp in cps:
            cp.wait_recv()
        for cp in cps:
            cp.wait_send()
        acc = buf[0]
        for d in range(1, 8):
            acc = acc + buf[d]
        o_ref[...] = acc

    vm = pl.BlockSpec(memory_space=pltpu.VMEM)
    return pl.pallas_call(
        body, name="all_reduce_small",
        out_shape=jax.ShapeDtypeStruct((VEC_ROWS, D), F32),
        in_specs=[vm], out_specs=vm,
        scratch_shapes=[pltpu.VMEM((8, VEC_ROWS, D), F32), pltpu.SemaphoreType.DMA((7,)),
                        pltpu.SemaphoreType.DMA((7,))],
    )(vec)


def _inproj(x, ln_pre, wall, tm):
    T = x.shape[0]

    def body(x_ref, g_ref, w_ref, z_ref, h_ref, hs):
        @pl.when(pl.program_id(1) == 0)
        def _():
            xv = x_ref[...]
            r = lax.rsqrt(jnp.mean(xv * xv, axis=-1, keepdims=True) + EPS)
            h = (xv * r * g_ref[...]).astype(BF16)
            hs[...] = h
            h_ref[...] = h

        z_ref[...] = _mm_nt(hs[...], w_ref[...]).astype(BF16)

    return pl.pallas_call(
        body, name="inproj", grid=(T // tm, 7),
        in_specs=[pl.BlockSpec((tm, D), lambda i, j: (i, 0)),
                  pl.BlockSpec((1, D), lambda i, j: (0, 0)),
                  pl.BlockSpec((D, D), lambda i, j: (5 + j, 0))],
        out_specs=(pl.BlockSpec((tm, D), lambda i, j: (i, j)),
                   pl.BlockSpec((tm, D), lambda i, j: (i, 0))),
        out_shape=(jax.ShapeDtypeStruct((T, 7 * D), BF16), jax.ShapeDtypeStruct((T, D), BF16)),
        scratch_shapes=[pltpu.VMEM((tm, D), BF16)],
        compiler_params=_params(("arbitrary", "arbitrary")),
    )(x, ln_pre, wall)


def _kvproj(h, wall, tm):
    T = h.shape[0]

    def body(h_ref, w_ref, o_ref):
        o_ref[...] = _mm_nt(h_ref[...], w_ref[...]).astype(BF16)

    return pl.pallas_call(
        body, name="kvproj", grid=(T // tm,),
        in_specs=[pl.BlockSpec((tm, D), lambda i: (i, 0)),
                  pl.BlockSpec((2 * BLOCK, D), lambda i: ((WT0 + 7 * D) // (2 * BLOCK), 0))],
        out_specs=pl.BlockSpec((tm, 2 * BLOCK), lambda i: (i, 0)),
        out_shape=jax.ShapeDtypeStruct((T, 2 * BLOCK), BF16),
        compiler_params=_params(("arbitrary",)),
    )(h, wall)


HALO = 32
CONV_RC = 64
CONV_LC = 256


def _conv_taps(w_ref, src, r0, lane0, offset_of_tap):
    lanes = pl.ds(lane0, CONV_LC)
    out = None
    for b in range(8):
        taps = [k for k in range(CONV_K) if offset_of_tap(k) % 8 == b]
        if not taps:
            continue
        rows = CONV_RC + (8 if b else 0)
        vb = None
        for k in taps:
            term = w_ref[k:k + 1, lanes] * src[pl.ds(r0 + (offset_of_tap(k) - b), rows), lanes]
            vb = term if vb is None else vb + term
        vb = vb[b:b + CONV_RC] if b else vb
        out = vb if out is None else out + vb
    return out


def _conv_fwd(z, wdw, b_dw, ln_g, ln_b, wall, S, tm):
    T = z.shape[0]
    nt = S // tm
    hb = tm // HALO

    def body(cv_ref, cg_ref, cgate_ref, hcv_ref, hcg_ref, wdw_ref, bdw_ref, lng_ref, lnb_ref, wpw_ref,
             wbrc_ref, ya_ref, y_ref, rstd_ref, pw_ref, ubuf, cbuf):
        t = pl.program_id(1)
        ubuf[HALO:HALO + tm, :] = cv_ref[...].astype(F32) * _sig(cg_ref[...].astype(F32))
        hu = hcv_ref[...].astype(F32) * _sig(hcg_ref[...].astype(F32))
        ubuf[0:HALO, :] = jnp.where(t > 0, hu, 0.0)
        ubuf[HALO + tm:HALO + tm + 8, :] = jnp.zeros((8, D), F32)

        def chunk(ci, carry):
            r0 = pl.multiple_of(ci * CONV_RC, CONV_RC)
            for lg in range(D // CONV_LC):
                acc = _conv_taps(wdw_ref, ubuf, r0, lg * CONV_LC, lambda k: HALO - (CONV_K - 1) + k)
                cbuf[pl.ds(r0, CONV_RC), pl.ds(lg * CONV_LC, CONV_LC)] = acc
            return carry

        lax.fori_loop(0, tm // CONV_RC, chunk, 0)
        cc = cbuf[...] + bdw_ref[...]
        mu = jnp.mean(cc, axis=-1, keepdims=True)
        dd = cc - mu
        rstd = lax.rsqrt(jnp.mean(dd * dd, axis=-1, keepdims=True) + EPS)
        yn = dd * rstd
        y_ref[...] = yn.astype(BF16)
        rstd_ref[...] = rstd
        n = yn * lng_ref[...] + lnb_ref[...]
        s = n * _sig(n)
        pw = _mm(s.astype(BF16), wpw_ref[...])
        pw_ref[...] = pw.astype(BF16)
        gt = cgate_ref[...].astype(F32)
        ya_in = pw * (gt * _sig(gt))
        ya_ref[...] = _mm(ya_in.astype(BF16), wbrc_ref[...]).astype(BF16)

    def row(b, t):
        return b * nt + t

    def halo(b, t):
        return jnp.maximum(row(b, t) * hb - 1, 0)

    vec = pl.BlockSpec((1, D), lambda b, t: (0, 0))
    tile = lambda j: pl.BlockSpec((tm, D), lambda b, t: (row(b, t), j))
    out_tile = pl.BlockSpec((tm, D), lambda b, t: (row(b, t), 0))
    return pl.pallas_call(
        body, name="conv_fwd", grid=(T // S, nt),
        in_specs=[tile(0), tile(1), tile(2),
                  pl.BlockSpec((HALO, D), lambda b, t: (halo(b, t), 0)),
                  pl.BlockSpec((HALO, D), lambda b, t: (halo(b, t), 1)),
                  pl.BlockSpec((32, D), lambda b, t: (0, 0)), vec, vec, vec,
                  pl.BlockSpec((D, D), lambda b, t: (0, 0)),
                  pl.BlockSpec((D, D), lambda b, t: (1, 0))],
        out_specs=(out_tile, out_tile, pl.BlockSpec((tm, 1), lambda b, t: (row(b, t), 0)), out_tile),
        out_shape=(jax.ShapeDtypeStruct((T, D), BF16), jax.ShapeDtypeStruct((T, D), BF16),
                   jax.ShapeDtypeStruct((T, 1), F32), jax.ShapeDtypeStruct((T, D), BF16)),
        scratch_shapes=[pltpu.VMEM((tm + HALO + 8, D), F32), pltpu.VMEM((tm, D), F32)],
        compiler_params=_params(("arbitrary", "arbitrary")),
    )(z, z, z, z, z, wdw, b_dw, ln_g, ln_b, wall, wall)


def _rope(tv, cos, sin):
    lane = lax.broadcasted_iota(jnp.int32, tv.shape, 1) & (HEAD_DIM - 1)
    swapped = jnp.where(lane < ROPE_DIM // 2, pltpu.roll(tv, 2 * HEAD_DIM - ROPE_DIM // 2, 1),
                        pltpu.roll(tv, ROPE_DIM // 2, 1))
    return tv * cos + swapped * sin


def _kv_variants(kv):
    lane = lax.broadcasted_iota(jnp.int32, kv.shape, 1)
    lo = lane < HEAD_DIM
    sw = pltpu.roll(kv, HEAD_DIM, 1)
    z = jnp.zeros_like(kv)
    g0 = (jnp.where(lo, kv, z).astype(BF16), jnp.where(lo, z, sw).astype(BF16))
    g1 = (jnp.where(lo, sw, z).astype(BF16), jnp.where(lo, z, kv).astype(BF16))
    return (g0, g1)


def _band_mask(nq):
    qi = lax.broadcasted_iota(jnp.int32, (nq * BLOCK, 2 * BLOCK), 0) & (BLOCK - 1)
    sj = lax.broadcasted_iota(jnp.int32, (nq * BLOCK, 2 * BLOCK), 1)
    return (sj <= qi + BLOCK) & (sj > qi), sj


def _sink_col(sink_ref, g, e):
    return jnp.concatenate(
        [jnp.full((BLOCK, 1), sink_ref[8 * g + 2 * j + e], F32) for j in range(4)], axis=0)


def _softmax_sink(s, valid, sk):
    s = jnp.where(valid, s, -1e30)
    m = jnp.maximum(jnp.max(s, axis=-1, keepdims=True), sk)
    p = jnp.exp(s - m)
    ps = jnp.exp(sk - m)
    inv = 1.0 / (jnp.sum(p, axis=-1, keepdims=True) + ps)
    return p * inv, ps * inv


def _attn_fwd(z, zkv, cos_t, sin_t, sinks, S, tq):
    T = z.shape[0]
    nt = S // tq
    nq = tq // BLOCK

    def body(sink_ref, q_ref, kv_ref, hkv_ref, cos_ref, sin_ref, hcos_ref, hsin_ref, o_ref):
        t = pl.program_id(1)
        cos = cos_ref[...]
        sin = sin_ref[...]
        kv = jnp.concatenate([hkv_ref[...], kv_ref[...]], axis=0).astype(F32)
        cos_k = jnp.concatenate([hcos_ref[...], cos], axis=0)
        sin_k = jnp.concatenate([hsin_ref[...], sin], axis=0)
        kx = _kv_variants(_rope(kv[:, :BLOCK], cos_k, sin_k))
        vx = _kv_variants(kv[:, BLOCK:])
        band, sj = _band_mask(4)
        qs = [(_rope(q_ref[:, 128 * hp:128 * hp + 128].astype(F32), cos, sin) * 0.125).astype(BF16)
              for hp in range(8)]
        for n in range(nq):
            first = (t == 0) & (n == 0)
            valid = band & (jnp.logical_not(first) | (sj >= BLOCK))
            r0 = n * BLOCK
            for g in range(2):
                lhs = jnp.concatenate([qs[4 * g + j][r0:r0 + BLOCK] for j in range(4)], axis=0)
                acc = jnp.zeros((4 * BLOCK, BLOCK), F32)
                for e in range(2):
                    s = _mm_nt(lhs, kx[g][e][r0:r0 + 2 * BLOCK])
                    p, _ = _softmax_sink(s, valid, _sink_col(sink_ref, g, e))
                    acc = acc + _mm(p.astype(BF16), vx[g][e][r0:r0 + 2 * BLOCK])
                for j in range(4):
                    o_ref[r0:r0 + BLOCK, 128 * (4 * g + j):128 * (4 * g + j) + 128] = (
                        acc[j * BLOCK:(j + 1) * BLOCK].astype(BF16))

    def row(b, t):
        return b * nt + t

    def halo(b, t):
        return jnp.maximum(row(b, t) * nq - 1, 0)

    return pl.pallas_call(
        body, name="attn_fwd", grid=(T // S, nt),
        in_specs=[pl.BlockSpec(memory_space=pltpu.SMEM),
                  pl.BlockSpec((tq, D), lambda b, t: (row(b, t), 3)),
                  pl.BlockSpec((tq, 2 * BLOCK), lambda b, t: (row(b, t), 0)),
                  pl.BlockSpec((BLOCK, 2 * BLOCK), lambda b, t: (halo(b, t), 0)),
                  pl.BlockSpec((tq, BLOCK), lambda b, t: (row(b, t), 0)),
                  pl.BlockSpec((tq, BLOCK), lambda b, t: (row(b, t), 0)),
                  pl.BlockSpec((BLOCK, BLOCK), lambda b, t: (halo(b, t), 0)),
                  pl.BlockSpec((BLOCK, BLOCK), lambda b, t: (halo(b, t), 0))],
        out_specs=pl.BlockSpec((tq, D), lambda b, t: (row(b, t), 0)),
        out_shape=jax.ShapeDtypeStruct((T, D), BF16),
        compiler_params=_params(("arbitrary", "arbitrary")),
    )(sinks, z, zkv, zkv, cos_t, sin_t, cos_t, sin_t)


def _tail_a(x, tgt, p, o, ya, z, ln_post, wall, wppt, tm):
    T = x.shape[0]
    last = T // tm - 1

    def body(x_ref, tgt_ref, p_ref, o_ref, ya_ref, ag_ref, gc_ref, ga_ref, lnp_ref, wbra_ref, wout_ref,
             wpg_ref, wppt_ref, loss_ref, dx1_ref, dm_ref, yb_ref, glnp_ref, gwout_ref, gwpg_ref, gwpp_ref,
             acc_out, acc_pg, sem):
        i = pl.program_id(0)

        @pl.when(i == 0)
        def _():
            acc_out[...] = jnp.zeros_like(acc_out)
            acc_pg[...] = jnp.zeros_like(acc_pg)
            gwpp_ref[...] = jnp.zeros_like(gwpp_ref)
            glnp_ref[...] = jnp.zeros_like(glnp_ref)
            loss_ref[...] = jnp.zeros_like(loss_ref)

        ag = ag_ref[...].astype(F32)
        yb_in = (o_ref[...].astype(F32) * (ag * _sig(ag))).astype(BF16)
        yb = _mm(yb_in, wbra_ref[...])
        yb_ref[...] = yb.astype(BF16)
        m = (_sig(gc_ref[...].astype(F32)) * ya_ref[...].astype(F32)
             + _sig(ga_ref[...].astype(F32)) * yb).astype(BF16)
        mo = _mm(m, wout_ref[...])
        r2 = lax.rsqrt(jnp.mean(mo * mo, axis=-1, keepdims=True) + EPS)
        nrm = mo * r2
        g_post = lnp_ref[...]
        x1 = x_ref[...] + nrm * g_post
        x1b = x1.astype(BF16)
        gate = _sig(_mm(x1b, wpg_ref[...]))
        pb = p_ref[...].astype(BF16)
        pp = _mm_nt(pb, wppt_ref[...])
        err = x1 + gate * pp - tgt_ref[...]
        loss_ref[...] += 0.5 * jnp.sum(jnp.sum(err * err, axis=-1, keepdims=True) * (1.0 / D),
                                       axis=0, keepdims=True)
        dx2 = err * (1.0 / D)
        dgp = (dx2 * pp * gate * (1.0 - gate)).astype(BF16)
        dpp = (dx2 * gate).astype(BF16)
        dx1 = dx2 + _mm_nt(dgp, wpg_ref[...])
        dx1_ref[...] = dx1
        acc_pg[...] += _mm_tn(x1b, dgp)
        gwpp_ref[...] += _mm_tn(dpp, pb)
        glnp_ref[...] += jnp.sum(dx1 * nrm, axis=0, keepdims=True)
        a = dx1 * g_post
        dmo = (r2 * (a - nrm * jnp.mean(a * nrm, axis=-1, keepdims=True))).astype(BF16)
        dm_ref[...] = _mm_nt(dmo, wout_ref[...]).astype(BF16)
        acc_out[...] += _mm_tn(m, dmo)

        @pl.when(i == last)
        def _():
            _flush(acc_out, gwout_ref, sem.at[0])
            _flush(acc_pg, gwpg_ref, sem.at[1])

    tile = pl.BlockSpec((tm, D), lambda i: (i, 0))
    ztile = lambda j: pl.BlockSpec((tm, D), lambda i: (i, j))
    wsq = lambda k: pl.BlockSpec((D, D), lambda i: (k, 0))
    const = lambda shp: pl.BlockSpec(shp, lambda i: (0, 0))
    any_spec = pl.BlockSpec(memory_space=pl.ANY)
    return pl.pallas_call(
        body, name="tail_a", grid=(T // tm,),
        in_specs=[tile, tile, pl.BlockSpec((tm, PLE), lambda i: (i, 0)), tile, tile, ztile(4), ztile(5),
                  ztile(6), const((1, D)), wsq(2), wsq(3), wsq(4), const((D, PLE))],
        out_specs=(const((1, 1)), tile, tile, tile, const((1, D)), any_spec, any_spec, const((D, PLE))),
        out_shape=(jax.ShapeDtypeStruct((1, 1), F32), jax.ShapeDtypeStruct((T, D), F32),
                   jax.ShapeDtypeStruct((T, D), BF16), jax.ShapeDtypeStruct((T, D), BF16),
                   jax.ShapeDtypeStruct((1, D), F32), jax.ShapeDtypeStruct((D, D), F32),
                   jax.ShapeDtypeStruct((D, D), F32), jax.ShapeDtypeStruct((D, PLE), F32)),
        scratch_shapes=[pltpu.VMEM((D, D), F32), pltpu.VMEM((D, D), F32), pltpu.SemaphoreType.DMA((2,))],
        compiler_params=_params(("arbitrary",)),
    )(x, tgt, p, o, ya, z, z, z, ln_post, wall, wall, wall, wppt)


def _dsilu(v, sg):
    return sg * (1.0 + v * (1.0 - sg))


def _tail_b(dm, ya, yb, o, z, pw, y, rstd, ln_g, ln_b, wall, tm):
    T = dm.shape[0]
    last = T // tm - 1

    def body(dm_ref, ya_ref, yb_ref, o_ref, ag_ref, gc_ref, ga_ref, cgate_ref, pw_ref, y_ref, rstd_ref,
             lng_ref, lnb_ref, wpw_ref, wbrc_ref, wbra_ref, dg_ref, do_ref, dcg_ref, dc_ref, gvec_ref,
             gbra_ref, gbrc_ref, gpw_ref, acc_bra, acc_brc, acc_pw, sem):
        i = pl.program_id(0)

        @pl.when(i == 0)
        def _():
            acc_bra[...] = jnp.zeros_like(acc_bra)
            acc_brc[...] = jnp.zeros_like(acc_brc)
            acc_pw[...] = jnp.zeros_like(acc_pw)
            gvec_ref[...] = jnp.zeros_like(gvec_ref)

        dm_v = dm_ref[...].astype(F32)
        sgc = _sig(gc_ref[...].astype(F32))
        sga = _sig(ga_ref[...].astype(F32))
        dya = (dm_v * sgc).astype(BF16)
        dyb = (dm_v * sga).astype(BF16)
        dg_ref[:, D:2 * D] = (dm_v * ya_ref[...].astype(F32) * sgc * (1.0 - sgc)).astype(BF16)
        dg_ref[:, 2 * D:3 * D] = (dm_v * yb_ref[...].astype(F32) * sga * (1.0 - sga)).astype(BF16)
        ag = ag_ref[...].astype(F32)
        sag = _sig(ag)
        sa = ag * sag
        ov = o_ref[...].astype(F32)
        dyb_in = _mm_nt(dyb, wbra_ref[...])
        acc_bra[...] += _mm_tn((ov * sa).astype(BF16), dyb)
        do_ref[...] = (dyb_in * sa).astype(BF16)
        dg_ref[:, 0:D] = (dyb_in * ov * _dsilu(ag, sag)).astype(BF16)
        gt = cgate_ref[...].astype(F32)
        sgt = _sig(gt)
        sgate = gt * sgt
        pw = pw_ref[...].astype(F32)
        dya_in = _mm_nt(dya, wbrc_ref[...])
        acc_brc[...] += _mm_tn((pw * sgate).astype(BF16), dya)
        dpw = (dya_in * sgate).astype(BF16)
        dcg_ref[...] = (dya_in * pw * _dsilu(gt, sgt)).astype(BF16)
        yn = y_ref[...].astype(F32)
        g = lng_ref[...]
        n = yn * g + lnb_ref[...]
        sn = _sig(n)
        acc_pw[...] += _mm_tn((n * sn).astype(BF16), dpw)
        dn = _mm_nt(dpw, wpw_ref[...]) * _dsilu(n, sn)
        gvec_ref[0:1, :] += jnp.sum(dn * yn, axis=0, keepdims=True)
        gvec_ref[1:2, :] += jnp.sum(dn, axis=0, keepdims=True)
        dy = dn * g
        dc = rstd_ref[...] * (dy - jnp.mean(dy, axis=-1, keepdims=True)
                              - yn * jnp.mean(dy * yn, axis=-1, keepdims=True))
        gvec_ref[2:3, :] += jnp.sum(dc, axis=0, keepdims=True)
        dc_ref[...] = dc.astype(BF16)

        @pl.when(i == last)
        def _():
            _flush(acc_bra, gbra_ref, sem.at[0])
            _flush(acc_brc, gbrc_ref, sem.at[1])
            _flush(acc_pw, gpw_ref, sem.at[2])

    tile = pl.BlockSpec((tm, D), lambda i: (i, 0))
    ztile = lambda j: pl.BlockSpec((tm, D), lambda i: (i, j))
    wsq = lambda k: pl.BlockSpec((D, D), lambda i: (k, 0))
    const = lambda shp: pl.BlockSpec(shp, lambda i: (0, 0))
    any_spec = pl.BlockSpec(memory_space=pl.ANY)
    sq = jax.ShapeDtypeStruct((D, D), F32)
    return pl.pallas_call(
        body, name="tail_b", grid=(T // tm,),
        in_specs=[tile, tile, tile, tile, ztile(4), ztile(5), ztile(6), ztile(2), tile, tile,
                  pl.BlockSpec((tm, 1), lambda i: (i, 0)), const((1, D)), const((1, D)), wsq(0), wsq(1),
                  wsq(2)],
        out_specs=(pl.BlockSpec((tm, 3 * D), lambda i: (i, 0)), tile, tile, tile, const((8, D)),
                   any_spec, any_spec, any_spec),
        out_shape=(jax.ShapeDtypeStruct((T, 3 * D), BF16), jax.ShapeDtypeStruct((T, D), BF16),
                   jax.ShapeDtypeStruct((T, D), BF16), jax.ShapeDtypeStruct((T, D), BF16),
                   jax.ShapeDtypeStruct((8, D), F32), sq, sq, sq),
        scratch_shapes=[pltpu.VMEM((D, D), F32), pltpu.VMEM((D, D), F32), pltpu.VMEM((D, D), F32),
                        pltpu.SemaphoreType.DMA((3,))],
        compiler_params=_params(("arbitrary",)),
    )(dm, ya, yb, o, z, z, z, z, pw, y, rstd, ln_g, ln_b, wall, wall, wall)


def _conv_bwd(dc, z, wdw, S, tm):
    T = dc.shape[0]
    nt = S // tm
    hb = tm // HALO
    nrows = T // HALO

    def body(dc_ref, hdc_ref, cv_ref, cg_ref, hcv_ref, hcg_ref, wdw_ref, dz_ref, gw_ref, ubuf, dcbuf, dubuf,
             dwacc, shbuf):
        b = pl.program_id(0)
        t = pl.program_id(1)

        @pl.when((b == 0) & (t == 0))
        def _():
            dwacc[...] = jnp.zeros_like(dwacc)

        cv = cv_ref[...].astype(F32)
        sg = _sig(cg_ref[...].astype(F32))
        ubuf[HALO:HALO + tm, :] = cv * sg
        hu = hcv_ref[...].astype(F32) * _sig(hcg_ref[...].astype(F32))
        ubuf[0:HALO, :] = jnp.where(t > 0, hu, 0.0)
        ubuf[HALO + tm:HALO + tm + 8, :] = jnp.zeros((8, D), F32)
        dcbuf[0:tm, :] = dc_ref[...].astype(F32)
        dcbuf[tm:tm + HALO, :] = jnp.where(t < nt - 1, hdc_ref[...].astype(F32), 0.0)
        dcbuf[tm + HALO:tm + HALO + 8, :] = jnp.zeros((8, D), F32)

        def chunk(ci, carry):
            r0 = pl.multiple_of(ci * CONV_RC, CONV_RC)
            for lg in range(D // CONV_LC):
                l0 = lg * CONV_LC
                dubuf[pl.ds(r0, CONV_RC), pl.ds(l0, CONV_LC)] = _conv_taps(
                    wdw_ref, dcbuf, r0, l0, lambda k: CONV_K - 1 - k)
                dcc = dcbuf[pl.ds(r0, CONV_RC), pl.ds(l0, CONV_LC)]
                zero8 = jnp.zeros((8, CONV_LC), F32)
                dcz = jnp.concatenate([zero8, dcc, zero8], axis=0)
                for bb in range(8):
                    taps = [k for k in range(CONV_K) if (HALO - (CONV_K - 1) + k) % 8 == bb]
                    if not taps:
                        continue
                    rows = CONV_RC + (8 if bb else 0)
                    if bb:
                        shbuf[bb] = dcz[8 - bb:8 - bb + rows]
                    for k in taps:
                        a8 = HALO - (CONV_K - 1) + k - bb
                        dcs = shbuf[bb] if bb else dcc
                        prod = dcs * ubuf[pl.ds(r0 + a8, rows), pl.ds(l0, CONV_LC)]
                        part = prod[0:8]
                        for q in range(1, rows // 8):
                            part = part + prod[8 * q:8 * q + 8]
                        dwacc[8 * k:8 * k + 8, pl.ds(l0, CONV_LC)] += part
            return carry

        lax.fori_loop(0, tm // CONV_RC, chunk, 0)
        du = dubuf[...]
        dz_ref[:, 0:D] = (du * sg).astype(BF16)
        dz_ref[:, D:2 * D] = (du * cv * sg * (1.0 - sg)).astype(BF16)

        @pl.when((b == pl.num_programs(0) - 1) & (t == nt - 1))
        def _():
            for k in range(32):
                gw_ref[k:k + 1, :] = jnp.sum(dwacc[8 * k:8 * k + 8, :], axis=0, keepdims=True)

    def row(b, t):
        return b * nt + t

    def prev_halo(b, t):
        return jnp.maximum(row(b, t) * hb - 1, 0)

    def next_halo(b, t):
        return jnp.minimum((row(b, t) + 1) * hb, nrows - 1)

    return pl.pallas_call(
        body, name="conv_bwd", grid=(T // S, nt),
        in_specs=[pl.BlockSpec((tm, D), lambda b, t: (row(b, t), 0)),
                  pl.BlockSpec((HALO, D), lambda b, t: (next_halo(b, t), 0)),
                  pl.BlockSpec((tm, D), lambda b, t: (row(b, t), 0)),
                  pl.BlockSpec((tm, D), lambda b, t: (row(b, t), 1)),
                  pl.BlockSpec((HALO, D), lambda b, t: (prev_halo(b, t), 0)),
                  pl.BlockSpec((HALO, D), lambda b, t: (prev_halo(b, t), 1)),
                  pl.BlockSpec((32, D), lambda b, t: (0, 0))],
        out_specs=(pl.BlockSpec((tm, 2 * D), lambda b, t: (row(b, t), 0)),
                   pl.BlockSpec((32, D), lambda b, t: (0, 0))),
        out_shape=(jax.ShapeDtypeStruct((T, 2 * D), BF16), jax.ShapeDtypeStruct((32, D), F32)),
        scratch_shapes=[pltpu.VMEM((tm + HALO + 8, D), F32), pltpu.VMEM((tm + HALO + 8, D), F32),
                        pltpu.VMEM((tm, D), F32), pltpu.VMEM((8 * 32, D), F32),
                        pltpu.VMEM((8, CONV_RC + 8, CONV_LC), F32)],
        compiler_params=_params(("arbitrary", "arbitrary")),
    )(dc, dc, z, z, z, z, wdw)


def _attn_bwd(z, zkv, o, do, cos_t, sin_t, sinks, S, tq):
    T = z.shape[0]
    nt = S // tq
    nq = tq // BLOCK

    def body(sink_ref, q_ref, kv_ref, hkv_ref, o_ref, do_ref, cos_ref, sin_ref, hcos_ref, hsin_ref,
             dq_ref, dkv_ref, gs_ref, carry, dkacc, dvacc):
        b = pl.program_id(0)
        tt = pl.program_id(1)
        t = nt - 1 - tt

        @pl.when((b == 0) & (tt == 0))
        def _():
            gs_ref[...] = jnp.zeros_like(gs_ref)

        @pl.when(tt == 0)
        def _():
            carry[...] = jnp.zeros_like(carry)

        cos = cos_ref[...]
        sin = sin_ref[...]
        kv = jnp.concatenate([hkv_ref[...], kv_ref[...]], axis=0).astype(F32)
        cos_k = jnp.concatenate([hcos_ref[...], cos], axis=0)
        sin_k = jnp.concatenate([hsin_ref[...], sin], axis=0)
        kx = _kv_variants(_rope(kv[:, :BLOCK], cos_k, sin_k))
        vx = _kv_variants(kv[:, BLOCK:])
        band, sj = _band_mask(4)
        lane = lax.broadcasted_iota(jnp.int32, (4 * BLOCK, BLOCK), 1)
        lo = lane < HEAD_DIM
        lo_k = lax.broadcasted_iota(jnp.int32, (2 * BLOCK, BLOCK), 1) < HEAD_DIM
        qs = [(_rope(q_ref[:, 128 * hp:128 * hp + 128].astype(F32), cos, sin) * 0.125).astype(BF16)
              for hp in range(8)]
        dkacc[...] = jnp.zeros_like(dkacc)
        dvacc[...] = jnp.zeros_like(dvacc)
        gsum = jnp.zeros((1, BLOCK), F32)
        hlane = lax.broadcasted_iota(jnp.int32, (1, BLOCK), 1)
        for n in range(nq):
            first = (t == 0) & (n == 0)
            valid = band & (jnp.logical_not(first) | (sj >= BLOCK))
            r0 = n * BLOCK
            for g in range(2):
                cols = [slice(128 * (4 * g + j), 128 * (4 * g + j) + 128) for j in range(4)]
                lhs = jnp.concatenate([qs[4 * g + j][r0:r0 + BLOCK] for j in range(4)], axis=0)
                dov = jnp.concatenate([do_ref[r0:r0 + BLOCK, cs] for cs in cols], axis=0)
                prod = dov.astype(F32) * jnp.concatenate(
                    [o_ref[r0:r0 + BLOCK, cs] for cs in cols], axis=0).astype(F32)
                dq = jnp.zeros((4 * BLOCK, BLOCK), F32)
                dk_e = []
                dv_e = []
                for e in range(2):
                    kw = kx[g][e][r0:r0 + 2 * BLOCK]
                    vw = vx[g][e][r0:r0 + 2 * BLOCK]
                    s = _mm_nt(lhs, kw)
                    p, psink = _softmax_sink(s, valid, _sink_col(sink_ref, g, e))
                    delta = jnp.sum(jnp.where(lo if e == 0 else jnp.logical_not(lo), prod, 0.0),
                                    axis=-1, keepdims=True)
                    ds = (p * (_mm_nt(dov, vw) - delta)).astype(BF16)
                    dq = dq + _mm(ds, kw)
                    dk_e.append(_mm_tn(ds, lhs))
                    dv_e.append(_mm_tn(p.astype(BF16), dov))
                    gs = -psink * delta
                    for j in range(4):
                        tot = jnp.sum(gs[j * BLOCK:(j + 1) * BLOCK], axis=0, keepdims=True)
                        gsum = gsum + jnp.where(hlane == 8 * g + 2 * j + e, tot, 0.0)
                for acc, parts in ((dkacc, dk_e), (dvacc, dv_e)):
                    if g == 0:
                        both = jnp.where(lo_k, parts[0] + pltpu.roll(parts[1], HEAD_DIM, 1), 0.0)
                    else:
                        both = jnp.where(lo_k, 0.0, parts[1] + pltpu.roll(parts[0], HEAD_DIM, 1))
                    acc[r0:r0 + 2 * BLOCK, :] += both
                for j in range(4):
                    dqj = _rope(dq[j * BLOCK:(j + 1) * BLOCK] * 0.125, cos[r0:r0 + BLOCK],
                                -sin[r0:r0 + BLOCK])
                    dq_ref[r0:r0 + BLOCK, cols[j]] = dqj.astype(BF16)
        gs_ref[0:1, :] += gsum
        dk_all = dkacc[...]
        dv_all = dvacc[...]
        last_rows = slice(tq, tq + BLOCK)
        dk_last = dk_all[last_rows] + carry[0:BLOCK, :]
        dv_last = dv_all[last_rows] + carry[BLOCK:2 * BLOCK, :]
        carry[0:BLOCK, :] = dk_all[0:BLOCK]
        carry[BLOCK:2 * BLOCK, :] = dv_all[0:BLOCK]
        if nq > 1:
            dk_t = jnp.concatenate([dk_all[BLOCK:tq], dk_last], axis=0)
            dv_t = jnp.concatenate([dv_all[BLOCK:tq], dv_last], axis=0)
        else:
            dk_t, dv_t = dk_last, dv_last
        dkv_ref[:, 0:BLOCK] = _rope(dk_t, cos, -sin).astype(BF16)
        dkv_ref[:, BLOCK:2 * BLOCK] = dv_t.astype(BF16)

    def row(b, tt):
        return b * nt + (nt - 1 - tt)

    def halo(b, tt):
        return jnp.maximum(row(b, tt) * nq - 1, 0)

    tile = pl.BlockSpec((tq, D), lambda b, tt: (row(b, tt), 0))
    return pl.pallas_call(
        body, name="attn_bwd", grid=(T // S, nt),
        in_specs=[pl.BlockSpec(memory_space=pltpu.SMEM),
                  pl.BlockSpec((tq, D), lambda b, tt: (row(b, tt), 3)),
                  pl.BlockSpec((tq, 2 * BLOCK), lambda b, tt: (row(b, tt), 0)),
                  pl.BlockSpec((BLOCK, 2 * BLOCK), lambda b, tt: (halo(b, tt), 0)),
                  tile, tile,
                  pl.BlockSpec((tq, BLOCK), lambda b, tt: (row(b, tt), 0)),
                  pl.BlockSpec((tq, BLOCK), lambda b, tt: (row(b, tt), 0)),
                  pl.BlockSpec((BLOCK, BLOCK), lambda b, tt: (halo(b, tt), 0)),
                  pl.BlockSpec((BLOCK, BLOCK), lambda b, tt: (halo(b, tt), 0))],
        out_specs=(tile, pl.BlockSpec((tq, 2 * BLOCK), lambda b, tt: (row(b, tt), 0)),
                   pl.BlockSpec((8, BLOCK), lambda b, tt: (0, 0))),
        out_shape=(jax.ShapeDtypeStruct((T, D), BF16), jax.ShapeDtypeStruct((T, 2 * BLOCK), BF16),
                   jax.ShapeDtypeStruct((8, BLOCK), F32)),
        scratch_shapes=[pltpu.VMEM((2 * BLOCK, BLOCK), F32), pltpu.VMEM((tq + BLOCK, BLOCK), F32),
                        pltpu.VMEM((tq + BLOCK, BLOCK), F32)],
        compiler_params=_params(("arbitrary", "arbitrary")),
    )(sinks, z, zkv, zkv, o, do, cos_t, sin_t, cos_t, sin_t)


def _dh(dz_cvg, dz_cgate, dz_q, dz_gates, dz_kv, wall, x, dx1, ln_pre, tm):
    T = x.shape[0]

    def body(cvg_ref, cgate_ref, q_ref, gates_ref, kv_ref, w_ref, wkv_ref, x_ref, dx1_ref, g_ref, gx_ref,
             glp_ref, acc):
        i = pl.program_id(0)
        k = pl.program_id(1)

        @pl.when((i == 0) & (k == 0))
        def _():
            glp_ref[...] = jnp.zeros_like(glp_ref)

        @pl.when(k == 0)
        def _():
            acc[...] = jnp.zeros_like(acc)

        for lo, hi, ref in ((0, 2, cvg_ref), (2, 3, cgate_ref), (3, 4, q_ref), (4, 7, gates_ref)):
            @pl.when((k >= lo) & (k < hi))
            def _(ref=ref):
                acc[...] += _mm(ref[...], w_ref[...])

        @pl.when(k == 7)
        def _():
            dh = acc[...] + _mm(kv_ref[...], wkv_ref[...])
            xv = x_ref[...]
            r = lax.rsqrt(jnp.mean(xv * xv, axis=-1, keepdims=True) + EPS)
            xr = xv * r
            glp_ref[...] += jnp.sum(dh * xr, axis=0, keepdims=True)
            a = dh * g_ref[...]
            gx_ref[...] = dx1_ref[...] + r * (a - xr * jnp.mean(a * xr, axis=-1, keepdims=True))

    def grp(lo, n):
        return pl.BlockSpec((tm, D), lambda i, k: (i, jnp.clip(k - lo, 0, n - 1)))

    tile = pl.BlockSpec((tm, D), lambda i, k: (i, 0))
    return pl.pallas_call(
        body, name="dh", grid=(T // tm, 8),
        in_specs=[grp(0, 2), grp(2, 1), grp(3, 1), grp(4, 3),
                  pl.BlockSpec((tm, 2 * BLOCK), lambda i, k: (i, 0)),
                  pl.BlockSpec((D, D), lambda i, k: (5 + jnp.minimum(k, 6), 0)),
                  pl.BlockSpec((2 * BLOCK, D), lambda i, k: ((WT0 + 7 * D) // (2 * BLOCK), 0)),
                  tile, tile, pl.BlockSpec((1, D), lambda i, k: (0, 0))],
        out_specs=(tile, pl.BlockSpec((1, D), lambda i, k: (0, 0))),
        out_shape=(jax.ShapeDtypeStruct((T, D), F32), jax.ShapeDtypeStruct((1, D), F32)),
        scratch_shapes=[pltpu.VMEM((tm, D), F32)],
        compiler_params=_params(("arbitrary", "arbitrary")),
    )(dz_cvg, dz_cgate, dz_q, dz_gates, dz_kv, wall, wall, x, dx1, ln_pre)


def _gwt(dz, h, tt, name):
    T, n = dz.shape
    bw = min(n, D)
    last = T // tt - 1

    def body(dz_ref, h_ref, o_ref, acc):
        t = pl.program_id(1)

        @pl.when(t == 0)
        def _():
            acc[...] = jnp.zeros_like(acc)

        acc[...] += _mm_tn(dz_ref[...], h_ref[...])

        @pl.when(t == last)
        def _():
            o_ref[...] = acc[...]

    return pl.pallas_call(
        body, name=name, grid=(n // bw, T // tt),
        in_specs=[pl.BlockSpec((tt, bw), lambda j, t: (t, j)), pl.BlockSpec((tt, D), lambda j, t: (t, 0))],
        out_specs=pl.BlockSpec((bw, D), lambda j, t: (j, 0)),
        out_shape=jax.ShapeDtypeStruct((n, D), F32),
        scratch_shapes=[pltpu.VMEM((bw, D), F32)],
        compiler_params=_params(("arbitrary", "arbitrary")),
    )(dz, h)


_BC1 = 1.0 - ADAM_B1 ** ADAM_STEP
_BC2 = 1.0 - ADAM_B2 ** ADAM_STEP


def _adamw_math(w, g, m, v):
    m = ADAM_B1 * m + (1.0 - ADAM_B1) * g
    v = ADAM_B2 * v + (1.0 - ADAM_B2) * (g * g)
    delta = -ADAM_LR * ((m / _BC1) / (jnp.sqrt(v / _BC2) + ADAM_EPS) + ADAM_WD * w)
    return delta, m, v


def _adamw_rows(g, w, m, v, rows, name):
    R, C = w.shape

    def body(g_ref, w_ref, m_ref, v_ref, d_ref, nm_ref, nv_ref):
        d, nm, nv = _adamw_math(w_ref[...], g_ref[...], m_ref[...], v_ref[...])
        d_ref[...] = d
        nm_ref[...] = nm
        nv_ref[...] = nv

    spec = pl.BlockSpec((rows, C), lambda i: (i, 0))
    shp = jax.ShapeDtypeStruct((R, C), F32)
    return pl.pallas_call(
        body, name=name, grid=(R // rows,), in_specs=[spec] * 4, out_specs=(spec,) * 3,
        out_shape=(shp,) * 3, compiler_params=_params(("arbitrary",)),
    )(g, w, m, v)


def _adamw_square(gfin, ws, ms, vs):
    rb = 64
    nb = SQ_SHARD // rb

    def body(*refs):
        g_refs = refs[0:5]
        w_refs, m_refs, v_refs = refs[5:10], refs[10:15], refs[15:20]
        outs = refs[20:]
        for k in range(5):
            gk = g_refs[k][...]
            d, nm, nv = _adamw_math(w_refs[k][...], gk, m_refs[k][...], v_refs[k][...])
            outs[4 * k][...] = gk
            outs[4 * k + 1][...] = d
            outs[4 * k + 2][...] = nm
            outs[4 * k + 3][...] = nv

    spec = pl.BlockSpec((rb, D), lambda i: (i, 0))
    gspecs = [pl.BlockSpec((rb, D), lambda i, k=k: ((WIN_SHARD + SQ_SHARD * k) // rb + i, 0))
              for k in range(5)]
    shp = jax.ShapeDtypeStruct((SQ_SHARD, D), F32)
    res = pl.pallas_call(
        body, name="adamw_square", grid=(nb,), in_specs=gspecs + [spec] * 15, out_specs=(spec,) * 20,
        out_shape=(shp,) * 20, compiler_params=_params(("arbitrary",)),
    )(*([gfin] * 5), *ws, *ms, *vs)
    return [tuple(res[4 * k:4 * k + 4]) for k in range(5)]


def _adamw_small(gs, ws, ms, vs):
    n = len(gs)

    def body(*refs):
        outs = refs[4 * n:]
        for k in range(n):
            d, nm, nv = _adamw_math(refs[n + k][...], refs[k][...], refs[2 * n + k][...],
                                    refs[3 * n + k][...])
            outs[3 * k][...] = d
            outs[3 * k + 1][...] = nm
            outs[3 * k + 2][...] = nv

    vm = pl.BlockSpec(memory_space=pltpu.VMEM)
    shapes = []
    for w in ws:
        shapes += [jax.ShapeDtypeStruct(w.shape, F32)] * 3
    res = pl.pallas_call(
        body, name="adamw_small", in_specs=[vm] * (4 * n), out_specs=(vm,) * (3 * n),
        out_shape=tuple(shapes),
    )(*gs, *ws, *ms, *vs)
    return [tuple(res[3 * k:3 * k + 3]) for k in range(n)]


def _rope_tables(positions):
    inv = jnp.power(ROPE_THETA, -jnp.arange(0, ROPE_DIM, 2, dtype=F32) / ROPE_DIM)
    inv_h = jnp.concatenate([inv, inv, jnp.zeros((HEAD_DIM - ROPE_DIM,), F32)])
    sign_h = np.array([-1.0] * (ROPE_DIM // 2) + [1.0] * (ROPE_DIM // 2) + [0.0] * (HEAD_DIM - ROPE_DIM),
                      np.float32)
    ang = positions.astype(F32).reshape(-1, 1) * jnp.concatenate([inv_h, inv_h])[None, :]
    return jnp.cos(ang), jnp.sin(ang) * np.concatenate([sign_h, sign_h])[None, :]


def _f32_to_bf16_rows(a):
    return lax.bitcast_convert_type(a, BF16).reshape(a.shape[0], 2 * a.shape[1])


def kernel(x, p, positions, w_in, ln_pre, ln_post, w_dw, b_dw, conv_ln_g, conv_ln_b, w_pw, sinks, w_br_conv, w_br_attn, w_out, w_ple_gate, w_ple_proj, loss_target, m_w_in, m_ln_pre, m_ln_post, m_w_dw, m_b_dw, m_conv_ln_g, m_conv_ln_b, m_w_pw, m_sinks, m_w_br_conv, m_w_br_attn, m_w_out, m_w_ple_gate, m_w_ple_proj, v_w_in, v_ln_pre, v_ln_post, v_w_dw, v_b_dw, v_conv_ln_g, v_conv_ln_b, v_w_pw, v_sinks, v_w_br_conv, v_w_br_attn, v_w_out, v_w_ple_gate, v_w_ple_proj):
    nb, S, _ = x.shape
    T = nb * S
    xc = lax.axis_index("x")
    yc = lax.axis_index("y")
    cc = lax.axis_index("c")
    shard = 2 * xc + yc

    sq_w = (w_pw, w_br_conv, w_br_attn, w_out, w_ple_gate)
    pack = jnp.concatenate(
        [w_in[0].T] + [w[0] for w in sq_w] + [w_ple_proj[0].T.reshape(WPP_SHARD, D)], axis=0).astype(BF16)
    wdw_shard = jnp.pad(w_dw[0], ((0, 1), (0, 0)))
    wall, wdw_all = _gather_weights(pack, wdw_shard)
    wdw = jnp.concatenate([wdw_all[s] for s in range(N_SHARDS)], axis=1)
    wppt = wall[WPP0:WALL_ROWS].reshape(D, PLE)

    x2 = x.reshape(T, D)
    tgt = loss_target.reshape(T, D)
    p2 = p.reshape(T, PLE)
    cos_t, sin_t = _rope_tables(positions)
    sinks1 = sinks.reshape(N_HEADS)

    tm_big = min(TILE_PROJ, T)
    tm = min(TILE_TOKEN, S)
    tq = min(TILE_ATTN, S)

    z, h = _inproj(x2, ln_pre, wall, tm_big)
    zkv = _kvproj(h, wall, tm_big)
    ya, y, rstd, pw = _conv_fwd(z, wdw, b_dw, conv_ln_g, conv_ln_b, wall, S, tm)
    o = _attn_fwd(z, zkv, cos_t, sin_t, sinks1, S, tq)
    loss_p, dx1, dm, yb, g_ln_post, gw_out, gw_pg, gw_ppt = _tail_a(
        x2, tgt, p2, o, ya, z, ln_post, wall, wppt, tm)

    dgates, do, dcg, dc, gvec, gw_bra, gw_brc, gw_pw = _tail_b(
        dm, ya, yb, o, z, pw, y, rstd, conv_ln_g, conv_ln_b, wall, tm)
    dz_cvg, g_wdw = _conv_bwd(dc, z, wdw, S, tm)
    dq, dkv, g_sinks = _attn_bwd(z, zkv, o, do, cos_t, sin_t, sinks1, S, tq)
    gx, g_ln_pre = _dh(dz_cvg, dcg, dq, dgates, dkv, wall, x2, dx1, ln_pre, min(512, T))
    gwt = jnp.concatenate(
        [_gwt(dz_cvg, h, tm_big, "gwt_cvg"), _gwt(dcg, h, tm_big, "gwt_cgate"), _gwt(dq, h, tm_big, "gwt_q"),
         _gwt(dgates, h, tm_big, "gwt_gates"), _gwt(dkv, h, tm_big, "gwt_kv")], axis=0)

    gwall = jnp.concatenate([gw_pw, gw_brc, gw_bra, gw_out, gw_pg, gwt, gw_ppt.reshape(PLE, D)], axis=0)
    gpack = jnp.stack([jnp.concatenate([gwall[r:r + n] for (_, n, r) in _pieces(s)], axis=0)
                       for s in range(N_SHARDS)])
    cidx = jnp.reshape(cc, (1,)).astype(jnp.int32)
    scidx = jnp.stack([shard, cc]).astype(jnp.int32)
    r1 = _exchange_halves(gpack)
    cs = _chip_sum(cidx, gpack, r1)
    r2 = _send_chip_sums(cs)
    fh = _final_half(scidx, gpack, r1, r2)
    gfin = lax.dynamic_update_slice(fh, _swap_halves(fh), ((1 - cc) * HALF_ROWS, 0))

    row37 = jnp.concatenate([g_sinks[0:1, 0:N_HEADS], loss_p, jnp.zeros((1, D - N_HEADS - 1), F32)], axis=1)
    vec = jnp.concatenate([g_wdw, g_ln_pre, g_ln_post, gvec[2:3], gvec[0:1], gvec[1:2], row37,
                           jnp.zeros((VEC_ROWS - 38, D), F32)], axis=0)
    tot = _all_reduce_small(vec)

    g_w_in = gfin[0:WIN_SHARD].T[None]
    d_w_in, nm_w_in, nv_w_in = _adamw_rows(g_w_in[0], w_in[0], m_w_in[0], v_w_in[0], 128, "adamw_w_in")
    sq_m = (m_w_pw, m_w_br_conv, m_w_br_attn, m_w_out, m_w_ple_gate)
    sq_v = (v_w_pw, v_w_br_conv, v_w_br_attn, v_w_out, v_w_ple_gate)
    sq_res = _adamw_square(gfin, [w[0] for w in sq_w], [m[0] for m in sq_m], [v[0] for v in sq_v])
    g_wpp = gfin[WIN_SHARD + 5 * SQ_SHARD:PACK_ROWS].reshape(PLE, PLE).T
    g_dw_all = tot[0:CONV_K]
    g_dw = lax.dynamic_slice_in_dim(g_dw_all, shard * PLE, PLE, axis=1)
    small_g = [g_wpp, g_dw, tot[32:33], tot[33:34], tot[34:35], tot[35:36], tot[36:37],
               tot[37:38, 0:N_HEADS]]
    small_w = [w_ple_proj[0], w_dw[0], ln_pre, ln_post, b_dw, conv_ln_g, conv_ln_b, sinks]
    small_m = [m_w_ple_proj[0], m_w_dw[0], m_ln_pre, m_ln_post, m_b_dw, m_conv_ln_g, m_conv_ln_b, m_sinks]
    small_v = [v_w_ple_proj[0], v_w_dw[0], v_ln_pre, v_ln_post, v_b_dw, v_conv_ln_g, v_conv_ln_b, v_sinks]
    small = _adamw_small(small_g, small_w, small_m, small_v)

    loss = tot[37, N_HEADS]
    grads = [g_w_in, small_g[2], small_g[3], g_dw[None], small_g[4], small_g[5], small_g[6],
             sq_res[0][0][None], small_g[7], sq_res[1][0][None], sq_res[2][0][None], sq_res[3][0][None],
             sq_res[4][0][None], g_wpp[None]]

    def triple(i):
        w_in_t = (d_w_in[None], nm_w_in[None], nv_w_in[None])
        sq = lambda k: tuple(a[None] for a in sq_res[k][1:4])
        sm = lambda k, lead: tuple(a[None] if lead else a for a in small[k])
        return [w_in_t[i], sm(2, False)[i], sm(3, False)[i], sm(1, True)[i], sm(4, False)[i],
                sm(5, False)[i], sm(6, False)[i], sq(0)[i], sm(7, False)[i], sq(1)[i], sq(2)[i], sq(3)[i],
                sq(4)[i], sm(0, True)[i]]

    return (loss, gx.reshape(nb, S, D), *grads, *triple(0), *triple(1), *triple(2))
```

```python
import functools

import jax
import jax.numpy as jnp
import numpy as np
from jax import lax
from jax.experimental import pallas as pl
from jax.experimental.pallas import tpu as pltpu

F32 = jnp.float32
BF16 = jnp.bfloat16

D = 1024
PLE = 256
N_HEADS = 16
HEAD_DIM = 64
BLOCK = 128
CONV_K = 31
ROPE_DIM = 16
ROPE_THETA = 500000.0
EPS = 1e-6
IN_WIDTH = 7424
N_SHARDS = 4

ADAM_LR = 0.001
ADAM_B1 = 0.9
ADAM_B2 = 0.999
ADAM_EPS = 1e-08
ADAM_WD = 0.01
ADAM_STEP = 10

SQ_NAMES = ("w_pw", "w_br_conv", "w_br_attn", "w_out", "w_ple_gate")
WT0 = 5 * D
WPP0 = WT0 + IN_WIDTH
WALL_ROWS = WPP0 + PLE
WIN_SHARD = IN_WIDTH // N_SHARDS
SQ_SHARD = D // N_SHARDS
WPP_SHARD = PLE * PLE // D
PACK_ROWS = WIN_SHARD + 5 * SQ_SHARD + WPP_SHARD
HALF_ROWS = PACK_ROWS // 2
VMEM_LIMIT = 56 * 1024 * 1024
MESH = pl.DeviceIdType.MESH
TILE_PROJ = 1024
TILE_TOKEN = 256
TILE_ATTN = 512


ZB_AGATE, ZB_GCONV, ZB_GATTN, ZB_CGATE, ZB_CVAL, ZB_CGLU, ZB_Q = range(7)
ZKV = 7 * D
_SEGMENTS = ((0, D, ZB_CVAL * D), (D, D, ZB_CGLU * D), (2 * D, D, ZB_CGATE * D), (3 * D, D, ZB_Q * D),
             (4 * D, 2 * BLOCK, ZKV), (4 * D + 2 * BLOCK, D, ZB_AGATE * D),
             (5 * D + 2 * BLOCK, D, ZB_GCONV * D), (6 * D + 2 * BLOCK, D, ZB_GATTN * D))
_WT_CUTS = (0, 192, 640, 1216, WIN_SHARD)


def _zp_row(o):
    for a, w, zp in _SEGMENTS:
        if a <= o < a + w:
            return zp + o - a
    raise ValueError(o)


def _pieces(s):
    out = []
    for a, b in zip(_WT_CUTS[:-1], _WT_CUTS[1:]):
        first = _zp_row(WIN_SHARD * s + a)
        assert _zp_row(WIN_SHARD * s + b - 1) == first + b - a - 1
        out.append((a, b - a, WT0 + first))
    for k in range(5):
        out.append((WIN_SHARD + SQ_SHARD * k, SQ_SHARD, D * k + SQ_SHARD * s))
    out.append((WIN_SHARD + 5 * SQ_SHARD, WPP_SHARD, WPP0 + WPP_SHARD * s))
    return out


N_PIECES = len(_pieces(0))


def _wall_segments(wall0, rows):
    out = []
    for s in range(N_SHARDS):
        for pr, n, wr in _pieces(s):
            lo, hi = max(wr, wall0), min(wr + n, wall0 + rows)
            if lo < hi:
                out.append((lo - wall0, hi - lo, s, pr + lo - wr))
    assert sum(n for _, n, _, _ in out) == rows
    return out


def _sel(s, vals):
    r = jnp.int32(vals[0])
    for i in range(1, len(vals)):
        r = jnp.where(s == i, jnp.int32(vals[i]), r)
    return r


def _sig(x):
    return 1.0 / (1.0 + jnp.exp(-x))


def _mm(a, b):
    return lax.dot_general(a, b, (((1,), (0,)), ((), ())), preferred_element_type=F32)


def _mm_nt(a, b):
    return lax.dot_general(a, b, (((1,), (1,)), ((), ())), preferred_element_type=F32)


def _mm_tn(a, b):
    return lax.dot_general(a, b, (((0,), (0,)), ((), ())), preferred_element_type=F32)


def _params(sem=None):
    return pltpu.CompilerParams(dimension_semantics=sem, vmem_limit_bytes=VMEM_LIMIT)


def _flush_to_pack(acc_ref, gpack_ref, wall0, sem):
    for r, n, s, pr in _wall_segments(wall0, acc_ref.shape[0]):
        cp = pltpu.make_async_copy(acc_ref.at[pl.ds(r, n)], gpack_ref.at[s, pl.ds(pr, n)], sem)
        cp.start()
        cp.wait()


def _coords():
    return lax.axis_index("x"), lax.axis_index("y"), lax.axis_index("c")


def _chip_peers(x, y):
    return [(1 - x, y), (x, 1 - y), (1 - x, 1 - y)]


def _gather_weights(pack, wdw_shard):
    tables = [[_pieces(s)[p][2] for s in range(N_SHARDS)] for p in range(N_PIECES)]
    src_rows = [(_pieces(0)[p][0], _pieces(0)[p][1]) for p in range(N_PIECES)]

    def body(pack_ref, wdw_ref, wall_ref, wdwall_ref, stage, send_sems, recv_sems, loc_sems):
        x, y, c = _coords()
        s_me = 2 * x + y
        peers = _chip_peers(x, y)

        def dst_row(p, s):
            return pl.multiple_of(_sel(s, tables[p]), 32)

        def rcopy(src, dst, k, dev):
            return pltpu.make_async_remote_copy(
                src_ref=src, dst_ref=dst, send_sem=send_sems.at[k], recv_sem=recv_sems.at[k],
                device_id=dev, device_id_type=MESH)

        def half_bytes(k):
            rows = wall_ref.at[pl.ds(0, HALF_ROWS)]
            return rcopy(rows, rows, k, (x, y, c))

        own_wdw = pltpu.make_async_copy(wdw_ref, wdwall_ref.at[s_me], loc_sems.at[1])
        own_wdw.start()
        wdw_sends = []
        for k, (px, py) in enumerate(peers):
            for p in range(N_PIECES):
                a, n = src_rows[p]
                h = n // 2
                rcopy(pack_ref.at[pl.ds(pl.multiple_of(a + c * h, 32), h)],
                      wall_ref.at[pl.ds(pl.multiple_of(dst_row(p, s_me) + c * h, 32), h)], k,
                      (px, py, c)).start()
            cp = rcopy(wdw_ref, wdwall_ref.at[s_me], 6 + k, (px, py, c))
            cp.start()
            wdw_sends.append(cp)
        for p in range(N_PIECES):
            a, n = src_rows[p]
            for src, dst in ((pack_ref.at[pl.ds(a, n)], stage.at[pl.ds(0, n)]),
                             (stage.at[pl.ds(0, n)], wall_ref.at[pl.ds(dst_row(p, s_me), n)])):
                cp = pltpu.make_async_copy(src, dst, loc_sems.at[0])
                cp.start()
                cp.wait()
        for k, (px, py) in enumerate(peers):
            s_p = 2 * px + py
            half_bytes(k).wait_recv()
            for p in range(N_PIECES):
                n = src_rows[p][1]
                h = n // 2
                rows = wall_ref.at[pl.ds(pl.multiple_of(dst_row(p, s_p) + c * h, 32), h)]
                rcopy(rows, rows, 3 + k, (x, y, 1 - c)).start()
        for k in range(3):
            half_bytes(3 + k).wait_recv()
        for k in range(3):
            wdw_sends[k].wait_recv()
        for k in range(6):
            half_bytes(k).wait_send()
        for k in range(3):
            wdw_sends[k].wait_send()
        own_wdw.wait()

    any_spec = pl.BlockSpec(memory_space=pl.ANY)
    return pl.pallas_call(
        body, name="gather_weights",
        out_shape=(jax.ShapeDtypeStruct((WALL_ROWS, D), BF16),
                   jax.ShapeDtypeStruct((N_SHARDS, 32, PLE), F32)),
        in_specs=[any_spec, any_spec], out_specs=(any_spec, any_spec),
        scratch_shapes=[pltpu.VMEM((max(n for _, n in src_rows), D), BF16),
                        pltpu.SemaphoreType.DMA((9,)), pltpu.SemaphoreType.DMA((9,)),
                        pltpu.SemaphoreType.DMA((2,))],
    )(pack, wdw_shard)


def _exchange_halves(gpack):
    def body(g_ref, r1_ref, send_sem, recv_sem):
        x, y, c = _coords()
        cp = pltpu.make_async_remote_copy(
            src_ref=g_ref.at[:, pl.ds(pl.multiple_of((1 - c) * HALF_ROWS, 32), HALF_ROWS), :],
            dst_ref=r1_ref, send_sem=send_sem, recv_sem=recv_sem,
            device_id=(x, y, 1 - c), device_id_type=MESH)
        cp.start()
        cp.wait()

    any_spec = pl.BlockSpec(memory_space=pl.ANY)
    return pl.pallas_call(
        body, name="exchange_halves",
        out_shape=jax.ShapeDtypeStruct((N_SHARDS, HALF_ROWS, D), F32),
        in_specs=[any_spec], out_specs=any_spec,
        scratch_shapes=[pltpu.SemaphoreType.DMA, pltpu.SemaphoreType.DMA],
    )(gpack)


RT = 320


def _chip_sum(cidx, gpack, r1):
    def body(c_ref, g_ref, r_ref, o_ref):
        o_ref[...] = (g_ref[...] + r_ref[...]).astype(BF16)

    nt = HALF_ROWS // RT
    return pl.pallas_call(
        body, name="chip_sum",
        grid_spec=pltpu.PrefetchScalarGridSpec(
            num_scalar_prefetch=1, grid=(N_SHARDS, nt),
            in_specs=[pl.BlockSpec((1, RT, D), lambda s, t, c: (s, c[0] * nt + t, 0)),
                      pl.BlockSpec((1, RT, D), lambda s, t, c: (s, t, 0))],
            out_specs=pl.BlockSpec((1, RT, D), lambda s, t, c: (s, t, 0))),
        out_shape=jax.ShapeDtypeStruct((N_SHARDS, HALF_ROWS, D), BF16),
        compiler_params=_params(("arbitrary", "arbitrary")),
    )(cidx, gpack, r1)


def _send_chip_sums(cs):
    def body(cs_ref, r2_ref, send_sems, recv_sems):
        x, y, c = _coords()
        cps = []
        for k, (px, py) in enumerate(_chip_peers(x, y)):
            cp = pltpu.make_async_remote_copy(
                src_ref=cs_ref.at[2 * px + py], dst_ref=r2_ref.at[k],
                send_sem=send_sems.at[k], recv_sem=recv_sems.at[k],
                device_id=(px, py, c), device_id_type=MESH)
            cp.start()
            cps.append(cp)
        for cp in cps:
            cp.wait_recv()
        for cp in cps:
            cp.wait_send()

    any_spec = pl.BlockSpec(memory_space=pl.ANY)
    return pl.pallas_call(
        body, name="send_chip_sums",
        out_shape=jax.ShapeDtypeStruct((3, HALF_ROWS, D), BF16),
        in_specs=[any_spec], out_specs=any_spec,
        scratch_shapes=[pltpu.SemaphoreType.DMA((3,)), pltpu.SemaphoreType.DMA((3,))],
    )(cs)


def _final_half(sc, gpack, r1, r2):
    def body(sc_ref, g_ref, r_ref, p_ref, o_ref):
        acc = g_ref[0] + r_ref[0]
        for k in range(3):
            acc = acc + p_ref[k].astype(F32)
        o_ref[...] = acc

    nt = HALF_ROWS // RT
    return pl.pallas_call(
        body, name="final_half",
        grid_spec=pltpu.PrefetchScalarGridSpec(
            num_scalar_prefetch=1, grid=(nt,),
            in_specs=[pl.BlockSpec((1, RT, D), lambda t, sc: (sc[0], sc[1] * nt + t, 0)),
                      pl.BlockSpec((1, RT, D), lambda t, sc: (sc[0], t, 0)),
                      pl.BlockSpec((3, RT, D), lambda t, sc: (0, t, 0))],
            out_specs=pl.BlockSpec((RT, D), lambda t, sc: (sc[1] * nt + t, 0))),
        out_shape=jax.ShapeDtypeStruct((PACK_ROWS, D), F32),
        compiler_params=_params(("arbitrary",)),
    )(sc, gpack, r1, r2)


def _swap_halves(fh):
    def body(f_ref, o_ref, send_sem, recv_sem):
        x, y, c = _coords()
        rows = pl.ds(pl.multiple_of(c * HALF_ROWS, 32), HALF_ROWS)
        cp = pltpu.make_async_remote_copy(
            src_ref=f_ref.at[rows], dst_ref=o_ref.at[rows],
            send_sem=send_sem, recv_sem=recv_sem, device_id=(x, y, 1 - c), device_id_type=MESH)
        cp.start()
        cp.wait()

    any_spec = pl.BlockSpec(memory_space=pl.ANY)
    return pl.pallas_call(
        body, name="swap_halves",
        out_shape=jax.ShapeDtypeStruct((PACK_ROWS, D), F32),
        in_specs=[any_spec], out_specs=any_spec, input_output_aliases={0: 0},
        scratch_shapes=[pltpu.SemaphoreType.DMA, pltpu.SemaphoreType.DMA],
    )(fh)


VEC_ROWS = 40


def _all_reduce_small(vec):
    def body(v_ref, o_ref, buf, send_sems, recv_sems):
        x, y, c = _coords()
        me = 4 * x + 2 * y + c
        buf[me] = v_ref[...]
        cps = []
        for r in range(1, 8):
            dx, dy, dc = (r >> 2) & 1, (r >> 1) & 1, r & 1
            peer = (1 - x if dx else x, 1 - y if dy else y, 1 - c if dc else c)
            cp = pltpu.make_async_remote_copy(
                src_ref=v_ref, dst_ref=buf.at[me], send_sem=send_sems.at[r - 1],
                recv_sem=recv_sems.at[r - 1], device_id=peer, device_id_type=MESH)
            cp.start()
            cps.append(cp)
        for cp in cps:
            cp.wait_recv()
        for cp in cps:
            cp.wait_send()
        acc = buf[0]
        for d in range(1, 8):
            acc = acc + buf[d]
        o_ref[...] = acc

    vm = pl.BlockSpec(memory_space=pltpu.VMEM)
    return pl.pallas_call(
        body, name="all_reduce_small",
        out_shape=jax.ShapeDtypeStruct((VEC_ROWS, D), F32),
        in_specs=[vm], out_specs=vm,
        scratch_shapes=[pltpu.VMEM((8, VEC_ROWS, D), F32), pltpu.SemaphoreType.DMA((7,)),
                        pltpu.SemaphoreType.DMA((7,))],
    )(vec)


def _inproj(x, ln_pre, wall, tm):
    T = x.shape[0]

    def body(x_ref, g_ref, w_ref, z_ref, h_ref, hs):
        @pl.when(pl.program_id(1) == 0)
        def _():
            xv = x_ref[...]
            r = lax.rsqrt(jnp.mean(xv * xv, axis=-1, keepdims=True) + EPS)
            h = (xv * r * g_ref[...]).astype(BF16)
            hs[...] = h
            h_ref[...] = h

        z_ref[...] = _mm_nt(hs[...], w_ref[...]).astype(BF16)

    return pl.pallas_call(
        body, name="inproj", grid=(T // tm, 7),
        in_specs=[pl.BlockSpec((tm, D), lambda i, j: (i, 0)),
                  pl.BlockSpec((1, D), lambda i, j: (0, 0)),
                  pl.BlockSpec((D, D), lambda i, j: (5 + j, 0))],
        out_specs=(pl.BlockSpec((tm, D), lambda i, j: (i, j)),
                   pl.BlockSpec((tm, D), lambda i, j: (i, 0))),
        out_shape=(jax.ShapeDtypeStruct((T, 7 * D), BF16), jax.ShapeDtypeStruct((T, D), BF16)),
        scratch_shapes=[pltpu.VMEM((tm, D), BF16)],
        compiler_params=_params(("arbitrary", "arbitrary")),
    )(x, ln_pre, wall)


def _kvproj(h, wall, tm):
    T = h.shape[0]

    def body(h_ref, w_ref, o_ref):
        o_ref[...] = _mm_nt(h_ref[...], w_ref[...]).astype(BF16)

    return pl.pallas_call(
        body, name="kvproj", grid=(T // tm,),
        in_specs=[pl.BlockSpec((tm, D), lambda i: (i, 0)),
                  pl.BlockSpec((2 * BLOCK, D), lambda i: ((WT0 + ZKV) // (2 * BLOCK), 0))],
        out_specs=pl.BlockSpec((tm, 2 * BLOCK), lambda i: (i, 0)),
        out_shape=jax.ShapeDtypeStruct((T, 2 * BLOCK), BF16),
        compiler_params=_params(("arbitrary",)),
    )(h, wall)


HALO = 32
CONV_RC = 64
CONV_LC = 256


def _conv_taps(w_ref, src, r0, lane0, offset_of_tap):
    lanes = pl.ds(lane0, CONV_LC)
    out = None
    for b in range(8):
        taps = [k for k in range(CONV_K) if offset_of_tap(k) % 8 == b]
        if not taps:
            continue
        rows = CONV_RC + (8 if b else 0)
        vb = None
        for k in taps:
            term = w_ref[k:k + 1, lanes] * src[pl.ds(r0 + (offset_of_tap(k) - b), rows), lanes]
            vb = term if vb is None else vb + term
        vb = vb[b:b + CONV_RC] if b else vb
        out = vb if out is None else out + vb
    return out


def _conv_fwd(z, wdw, b_dw, ln_g, ln_b, wall, S, tm):
    T = z.shape[0]
    nt = S // tm
    hb = tm // HALO

    def body(cv_ref, cg_ref, cgate_ref, hcv_ref, hcg_ref, wdw_ref, bdw_ref, lng_ref, lnb_ref, wpw_ref,
             wbrc_ref, ya_ref, y_ref, rstd_ref, pw_ref, ubuf, cbuf):
        t = pl.program_id(1)
        ubuf[HALO:HALO + tm, :] = cv_ref[...].astype(F32) * _sig(cg_ref[...].astype(F32))
        hu = hcv_ref[...].astype(F32) * _sig(hcg_ref[...].astype(F32))
        ubuf[0:HALO, :] = jnp.where(t > 0, hu, 0.0)
        ubuf[HALO + tm:HALO + tm + 8, :] = jnp.zeros((8, D), F32)

        def chunk(ci, carry):
            r0 = pl.multiple_of(ci * CONV_RC, CONV_RC)
            for lg in range(D // CONV_LC):
                acc = _conv_taps(wdw_ref, ubuf, r0, lg * CONV_LC, lambda k: HALO - (CONV_K - 1) + k)
                cbuf[pl.ds(r0, CONV_RC), pl.ds(lg * CONV_LC, CONV_LC)] = acc
            return carry

        lax.fori_loop(0, tm // CONV_RC, chunk, 0)
        cc = cbuf[...] + bdw_ref[...]
        mu = jnp.mean(cc, axis=-1, keepdims=True)
        dd = cc - mu
        rstd = lax.rsqrt(jnp.mean(dd * dd, axis=-1, keepdims=True) + EPS)
        yn = dd * rstd
        y_ref[...] = yn.astype(BF16)
        rstd_ref[...] = rstd
        n = yn * lng_ref[...] + lnb_ref[...]
        s = n * _sig(n)
        pw = _mm(s.astype(BF16), wpw_ref[...])
        pw_ref[...] = pw.astype(BF16)
        gt = cgate_ref[...].astype(F32)
        ya_in = pw * (gt * _sig(gt))
        ya_ref[...] = _mm(ya_in.astype(BF16), wbrc_ref[...]).astype(BF16)

    def row(b, t):
        return b * nt + t

    def halo(b, t):
        return jnp.maximum(row(b, t) * hb - 1, 0)

    vec = pl.BlockSpec((1, D), lambda b, t: (0, 0))
    tile = lambda j: pl.BlockSpec((tm, D), lambda b, t: (row(b, t), j))
    out_tile = pl.BlockSpec((tm, D), lambda b, t: (row(b, t), 0))
    return pl.pallas_call(
        body, name="conv_fwd", grid=(T // S, nt),
        in_specs=[tile(ZB_CVAL), tile(ZB_CGLU), tile(ZB_CGATE),
                  pl.BlockSpec((HALO, D), lambda b, t: (halo(b, t), ZB_CVAL)),
                  pl.BlockSpec((HALO, D), lambda b, t: (halo(b, t), ZB_CGLU)),
                  pl.BlockSpec((32, D), lambda b, t: (0, 0)), vec, vec, vec,
                  pl.BlockSpec((D, D), lambda b, t: (0, 0)),
                  pl.BlockSpec((D, D), lambda b, t: (1, 0))],
        out_specs=(out_tile, out_tile, pl.BlockSpec((tm, 1), lambda b, t: (row(b, t), 0)), out_tile),
        out_shape=(jax.ShapeDtypeStruct((T, D), BF16), jax.ShapeDtypeStruct((T, D), BF16),
                   jax.ShapeDtypeStruct((T, 1), F32), jax.ShapeDtypeStruct((T, D), BF16)),
        scratch_shapes=[pltpu.VMEM((tm + HALO + 8, D), F32), pltpu.VMEM((tm, D), F32)],
        compiler_params=_params(("arbitrary", "arbitrary")),
    )(z, z, z, z, z, wdw, b_dw, ln_g, ln_b, wall, wall)


def _rope(tv, cos, sin):
    lane = lax.broadcasted_iota(jnp.int32, tv.shape, 1) & (HEAD_DIM - 1)
    swapped = jnp.where(lane < ROPE_DIM // 2, pltpu.roll(tv, 2 * HEAD_DIM - ROPE_DIM // 2, 1),
                        pltpu.roll(tv, ROPE_DIM // 2, 1))
    return tv * cos + swapped * sin


def _kv_variants(kv):
    lane = lax.broadcasted_iota(jnp.int32, kv.shape, 1)
    lo = lane < HEAD_DIM
    sw = pltpu.roll(kv, HEAD_DIM, 1)
    z = jnp.zeros_like(kv)
    g0 = (jnp.where(lo, kv, z).astype(BF16), jnp.where(lo, z, sw).astype(BF16))
    g1 = (jnp.where(lo, sw, z).astype(BF16), jnp.where(lo, z, kv).astype(BF16))
    return (g0, g1)


def _band_mask(nq):
    qi = lax.broadcasted_iota(jnp.int32, (nq * BLOCK, 2 * BLOCK), 0) & (BLOCK - 1)
    sj = lax.broadcasted_iota(jnp.int32, (nq * BLOCK, 2 * BLOCK), 1)
    return (sj <= qi + BLOCK) & (sj > qi), sj


def _sink_col(sink_ref, g, e):
    return jnp.concatenate(
        [jnp.full((BLOCK, 1), sink_ref[8 * g + 2 * j + e], F32) for j in range(4)], axis=0)


def _softmax_sink(s, valid, sk):
    s = jnp.where(valid, s, -1e30)
    m = jnp.maximum(jnp.max(s, axis=-1, keepdims=True), sk)
    p = jnp.exp(s - m)
    ps = jnp.exp(sk - m)
    inv = 1.0 / (jnp.sum(p, axis=-1, keepdims=True) + ps)
    return p * inv, ps * inv


def _attn_fwd(z, zkv, cos_t, sin_t, sinks, S, tq):
    T = z.shape[0]
    nt = S // tq
    nq = tq // BLOCK

    def body(sink_ref, q_ref, kv_ref, hkv_ref, cos_ref, sin_ref, hcos_ref, hsin_ref, o_ref):
        t = pl.program_id(1)
        cos = cos_ref[...]
        sin = sin_ref[...]
        kv = jnp.concatenate([hkv_ref[...], kv_ref[...]], axis=0).astype(F32)
        cos_k = jnp.concatenate([hcos_ref[...], cos], axis=0)
        sin_k = jnp.concatenate([hsin_ref[...], sin], axis=0)
        kx = _kv_variants(_rope(kv[:, :BLOCK], cos_k, sin_k))
        vx = _kv_variants(kv[:, BLOCK:])
        band, sj = _band_mask(4)
        qs = [(_rope(q_ref[:, 128 * hp:128 * hp + 128].astype(F32), cos, sin) * 0.125).astype(BF16)
              for hp in range(8)]
        for n in range(nq):
            first = (t == 0) & (n == 0)
            valid = band & (jnp.logical_not(first) | (sj >= BLOCK))
            r0 = n * BLOCK
            for g in range(2):
                lhs = jnp.concatenate([qs[4 * g + j][r0:r0 + BLOCK] for j in range(4)], axis=0)
                acc = jnp.zeros((4 * BLOCK, BLOCK), F32)
                for e in range(2):
                    s = _mm_nt(lhs, kx[g][e][r0:r0 + 2 * BLOCK])
                    p, _ = _softmax_sink(s, valid, _sink_col(sink_ref, g, e))
                    acc = acc + _mm(p.astype(BF16), vx[g][e][r0:r0 + 2 * BLOCK])
                for j in range(4):
                    o_ref[r0:r0 + BLOCK, 128 * (4 * g + j):128 * (4 * g + j) + 128] = (
                        acc[j * BLOCK:(j + 1) * BLOCK].astype(BF16))

    def row(b, t):
        return b * nt + t

    def halo(b, t):
        return jnp.maximum(row(b, t) * nq - 1, 0)

    return pl.pallas_call(
        body, name="attn_fwd", grid=(T // S, nt),
        in_specs=[pl.BlockSpec(memory_space=pltpu.SMEM),
                  pl.BlockSpec((tq, D), lambda b, t: (row(b, t), ZB_Q)),
                  pl.BlockSpec((tq, 2 * BLOCK), lambda b, t: (row(b, t), 0)),
                  pl.BlockSpec((BLOCK, 2 * BLOCK), lambda b, t: (halo(b, t), 0)),
                  pl.BlockSpec((tq, BLOCK), lambda b, t: (row(b, t), 0)),
                  pl.BlockSpec((tq, BLOCK), lambda b, t: (row(b, t), 0)),
                  pl.BlockSpec((BLOCK, BLOCK), lambda b, t: (halo(b, t), 0)),
                  pl.BlockSpec((BLOCK, BLOCK), lambda b, t: (halo(b, t), 0))],
        out_specs=pl.BlockSpec((tq, D), lambda b, t: (row(b, t), 0)),
        out_shape=jax.ShapeDtypeStruct((T, D), BF16),
        compiler_params=_params(("arbitrary", "arbitrary")),
    )(sinks, z, zkv, zkv, cos_t, sin_t, cos_t, sin_t)


def _tail_a(x, tgt, p, o, ya, z, ln_post, wall, wppt, tm):
    T = x.shape[0]
    last = T // tm - 1

    def body(x_ref, tgt_ref, p_ref, o_ref, ya_ref, ag_ref, gc_ref, ga_ref, lnp_ref, wbra_ref, wout_ref,
             wpg_ref, wppt_ref, loss_ref, dx1_ref, dm_ref, yb_ref, glnp_ref, gpack_ref, gwpp_ref,
             acc_out, acc_pg, sem):
        i = pl.program_id(0)

        @pl.when(i == 0)
        def _():
            acc_out[...] = jnp.zeros_like(acc_out)
            acc_pg[...] = jnp.zeros_like(acc_pg)
            gwpp_ref[...] = jnp.zeros_like(gwpp_ref)
            glnp_ref[...] = jnp.zeros_like(glnp_ref)
            loss_ref[...] = jnp.zeros_like(loss_ref)

        ag = ag_ref[...].astype(F32)
        yb_in = (o_ref[...].astype(F32) * (ag * _sig(ag))).astype(BF16)
        yb = _mm(yb_in, wbra_ref[...])
        yb_ref[...] = yb.astype(BF16)
        m = (_sig(gc_ref[...].astype(F32)) * ya_ref[...].astype(F32)
             + _sig(ga_ref[...].astype(F32)) * yb).astype(BF16)
        mo = _mm(m, wout_ref[...])
        r2 = lax.rsqrt(jnp.mean(mo * mo, axis=-1, keepdims=True) + EPS)
        nrm = mo * r2
        g_post = lnp_ref[...]
        x1 = x_ref[...] + nrm * g_post
        x1b = x1.astype(BF16)
        gate = _sig(_mm(x1b, wpg_ref[...]))
        pb = p_ref[...].astype(BF16)
        pp = _mm_nt(pb, wppt_ref[...])
        err = x1 + gate * pp - tgt_ref[...]
        loss_ref[...] += 0.5 * jnp.sum(jnp.sum(err * err, axis=-1, keepdims=True) * (1.0 / D),
                                       axis=0, keepdims=True)
        dx2 = err * (1.0 / D)
        dgp = (dx2 * pp * gate * (1.0 - gate)).astype(BF16)
        dpp = (dx2 * gate).astype(BF16)
        dx1 = dx2 + _mm_nt(dgp, wpg_ref[...])
        dx1_ref[...] = dx1
        acc_pg[...] += _mm_tn(x1b, dgp)
        gwpp_ref[...] += _mm_tn(dpp, pb)
        glnp_ref[...] += jnp.sum(dx1 * nrm, axis=0, keepdims=True)
        a = dx1 * g_post
        dmo = (r2 * (a - nrm * jnp.mean(a * nrm, axis=-1, keepdims=True))).astype(BF16)
        dm_ref[...] = _mm_nt(dmo, wout_ref[...]).astype(BF16)
        acc_out[...] += _mm_tn(m, dmo)

        @pl.when(i == last)
        def _():
            _flush_to_pack(acc_out, gpack_ref, 3 * D, sem.at[0])
            _flush_to_pack(acc_pg, gpack_ref, 4 * D, sem.at[1])

    tile = pl.BlockSpec((tm, D), lambda i: (i, 0))
    ztile = lambda j: pl.BlockSpec((tm, D), lambda i: (i, j))
    wsq = lambda k: pl.BlockSpec((D, D), lambda i: (k, 0))
    const = lambda shp: pl.BlockSpec(shp, lambda i: (0, 0))
    any_spec = pl.BlockSpec(memory_space=pl.ANY)
    return pl.pallas_call(
        body, name="tail_a", grid=(T // tm,),
        in_specs=[tile, tile, pl.BlockSpec((tm, PLE), lambda i: (i, 0)), tile, tile, ztile(ZB_AGATE),
                  ztile(ZB_GCONV), ztile(ZB_GATTN), const((1, D)), wsq(2), wsq(3), wsq(4), const((D, PLE))],
        out_specs=(const((1, 1)), tile, tile, tile, const((1, D)), any_spec, const((D, PLE))),
        out_shape=(jax.ShapeDtypeStruct((1, 1), F32), jax.ShapeDtypeStruct((T, D), F32),
                   jax.ShapeDtypeStruct((T, D), BF16), jax.ShapeDtypeStruct((T, D), BF16),
                   jax.ShapeDtypeStruct((1, D), F32), jax.ShapeDtypeStruct((N_SHARDS, PACK_ROWS, D), F32),
                   jax.ShapeDtypeStruct((D, PLE), F32)),
        scratch_shapes=[pltpu.VMEM((D, D), F32), pltpu.VMEM((D, D), F32), pltpu.SemaphoreType.DMA((2,))],
        compiler_params=_params(("arbitrary",)),
    )(x, tgt, p, o, ya, z, z, z, ln_post, wall, wall, wall, wppt)


def _dsilu(v, sg):
    return sg * (1.0 + v * (1.0 - sg))


def _tail_b(dm, ya, yb, o, z, pw, y, rstd, ln_g, ln_b, wall, gpack, tm):
    T = dm.shape[0]
    last = T // tm - 1

    def body(dm_ref, ya_ref, yb_ref, o_ref, ag_ref, gc_ref, ga_ref, cgate_ref, pw_ref, y_ref, rstd_ref,
             lng_ref, lnb_ref, wpw_ref, wbrc_ref, wbra_ref, gpack_in, dg_ref, do_ref, dc_ref, gvec_ref,
             gpack_ref, acc_bra, acc_brc, acc_pw, sem):
        i = pl.program_id(0)

        @pl.when(i == 0)
        def _():
            acc_bra[...] = jnp.zeros_like(acc_bra)
            acc_brc[...] = jnp.zeros_like(acc_brc)
            acc_pw[...] = jnp.zeros_like(acc_pw)
            gvec_ref[...] = jnp.zeros_like(gvec_ref)

        dm_v = dm_ref[...].astype(F32)
        sgc = _sig(gc_ref[...].astype(F32))
        sga = _sig(ga_ref[...].astype(F32))
        dya = (dm_v * sgc).astype(BF16)
        dyb = (dm_v * sga).astype(BF16)
        dg_ref[:, D:2 * D] = (dm_v * ya_ref[...].astype(F32) * sgc * (1.0 - sgc)).astype(BF16)
        dg_ref[:, 2 * D:3 * D] = (dm_v * yb_ref[...].astype(F32) * sga * (1.0 - sga)).astype(BF16)
        ag = ag_ref[...].astype(F32)
        sag = _sig(ag)
        sa = ag * sag
        ov = o_ref[...].astype(F32)
        dyb_in = _mm_nt(dyb, wbra_ref[...])
        acc_bra[...] += _mm_tn((ov * sa).astype(BF16), dyb)
        do_ref[...] = (dyb_in * sa).astype(BF16)
        dg_ref[:, 0:D] = (dyb_in * ov * _dsilu(ag, sag)).astype(BF16)
        gt = cgate_ref[...].astype(F32)
        sgt = _sig(gt)
        sgate = gt * sgt
        pw = pw_ref[...].astype(F32)
        dya_in = _mm_nt(dya, wbrc_ref[...])
        acc_brc[...] += _mm_tn((pw * sgate).astype(BF16), dya)
        dpw = (dya_in * sgate).astype(BF16)
        dg_ref[:, 3 * D:4 * D] = (dya_in * pw * _dsilu(gt, sgt)).astype(BF16)
        yn = y_ref[...].astype(F32)
        g = lng_ref[...]
        n = yn * g + lnb_ref[...]
        sn = _sig(n)
        acc_pw[...] += _mm_tn((n * sn).astype(BF16), dpw)
        dn = _mm_nt(dpw, wpw_ref[...]) * _dsilu(n, sn)
        gvec_ref[0:1, :] += jnp.sum(dn * yn, axis=0, keepdims=True)
        gvec_ref[1:2, :] += jnp.sum(dn, axis=0, keepdims=True)
        dy = dn * g
        dc = rstd_ref[...] * (dy - jnp.mean(dy, axis=-1, keepdims=True)
                              - yn * jnp.mean(dy * yn, axis=-1, keepdims=True))
        gvec_ref[2:3, :] += jnp.sum(dc, axis=0, keepdims=True)
        dc_ref[...] = dc.astype(BF16)

        @pl.when(i == last)
        def _():
            _flush_to_pack(acc_pw, gpack_ref, 0, sem.at[0])
            _flush_to_pack(acc_brc, gpack_ref, D, sem.at[1])
            _flush_to_pack(acc_bra, gpack_ref, 2 * D, sem.at[2])

    tile = pl.BlockSpec((tm, D), lambda i: (i, 0))
    ztile = lambda j: pl.BlockSpec((tm, D), lambda i: (i, j))
    wsq = lambda k: pl.BlockSpec((D, D), lambda i: (k, 0))
    const = lambda shp: pl.BlockSpec(shp, lambda i: (0, 0))
    any_spec = pl.BlockSpec(memory_space=pl.ANY)
    return pl.pallas_call(
        body, name="tail_b", grid=(T // tm,),
        in_specs=[tile, tile, tile, tile, ztile(ZB_AGATE), ztile(ZB_GCONV), ztile(ZB_GATTN), ztile(ZB_CGATE),
                  tile, tile, pl.BlockSpec((tm, 1), lambda i: (i, 0)), const((1, D)), const((1, D)), wsq(0),
                  wsq(1), wsq(2), any_spec],
        out_specs=(pl.BlockSpec((tm, 4 * D), lambda i: (i, 0)), tile, tile, const((8, D)), any_spec),
        out_shape=(jax.ShapeDtypeStruct((T, 7 * D), BF16), jax.ShapeDtypeStruct((T, D), BF16),
                   jax.ShapeDtypeStruct((T, D), BF16), jax.ShapeDtypeStruct((8, D), F32),
                   jax.ShapeDtypeStruct(gpack.shape, F32)),
        input_output_aliases={16: 4},
        scratch_shapes=[pltpu.VMEM((D, D), F32), pltpu.VMEM((D, D), F32), pltpu.VMEM((D, D), F32),
                        pltpu.SemaphoreType.DMA((3,))],
        compiler_params=_params(("arbitrary",)),
    )(dm, ya, yb, o, z, z, z, z, pw, y, rstd, ln_g, ln_b, wall, wall, wall, gpack)


def _conv_bwd(dc, z, wdw, dz, S, tm):
    T = dc.shape[0]
    nt = S // tm
    hb = tm // HALO
    nrows = T // HALO

    def body(dc_ref, hdc_ref, cv_ref, cg_ref, hcv_ref, hcg_ref, wdw_ref, dz_in, dz_ref, gw_ref, ubuf, dcbuf,
             dubuf, dwacc, shbuf):
        b = pl.program_id(0)
        t = pl.program_id(1)

        @pl.when((b == 0) & (t == 0))
        def _():
            dwacc[...] = jnp.zeros_like(dwacc)

        cv = cv_ref[...].astype(F32)
        sg = _sig(cg_ref[...].astype(F32))
        ubuf[HALO:HALO + tm, :] = cv * sg
        hu = hcv_ref[...].astype(F32) * _sig(hcg_ref[...].astype(F32))
        ubuf[0:HALO, :] = jnp.where(t > 0, hu, 0.0)
        ubuf[HALO + tm:HALO + tm + 8, :] = jnp.zeros((8, D), F32)
        dcbuf[0:tm, :] = dc_ref[...].astype(F32)
        dcbuf[tm:tm + HALO, :] = jnp.where(t < nt - 1, hdc_ref[...].astype(F32), 0.0)
        dcbuf[tm + HALO:tm + HALO + 8, :] = jnp.zeros((8, D), F32)

        def chunk(ci, carry):
            r0 = pl.multiple_of(ci * CONV_RC, CONV_RC)
            for lg in range(D // CONV_LC):
                l0 = lg * CONV_LC
                dubuf[pl.ds(r0, CONV_RC), pl.ds(l0, CONV_LC)] = _conv_taps(
                    wdw_ref, dcbuf, r0, l0, lambda k: CONV_K - 1 - k)
                dcc = dcbuf[pl.ds(r0, CONV_RC), pl.ds(l0, CONV_LC)]
                zero8 = jnp.zeros((8, CONV_LC), F32)
                dcz = jnp.concatenate([zero8, dcc, zero8], axis=0)
                for bb in range(8):
                    taps = [k for k in range(CONV_K) if (HALO - (CONV_K - 1) + k) % 8 == bb]
                    if not taps:
                        continue
                    rows = CONV_RC + (8 if bb else 0)
                    if bb:
                        shbuf[bb] = dcz[8 - bb:8 - bb + rows]
                    for k in taps:
                        a8 = HALO - (CONV_K - 1) + k - bb
                        dcs = shbuf[bb] if bb else dcc
                        prod = dcs * ubuf[pl.ds(r0 + a8, rows), pl.ds(l0, CONV_LC)]
                        part = prod[0:8]
                        for q in range(1, rows // 8):
                            part = part + prod[8 * q:8 * q + 8]
                        dwacc[8 * k:8 * k + 8, pl.ds(l0, CONV_LC)] += part
            return carry

        lax.fori_loop(0, tm // CONV_RC, chunk, 0)
        du = dubuf[...]
        dz_ref[:, 0:D] = (du * sg).astype(BF16)
        dz_ref[:, D:2 * D] = (du * cv * sg * (1.0 - sg)).astype(BF16)

        @pl.when((b == pl.num_programs(0) - 1) & (t == nt - 1))
        def _():
            for k in range(32):
                gw_ref[k:k + 1, :] = jnp.sum(dwacc[8 * k:8 * k + 8, :], axis=0, keepdims=True)

    def row(b, t):
        return b * nt + t

    def prev_halo(b, t):
        return jnp.maximum(row(b, t) * hb - 1, 0)

    def next_halo(b, t):
        return jnp.minimum((row(b, t) + 1) * hb, nrows - 1)

    return pl.pallas_call(
        body, name="conv_bwd", grid=(T // S, nt),
        in_specs=[pl.BlockSpec((tm, D), lambda b, t: (row(b, t), 0)),
                  pl.BlockSpec((HALO, D), lambda b, t: (next_halo(b, t), 0)),
                  pl.BlockSpec((tm, D), lambda b, t: (row(b, t), ZB_CVAL)),
                  pl.BlockSpec((tm, D), lambda b, t: (row(b, t), ZB_CGLU)),
                  pl.BlockSpec((HALO, D), lambda b, t: (prev_halo(b, t), ZB_CVAL)),
                  pl.BlockSpec((HALO, D), lambda b, t: (prev_halo(b, t), ZB_CGLU)),
                  pl.BlockSpec((32, D), lambda b, t: (0, 0)),
                  pl.BlockSpec(memory_space=pl.ANY)],
        out_specs=(pl.BlockSpec((tm, 2 * D), lambda b, t: (row(b, t), ZB_CVAL // 2)),
                   pl.BlockSpec((32, D), lambda b, t: (0, 0))),
        out_shape=(jax.ShapeDtypeStruct(dz.shape, BF16), jax.ShapeDtypeStruct((32, D), F32)),
        input_output_aliases={7: 0},
        scratch_shapes=[pltpu.VMEM((tm + HALO + 8, D), F32), pltpu.VMEM((tm + HALO + 8, D), F32),
                        pltpu.VMEM((tm, D), F32), pltpu.VMEM((8 * 32, D), F32),
                        pltpu.VMEM((8, CONV_RC + 8, CONV_LC), F32)],
        compiler_params=_params(("arbitrary", "arbitrary")),
    )(dc, dc, z, z, z, z, wdw, dz)


def _attn_bwd(z, zkv, o, do, cos_t, sin_t, sinks, dz, S, tq):
    T = z.shape[0]
    nt = S // tq
    nq = tq // BLOCK

    def body(sink_ref, q_ref, kv_ref, hkv_ref, o_ref, do_ref, cos_ref, sin_ref, hcos_ref, hsin_ref, dz_in,
             dq_ref, dkv_ref, gs_ref, carry, dkacc, dvacc):
        b = pl.program_id(0)
        tt = pl.program_id(1)
        t = nt - 1 - tt

        @pl.when((b == 0) & (tt == 0))
        def _():
            gs_ref[...] = jnp.zeros_like(gs_ref)

        @pl.when(tt == 0)
        def _():
            carry[...] = jnp.zeros_like(carry)

        cos = cos_ref[...]
        sin = sin_ref[...]
        kv = jnp.concatenate([hkv_ref[...], kv_ref[...]], axis=0).astype(F32)
        cos_k = jnp.concatenate([hcos_ref[...], cos], axis=0)
        sin_k = jnp.concatenate([hsin_ref[...], sin], axis=0)
        kx = _kv_variants(_rope(kv[:, :BLOCK], cos_k, sin_k))
        vx = _kv_variants(kv[:, BLOCK:])
        band, sj = _band_mask(4)
        lane = lax.broadcasted_iota(jnp.int32, (4 * BLOCK, BLOCK), 1)
        lo = lane < HEAD_DIM
        lo_k = lax.broadcasted_iota(jnp.int32, (2 * BLOCK, BLOCK), 1) < HEAD_DIM
        qs = [(_rope(q_ref[:, 128 * hp:128 * hp + 128].astype(F32), cos, sin) * 0.125).astype(BF16)
              for hp in range(8)]
        dkacc[...] = jnp.zeros_like(dkacc)
        dvacc[...] = jnp.zeros_like(dvacc)
        gsum = jnp.zeros((1, BLOCK), F32)
        hlane = lax.broadcasted_iota(jnp.int32, (1, BLOCK), 1)
        for n in range(nq):
            first = (t == 0) & (n == 0)
            valid = band & (jnp.logical_not(first) | (sj >= BLOCK))
            r0 = n * BLOCK
            for g in range(2):
                cols = [slice(128 * (4 * g + j), 128 * (4 * g + j) + 128) for j in range(4)]
                lhs = jnp.concatenate([qs[4 * g + j][r0:r0 + BLOCK] for j in range(4)], axis=0)
                dov = jnp.concatenate([do_ref[r0:r0 + BLOCK, cs] for cs in cols], axis=0)
                prod = dov.astype(F32) * jnp.concatenate(
                    [o_ref[r0:r0 + BLOCK, cs] for cs in cols], axis=0).astype(F32)
                dq = jnp.zeros((4 * BLOCK, BLOCK), F32)
                dk_e = []
                dv_e = []
                for e in range(2):
                    kw = kx[g][e][r0:r0 + 2 * BLOCK]
                    vw = vx[g][e][r0:r0 + 2 * BLOCK]
                    s = _mm_nt(lhs, kw)
                    p, psink = _softmax_sink(s, valid, _sink_col(sink_ref, g, e))
                    delta = jnp.sum(jnp.where(lo if e == 0 else jnp.logical_not(lo), prod, 0.0),
                                    axis=-1, keepdims=True)
                    ds = (p * (_mm_nt(dov, vw) - delta)).astype(BF16)
                    dq = dq + _mm(ds, kw)
                    dk_e.append(_mm_tn(ds, lhs))
                    dv_e.append(_mm_tn(p.astype(BF16), dov))
                    gs = -psink * delta
                    for j in range(4):
                        tot = jnp.sum(gs[j * BLOCK:(j + 1) * BLOCK], axis=0, keepdims=True)
                        gsum = gsum + jnp.where(hlane == 8 * g + 2 * j + e, tot, 0.0)
                for acc, parts in ((dkacc, dk_e), (dvacc, dv_e)):
                    if g == 0:
                        both = jnp.where(lo_k, parts[0] + pltpu.roll(parts[1], HEAD_DIM, 1), 0.0)
                    else:
                        both = jnp.where(lo_k, 0.0, parts[1] + pltpu.roll(parts[0], HEAD_DIM, 1))
                    acc[r0:r0 + 2 * BLOCK, :] += both
                for j in range(4):
                    dqj = _rope(dq[j * BLOCK:(j + 1) * BLOCK] * 0.125, cos[r0:r0 + BLOCK],
                                -sin[r0:r0 + BLOCK])
                    dq_ref[r0:r0 + BLOCK, cols[j]] = dqj.astype(BF16)
        gs_ref[0:1, :] += gsum
        dk_all = dkacc[...]
        dv_all = dvacc[...]
        last_rows = slice(tq, tq + BLOCK)
        dk_last = dk_all[last_rows] + carry[0:BLOCK, :]
        dv_last = dv_all[last_rows] + carry[BLOCK:2 * BLOCK, :]
        carry[0:BLOCK, :] = dk_all[0:BLOCK]
        carry[BLOCK:2 * BLOCK, :] = dv_all[0:BLOCK]
        if nq > 1:
            dk_t = jnp.concatenate([dk_all[BLOCK:tq], dk_last], axis=0)
            dv_t = jnp.concatenate([dv_all[BLOCK:tq], dv_last], axis=0)
        else:
            dk_t, dv_t = dk_last, dv_last
        dkv_ref[:, 0:BLOCK] = _rope(dk_t, cos, -sin).astype(BF16)
        dkv_ref[:, BLOCK:2 * BLOCK] = dv_t.astype(BF16)

    def row(b, tt):
        return b * nt + (nt - 1 - tt)

    def halo(b, tt):
        return jnp.maximum(row(b, tt) * nq - 1, 0)

    tile = pl.BlockSpec((tq, D), lambda b, tt: (row(b, tt), 0))
    return pl.pallas_call(
        body, name="attn_bwd", grid=(T // S, nt),
        in_specs=[pl.BlockSpec(memory_space=pltpu.SMEM),
                  pl.BlockSpec((tq, D), lambda b, tt: (row(b, tt), ZB_Q)),
                  pl.BlockSpec((tq, 2 * BLOCK), lambda b, tt: (row(b, tt), 0)),
                  pl.BlockSpec((BLOCK, 2 * BLOCK), lambda b, tt: (halo(b, tt), 0)),
                  tile, tile,
                  pl.BlockSpec((tq, BLOCK), lambda b, tt: (row(b, tt), 0)),
                  pl.BlockSpec((tq, BLOCK), lambda b, tt: (row(b, tt), 0)),
                  pl.BlockSpec((BLOCK, BLOCK), lambda b, tt: (halo(b, tt), 0)),
                  pl.BlockSpec((BLOCK, BLOCK), lambda b, tt: (halo(b, tt), 0)),
                  pl.BlockSpec(memory_space=pl.ANY)],
        out_specs=(pl.BlockSpec((tq, D), lambda b, tt: (row(b, tt), ZB_Q)),
                   pl.BlockSpec((tq, 2 * BLOCK), lambda b, tt: (row(b, tt), 0)),
                   pl.BlockSpec((8, BLOCK), lambda b, tt: (0, 0))),
        out_shape=(jax.ShapeDtypeStruct(dz.shape, BF16), jax.ShapeDtypeStruct((T, 2 * BLOCK), BF16),
                   jax.ShapeDtypeStruct((8, BLOCK), F32)),
        input_output_aliases={10: 0},
        scratch_shapes=[pltpu.VMEM((2 * BLOCK, BLOCK), F32), pltpu.VMEM((tq + BLOCK, BLOCK), F32),
                        pltpu.VMEM((tq + BLOCK, BLOCK), F32)],
        compiler_params=_params(("arbitrary", "arbitrary")),
    )(sinks, z, zkv, zkv, o, do, cos_t, sin_t, cos_t, sin_t, dz)


def _dh(dz, dz_kv, wall, x, dx1, ln_pre, tm):
    T = x.shape[0]

    def body(dz_ref, kv_ref, w_ref, wkv_ref, x_ref, dx1_ref, g_ref, gx_ref, glp_ref, acc):
        i = pl.program_id(0)
        k = pl.program_id(1)

        @pl.when((i == 0) & (k == 0))
        def _():
            glp_ref[...] = jnp.zeros_like(glp_ref)

        @pl.when(k == 0)
        def _():
            acc[...] = _mm(dz_ref[...], w_ref[...])

        @pl.when((k > 0) & (k < 7))
        def _():
            acc[...] += _mm(dz_ref[...], w_ref[...])

        @pl.when(k == 7)
        def _():
            dh = acc[...] + _mm(kv_ref[...], wkv_ref[...])
            xv = x_ref[...]
            r = lax.rsqrt(jnp.mean(xv * xv, axis=-1, keepdims=True) + EPS)
            xr = xv * r
            glp_ref[...] += jnp.sum(dh * xr, axis=0, keepdims=True)
            a = dh * g_ref[...]
            gx_ref[...] = dx1_ref[...] + r * (a - xr * jnp.mean(a * xr, axis=-1, keepdims=True))

    tile = pl.BlockSpec((tm, D), lambda i, k: (i, 0))
    return pl.pallas_call(
        body, name="dh", grid=(T // tm, 8),
        in_specs=[pl.BlockSpec((tm, D), lambda i, k: (i, jnp.minimum(k, 6))),
                  pl.BlockSpec((tm, 2 * BLOCK), lambda i, k: (i, 0)),
                  pl.BlockSpec((D, D), lambda i, k: (5 + jnp.minimum(k, 6), 0)),
                  pl.BlockSpec((2 * BLOCK, D), lambda i, k: ((WT0 + ZKV) // (2 * BLOCK), 0)),
                  tile, tile, pl.BlockSpec((1, D), lambda i, k: (0, 0))],
        out_specs=(tile, pl.BlockSpec((1, D), lambda i, k: (0, 0))),
        out_shape=(jax.ShapeDtypeStruct((T, D), F32), jax.ShapeDtypeStruct((1, D), F32)),
        scratch_shapes=[pltpu.VMEM((tm, D), F32)],
        compiler_params=_params(("arbitrary", "arbitrary")),
    )(dz, dz_kv, wall, wall, x, dx1, ln_pre)


def _gwt(dz, h, gpack, tt):
    T = dz.shape[0]
    last = T // tt - 1

    def body(dz_ref, h_ref, gpack_in, gpack_ref, acc, sem):
        j = pl.program_id(0)
        t = pl.program_id(1)

        @pl.when(t == 0)
        def _():
            acc[...] = _mm_tn(dz_ref[...], h_ref[...])

        @pl.when(t > 0)
        def _():
            acc[...] += _mm_tn(dz_ref[...], h_ref[...])

        for jj in range(7):
            @pl.when((t == last) & (j == jj))
            def _(jj=jj):
                _flush_to_pack(acc, gpack_ref, WT0 + jj * D, sem)

    any_spec = pl.BlockSpec(memory_space=pl.ANY)
    return pl.pallas_call(
        body, name="gwt", grid=(7, T // tt),
        in_specs=[pl.BlockSpec((tt, D), lambda j, t: (t, j)), pl.BlockSpec((tt, D), lambda j, t: (t, 0)),
                  any_spec],
        out_specs=any_spec, out_shape=jax.ShapeDtypeStruct(gpack.shape, F32), input_output_aliases={2: 0},
        scratch_shapes=[pltpu.VMEM((D, D), F32), pltpu.SemaphoreType.DMA],
        compiler_params=_params(("arbitrary", "arbitrary")),
    )(dz, h, gpack)


def _gwt_kv(dz_kv, h, gppt, gpack, tt):
    T = dz_kv.shape[0]
    last = T // tt - 1

    def body(dz_ref, h_ref, gppt_ref, gpack_in, gpack_ref, acc, sem):
        t = pl.program_id(0)

        @pl.when(t == 0)
        def _():
            acc[...] = _mm_tn(dz_ref[...], h_ref[...])

        @pl.when(t > 0)
        def _():
            acc[...] += _mm_tn(dz_ref[...], h_ref[...])

        @pl.when(t == last)
        def _():
            _flush_to_pack(acc, gpack_ref, WT0 + ZKV, sem)
            _flush_to_pack(gppt_ref, gpack_ref, WPP0, sem)

    any_spec = pl.BlockSpec(memory_space=pl.ANY)
    return pl.pallas_call(
        body, name="gwt_kv", grid=(T // tt,),
        in_specs=[pl.BlockSpec((tt, 2 * BLOCK), lambda t: (t, 0)), pl.BlockSpec((tt, D), lambda t: (t, 0)),
                  pl.BlockSpec((PLE, D), lambda t: (0, 0)), any_spec],
        out_specs=any_spec, out_shape=jax.ShapeDtypeStruct(gpack.shape, F32), input_output_aliases={3: 0},
        scratch_shapes=[pltpu.VMEM((2 * BLOCK, D), F32), pltpu.SemaphoreType.DMA],
        compiler_params=_params(("arbitrary",)),
    )(dz_kv, h, gppt, gpack)


_BC1 = 1.0 - ADAM_B1 ** ADAM_STEP
_BC2 = 1.0 - ADAM_B2 ** ADAM_STEP


def _adamw_math(w, g, m, v):
    m = ADAM_B1 * m + (1.0 - ADAM_B1) * g
    v = ADAM_B2 * v + (1.0 - ADAM_B2) * (g * g)
    delta = -ADAM_LR * ((m / _BC1) / (jnp.sqrt(v / _BC2) + ADAM_EPS) + ADAM_WD * w)
    return delta, m, v


def _adamw_rows(g, w, m, v, rows, name):
    R, C = w.shape

    def body(g_ref, w_ref, m_ref, v_ref, d_ref, nm_ref, nv_ref):
        d, nm, nv = _adamw_math(w_ref[...], g_ref[...], m_ref[...], v_ref[...])
        d_ref[...] = d
        nm_ref[...] = nm
        nv_ref[...] = nv

    spec = pl.BlockSpec((rows, C), lambda i: (i, 0))
    shp = jax.ShapeDtypeStruct((R, C), F32)
    return pl.pallas_call(
        body, name=name, grid=(R // rows,), in_specs=[spec] * 4, out_specs=(spec,) * 3,
        out_shape=(shp,) * 3, compiler_params=_params(("arbitrary",)),
    )(g, w, m, v)


def _adamw_square(gfin, ws, ms, vs):
    rb = 64
    nb = SQ_SHARD // rb

    def body(*refs):
        g_refs = refs[0:5]
        w_refs, m_refs, v_refs = refs[5:10], refs[10:15], refs[15:20]
        outs = refs[20:]
        for k in range(5):
            gk = g_refs[k][...]
            d, nm, nv = _adamw_math(w_refs[k][...], gk, m_refs[k][...], v_refs[k][...])
            outs[4 * k][...] = gk
            outs[4 * k + 1][...] = d
            outs[4 * k + 2][...] = nm
            outs[4 * k + 3][...] = nv

    spec = pl.BlockSpec((rb, D), lambda i: (i, 0))
    gspecs = [pl.BlockSpec((rb, D), lambda i, k=k: ((WIN_SHARD + SQ_SHARD * k) // rb + i, 0))
              for k in range(5)]
    shp = jax.ShapeDtypeStruct((SQ_SHARD, D), F32)
    res = pl.pallas_call(
        body, name="adamw_square", grid=(nb,), in_specs=gspecs + [spec] * 15, out_specs=(spec,) * 20,
        out_shape=(shp,) * 20, compiler_params=_params(("arbitrary",)),
    )(*([gfin] * 5), *ws, *ms, *vs)
    return [tuple(res[4 * k:4 * k + 4]) for k in range(5)]


def _adamw_small(gs, ws, ms, vs):
    n = len(gs)

    def body(*refs):
        outs = refs[4 * n:]
        for k in range(n):
            d, nm, nv = _adamw_math(refs[n + k][...], refs[k][...], refs[2 * n + k][...],
                                    refs[3 * n + k][...])
            outs[3 * k][...] = d
            outs[3 * k + 1][...] = nm
            outs[3 * k + 2][...] = nv

    vm = pl.BlockSpec(memory_space=pltpu.VMEM)
    shapes = []
    for w in ws:
        shapes += [jax.ShapeDtypeStruct(w.shape, F32)] * 3
    res = pl.pallas_call(
        body, name="adamw_small", in_specs=[vm] * (4 * n), out_specs=(vm,) * (3 * n),
        out_shape=tuple(shapes),
    )(*gs, *ws, *ms, *vs)
    return [tuple(res[3 * k:3 * k + 3]) for k in range(n)]


def _rope_tables(positions):
    inv = jnp.power(ROPE_THETA, -jnp.arange(0, ROPE_DIM, 2, dtype=F32) / ROPE_DIM)
    inv_h = jnp.concatenate([inv, inv, jnp.zeros((HEAD_DIM - ROPE_DIM,), F32)])
    sign_h = np.array([-1.0] * (ROPE_DIM // 2) + [1.0] * (ROPE_DIM // 2) + [0.0] * (HEAD_DIM - ROPE_DIM),
                      np.float32)
    ang = positions.astype(F32).reshape(-1, 1) * jnp.concatenate([inv_h, inv_h])[None, :]
    return jnp.cos(ang), jnp.sin(ang) * np.concatenate([sign_h, sign_h])[None, :]


def _f32_to_bf16_rows(a):
    return lax.bitcast_convert_type(a, BF16).reshape(a.shape[0], 2 * a.shape[1])


def kernel(x, p, positions, w_in, ln_pre, ln_post, w_dw, b_dw, conv_ln_g, conv_ln_b, w_pw, sinks, w_br_conv, w_br_attn, w_out, w_ple_gate, w_ple_proj, loss_target, m_w_in, m_ln_pre, m_ln_post, m_w_dw, m_b_dw, m_conv_ln_g, m_conv_ln_b, m_w_pw, m_sinks, m_w_br_conv, m_w_br_attn, m_w_out, m_w_ple_gate, m_w_ple_proj, v_w_in, v_ln_pre, v_ln_post, v_w_dw, v_b_dw, v_conv_ln_g, v_conv_ln_b, v_w_pw, v_sinks, v_w_br_conv, v_w_br_attn, v_w_out, v_w_ple_gate, v_w_ple_proj):
    nb, S, _ = x.shape
    T = nb * S
    xc = lax.axis_index("x")
    yc = lax.axis_index("y")
    cc = lax.axis_index("c")
    shard = 2 * xc + yc

    sq_w = (w_pw, w_br_conv, w_br_attn, w_out, w_ple_gate)
    pack = jnp.concatenate(
        [w_in[0].T] + [w[0] for w in sq_w] + [w_ple_proj[0].T.reshape(WPP_SHARD, D)], axis=0).astype(BF16)
    wdw_shard = jnp.pad(w_dw[0], ((0, 1), (0, 0)))
    wall, wdw_all = _gather_weights(pack, wdw_shard)
    wdw = jnp.concatenate([wdw_all[s] for s in range(N_SHARDS)], axis=1)
    wppt = wall[WPP0:WALL_ROWS].reshape(D, PLE)

    x2 = x.reshape(T, D)
    tgt = loss_target.reshape(T, D)
    p2 = p.reshape(T, PLE)
    cos_t, sin_t = _rope_tables(positions)
    sinks1 = sinks.reshape(N_HEADS)

    tm_big = min(TILE_PROJ, T)
    tm = min(TILE_TOKEN, S)
    tq = min(TILE_ATTN, S)

    z, h = _inproj(x2, ln_pre, wall, tm_big)
    zkv = _kvproj(h, wall, tm_big)
    ya, y, rstd, pw = _conv_fwd(z, wdw, b_dw, conv_ln_g, conv_ln_b, wall, S, tm)
    o = _attn_fwd(z, zkv, cos_t, sin_t, sinks1, S, tq)
    loss_p, dx1, dm, yb, g_ln_post, gpack, gw_ppt = _tail_a(x2, tgt, p2, o, ya, z, ln_post, wall, wppt, tm)

    dz, do, dc, gvec, gpack = _tail_b(dm, ya, yb, o, z, pw, y, rstd, conv_ln_g, conv_ln_b, wall, gpack, tm)
    dz, g_wdw = _conv_bwd(dc, z, wdw, dz, S, tm)
    dz, dkv, g_sinks = _attn_bwd(z, zkv, o, do, cos_t, sin_t, sinks1, dz, S, tq)
    gpack = _gwt(dz, h, gpack, tm_big)
    gpack = _gwt_kv(dkv, h, gw_ppt.reshape(PLE, D), gpack, tm_big)
    gx, g_ln_pre = _dh(dz, dkv, wall, x2, dx1, ln_pre, tm_big)

    cidx = jnp.reshape(cc, (1,)).astype(jnp.int32)
    scidx = jnp.stack([shard, cc]).astype(jnp.int32)
    r1 = _exchange_halves(gpack)
    cs = _chip_sum(cidx, gpack, r1)
    r2 = _send_chip_sums(cs)
    gfin = _swap_halves(_final_half(scidx, gpack, r1, r2))

    row37 = jnp.concatenate([g_sinks[0:1, 0:N_HEADS], loss_p, jnp.zeros((1, D - N_HEADS - 1), F32)], axis=1)
    vec = jnp.concatenate([g_wdw, g_ln_pre, g_ln_post, gvec[2:3], gvec[0:1], gvec[1:2], row37,
                           jnp.zeros((VEC_ROWS - 38, D), F32)], axis=0)
    tot = _all_reduce_small(vec)

    g_w_in = gfin[0:WIN_SHARD].T[None]
    d_w_in, nm_w_in, nv_w_in = _adamw_rows(g_w_in[0], w_in[0], m_w_in[0], v_w_in[0], 128, "adamw_w_in")
    sq_m = (m_w_pw, m_w_br_conv, m_w_br_attn, m_w_out, m_w_ple_gate)
    sq_v = (v_w_pw, v_w_br_conv, v_w_br_attn, v_w_out, v_w_ple_gate)
    sq_res = _adamw_square(gfin, [w[0] for w in sq_w], [m[0] for m in sq_m], [v[0] for v in sq_v])
    g_wpp = gfin[WIN_SHARD + 5 * SQ_SHARD:PACK_ROWS].reshape(PLE, PLE).T
    g_dw_all = tot[0:CONV_K]
    g_dw = lax.dynamic_slice_in_dim(g_dw_all, shard * PLE, PLE, axis=1)
    small_g = [g_wpp, g_dw, tot[32:33], tot[33:34], tot[34:35], tot[35:36], tot[36:37],
               tot[37:38, 0:N_HEADS]]
    small_w = [w_ple_proj[0], w_dw[0], ln_pre, ln_post, b_dw, conv_ln_g, conv_ln_b, sinks]
    small_m = [m_w_ple_proj[0], m_w_dw[0], m_ln_pre, m_ln_post, m_b_dw, m_conv_ln_g, m_conv_ln_b, m_sinks]
    small_v = [v_w_ple_proj[0], v_w_dw[0], v_ln_pre, v_ln_post, v_b_dw, v_conv_ln_g, v_conv_ln_b, v_sinks]
    small = _adamw_small(small_g, small_w, small_m, small_v)

    loss = tot[37, N_HEADS]
    grads = [g_w_in, small_g[2], small_g[3], g_dw[None], small_g[4], small_g[5], small_g[6],
             sq_res[0][0][None], small_g[7], sq_res[1][0][None], sq_res[2][0][None], sq_res[3][0][None],
             sq_res[4][0][None], g_wpp[None]]

    def triple(i):
        w_in_t = (d_w_in[None], nm_w_in[None], nv_w_in[None])
        sq = lambda k: tuple(a[None] for a in sq_res[k][1:4])
        sm = lambda k, lead: tuple(a[None] if lead else a for a in small[k])
        return [w_in_t[i], sm(2, False)[i], sm(3, False)[i], sm(1, True)[i], sm(4, False)[i],
                sm(5, False)[i], sm(6, False)[i], sq(0)[i], sm(7, False)[i], sq(1)[i], sq(2)[i], sq(3)[i],
                sq(4)[i], sm(0, True)[i]]

    return (loss, gx.reshape(nb, S, D), *grads, *triple(0), *triple(1), *triple(2))
```

```python
import functools

import jax
import jax.numpy as jnp
import numpy as np
from jax import lax
from jax.experimental import pallas as pl
from jax.experimental.pallas import tpu as pltpu

F32 = jnp.float32
BF16 = jnp.bfloat16

D = 1024
PLE = 256
N_HEADS = 16
HEAD_DIM = 64
BLOCK = 128
CONV_K = 31
ROPE_DIM = 16
ROPE_THETA = 500000.0
EPS = 1e-6
IN_WIDTH = 7424
N_SHARDS = 4

ADAM_LR = 0.001
ADAM_B1 = 0.9
ADAM_B2 = 0.999
ADAM_EPS = 1e-08
ADAM_WD = 0.01
ADAM_STEP = 10

SQ_NAMES = ("w_pw", "w_br_conv", "w_br_attn", "w_out", "w_ple_gate")
WT0 = 5 * D
WPP0 = WT0 + IN_WIDTH
WALL_ROWS = WPP0 + PLE
WIN_SHARD = IN_WIDTH // N_SHARDS
SQ_SHARD = D // N_SHARDS
WPP_SHARD = PLE * PLE // D
PACK_ROWS = WIN_SHARD + 5 * SQ_SHARD + WPP_SHARD
HALF_ROWS = PACK_ROWS // 2
VMEM_LIMIT = 56 * 1024 * 1024
MESH = pl.DeviceIdType.MESH
TILE_PROJ = 1024
TILE_TOKEN = 256
TILE_ATTN = 512


ZB_AGATE, ZB_GCONV, ZB_GATTN, ZB_CGATE, ZB_CVAL, ZB_CGLU, ZB_Q = range(7)
ZKV = 7 * D
_SEGMENTS = ((0, D, ZB_CVAL * D), (D, D, ZB_CGLU * D), (2 * D, D, ZB_CGATE * D), (3 * D, D, ZB_Q * D),
             (4 * D, 2 * BLOCK, ZKV), (4 * D + 2 * BLOCK, D, ZB_AGATE * D),
             (5 * D + 2 * BLOCK, D, ZB_GCONV * D), (6 * D + 2 * BLOCK, D, ZB_GATTN * D))
_WT_CUTS = (0, 192, 640, 1216, WIN_SHARD)


def _zp_row(o):
    for a, w, zp in _SEGMENTS:
        if a <= o < a + w:
            return zp + o - a
    raise ValueError(o)


def _pieces(s):
    out = []
    for a, b in zip(_WT_CUTS[:-1], _WT_CUTS[1:]):
        first = _zp_row(WIN_SHARD * s + a)
        assert _zp_row(WIN_SHARD * s + b - 1) == first + b - a - 1
        out.append((a, b - a, WT0 + first))
    for k in range(5):
        out.append((WIN_SHARD + SQ_SHARD * k, SQ_SHARD, D * k + SQ_SHARD * s))
    out.append((WIN_SHARD + 5 * SQ_SHARD, WPP_SHARD, WPP0 + WPP_SHARD * s))
    return out


N_PIECES = len(_pieces(0))


def _wall_segments(wall0, rows):
    out = []
    for s in range(N_SHARDS):
        for pr, n, wr in _pieces(s):
            lo, hi = max(wr, wall0), min(wr + n, wall0 + rows)
            if lo < hi:
                out.append((lo - wall0, hi - lo, s, pr + lo - wr))
    assert sum(n for _, n, _, _ in out) == rows
    return out


def _sel(s, vals):
    r = jnp.int32(vals[0])
    for i in range(1, len(vals)):
        r = jnp.where(s == i, jnp.int32(vals[i]), r)
    return r


def _sig(x):
    return 1.0 / (1.0 + jnp.exp(-x))


def _mm(a, b):
    return lax.dot_general(a, b, (((1,), (0,)), ((), ())), preferred_element_type=F32)


def _mm_nt(a, b):
    return lax.dot_general(a, b, (((1,), (1,)), ((), ())), preferred_element_type=F32)


def _mm_tn(a, b):
    return lax.dot_general(a, b, (((0,), (0,)), ((), ())), preferred_element_type=F32)


def _params(sem=None):
    return pltpu.CompilerParams(dimension_semantics=sem, vmem_limit_bytes=VMEM_LIMIT)


def _flush_to_pack(acc_ref, gpack_ref, wall0, sem):
    for r, n, s, pr in _wall_segments(wall0, acc_ref.shape[0]):
        cp = pltpu.make_async_copy(acc_ref.at[pl.ds(r, n)], gpack_ref.at[s, pl.ds(pr, n)], sem)
        cp.start()
        cp.wait()


def _coords():
    return lax.axis_index("x"), lax.axis_index("y"), lax.axis_index("c")


def _chip_peers(x, y):
    return [(1 - x, y), (x, 1 - y), (1 - x, 1 - y)]


def _gather_weights(pack, wdw_shard):
    tables = [[_pieces(s)[p][2] for s in range(N_SHARDS)] for p in range(N_PIECES)]
    src_rows = [(_pieces(0)[p][0], _pieces(0)[p][1]) for p in range(N_PIECES)]

    def body(pack_ref, wdw_ref, wall_ref, wdwall_ref, stage, send_sems, recv_sems, loc_sems):
        x, y, c = _coords()
        s_me = 2 * x + y
        peers = _chip_peers(x, y)

        def dst_row(p, s):
            return pl.multiple_of(_sel(s, tables[p]), 32)

        def rcopy(src, dst, k, dev):
            return pltpu.make_async_remote_copy(
                src_ref=src, dst_ref=dst, send_sem=send_sems.at[k], recv_sem=recv_sems.at[k],
                device_id=dev, device_id_type=MESH)

        def half_bytes(k):
            rows = wall_ref.at[pl.ds(0, HALF_ROWS)]
            return rcopy(rows, rows, k, (x, y, c))

        own_wdw = pltpu.make_async_copy(wdw_ref, wdwall_ref.at[s_me], loc_sems.at[1])
        own_wdw.start()
        wdw_sends = []
        for k, (px, py) in enumerate(peers):
            for p in range(N_PIECES):
                a, n = src_rows[p]
                h = n // 2
                rcopy(pack_ref.at[pl.ds(pl.multiple_of(a + c * h, 32), h)],
                      wall_ref.at[pl.ds(pl.multiple_of(dst_row(p, s_me) + c * h, 32), h)], k,
                      (px, py, c)).start()
            cp = rcopy(wdw_ref, wdwall_ref.at[s_me], 6 + k, (px, py, c))
            cp.start()
            wdw_sends.append(cp)
        for p in range(N_PIECES):
            a, n = src_rows[p]
            for src, dst in ((pack_ref.at[pl.ds(a, n)], stage.at[pl.ds(0, n)]),
                             (stage.at[pl.ds(0, n)], wall_ref.at[pl.ds(dst_row(p, s_me), n)])):
                cp = pltpu.make_async_copy(src, dst, loc_sems.at[0])
                cp.start()
                cp.wait()
        for k, (px, py) in enumerate(peers):
            s_p = 2 * px + py
            half_bytes(k).wait_recv()
            for p in range(N_PIECES):
                n = src_rows[p][1]
                h = n // 2
                rows = wall_ref.at[pl.ds(pl.multiple_of(dst_row(p, s_p) + c * h, 32), h)]
                rcopy(rows, rows, 3 + k, (x, y, 1 - c)).start()
        for k in range(3):
            half_bytes(3 + k).wait_recv()
        for k in range(3):
            wdw_sends[k].wait_recv()
        for k in range(6):
            half_bytes(k).wait_send()
        for k in range(3):
            wdw_sends[k].wait_send()
        own_wdw.wait()

    any_spec = pl.BlockSpec(memory_space=pl.ANY)
    return pl.pallas_call(
        body, name="gather_weights",
        out_shape=(jax.ShapeDtypeStruct((WALL_ROWS, D), BF16),
                   jax.ShapeDtypeStruct((N_SHARDS, 32, PLE), F32)),
        in_specs=[any_spec, any_spec], out_specs=(any_spec, any_spec),
        scratch_shapes=[pltpu.VMEM((max(n for _, n in src_rows), D), BF16),
                        pltpu.SemaphoreType.DMA((9,)), pltpu.SemaphoreType.DMA((9,)),
                        pltpu.SemaphoreType.DMA((2,))],
    )(pack, wdw_shard)


RT = 320


def _chip_sum(cidx, gpack, r1):
    def body(c_ref, g_ref, r_ref, o_ref):
        o_ref[...] = (g_ref[...] + r_ref[...]).astype(BF16)

    nt = HALF_ROWS // RT
    return pl.pallas_call(
        body, name="chip_sum",
        grid_spec=pltpu.PrefetchScalarGridSpec(
            num_scalar_prefetch=1, grid=(N_SHARDS, nt),
            in_specs=[pl.BlockSpec((1, RT, D), lambda s, t, c: (s, c[0] * nt + t, 0)),
                      pl.BlockSpec((1, RT, D), lambda s, t, c: (s, t, 0))],
            out_specs=pl.BlockSpec((1, RT, D), lambda s, t, c: (s, t, 0))),
        out_shape=jax.ShapeDtypeStruct((N_SHARDS, HALF_ROWS, D), BF16),
        compiler_params=_params(("arbitrary", "arbitrary")),
    )(cidx, gpack, r1)


def _final_half(sc, gpack, r1, r2):
    def body(sc_ref, g_ref, r_ref, p_ref, o_ref):
        acc = g_ref[0] + r_ref[0]
        for k in range(3):
            acc = acc + p_ref[k].astype(F32)
        o_ref[...] = acc

    nt = HALF_ROWS // RT
    return pl.pallas_call(
        body, name="final_half",
        grid_spec=pltpu.PrefetchScalarGridSpec(
            num_scalar_prefetch=1, grid=(nt,),
            in_specs=[pl.BlockSpec((1, RT, D), lambda t, sc: (sc[0], sc[1] * nt + t, 0)),
                      pl.BlockSpec((1, RT, D), lambda t, sc: (sc[0], t, 0)),
                      pl.BlockSpec((3, RT, D), lambda t, sc: (0, t, 0))],
            out_specs=pl.BlockSpec((RT, D), lambda t, sc: (sc[1] * nt + t, 0))),
        out_shape=jax.ShapeDtypeStruct((PACK_ROWS, D), F32),
        compiler_params=_params(("arbitrary",)),
    )(sc, gpack, r1, r2)


def _swap_halves(fh):
    def body(f_ref, o_ref, send_sem, recv_sem):
        x, y, c = _coords()
        rows = pl.ds(pl.multiple_of(c * HALF_ROWS, 32), HALF_ROWS)
        cp = pltpu.make_async_remote_copy(
            src_ref=f_ref.at[rows], dst_ref=o_ref.at[rows],
            send_sem=send_sem, recv_sem=recv_sem, device_id=(x, y, 1 - c), device_id_type=MESH)
        cp.start()
        cp.wait()

    any_spec = pl.BlockSpec(memory_space=pl.ANY)
    return pl.pallas_call(
        body, name="swap_halves",
        out_shape=jax.ShapeDtypeStruct((PACK_ROWS, D), F32),
        in_specs=[any_spec], out_specs=any_spec, input_output_aliases={0: 0},
        scratch_shapes=[pltpu.SemaphoreType.DMA, pltpu.SemaphoreType.DMA],
    )(fh)


VEC_ROWS = 40


def _all_reduce_small(vec):
    def body(v_ref, o_ref, buf, send_sems, recv_sems):
        x, y, c = _coords()
        me = 4 * x + 2 * y + c
        buf[me] = v_ref[...]
        cps = []
        for r in range(1, 8):
            dx, dy, dc = (r >> 2) & 1, (r >> 1) & 1, r & 1
            peer = (1 - x if dx else x, 1 - y if dy else y, 1 - c if dc else c)
            cp = pltpu.make_async_remote_copy(
                src_ref=v_ref, dst_ref=buf.at[me], send_sem=send_sems.at[r - 1],
                recv_sem=recv_sems.at[r - 1], device_id=peer, device_id_type=MESH)
            cp.start()
            cps.append(cp)
        for cp in cps:
            cp.wait_recv()
        for cp in cps:
            cp.wait_send()
        acc = buf[0]
        for d in range(1, 8):
            acc = acc + buf[d]
        o_ref[...] = acc

    vm = pl.BlockSpec(memory_space=pltpu.VMEM)
    return pl.pallas_call(
        body, name="all_reduce_small",
        out_shape=jax.ShapeDtypeStruct((VEC_ROWS, D), F32),
        in_specs=[vm], out_specs=vm,
        scratch_shapes=[pltpu.VMEM((8, VEC_ROWS, D), F32), pltpu.SemaphoreType.DMA((7,)),
                        pltpu.SemaphoreType.DMA((7,))],
    )(vec)


def _inproj(x, ln_pre, wall, tm):
    T = x.shape[0]

    def body(x_ref, g_ref, w_ref, z_ref, h_ref, hs):
        @pl.when(pl.program_id(1) == 0)
        def _():
            xv = x_ref[...]
            r = lax.rsqrt(jnp.mean(xv * xv, axis=-1, keepdims=True) + EPS)
            h = (xv * r * g_ref[...]).astype(BF16)
            hs[...] = h
            h_ref[...] = h

        z_ref[...] = _mm_nt(hs[...], w_ref[...]).astype(BF16)

    return pl.pallas_call(
        body, name="inproj", grid=(T // tm, 7),
        in_specs=[pl.BlockSpec((tm, D), lambda i, j: (i, 0)),
                  pl.BlockSpec((1, D), lambda i, j: (0, 0)),
                  pl.BlockSpec((D, D), lambda i, j: (5 + j, 0))],
        out_specs=(pl.BlockSpec((tm, D), lambda i, j: (i, j)),
                   pl.BlockSpec((tm, D), lambda i, j: (i, 0))),
        out_shape=(jax.ShapeDtypeStruct((T, 7 * D), BF16), jax.ShapeDtypeStruct((T, D), BF16)),
        scratch_shapes=[pltpu.VMEM((tm, D), BF16)],
        compiler_params=_params(("arbitrary", "arbitrary")),
    )(x, ln_pre, wall)


def _kvproj(h, wall, tm):
    T = h.shape[0]

    def body(h_ref, w_ref, o_ref):
        o_ref[...] = _mm_nt(h_ref[...], w_ref[...]).astype(BF16)

    return pl.pallas_call(
        body, name="kvproj", grid=(T // tm,),
        in_specs=[pl.BlockSpec((tm, D), lambda i: (i, 0)),
                  pl.BlockSpec((2 * BLOCK, D), lambda i: ((WT0 + ZKV) // (2 * BLOCK), 0))],
        out_specs=pl.BlockSpec((tm, 2 * BLOCK), lambda i: (i, 0)),
        out_shape=jax.ShapeDtypeStruct((T, 2 * BLOCK), BF16),
        compiler_params=_params(("arbitrary",)),
    )(h, wall)


HALO = 32
CONV_RC = 64
CONV_LC = 256


def _conv_taps(w_ref, src, r0, lane0, offset_of_tap):
    lanes = pl.ds(lane0, CONV_LC)
    out = None
    for b in range(8):
        taps = [k for k in range(CONV_K) if offset_of_tap(k) % 8 == b]
        if not taps:
            continue
        rows = CONV_RC + (8 if b else 0)
        vb = None
        for k in taps:
            term = w_ref[k:k + 1, lanes] * src[pl.ds(r0 + (offset_of_tap(k) - b), rows), lanes]
            vb = term if vb is None else vb + term
        vb = vb[b:b + CONV_RC] if b else vb
        out = vb if out is None else out + vb
    return out


def _conv_fwd(z, wdw, b_dw, ln_g, ln_b, wall, S, tm):
    T = z.shape[0]
    nt = S // tm
    hb = tm // HALO

    def body(cv_ref, cg_ref, cgate_ref, hcv_ref, hcg_ref, wdw_ref, bdw_ref, lng_ref, lnb_ref, wpw_ref,
             wbrc_ref, ya_ref, y_ref, rstd_ref, pw_ref, ubuf, cbuf):
        t = pl.program_id(1)
        ubuf[HALO:HALO + tm, :] = cv_ref[...].astype(F32) * _sig(cg_ref[...].astype(F32))
        hu = hcv_ref[...].astype(F32) * _sig(hcg_ref[...].astype(F32))
        ubuf[0:HALO, :] = jnp.where(t > 0, hu, 0.0)
        ubuf[HALO + tm:HALO + tm + 8, :] = jnp.zeros((8, D), F32)

        def chunk(ci, carry):
            r0 = pl.multiple_of(ci * CONV_RC, CONV_RC)
            for lg in range(D // CONV_LC):
                acc = _conv_taps(wdw_ref, ubuf, r0, lg * CONV_LC, lambda k: HALO - (CONV_K - 1) + k)
                cbuf[pl.ds(r0, CONV_RC), pl.ds(lg * CONV_LC, CONV_LC)] = acc
            return carry

        lax.fori_loop(0, tm // CONV_RC, chunk, 0)
        cc = cbuf[...] + bdw_ref[...]
        mu = jnp.mean(cc, axis=-1, keepdims=True)
        dd = cc - mu
        rstd = lax.rsqrt(jnp.mean(dd * dd, axis=-1, keepdims=True) + EPS)
        yn = dd * rstd
        y_ref[...] = yn.astype(BF16)
        rstd_ref[...] = rstd
        n = yn * lng_ref[...] + lnb_ref[...]
        s = n * _sig(n)
        pw = _mm(s.astype(BF16), wpw_ref[...])
        pw_ref[...] = pw.astype(BF16)
        gt = cgate_ref[...].astype(F32)
        ya_in = pw * (gt * _sig(gt))
        ya_ref[...] = _mm(ya_in.astype(BF16), wbrc_ref[...]).astype(BF16)

    def row(b, t):
        return b * nt + t

    def halo(b, t):
        return jnp.maximum(row(b, t) * hb - 1, 0)

    vec = pl.BlockSpec((1, D), lambda b, t: (0, 0))
    tile = lambda j: pl.BlockSpec((tm, D), lambda b, t: (row(b, t), j))
    out_tile = pl.BlockSpec((tm, D), lambda b, t: (row(b, t), 0))
    return pl.pallas_call(
        body, name="conv_fwd", grid=(T // S, nt),
        in_specs=[tile(ZB_CVAL), tile(ZB_CGLU), tile(ZB_CGATE),
                  pl.BlockSpec((HALO, D), lambda b, t: (halo(b, t), ZB_CVAL)),
                  pl.BlockSpec((HALO, D), lambda b, t: (halo(b, t), ZB_CGLU)),
                  pl.BlockSpec((32, D), lambda b, t: (0, 0)), vec, vec, vec,
                  pl.BlockSpec((D, D), lambda b, t: (0, 0)),
                  pl.BlockSpec((D, D), lambda b, t: (1, 0))],
        out_specs=(out_tile, out_tile, pl.BlockSpec((tm, 1), lambda b, t: (row(b, t), 0)), out_tile),
        out_shape=(jax.ShapeDtypeStruct((T, D), BF16), jax.ShapeDtypeStruct((T, D), BF16),
                   jax.ShapeDtypeStruct((T, 1), F32), jax.ShapeDtypeStruct((T, D), BF16)),
        scratch_shapes=[pltpu.VMEM((tm + HALO + 8, D), F32), pltpu.VMEM((tm, D), F32)],
        compiler_params=_params(("arbitrary", "arbitrary")),
    )(z, z, z, z, z, wdw, b_dw, ln_g, ln_b, wall, wall)


def _rope(tv, cos, sin):
    lane = lax.broadcasted_iota(jnp.int32, tv.shape, 1) & (HEAD_DIM - 1)
    swapped = jnp.where(lane < ROPE_DIM // 2, pltpu.roll(tv, 2 * HEAD_DIM - ROPE_DIM // 2, 1),
                        pltpu.roll(tv, ROPE_DIM // 2, 1))
    return tv * cos + swapped * sin


def _kv_variants(kv):
    lane = lax.broadcasted_iota(jnp.int32, kv.shape, 1)
    lo = lane < HEAD_DIM
    sw = pltpu.roll(kv, HEAD_DIM, 1)
    z = jnp.zeros_like(kv)
    g0 = (jnp.where(lo, kv, z).astype(BF16), jnp.where(lo, z, sw).astype(BF16))
    g1 = (jnp.where(lo, sw, z).astype(BF16), jnp.where(lo, z, kv).astype(BF16))
    return (g0, g1)


def _band_mask(nq):
    qi = lax.broadcasted_iota(jnp.int32, (nq * BLOCK, 2 * BLOCK), 0) & (BLOCK - 1)
    sj = lax.broadcasted_iota(jnp.int32, (nq * BLOCK, 2 * BLOCK), 1)
    return (sj <= qi + BLOCK) & (sj > qi), sj


def _sink_col(sink_ref, g, e):
    return jnp.concatenate(
        [jnp.full((BLOCK, 1), sink_ref[8 * g + 2 * j + e], F32) for j in range(4)], axis=0)


def _softmax_sink(s, valid, sk):
    s = jnp.where(valid, s, -1e30)
    m = jnp.maximum(jnp.max(s, axis=-1, keepdims=True), sk)
    p = jnp.exp(s - m)
    ps = jnp.exp(sk - m)
    inv = 1.0 / (jnp.sum(p, axis=-1, keepdims=True) + ps)
    return p * inv, ps * inv


def _attn_fwd(z, zkv, cos_t, sin_t, sinks, S, tq):
    T = z.shape[0]
    nt = S // tq
    nq = tq // BLOCK

    def body(sink_ref, q_ref, kv_ref, hkv_ref, cos_ref, sin_ref, hcos_ref, hsin_ref, o_ref):
        t = pl.program_id(1)
        cos = cos_ref[...]
        sin = sin_ref[...]
        kv = jnp.concatenate([hkv_ref[...], kv_ref[...]], axis=0).astype(F32)
        cos_k = jnp.concatenate([hcos_ref[...], cos], axis=0)
        sin_k = jnp.concatenate([hsin_ref[...], sin], axis=0)
        kx = _kv_variants(_rope(kv[:, :BLOCK], cos_k, sin_k))
        vx = _kv_variants(kv[:, BLOCK:])
        band, sj = _band_mask(4)
        qs = [(_rope(q_ref[:, 128 * hp:128 * hp + 128].astype(F32), cos, sin) * 0.125).astype(BF16)
              for hp in range(8)]
        for n in range(nq):
            first = (t == 0) & (n == 0)
            valid = band & (jnp.logical_not(first) | (sj >= BLOCK))
            r0 = n * BLOCK
            for g in range(2):
                lhs = jnp.concatenate([qs[4 * g + j][r0:r0 + BLOCK] for j in range(4)], axis=0)
                acc = jnp.zeros((4 * BLOCK, BLOCK), F32)
                for e in range(2):
                    s = _mm_nt(lhs, kx[g][e][r0:r0 + 2 * BLOCK])
                    p, _ = _softmax_sink(s, valid, _sink_col(sink_ref, g, e))
                    acc = acc + _mm(p.astype(BF16), vx[g][e][r0:r0 + 2 * BLOCK])
                for j in range(4):
                    o_ref[r0:r0 + BLOCK, 128 * (4 * g + j):128 * (4 * g + j) + 128] = (
                        acc[j * BLOCK:(j + 1) * BLOCK].astype(BF16))

    def row(b, t):
        return b * nt + t

    def halo(b, t):
        return jnp.maximum(row(b, t) * nq - 1, 0)

    return pl.pallas_call(
        body, name="attn_fwd", grid=(T // S, nt),
        in_specs=[pl.BlockSpec(memory_space=pltpu.SMEM),
                  pl.BlockSpec((tq, D), lambda b, t: (row(b, t), ZB_Q)),
                  pl.BlockSpec((tq, 2 * BLOCK), lambda b, t: (row(b, t), 0)),
                  pl.BlockSpec((BLOCK, 2 * BLOCK), lambda b, t: (halo(b, t), 0)),
                  pl.BlockSpec((tq, BLOCK), lambda b, t: (row(b, t), 0)),
                  pl.BlockSpec((tq, BLOCK), lambda b, t: (row(b, t), 0)),
                  pl.BlockSpec((BLOCK, BLOCK), lambda b, t: (halo(b, t), 0)),
                  pl.BlockSpec((BLOCK, BLOCK), lambda b, t: (halo(b, t), 0))],
        out_specs=pl.BlockSpec((tq, D), lambda b, t: (row(b, t), 0)),
        out_shape=jax.ShapeDtypeStruct((T, D), BF16),
        compiler_params=_params(("arbitrary", "arbitrary")),
    )(sinks, z, zkv, zkv, cos_t, sin_t, cos_t, sin_t)


def _tail_a(x, tgt, p, o, ya, z, ln_post, wall, wppt, tm):
    T = x.shape[0]
    last = T // tm - 1

    def body(x_ref, tgt_ref, p_ref, o_ref, ya_ref, ag_ref, gc_ref, ga_ref, lnp_ref, wbra_ref, wout_ref,
             wpg_ref, wppt_ref, loss_ref, dx1_ref, dm_ref, yb_ref, glnp_ref, gpack_ref, gwpp_ref,
             acc_out, acc_pg, sem):
        i = pl.program_id(0)

        @pl.when(i == 0)
        def _():
            acc_out[...] = jnp.zeros_like(acc_out)
            acc_pg[...] = jnp.zeros_like(acc_pg)
            gwpp_ref[...] = jnp.zeros_like(gwpp_ref)
            glnp_ref[...] = jnp.zeros_like(glnp_ref)
            loss_ref[...] = jnp.zeros_like(loss_ref)

        ag = ag_ref[...].astype(F32)
        yb_in = (o_ref[...].astype(F32) * (ag * _sig(ag))).astype(BF16)
        yb = _mm(yb_in, wbra_ref[...])
        yb_ref[...] = yb.astype(BF16)
        m = (_sig(gc_ref[...].astype(F32)) * ya_ref[...].astype(F32)
             + _sig(ga_ref[...].astype(F32)) * yb).astype(BF16)
        mo = _mm(m, wout_ref[...])
        r2 = lax.rsqrt(jnp.mean(mo * mo, axis=-1, keepdims=True) + EPS)
        nrm = mo * r2
        g_post = lnp_ref[...]
        x1 = x_ref[...] + nrm * g_post
        x1b = x1.astype(BF16)
        gate = _sig(_mm(x1b, wpg_ref[...]))
        pb = p_ref[...].astype(BF16)
        pp = _mm_nt(pb, wppt_ref[...])
        err = x1 + gate * pp - tgt_ref[...]
        loss_ref[...] += 0.5 * jnp.sum(jnp.sum(err * err, axis=-1, keepdims=True) * (1.0 / D),
                                       axis=0, keepdims=True)
        dx2 = err * (1.0 / D)
        dgp = (dx2 * pp * gate * (1.0 - gate)).astype(BF16)
        dpp = (dx2 * gate).astype(BF16)
        dx1 = dx2 + _mm_nt(dgp, wpg_ref[...])
        dx1_ref[...] = dx1
        acc_pg[...] += _mm_tn(x1b, dgp)
        gwpp_ref[...] += _mm_tn(dpp, pb)
        glnp_ref[...] += jnp.sum(dx1 * nrm, axis=0, keepdims=True)
        a = dx1 * g_post
        dmo = (r2 * (a - nrm * jnp.mean(a * nrm, axis=-1, keepdims=True))).astype(BF16)
        dm_ref[...] = _mm_nt(dmo, wout_ref[...]).astype(BF16)
        acc_out[...] += _mm_tn(m, dmo)

        @pl.when(i == last)
        def _():
            _flush_to_pack(acc_out, gpack_ref, 3 * D, sem.at[0])
            _flush_to_pack(acc_pg, gpack_ref, 4 * D, sem.at[1])

    tile = pl.BlockSpec((tm, D), lambda i: (i, 0))
    ztile = lambda j: pl.BlockSpec((tm, D), lambda i: (i, j))
    wsq = lambda k: pl.BlockSpec((D, D), lambda i: (k, 0))
    const = lambda shp: pl.BlockSpec(shp, lambda i: (0, 0))
    any_spec = pl.BlockSpec(memory_space=pl.ANY)
    return pl.pallas_call(
        body, name="tail_a", grid=(T // tm,),
        in_specs=[tile, tile, pl.BlockSpec((tm, PLE), lambda i: (i, 0)), tile, tile, ztile(ZB_AGATE),
                  ztile(ZB_GCONV), ztile(ZB_GATTN), const((1, D)), wsq(2), wsq(3), wsq(4), const((D, PLE))],
        out_specs=(const((1, 1)), tile, tile, tile, const((1, D)), any_spec, const((D, PLE))),
        out_shape=(jax.ShapeDtypeStruct((1, 1), F32), jax.ShapeDtypeStruct((T, D), F32),
                   jax.ShapeDtypeStruct((T, D), BF16), jax.ShapeDtypeStruct((T, D), BF16),
                   jax.ShapeDtypeStruct((1, D), F32), jax.ShapeDtypeStruct((N_SHARDS, PACK_ROWS, D), F32),
                   jax.ShapeDtypeStruct((D, PLE), F32)),
        scratch_shapes=[pltpu.VMEM((D, D), F32), pltpu.VMEM((D, D), F32), pltpu.SemaphoreType.DMA((2,))],
        compiler_params=_params(("arbitrary",)),
    )(x, tgt, p, o, ya, z, z, z, ln_post, wall, wall, wall, wppt)


def _dsilu(v, sg):
    return sg * (1.0 + v * (1.0 - sg))


def _tail_b(dm, ya, yb, o, z, pw, y, rstd, ln_g, ln_b, wall, gpack, tm):
    T = dm.shape[0]
    last = T // tm - 1

    def body(dm_ref, ya_ref, yb_ref, o_ref, ag_ref, gc_ref, ga_ref, cgate_ref, pw_ref, y_ref, rstd_ref,
             lng_ref, lnb_ref, wpw_ref, wbrc_ref, wbra_ref, gpack_in, dg_ref, do_ref, dc_ref, gvec_ref,
             gpack_ref, acc_bra, acc_brc, acc_pw, sem):
        i = pl.program_id(0)

        @pl.when(i == 0)
        def _():
            acc_bra[...] = jnp.zeros_like(acc_bra)
            acc_brc[...] = jnp.zeros_like(acc_brc)
            acc_pw[...] = jnp.zeros_like(acc_pw)
            gvec_ref[...] = jnp.zeros_like(gvec_ref)

        dm_v = dm_ref[...].astype(F32)
        sgc = _sig(gc_ref[...].astype(F32))
        sga = _sig(ga_ref[...].astype(F32))
        dya = (dm_v * sgc).astype(BF16)
        dyb = (dm_v * sga).astype(BF16)
        dg_ref[:, D:2 * D] = (dm_v * ya_ref[...].astype(F32) * sgc * (1.0 - sgc)).astype(BF16)
        dg_ref[:, 2 * D:3 * D] = (dm_v * yb_ref[...].astype(F32) * sga * (1.0 - sga)).astype(BF16)
        ag = ag_ref[...].astype(F32)
        sag = _sig(ag)
        sa = ag * sag
        ov = o_ref[...].astype(F32)
        dyb_in = _mm_nt(dyb, wbra_ref[...])
        acc_bra[...] += _mm_tn((ov * sa).astype(BF16), dyb)
        do_ref[...] = (dyb_in * sa).astype(BF16)
        dg_ref[:, 0:D] = (dyb_in * ov * _dsilu(ag, sag)).astype(BF16)
        gt = cgate_ref[...].astype(F32)
        sgt = _sig(gt)
        sgate = gt * sgt
        pw = pw_ref[...].astype(F32)
        dya_in = _mm_nt(dya, wbrc_ref[...])
        acc_brc[...] += _mm_tn((pw * sgate).astype(BF16), dya)
        dpw = (dya_in * sgate).astype(BF16)
        dg_ref[:, 3 * D:4 * D] = (dya_in * pw * _dsilu(gt, sgt)).astype(BF16)
        yn = y_ref[...].astype(F32)
        g = lng_ref[...]
        n = yn * g + lnb_ref[...]
        sn = _sig(n)
        acc_pw[...] += _mm_tn((n * sn).astype(BF16), dpw)
        dn = _mm_nt(dpw, wpw_ref[...]) * _dsilu(n, sn)
        gvec_ref[0:1, :] += jnp.sum(dn * yn, axis=0, keepdims=True)
        gvec_ref[1:2, :] += jnp.sum(dn, axis=0, keepdims=True)
        dy = dn * g
        dc = rstd_ref[...] * (dy - jnp.mean(dy, axis=-1, keepdims=True)
                              - yn * jnp.mean(dy * yn, axis=-1, keepdims=True))
        gvec_ref[2:3, :] += jnp.sum(dc, axis=0, keepdims=True)
        dc_ref[...] = dc.astype(BF16)

        @pl.when(i == last)
        def _():
            _flush_to_pack(acc_pw, gpack_ref, 0, sem.at[0])
            _flush_to_pack(acc_brc, gpack_ref, D, sem.at[1])
            _flush_to_pack(acc_bra, gpack_ref, 2 * D, sem.at[2])

    tile = pl.BlockSpec((tm, D), lambda i: (i, 0))
    ztile = lambda j: pl.BlockSpec((tm, D), lambda i: (i, j))
    wsq = lambda k: pl.BlockSpec((D, D), lambda i: (k, 0))
    const = lambda shp: pl.BlockSpec(shp, lambda i: (0, 0))
    any_spec = pl.BlockSpec(memory_space=pl.ANY)
    return pl.pallas_call(
        body, name="tail_b", grid=(T // tm,),
        in_specs=[tile, tile, tile, tile, ztile(ZB_AGATE), ztile(ZB_GCONV), ztile(ZB_GATTN), ztile(ZB_CGATE),
                  tile, tile, pl.BlockSpec((tm, 1), lambda i: (i, 0)), const((1, D)), const((1, D)), wsq(0),
                  wsq(1), wsq(2), any_spec],
        out_specs=(pl.BlockSpec((tm, 4 * D), lambda i: (i, 0)), tile, tile, const((8, D)), any_spec),
        out_shape=(jax.ShapeDtypeStruct((T, 7 * D), BF16), jax.ShapeDtypeStruct((T, D), BF16),
                   jax.ShapeDtypeStruct((T, D), BF16), jax.ShapeDtypeStruct((8, D), F32),
                   jax.ShapeDtypeStruct(gpack.shape, F32)),
        input_output_aliases={16: 4},
        scratch_shapes=[pltpu.VMEM((D, D), F32), pltpu.VMEM((D, D), F32), pltpu.VMEM((D, D), F32),
                        pltpu.SemaphoreType.DMA((3,))],
        compiler_params=_params(("arbitrary",)),
    )(dm, ya, yb, o, z, z, z, z, pw, y, rstd, ln_g, ln_b, wall, wall, wall, gpack)


def _conv_bwd(dc, z, wdw, dz, S, tm):
    T = dc.shape[0]
    nt = S // tm
    hb = tm // HALO
    nrows = T // HALO

    def body(dc_ref, hdc_ref, cv_ref, cg_ref, hcv_ref, hcg_ref, wdw_ref, dz_in, dz_ref, gw_ref, ubuf, dcbuf,
             dubuf, dwacc, shbuf):
        b = pl.program_id(0)
        t = pl.program_id(1)

        @pl.when((b == 0) & (t == 0))
        def _():
            dwacc[...] = jnp.zeros_like(dwacc)

        cv = cv_ref[...].astype(F32)
        sg = _sig(cg_ref[...].astype(F32))
        ubuf[HALO:HALO + tm, :] = cv * sg
        hu = hcv_ref[...].astype(F32) * _sig(hcg_ref[...].astype(F32))
        ubuf[0:HALO, :] = jnp.where(t > 0, hu, 0.0)
        ubuf[HALO + tm:HALO + tm + 8, :] = jnp.zeros((8, D), F32)
        dcbuf[0:tm, :] = dc_ref[...].astype(F32)
        dcbuf[tm:tm + HALO, :] = jnp.where(t < nt - 1, hdc_ref[...].astype(F32), 0.0)
        dcbuf[tm + HALO:tm + HALO + 8, :] = jnp.zeros((8, D), F32)

        def chunk(ci, carry):
            r0 = pl.multiple_of(ci * CONV_RC, CONV_RC)
            for lg in range(D // CONV_LC):
                l0 = lg * CONV_LC
                dubuf[pl.ds(r0, CONV_RC), pl.ds(l0, CONV_LC)] = _conv_taps(
                    wdw_ref, dcbuf, r0, l0, lambda k: CONV_K - 1 - k)
                dcc = dcbuf[pl.ds(r0, CONV_RC), pl.ds(l0, CONV_LC)]
                zero8 = jnp.zeros((8, CONV_LC), F32)
                dcz = jnp.concatenate([zero8, dcc, zero8], axis=0)
                for bb in range(8):
                    taps = [k for k in range(CONV_K) if (HALO - (CONV_K - 1) + k) % 8 == bb]
                    if not taps:
                        continue
                    rows = CONV_RC + (8 if bb else 0)
                    if bb:
                        shbuf[bb] = dcz[8 - bb:8 - bb + rows]
                    for k in taps:
                        a8 = HALO - (CONV_K - 1) + k - bb
                        dcs = shbuf[bb] if bb else dcc
                        prod = dcs * ubuf[pl.ds(r0 + a8, rows), pl.ds(l0, CONV_LC)]
                        part = prod[0:8]
                        for q in range(1, rows // 8):
                            part = part + prod[8 * q:8 * q + 8]
                        dwacc[8 * k:8 * k + 8, pl.ds(l0, CONV_LC)] += part
            return carry

        lax.fori_loop(0, tm // CONV_RC, chunk, 0)
        du = dubuf[...]
        dz_ref[:, 0:D] = (du * sg).astype(BF16)
        dz_ref[:, D:2 * D] = (du * cv * sg * (1.0 - sg)).astype(BF16)

        @pl.when((b == pl.num_programs(0) - 1) & (t == nt - 1))
        def _():
            for k in range(32):
                gw_ref[k:k + 1, :] = jnp.sum(dwacc[8 * k:8 * k + 8, :], axis=0, keepdims=True)

    def row(b, t):
        return b * nt + t

    def prev_halo(b, t):
        return jnp.maximum(row(b, t) * hb - 1, 0)

    def next_halo(b, t):
        return jnp.minimum((row(b, t) + 1) * hb, nrows - 1)

    return pl.pallas_call(
        body, name="conv_bwd", grid=(T // S, nt),
        in_specs=[pl.BlockSpec((tm, D), lambda b, t: (row(b, t), 0)),
                  pl.BlockSpec((HALO, D), lambda b, t: (next_halo(b, t), 0)),
                  pl.BlockSpec((tm, D), lambda b, t: (row(b, t), ZB_CVAL)),
                  pl.BlockSpec((tm, D), lambda b, t: (row(b, t), ZB_CGLU)),
                  pl.BlockSpec((HALO, D), lambda b, t: (prev_halo(b, t), ZB_CVAL)),
                  pl.BlockSpec((HALO, D), lambda b, t: (prev_halo(b, t), ZB_CGLU)),
                  pl.BlockSpec((32, D), lambda b, t: (0, 0)),
                  pl.BlockSpec(memory_space=pl.ANY)],
        out_specs=(pl.BlockSpec((tm, 2 * D), lambda b, t: (row(b, t), ZB_CVAL // 2)),
                   pl.BlockSpec((32, D), lambda b, t: (0, 0))),
        out_shape=(jax.ShapeDtypeStruct(dz.shape, BF16), jax.ShapeDtypeStruct((32, D), F32)),
        input_output_aliases={7: 0},
        scratch_shapes=[pltpu.VMEM((tm + HALO + 8, D), F32), pltpu.VMEM((tm + HALO + 8, D), F32),
                        pltpu.VMEM((tm, D), F32), pltpu.VMEM((8 * 32, D), F32),
                        pltpu.VMEM((8, CONV_RC + 8, CONV_LC), F32)],
        compiler_params=_params(("arbitrary", "arbitrary")),
    )(dc, dc, z, z, z, z, wdw, dz)


def _attn_bwd(z, zkv, o, do, cos_t, sin_t, sinks, dz, S, tq):
    T = z.shape[0]
    nt = S // tq
    nq = tq // BLOCK

    def body(sink_ref, q_ref, kv_ref, hkv_ref, o_ref, do_ref, cos_ref, sin_ref, hcos_ref, hsin_ref, dz_in,
             dq_ref, dkv_ref, gs_ref, carry, dkacc, dvacc):
        b = pl.program_id(0)
        tt = pl.program_id(1)
        t = nt - 1 - tt

        @pl.when((b == 0) & (tt == 0))
        def _():
            gs_ref[...] = jnp.zeros_like(gs_ref)

        @pl.when(tt == 0)
        def _():
            carry[...] = jnp.zeros_like(carry)

        cos = cos_ref[...]
        sin = sin_ref[...]
        kv = jnp.concatenate([hkv_ref[...], kv_ref[...]], axis=0).astype(F32)
        cos_k = jnp.concatenate([hcos_ref[...], cos], axis=0)
        sin_k = jnp.concatenate([hsin_ref[...], sin], axis=0)
        kx = _kv_variants(_rope(kv[:, :BLOCK], cos_k, sin_k))
        vx = _kv_variants(kv[:, BLOCK:])
        band, sj = _band_mask(4)
        lane = lax.broadcasted_iota(jnp.int32, (4 * BLOCK, BLOCK), 1)
        lo = lane < HEAD_DIM
        lo_k = lax.broadcasted_iota(jnp.int32, (2 * BLOCK, BLOCK), 1) < HEAD_DIM
        qs = [(_rope(q_ref[:, 128 * hp:128 * hp + 128].astype(F32), cos, sin) * 0.125).astype(BF16)
              for hp in range(8)]
        dkacc[...] = jnp.zeros_like(dkacc)
        dvacc[...] = jnp.zeros_like(dvacc)
        gsum = jnp.zeros((1, BLOCK), F32)
        hlane = lax.broadcasted_iota(jnp.int32, (1, BLOCK), 1)
        for n in range(nq):
            first = (t == 0) & (n == 0)
            valid = band & (jnp.logical_not(first) | (sj >= BLOCK))
            r0 = n * BLOCK
            for g in range(2):
                cols = [slice(128 * (4 * g + j), 128 * (4 * g + j) + 128) for j in range(4)]
                lhs = jnp.concatenate([qs[4 * g + j][r0:r0 + BLOCK] for j in range(4)], axis=0)
                dov = jnp.concatenate([do_ref[r0:r0 + BLOCK, cs] for cs in cols], axis=0)
                prod = dov.astype(F32) * jnp.concatenate(
                    [o_ref[r0:r0 + BLOCK, cs] for cs in cols], axis=0).astype(F32)
                dq = jnp.zeros((4 * BLOCK, BLOCK), F32)
                dk_e = []
                dv_e = []
                for e in range(2):
                    kw = kx[g][e][r0:r0 + 2 * BLOCK]
                    vw = vx[g][e][r0:r0 + 2 * BLOCK]
                    s = _mm_nt(lhs, kw)
                    p, psink = _softmax_sink(s, valid, _sink_col(sink_ref, g, e))
                    delta = jnp.sum(jnp.where(lo if e == 0 else jnp.logical_not(lo), prod, 0.0),
                                    axis=-1, keepdims=True)
                    ds = (p * (_mm_nt(dov, vw) - delta)).astype(BF16)
                    dq = dq + _mm(ds, kw)
                    dk_e.append(_mm_tn(ds, lhs))
                    dv_e.append(_mm_tn(p.astype(BF16), dov))
                    gs = -psink * delta
                    for j in range(4):
                        tot = jnp.sum(gs[j * BLOCK:(j + 1) * BLOCK], axis=0, keepdims=True)
                        gsum = gsum + jnp.where(hlane == 8 * g + 2 * j + e, tot, 0.0)
                for acc, parts in ((dkacc, dk_e), (dvacc, dv_e)):
                    if g == 0:
                        both = jnp.where(lo_k, parts[0] + pltpu.roll(parts[1], HEAD_DIM, 1), 0.0)
                    else:
                        both = jnp.where(lo_k, 0.0, parts[1] + pltpu.roll(parts[0], HEAD_DIM, 1))
                    acc[r0:r0 + 2 * BLOCK, :] += both
                for j in range(4):
                    dqj = _rope(dq[j * BLOCK:(j + 1) * BLOCK] * 0.125, cos[r0:r0 + BLOCK],
                                -sin[r0:r0 + BLOCK])
                    dq_ref[r0:r0 + BLOCK, cols[j]] = dqj.astype(BF16)
        gs_ref[0:1, :] += gsum
        dk_all = dkacc[...]
        dv_all = dvacc[...]
        last_rows = slice(tq, tq + BLOCK)
        dk_last = dk_all[last_rows] + carry[0:BLOCK, :]
        dv_last = dv_all[last_rows] + carry[BLOCK:2 * BLOCK, :]
        carry[0:BLOCK, :] = dk_all[0:BLOCK]
        carry[BLOCK:2 * BLOCK, :] = dv_all[0:BLOCK]
        if nq > 1:
            dk_t = jnp.concatenate([dk_all[BLOCK:tq], dk_last], axis=0)
            dv_t = jnp.concatenate([dv_all[BLOCK:tq], dv_last], axis=0)
        else:
            dk_t, dv_t = dk_last, dv_last
        dkv_ref[:, 0:BLOCK] = _rope(dk_t, cos, -sin).astype(BF16)
        dkv_ref[:, BLOCK:2 * BLOCK] = dv_t.astype(BF16)

    def row(b, tt):
        return b * nt + (nt - 1 - tt)

    def halo(b, tt):
        return jnp.maximum(row(b, tt) * nq - 1, 0)

    tile = pl.BlockSpec((tq, D), lambda b, tt: (row(b, tt), 0))
    return pl.pallas_call(
        body, name="attn_bwd", grid=(T // S, nt),
        in_specs=[pl.BlockSpec(memory_space=pltpu.SMEM),
                  pl.BlockSpec((tq, D), lambda b, tt: (row(b, tt), ZB_Q)),
                  pl.BlockSpec((tq, 2 * BLOCK), lambda b, tt: (row(b, tt), 0)),
                  pl.BlockSpec((BLOCK, 2 * BLOCK), lambda b, tt: (halo(b, tt), 0)),
                  tile, tile,
                  pl.BlockSpec((tq, BLOCK), lambda b, tt: (row(b, tt), 0)),
                  pl.BlockSpec((tq, BLOCK), lambda b, tt: (row(b, tt), 0)),
                  pl.BlockSpec((BLOCK, BLOCK), lambda b, tt: (halo(b, tt), 0)),
                  pl.BlockSpec((BLOCK, BLOCK), lambda b, tt: (halo(b, tt), 0)),
                  pl.BlockSpec(memory_space=pl.ANY)],
        out_specs=(pl.BlockSpec((tq, D), lambda b, tt: (row(b, tt), ZB_Q)),
                   pl.BlockSpec((tq, 2 * BLOCK), lambda b, tt: (row(b, tt), 0)),
                   pl.BlockSpec((8, BLOCK), lambda b, tt: (0, 0))),
        out_shape=(jax.ShapeDtypeStruct(dz.shape, BF16), jax.ShapeDtypeStruct((T, 2 * BLOCK), BF16),
                   jax.ShapeDtypeStruct((8, BLOCK), F32)),
        input_output_aliases={10: 0},
        scratch_shapes=[pltpu.VMEM((2 * BLOCK, BLOCK), F32), pltpu.VMEM((tq + BLOCK, BLOCK), F32),
                        pltpu.VMEM((tq + BLOCK, BLOCK), F32)],
        compiler_params=_params(("arbitrary", "arbitrary")),
    )(sinks, z, zkv, zkv, o, do, cos_t, sin_t, cos_t, sin_t, dz)


def _exchange_copies(g_ref, r1_ref, send_sems, recv_sems):
    x, y, c = _coords()
    return [pltpu.make_async_remote_copy(
        src_ref=g_ref.at[:, pl.ds(pl.multiple_of((1 - c) * HALF_ROWS, 32), HALF_ROWS), :], dst_ref=r1_ref,
        send_sem=send_sems.at[0], recv_sem=recv_sems.at[0], device_id=(x, y, 1 - c), device_id_type=MESH)]


def _chip_sum_copies(cs_ref, r2_ref, send_sems, recv_sems):
    x, y, c = _coords()
    return [pltpu.make_async_remote_copy(
        src_ref=cs_ref.at[2 * px + py], dst_ref=r2_ref.at[k], send_sem=send_sems.at[k],
        recv_sem=recv_sems.at[k], device_id=(px, py, c), device_id_type=MESH)
        for k, (px, py) in enumerate(_chip_peers(x, y))]


def _dh(dz, dz_kv, wall, x, dx1, ln_pre, tm, tile0, ntiles, gx_prev, name, copies, src, landing):
    T = x.shape[0]
    nsem = 3

    def body(*refs):
        dz_ref, kv_ref, w_ref, wkv_ref, x_ref, dx1_ref, g_ref, src_ref = refs[:8]
        gx_ref, glp_ref, land_ref, acc, send_sems, recv_sems = refs[-6:]
        i = pl.program_id(0)
        k = pl.program_id(1)

        @pl.when((i == 0) & (k == 0))
        def _():
            glp_ref[...] = jnp.zeros_like(glp_ref)
            for cp in copies(src_ref, land_ref, send_sems, recv_sems):
                cp.start()

        @pl.when(k == 0)
        def _():
            acc[...] = _mm(dz_ref[...], w_ref[...])

        @pl.when((k > 0) & (k < 7))
        def _():
            acc[...] += _mm(dz_ref[...], w_ref[...])

        @pl.when(k == 7)
        def _():
            dh = acc[...] + _mm(kv_ref[...], wkv_ref[...])
            xv = x_ref[...]
            r = lax.rsqrt(jnp.mean(xv * xv, axis=-1, keepdims=True) + EPS)
            xr = xv * r
            glp_ref[...] += jnp.sum(dh * xr, axis=0, keepdims=True)
            a = dh * g_ref[...]
            gx_ref[...] = dx1_ref[...] + r * (a - xr * jnp.mean(a * xr, axis=-1, keepdims=True))

        @pl.when((i == ntiles - 1) & (k == 7))
        def _():
            cps = copies(src_ref, land_ref, send_sems, recv_sems)
            for cp in cps:
                cp.wait_recv()
            for cp in cps:
                cp.wait_send()

    tile = pl.BlockSpec((tm, D), lambda i, k: (tile0 + i, 0))
    any_spec = pl.BlockSpec(memory_space=pl.ANY)
    operands = [dz, dz_kv, wall, wall, x, dx1, ln_pre, src] + ([] if gx_prev is None else [gx_prev])
    return pl.pallas_call(
        body, name=name, grid=(ntiles, 8),
        in_specs=[pl.BlockSpec((tm, D), lambda i, k: (tile0 + i, jnp.minimum(k, 6))),
                  pl.BlockSpec((tm, 2 * BLOCK), lambda i, k: (tile0 + i, 0)),
                  pl.BlockSpec((D, D), lambda i, k: (5 + jnp.minimum(k, 6), 0)),
                  pl.BlockSpec((2 * BLOCK, D), lambda i, k: ((WT0 + ZKV) // (2 * BLOCK), 0)),
                  tile, tile, pl.BlockSpec((1, D), lambda i, k: (0, 0)), any_spec]
        + ([] if gx_prev is None else [any_spec]),
        out_specs=(tile, pl.BlockSpec((1, D), lambda i, k: (0, 0)), any_spec),
        out_shape=(jax.ShapeDtypeStruct((T, D), F32), jax.ShapeDtypeStruct((1, D), F32), landing),
        input_output_aliases={} if gx_prev is None else {8: 0},
        scratch_shapes=[pltpu.VMEM((tm, D), F32), pltpu.SemaphoreType.DMA((nsem,)),
                        pltpu.SemaphoreType.DMA((nsem,))],
        compiler_params=_params(("arbitrary", "arbitrary")),
    )(*operands)


def _gwt(dz, h, gpack, tt):
    T = dz.shape[0]
    last = T // tt - 1

    def body(dz_ref, h_ref, gpack_in, gpack_ref, acc, sem):
        j = pl.program_id(0)
        t = pl.program_id(1)

        @pl.when(t == 0)
        def _():
            acc[...] = _mm_tn(dz_ref[...], h_ref[...])

        @pl.when(t > 0)
        def _():
            acc[...] += _mm_tn(dz_ref[...], h_ref[...])

        for jj in range(7):
            @pl.when((t == last) & (j == jj))
            def _(jj=jj):
                _flush_to_pack(acc, gpack_ref, WT0 + jj * D, sem)

    any_spec = pl.BlockSpec(memory_space=pl.ANY)
    return pl.pallas_call(
        body, name="gwt", grid=(7, T // tt),
        in_specs=[pl.BlockSpec((tt, D), lambda j, t: (t, j)), pl.BlockSpec((tt, D), lambda j, t: (t, 0)),
                  any_spec],
        out_specs=any_spec, out_shape=jax.ShapeDtypeStruct(gpack.shape, F32), input_output_aliases={2: 0},
        scratch_shapes=[pltpu.VMEM((D, D), F32), pltpu.SemaphoreType.DMA],
        compiler_params=_params(("arbitrary", "arbitrary")),
    )(dz, h, gpack)


def _gwt_kv(dz_kv, h, gppt, gpack, tt):
    T = dz_kv.shape[0]
    last = T // tt - 1

    def body(dz_ref, h_ref, gppt_ref, gpack_in, gpack_ref, acc, sem):
        t = pl.program_id(0)

        @pl.when(t == 0)
        def _():
            acc[...] = _mm_tn(dz_ref[...], h_ref[...])

        @pl.when(t > 0)
        def _():
            acc[...] += _mm_tn(dz_ref[...], h_ref[...])

        @pl.when(t == last)
        def _():
            _flush_to_pack(acc, gpack_ref, WT0 + ZKV, sem)
            _flush_to_pack(gppt_ref, gpack_ref, WPP0, sem)

    any_spec = pl.BlockSpec(memory_space=pl.ANY)
    return pl.pallas_call(
        body, name="gwt_kv", grid=(T // tt,),
        in_specs=[pl.BlockSpec((tt, 2 * BLOCK), lambda t: (t, 0)), pl.BlockSpec((tt, D), lambda t: (t, 0)),
                  pl.BlockSpec((PLE, D), lambda t: (0, 0)), any_spec],
        out_specs=any_spec, out_shape=jax.ShapeDtypeStruct(gpack.shape, F32), input_output_aliases={3: 0},
        scratch_shapes=[pltpu.VMEM((2 * BLOCK, D), F32), pltpu.SemaphoreType.DMA],
        compiler_params=_params(("arbitrary",)),
    )(dz_kv, h, gppt, gpack)


_BC1 = 1.0 - ADAM_B1 ** ADAM_STEP
_BC2 = 1.0 - ADAM_B2 ** ADAM_STEP


def _adamw_math(w, g, m, v):
    m = ADAM_B1 * m + (1.0 - ADAM_B1) * g
    v = ADAM_B2 * v + (1.0 - ADAM_B2) * (g * g)
    delta = -ADAM_LR * ((m / _BC1) / (jnp.sqrt(v / _BC2) + ADAM_EPS) + ADAM_WD * w)
    return delta, m, v


def _adamw_rows(g, w, m, v, rows, name):
    R, C = w.shape

    def body(g_ref, w_ref, m_ref, v_ref, d_ref, nm_ref, nv_ref):
        d, nm, nv = _adamw_math(w_ref[...], g_ref[...], m_ref[...], v_ref[...])
        d_ref[...] = d
        nm_ref[...] = nm
        nv_ref[...] = nv

    spec = pl.BlockSpec((rows, C), lambda i: (i, 0))
    shp = jax.ShapeDtypeStruct((R, C), F32)
    return pl.pallas_call(
        body, name=name, grid=(R // rows,), in_specs=[spec] * 4, out_specs=(spec,) * 3,
        out_shape=(shp,) * 3, compiler_params=_params(("arbitrary",)),
    )(g, w, m, v)


def _adamw_square(gfin, ws, ms, vs):
    rb = 64
    nb = SQ_SHARD // rb

    def body(*refs):
        g_refs = refs[0:5]
        w_refs, m_refs, v_refs = refs[5:10], refs[10:15], refs[15:20]
        outs = refs[20:]
        for k in range(5):
            gk = g_refs[k][...]
            d, nm, nv = _adamw_math(w_refs[k][...], gk, m_refs[k][...], v_refs[k][...])
            outs[4 * k][...] = gk
            outs[4 * k + 1][...] = d
            outs[4 * k + 2][...] = nm
            outs[4 * k + 3][...] = nv

    spec = pl.BlockSpec((rb, D), lambda i: (i, 0))
    gspecs = [pl.BlockSpec((rb, D), lambda i, k=k: ((WIN_SHARD + SQ_SHARD * k) // rb + i, 0))
              for k in range(5)]
    shp = jax.ShapeDtypeStruct((SQ_SHARD, D), F32)
    res = pl.pallas_call(
        body, name="adamw_square", grid=(nb,), in_specs=gspecs + [spec] * 15, out_specs=(spec,) * 20,
        out_shape=(shp,) * 20, compiler_params=_params(("arbitrary",)),
    )(*([gfin] * 5), *ws, *ms, *vs)
    return [tuple(res[4 * k:4 * k + 4]) for k in range(5)]


def _adamw_small(gs, ws, ms, vs):
    n = len(gs)

    def body(*refs):
        outs = refs[4 * n:]
        for k in range(n):
            d, nm, nv = _adamw_math(refs[n + k][...], refs[k][...], refs[2 * n + k][...],
                                    refs[3 * n + k][...])
            outs[3 * k][...] = d
            outs[3 * k + 1][...] = nm
            outs[3 * k + 2][...] = nv

    vm = pl.BlockSpec(memory_space=pltpu.VMEM)
    shapes = []
    for w in ws:
        shapes += [jax.ShapeDtypeStruct(w.shape, F32)] * 3
    res = pl.pallas_call(
        body, name="adamw_small", in_specs=[vm] * (4 * n), out_specs=(vm,) * (3 * n),
        out_shape=tuple(shapes),
    )(*gs, *ws, *ms, *vs)
    return [tuple(res[3 * k:3 * k + 3]) for k in range(n)]


def _rope_tables(positions):
    inv = jnp.power(ROPE_THETA, -jnp.arange(0, ROPE_DIM, 2, dtype=F32) / ROPE_DIM)
    inv_h = jnp.concatenate([inv, inv, jnp.zeros((HEAD_DIM - ROPE_DIM,), F32)])
    sign_h = np.array([-1.0] * (ROPE_DIM // 2) + [1.0] * (ROPE_DIM // 2) + [0.0] * (HEAD_DIM - ROPE_DIM),
                      np.float32)
    ang = positions.astype(F32).reshape(-1, 1) * jnp.concatenate([inv_h, inv_h])[None, :]
    return jnp.cos(ang), jnp.sin(ang) * np.concatenate([sign_h, sign_h])[None, :]


def _f32_to_bf16_rows(a):
    return lax.bitcast_convert_type(a, BF16).reshape(a.shape[0], 2 * a.shape[1])


def kernel(x, p, positions, w_in, ln_pre, ln_post, w_dw, b_dw, conv_ln_g, conv_ln_b, w_pw, sinks, w_br_conv, w_br_attn, w_out, w_ple_gate, w_ple_proj, loss_target, m_w_in, m_ln_pre, m_ln_post, m_w_dw, m_b_dw, m_conv_ln_g, m_conv_ln_b, m_w_pw, m_sinks, m_w_br_conv, m_w_br_attn, m_w_out, m_w_ple_gate, m_w_ple_proj, v_w_in, v_ln_pre, v_ln_post, v_w_dw, v_b_dw, v_conv_ln_g, v_conv_ln_b, v_w_pw, v_sinks, v_w_br_conv, v_w_br_attn, v_w_out, v_w_ple_gate, v_w_ple_proj):
    nb, S, _ = x.shape
    T = nb * S
    xc = lax.axis_index("x")
    yc = lax.axis_index("y")
    cc = lax.axis_index("c")
    shard = 2 * xc + yc

    sq_w = (w_pw, w_br_conv, w_br_attn, w_out, w_ple_gate)
    pack = jnp.concatenate(
        [w_in[0].T] + [w[0] for w in sq_w] + [w_ple_proj[0].T.reshape(WPP_SHARD, D)], axis=0).astype(BF16)
    wdw_shard = jnp.pad(w_dw[0], ((0, 1), (0, 0)))
    wall, wdw_all = _gather_weights(pack, wdw_shard)
    wdw = jnp.concatenate([wdw_all[s] for s in range(N_SHARDS)], axis=1)
    wppt = wall[WPP0:WALL_ROWS].reshape(D, PLE)

    x2 = x.reshape(T, D)
    tgt = loss_target.reshape(T, D)
    p2 = p.reshape(T, PLE)
    cos_t, sin_t = _rope_tables(positions)
    sinks1 = sinks.reshape(N_HEADS)

    tm_big = min(TILE_PROJ, T)
    tm = min(TILE_TOKEN, S)
    tq = min(TILE_ATTN, S)

    z, h = _inproj(x2, ln_pre, wall, tm_big)
    zkv = _kvproj(h, wall, tm_big)
    ya, y, rstd, pw = _conv_fwd(z, wdw, b_dw, conv_ln_g, conv_ln_b, wall, S, tm)
    o = _attn_fwd(z, zkv, cos_t, sin_t, sinks1, S, tq)
    loss_p, dx1, dm, yb, g_ln_post, gpack, gw_ppt = _tail_a(x2, tgt, p2, o, ya, z, ln_post, wall, wppt, tm)

    dz, do, dc, gvec, gpack = _tail_b(dm, ya, yb, o, z, pw, y, rstd, conv_ln_g, conv_ln_b, wall, gpack, tm)
    dz, g_wdw = _conv_bwd(dc, z, wdw, dz, S, tm)
    dz, dkv, g_sinks = _attn_bwd(z, zkv, o, do, cos_t, sin_t, sinks1, dz, S, tq)
    gpack = _gwt(dz, h, gpack, tm_big)
    gpack = _gwt_kv(dkv, h, gw_ppt.reshape(PLE, D), gpack, tm_big)

    cidx = jnp.reshape(cc, (1,)).astype(jnp.int32)
    scidx = jnp.stack([shard, cc]).astype(jnp.int32)
    tm_dh = min(tm_big, T // 2)
    n_dh = T // tm_dh
    n_a = max(1, n_dh // 4)
    gx, g_ln_pre_a, r1 = _dh(
        dz, dkv, wall, x2, dx1, ln_pre, tm_dh, 0, n_a, None, "dh_exchange", _exchange_copies, gpack,
        jax.ShapeDtypeStruct((N_SHARDS, HALF_ROWS, D), F32))
    cs = _chip_sum(cidx, gpack, r1)
    gx, g_ln_pre_b, r2 = _dh(
        dz, dkv, wall, x2, dx1, ln_pre, tm_dh, n_a, n_dh - n_a, gx, "dh_send", _chip_sum_copies, cs,
        jax.ShapeDtypeStruct((3, HALF_ROWS, D), BF16))
    g_ln_pre = g_ln_pre_a + g_ln_pre_b
    gfin = _swap_halves(_final_half(scidx, gpack, r1, r2))

    row37 = jnp.concatenate([g_sinks[0:1, 0:N_HEADS], loss_p, jnp.zeros((1, D - N_HEADS - 1), F32)], axis=1)
    vec = jnp.concatenate([g_wdw, g_ln_pre, g_ln_post, gvec[2:3], gvec[0:1], gvec[1:2], row37,
                           jnp.zeros((VEC_ROWS - 38, D), F32)], axis=0)
    tot = _all_reduce_small(vec)

    g_w_in = gfin[0:WIN_SHARD].T[None]
    d_w_in, nm_w_in, nv_w_in = _adamw_rows(g_w_in[0], w_in[0], m_w_in[0], v_w_in[0], 128, "adamw_w_in")
    sq_m = (m_w_pw, m_w_br_conv, m_w_br_attn, m_w_out, m_w_ple_gate)
    sq_v = (v_w_pw, v_w_br_conv, v_w_br_attn, v_w_out, v_w_ple_gate)
    sq_res = _adamw_square(gfin, [w[0] for w in sq_w], [m[0] for m in sq_m], [v[0] for v in sq_v])
    g_wpp = gfin[WIN_SHARD + 5 * SQ_SHARD:PACK_ROWS].reshape(PLE, PLE).T
    g_dw_all = tot[0:CONV_K]
    g_dw = lax.dynamic_slice_in_dim(g_dw_all, shard * PLE, PLE, axis=1)
    small_g = [g_wpp, g_dw, tot[32:33], tot[33:34], tot[34:35], tot[35:36], tot[36:37],
               tot[37:38, 0:N_HEADS]]
    small_w = [w_ple_proj[0], w_dw[0], ln_pre, ln_post, b_dw, conv_ln_g, conv_ln_b, sinks]
    small_m = [m_w_ple_proj[0], m_w_dw[0], m_ln_pre, m_ln_post, m_b_dw, m_conv_ln_g, m_conv_ln_b, m_sinks]
    small_v = [v_w_ple_proj[0], v_w_dw[0], v_ln_pre, v_ln_post, v_b_dw, v_conv_ln_g, v_conv_ln_b, v_sinks]
    small = _adamw_small(small_g, small_w, small_m, small_v)

    loss = tot[37, N_HEADS]
    grads = [g_w_in, small_g[2], small_g[3], g_dw[None], small_g[4], small_g[5], small_g[6],
             sq_res[0][0][None], small_g[7], sq_res[1][0][None], sq_res[2][0][None], sq_res[3][0][None],
             sq_res[4][0][None], g_wpp[None]]

    def triple(i):
        w_in_t = (d_w_in[None], nm_w_in[None], nv_w_in[None])
        sq = lambda k: tuple(a[None] for a in sq_res[k][1:4])
        sm = lambda k, lead: tuple(a[None] if lead else a for a in small[k])
        return [w_in_t[i], sm(2, False)[i], sm(3, False)[i], sm(1, True)[i], sm(4, False)[i],
                sm(5, False)[i], sm(6, False)[i], sq(0)[i], sm(7, False)[i], sq(1)[i], sq(2)[i], sq(3)[i],
                sq(4)[i], sm(0, True)[i]]

    return (loss, gx.reshape(nb, S, D), *grads, *triple(0), *triple(1), *triple(2))
```

```python
import functools

import jax
import jax.numpy as jnp
import numpy as np
from jax import lax
from jax.experimental import pallas as pl
from jax.experimental.pallas import tpu as pltpu

F32 = jnp.float32
BF16 = jnp.bfloat16

D = 1024
PLE = 256
N_HEADS = 16
HEAD_DIM = 64
BLOCK = 128
CONV_K = 31
ROPE_DIM = 16
ROPE_THETA = 500000.0
EPS = 1e-6
IN_WIDTH = 7424
N_SHARDS = 4

ADAM_LR = 0.001
ADAM_B1 = 0.9
ADAM_B2 = 0.999
ADAM_EPS = 1e-08
ADAM_WD = 0.01
ADAM_STEP = 10

SQ_NAMES = ("w_pw", "w_br_conv", "w_br_attn", "w_out", "w_ple_gate")
WT0 = 5 * D
WPP0 = WT0 + IN_WIDTH
WALL_ROWS = WPP0 + PLE
WIN_SHARD = IN_WIDTH // N_SHARDS
SQ_SHARD = D // N_SHARDS
WPP_SHARD = PLE * PLE // D
PACK_ROWS = WIN_SHARD + 5 * SQ_SHARD + WPP_SHARD
HALF_ROWS = PACK_ROWS // 2
VMEM_LIMIT = 56 * 1024 * 1024
MESH = pl.DeviceIdType.MESH
TILE_PROJ = 1024
TILE_TOKEN = 256
TILE_ATTN = 512
TAIL_PARTS = 1


ZB_AGATE, ZB_GCONV, ZB_GATTN, ZB_CGATE, ZB_CVAL, ZB_CGLU, ZB_Q = range(7)
ZKV = 7 * D
_SEGMENTS = ((0, D, ZB_CVAL * D), (D, D, ZB_CGLU * D), (2 * D, D, ZB_CGATE * D), (3 * D, D, ZB_Q * D),
             (4 * D, 2 * BLOCK, ZKV), (4 * D + 2 * BLOCK, D, ZB_AGATE * D),
             (5 * D + 2 * BLOCK, D, ZB_GCONV * D), (6 * D + 2 * BLOCK, D, ZB_GATTN * D))
_WT_CUTS = (0, 192, 640, 1216, WIN_SHARD)


def _zp_row(o):
    for a, w, zp in _SEGMENTS:
        if a <= o < a + w:
            return zp + o - a
    raise ValueError(o)


def _pieces(s):
    out = []
    for a, b in zip(_WT_CUTS[:-1], _WT_CUTS[1:]):
        first = _zp_row(WIN_SHARD * s + a)
        assert _zp_row(WIN_SHARD * s + b - 1) == first + b - a - 1
        out.append((a, b - a, WT0 + first))
    for k in range(5):
        out.append((WIN_SHARD + SQ_SHARD * k, SQ_SHARD, D * k + SQ_SHARD * s))
    out.append((WIN_SHARD + 5 * SQ_SHARD, WPP_SHARD, WPP0 + WPP_SHARD * s))
    return out


N_PIECES = len(_pieces(0))


def _wall_segments(wall0, rows):
    out = []
    for s in range(N_SHARDS):
        for pr, n, wr in _pieces(s):
            lo, hi = max(wr, wall0), min(wr + n, wall0 + rows)
            if lo < hi:
                out.append((lo - wall0, hi - lo, s, pr + lo - wr))
    assert sum(n for _, n, _, _ in out) == rows
    return out


def _sel(s, vals):
    r = jnp.int32(vals[0])
    for i in range(1, len(vals)):
        r = jnp.where(s == i, jnp.int32(vals[i]), r)
    return r


def _sig(x):
    return 1.0 / (1.0 + jnp.exp(-x))


def _mm(a, b):
    return lax.dot_general(a, b, (((1,), (0,)), ((), ())), preferred_element_type=F32)


def _mm_nt(a, b):
    return lax.dot_general(a, b, (((1,), (1,)), ((), ())), preferred_element_type=F32)


def _mm_tn(a, b):
    return lax.dot_general(a, b, (((0,), (0,)), ((), ())), preferred_element_type=F32)


def _params(sem=None):
    return pltpu.CompilerParams(dimension_semantics=sem, vmem_limit_bytes=VMEM_LIMIT)


def _flush_to_pack(acc_ref, gpack_ref, wall0, sem):
    for r, n, s, pr in _wall_segments(wall0, acc_ref.shape[0]):
        cp = pltpu.make_async_copy(acc_ref.at[pl.ds(r, n)], gpack_ref.at[s, pl.ds(pr, n)], sem)
        cp.start()
        cp.wait()


def _coords():
    return lax.axis_index("x"), lax.axis_index("y"), lax.axis_index("c")


def _chip_peers(x, y):
    return [(1 - x, y), (x, 1 - y), (1 - x, 1 - y)]


def _gather_weights(pack, wdw_shard):
    tables = [[_pieces(s)[p][2] for s in range(N_SHARDS)] for p in range(N_PIECES)]
    src_rows = [(_pieces(0)[p][0], _pieces(0)[p][1]) for p in range(N_PIECES)]

    def body(pack_ref, wdw_ref, wall_ref, wppf_ref, wdwall_ref, stage, send_sems, recv_sems, loc_sems):
        x, y, c = _coords()
        s_me = 2 * x + y
        peers = _chip_peers(x, y)

        def landing(p, s, off, n):
            if p == N_PIECES - 1:
                return wppf_ref.at[pl.ds(pl.multiple_of(WPP_SHARD * s + off, 32), n)]
            return wall_ref.at[pl.ds(pl.multiple_of(_sel(s, tables[p]) + off, 32), n)]

        def rcopy(src, dst, k, dev):
            return pltpu.make_async_remote_copy(
                src_ref=src, dst_ref=dst, send_sem=send_sems.at[k], recv_sem=recv_sems.at[k],
                device_id=dev, device_id_type=MESH)

        def half_bytes(k):
            rows = wall_ref.at[pl.ds(0, HALF_ROWS)]
            return rcopy(rows, rows, k, (x, y, c))

        own_wdw = pltpu.make_async_copy(wdw_ref, wdwall_ref.at[s_me], loc_sems.at[1])
        own_wdw.start()
        wdw_sends = []
        for k, (px, py) in enumerate(peers):
            for p in range(N_PIECES):
                a, n = src_rows[p]
                h = n // 2
                rcopy(pack_ref.at[pl.ds(pl.multiple_of(a + c * h, 32), h)], landing(p, s_me, c * h, h), k,
                      (px, py, c)).start()
            cp = rcopy(wdw_ref, wdwall_ref.at[s_me], 6 + k, (px, py, c))
            cp.start()
            wdw_sends.append(cp)
        for p in range(N_PIECES):
            a, n = src_rows[p]
            for src, dst in ((pack_ref.at[pl.ds(a, n)], stage.at[pl.ds(0, n)]),
                             (stage.at[pl.ds(0, n)], landing(p, s_me, 0, n))):
                cp = pltpu.make_async_copy(src, dst, loc_sems.at[0])
                cp.start()
                cp.wait()
        for k, (px, py) in enumerate(peers):
            s_p = 2 * px + py
            half_bytes(k).wait_recv()
            for p in range(N_PIECES):
                n = src_rows[p][1]
                h = n // 2
                rows = landing(p, s_p, c * h, h)
                rcopy(rows, rows, 3 + k, (x, y, 1 - c)).start()
        for k in range(3):
            half_bytes(3 + k).wait_recv()
        for k in range(3):
            wdw_sends[k].wait_recv()
        for k in range(6):
            half_bytes(k).wait_send()
        for k in range(3):
            wdw_sends[k].wait_send()
        own_wdw.wait()

    any_spec = pl.BlockSpec(memory_space=pl.ANY)
    return pl.pallas_call(
        body, name="gather_weights",
        out_shape=(jax.ShapeDtypeStruct((WPP0, D), BF16), jax.ShapeDtypeStruct((PLE, D), BF16),
                   jax.ShapeDtypeStruct((N_SHARDS, 32, PLE), F32)),
        in_specs=[any_spec, any_spec], out_specs=(any_spec, any_spec, any_spec),
        scratch_shapes=[pltpu.VMEM((max(n for _, n in src_rows), D), BF16),
                        pltpu.SemaphoreType.DMA((9,)), pltpu.SemaphoreType.DMA((9,)),
                        pltpu.SemaphoreType.DMA((2,))],
    )(pack, wdw_shard)


RT = 320


def _chip_sum(cidx, gpack, r1):
    def body(c_ref, g_ref, r_ref, o_ref):
        o_ref[...] = (g_ref[...] + r_ref[...]).astype(BF16)

    nt = HALF_ROWS // RT
    return pl.pallas_call(
        body, name="chip_sum",
        grid_spec=pltpu.PrefetchScalarGridSpec(
            num_scalar_prefetch=1, grid=(N_SHARDS, nt),
            in_specs=[pl.BlockSpec((1, RT, D), lambda s, t, c: (s, c[0] * nt + t, 0)),
                      pl.BlockSpec((1, RT, D), lambda s, t, c: (s, t, 0))],
            out_specs=pl.BlockSpec((1, RT, D), lambda s, t, c: (s, t, 0))),
        out_shape=jax.ShapeDtypeStruct((N_SHARDS, HALF_ROWS, D), BF16),
        compiler_params=_params(("arbitrary", "arbitrary")),
    )(cidx, gpack, r1)


def _final_half(sc, gpack, r1, r2):
    def body(sc_ref, g_ref, r_ref, p_ref, o_ref):
        acc = g_ref[0] + r_ref[0]
        for k in range(3):
            acc = acc + p_ref[k].astype(F32)
        o_ref[...] = acc

    nt = HALF_ROWS // RT
    return pl.pallas_call(
        body, name="final_half",
        grid_spec=pltpu.PrefetchScalarGridSpec(
            num_scalar_prefetch=1, grid=(nt,),
            in_specs=[pl.BlockSpec((1, RT, D), lambda t, sc: (sc[0], sc[1] * nt + t, 0)),
                      pl.BlockSpec((1, RT, D), lambda t, sc: (sc[0], t, 0)),
                      pl.BlockSpec((3, RT, D), lambda t, sc: (0, t, 0))],
            out_specs=pl.BlockSpec((RT, D), lambda t, sc: (sc[1] * nt + t, 0))),
        out_shape=jax.ShapeDtypeStruct((PACK_ROWS, D), F32),
        compiler_params=_params(("arbitrary",)),
    )(sc, gpack, r1, r2)


def _swap_halves(fh):
    def body(f_ref, o_ref, send_sem, recv_sem):
        x, y, c = _coords()
        rows = pl.ds(pl.multiple_of(c * HALF_ROWS, 32), HALF_ROWS)
        cp = pltpu.make_async_remote_copy(
            src_ref=f_ref.at[rows], dst_ref=o_ref.at[rows],
            send_sem=send_sem, recv_sem=recv_sem, device_id=(x, y, 1 - c), device_id_type=MESH)
        cp.start()
        cp.wait()

    any_spec = pl.BlockSpec(memory_space=pl.ANY)
    return pl.pallas_call(
        body, name="swap_halves",
        out_shape=jax.ShapeDtypeStruct((PACK_ROWS, D), F32),
        in_specs=[any_spec], out_specs=any_spec, input_output_aliases={0: 0},
        scratch_shapes=[pltpu.SemaphoreType.DMA, pltpu.SemaphoreType.DMA],
    )(fh)


VEC_ROWS = 40


def _all_reduce_small(vec):
    def body(v_ref, o_ref, buf, send_sems, recv_sems):
        x, y, c = _coords()
        me = 4 * x + 2 * y + c
        buf[me] = v_ref[...]
        cps = []
        for r in range(1, 8):
            dx, dy, dc = (r >> 2) & 1, (r >> 1) & 1, r & 1
            peer = (1 - x if dx else x, 1 - y if dy else y, 1 - c if dc else c)
            cp = pltpu.make_async_remote_copy(
                src_ref=v_ref, dst_ref=buf.at[me], send_sem=send_sems.at[r - 1],
                recv_sem=recv_sems.at[r - 1], device_id=peer, device_id_type=MESH)
            cp.start()
            cps.append(cp)
        for cp in cps:
            cp.wait_recv()
        for cp in cps:
            cp.wait_send()
        acc = buf[0]
        for d in range(1, 8):
            acc = acc + buf[d]
        o_ref[...] = acc

    vm = pl.BlockSpec(memory_space=pltpu.VMEM)
    return pl.pallas_call(
        body, name="all_reduce_small",
        out_shape=jax.ShapeDtypeStruct((VEC_ROWS, D), F32),
        in_specs=[vm], out_specs=vm,
        scratch_shapes=[pltpu.VMEM((8, VEC_ROWS, D), F32), pltpu.SemaphoreType.DMA((7,)),
                        pltpu.SemaphoreType.DMA((7,))],
    )(vec)


def _inproj(x, ln_pre, wall, tm):
    T = x.shape[0]

    def body(x_ref, g_ref, w_ref, z_ref, h_ref, hs):
        @pl.when(pl.program_id(1) == 0)
        def _():
            xv = x_ref[...]
            r = lax.rsqrt(jnp.mean(xv * xv, axis=-1, keepdims=True) + EPS)
            h = (xv * r * g_ref[...]).astype(BF16)
            hs[...] = h
            h_ref[...] = h

        z_ref[...] = _mm_nt(hs[...], w_ref[...]).astype(BF16)

    return pl.pallas_call(
        body, name="inproj", grid=(T // tm, 7),
        in_specs=[pl.BlockSpec((tm, D), lambda i, j: (i, 0)),
                  pl.BlockSpec((1, D), lambda i, j: (0, 0)),
                  pl.BlockSpec((D, D), lambda i, j: (5 + j, 0))],
        out_specs=(pl.BlockSpec((tm, D), lambda i, j: (i, j)),
                   pl.BlockSpec((tm, D), lambda i, j: (i, 0))),
        out_shape=(jax.ShapeDtypeStruct((T, 7 * D), BF16), jax.ShapeDtypeStruct((T, D), BF16)),
        scratch_shapes=[pltpu.VMEM((tm, D), BF16)],
        compiler_params=_params(("arbitrary", "arbitrary")),
    )(x, ln_pre, wall)


def _kvproj(h, wall, tm):
    T = h.shape[0]

    def body(h_ref, w_ref, o_ref):
        o_ref[...] = _mm_nt(h_ref[...], w_ref[...]).astype(BF16)

    return pl.pallas_call(
        body, name="kvproj", grid=(T // tm,),
        in_specs=[pl.BlockSpec((tm, D), lambda i: (i, 0)),
                  pl.BlockSpec((2 * BLOCK, D), lambda i: ((WT0 + ZKV) // (2 * BLOCK), 0))],
        out_specs=pl.BlockSpec((tm, 2 * BLOCK), lambda i: (i, 0)),
        out_shape=jax.ShapeDtypeStruct((T, 2 * BLOCK), BF16),
        compiler_params=_params(("arbitrary",)),
    )(h, wall)


HALO = 32
CONV_RC = 64
CONV_LC = 256


def _conv_taps(w_ref, src, r0, lane0, offset_of_tap):
    lanes = pl.ds(lane0, CONV_LC)
    out = None
    for b in range(8):
        taps = [k for k in range(CONV_K) if offset_of_tap(k) % 8 == b]
        if not taps:
            continue
        rows = CONV_RC + (8 if b else 0)
        vb = None
        for k in taps:
            term = w_ref[k:k + 1, lanes] * src[pl.ds(r0 + (offset_of_tap(k) - b), rows), lanes]
            vb = term if vb is None else vb + term
        vb = vb[b:b + CONV_RC] if b else vb
        out = vb if out is None else out + vb
    return out


def _conv_fwd(z, wdw, b_dw, ln_g, ln_b, wall, S, tm):
    T = z.shape[0]
    nt = S // tm
    hb = tm // HALO

    def body(cv_ref, cg_ref, cgate_ref, hcv_ref, hcg_ref, wdw_ref, bdw_ref, lng_ref, lnb_ref, wpw_ref,
             wbrc_ref, ya_ref, y_ref, rstd_ref, pw_ref, ubuf, cbuf):
        t = pl.program_id(1)
        ubuf[HALO:HALO + tm, :] = cv_ref[...].astype(F32) * _sig(cg_ref[...].astype(F32))
        hu = hcv_ref[...].astype(F32) * _sig(hcg_ref[...].astype(F32))
        ubuf[0:HALO, :] = jnp.where(t > 0, hu, 0.0)
        ubuf[HALO + tm:HALO + tm + 8, :] = jnp.zeros((8, D), F32)

        def chunk(ci, carry):
            r0 = pl.multiple_of(ci * CONV_RC, CONV_RC)
            for lg in range(D // CONV_LC):
                acc = _conv_taps(wdw_ref, ubuf, r0, lg * CONV_LC, lambda k: HALO - (CONV_K - 1) + k)
                cbuf[pl.ds(r0, CONV_RC), pl.ds(lg * CONV_LC, CONV_LC)] = acc
            return carry

        lax.fori_loop(0, tm // CONV_RC, chunk, 0)
        cc = cbuf[...] + bdw_ref[...]
        mu = jnp.mean(cc, axis=-1, keepdims=True)
        dd = cc - mu
        rstd = lax.rsqrt(jnp.mean(dd * dd, axis=-1, keepdims=True) + EPS)
        yn = dd * rstd
        y_ref[...] = yn.astype(BF16)
        rstd_ref[...] = rstd
        n = yn * lng_ref[...] + lnb_ref[...]
        s = n * _sig(n)
        pw = _mm(s.astype(BF16), wpw_ref[...])
        pw_ref[...] = pw.astype(BF16)
        gt = cgate_ref[...].astype(F32)
        ya_in = pw * (gt * _sig(gt))
        ya_ref[...] = _mm(ya_in.astype(BF16), wbrc_ref[...]).astype(BF16)

    def row(b, t):
        return b * nt + t

    def halo(b, t):
        return jnp.maximum(row(b, t) * hb - 1, 0)

    vec = pl.BlockSpec((1, D), lambda b, t: (0, 0))
    tile = lambda j: pl.BlockSpec((tm, D), lambda b, t: (row(b, t), j))
    out_tile = pl.BlockSpec((tm, D), lambda b, t: (row(b, t), 0))
    return pl.pallas_call(
        body, name="conv_fwd", grid=(T // S, nt),
        in_specs=[tile(ZB_CVAL), tile(ZB_CGLU), tile(ZB_CGATE),
                  pl.BlockSpec((HALO, D), lambda b, t: (halo(b, t), ZB_CVAL)),
                  pl.BlockSpec((HALO, D), lambda b, t: (halo(b, t), ZB_CGLU)),
                  pl.BlockSpec((32, D), lambda b, t: (0, 0)), vec, vec, vec,
                  pl.BlockSpec((D, D), lambda b, t: (0, 0)),
                  pl.BlockSpec((D, D), lambda b, t: (1, 0))],
        out_specs=(out_tile, out_tile, pl.BlockSpec((tm, 1), lambda b, t: (row(b, t), 0)), out_tile),
        out_shape=(jax.ShapeDtypeStruct((T, D), BF16), jax.ShapeDtypeStruct((T, D), BF16),
                   jax.ShapeDtypeStruct((T, 1), F32), jax.ShapeDtypeStruct((T, D), BF16)),
        scratch_shapes=[pltpu.VMEM((tm + HALO + 8, D), F32), pltpu.VMEM((tm, D), F32)],
        compiler_params=_params(("arbitrary", "arbitrary")),
    )(z, z, z, z, z, wdw, b_dw, ln_g, ln_b, wall, wall)


def _swap_matrix():
    r = lax.broadcasted_iota(jnp.int32, (BLOCK, BLOCK), 0)
    l = lax.broadcasted_iota(jnp.int32, (BLOCK, BLOCK), 1)
    lh = l & (HEAD_DIM - 1)
    half = ROPE_DIM // 2
    hit = ((lh < half) & (r == l + half)) | ((lh >= half) & (lh < ROPE_DIM) & (r == l - half))
    return jnp.where(hit, 1.0, 0.0).astype(BF16)


def _rope(tb, cos, sin, pswap):
    return tb.astype(F32) * cos + _mm(tb, pswap) * sin


def _rope_f32(tv, cos, sin, pswap):
    hi = tv.astype(BF16)
    lo = (tv - hi.astype(F32)).astype(BF16)
    return tv * cos + (_mm(hi, pswap) + _mm(lo, pswap)) * sin


def _kv_variants(kv):
    lane = lax.broadcasted_iota(jnp.int32, kv.shape, 1)
    lo = lane < HEAD_DIM
    sw = pltpu.roll(kv, HEAD_DIM, 1)
    z = jnp.zeros_like(kv)
    g0 = (jnp.where(lo, kv, z).astype(BF16), jnp.where(lo, z, sw).astype(BF16))
    g1 = (jnp.where(lo, sw, z).astype(BF16), jnp.where(lo, z, kv).astype(BF16))
    return (g0, g1)


def _band_mask(nq):
    qi = lax.broadcasted_iota(jnp.int32, (nq * BLOCK, 2 * BLOCK), 0) & (BLOCK - 1)
    sj = lax.broadcasted_iota(jnp.int32, (nq * BLOCK, 2 * BLOCK), 1)
    return (sj <= qi + BLOCK) & (sj > qi), sj


def _sink_rep(sink_ref, g, e):
    return jnp.concatenate(
        [jnp.full((BLOCK, BLOCK), sink_ref[8 * g + 2 * j + e], F32) for j in range(4)], axis=0)


def _softmax_sink(s, valid, sk):
    rows = s.shape[0]
    s = jnp.where(valid, s, -1e30)
    m = jnp.maximum(jnp.broadcast_to(jnp.max(s, axis=-1, keepdims=True), (rows, BLOCK)), sk)
    p = jnp.exp(s - jnp.concatenate([m, m], axis=1))
    ps = jnp.exp(sk - m)
    inv = 1.0 / (_mm(p.astype(BF16), jnp.ones((2 * BLOCK, BLOCK), BF16)) + ps)
    return p * jnp.concatenate([inv, inv], axis=1), ps * inv


def _attn_fwd(z, zkv, cos_t, sin_t, sinks, S, tq):
    T = z.shape[0]
    nt = S // tq
    nq = tq // BLOCK

    def body(sink_ref, q_ref, kv_ref, hkv_ref, cos_ref, sin_ref, hcos_ref, hsin_ref, o_ref):
        t = pl.program_id(1)
        cos = cos_ref[...]
        sin = sin_ref[...]
        pswap = _swap_matrix()
        kv = jnp.concatenate([hkv_ref[...], kv_ref[...]], axis=0)
        cos_k = jnp.concatenate([hcos_ref[...], cos], axis=0)
        sin_k = jnp.concatenate([hsin_ref[...], sin], axis=0)
        kx = _kv_variants(_rope(kv[:, :BLOCK], cos_k, sin_k, pswap))
        vx = _kv_variants(kv[:, BLOCK:].astype(F32))
        band, sj = _band_mask(4)
        qs = [(_rope(q_ref[:, 128 * hp:128 * hp + 128], cos, sin, pswap) * 0.125).astype(BF16)
              for hp in range(8)]
        for n in range(nq):
            first = (t == 0) & (n == 0)
            valid = band & (jnp.logical_not(first) | (sj >= BLOCK))
            r0 = n * BLOCK
            for g in range(2):
                lhs = jnp.concatenate([qs[4 * g + j][r0:r0 + BLOCK] for j in range(4)], axis=0)
                acc = jnp.zeros((4 * BLOCK, BLOCK), F32)
                for e in range(2):
                    s = _mm_nt(lhs, kx[g][e][r0:r0 + 2 * BLOCK])
                    p, _ = _softmax_sink(s, valid, _sink_rep(sink_ref, g, e))
                    acc = acc + _mm(p.astype(BF16), vx[g][e][r0:r0 + 2 * BLOCK])
                for j in range(4):
                    o_ref[r0:r0 + BLOCK, 128 * (4 * g + j):128 * (4 * g + j) + 128] = (
                        acc[j * BLOCK:(j + 1) * BLOCK].astype(BF16))

    def row(b, t):
        return b * nt + t

    def halo(b, t):
        return jnp.maximum(row(b, t) * nq - 1, 0)

    return pl.pallas_call(
        body, name="attn_fwd", grid=(T // S, nt),
        in_specs=[pl.BlockSpec(memory_space=pltpu.SMEM),
                  pl.BlockSpec((tq, D), lambda b, t: (row(b, t), ZB_Q)),
                  pl.BlockSpec((tq, 2 * BLOCK), lambda b, t: (row(b, t), 0)),
                  pl.BlockSpec((BLOCK, 2 * BLOCK), lambda b, t: (halo(b, t), 0)),
                  pl.BlockSpec((tq, BLOCK), lambda b, t: (row(b, t), 0)),
                  pl.BlockSpec((tq, BLOCK), lambda b, t: (row(b, t), 0)),
                  pl.BlockSpec((BLOCK, BLOCK), lambda b, t: (halo(b, t), 0)),
                  pl.BlockSpec((BLOCK, BLOCK), lambda b, t: (halo(b, t), 0))],
        out_specs=pl.BlockSpec((tq, D), lambda b, t: (row(b, t), 0)),
        out_shape=jax.ShapeDtypeStruct((T, D), BF16),
        compiler_params=_params(("arbitrary", "arbitrary")),
    )(sinks, z, zkv, zkv, cos_t, sin_t, cos_t, sin_t)


def _tail_a(x, tgt, p, o, ya, z, ln_post, wall, wppt, tm):
    T = x.shape[0]
    last = T // tm - 1

    def body(x_ref, tgt_ref, p_ref, o_ref, ya_ref, ag_ref, gc_ref, ga_ref, lnp_ref, wbra_ref, wout_ref,
             wpg_ref, wppt_ref, loss_ref, dx1_ref, dm_ref, yb_ref, glnp_ref, gpack_ref, gwpp_ref,
             acc_out, acc_pg, sem):
        i = pl.program_id(0)

        @pl.when(i == 0)
        def _():
            acc_out[...] = jnp.zeros_like(acc_out)
            acc_pg[...] = jnp.zeros_like(acc_pg)
            gwpp_ref[...] = jnp.zeros_like(gwpp_ref)
            glnp_ref[...] = jnp.zeros_like(glnp_ref)
            loss_ref[...] = jnp.zeros_like(loss_ref)

        ag = ag_ref[...].astype(F32)
        yb_in = (o_ref[...].astype(F32) * (ag * _sig(ag))).astype(BF16)
        yb = _mm(yb_in, wbra_ref[...])
        yb_ref[...] = yb.astype(BF16)
        m = (_sig(gc_ref[...].astype(F32)) * ya_ref[...].astype(F32)
             + _sig(ga_ref[...].astype(F32)) * yb).astype(BF16)
        mo = _mm(m, wout_ref[...])
        r2 = lax.rsqrt(jnp.mean(mo * mo, axis=-1, keepdims=True) + EPS)
        nrm = mo * r2
        g_post = lnp_ref[...]
        x1 = x_ref[...] + nrm * g_post
        x1b = x1.astype(BF16)
        gate = _sig(_mm(x1b, wpg_ref[...]))
        pb = p_ref[...].astype(BF16)
        pp = _mm_nt(pb, wppt_ref[...])
        err = x1 + gate * pp - tgt_ref[...]
        loss_ref[...] += 0.5 * jnp.sum(jnp.sum(err * err, axis=-1, keepdims=True) * (1.0 / D),
                                       axis=0, keepdims=True)
        dx2 = err * (1.0 / D)
        dgp = (dx2 * pp * gate * (1.0 - gate)).astype(BF16)
        dpp = (dx2 * gate).astype(BF16)
        dx1 = dx2 + _mm_nt(dgp, wpg_ref[...])
        dx1_ref[...] = dx1
        acc_pg[...] += _mm_tn(x1b, dgp)
        gwpp_ref[...] += _mm_tn(dpp, pb)
        glnp_ref[...] += jnp.sum(dx1 * nrm, axis=0, keepdims=True)
        a = dx1 * g_post
        dmo = (r2 * (a - nrm * jnp.mean(a * nrm, axis=-1, keepdims=True))).astype(BF16)
        dm_ref[...] = _mm_nt(dmo, wout_ref[...]).astype(BF16)
        acc_out[...] += _mm_tn(m, dmo)

        @pl.when(i == last)
        def _():
            _flush_to_pack(acc_out, gpack_ref, 3 * D, sem.at[0])
            _flush_to_pack(acc_pg, gpack_ref, 4 * D, sem.at[1])

    tile = pl.BlockSpec((tm, D), lambda i: (i, 0))
    ztile = lambda j: pl.BlockSpec((tm, D), lambda i: (i, j))
    wsq = lambda k: pl.BlockSpec((D, D), lambda i: (k, 0))
    const = lambda shp: pl.BlockSpec(shp, lambda i: (0, 0))
    any_spec = pl.BlockSpec(memory_space=pl.ANY)
    return pl.pallas_call(
        body, name="tail_a", grid=(T // tm,),
        in_specs=[tile, tile, pl.BlockSpec((tm, PLE), lambda i: (i, 0)), tile, tile, ztile(ZB_AGATE),
                  ztile(ZB_GCONV), ztile(ZB_GATTN), const((1, D)), wsq(2), wsq(3), wsq(4), const((D, PLE))],
        out_specs=(const((1, 1)), tile, tile, tile, const((1, D)), any_spec, const((D, PLE))),
        out_shape=(jax.ShapeDtypeStruct((1, 1), F32), jax.ShapeDtypeStruct((T, D), F32),
                   jax.ShapeDtypeStruct((T, D), BF16), jax.ShapeDtypeStruct((T, D), BF16),
                   jax.ShapeDtypeStruct((1, D), F32), jax.ShapeDtypeStruct((N_SHARDS, PACK_ROWS, D), F32),
                   jax.ShapeDtypeStruct((D, PLE), F32)),
        scratch_shapes=[pltpu.VMEM((D, D), F32), pltpu.VMEM((D, D), F32), pltpu.SemaphoreType.DMA((2,))],
        compiler_params=_params(("arbitrary",)),
    )(x, tgt, p, o, ya, z, z, z, ln_post, wall, wall, wall, wppt)


def _dsilu(v, sg):
    return sg * (1.0 + v * (1.0 - sg))


def _tail_b(dm, ya, yb, o, z, pw, y, rstd, ln_g, ln_b, wall, gpack, tm):
    T = dm.shape[0]
    last = T // tm - 1

    def body(dm_ref, ya_ref, yb_ref, o_ref, ag_ref, gc_ref, ga_ref, cgate_ref, pw_ref, y_ref, rstd_ref,
             lng_ref, lnb_ref, wpw_ref, wbrc_ref, wbra_ref, gpack_in, dg_ref, do_ref, dc_ref, gvec_ref,
             gpack_ref, acc_bra, acc_brc, acc_pw, sem):
        i = pl.program_id(0)

        @pl.when(i == 0)
        def _():
            acc_bra[...] = jnp.zeros_like(acc_bra)
            acc_brc[...] = jnp.zeros_like(acc_brc)
            acc_pw[...] = jnp.zeros_like(acc_pw)
            gvec_ref[...] = jnp.zeros_like(gvec_ref)

        g = lng_ref[...]

        def part(rs):
            dm_v = dm_ref[rs, :].astype(F32)
            sgc = _sig(gc_ref[rs, :].astype(F32))
            sga = _sig(ga_ref[rs, :].astype(F32))
            dya = (dm_v * sgc).astype(BF16)
            dyb = (dm_v * sga).astype(BF16)
            dg_ref[rs, D:2 * D] = (dm_v * ya_ref[rs, :].astype(F32) * sgc * (1.0 - sgc)).astype(BF16)
            dg_ref[rs, 2 * D:3 * D] = (dm_v * yb_ref[rs, :].astype(F32) * sga * (1.0 - sga)).astype(BF16)
            ag = ag_ref[rs, :].astype(F32)
            sag = _sig(ag)
            sa = ag * sag
            ov = o_ref[rs, :].astype(F32)
            dyb_in = _mm_nt(dyb, wbra_ref[...])
            do_ref[rs, :] = (dyb_in * sa).astype(BF16)
            dg_ref[rs, 0:D] = (dyb_in * ov * _dsilu(ag, sag)).astype(BF16)
            gt = cgate_ref[rs, :].astype(F32)
            sgt = _sig(gt)
            sgate = gt * sgt
            pw = pw_ref[rs, :].astype(F32)
            dya_in = _mm_nt(dya, wbrc_ref[...])
            dpw = (dya_in * sgate).astype(BF16)
            dg_ref[rs, 3 * D:4 * D] = (dya_in * pw * _dsilu(gt, sgt)).astype(BF16)
            yn = y_ref[rs, :].astype(F32)
            n = yn * g + lnb_ref[...]
            sn = _sig(n)
            dn = _mm_nt(dpw, wpw_ref[...]) * _dsilu(n, sn)
            dy = dn * g
            dc = rstd_ref[rs, :] * (dy - jnp.mean(dy, axis=-1, keepdims=True)
                                    - yn * jnp.mean(dy * yn, axis=-1, keepdims=True))
            dc_ref[rs, :] = dc.astype(BF16)
            sums = (jnp.sum(dn * yn, axis=0, keepdims=True), jnp.sum(dn, axis=0, keepdims=True),
                    jnp.sum(dc, axis=0, keepdims=True))
            return ((ov * sa).astype(BF16), dyb, (pw * sgate).astype(BF16), dya, (n * sn).astype(BF16), dpw,
                    sums)

        parts = [part(pl.ds(r * (tm // TAIL_PARTS), tm // TAIL_PARTS)) for r in range(TAIL_PARTS)]
        cat = lambda j: jnp.concatenate([pt[j] for pt in parts], axis=0)
        acc_bra[...] += _mm_tn(cat(0), cat(1))
        acc_brc[...] += _mm_tn(cat(2), cat(3))
        acc_pw[...] += _mm_tn(cat(4), cat(5))
        for j in range(3):
            gvec_ref[j:j + 1, :] += sum(pt[6][j] for pt in parts)

        @pl.when(i == last)
        def _():
            _flush_to_pack(acc_pw, gpack_ref, 0, sem.at[0])
            _flush_to_pack(acc_brc, gpack_ref, D, sem.at[1])
            _flush_to_pack(acc_bra, gpack_ref, 2 * D, sem.at[2])

    tile = pl.BlockSpec((tm, D), lambda i: (i, 0))
    ztile = lambda j: pl.BlockSpec((tm, D), lambda i: (i, j))
    wsq = lambda k: pl.BlockSpec((D, D), lambda i: (k, 0))
    const = lambda shp: pl.BlockSpec(shp, lambda i: (0, 0))
    any_spec = pl.BlockSpec(memory_space=pl.ANY)
    return pl.pallas_call(
        body, name="tail_b", grid=(T // tm,),
        in_specs=[tile, tile, tile, tile, ztile(ZB_AGATE), ztile(ZB_GCONV), ztile(ZB_GATTN), ztile(ZB_CGATE),
                  tile, tile, pl.BlockSpec((tm, 1), lambda i: (i, 0)), const((1, D)), const((1, D)), wsq(0),
                  wsq(1), wsq(2), any_spec],
        out_specs=(pl.BlockSpec((tm, 4 * D), lambda i: (i, 0)), tile, tile, const((8, D)), any_spec),
        out_shape=(jax.ShapeDtypeStruct((T, 7 * D), BF16), jax.ShapeDtypeStruct((T, D), BF16),
                   jax.ShapeDtypeStruct((T, D), BF16), jax.ShapeDtypeStruct((8, D), F32),
                   jax.ShapeDtypeStruct(gpack.shape, F32)),
        input_output_aliases={16: 4},
        scratch_shapes=[pltpu.VMEM((D, D), F32), pltpu.VMEM((D, D), F32), pltpu.VMEM((D, D), F32),
                        pltpu.SemaphoreType.DMA((3,))],
        compiler_params=_params(("arbitrary",)),
    )(dm, ya, yb, o, z, z, z, z, pw, y, rstd, ln_g, ln_b, wall, wall, wall, gpack)


def _conv_bwd(dc, z, wdw, dz, S, tm):
    T = dc.shape[0]
    nt = S // tm
    hb = tm // HALO
    nrows = T // HALO

    def body(dc_ref, hdc_ref, cv_ref, cg_ref, hcv_ref, hcg_ref, wdw_ref, dz_in, dz_ref, gw_ref, ubuf, dcbuf,
             dubuf, dwacc, shbuf):
        b = pl.program_id(0)
        t = pl.program_id(1)

        @pl.when((b == 0) & (t == 0))
        def _():
            dwacc[...] = jnp.zeros_like(dwacc)

        cv = cv_ref[...].astype(F32)
        sg = _sig(cg_ref[...].astype(F32))
        ubuf[HALO:HALO + tm, :] = cv * sg
        hu = hcv_ref[...].astype(F32) * _sig(hcg_ref[...].astype(F32))
        ubuf[0:HALO, :] = jnp.where(t > 0, hu, 0.0)
        ubuf[HALO + tm:HALO + tm + 8, :] = jnp.zeros((8, D), F32)
        dcbuf[0:tm, :] = dc_ref[...].astype(F32)
        dcbuf[tm:tm + HALO, :] = jnp.where(t < nt - 1, hdc_ref[...].astype(F32), 0.0)
        dcbuf[tm + HALO:tm + HALO + 8, :] = jnp.zeros((8, D), F32)

        def chunk(ci, carry):
            r0 = pl.multiple_of(ci * CONV_RC, CONV_RC)
            for lg in range(D // CONV_LC):
                l0 = lg * CONV_LC
                dubuf[pl.ds(r0, CONV_RC), pl.ds(l0, CONV_LC)] = _conv_taps(
                    wdw_ref, dcbuf, r0, l0, lambda k: CONV_K - 1 - k)
                dcc = dcbuf[pl.ds(r0, CONV_RC), pl.ds(l0, CONV_LC)]
                zero8 = jnp.zeros((8, CONV_LC), F32)
                dcz = jnp.concatenate([zero8, dcc, zero8], axis=0)
                for bb in range(8):
                    taps = [k for k in range(CONV_K) if (HALO - (CONV_K - 1) + k) % 8 == bb]
                    if not taps:
                        continue
                    rows = CONV_RC + (8 if bb else 0)
                    if bb:
                        shbuf[bb] = dcz[8 - bb:8 - bb + rows]
                    for k in taps:
                        a8 = HALO - (CONV_K - 1) + k - bb
                        dcs = shbuf[bb] if bb else dcc
                        prod = dcs * ubuf[pl.ds(r0 + a8, rows), pl.ds(l0, CONV_LC)]
                        part = prod[0:8]
                        for q in range(1, rows // 8):
                            part = part + prod[8 * q:8 * q + 8]
                        dwacc[8 * k:8 * k + 8, pl.ds(l0, CONV_LC)] += part
            return carry

        lax.fori_loop(0, tm // CONV_RC, chunk, 0)
        du = dubuf[...]
        dz_ref[:, 0:D] = (du * sg).astype(BF16)
        dz_ref[:, D:2 * D] = (du * cv * sg * (1.0 - sg)).astype(BF16)

        @pl.when((b == pl.num_programs(0) - 1) & (t == nt - 1))
        def _():
            for k in range(32):
                gw_ref[k:k + 1, :] = jnp.sum(dwacc[8 * k:8 * k + 8, :], axis=0, keepdims=True)

    def row(b, t):
        return b * nt + t

    def prev_halo(b, t):
        return jnp.maximum(row(b, t) * hb - 1, 0)

    def next_halo(b, t):
        return jnp.minimum((row(b, t) + 1) * hb, nrows - 1)

    return pl.pallas_call(
        body, name="conv_bwd", grid=(T // S, nt),
        in_specs=[pl.BlockSpec((tm, D), lambda b, t: (row(b, t), 0)),
                  pl.BlockSpec((HALO, D), lambda b, t: (next_halo(b, t), 0)),
                  pl.BlockSpec((tm, D), lambda b, t: (row(b, t), ZB_CVAL)),
                  pl.BlockSpec((tm, D), lambda b, t: (row(b, t), ZB_CGLU)),
                  pl.BlockSpec((HALO, D), lambda b, t: (prev_halo(b, t), ZB_CVAL)),
                  pl.BlockSpec((HALO, D), lambda b, t: (prev_halo(b, t), ZB_CGLU)),
                  pl.BlockSpec((32, D), lambda b, t: (0, 0)),
                  pl.BlockSpec(memory_space=pl.ANY)],
        out_specs=(pl.BlockSpec((tm, 2 * D), lambda b, t: (row(b, t), ZB_CVAL // 2)),
                   pl.BlockSpec((32, D), lambda b, t: (0, 0))),
        out_shape=(jax.ShapeDtypeStruct(dz.shape, BF16), jax.ShapeDtypeStruct((32, D), F32)),
        input_output_aliases={7: 0},
        scratch_shapes=[pltpu.VMEM((tm + HALO + 8, D), F32), pltpu.VMEM((tm + HALO + 8, D), F32),
                        pltpu.VMEM((tm, D), F32), pltpu.VMEM((8 * 32, D), F32),
                        pltpu.VMEM((8, CONV_RC + 8, CONV_LC), F32)],
        compiler_params=_params(("arbitrary", "arbitrary")),
    )(dc, dc, z, z, z, z, wdw, dz)


def _attn_bwd(z, zkv, o, do, cos_t, sin_t, sinks, dz, S, tq):
    T = z.shape[0]
    nt = S // tq
    nq = tq // BLOCK

    def body(sink_ref, q_ref, kv_ref, hkv_ref, o_ref, do_ref, cos_ref, sin_ref, hcos_ref, hsin_ref, dz_in,
             dq_ref, dkv_ref, gs_ref, carry, dkacc, dvacc):
        b = pl.program_id(0)
        tt = pl.program_id(1)
        t = nt - 1 - tt

        @pl.when((b == 0) & (tt == 0))
        def _():
            gs_ref[...] = jnp.zeros_like(gs_ref)

        @pl.when(tt == 0)
        def _():
            carry[...] = jnp.zeros_like(carry)

        cos = cos_ref[...]
        sin = sin_ref[...]
        pswap = _swap_matrix()
        kv = jnp.concatenate([hkv_ref[...], kv_ref[...]], axis=0)
        cos_k = jnp.concatenate([hcos_ref[...], cos], axis=0)
        sin_k = jnp.concatenate([hsin_ref[...], sin], axis=0)
        kx = _kv_variants(_rope(kv[:, :BLOCK], cos_k, sin_k, pswap))
        vx = _kv_variants(kv[:, BLOCK:].astype(F32))
        band, sj = _band_mask(4)
        lo = lax.broadcasted_iota(jnp.int32, (4 * BLOCK, BLOCK), 1) < HEAD_DIM
        ones = jnp.ones((2 * BLOCK, 2 * BLOCK), BF16)
        qs = [(_rope(q_ref[:, 128 * hp:128 * hp + 128], cos, sin, pswap) * 0.125).astype(BF16)
              for hp in range(8)]
        dkacc[...] = jnp.zeros_like(dkacc)
        dvacc[...] = jnp.zeros_like(dvacc)
        gsum = jnp.zeros((1, BLOCK), F32)
        hlane = lax.broadcasted_iota(jnp.int32, (1, BLOCK), 1)
        for n in range(nq):
            first = (t == 0) & (n == 0)
            valid = band & (jnp.logical_not(first) | (sj >= BLOCK))
            r0 = n * BLOCK
            for g in range(2):
                cols = [slice(128 * (4 * g + j), 128 * (4 * g + j) + 128) for j in range(4)]
                lhs = jnp.concatenate([qs[4 * g + j][r0:r0 + BLOCK] for j in range(4)], axis=0)
                dov = jnp.concatenate([do_ref[r0:r0 + BLOCK, cs] for cs in cols], axis=0)
                prod = dov.astype(F32) * jnp.concatenate(
                    [o_ref[r0:r0 + BLOCK, cs] for cs in cols], axis=0).astype(F32)
                lhs_t = lhs.T
                dov_t = dov.T
                dq = jnp.zeros((4 * BLOCK, BLOCK), F32)
                dk_t = jnp.zeros((HEAD_DIM, 2 * BLOCK), F32)
                dv_t = jnp.zeros((HEAD_DIM, 2 * BLOCK), F32)
                for e in range(2):
                    kw = kx[g][e][r0:r0 + 2 * BLOCK]
                    vw = vx[g][e][r0:r0 + 2 * BLOCK]
                    s = _mm_nt(lhs, kw)
                    p, psink = _softmax_sink(s, valid, _sink_rep(sink_ref, g, e))
                    pe = jnp.where(lo if e == 0 else jnp.logical_not(lo), prod, 0.0)
                    pe_hi = pe.astype(BF16)
                    pe_lo = (pe - pe_hi.astype(F32)).astype(BF16)
                    delta = _mm(jnp.concatenate([pe_hi, pe_lo], axis=1), ones)
                    ds = (p * (_mm_nt(dov, vw) - delta)).astype(BF16)
                    dq = dq + _mm(ds, kw)
                    dims = slice(HEAD_DIM * e, HEAD_DIM * (e + 1))
                    dk_t = dk_t + _mm(lhs_t[dims], ds)
                    dv_t = dv_t + _mm(dov_t[dims], p.astype(BF16))
                    gs = -psink * delta[:, 0:BLOCK]
                    for j in range(4):
                        tot = jnp.sum(gs[j * BLOCK:(j + 1) * BLOCK], axis=0, keepdims=True)
                        gsum = gsum + jnp.where(hlane == 8 * g + 2 * j + e, tot, 0.0)
                dkacc[HEAD_DIM * g:HEAD_DIM * (g + 1), r0:r0 + 2 * BLOCK] += dk_t
                dvacc[HEAD_DIM * g:HEAD_DIM * (g + 1), r0:r0 + 2 * BLOCK] += dv_t
                for j in range(4):
                    dqj = _rope_f32(dq[j * BLOCK:(j + 1) * BLOCK] * 0.125, cos[r0:r0 + BLOCK],
                                    -sin[r0:r0 + BLOCK], pswap)
                    dq_ref[r0:r0 + BLOCK, cols[j]] = dqj.astype(BF16)
        gs_ref[0:1, :] += gsum
        dk_all = dkacc[...]
        dv_all = dvacc[...]
        dk_last = dk_all[:, tq:tq + BLOCK] + carry[0:BLOCK, :]
        dv_last = dv_all[:, tq:tq + BLOCK] + carry[BLOCK:2 * BLOCK, :]
        carry[0:BLOCK, :] = dk_all[:, 0:BLOCK]
        carry[BLOCK:2 * BLOCK, :] = dv_all[:, 0:BLOCK]
        if nq > 1:
            dk_tile = jnp.concatenate([dk_all[:, BLOCK:tq], dk_last], axis=1)
            dv_tile = jnp.concatenate([dv_all[:, BLOCK:tq], dv_last], axis=1)
        else:
            dk_tile, dv_tile = dk_last, dv_last
        dkv_ref[:, 0:BLOCK] = _rope_f32(dk_tile.T, cos, -sin, pswap).astype(BF16)
        dkv_ref[:, BLOCK:2 * BLOCK] = dv_tile.T.astype(BF16)

    def row(b, tt):
        return b * nt + (nt - 1 - tt)

    def halo(b, tt):
        return jnp.maximum(row(b, tt) * nq - 1, 0)

    tile = pl.BlockSpec((tq, D), lambda b, tt: (row(b, tt), 0))
    return pl.pallas_call(
        body, name="attn_bwd", grid=(T // S, nt),
        in_specs=[pl.BlockSpec(memory_space=pltpu.SMEM),
                  pl.BlockSpec((tq, D), lambda b, tt: (row(b, tt), ZB_Q)),
                  pl.BlockSpec((tq, 2 * BLOCK), lambda b, tt: (row(b, tt), 0)),
                  pl.BlockSpec((BLOCK, 2 * BLOCK), lambda b, tt: (halo(b, tt), 0)),
                  tile, tile,
                  pl.BlockSpec((tq, BLOCK), lambda b, tt: (row(b, tt), 0)),
                  pl.BlockSpec((tq, BLOCK), lambda b, tt: (row(b, tt), 0)),
                  pl.BlockSpec((BLOCK, BLOCK), lambda b, tt: (halo(b, tt), 0)),
                  pl.BlockSpec((BLOCK, BLOCK), lambda b, tt: (halo(b, tt), 0)),
                  pl.BlockSpec(memory_space=pl.ANY)],
        out_specs=(pl.BlockSpec((tq, D), lambda b, tt: (row(b, tt), ZB_Q)),
                   pl.BlockSpec((tq, 2 * BLOCK), lambda b, tt: (row(b, tt), 0)),
                   pl.BlockSpec((8, BLOCK), lambda b, tt: (0, 0))),
        out_shape=(jax.ShapeDtypeStruct(dz.shape, BF16), jax.ShapeDtypeStruct((T, 2 * BLOCK), BF16),
                   jax.ShapeDtypeStruct((8, BLOCK), F32)),
        input_output_aliases={10: 0},
        scratch_shapes=[pltpu.VMEM((2 * BLOCK, BLOCK), F32), pltpu.VMEM((BLOCK, tq + BLOCK), F32),
                        pltpu.VMEM((BLOCK, tq + BLOCK), F32)],
        compiler_params=_params(("arbitrary", "arbitrary")),
    )(sinks, z, zkv, zkv, o, do, cos_t, sin_t, cos_t, sin_t, dz)


def _exchange_copies(g_ref, r1_ref, send_sems, recv_sems):
    x, y, c = _coords()
    return [pltpu.make_async_remote_copy(
        src_ref=g_ref.at[:, pl.ds(pl.multiple_of((1 - c) * HALF_ROWS, 32), HALF_ROWS), :], dst_ref=r1_ref,
        send_sem=send_sems.at[0], recv_sem=recv_sems.at[0], device_id=(x, y, 1 - c), device_id_type=MESH)]


def _chip_sum_copies(cs_ref, r2_ref, send_sems, recv_sems):
    x, y, c = _coords()
    return [pltpu.make_async_remote_copy(
        src_ref=cs_ref.at[2 * px + py], dst_ref=r2_ref.at[k], send_sem=send_sems.at[k],
        recv_sem=recv_sems.at[k], device_id=(px, py, c), device_id_type=MESH)
        for k, (px, py) in enumerate(_chip_peers(x, y))]


def _dh(dz, dz_kv, wall, x, dx1, ln_pre, tm, tile0, ntiles, gx_prev, name, copies, src, landing):
    T = x.shape[0]
    nsem = 3

    def body(*refs):
        dz_ref, kv_ref, w_ref, wkv_ref, x_ref, dx1_ref, g_ref, src_ref = refs[:8]
        gx_ref, glp_ref, land_ref, acc, send_sems, recv_sems = refs[-6:]
        i = pl.program_id(0)
        k = pl.program_id(1)

        @pl.when((i == 0) & (k == 0))
        def _():
            glp_ref[...] = jnp.zeros_like(glp_ref)
            for cp in copies(src_ref, land_ref, send_sems, recv_sems):
                cp.start()

        @pl.when(k == 0)
        def _():
            acc[...] = _mm(dz_ref[...], w_ref[...])

        @pl.when((k > 0) & (k < 7))
        def _():
            acc[...] += _mm(dz_ref[...], w_ref[...])

        @pl.when(k == 7)
        def _():
            dh = acc[...] + _mm(kv_ref[...], wkv_ref[...])
            xv = x_ref[...]
            r = lax.rsqrt(jnp.mean(xv * xv, axis=-1, keepdims=True) + EPS)
            xr = xv * r
            glp_ref[...] += jnp.sum(dh * xr, axis=0, keepdims=True)
            a = dh * g_ref[...]
            gx_ref[...] = dx1_ref[...] + r * (a - xr * jnp.mean(a * xr, axis=-1, keepdims=True))

        @pl.when((i == ntiles - 1) & (k == 7))
        def _():
            cps = copies(src_ref, land_ref, send_sems, recv_sems)
            for cp in cps:
                cp.wait_recv()
            for cp in cps:
                cp.wait_send()

    tile = pl.BlockSpec((tm, D), lambda i, k: (tile0 + i, 0))
    any_spec = pl.BlockSpec(memory_space=pl.ANY)
    operands = [dz, dz_kv, wall, wall, x, dx1, ln_pre, src] + ([] if gx_prev is None else [gx_prev])
    return pl.pallas_call(
        body, name=name, grid=(ntiles, 8),
        in_specs=[pl.BlockSpec((tm, D), lambda i, k: (tile0 + i, jnp.minimum(k, 6))),
                  pl.BlockSpec((tm, 2 * BLOCK), lambda i, k: (tile0 + i, 0)),
                  pl.BlockSpec((D, D), lambda i, k: (5 + jnp.minimum(k, 6), 0)),
                  pl.BlockSpec((2 * BLOCK, D), lambda i, k: ((WT0 + ZKV) // (2 * BLOCK), 0)),
                  tile, tile, pl.BlockSpec((1, D), lambda i, k: (0, 0)), any_spec]
        + ([] if gx_prev is None else [any_spec]),
        out_specs=(tile, pl.BlockSpec((1, D), lambda i, k: (0, 0)), any_spec),
        out_shape=(jax.ShapeDtypeStruct((T, D), F32), jax.ShapeDtypeStruct((1, D), F32), landing),
        input_output_aliases={} if gx_prev is None else {8: 0},
        scratch_shapes=[pltpu.VMEM((tm, D), F32), pltpu.SemaphoreType.DMA((nsem,)),
                        pltpu.SemaphoreType.DMA((nsem,))],
        compiler_params=_params(("arbitrary", "arbitrary")),
    )(*operands)


def _hand_over(a):
    def body(a_ref, o_ref):
        del a_ref, o_ref

    any_spec = pl.BlockSpec(memory_space=pl.ANY)
    return pl.pallas_call(
        body, name="hand_over", out_shape=jax.ShapeDtypeStruct(a.shape, a.dtype),
        in_specs=[any_spec], out_specs=any_spec, input_output_aliases={0: 0},
    )(a)


def _gwt(dz, h, gpack, tt):
    T = dz.shape[0]
    last = T // tt - 1

    def body(dz_ref, h_ref, gpack_in, gpack_ref, acc, sem):
        j = pl.program_id(0)
        t = pl.program_id(1)

        @pl.when(t == 0)
        def _():
            acc[...] = _mm_tn(dz_ref[...], h_ref[...])

        @pl.when(t > 0)
        def _():
            acc[...] += _mm_tn(dz_ref[...], h_ref[...])

        for jj in range(7):
            @pl.when((t == last) & (j == jj))
            def _(jj=jj):
                _flush_to_pack(acc, gpack_ref, WT0 + jj * D, sem)

    any_spec = pl.BlockSpec(memory_space=pl.ANY)
    return pl.pallas_call(
        body, name="gwt", grid=(7, T // tt),
        in_specs=[pl.BlockSpec((tt, D), lambda j, t: (t, j)), pl.BlockSpec((tt, D), lambda j, t: (t, 0)),
                  any_spec],
        out_specs=any_spec, out_shape=jax.ShapeDtypeStruct(gpack.shape, F32), input_output_aliases={2: 0},
        scratch_shapes=[pltpu.VMEM((D, D), F32), pltpu.SemaphoreType.DMA],
        compiler_params=_params(("arbitrary", "arbitrary")),
    )(dz, h, gpack)


def _gwt_kv(dz_kv, h, gppt, gpack, tt):
    T = dz_kv.shape[0]
    last = T // tt - 1

    def body(dz_ref, h_ref, gppt_ref, gpack_in, gpack_ref, acc, sem):
        t = pl.program_id(0)

        @pl.when(t == 0)
        def _():
            acc[...] = _mm_tn(dz_ref[...], h_ref[...])

        @pl.when(t > 0)
        def _():
            acc[...] += _mm_tn(dz_ref[...], h_ref[...])

        @pl.when(t == last)
        def _():
            _flush_to_pack(acc, gpack_ref, WT0 + ZKV, sem)
            _flush_to_pack(gppt_ref, gpack_ref, WPP0, sem)

    any_spec = pl.BlockSpec(memory_space=pl.ANY)
    return pl.pallas_call(
        body, name="gwt_kv", grid=(T // tt,),
        in_specs=[pl.BlockSpec((tt, 2 * BLOCK), lambda t: (t, 0)), pl.BlockSpec((tt, D), lambda t: (t, 0)),
                  pl.BlockSpec((PLE, D), lambda t: (0, 0)), any_spec],
        out_specs=any_spec, out_shape=jax.ShapeDtypeStruct(gpack.shape, F32), input_output_aliases={3: 0},
        scratch_shapes=[pltpu.VMEM((2 * BLOCK, D), F32), pltpu.SemaphoreType.DMA],
        compiler_params=_params(("arbitrary",)),
    )(dz_kv, h, gppt, gpack)


_BC1 = 1.0 - ADAM_B1 ** ADAM_STEP
_BC2 = 1.0 - ADAM_B2 ** ADAM_STEP


def _adamw_math(w, g, m, v):
    m = ADAM_B1 * m + (1.0 - ADAM_B1) * g
    v = ADAM_B2 * v + (1.0 - ADAM_B2) * (g * g)
    delta = -ADAM_LR * ((m / _BC1) / (jnp.sqrt(v / _BC2) + ADAM_EPS) + ADAM_WD * w)
    return delta, m, v


def _adamw_rows(g, w, m, v, rows, name):
    R, C = w.shape

    def body(g_ref, w_ref, m_ref, v_ref, d_ref, nm_ref, nv_ref):
        d, nm, nv = _adamw_math(w_ref[...], g_ref[...], m_ref[...], v_ref[...])
        d_ref[...] = d
        nm_ref[...] = nm
        nv_ref[...] = nv

    spec = pl.BlockSpec((rows, C), lambda i: (i, 0))
    shp = jax.ShapeDtypeStruct((R, C), F32)
    return pl.pallas_call(
        body, name=name, grid=(R // rows,), in_specs=[spec] * 4, out_specs=(spec,) * 3,
        out_shape=(shp,) * 3, compiler_params=_params(("arbitrary",)),
    )(g, w, m, v)


def _adamw_square(gfin, ws, ms, vs):
    rb = 64
    nb = SQ_SHARD // rb

    def body(*refs):
        g_refs = refs[0:5]
        w_refs, m_refs, v_refs = refs[5:10], refs[10:15], refs[15:20]
        outs = refs[20:]
        for k in range(5):
            gk = g_refs[k][...]
            d, nm, nv = _adamw_math(w_refs[k][...], gk, m_refs[k][...], v_refs[k][...])
            outs[4 * k][...] = gk
            outs[4 * k + 1][...] = d
            outs[4 * k + 2][...] = nm
            outs[4 * k + 3][...] = nv

    spec = pl.BlockSpec((rb, D), lambda i: (i, 0))
    gspecs = [pl.BlockSpec((rb, D), lambda i, k=k: ((WIN_SHARD + SQ_SHARD * k) // rb + i, 0))
              for k in range(5)]
    shp = jax.ShapeDtypeStruct((SQ_SHARD, D), F32)
    res = pl.pallas_call(
        body, name="adamw_square", grid=(nb,), in_specs=gspecs + [spec] * 15, out_specs=(spec,) * 20,
        out_shape=(shp,) * 20, compiler_params=_params(("arbitrary",)),
    )(*([gfin] * 5), *ws, *ms, *vs)
    return [tuple(res[4 * k:4 * k + 4]) for k in range(5)]


def _adamw_small(gs, ws, ms, vs):
    n = len(gs)

    def body(*refs):
        outs = refs[4 * n:]
        for k in range(n):
            d, nm, nv = _adamw_math(refs[n + k][...], refs[k][...], refs[2 * n + k][...],
                                    refs[3 * n + k][...])
            outs[3 * k][...] = d
            outs[3 * k + 1][...] = nm
            outs[3 * k + 2][...] = nv

    vm = pl.BlockSpec(memory_space=pltpu.VMEM)
    shapes = []
    for w in ws:
        shapes += [jax.ShapeDtypeStruct(w.shape, F32)] * 3
    res = pl.pallas_call(
        body, name="adamw_small", in_specs=[vm] * (4 * n), out_specs=(vm,) * (3 * n),
        out_shape=tuple(shapes),
    )(*gs, *ws, *ms, *vs)
    return [tuple(res[3 * k:3 * k + 3]) for k in range(n)]


def _rope_tables(positions):
    inv = jnp.power(ROPE_THETA, -jnp.arange(0, ROPE_DIM, 2, dtype=F32) / ROPE_DIM)
    inv_h = jnp.concatenate([inv, inv, jnp.zeros((HEAD_DIM - ROPE_DIM,), F32)])
    sign_h = np.array([-1.0] * (ROPE_DIM // 2) + [1.0] * (ROPE_DIM // 2) + [0.0] * (HEAD_DIM - ROPE_DIM),
                      np.float32)
    ang = positions.astype(F32).reshape(-1, 1) * jnp.concatenate([inv_h, inv_h])[None, :]
    return jnp.cos(ang), jnp.sin(ang) * np.concatenate([sign_h, sign_h])[None, :]


def _f32_to_bf16_rows(a):
    return lax.bitcast_convert_type(a, BF16).reshape(a.shape[0], 2 * a.shape[1])


def kernel(x, p, positions, w_in, ln_pre, ln_post, w_dw, b_dw, conv_ln_g, conv_ln_b, w_pw, sinks, w_br_conv, w_br_attn, w_out, w_ple_gate, w_ple_proj, loss_target, m_w_in, m_ln_pre, m_ln_post, m_w_dw, m_b_dw, m_conv_ln_g, m_conv_ln_b, m_w_pw, m_sinks, m_w_br_conv, m_w_br_attn, m_w_out, m_w_ple_gate, m_w_ple_proj, v_w_in, v_ln_pre, v_ln_post, v_w_dw, v_b_dw, v_conv_ln_g, v_conv_ln_b, v_w_pw, v_sinks, v_w_br_conv, v_w_br_attn, v_w_out, v_w_ple_gate, v_w_ple_proj):
    nb, S, _ = x.shape
    T = nb * S
    xc = lax.axis_index("x")
    yc = lax.axis_index("y")
    cc = lax.axis_index("c")
    shard = 2 * xc + yc

    sq_w = (w_pw, w_br_conv, w_br_attn, w_out, w_ple_gate)
    pack = jnp.concatenate(
        [w_in[0].T] + [w[0] for w in sq_w] + [w_ple_proj[0].T.reshape(WPP_SHARD, D)], axis=0).astype(BF16)
    wdw_shard = jnp.pad(w_dw[0], ((0, 1), (0, 0)))
    wall, wppf, wdw_all = _gather_weights(pack, wdw_shard)
    wdw = jnp.concatenate([wdw_all[s] for s in range(N_SHARDS)], axis=1)
    wppt = wppf.reshape(D, PLE)

    x2 = x.reshape(T, D)
    tgt = loss_target.reshape(T, D)
    p2 = p.reshape(T, PLE)
    cos_t, sin_t = _rope_tables(positions)
    sinks1 = sinks.reshape(N_HEADS)

    tm_big = min(TILE_PROJ, T)
    tm = min(TILE_TOKEN, S)
    tq = min(TILE_ATTN, S)

    z, h = _inproj(x2, ln_pre, wall, tm_big)
    zkv = _kvproj(h, wall, tm_big)
    ya, y, rstd, pw = _conv_fwd(z, wdw, b_dw, conv_ln_g, conv_ln_b, wall, S, tm)
    o = _attn_fwd(z, zkv, cos_t, sin_t, sinks1, S, tq)
    loss_p, dx1, dm, yb, g_ln_post, gpack, gw_ppt = _tail_a(x2, tgt, p2, o, ya, z, ln_post, wall, wppt, tm)

    dz, do, dc, gvec, gpack = _tail_b(dm, ya, yb, o, z, pw, y, rstd, conv_ln_g, conv_ln_b, wall, gpack, tm)
    dz, g_wdw = _conv_bwd(dc, z, wdw, dz, S, tm)
    dz, dkv, g_sinks = _attn_bwd(z, zkv, o, do, cos_t, sin_t, sinks1, dz, S, tq)
    gpack = _gwt(dz, h, gpack, tm_big)
    gpack = _gwt_kv(dkv, h, gw_ppt.reshape(PLE, D), gpack, tm_big)

    cidx = jnp.reshape(cc, (1,)).astype(jnp.int32)
    scidx = jnp.stack([shard, cc]).astype(jnp.int32)
    tm_dh = min(tm_big, T // 2)
    n_dh = T // tm_dh
    n_a = max(1, n_dh // 4)
    gx, g_ln_pre_a, r1 = _dh(
        dz, dkv, wall, x2, dx1, ln_pre, tm_dh, 0, n_a, None, "dh_exchange", _exchange_copies, gpack,
        jax.ShapeDtypeStruct((N_SHARDS, HALF_ROWS, D), F32))
    cs = _chip_sum(cidx, gpack, r1)
    gx, g_ln_pre_b, r2 = _dh(
        dz, dkv, wall, x2, dx1, ln_pre, tm_dh, n_a, n_dh - n_a, gx, "dh_send", _chip_sum_copies, cs,
        jax.ShapeDtypeStruct((3, HALF_ROWS, D), BF16))
    g_ln_pre = g_ln_pre_a + g_ln_pre_b
    gx = _hand_over(gx)
    gfin = _swap_halves(_final_half(scidx, gpack, r1, r2))

    row37 = jnp.concatenate([g_sinks[0:1, 0:N_HEADS], loss_p, jnp.zeros((1, D - N_HEADS - 1), F32)], axis=1)
    vec = jnp.concatenate([g_wdw, g_ln_pre, g_ln_post, gvec[2:3], gvec[0:1], gvec[1:2], row37,
                           jnp.zeros((VEC_ROWS - 38, D), F32)], axis=0)
    tot = _all_reduce_small(vec)

    g_w_in = gfin[0:WIN_SHARD].T[None]
    d_w_in, nm_w_in, nv_w_in = _adamw_rows(g_w_in[0], w_in[0], m_w_in[0], v_w_in[0], 128, "adamw_w_in")
    sq_m = (m_w_pw, m_w_br_conv, m_w_br_attn, m_w_out, m_w_ple_gate)
    sq_v = (v_w_pw, v_w_br_conv, v_w_br_attn, v_w_out, v_w_ple_gate)
    sq_res = _adamw_square(gfin, [w[0] for w in sq_w], [m[0] for m in sq_m], [v[0] for v in sq_v])
    g_wpp = gfin[WIN_SHARD + 5 * SQ_SHARD:PACK_ROWS].reshape(PLE, PLE).T
    g_dw_all = tot[0:CONV_K]
    g_dw = lax.dynamic_slice_in_dim(g_dw_all, shard * PLE, PLE, axis=1)
    small_g = [g_wpp, g_dw, tot[32:33], tot[33:34], tot[34:35], tot[35:36], tot[36:37],
               tot[37:38, 0:N_HEADS]]
    small_w = [w_ple_proj[0], w_dw[0], ln_pre, ln_post, b_dw, conv_ln_g, conv_ln_b, sinks]
    small_m = [m_w_ple_proj[0], m_w_dw[0], m_ln_pre, m_ln_post, m_b_dw, m_conv_ln_g, m_conv_ln_b, m_sinks]
    small_v = [v_w_ple_proj[0], v_w_dw[0], v_ln_pre, v_ln_post, v_b_dw, v_conv_ln_g, v_conv_ln_b, v_sinks]
    small = _adamw_small(small_g, small_w, small_m, small_v)

    loss = tot[37, N_HEADS]
    grads = [g_w_in, small_g[2], small_g[3], g_dw[None], small_g[4], small_g[5], small_g[6],
             sq_res[0][0][None], small_g[7], sq_res[1][0][None], sq_res[2][0][None], sq_res[3][0][None],
             sq_res[4][0][None], g_wpp[None]]

    def triple(i):
        w_in_t = (d_w_in[None], nm_w_in[None], nv_w_in[None])
        sq = lambda k: tuple(a[None] for a in sq_res[k][1:4])
        sm = lambda k, lead: tuple(a[None] if lead else a for a in small[k])
        return [w_in_t[i], sm(2, False)[i], sm(3, False)[i], sm(1, True)[i], sm(4, False)[i],
                sm(5, False)[i], sm(6, False)[i], sq(0)[i], sm(7, False)[i], sq(1)[i], sq(2)[i], sq(3)[i],
                sq(4)[i], sm(0, True)[i]]

    return (loss, gx.reshape(nb, S, D), *grads, *triple(0), *triple(1), *triple(2))
```

```python
import functools

import jax
import jax.numpy as jnp
import numpy as np
from jax import lax
from jax.experimental import pallas as pl
from jax.experimental.pallas import tpu as pltpu

F32 = jnp.float32
BF16 = jnp.bfloat16

D = 1024
PLE = 256
N_HEADS = 16
HEAD_DIM = 64
BLOCK = 128
CONV_K = 31
ROPE_DIM = 16
ROPE_THETA = 500000.0
EPS = 1e-6
IN_WIDTH = 7424
N_SHARDS = 4

ADAM_LR = 0.001
ADAM_B1 = 0.9
ADAM_B2 = 0.999
ADAM_EPS = 1e-08
ADAM_WD = 0.01
ADAM_STEP = 10

SQ_NAMES = ("w_pw", "w_br_conv", "w_br_attn", "w_out", "w_ple_gate")
WT0 = 5 * D
WPP0 = WT0 + IN_WIDTH
WALL_ROWS = WPP0 + PLE
WIN_SHARD = IN_WIDTH // N_SHARDS
SQ_SHARD = D // N_SHARDS
WPP_SHARD = PLE * PLE // D
PACK_ROWS = WIN_SHARD + 5 * SQ_SHARD + WPP_SHARD
HALF_ROWS = PACK_ROWS // 2
VMEM_LIMIT = 56 * 1024 * 1024
MESH = pl.DeviceIdType.MESH
TILE_PROJ = 1024
TILE_TOKEN = 256
TILE_ATTN = 512
TAIL_PARTS = 1


ZB_AGATE, ZB_GCONV, ZB_GATTN, ZB_CGATE, ZB_CVAL, ZB_CGLU, ZB_Q = range(7)
ZKV = 7 * D
_SEGMENTS = ((0, D, ZB_CVAL * D), (D, D, ZB_CGLU * D), (2 * D, D, ZB_CGATE * D), (3 * D, D, ZB_Q * D),
             (4 * D, 2 * BLOCK, ZKV), (4 * D + 2 * BLOCK, D, ZB_AGATE * D),
             (5 * D + 2 * BLOCK, D, ZB_GCONV * D), (6 * D + 2 * BLOCK, D, ZB_GATTN * D))
_WT_CUTS = (0, 192, 640, 1216, WIN_SHARD)


def _zp_row(o):
    for a, w, zp in _SEGMENTS:
        if a <= o < a + w:
            return zp + o - a
    raise ValueError(o)


def _pieces(s):
    out = []
    for a, b in zip(_WT_CUTS[:-1], _WT_CUTS[1:]):
        first = _zp_row(WIN_SHARD * s + a)
        assert _zp_row(WIN_SHARD * s + b - 1) == first + b - a - 1
        out.append((a, b - a, WT0 + first))
    for k in range(5):
        out.append((WIN_SHARD + SQ_SHARD * k, SQ_SHARD, D * k + SQ_SHARD * s))
    out.append((WIN_SHARD + 5 * SQ_SHARD, WPP_SHARD, WPP0 + WPP_SHARD * s))
    return out


N_PIECES = len(_pieces(0))


def _wall_segments(wall0, rows):
    out = []
    for s in range(N_SHARDS):
        for pr, n, wr in _pieces(s):
            lo, hi = max(wr, wall0), min(wr + n, wall0 + rows)
            if lo < hi:
                out.append((lo - wall0, hi - lo, s, pr + lo - wr))
    assert sum(n for _, n, _, _ in out) == rows
    return out


def _sel(s, vals):
    r = jnp.int32(vals[0])
    for i in range(1, len(vals)):
        r = jnp.where(s == i, jnp.int32(vals[i]), r)
    return r


def _sig(x):
    return 1.0 / (1.0 + jnp.exp(-x))


def _mm(a, b):
    return lax.dot_general(a, b, (((1,), (0,)), ((), ())), preferred_element_type=F32)


def _mm_nt(a, b):
    return lax.dot_general(a, b, (((1,), (1,)), ((), ())), preferred_element_type=F32)


def _mm_tn(a, b):
    return lax.dot_general(a, b, (((0,), (0,)), ((), ())), preferred_element_type=F32)


def _params(sem=None):
    return pltpu.CompilerParams(dimension_semantics=sem, vmem_limit_bytes=VMEM_LIMIT)


def _flush_to_pack(acc_ref, gpack_ref, wall0, sem):
    for r, n, s, pr in _wall_segments(wall0, acc_ref.shape[0]):
        cp = pltpu.make_async_copy(acc_ref.at[pl.ds(r, n)], gpack_ref.at[s, pl.ds(pr, n)], sem)
        cp.start()
        cp.wait()


def _coords():
    return lax.axis_index("x"), lax.axis_index("y"), lax.axis_index("c")


def _chip_peers(x, y):
    return [(1 - x, y), (x, 1 - y), (1 - x, 1 - y)]


WIN_PIECES = tuple(range(len(_WT_CUTS) - 1))
SQ_PIECES = tuple(range(len(WIN_PIECES), N_PIECES))


def _gather_ops(group, src, landing, bytes_ref, stage, send_sems, recv_sems, loc_sem):
    sizes = [_pieces(0)[p][1] for p in group]
    half_rows = sum(n // 2 for n in sizes)

    def rcopy(a, b, k, dev):
        return pltpu.make_async_remote_copy(src_ref=a, dst_ref=b, send_sem=send_sems.at[k],
                                            recv_sem=recv_sems.at[k], device_id=dev, device_id_type=MESH)

    def total(k):
        x, y, c = _coords()
        rows = bytes_ref.at[pl.ds(0, half_rows)]
        return rcopy(rows, rows, k, (x, y, c))

    def send():
        x, y, c = _coords()
        s_me = 2 * x + y
        for k, (px, py) in enumerate(_chip_peers(x, y)):
            for p, n in zip(group, sizes):
                h = n // 2
                rcopy(src(p, c * h, h), landing(p, s_me, c * h, h), k, (px, py, c)).start()
        for p, n in zip(group, sizes):
            for a, b in ((src(p, 0, n), stage.at[pl.ds(0, n)]), (stage.at[pl.ds(0, n)], landing(p, s_me, 0, n))):
                cp = pltpu.make_async_copy(a, b, loc_sem)
                cp.start()
                cp.wait()

    def forward():
        x, y, c = _coords()
        for k, (px, py) in enumerate(_chip_peers(x, y)):
            total(k).wait_recv()
            for p, n in zip(group, sizes):
                rows = landing(p, 2 * px + py, c * (n // 2), n // 2)
                rcopy(rows, rows, 3 + k, (x, y, 1 - c)).start()

    def finish():
        for k in range(3):
            total(3 + k).wait_recv()
        for k in range(6):
            total(k).wait_send()

    return send, forward, finish


def _piece_rows(ref, start, off, n):
    first = start + off
    return ref.at[pl.ds(first if isinstance(first, int) else pl.multiple_of(first, 32), n)]


def _gather_win(win_t, wdw_shard):
    tables = [[_pieces(s)[p][2] - WT0 for s in range(N_SHARDS)] for p in WIN_PIECES]

    def body(win_ref, wdw_ref, wt_ref, wdwall_ref, stage, send_sems, recv_sems, loc_sems):
        x, y, c = _coords()
        s_me = 2 * x + y
        send, forward, finish = _gather_ops(
            WIN_PIECES, lambda p, off, n: _piece_rows(win_ref, _WT_CUTS[p], off, n),
            lambda p, s, off, n: _piece_rows(wt_ref, _sel(s, tables[p]), off, n),
            wt_ref, stage, send_sems, recv_sems, loc_sems.at[0])
        own_wdw = pltpu.make_async_copy(wdw_ref, wdwall_ref.at[s_me], loc_sems.at[1])
        own_wdw.start()
        wdw_sends = [pltpu.make_async_remote_copy(
            src_ref=wdw_ref, dst_ref=wdwall_ref.at[s_me], send_sem=send_sems.at[6 + k],
            recv_sem=recv_sems.at[6 + k], device_id=(px, py, c), device_id_type=MESH)
            for k, (px, py) in enumerate(_chip_peers(x, y))]
        for cp in wdw_sends:
            cp.start()
        send()
        forward()
        finish()
        for cp in wdw_sends:
            cp.wait_recv()
        for cp in wdw_sends:
            cp.wait_send()
        own_wdw.wait()

    any_spec = pl.BlockSpec(memory_space=pl.ANY)
    return pl.pallas_call(
        body, name="gather_win",
        out_shape=(jax.ShapeDtypeStruct((IN_WIDTH, D), BF16), jax.ShapeDtypeStruct((N_SHARDS, 32, PLE), F32)),
        in_specs=[any_spec, any_spec], out_specs=(any_spec, any_spec),
        scratch_shapes=[pltpu.VMEM((max(_pieces(0)[p][1] for p in WIN_PIECES), D), BF16),
                        pltpu.SemaphoreType.DMA((9,)), pltpu.SemaphoreType.DMA((9,)),
                        pltpu.SemaphoreType.DMA((2,))],
    )(win_t, wdw_shard)


RT = 320


def _chip_sum(cidx, gpack, r1):
    def body(c_ref, g_ref, r_ref, o_ref):
        o_ref[...] = (g_ref[...] + r_ref[...]).astype(BF16)

    nt = HALF_ROWS // RT
    return pl.pallas_call(
        body, name="chip_sum",
        grid_spec=pltpu.PrefetchScalarGridSpec(
            num_scalar_prefetch=1, grid=(N_SHARDS, nt),
            in_specs=[pl.BlockSpec((1, RT, D), lambda s, t, c: (s, c[0] * nt + t, 0)),
                      pl.BlockSpec((1, RT, D), lambda s, t, c: (s, t, 0))],
            out_specs=pl.BlockSpec((1, RT, D), lambda s, t, c: (s, t, 0))),
        out_shape=jax.ShapeDtypeStruct((N_SHARDS, HALF_ROWS, D), BF16),
        compiler_params=_params(("arbitrary", "arbitrary")),
    )(cidx, gpack, r1)


def _final_half(sc, gpack, r1, r2):
    def body(sc_ref, g_ref, r_ref, p_ref, o_ref):
        acc = g_ref[0] + r_ref[0]
        for k in range(3):
            acc = acc + p_ref[k].astype(F32)
        o_ref[...] = acc

    nt = HALF_ROWS // RT
    return pl.pallas_call(
        body, name="final_half",
        grid_spec=pltpu.PrefetchScalarGridSpec(
            num_scalar_prefetch=1, grid=(nt,),
            in_specs=[pl.BlockSpec((1, RT, D), lambda t, sc: (sc[0], sc[1] * nt + t, 0)),
                      pl.BlockSpec((1, RT, D), lambda t, sc: (sc[0], t, 0)),
                      pl.BlockSpec((3, RT, D), lambda t, sc: (0, t, 0))],
            out_specs=pl.BlockSpec((RT, D), lambda t, sc: (sc[1] * nt + t, 0))),
        out_shape=jax.ShapeDtypeStruct((PACK_ROWS, D), F32),
        compiler_params=_params(("arbitrary",)),
    )(sc, gpack, r1, r2)


def _swap_halves(fh):
    def body(f_ref, o_ref, send_sem, recv_sem):
        x, y, c = _coords()
        rows = pl.ds(pl.multiple_of(c * HALF_ROWS, 32), HALF_ROWS)
        cp = pltpu.make_async_remote_copy(
            src_ref=f_ref.at[rows], dst_ref=o_ref.at[rows],
            send_sem=send_sem, recv_sem=recv_sem, device_id=(x, y, 1 - c), device_id_type=MESH)
        cp.start()
        cp.wait()

    any_spec = pl.BlockSpec(memory_space=pl.ANY)
    return pl.pallas_call(
        body, name="swap_halves",
        out_shape=jax.ShapeDtypeStruct((PACK_ROWS, D), F32),
        in_specs=[any_spec], out_specs=any_spec, input_output_aliases={0: 0},
        scratch_shapes=[pltpu.SemaphoreType.DMA, pltpu.SemaphoreType.DMA],
    )(fh)


VEC_ROWS = 40


def _all_reduce_small(vec):
    def body(v_ref, o_ref, buf, send_sems, recv_sems):
        x, y, c = _coords()
        me = 4 * x + 2 * y + c
        buf[me] = v_ref[...]
        cps = []
        for r in range(1, 8):
            dx, dy, dc = (r >> 2) & 1, (r >> 1) & 1, r & 1
            peer = (1 - x if dx else x, 1 - y if dy else y, 1 - c if dc else c)
            cp = pltpu.make_async_remote_copy(
                src_ref=v_ref, dst_ref=buf.at[me], send_sem=send_sems.at[r - 1],
                recv_sem=recv_sems.at[r - 1], device_id=peer, device_id_type=MESH)
            cp.start()
            cps.append(cp)
        for cp in cps:
            cp.wait_recv()
        for cp in cps:
            cp.wait_send()
        acc = buf[0]
        for d in range(1, 8):
            acc = acc + buf[d]
        o_ref[...] = acc

    vm = pl.BlockSpec(memory_space=pltpu.VMEM)
    return pl.pallas_call(
        body, name="all_reduce_small",
        out_shape=jax.ShapeDtypeStruct((VEC_ROWS, D), F32),
        in_specs=[vm], out_specs=vm,
        scratch_shapes=[pltpu.VMEM((8, VEC_ROWS, D), F32), pltpu.SemaphoreType.DMA((7,)),
                        pltpu.SemaphoreType.DMA((7,))],
    )(vec)


def _inproj(x, ln_pre, wt, sq_shards, wpp_shard, tm):
    T = x.shape[0]
    nsq = len(sq_shards)
    n_i = T // tm

    def body(*refs):
        x_ref, g_ref, w_ref = refs[:3]
        sq_refs = refs[3:3 + nsq]
        wpp_ref = refs[3 + nsq]
        z_ref, h_ref, wsq_ref, wppf_ref, hs, stage, send_sems, recv_sems, loc_sem = refs[4 + nsq:]
        i = pl.program_id(0)
        j = pl.program_id(1)

        def src(p, off, n):
            k = p - SQ_PIECES[0]
            return _piece_rows(wpp_ref if k == nsq else sq_refs[k], 0, off, n)

        def landing(p, s, off, n):
            k = p - SQ_PIECES[0]
            if k == nsq:
                return _piece_rows(wppf_ref, WPP_SHARD * s, off, n)
            return _piece_rows(wsq_ref, D * k + SQ_SHARD * s, off, n)

        send, forward, finish = _gather_ops(SQ_PIECES, src, landing, wsq_ref, stage, send_sems, recv_sems,
                                            loc_sem)

        @pl.when((i == 0) & (j == 0))
        def _():
            send()

        @pl.when((i == n_i // 2) & (j == 0))
        def _():
            forward()

        @pl.when(j == 0)
        def _():
            xv = x_ref[...]
            r = lax.rsqrt(jnp.mean(xv * xv, axis=-1, keepdims=True) + EPS)
            h = (xv * r * g_ref[...]).astype(BF16)
            hs[...] = h
            h_ref[...] = h

        z_ref[...] = _mm_nt(hs[...], w_ref[...]).astype(BF16)

        @pl.when((i == n_i - 1) & (j == 6))
        def _():
            finish()

    any_spec = pl.BlockSpec(memory_space=pl.ANY)
    return pl.pallas_call(
        body, name="inproj", grid=(n_i, 7),
        in_specs=[pl.BlockSpec((tm, D), lambda i, j: (i, 0)),
                  pl.BlockSpec((1, D), lambda i, j: (0, 0)),
                  pl.BlockSpec((D, D), lambda i, j: (j, 0))] + [any_spec] * (nsq + 1),
        out_specs=(pl.BlockSpec((tm, D), lambda i, j: (i, j)),
                   pl.BlockSpec((tm, D), lambda i, j: (i, 0)), any_spec, any_spec),
        out_shape=(jax.ShapeDtypeStruct((T, 7 * D), BF16), jax.ShapeDtypeStruct((T, D), BF16),
                   jax.ShapeDtypeStruct((nsq * D, D), BF16), jax.ShapeDtypeStruct((PLE, D), BF16)),
        scratch_shapes=[pltpu.VMEM((tm, D), BF16), pltpu.VMEM((SQ_SHARD, D), BF16),
                        pltpu.SemaphoreType.DMA((6,)), pltpu.SemaphoreType.DMA((6,)), pltpu.SemaphoreType.DMA],
        compiler_params=_params(("arbitrary", "arbitrary")),
    )(x, ln_pre, wt, *sq_shards, wpp_shard)


def _kvproj(h, wt, tm):
    T = h.shape[0]

    def body(h_ref, w_ref, o_ref):
        o_ref[...] = _mm_nt(h_ref[...], w_ref[...]).astype(BF16)

    return pl.pallas_call(
        body, name="kvproj", grid=(T // tm,),
        in_specs=[pl.BlockSpec((tm, D), lambda i: (i, 0)),
                  pl.BlockSpec((2 * BLOCK, D), lambda i: (ZKV // (2 * BLOCK), 0))],
        out_specs=pl.BlockSpec((tm, 2 * BLOCK), lambda i: (i, 0)),
        out_shape=jax.ShapeDtypeStruct((T, 2 * BLOCK), BF16),
        compiler_params=_params(("arbitrary",)),
    )(h, wt)


HALO = 32
CONV_RC = 64
CONV_LC = 256


def _conv_taps(w_ref, src, r0, lane0, offset_of_tap):
    lanes = pl.ds(lane0, CONV_LC)
    out = None
    for b in range(8):
        taps = [k for k in range(CONV_K) if offset_of_tap(k) % 8 == b]
        if not taps:
            continue
        rows = CONV_RC + (8 if b else 0)
        vb = None
        for k in taps:
            term = w_ref[k:k + 1, lanes] * src[pl.ds(r0 + (offset_of_tap(k) - b), rows), lanes]
            vb = term if vb is None else vb + term
        vb = vb[b:b + CONV_RC] if b else vb
        out = vb if out is None else out + vb
    return out


def _conv_fwd(z, wdw, b_dw, ln_g, ln_b, wall, S, tm):
    T = z.shape[0]
    nt = S // tm
    hb = tm // HALO

    def body(cv_ref, cg_ref, cgate_ref, hcv_ref, hcg_ref, wdw_ref, bdw_ref, lng_ref, lnb_ref, wpw_ref,
             wbrc_ref, ya_ref, y_ref, rstd_ref, pw_ref, ubuf, cbuf):
        t = pl.program_id(1)
        ubuf[HALO:HALO + tm, :] = cv_ref[...].astype(F32) * _sig(cg_ref[...].astype(F32))
        hu = hcv_ref[...].astype(F32) * _sig(hcg_ref[...].astype(F32))
        ubuf[0:HALO, :] = jnp.where(t > 0, hu, 0.0)
        ubuf[HALO + tm:HALO + tm + 8, :] = jnp.zeros((8, D), F32)

        def chunk(ci, carry):
            r0 = pl.multiple_of(ci * CONV_RC, CONV_RC)
            for lg in range(D // CONV_LC):
                acc = _conv_taps(wdw_ref, ubuf, r0, lg * CONV_LC, lambda k: HALO - (CONV_K - 1) + k)
                cbuf[pl.ds(r0, CONV_RC), pl.ds(lg * CONV_LC, CONV_LC)] = acc
            return carry

        lax.fori_loop(0, tm // CONV_RC, chunk, 0)
        cc = cbuf[...] + bdw_ref[...]
        mu = jnp.mean(cc, axis=-1, keepdims=True)
        dd = cc - mu
        rstd = lax.rsqrt(jnp.mean(dd * dd, axis=-1, keepdims=True) + EPS)
        yn = dd * rstd
        y_ref[...] = yn.astype(BF16)
        rstd_ref[...] = rstd
        n = yn * lng_ref[...] + lnb_ref[...]
        s = n * _sig(n)
        pw = _mm(s.astype(BF16), wpw_ref[...])
        pw_ref[...] = pw.astype(BF16)
        gt = cgate_ref[...].astype(F32)
        ya_in = pw * (gt * _sig(gt))
        ya_ref[...] = _mm(ya_in.astype(BF16), wbrc_ref[...]).astype(BF16)

    def row(b, t):
        return b * nt + t

    def halo(b, t):
        return jnp.maximum(row(b, t) * hb - 1, 0)

    vec = pl.BlockSpec((1, D), lambda b, t: (0, 0))
    tile = lambda j: pl.BlockSpec((tm, D), lambda b, t: (row(b, t), j))
    out_tile = pl.BlockSpec((tm, D), lambda b, t: (row(b, t), 0))
    return pl.pallas_call(
        body, name="conv_fwd", grid=(T // S, nt),
        in_specs=[tile(ZB_CVAL), tile(ZB_CGLU), tile(ZB_CGATE),
                  pl.BlockSpec((HALO, D), lambda b, t: (halo(b, t), ZB_CVAL)),
                  pl.BlockSpec((HALO, D), lambda b, t: (halo(b, t), ZB_CGLU)),
                  pl.BlockSpec((32, D), lambda b, t: (0, 0)), vec, vec, vec,
                  pl.BlockSpec((D, D), lambda b, t: (0, 0)),
                  pl.BlockSpec((D, D), lambda b, t: (1, 0))],
        out_specs=(out_tile, out_tile, pl.BlockSpec((tm, 1), lambda b, t: (row(b, t), 0)), out_tile),
        out_shape=(jax.ShapeDtypeStruct((T, D), BF16), jax.ShapeDtypeStruct((T, D), BF16),
                   jax.ShapeDtypeStruct((T, 1), F32), jax.ShapeDtypeStruct((T, D), BF16)),
        scratch_shapes=[pltpu.VMEM((tm + HALO + 8, D), F32), pltpu.VMEM((tm, D), F32)],
        compiler_params=_params(("arbitrary", "arbitrary")),
    )(z, z, z, z, z, wdw, b_dw, ln_g, ln_b, wall, wall)


def _swap_matrix():
    r = lax.broadcasted_iota(jnp.int32, (BLOCK, BLOCK), 0)
    l = lax.broadcasted_iota(jnp.int32, (BLOCK, BLOCK), 1)
    lh = l & (HEAD_DIM - 1)
    half = ROPE_DIM // 2
    hit = ((lh < half) & (r == l + half)) | ((lh >= half) & (lh < ROPE_DIM) & (r == l - half))
    return jnp.where(hit, 1.0, 0.0).astype(BF16)


def _rope(tb, cos, sin, pswap):
    return tb.astype(F32) * cos + _mm(tb, pswap) * sin


def _rope_f32(tv, cos, sin, pswap):
    hi = tv.astype(BF16)
    lo = (tv - hi.astype(F32)).astype(BF16)
    return tv * cos + (_mm(hi, pswap) + _mm(lo, pswap)) * sin


def _kv_variants(kv):
    lane = lax.broadcasted_iota(jnp.int32, kv.shape, 1)
    lo = lane < HEAD_DIM
    sw = pltpu.roll(kv, HEAD_DIM, 1)
    z = jnp.zeros_like(kv)
    g0 = (jnp.where(lo, kv, z).astype(BF16), jnp.where(lo, z, sw).astype(BF16))
    g1 = (jnp.where(lo, sw, z).astype(BF16), jnp.where(lo, z, kv).astype(BF16))
    return (g0, g1)


def _band_mask(nq):
    qi = lax.broadcasted_iota(jnp.int32, (nq * BLOCK, 2 * BLOCK), 0) & (BLOCK - 1)
    sj = lax.broadcasted_iota(jnp.int32, (nq * BLOCK, 2 * BLOCK), 1)
    return (sj <= qi + BLOCK) & (sj > qi), sj


def _sink_rep(sink_ref, g, e):
    return jnp.concatenate(
        [jnp.full((BLOCK, BLOCK), sink_ref[8 * g + 2 * j + e], F32) for j in range(4)], axis=0)


def _softmax_sink(s, valid, sk):
    rows = s.shape[0]
    s = jnp.where(valid, s, -1e30)
    m = jnp.maximum(jnp.broadcast_to(jnp.max(s, axis=-1, keepdims=True), (rows, BLOCK)), sk)
    p = jnp.exp(s - jnp.concatenate([m, m], axis=1))
    ps = jnp.exp(sk - m)
    inv = 1.0 / (_mm(p.astype(BF16), jnp.ones((2 * BLOCK, BLOCK), BF16)) + ps)
    return p * jnp.concatenate([inv, inv], axis=1), ps * inv


def _attn_fwd(z, zkv, cos_t, sin_t, sinks, S, tq):
    T = z.shape[0]
    nt = S // tq
    nq = tq // BLOCK

    def body(sink_ref, q_ref, kv_ref, hkv_ref, cos_ref, sin_ref, hcos_ref, hsin_ref, o_ref):
        t = pl.program_id(1)
        cos = cos_ref[...]
        sin = sin_ref[...]
        pswap = _swap_matrix()
        kv = jnp.concatenate([hkv_ref[...], kv_ref[...]], axis=0)
        cos_k = jnp.concatenate([hcos_ref[...], cos], axis=0)
        sin_k = jnp.concatenate([hsin_ref[...], sin], axis=0)
        kx = _kv_variants(_rope(kv[:, :BLOCK], cos_k, sin_k, pswap))
        vx = _kv_variants(kv[:, BLOCK:].astype(F32))
        band, sj = _band_mask(4)
        qs = [(_rope(q_ref[:, 128 * hp:128 * hp + 128], cos, sin, pswap) * 0.125).astype(BF16)
              for hp in range(8)]
        for n in range(nq):
            first = (t == 0) & (n == 0)
            valid = band & (jnp.logical_not(first) | (sj >= BLOCK))
            r0 = n * BLOCK
            for g in range(2):
                lhs = jnp.concatenate([qs[4 * g + j][r0:r0 + BLOCK] for j in range(4)], axis=0)
                acc = jnp.zeros((4 * BLOCK, BLOCK), F32)
                for e in range(2):
                    s = _mm_nt(lhs, kx[g][e][r0:r0 + 2 * BLOCK])
                    p, _ = _softmax_sink(s, valid, _sink_rep(sink_ref, g, e))
                    acc = acc + _mm(p.astype(BF16), vx[g][e][r0:r0 + 2 * BLOCK])
                for j in range(4):
                    o_ref[r0:r0 + BLOCK, 128 * (4 * g + j):128 * (4 * g + j) + 128] = (
                        acc[j * BLOCK:(j + 1) * BLOCK].astype(BF16))

    def row(b, t):
        return b * nt + t

    def halo(b, t):
        return jnp.maximum(row(b, t) * nq - 1, 0)

    return pl.pallas_call(
        body, name="attn_fwd", grid=(T // S, nt),
        in_specs=[pl.BlockSpec(memory_space=pltpu.SMEM),
                  pl.BlockSpec((tq, D), lambda b, t: (row(b, t), ZB_Q)),
                  pl.BlockSpec((tq, 2 * BLOCK), lambda b, t: (row(b, t), 0)),
                  pl.BlockSpec((BLOCK, 2 * BLOCK), lambda b, t: (halo(b, t), 0)),
                  pl.BlockSpec((tq, BLOCK), lambda b, t: (row(b, t), 0)),
                  pl.BlockSpec((tq, BLOCK), lambda b, t: (row(b, t), 0)),
                  pl.BlockSpec((BLOCK, BLOCK), lambda b, t: (halo(b, t), 0)),
                  pl.BlockSpec((BLOCK, BLOCK), lambda b, t: (halo(b, t), 0))],
        out_specs=pl.BlockSpec((tq, D), lambda b, t: (row(b, t), 0)),
        out_shape=jax.ShapeDtypeStruct((T, D), BF16),
        compiler_params=_params(("arbitrary", "arbitrary")),
    )(sinks, z, zkv, zkv, cos_t, sin_t, cos_t, sin_t)


def _tail_a(x, tgt, p, o, ya, z, ln_post, wall, wppt, tm):
    T = x.shape[0]
    last = T // tm - 1

    def body(x_ref, tgt_ref, p_ref, o_ref, ya_ref, ag_ref, gc_ref, ga_ref, lnp_ref, wbra_ref, wout_ref,
             wpg_ref, wppt_ref, loss_ref, dx1_ref, dm_ref, yb_ref, glnp_ref, gpack_ref, gwpp_ref,
             acc_out, acc_pg, sem):
        i = pl.program_id(0)

        @pl.when(i == 0)
        def _():
            acc_out[...] = jnp.zeros_like(acc_out)
            acc_pg[...] = jnp.zeros_like(acc_pg)
            gwpp_ref[...] = jnp.zeros_like(gwpp_ref)
            glnp_ref[...] = jnp.zeros_like(glnp_ref)
            loss_ref[...] = jnp.zeros_like(loss_ref)

        ag = ag_ref[...].astype(F32)
        yb_in = (o_ref[...].astype(F32) * (ag * _sig(ag))).astype(BF16)
        yb = _mm(yb_in, wbra_ref[...])
        yb_ref[...] = yb.astype(BF16)
        m = (_sig(gc_ref[...].astype(F32)) * ya_ref[...].astype(F32)
             + _sig(ga_ref[...].astype(F32)) * yb).astype(BF16)
        mo = _mm(m, wout_ref[...])
        r2 = lax.rsqrt(jnp.mean(mo * mo, axis=-1, keepdims=True) + EPS)
        nrm = mo * r2
        g_post = lnp_ref[...]
        x1 = x_ref[...] + nrm * g_post
        x1b = x1.astype(BF16)
        gate = _sig(_mm(x1b, wpg_ref[...]))
        pb = p_ref[...].astype(BF16)
        pp = _mm_nt(pb, wppt_ref[...])
        err = x1 + gate * pp - tgt_ref[...]
        loss_ref[...] += 0.5 * jnp.sum(jnp.sum(err * err, axis=-1, keepdims=True) * (1.0 / D),
                                       axis=0, keepdims=True)
        dx2 = err * (1.0 / D)
        dgp = (dx2 * pp * gate * (1.0 - gate)).astype(BF16)
        dpp = (dx2 * gate).astype(BF16)
        dx1 = dx2 + _mm_nt(dgp, wpg_ref[...])
        dx1_ref[...] = dx1
        acc_pg[...] += _mm_tn(x1b, dgp)
        gwpp_ref[...] += _mm_tn(dpp, pb)
        glnp_ref[...] += jnp.sum(dx1 * nrm, axis=0, keepdims=True)
        a = dx1 * g_post
        dmo = (r2 * (a - nrm * jnp.mean(a * nrm, axis=-1, keepdims=True))).astype(BF16)
        dm_ref[...] = _mm_nt(dmo, wout_ref[...]).astype(BF16)
        acc_out[...] += _mm_tn(m, dmo)

        @pl.when(i == last)
        def _():
            _flush_to_pack(acc_out, gpack_ref, 3 * D, sem.at[0])
            _flush_to_pack(acc_pg, gpack_ref, 4 * D, sem.at[1])

    tile = pl.BlockSpec((tm, D), lambda i: (i, 0))
    ztile = lambda j: pl.BlockSpec((tm, D), lambda i: (i, j))
    wsq = lambda k: pl.BlockSpec((D, D), lambda i: (k, 0))
    const = lambda shp: pl.BlockSpec(shp, lambda i: (0, 0))
    any_spec = pl.BlockSpec(memory_space=pl.ANY)
    return pl.pallas_call(
        body, name="tail_a", grid=(T // tm,),
        in_specs=[tile, tile, pl.BlockSpec((tm, PLE), lambda i: (i, 0)), tile, tile, ztile(ZB_AGATE),
                  ztile(ZB_GCONV), ztile(ZB_GATTN), const((1, D)), wsq(2), wsq(3), wsq(4), const((D, PLE))],
        out_specs=(const((1, 1)), tile, tile, tile, const((1, D)), any_spec, const((D, PLE))),
        out_shape=(jax.ShapeDtypeStruct((1, 1), F32), jax.ShapeDtypeStruct((T, D), F32),
                   jax.ShapeDtypeStruct((T, D), BF16), jax.ShapeDtypeStruct((T, D), BF16),
                   jax.ShapeDtypeStruct((1, D), F32), jax.ShapeDtypeStruct((N_SHARDS, PACK_ROWS, D), F32),
                   jax.ShapeDtypeStruct((D, PLE), F32)),
        scratch_shapes=[pltpu.VMEM((D, D), F32), pltpu.VMEM((D, D), F32), pltpu.SemaphoreType.DMA((2,))],
        compiler_params=_params(("arbitrary",)),
    )(x, tgt, p, o, ya, z, z, z, ln_post, wall, wall, wall, wppt)


def _dsilu(v, sg):
    return sg * (1.0 + v * (1.0 - sg))


def _tail_b(dm, ya, yb, o, z, pw, y, rstd, ln_g, ln_b, wall, gpack, tm):
    T = dm.shape[0]
    last = T // tm - 1

    def body(dm_ref, ya_ref, yb_ref, o_ref, ag_ref, gc_ref, ga_ref, cgate_ref, pw_ref, y_ref, rstd_ref,
             lng_ref, lnb_ref, wpw_ref, wbrc_ref, wbra_ref, gpack_in, dg_ref, do_ref, dc_ref, gvec_ref,
             gpack_ref, acc_bra, acc_brc, acc_pw, sem):
        i = pl.program_id(0)

        @pl.when(i == 0)
        def _():
            acc_bra[...] = jnp.zeros_like(acc_bra)
            acc_brc[...] = jnp.zeros_like(acc_brc)
            acc_pw[...] = jnp.zeros_like(acc_pw)
            gvec_ref[...] = jnp.zeros_like(gvec_ref)

        g = lng_ref[...]

        def part(rs):
            dm_v = dm_ref[rs, :].astype(F32)
            sgc = _sig(gc_ref[rs, :].astype(F32))
            sga = _sig(ga_ref[rs, :].astype(F32))
            dya = (dm_v * sgc).astype(BF16)
            dyb = (dm_v * sga).astype(BF16)
            dg_ref[rs, D:2 * D] = (dm_v * ya_ref[rs, :].astype(F32) * sgc * (1.0 - sgc)).astype(BF16)
            dg_ref[rs, 2 * D:3 * D] = (dm_v * yb_ref[rs, :].astype(F32) * sga * (1.0 - sga)).astype(BF16)
            ag = ag_ref[rs, :].astype(F32)
            sag = _sig(ag)
            sa = ag * sag
            ov = o_ref[rs, :].astype(F32)
            dyb_in = _mm_nt(dyb, wbra_ref[...])
            do_ref[rs, :] = (dyb_in * sa).astype(BF16)
            dg_ref[rs, 0:D] = (dyb_in * ov * _dsilu(ag, sag)).astype(BF16)
            gt = cgate_ref[rs, :].astype(F32)
            sgt = _sig(gt)
            sgate = gt * sgt
            pw = pw_ref[rs, :].astype(F32)
            dya_in = _mm_nt(dya, wbrc_ref[...])
            dpw = (dya_in * sgate).astype(BF16)
            dg_ref[rs, 3 * D:4 * D] = (dya_in * pw * _dsilu(gt, sgt)).astype(BF16)
            yn = y_ref[rs, :].astype(F32)
            n = yn * g + lnb_ref[...]
            sn = _sig(n)
            dn = _mm_nt(dpw, wpw_ref[...]) * _dsilu(n, sn)
            dy = dn * g
            dc = rstd_ref[rs, :] * (dy - jnp.mean(dy, axis=-1, keepdims=True)
                                    - yn * jnp.mean(dy * yn, axis=-1, keepdims=True))
            dc_ref[rs, :] = dc.astype(BF16)
            sums = (jnp.sum(dn * yn, axis=0, keepdims=True), jnp.sum(dn, axis=0, keepdims=True),
                    jnp.sum(dc, axis=0, keepdims=True))
            return ((ov * sa).astype(BF16), dyb, (pw * sgate).astype(BF16), dya, (n * sn).astype(BF16), dpw,
                    sums)

        parts = [part(pl.ds(r * (tm // TAIL_PARTS), tm // TAIL_PARTS)) for r in range(TAIL_PARTS)]
        cat = lambda j: jnp.concatenate([pt[j] for pt in parts], axis=0)
        acc_bra[...] += _mm_tn(cat(0), cat(1))
        acc_brc[...] += _mm_tn(cat(2), cat(3))
        acc_pw[...] += _mm_tn(cat(4), cat(5))
        for j in range(3):
            gvec_ref[j:j + 1, :] += sum(pt[6][j] for pt in parts)

        @pl.when(i == last)
        def _():
            _flush_to_pack(acc_pw, gpack_ref, 0, sem.at[0])
            _flush_to_pack(acc_brc, gpack_ref, D, sem.at[1])
            _flush_to_pack(acc_bra, gpack_ref, 2 * D, sem.at[2])

    tile = pl.BlockSpec((tm, D), lambda i: (i, 0))
    ztile = lambda j: pl.BlockSpec((tm, D), lambda i: (i, j))
    wsq = lambda k: pl.BlockSpec((D, D), lambda i: (k, 0))
    const = lambda shp: pl.BlockSpec(shp, lambda i: (0, 0))
    any_spec = pl.BlockSpec(memory_space=pl.ANY)
    return pl.pallas_call(
        body, name="tail_b", grid=(T // tm,),
        in_specs=[tile, tile, tile, tile, ztile(ZB_AGATE), ztile(ZB_GCONV), ztile(ZB_GATTN), ztile(ZB_CGATE),
                  tile, tile, pl.BlockSpec((tm, 1), lambda i: (i, 0)), const((1, D)), const((1, D)), wsq(0),
                  wsq(1), wsq(2), any_spec],
        out_specs=(pl.BlockSpec((tm, 4 * D), lambda i: (i, 0)), tile, tile, const((8, D)), any_spec),
        out_shape=(jax.ShapeDtypeStruct((T, 7 * D), BF16), jax.ShapeDtypeStruct((T, D), BF16),
                   jax.ShapeDtypeStruct((T, D), BF16), jax.ShapeDtypeStruct((8, D), F32),
                   jax.ShapeDtypeStruct(gpack.shape, F32)),
        input_output_aliases={16: 4},
        scratch_shapes=[pltpu.VMEM((D, D), F32), pltpu.VMEM((D, D), F32), pltpu.VMEM((D, D), F32),
                        pltpu.SemaphoreType.DMA((3,))],
        compiler_params=_params(("arbitrary",)),
    )(dm, ya, yb, o, z, z, z, z, pw, y, rstd, ln_g, ln_b, wall, wall, wall, gpack)


def _conv_bwd(dc, z, wdw, dz, S, tm):
    T = dc.shape[0]
    nt = S // tm
    hb = tm // HALO
    nrows = T // HALO

    def body(dc_ref, hdc_ref, cv_ref, cg_ref, hcv_ref, hcg_ref, wdw_ref, dz_in, dz_ref, gw_ref, ubuf, dcbuf,
             dubuf, dwacc, shbuf):
        b = pl.program_id(0)
        t = pl.program_id(1)

        @pl.when((b == 0) & (t == 0))
        def _():
            dwacc[...] = jnp.zeros_like(dwacc)

        cv = cv_ref[...].astype(F32)
        sg = _sig(cg_ref[...].astype(F32))
        ubuf[HALO:HALO + tm, :] = cv * sg
        hu = hcv_ref[...].astype(F32) * _sig(hcg_ref[...].astype(F32))
        ubuf[0:HALO, :] = jnp.where(t > 0, hu, 0.0)
        ubuf[HALO + tm:HALO + tm + 8, :] = jnp.zeros((8, D), F32)
        dcbuf[0:tm, :] = dc_ref[...].astype(F32)
        dcbuf[tm:tm + HALO, :] = jnp.where(t < nt - 1, hdc_ref[...].astype(F32), 0.0)
        dcbuf[tm + HALO:tm + HALO + 8, :] = jnp.zeros((8, D), F32)

        def chunk(ci, carry):
            r0 = pl.multiple_of(ci * CONV_RC, CONV_RC)
            for lg in range(D // CONV_LC):
                l0 = lg * CONV_LC
                dubuf[pl.ds(r0, CONV_RC), pl.ds(l0, CONV_LC)] = _conv_taps(
                    wdw_ref, dcbuf, r0, l0, lambda k: CONV_K - 1 - k)
                dcc = dcbuf[pl.ds(r0, CONV_RC), pl.ds(l0, CONV_LC)]
                zero8 = jnp.zeros((8, CONV_LC), F32)
                dcz = jnp.concatenate([zero8, dcc, zero8], axis=0)
                for bb in range(8):
                    taps = [k for k in range(CONV_K) if (HALO - (CONV_K - 1) + k) % 8 == bb]
                    if not taps:
                        continue
                    rows = CONV_RC + (8 if bb else 0)
                    if bb:
                        shbuf[bb] = dcz[8 - bb:8 - bb + rows]
                    for k in taps:
                        a8 = HALO - (CONV_K - 1) + k - bb
                        dcs = shbuf[bb] if bb else dcc
                        prod = dcs * ubuf[pl.ds(r0 + a8, rows), pl.ds(l0, CONV_LC)]
                        part = prod[0:8]
                        for q in range(1, rows // 8):
                            part = part + prod[8 * q:8 * q + 8]
                        dwacc[8 * k:8 * k + 8, pl.ds(l0, CONV_LC)] += part
            return carry

        lax.fori_loop(0, tm // CONV_RC, chunk, 0)
        du = dubuf[...]
        dz_ref[:, 0:D] = (du * sg).astype(BF16)
        dz_ref[:, D:2 * D] = (du * cv * sg * (1.0 - sg)).astype(BF16)

        @pl.when((b == pl.num_programs(0) - 1) & (t == nt - 1))
        def _():
            for k in range(32):
                gw_ref[k:k + 1, :] = jnp.sum(dwacc[8 * k:8 * k + 8, :], axis=0, keepdims=True)

    def row(b, t):
        return b * nt + t

    def prev_halo(b, t):
        return jnp.maximum(row(b, t) * hb - 1, 0)

    def next_halo(b, t):
        return jnp.minimum((row(b, t) + 1) * hb, nrows - 1)

    return pl.pallas_call(
        body, name="conv_bwd", grid=(T // S, nt),
        in_specs=[pl.BlockSpec((tm, D), lambda b, t: (row(b, t), 0)),
                  pl.BlockSpec((HALO, D), lambda b, t: (next_halo(b, t), 0)),
                  pl.BlockSpec((tm, D), lambda b, t: (row(b, t), ZB_CVAL)),
                  pl.BlockSpec((tm, D), lambda b, t: (row(b, t), ZB_CGLU)),
                  pl.BlockSpec((HALO, D), lambda b, t: (prev_halo(b, t), ZB_CVAL)),
                  pl.BlockSpec((HALO, D), lambda b, t: (prev_halo(b, t), ZB_CGLU)),
                  pl.BlockSpec((32, D), lambda b, t: (0, 0)),
                  pl.BlockSpec(memory_space=pl.ANY)],
        out_specs=(pl.BlockSpec((tm, 2 * D), lambda b, t: (row(b, t), ZB_CVAL // 2)),
                   pl.BlockSpec((32, D), lambda b, t: (0, 0))),
        out_shape=(jax.ShapeDtypeStruct(dz.shape, BF16), jax.ShapeDtypeStruct((32, D), F32)),
        input_output_aliases={7: 0},
        scratch_shapes=[pltpu.VMEM((tm + HALO + 8, D), F32), pltpu.VMEM((tm + HALO + 8, D), F32),
                        pltpu.VMEM((tm, D), F32), pltpu.VMEM((8 * 32, D), F32),
                        pltpu.VMEM((8, CONV_RC + 8, CONV_LC), F32)],
        compiler_params=_params(("arbitrary", "arbitrary")),
    )(dc, dc, z, z, z, z, wdw, dz)


def _attn_bwd(z, zkv, o, do, cos_t, sin_t, sinks, dz, S, tq):
    T = z.shape[0]
    nt = S // tq
    nq = tq // BLOCK

    def body(sink_ref, q_ref, kv_ref, hkv_ref, o_ref, do_ref, cos_ref, sin_ref, hcos_ref, hsin_ref, dz_in,
             dq_ref, dkv_ref, gs_ref, carry, dkacc, dvacc):
        b = pl.program_id(0)
        tt = pl.program_id(1)
        t = nt - 1 - tt

        @pl.when((b == 0) & (tt == 0))
        def _():
            gs_ref[...] = jnp.zeros_like(gs_ref)

        @pl.when(tt == 0)
        def _():
            carry[...] = jnp.zeros_like(carry)

        cos = cos_ref[...]
        sin = sin_ref[...]
        pswap = _swap_matrix()
        kv = jnp.concatenate([hkv_ref[...], kv_ref[...]], axis=0)
        cos_k = jnp.concatenate([hcos_ref[...], cos], axis=0)
        sin_k = jnp.concatenate([hsin_ref[...], sin], axis=0)
        kx = _kv_variants(_rope(kv[:, :BLOCK], cos_k, sin_k, pswap))
        vx = _kv_variants(kv[:, BLOCK:].astype(F32))
        band, sj = _band_mask(4)
        lo = lax.broadcasted_iota(jnp.int32, (4 * BLOCK, BLOCK), 1) < HEAD_DIM
        ones = jnp.ones((2 * BLOCK, 2 * BLOCK), BF16)
        qs = [(_rope(q_ref[:, 128 * hp:128 * hp + 128], cos, sin, pswap) * 0.125).astype(BF16)
              for hp in range(8)]
        dkacc[...] = jnp.zeros_like(dkacc)
        dvacc[...] = jnp.zeros_like(dvacc)
        gsum = jnp.zeros((1, BLOCK), F32)
        hlane = lax.broadcasted_iota(jnp.int32, (1, BLOCK), 1)
        for n in range(nq):
            first = (t == 0) & (n == 0)
            valid = band & (jnp.logical_not(first) | (sj >= BLOCK))
            r0 = n * BLOCK
            for g in range(2):
                cols = [slice(128 * (4 * g + j), 128 * (4 * g + j) + 128) for j in range(4)]
                lhs = jnp.concatenate([qs[4 * g + j][r0:r0 + BLOCK] for j in range(4)], axis=0)
                dov = jnp.concatenate([do_ref[r0:r0 + BLOCK, cs] for cs in cols], axis=0)
                prod = dov.astype(F32) * jnp.concatenate(
                    [o_ref[r0:r0 + BLOCK, cs] for cs in cols], axis=0).astype(F32)
                lhs_t = lhs.T
                dov_t = dov.T
                dq = jnp.zeros((4 * BLOCK, BLOCK), F32)
                dk_t = jnp.zeros((HEAD_DIM, 2 * BLOCK), F32)
                dv_t = jnp.zeros((HEAD_DIM, 2 * BLOCK), F32)
                for e in range(2):
                    kw = kx[g][e][r0:r0 + 2 * BLOCK]
                    vw = vx[g][e][r0:r0 + 2 * BLOCK]
                    s = _mm_nt(lhs, kw)
                    p, psink = _softmax_sink(s, valid, _sink_rep(sink_ref, g, e))
                    pe = jnp.where(lo if e == 0 else jnp.logical_not(lo), prod, 0.0)
                    pe_hi = pe.astype(BF16)
                    pe_lo = (pe - pe_hi.astype(F32)).astype(BF16)
                    delta = _mm(jnp.concatenate([pe_hi, pe_lo], axis=1), ones)
                    ds = (p * (_mm_nt(dov, vw) - delta)).astype(BF16)
                    dq = dq + _mm(ds, kw)
                    dims = slice(HEAD_DIM * e, HEAD_DIM * (e + 1))
                    dk_t = dk_t + _mm(lhs_t[dims], ds)
                    dv_t = dv_t + _mm(dov_t[dims], p.astype(BF16))
                    gs = -psink * delta[:, 0:BLOCK]
                    for j in range(4):
                        tot = jnp.sum(gs[j * BLOCK:(j + 1) * BLOCK], axis=0, keepdims=True)
                        gsum = gsum + jnp.where(hlane == 8 * g + 2 * j + e, tot, 0.0)
                dkacc[HEAD_DIM * g:HEAD_DIM * (g + 1), r0:r0 + 2 * BLOCK] += dk_t
                dvacc[HEAD_DIM * g:HEAD_DIM * (g + 1), r0:r0 + 2 * BLOCK] += dv_t
                for j in range(4):
                    dqj = _rope_f32(dq[j * BLOCK:(j + 1) * BLOCK] * 0.125, cos[r0:r0 + BLOCK],
                                    -sin[r0:r0 + BLOCK], pswap)
                    dq_ref[r0:r0 + BLOCK, cols[j]] = dqj.astype(BF16)
        gs_ref[0:1, :] += gsum
        dk_all = dkacc[...]
        dv_all = dvacc[...]
        dk_last = dk_all[:, tq:tq + BLOCK] + carry[0:BLOCK, :]
        dv_last = dv_all[:, tq:tq + BLOCK] + carry[BLOCK:2 * BLOCK, :]
        carry[0:BLOCK, :] = dk_all[:, 0:BLOCK]
        carry[BLOCK:2 * BLOCK, :] = dv_all[:, 0:BLOCK]
        if nq > 1:
            dk_tile = jnp.concatenate([dk_all[:, BLOCK:tq], dk_last], axis=1)
            dv_tile = jnp.concatenate([dv_all[:, BLOCK:tq], dv_last], axis=1)
        else:
            dk_tile, dv_tile = dk_last, dv_last
        dkv_ref[:, 0:BLOCK] = _rope_f32(dk_tile.T, cos, -sin, pswap).astype(BF16)
        dkv_ref[:, BLOCK:2 * BLOCK] = dv_tile.T.astype(BF16)

    def row(b, tt):
        return b * nt + (nt - 1 - tt)

    def halo(b, tt):
        return jnp.maximum(row(b, tt) * nq - 1, 0)

    tile = pl.BlockSpec((tq, D), lambda b, tt: (row(b, tt), 0))
    return pl.pallas_call(
        body, name="attn_bwd", grid=(T // S, nt),
        in_specs=[pl.BlockSpec(memory_space=pltpu.SMEM),
                  pl.BlockSpec((tq, D), lambda b, tt: (row(b, tt), ZB_Q)),
                  pl.BlockSpec((tq, 2 * BLOCK), lambda b, tt: (row(b, tt), 0)),
                  pl.BlockSpec((BLOCK, 2 * BLOCK), lambda b, tt: (halo(b, tt), 0)),
                  tile, tile,
                  pl.BlockSpec((tq, BLOCK), lambda b, tt: (row(b, tt), 0)),
                  pl.BlockSpec((tq, BLOCK), lambda b, tt: (row(b, tt), 0)),
                  pl.BlockSpec((BLOCK, BLOCK), lambda b, tt: (halo(b, tt), 0)),
                  pl.BlockSpec((BLOCK, BLOCK), lambda b, tt: (halo(b, tt), 0)),
                  pl.BlockSpec(memory_space=pl.ANY)],
        out_specs=(pl.BlockSpec((tq, D), lambda b, tt: (row(b, tt), ZB_Q)),
                   pl.BlockSpec((tq, 2 * BLOCK), lambda b, tt: (row(b, tt), 0)),
                   pl.BlockSpec((8, BLOCK), lambda b, tt: (0, 0))),
        out_shape=(jax.ShapeDtypeStruct(dz.shape, BF16), jax.ShapeDtypeStruct((T, 2 * BLOCK), BF16),
                   jax.ShapeDtypeStruct((8, BLOCK), F32)),
        input_output_aliases={10: 0},
        scratch_shapes=[pltpu.VMEM((2 * BLOCK, BLOCK), F32), pltpu.VMEM((BLOCK, tq + BLOCK), F32),
                        pltpu.VMEM((BLOCK, tq + BLOCK), F32)],
        compiler_params=_params(("arbitrary", "arbitrary")),
    )(sinks, z, zkv, zkv, o, do, cos_t, sin_t, cos_t, sin_t, dz)


def _exchange_copies(g_ref, r1_ref, send_sems, recv_sems):
    x, y, c = _coords()
    return [pltpu.make_async_remote_copy(
        src_ref=g_ref.at[:, pl.ds(pl.multiple_of((1 - c) * HALF_ROWS, 32), HALF_ROWS), :], dst_ref=r1_ref,
        send_sem=send_sems.at[0], recv_sem=recv_sems.at[0], device_id=(x, y, 1 - c), device_id_type=MESH)]


def _chip_sum_copies(cs_ref, r2_ref, send_sems, recv_sems):
    x, y, c = _coords()
    return [pltpu.make_async_remote_copy(
        src_ref=cs_ref.at[2 * px + py], dst_ref=r2_ref.at[k], send_sem=send_sems.at[k],
        recv_sem=recv_sems.at[k], device_id=(px, py, c), device_id_type=MESH)
        for k, (px, py) in enumerate(_chip_peers(x, y))]


def _dh(dz, dz_kv, wall, x, dx1, ln_pre, tm, tile0, ntiles, gx_prev, name, copies, src, landing):
    T = x.shape[0]
    nsem = 3

    def body(*refs):
        dz_ref, kv_ref, w_ref, wkv_ref, x_ref, dx1_ref, g_ref, src_ref = refs[:8]
        gx_ref, glp_ref, land_ref, acc, send_sems, recv_sems = refs[-6:]
        i = pl.program_id(0)
        k = pl.program_id(1)

        @pl.when((i == 0) & (k == 0))
        def _():
            glp_ref[...] = jnp.zeros_like(glp_ref)
            for cp in copies(src_ref, land_ref, send_sems, recv_sems):
                cp.start()

        @pl.when(k == 0)
        def _():
            acc[...] = _mm(dz_ref[...], w_ref[...])

        @pl.when((k > 0) & (k < 7))
        def _():
            acc[...] += _mm(dz_ref[...], w_ref[...])

        @pl.when(k == 7)
        def _():
            dh = acc[...] + _mm(kv_ref[...], wkv_ref[...])
            xv = x_ref[...]
            r = lax.rsqrt(jnp.mean(xv * xv, axis=-1, keepdims=True) + EPS)
            xr = xv * r
            glp_ref[...] += jnp.sum(dh * xr, axis=0, keepdims=True)
            a = dh * g_ref[...]
            gx_ref[...] = dx1_ref[...] + r * (a - xr * jnp.mean(a * xr, axis=-1, keepdims=True))

        @pl.when((i == ntiles - 1) & (k == 7))
        def _():
            cps = copies(src_ref, land_ref, send_sems, recv_sems)
            for cp in cps:
                cp.wait_recv()
            for cp in cps:
                cp.wait_send()

    tile = pl.BlockSpec((tm, D), lambda i, k: (tile0 + i, 0))
    any_spec = pl.BlockSpec(memory_space=pl.ANY)
    operands = [dz, dz_kv, wall, wall, x, dx1, ln_pre, src] + ([] if gx_prev is None else [gx_prev])
    return pl.pallas_call(
        body, name=name, grid=(ntiles, 8),
        in_specs=[pl.BlockSpec((tm, D), lambda i, k: (tile0 + i, jnp.minimum(k, 6))),
                  pl.BlockSpec((tm, 2 * BLOCK), lambda i, k: (tile0 + i, 0)),
                  pl.BlockSpec((D, D), lambda i, k: (jnp.minimum(k, 6), 0)),
                  pl.BlockSpec((2 * BLOCK, D), lambda i, k: (ZKV // (2 * BLOCK), 0)),
                  tile, tile, pl.BlockSpec((1, D), lambda i, k: (0, 0)), any_spec]
        + ([] if gx_prev is None else [any_spec]),
        out_specs=(tile, pl.BlockSpec((1, D), lambda i, k: (0, 0)), any_spec),
        out_shape=(jax.ShapeDtypeStruct((T, D), F32), jax.ShapeDtypeStruct((1, D), F32), landing),
        input_output_aliases={} if gx_prev is None else {8: 0},
        scratch_shapes=[pltpu.VMEM((tm, D), F32), pltpu.SemaphoreType.DMA((nsem,)),
                        pltpu.SemaphoreType.DMA((nsem,))],
        compiler_params=_params(("arbitrary", "arbitrary")),
    )(*operands)


def _gwt(dz, h, gpack, tt):
    T = dz.shape[0]
    last = T // tt - 1

    def body(dz_ref, h_ref, gpack_in, gpack_ref, acc, sem):
        j = pl.program_id(0)
        t = pl.program_id(1)

        @pl.when(t == 0)
        def _():
            acc[...] = _mm_tn(dz_ref[...], h_ref[...])

        @pl.when(t > 0)
        def _():
            acc[...] += _mm_tn(dz_ref[...], h_ref[...])

        for jj in range(7):
            @pl.when((t == last) & (j == jj))
            def _(jj=jj):
                _flush_to_pack(acc, gpack_ref, WT0 + jj * D, sem)

    any_spec = pl.BlockSpec(memory_space=pl.ANY)
    return pl.pallas_call(
        body, name="gwt", grid=(7, T // tt),
        in_specs=[pl.BlockSpec((tt, D), lambda j, t: (t, j)), pl.BlockSpec((tt, D), lambda j, t: (t, 0)),
                  any_spec],
        out_specs=any_spec, out_shape=jax.ShapeDtypeStruct(gpack.shape, F32), input_output_aliases={2: 0},
        scratch_shapes=[pltpu.VMEM((D, D), F32), pltpu.SemaphoreType.DMA],
        compiler_params=_params(("arbitrary", "arbitrary")),
    )(dz, h, gpack)


def _gwt_kv(dz_kv, h, gppt, gpack, tt):
    T = dz_kv.shape[0]
    last = T // tt - 1

    def body(dz_ref, h_ref, gppt_ref, gpack_in, gpack_ref, acc, sem):
        t = pl.program_id(0)

        @pl.when(t == 0)
        def _():
            acc[...] = _mm_tn(dz_ref[...], h_ref[...])

        @pl.when(t > 0)
        def _():
            acc[...] += _mm_tn(dz_ref[...], h_ref[...])

        @pl.when(t == last)
        def _():
            _flush_to_pack(acc, gpack_ref, WT0 + ZKV, sem)
            _flush_to_pack(gppt_ref, gpack_ref, WPP0, sem)

    any_spec = pl.BlockSpec(memory_space=pl.ANY)
    return pl.pallas_call(
        body, name="gwt_kv", grid=(T // tt,),
        in_specs=[pl.BlockSpec((tt, 2 * BLOCK), lambda t: (t, 0)), pl.BlockSpec((tt, D), lambda t: (t, 0)),
                  pl.BlockSpec((PLE, D), lambda t: (0, 0)), any_spec],
        out_specs=any_spec, out_shape=jax.ShapeDtypeStruct(gpack.shape, F32), input_output_aliases={3: 0},
        scratch_shapes=[pltpu.VMEM((2 * BLOCK, D), F32), pltpu.SemaphoreType.DMA],
        compiler_params=_params(("arbitrary",)),
    )(dz_kv, h, gppt, gpack)


_BC1 = 1.0 - ADAM_B1 ** ADAM_STEP
_BC2 = 1.0 - ADAM_B2 ** ADAM_STEP


def _adamw_math(w, g, m, v):
    m = ADAM_B1 * m + (1.0 - ADAM_B1) * g
    v = ADAM_B2 * v + (1.0 - ADAM_B2) * (g * g)
    delta = -ADAM_LR * ((m / _BC1) / (jnp.sqrt(v / _BC2) + ADAM_EPS) + ADAM_WD * w)
    return delta, m, v


def _adamw_rows(g, w, m, v, rows, name):
    R, C = w.shape

    def body(g_ref, w_ref, m_ref, v_ref, go_ref, d_ref, nm_ref, nv_ref):
        gv = g_ref[...]
        d, nm, nv = _adamw_math(w_ref[...], gv, m_ref[...], v_ref[...])
        go_ref[...] = gv
        d_ref[...] = d
        nm_ref[...] = nm
        nv_ref[...] = nv

    spec = pl.BlockSpec((rows, C), lambda i: (i, 0))
    shp = jax.ShapeDtypeStruct((R, C), F32)
    return pl.pallas_call(
        body, name=name, grid=(R // rows,), in_specs=[spec] * 4, out_specs=(spec,) * 4,
        out_shape=(shp,) * 4, compiler_params=_params(("arbitrary",)),
    )(g, w, m, v)


def _adamw_square(gfin, ws, ms, vs):
    rb = 64
    nb = SQ_SHARD // rb

    def body(*refs):
        g_refs = refs[0:5]
        w_refs, m_refs, v_refs = refs[5:10], refs[10:15], refs[15:20]
        outs = refs[20:]
        for k in range(5):
            gk = g_refs[k][...]
            d, nm, nv = _adamw_math(w_refs[k][...], gk, m_refs[k][...], v_refs[k][...])
            outs[4 * k][...] = gk
            outs[4 * k + 1][...] = d
            outs[4 * k + 2][...] = nm
            outs[4 * k + 3][...] = nv

    spec = pl.BlockSpec((rb, D), lambda i: (i, 0))
    gspecs = [pl.BlockSpec((rb, D), lambda i, k=k: ((WIN_SHARD + SQ_SHARD * k) // rb + i, 0))
              for k in range(5)]
    shp = jax.ShapeDtypeStruct((SQ_SHARD, D), F32)
    res = pl.pallas_call(
        body, name="adamw_square", grid=(nb,), in_specs=gspecs + [spec] * 15, out_specs=(spec,) * 20,
        out_shape=(shp,) * 20, compiler_params=_params(("arbitrary",)),
    )(*([gfin] * 5), *ws, *ms, *vs)
    return [tuple(res[4 * k:4 * k + 4]) for k in range(5)]


def _adamw_small(gs, ws, ms, vs):
    n = len(gs)

    def body(*refs):
        outs = refs[4 * n:]
        for k in range(n):
            d, nm, nv = _adamw_math(refs[n + k][...], refs[k][...], refs[2 * n + k][...],
                                    refs[3 * n + k][...])
            outs[3 * k][...] = d
            outs[3 * k + 1][...] = nm
            outs[3 * k + 2][...] = nv

    vm = pl.BlockSpec(memory_space=pltpu.VMEM)
    shapes = []
    for w in ws:
        shapes += [jax.ShapeDtypeStruct(w.shape, F32)] * 3
    res = pl.pallas_call(
        body, name="adamw_small", in_specs=[vm] * (4 * n), out_specs=(vm,) * (3 * n),
        out_shape=tuple(shapes),
    )(*gs, *ws, *ms, *vs)
    return [tuple(res[3 * k:3 * k + 3]) for k in range(n)]


def _rope_tables(positions):
    inv = jnp.power(ROPE_THETA, -jnp.arange(0, ROPE_DIM, 2, dtype=F32) / ROPE_DIM)
    inv_h = jnp.concatenate([inv, inv, jnp.zeros((HEAD_DIM - ROPE_DIM,), F32)])
    sign_h = np.array([-1.0] * (ROPE_DIM // 2) + [1.0] * (ROPE_DIM // 2) + [0.0] * (HEAD_DIM - ROPE_DIM),
                      np.float32)
    ang = positions.astype(F32).reshape(-1, 1) * jnp.concatenate([inv_h, inv_h])[None, :]
    return jnp.cos(ang), jnp.sin(ang) * np.concatenate([sign_h, sign_h])[None, :]


def kernel(x, p, positions, w_in, ln_pre, ln_post, w_dw, b_dw, conv_ln_g, conv_ln_b, w_pw, sinks, w_br_conv, w_br_attn, w_out, w_ple_gate, w_ple_proj, loss_target, m_w_in, m_ln_pre, m_ln_post, m_w_dw, m_b_dw, m_conv_ln_g, m_conv_ln_b, m_w_pw, m_sinks, m_w_br_conv, m_w_br_attn, m_w_out, m_w_ple_gate, m_w_ple_proj, v_w_in, v_ln_pre, v_ln_post, v_w_dw, v_b_dw, v_conv_ln_g, v_conv_ln_b, v_w_pw, v_sinks, v_w_br_conv, v_w_br_attn, v_w_out, v_w_ple_gate, v_w_ple_proj):
    nb, S, _ = x.shape
    T = nb * S
    xc = lax.axis_index("x")
    yc = lax.axis_index("y")
    cc = lax.axis_index("c")
    shard = 2 * xc + yc

    sq_w = (w_pw, w_br_conv, w_br_attn, w_out, w_ple_gate)
    wdw_shard = jnp.pad(w_dw[0], ((0, 1), (0, 0)))
    wt, wdw_all = _gather_win(w_in[0].T.astype(BF16), wdw_shard)
    wdw = jnp.concatenate([wdw_all[s] for s in range(N_SHARDS)], axis=1)

    x2 = x.reshape(T, D)
    tgt = loss_target.reshape(T, D)
    p2 = p.reshape(T, PLE)
    cos_t, sin_t = _rope_tables(positions)
    sinks1 = sinks.reshape(N_HEADS)

    tm_big = min(TILE_PROJ, T)
    tm = min(TILE_TOKEN, S)
    tq = min(TILE_ATTN, S)

    z, h, wall, wppf = _inproj(x2, ln_pre, wt, [w[0].astype(BF16) for w in sq_w],
                               w_ple_proj[0].T.reshape(WPP_SHARD, D).astype(BF16), tm_big)
    wppt = wppf.reshape(D, PLE)
    zkv = _kvproj(h, wt, tm_big)
    ya, y, rstd, pw = _conv_fwd(z, wdw, b_dw, conv_ln_g, conv_ln_b, wall, S, tm)
    o = _attn_fwd(z, zkv, cos_t, sin_t, sinks1, S, tq)
    loss_p, dx1, dm, yb, g_ln_post, gpack, gw_ppt = _tail_a(x2, tgt, p2, o, ya, z, ln_post, wall, wppt, tm)

    dz, do, dc, gvec, gpack = _tail_b(dm, ya, yb, o, z, pw, y, rstd, conv_ln_g, conv_ln_b, wall, gpack, tm)
    dz, g_wdw = _conv_bwd(dc, z, wdw, dz, S, tm)
    dz, dkv, g_sinks = _attn_bwd(z, zkv, o, do, cos_t, sin_t, sinks1, dz, S, tq)
    gpack = _gwt(dz, h, gpack, tm_big)
    gpack = _gwt_kv(dkv, h, gw_ppt.reshape(PLE, D), gpack, tm_big)

    cidx = jnp.reshape(cc, (1,)).astype(jnp.int32)
    scidx = jnp.stack([shard, cc]).astype(jnp.int32)
    tm_dh = min(tm_big, T // 2)
    n_dh = T // tm_dh
    n_a = max(1, n_dh // 4)
    gx, g_ln_pre_a, r1 = _dh(
        dz, dkv, wt, x2, dx1, ln_pre, tm_dh, 0, n_a, None, "dh_exchange", _exchange_copies, gpack,
        jax.ShapeDtypeStruct((N_SHARDS, HALF_ROWS, D), F32))
    cs = _chip_sum(cidx, gpack, r1)
    gx, g_ln_pre_b, r2 = _dh(
        dz, dkv, wt, x2, dx1, ln_pre, tm_dh, n_a, n_dh - n_a, gx, "dh_send", _chip_sum_copies, cs,
        jax.ShapeDtypeStruct((3, HALF_ROWS, D), BF16))
    g_ln_pre = g_ln_pre_a + g_ln_pre_b
    gfin = _swap_halves(_final_half(scidx, gpack, r1, r2))

    row37 = jnp.concatenate([g_sinks[0:1, 0:N_HEADS], loss_p, jnp.zeros((1, D - N_HEADS - 1), F32)], axis=1)
    vec = jnp.concatenate([g_wdw, g_ln_pre, g_ln_post, gvec[2:3], gvec[0:1], gvec[1:2], row37,
                           jnp.zeros((VEC_ROWS - 38, D), F32)], axis=0)
    tot = _all_reduce_small(vec)

    g_w_in, d_w_in, nm_w_in, nv_w_in = [a.T for a in _adamw_rows(
        gfin, w_in[0].T, m_w_in[0].T, v_w_in[0].T, WIN_SHARD // 8, "adamw_w_in")]
    g_w_in = g_w_in[None]
    sq_m = (m_w_pw, m_w_br_conv, m_w_br_attn, m_w_out, m_w_ple_gate)
    sq_v = (v_w_pw, v_w_br_conv, v_w_br_attn, v_w_out, v_w_ple_gate)
    sq_res = _adamw_square(gfin, [w[0] for w in sq_w], [m[0] for m in sq_m], [v[0] for v in sq_v])
    g_wpp = gfin[WIN_SHARD + 5 * SQ_SHARD:PACK_ROWS].reshape(PLE, PLE).T
    g_dw_all = tot[0:CONV_K]
    g_dw = lax.dynamic_slice_in_dim(g_dw_all, shard * PLE, PLE, axis=1)
    small_g = [g_wpp, g_dw, tot[32:33], tot[33:34], tot[34:35], tot[35:36], tot[36:37],
               tot[37:38, 0:N_HEADS]]
    small_w = [w_ple_proj[0], w_dw[0], ln_pre, ln_post, b_dw, conv_ln_g, conv_ln_b, sinks]
    small_m = [m_w_ple_proj[0], m_w_dw[0], m_ln_pre, m_ln_post, m_b_dw, m_conv_ln_g, m_conv_ln_b, m_sinks]
    small_v = [v_w_ple_proj[0], v_w_dw[0], v_ln_pre, v_ln_post, v_b_dw, v_conv_ln_g, v_conv_ln_b, v_sinks]
    small = _adamw_small(small_g, small_w, small_m, small_v)

    loss = tot[37, N_HEADS]
    grads = [g_w_in, small_g[2], small_g[3], g_dw[None], small_g[4], small_g[5], small_g[6],
             sq_res[0][0][None], small_g[7], sq_res[1][0][None], sq_res[2][0][None], sq_res[3][0][None],
             sq_res[4][0][None], g_wpp[None]]

    def triple(i):
        w_in_t = (d_w_in[None], nm_w_in[None], nv_w_in[None])
        sq = lambda k: tuple(a[None] for a in sq_res[k][1:4])
        sm = lambda k, lead: tuple(a[None] if lead else a for a in small[k])
        return [w_in_t[i], sm(2, False)[i], sm(3, False)[i], sm(1, True)[i], sm(4, False)[i],
                sm(5, False)[i], sm(6, False)[i], sq(0)[i], sm(7, False)[i], sq(1)[i], sq(2)[i], sq(3)[i],
                sq(4)[i], sm(0, True)[i]]

    return (loss, gx.reshape(nb, S, D), *grads, *triple(0), *triple(1), *triple(2))
```

```python
import functools

import jax
import jax.numpy as jnp
import numpy as np
from jax import lax
from jax.experimental import pallas as pl
from jax.experimental.pallas import tpu as pltpu

F32 = jnp.float32
BF16 = jnp.bfloat16

D = 1024
PLE = 256
N_HEADS = 16
HEAD_DIM = 64
BLOCK = 128
CONV_K = 31
ROPE_DIM = 16
ROPE_THETA = 500000.0
EPS = 1e-6
IN_WIDTH = 7424
N_SHARDS = 4

ADAM_LR = 0.001
ADAM_B1 = 0.9
ADAM_B2 = 0.999
ADAM_EPS = 1e-08
ADAM_WD = 0.01
ADAM_STEP = 10

SQ_NAMES = ("w_pw", "w_br_conv", "w_br_attn", "w_out", "w_ple_gate")
WT0 = 5 * D
WPP0 = WT0 + IN_WIDTH
WALL_ROWS = WPP0 + PLE
WIN_SHARD = IN_WIDTH // N_SHARDS
SQ_SHARD = D // N_SHARDS
WPP_SHARD = PLE * PLE // D
PACK_ROWS = WIN_SHARD + 5 * SQ_SHARD + WPP_SHARD
HALF_ROWS = PACK_ROWS // 2
VMEM_LIMIT = 56 * 1024 * 1024
MESH = pl.DeviceIdType.MESH
TILE_RESIDENT = 512
TILE_PROJ = 1024
TILE_TOKEN = 256
TILE_ATTN = 512
TAIL_PARTS = 1


ZB_AGATE, ZB_GCONV, ZB_GATTN, ZB_CGATE, ZB_CVAL, ZB_CGLU, ZB_Q = range(7)
ZKV = 7 * D
_SEGMENTS = ((0, D, ZB_CVAL * D), (D, D, ZB_CGLU * D), (2 * D, D, ZB_CGATE * D), (3 * D, D, ZB_Q * D),
             (4 * D, 2 * BLOCK, ZKV), (4 * D + 2 * BLOCK, D, ZB_AGATE * D),
             (5 * D + 2 * BLOCK, D, ZB_GCONV * D), (6 * D + 2 * BLOCK, D, ZB_GATTN * D))
_WT_CUTS = (0, 192, 640, 1216, WIN_SHARD)


def _zp_row(o):
    for a, w, zp in _SEGMENTS:
        if a <= o < a + w:
            return zp + o - a
    raise ValueError(o)


def _pieces(s):
    out = []
    for a, b in zip(_WT_CUTS[:-1], _WT_CUTS[1:]):
        first = _zp_row(WIN_SHARD * s + a)
        assert _zp_row(WIN_SHARD * s + b - 1) == first + b - a - 1
        out.append((a, b - a, WT0 + first))
    for k in range(5):
        out.append((WIN_SHARD + SQ_SHARD * k, SQ_SHARD, D * k + SQ_SHARD * s))
    out.append((WIN_SHARD + 5 * SQ_SHARD, WPP_SHARD, WPP0 + WPP_SHARD * s))
    return out


N_PIECES = len(_pieces(0))


def _wall_segments(wall0, rows):
    out = []
    for s in range(N_SHARDS):
        for pr, n, wr in _pieces(s):
            lo, hi = max(wr, wall0), min(wr + n, wall0 + rows)
            if lo < hi:
                out.append((lo - wall0, hi - lo, s, pr + lo - wr))
    assert sum(n for _, n, _, _ in out) == rows
    return out


def _sel(s, vals):
    r = jnp.int32(vals[0])
    for i in range(1, len(vals)):
        r = jnp.where(s == i, jnp.int32(vals[i]), r)
    return r


def _sig(x):
    return 1.0 / (1.0 + jnp.exp(-x))


def _mm(a, b):
    return lax.dot_general(a, b, (((1,), (0,)), ((), ())), preferred_element_type=F32)


def _mm_nt(a, b):
    return lax.dot_general(a, b, (((1,), (1,)), ((), ())), preferred_element_type=F32)


def _mm_tn(a, b):
    return lax.dot_general(a, b, (((0,), (0,)), ((), ())), preferred_element_type=F32)


def _params(sem=None):
    return pltpu.CompilerParams(dimension_semantics=sem, vmem_limit_bytes=VMEM_LIMIT)


def _flush_to_pack(acc_ref, gpack_ref, wall0, sem):
    for r, n, s, pr in _wall_segments(wall0, acc_ref.shape[0]):
        cp = pltpu.make_async_copy(acc_ref.at[pl.ds(r, n)], gpack_ref.at[s, pl.ds(pr, n)], sem)
        cp.start()
        cp.wait()


def _coords():
    return lax.axis_index("x"), lax.axis_index("y"), lax.axis_index("c")


def _chip_peers(x, y):
    return [(1 - x, y), (x, 1 - y), (1 - x, 1 - y)]


WIN_PIECES = tuple(range(len(_WT_CUTS) - 1))
SQ_PIECES = tuple(range(len(WIN_PIECES), N_PIECES))


def _gather_ops(group, src, landing, bytes_ref, stage, send_sems, recv_sems, loc_sem):
    sizes = [_pieces(0)[p][1] for p in group]
    half_rows = sum(n // 2 for n in sizes)

    def rcopy(a, b, k, dev):
        return pltpu.make_async_remote_copy(src_ref=a, dst_ref=b, send_sem=send_sems.at[k],
                                            recv_sem=recv_sems.at[k], device_id=dev, device_id_type=MESH)

    def total(k):
        x, y, c = _coords()
        rows = bytes_ref.at[pl.ds(0, half_rows)]
        return rcopy(rows, rows, k, (x, y, c))

    def send():
        x, y, c = _coords()
        s_me = 2 * x + y
        for k, (px, py) in enumerate(_chip_peers(x, y)):
            for p, n in zip(group, sizes):
                h = n // 2
                rcopy(src(p, c * h, h), landing(p, s_me, c * h, h), k, (px, py, c)).start()
        for p, n in zip(group, sizes):
            for a, b in ((src(p, 0, n), stage.at[pl.ds(0, n)]), (stage.at[pl.ds(0, n)], landing(p, s_me, 0, n))):
                cp = pltpu.make_async_copy(a, b, loc_sem)
                cp.start()
                cp.wait()

    def forward():
        x, y, c = _coords()
        for k, (px, py) in enumerate(_chip_peers(x, y)):
            total(k).wait_recv()
            for p, n in zip(group, sizes):
                rows = landing(p, 2 * px + py, c * (n // 2), n // 2)
                rcopy(rows, rows, 3 + k, (x, y, 1 - c)).start()

    def finish():
        for k in range(3):
            total(3 + k).wait_recv()
        for k in range(6):
            total(k).wait_send()

    return send, forward, finish


def _piece_rows(ref, start, off, n):
    first = start + off
    return ref.at[pl.ds(first if isinstance(first, int) else pl.multiple_of(first, 32), n)]


def _gather_win(win_t, wdw_shard):
    tables = [[_pieces(s)[p][2] - WT0 for s in range(N_SHARDS)] for p in WIN_PIECES]

    def body(win_ref, wdw_ref, wt_ref, wdwall_ref, stage, send_sems, recv_sems, loc_sems):
        x, y, c = _coords()
        s_me = 2 * x + y
        send, forward, finish = _gather_ops(
            WIN_PIECES, lambda p, off, n: _piece_rows(win_ref, _WT_CUTS[p], off, n),
            lambda p, s, off, n: _piece_rows(wt_ref, _sel(s, tables[p]), off, n),
            wt_ref, stage, send_sems, recv_sems, loc_sems.at[0])
        own_wdw = pltpu.make_async_copy(wdw_ref, wdwall_ref.at[s_me], loc_sems.at[1])
        own_wdw.start()
        wdw_sends = [pltpu.make_async_remote_copy(
            src_ref=wdw_ref, dst_ref=wdwall_ref.at[s_me], send_sem=send_sems.at[6 + k],
            recv_sem=recv_sems.at[6 + k], device_id=(px, py, c), device_id_type=MESH)
            for k, (px, py) in enumerate(_chip_peers(x, y))]
        for cp in wdw_sends:
            cp.start()
        send()
        forward()
        finish()
        for cp in wdw_sends:
            cp.wait_recv()
        for cp in wdw_sends:
            cp.wait_send()
        own_wdw.wait()

    any_spec = pl.BlockSpec(memory_space=pl.ANY)
    return pl.pallas_call(
        body, name="gather_win",
        out_shape=(jax.ShapeDtypeStruct((IN_WIDTH, D), BF16), jax.ShapeDtypeStruct((N_SHARDS, 32, PLE), F32)),
        in_specs=[any_spec, any_spec], out_specs=(any_spec, any_spec),
        scratch_shapes=[pltpu.VMEM((max(_pieces(0)[p][1] for p in WIN_PIECES), D), BF16),
                        pltpu.SemaphoreType.DMA((9,)), pltpu.SemaphoreType.DMA((9,)),
                        pltpu.SemaphoreType.DMA((2,))],
    )(win_t, wdw_shard)


RT = 320


def _chip_sum(cidx, gpack, r1):
    def body(c_ref, g_ref, r_ref, o_ref):
        o_ref[...] = (g_ref[...] + r_ref[...]).astype(BF16)

    nt = HALF_ROWS // RT
    return pl.pallas_call(
        body, name="chip_sum",
        grid_spec=pltpu.PrefetchScalarGridSpec(
            num_scalar_prefetch=1, grid=(N_SHARDS, nt),
            in_specs=[pl.BlockSpec((1, RT, D), lambda s, t, c: (s, c[0] * nt + t, 0)),
                      pl.BlockSpec((1, RT, D), lambda s, t, c: (s, t, 0))],
            out_specs=pl.BlockSpec((1, RT, D), lambda s, t, c: (s, t, 0))),
        out_shape=jax.ShapeDtypeStruct((N_SHARDS, HALF_ROWS, D), BF16),
        compiler_params=_params(("arbitrary", "arbitrary")),
    )(cidx, gpack, r1)


def _final_half(sc, gpack, r1, r2):
    def body(sc_ref, g_ref, r_ref, p_ref, o_ref):
        acc = g_ref[0] + r_ref[0]
        for k in range(3):
            acc = acc + p_ref[k].astype(F32)
        o_ref[...] = acc

    nt = HALF_ROWS // RT
    return pl.pallas_call(
        body, name="final_half",
        grid_spec=pltpu.PrefetchScalarGridSpec(
            num_scalar_prefetch=1, grid=(nt,),
            in_specs=[pl.BlockSpec((1, RT, D), lambda t, sc: (sc[0], sc[1] * nt + t, 0)),
                      pl.BlockSpec((1, RT, D), lambda t, sc: (sc[0], t, 0)),
                      pl.BlockSpec((3, RT, D), lambda t, sc: (0, t, 0))],
            out_specs=pl.BlockSpec((RT, D), lambda t, sc: (sc[1] * nt + t, 0))),
        out_shape=jax.ShapeDtypeStruct((PACK_ROWS, D), F32),
        compiler_params=_params(("arbitrary",)),
    )(sc, gpack, r1, r2)


def _swap_halves(fh):
    def body(f_ref, o_ref, send_sem, recv_sem):
        x, y, c = _coords()
        rows = pl.ds(pl.multiple_of(c * HALF_ROWS, 32), HALF_ROWS)
        cp = pltpu.make_async_remote_copy(
            src_ref=f_ref.at[rows], dst_ref=o_ref.at[rows],
            send_sem=send_sem, recv_sem=recv_sem, device_id=(x, y, 1 - c), device_id_type=MESH)
        cp.start()
        cp.wait()

    any_spec = pl.BlockSpec(memory_space=pl.ANY)
    return pl.pallas_call(
        body, name="swap_halves",
        out_shape=jax.ShapeDtypeStruct((PACK_ROWS, D), F32),
        in_specs=[any_spec], out_specs=any_spec, input_output_aliases={0: 0},
        scratch_shapes=[pltpu.SemaphoreType.DMA, pltpu.SemaphoreType.DMA],
    )(fh)


VEC_ROWS = 40


def _all_reduce_small(vec):
    def body(v_ref, o_ref, buf, send_sems, recv_sems):
        x, y, c = _coords()
        me = 4 * x + 2 * y + c
        buf[me] = v_ref[...]
        cps = []
        for r in range(1, 8):
            dx, dy, dc = (r >> 2) & 1, (r >> 1) & 1, r & 1
            peer = (1 - x if dx else x, 1 - y if dy else y, 1 - c if dc else c)
            cp = pltpu.make_async_remote_copy(
                src_ref=v_ref, dst_ref=buf.at[me], send_sem=send_sems.at[r - 1],
                recv_sem=recv_sems.at[r - 1], device_id=peer, device_id_type=MESH)
            cp.start()
            cps.append(cp)
        for cp in cps:
            cp.wait_recv()
        for cp in cps:
            cp.wait_send()
        acc = buf[0]
        for d in range(1, 8):
            acc = acc + buf[d]
        o_ref[...] = acc

    vm = pl.BlockSpec(memory_space=pltpu.VMEM)
    return pl.pallas_call(
        body, name="all_reduce_small",
        out_shape=jax.ShapeDtypeStruct((VEC_ROWS, D), F32),
        in_specs=[vm], out_specs=vm,
        scratch_shapes=[pltpu.VMEM((8, VEC_ROWS, D), F32), pltpu.SemaphoreType.DMA((7,)),
                        pltpu.SemaphoreType.DMA((7,))],
    )(vec)


def _inproj(x, ln_pre, wt, sq_shards, wpp_shard, tm):
    T = x.shape[0]
    nsq = len(sq_shards)
    n_i = T // tm

    def body(*refs):
        x_ref, g_ref, wt_ref = refs[:3]
        sq_refs = refs[3:3 + nsq]
        wpp_ref = refs[3 + nsq]
        z_ref, zkv_ref, h_ref, wsq_ref, wppf_ref, wbuf, stage, send_sems, recv_sems, loc_sem, wsem = (
            refs[4 + nsq:])
        i = pl.program_id(0)

        def src(p, off, n):
            k = p - SQ_PIECES[0]
            return _piece_rows(wpp_ref if k == nsq else sq_refs[k], 0, off, n)

        def landing(p, s, off, n):
            k = p - SQ_PIECES[0]
            if k == nsq:
                return _piece_rows(wppf_ref, WPP_SHARD * s, off, n)
            return _piece_rows(wsq_ref, D * k + SQ_SHARD * s, off, n)

        send, forward, finish = _gather_ops(SQ_PIECES, src, landing, wsq_ref, stage, send_sems, recv_sems,
                                            loc_sem)

        @pl.when(i == 0)
        def _():
            send()
            load = pltpu.make_async_copy(wt_ref, wbuf, wsem)
            load.start()
            load.wait()

        @pl.when(i == n_i // 2)
        def _():
            forward()

        xv = x_ref[...]
        r = lax.rsqrt(jnp.mean(xv * xv, axis=-1, keepdims=True) + EPS)
        h = (xv * r * g_ref[...]).astype(BF16)
        h_ref[...] = h
        for j in range(7):
            z_ref[:, j * D:(j + 1) * D] = _mm_nt(h, wbuf[j * D:(j + 1) * D, :]).astype(BF16)
        zkv_ref[...] = _mm_nt(h, wbuf[ZKV:IN_WIDTH, :]).astype(BF16)

        @pl.when(i == n_i - 1)
        def _():
            finish()

    any_spec = pl.BlockSpec(memory_space=pl.ANY)
    return pl.pallas_call(
        body, name="inproj", grid=(n_i,),
        in_specs=[pl.BlockSpec((tm, D), lambda i: (i, 0)), pl.BlockSpec((1, D), lambda i: (0, 0))]
        + [any_spec] * (nsq + 2),
        out_specs=(pl.BlockSpec((tm, ZKV), lambda i: (i, 0)), pl.BlockSpec((tm, 2 * BLOCK), lambda i: (i, 0)),
                   pl.BlockSpec((tm, D), lambda i: (i, 0)), any_spec, any_spec),
        out_shape=(jax.ShapeDtypeStruct((T, ZKV), BF16), jax.ShapeDtypeStruct((T, 2 * BLOCK), BF16),
                   jax.ShapeDtypeStruct((T, D), BF16), jax.ShapeDtypeStruct((nsq * D, D), BF16),
                   jax.ShapeDtypeStruct((PLE, D), BF16)),
        scratch_shapes=[pltpu.VMEM((IN_WIDTH, D), BF16), pltpu.VMEM((SQ_SHARD, D), BF16),
                        pltpu.SemaphoreType.DMA((6,)), pltpu.SemaphoreType.DMA((6,)), pltpu.SemaphoreType.DMA,
                        pltpu.SemaphoreType.DMA],
        compiler_params=_params(("arbitrary",)),
    )(x, ln_pre, wt, *sq_shards, wpp_shard)


HALO = 32
CONV_RC = 64
CONV_LC = 256


def _conv_taps(w_ref, src, r0, lane0, offset_of_tap):
    lanes = pl.ds(lane0, CONV_LC)
    out = None
    for b in range(8):
        taps = [k for k in range(CONV_K) if offset_of_tap(k) % 8 == b]
        if not taps:
            continue
        rows = CONV_RC + (8 if b else 0)
        vb = None
        for k in taps:
            term = w_ref[k:k + 1, lanes] * src[pl.ds(r0 + (offset_of_tap(k) - b), rows), lanes]
            vb = term if vb is None else vb + term
        vb = vb[b:b + CONV_RC] if b else vb
        out = vb if out is None else out + vb
    return out


def _conv_fwd(z, wdw, b_dw, ln_g, ln_b, wall, S, tm):
    T = z.shape[0]
    nt = S // tm
    hb = tm // HALO

    def body(cv_ref, cg_ref, cgate_ref, hcv_ref, hcg_ref, wdw_ref, bdw_ref, lng_ref, lnb_ref, wpw_ref,
             wbrc_ref, ya_ref, y_ref, rstd_ref, pw_ref, ubuf, cbuf):
        t = pl.program_id(1)
        ubuf[HALO:HALO + tm, :] = cv_ref[...].astype(F32) * _sig(cg_ref[...].astype(F32))
        hu = hcv_ref[...].astype(F32) * _sig(hcg_ref[...].astype(F32))
        ubuf[0:HALO, :] = jnp.where(t > 0, hu, 0.0)
        ubuf[HALO + tm:HALO + tm + 8, :] = jnp.zeros((8, D), F32)

        def chunk(ci, carry):
            r0 = pl.multiple_of(ci * CONV_RC, CONV_RC)
            for lg in range(D // CONV_LC):
                acc = _conv_taps(wdw_ref, ubuf, r0, lg * CONV_LC, lambda k: HALO - (CONV_K - 1) + k)
                cbuf[pl.ds(r0, CONV_RC), pl.ds(lg * CONV_LC, CONV_LC)] = acc
            return carry

        lax.fori_loop(0, tm // CONV_RC, chunk, 0)
        cc = cbuf[...] + bdw_ref[...]
        mu = jnp.mean(cc, axis=-1, keepdims=True)
        dd = cc - mu
        rstd = lax.rsqrt(jnp.mean(dd * dd, axis=-1, keepdims=True) + EPS)
        yn = dd * rstd
        y_ref[...] = yn.astype(BF16)
        rstd_ref[...] = rstd
        n = yn * lng_ref[...] + lnb_ref[...]
        s = n * _sig(n)
        pw = _mm(s.astype(BF16), wpw_ref[...])
        pw_ref[...] = pw.astype(BF16)
        gt = cgate_ref[...].astype(F32)
        ya_in = pw * (gt * _sig(gt))
        ya_ref[...] = _mm(ya_in.astype(BF16), wbrc_ref[...]).astype(BF16)

    def row(b, t):
        return b * nt + t

    def halo(b, t):
        return jnp.maximum(row(b, t) * hb - 1, 0)

    vec = pl.BlockSpec((1, D), lambda b, t: (0, 0))
    tile = lambda j: pl.BlockSpec((tm, D), lambda b, t: (row(b, t), j))
    out_tile = pl.BlockSpec((tm, D), lambda b, t: (row(b, t), 0))
    return pl.pallas_call(
        body, name="conv_fwd", grid=(T // S, nt),
        in_specs=[tile(ZB_CVAL), tile(ZB_CGLU), tile(ZB_CGATE),
                  pl.BlockSpec((HALO, D), lambda b, t: (halo(b, t), ZB_CVAL)),
                  pl.BlockSpec((HALO, D), lambda b, t: (halo(b, t), ZB_CGLU)),
                  pl.BlockSpec((32, D), lambda b, t: (0, 0)), vec, vec, vec,
                  pl.BlockSpec((D, D), lambda b, t: (0, 0)),
                  pl.BlockSpec((D, D), lambda b, t: (1, 0))],
        out_specs=(out_tile, out_tile, pl.BlockSpec((tm, 1), lambda b, t: (row(b, t), 0)), out_tile),
        out_shape=(jax.ShapeDtypeStruct((T, D), BF16), jax.ShapeDtypeStruct((T, D), BF16),
                   jax.ShapeDtypeStruct((T, 1), F32), jax.ShapeDtypeStruct((T, D), BF16)),
        scratch_shapes=[pltpu.VMEM((tm + HALO + 8, D), F32), pltpu.VMEM((tm, D), F32)],
        compiler_params=_params(("arbitrary", "arbitrary")),
    )(z, z, z, z, z, wdw, b_dw, ln_g, ln_b, wall, wall)


def _swap_matrix():
    r = lax.broadcasted_iota(jnp.int32, (BLOCK, BLOCK), 0)
    l = lax.broadcasted_iota(jnp.int32, (BLOCK, BLOCK), 1)
    lh = l & (HEAD_DIM - 1)
    half = ROPE_DIM // 2
    hit = ((lh < half) & (r == l + half)) | ((lh >= half) & (lh < ROPE_DIM) & (r == l - half))
    return jnp.where(hit, 1.0, 0.0).astype(BF16)


def _rope(tb, cos, sin, pswap):
    return tb.astype(F32) * cos + _mm(tb, pswap) * sin


def _rope_f32(tv, cos, sin, pswap):
    hi = tv.astype(BF16)
    lo = (tv - hi.astype(F32)).astype(BF16)
    return tv * cos + (_mm(hi, pswap) + _mm(lo, pswap)) * sin


def _kv_variants(kv):
    lane = lax.broadcasted_iota(jnp.int32, kv.shape, 1)
    lo = lane < HEAD_DIM
    sw = pltpu.roll(kv, HEAD_DIM, 1)
    z = jnp.zeros_like(kv)
    g0 = (jnp.where(lo, kv, z).astype(BF16), jnp.where(lo, z, sw).astype(BF16))
    g1 = (jnp.where(lo, sw, z).astype(BF16), jnp.where(lo, z, kv).astype(BF16))
    return (g0, g1)


def _band_mask(nq):
    qi = lax.broadcasted_iota(jnp.int32, (nq * BLOCK, 2 * BLOCK), 0) & (BLOCK - 1)
    sj = lax.broadcasted_iota(jnp.int32, (nq * BLOCK, 2 * BLOCK), 1)
    return (sj <= qi + BLOCK) & (sj > qi), sj


def _sink_rep(sink_ref, g, e):
    return jnp.concatenate(
        [jnp.full((BLOCK, BLOCK), sink_ref[8 * g + 2 * j + e], F32) for j in range(4)], axis=0)


def _softmax_sink(s, valid, sk):
    rows = s.shape[0]
    s = jnp.where(valid, s, -1e30)
    m = jnp.maximum(jnp.broadcast_to(jnp.max(s, axis=-1, keepdims=True), (rows, BLOCK)), sk)
    p = jnp.exp(s - jnp.concatenate([m, m], axis=1))
    ps = jnp.exp(sk - m)
    inv = 1.0 / (_mm(p.astype(BF16), jnp.ones((2 * BLOCK, BLOCK), BF16)) + ps)
    return p * jnp.concatenate([inv, inv], axis=1), ps * inv


def _attn_fwd(z, zkv, cos_t, sin_t, sinks, S, tq):
    T = z.shape[0]
    nt = S // tq
    nq = tq // BLOCK

    def body(sink_ref, q_ref, kv_ref, hkv_ref, cos_ref, sin_ref, hcos_ref, hsin_ref, o_ref):
        t = pl.program_id(1)
        cos = cos_ref[...]
        sin = sin_ref[...]
        pswap = _swap_matrix()
        kv = jnp.concatenate([hkv_ref[...], kv_ref[...]], axis=0)
        cos_k = jnp.concatenate([hcos_ref[...], cos], axis=0)
        sin_k = jnp.concatenate([hsin_ref[...], sin], axis=0)
        kx = _kv_variants(_rope(kv[:, :BLOCK], cos_k, sin_k, pswap))
        vx = _kv_variants(kv[:, BLOCK:].astype(F32))
        band, sj = _band_mask(4)
        qs = [(_rope(q_ref[:, 128 * hp:128 * hp + 128], cos, sin, pswap) * 0.125).astype(BF16)
              for hp in range(8)]
        for n in range(nq):
            first = (t == 0) & (n == 0)
            valid = band & (jnp.logical_not(first) | (sj >= BLOCK))
            r0 = n * BLOCK
            for g in range(2):
                lhs = jnp.concatenate([qs[4 * g + j][r0:r0 + BLOCK] for j in range(4)], axis=0)
                acc = jnp.zeros((4 * BLOCK, BLOCK), F32)
                for e in range(2):
                    s = _mm_nt(lhs, kx[g][e][r0:r0 + 2 * BLOCK])
                    p, _ = _softmax_sink(s, valid, _sink_rep(sink_ref, g, e))
                    acc = acc + _mm(p.astype(BF16), vx[g][e][r0:r0 + 2 * BLOCK])
                for j in range(4):
                    o_ref[r0:r0 + BLOCK, 128 * (4 * g + j):128 * (4 * g + j) + 128] = (
                        acc[j * BLOCK:(j + 1) * BLOCK].astype(BF16))

    def row(b, t):
        return b * nt + t

    def halo(b, t):
        return jnp.maximum(row(b, t) * nq - 1, 0)

    return pl.pallas_call(
        body, name="attn_fwd", grid=(T // S, nt),
        in_specs=[pl.BlockSpec(memory_space=pltpu.SMEM),
                  pl.BlockSpec((tq, D), lambda b, t: (row(b, t), ZB_Q)),
                  pl.BlockSpec((tq, 2 * BLOCK), lambda b, t: (row(b, t), 0)),
                  pl.BlockSpec((BLOCK, 2 * BLOCK), lambda b, t: (halo(b, t), 0)),
                  pl.BlockSpec((tq, BLOCK), lambda b, t: (row(b, t), 0)),
                  pl.BlockSpec((tq, BLOCK), lambda b, t: (row(b, t), 0)),
                  pl.BlockSpec((BLOCK, BLOCK), lambda b, t: (halo(b, t), 0)),
                  pl.BlockSpec((BLOCK, BLOCK), lambda b, t: (halo(b, t), 0))],
        out_specs=pl.BlockSpec((tq, D), lambda b, t: (row(b, t), 0)),
        out_shape=jax.ShapeDtypeStruct((T, D), BF16),
        compiler_params=_params(("arbitrary", "arbitrary")),
    )(sinks, z, zkv, zkv, cos_t, sin_t, cos_t, sin_t)


def _tail_a(x, tgt, p, o, ya, z, ln_post, wall, wppt, tm):
    T = x.shape[0]
    last = T // tm - 1

    def body(x_ref, tgt_ref, p_ref, o_ref, ya_ref, ag_ref, gc_ref, ga_ref, lnp_ref, wbra_ref, wout_ref,
             wpg_ref, wppt_ref, loss_ref, dx1_ref, dm_ref, yb_ref, glnp_ref, gpack_ref, gwpp_ref,
             acc_out, acc_pg, sem):
        i = pl.program_id(0)

        @pl.when(i == 0)
        def _():
            acc_out[...] = jnp.zeros_like(acc_out)
            acc_pg[...] = jnp.zeros_like(acc_pg)
            gwpp_ref[...] = jnp.zeros_like(gwpp_ref)
            glnp_ref[...] = jnp.zeros_like(glnp_ref)
            loss_ref[...] = jnp.zeros_like(loss_ref)

        ag = ag_ref[...].astype(F32)
        yb_in = (o_ref[...].astype(F32) * (ag * _sig(ag))).astype(BF16)
        yb = _mm(yb_in, wbra_ref[...])
        yb_ref[...] = yb.astype(BF16)
        m = (_sig(gc_ref[...].astype(F32)) * ya_ref[...].astype(F32)
             + _sig(ga_ref[...].astype(F32)) * yb).astype(BF16)
        mo = _mm(m, wout_ref[...])
        r2 = lax.rsqrt(jnp.mean(mo * mo, axis=-1, keepdims=True) + EPS)
        nrm = mo * r2
        g_post = lnp_ref[...]
        x1 = x_ref[...] + nrm * g_post
        x1b = x1.astype(BF16)
        gate = _sig(_mm(x1b, wpg_ref[...]))
        pb = p_ref[...].astype(BF16)
        pp = _mm_nt(pb, wppt_ref[...])
        err = x1 + gate * pp - tgt_ref[...]
        loss_ref[...] += 0.5 * jnp.sum(jnp.sum(err * err, axis=-1, keepdims=True) * (1.0 / D),
                                       axis=0, keepdims=True)
        dx2 = err * (1.0 / D)
        dgp = (dx2 * pp * gate * (1.0 - gate)).astype(BF16)
        dpp = (dx2 * gate).astype(BF16)
        dx1 = dx2 + _mm_nt(dgp, wpg_ref[...])
        dx1_ref[...] = dx1
        acc_pg[...] += _mm_tn(x1b, dgp)
        gwpp_ref[...] += _mm_tn(dpp, pb)
        glnp_ref[...] += jnp.sum(dx1 * nrm, axis=0, keepdims=True)
        a = dx1 * g_post
        dmo = (r2 * (a - nrm * jnp.mean(a * nrm, axis=-1, keepdims=True))).astype(BF16)
        dm_ref[...] = _mm_nt(dmo, wout_ref[...]).astype(BF16)
        acc_out[...] += _mm_tn(m, dmo)

        @pl.when(i == last)
        def _():
            _flush_to_pack(acc_out, gpack_ref, 3 * D, sem.at[0])
            _flush_to_pack(acc_pg, gpack_ref, 4 * D, sem.at[1])

    tile = pl.BlockSpec((tm, D), lambda i: (i, 0))
    ztile = lambda j: pl.BlockSpec((tm, D), lambda i: (i, j))
    wsq = lambda k: pl.BlockSpec((D, D), lambda i: (k, 0))
    const = lambda shp: pl.BlockSpec(shp, lambda i: (0, 0))
    any_spec = pl.BlockSpec(memory_space=pl.ANY)
    return pl.pallas_call(
        body, name="tail_a", grid=(T // tm,),
        in_specs=[tile, tile, pl.BlockSpec((tm, PLE), lambda i: (i, 0)), tile, tile, ztile(ZB_AGATE),
                  ztile(ZB_GCONV), ztile(ZB_GATTN), const((1, D)), wsq(2), wsq(3), wsq(4), const((D, PLE))],
        out_specs=(const((1, 1)), tile, tile, tile, const((1, D)), any_spec, const((D, PLE))),
        out_shape=(jax.ShapeDtypeStruct((1, 1), F32), jax.ShapeDtypeStruct((T, D), F32),
                   jax.ShapeDtypeStruct((T, D), BF16), jax.ShapeDtypeStruct((T, D), BF16),
                   jax.ShapeDtypeStruct((1, D), F32), jax.ShapeDtypeStruct((N_SHARDS, PACK_ROWS, D), F32),
                   jax.ShapeDtypeStruct((D, PLE), F32)),
        scratch_shapes=[pltpu.VMEM((D, D), F32), pltpu.VMEM((D, D), F32), pltpu.SemaphoreType.DMA((2,))],
        compiler_params=_params(("arbitrary",)),
    )(x, tgt, p, o, ya, z, z, z, ln_post, wall, wall, wall, wppt)


def _dsilu(v, sg):
    return sg * (1.0 + v * (1.0 - sg))


def _tail_b(dm, ya, yb, o, z, pw, y, rstd, ln_g, ln_b, wall, gpack, tm):
    T = dm.shape[0]
    last = T // tm - 1

    def body(dm_ref, ya_ref, yb_ref, o_ref, ag_ref, gc_ref, ga_ref, cgate_ref, pw_ref, y_ref, rstd_ref,
             lng_ref, lnb_ref, wpw_ref, wbrc_ref, wbra_ref, gpack_in, dg_ref, do_ref, dc_ref, gvec_ref,
             gpack_ref, acc_bra, acc_brc, acc_pw, sem):
        i = pl.program_id(0)

        @pl.when(i == 0)
        def _():
            acc_bra[...] = jnp.zeros_like(acc_bra)
            acc_brc[...] = jnp.zeros_like(acc_brc)
            acc_pw[...] = jnp.zeros_like(acc_pw)
            gvec_ref[...] = jnp.zeros_like(gvec_ref)

        g = lng_ref[...]

        def part(rs):
            dm_v = dm_ref[rs, :].astype(F32)
            sgc = _sig(gc_ref[rs, :].astype(F32))
            sga = _sig(ga_ref[rs, :].astype(F32))
            dya = (dm_v * sgc).astype(BF16)
            dyb = (dm_v * sga).astype(BF16)
            dg_ref[rs, D:2 * D] = (dm_v * ya_ref[rs, :].astype(F32) * sgc * (1.0 - sgc)).astype(BF16)
            dg_ref[rs, 2 * D:3 * D] = (dm_v * yb_ref[rs, :].astype(F32) * sga * (1.0 - sga)).astype(BF16)
            ag = ag_ref[rs, :].astype(F32)
            sag = _sig(ag)
            sa = ag * sag
            ov = o_ref[rs, :].astype(F32)
            dyb_in = _mm_nt(dyb, wbra_ref[...])
            do_ref[rs, :] = (dyb_in * sa).astype(BF16)
            dg_ref[rs, 0:D] = (dyb_in * ov * _dsilu(ag, sag)).astype(BF16)
            gt = cgate_ref[rs, :].astype(F32)
            sgt = _sig(gt)
            sgate = gt * sgt
            pw = pw_ref[rs, :].astype(F32)
            dya_in = _mm_nt(dya, wbrc_ref[...])
            dpw = (dya_in * sgate).astype(BF16)
            dg_ref[rs, 3 * D:4 * D] = (dya_in * pw * _dsilu(gt, sgt)).astype(BF16)
            yn = y_ref[rs, :].astype(F32)
            n = yn * g + lnb_ref[...]
            sn = _sig(n)
            dn = _mm_nt(dpw, wpw_ref[...]) * _dsilu(n, sn)
            dy = dn * g
            dc = rstd_ref[rs, :] * (dy - jnp.mean(dy, axis=-1, keepdims=True)
                                    - yn * jnp.mean(dy * yn, axis=-1, keepdims=True))
            dc_ref[rs, :] = dc.astype(BF16)
            sums = (jnp.sum(dn * yn, axis=0, keepdims=True), jnp.sum(dn, axis=0, keepdims=True),
                    jnp.sum(dc, axis=0, keepdims=True))
            return ((ov * sa).astype(BF16), dyb, (pw * sgate).astype(BF16), dya, (n * sn).astype(BF16), dpw,
                    sums)

        parts = [part(pl.ds(r * (tm // TAIL_PARTS), tm // TAIL_PARTS)) for r in range(TAIL_PARTS)]
        cat = lambda j: jnp.concatenate([pt[j] for pt in parts], axis=0)
        acc_bra[...] += _mm_tn(cat(0), cat(1))
        acc_brc[...] += _mm_tn(cat(2), cat(3))
        acc_pw[...] += _mm_tn(cat(4), cat(5))
        for j in range(3):
            gvec_ref[j:j + 1, :] += sum(pt[6][j] for pt in parts)

        @pl.when(i == last)
        def _():
            _flush_to_pack(acc_pw, gpack_ref, 0, sem.at[0])
            _flush_to_pack(acc_brc, gpack_ref, D, sem.at[1])
            _flush_to_pack(acc_bra, gpack_ref, 2 * D, sem.at[2])

    tile = pl.BlockSpec((tm, D), lambda i: (i, 0))
    ztile = lambda j: pl.BlockSpec((tm, D), lambda i: (i, j))
    wsq = lambda k: pl.BlockSpec((D, D), lambda i: (k, 0))
    const = lambda shp: pl.BlockSpec(shp, lambda i: (0, 0))
    any_spec = pl.BlockSpec(memory_space=pl.ANY)
    return pl.pallas_call(
        body, name="tail_b", grid=(T // tm,),
        in_specs=[tile, tile, tile, tile, ztile(ZB_AGATE), ztile(ZB_GCONV), ztile(ZB_GATTN), ztile(ZB_CGATE),
                  tile, tile, pl.BlockSpec((tm, 1), lambda i: (i, 0)), const((1, D)), const((1, D)), wsq(0),
                  wsq(1), wsq(2), any_spec],
        out_specs=(pl.BlockSpec((tm, 4 * D), lambda i: (i, 0)), tile, tile, const((8, D)), any_spec),
        out_shape=(jax.ShapeDtypeStruct((T, 7 * D), BF16), jax.ShapeDtypeStruct((T, D), BF16),
                   jax.ShapeDtypeStruct((T, D), BF16), jax.ShapeDtypeStruct((8, D), F32),
                   jax.ShapeDtypeStruct(gpack.shape, F32)),
        input_output_aliases={16: 4},
        scratch_shapes=[pltpu.VMEM((D, D), F32), pltpu.VMEM((D, D), F32), pltpu.VMEM((D, D), F32),
                        pltpu.SemaphoreType.DMA((3,))],
        compiler_params=_params(("arbitrary",)),
    )(dm, ya, yb, o, z, z, z, z, pw, y, rstd, ln_g, ln_b, wall, wall, wall, gpack)


def _conv_bwd(dc, z, wdw, dz, S, tm):
    T = dc.shape[0]
    nt = S // tm
    hb = tm // HALO
    nrows = T // HALO

    def body(dc_ref, hdc_ref, cv_ref, cg_ref, hcv_ref, hcg_ref, wdw_ref, dz_in, dz_ref, gw_ref, ubuf, dcbuf,
             dubuf, dwacc, shbuf):
        b = pl.program_id(0)
        t = pl.program_id(1)

        @pl.when((b == 0) & (t == 0))
        def _():
            dwacc[...] = jnp.zeros_like(dwacc)

        cv = cv_ref[...].astype(F32)
        sg = _sig(cg_ref[...].astype(F32))
        ubuf[HALO:HALO + tm, :] = cv * sg
        hu = hcv_ref[...].astype(F32) * _sig(hcg_ref[...].astype(F32))
        ubuf[0:HALO, :] = jnp.where(t > 0, hu, 0.0)
        ubuf[HALO + tm:HALO + tm + 8, :] = jnp.zeros((8, D), F32)
        dcbuf[0:tm, :] = dc_ref[...].astype(F32)
        dcbuf[tm:tm + HALO, :] = jnp.where(t < nt - 1, hdc_ref[...].astype(F32), 0.0)
        dcbuf[tm + HALO:tm + HALO + 8, :] = jnp.zeros((8, D), F32)

        def chunk(ci, carry):
            r0 = pl.multiple_of(ci * CONV_RC, CONV_RC)
            for lg in range(D // CONV_LC):
                l0 = lg * CONV_LC
                dubuf[pl.ds(r0, CONV_RC), pl.ds(l0, CONV_LC)] = _conv_taps(
                    wdw_ref, dcbuf, r0, l0, lambda k: CONV_K - 1 - k)
                dcc = dcbuf[pl.ds(r0, CONV_RC), pl.ds(l0, CONV_LC)]
                zero8 = jnp.zeros((8, CONV_LC), F32)
                dcz = jnp.concatenate([zero8, dcc, zero8], axis=0)
                for bb in range(8):
                    taps = [k for k in range(CONV_K) if (HALO - (CONV_K - 1) + k) % 8 == bb]
                    if not taps:
                        continue
                    rows = CONV_RC + (8 if bb else 0)
                    if bb:
                        shbuf[bb] = dcz[8 - bb:8 - bb + rows]
                    for k in taps:
                        a8 = HALO - (CONV_K - 1) + k - bb
                        dcs = shbuf[bb] if bb else dcc
                        prod = dcs * ubuf[pl.ds(r0 + a8, rows), pl.ds(l0, CONV_LC)]
                        part = prod[0:8]
                        for q in range(1, rows // 8):
                            part = part + prod[8 * q:8 * q + 8]
                        dwacc[8 * k:8 * k + 8, pl.ds(l0, CONV_LC)] += part
            return carry

        lax.fori_loop(0, tm // CONV_RC, chunk, 0)
        du = dubuf[...]
        dz_ref[:, 0:D] = (du * sg).astype(BF16)
        dz_ref[:, D:2 * D] = (du * cv * sg * (1.0 - sg)).astype(BF16)

        @pl.when((b == pl.num_programs(0) - 1) & (t == nt - 1))
        def _():
            for k in range(32):
                gw_ref[k:k + 1, :] = jnp.sum(dwacc[8 * k:8 * k + 8, :], axis=0, keepdims=True)

    def row(b, t):
        return b * nt + t

    def prev_halo(b, t):
        return jnp.maximum(row(b, t) * hb - 1, 0)

    def next_halo(b, t):
        return jnp.minimum((row(b, t) + 1) * hb, nrows - 1)

    return pl.pallas_call(
        body, name="conv_bwd", grid=(T // S, nt),
        in_specs=[pl.BlockSpec((tm, D), lambda b, t: (row(b, t), 0)),
                  pl.BlockSpec((HALO, D), lambda b, t: (next_halo(b, t), 0)),
                  pl.BlockSpec((tm, D), lambda b, t: (row(b, t), ZB_CVAL)),
                  pl.BlockSpec((tm, D), lambda b, t: (row(b, t), ZB_CGLU)),
                  pl.BlockSpec((HALO, D), lambda b, t: (prev_halo(b, t), ZB_CVAL)),
                  pl.BlockSpec((HALO, D), lambda b, t: (prev_halo(b, t), ZB_CGLU)),
                  pl.BlockSpec((32, D), lambda b, t: (0, 0)),
                  pl.BlockSpec(memory_space=pl.ANY)],
        out_specs=(pl.BlockSpec((tm, 2 * D), lambda b, t: (row(b, t), ZB_CVAL // 2)),
                   pl.BlockSpec((32, D), lambda b, t: (0, 0))),
        out_shape=(jax.ShapeDtypeStruct(dz.shape, BF16), jax.ShapeDtypeStruct((32, D), F32)),
        input_output_aliases={7: 0},
        scratch_shapes=[pltpu.VMEM((tm + HALO + 8, D), F32), pltpu.VMEM((tm + HALO + 8, D), F32),
                        pltpu.VMEM((tm, D), F32), pltpu.VMEM((8 * 32, D), F32),
                        pltpu.VMEM((8, CONV_RC + 8, CONV_LC), F32)],
        compiler_params=_params(("arbitrary", "arbitrary")),
    )(dc, dc, z, z, z, z, wdw, dz)


def _attn_bwd(z, zkv, o, do, cos_t, sin_t, sinks, dz, S, tq):
    T = z.shape[0]
    nt = S // tq
    nq = tq // BLOCK

    def body(sink_ref, q_ref, kv_ref, hkv_ref, o_ref, do_ref, cos_ref, sin_ref, hcos_ref, hsin_ref, dz_in,
             dq_ref, dkv_ref, gs_ref, carry, dkacc, dvacc):
        b = pl.program_id(0)
        tt = pl.program_id(1)
        t = nt - 1 - tt

        @pl.when((b == 0) & (tt == 0))
        def _():
            gs_ref[...] = jnp.zeros_like(gs_ref)

        @pl.when(tt == 0)
        def _():
            carry[...] = jnp.zeros_like(carry)

        cos = cos_ref[...]
        sin = sin_ref[...]
        pswap = _swap_matrix()
        kv = jnp.concatenate([hkv_ref[...], kv_ref[...]], axis=0)
        cos_k = jnp.concatenate([hcos_ref[...], cos], axis=0)
        sin_k = jnp.concatenate([hsin_ref[...], sin], axis=0)
        kx = _kv_variants(_rope(kv[:, :BLOCK], cos_k, sin_k, pswap))
        vx = _kv_variants(kv[:, BLOCK:].astype(F32))
        band, sj = _band_mask(4)
        lo = lax.broadcasted_iota(jnp.int32, (4 * BLOCK, BLOCK), 1) < HEAD_DIM
        ones = jnp.ones((2 * BLOCK, 2 * BLOCK), BF16)
        qs = [(_rope(q_ref[:, 128 * hp:128 * hp + 128], cos, sin, pswap) * 0.125).astype(BF16)
              for hp in range(8)]
        dkacc[...] = jnp.zeros_like(dkacc)
        dvacc[...] = jnp.zeros_like(dvacc)
        gsum = jnp.zeros((1, BLOCK), F32)
        hlane = lax.broadcasted_iota(jnp.int32, (1, BLOCK), 1)
        for n in range(nq):
            first = (t == 0) & (n == 0)
            valid = band & (jnp.logical_not(first) | (sj >= BLOCK))
            r0 = n * BLOCK
            for g in range(2):
                cols = [slice(128 * (4 * g + j), 128 * (4 * g + j) + 128) for j in range(4)]
                lhs = jnp.concatenate([qs[4 * g + j][r0:r0 + BLOCK] for j in range(4)], axis=0)
                dov = jnp.concatenate([do_ref[r0:r0 + BLOCK, cs] for cs in cols], axis=0)
                prod = dov.astype(F32) * jnp.concatenate(
                    [o_ref[r0:r0 + BLOCK, cs] for cs in cols], axis=0).astype(F32)
                lhs_t = lhs.T
                dov_t = dov.T
                dq = jnp.zeros((4 * BLOCK, BLOCK), F32)
                dk_t = jnp.zeros((HEAD_DIM, 2 * BLOCK), F32)
                dv_t = jnp.zeros((HEAD_DIM, 2 * BLOCK), F32)
                for e in range(2):
                    kw = kx[g][e][r0:r0 + 2 * BLOCK]
                    vw = vx[g][e][r0:r0 + 2 * BLOCK]
                    s = _mm_nt(lhs, kw)
                    p, psink = _softmax_sink(s, valid, _sink_rep(sink_ref, g, e))
                    pe = jnp.where(lo if e == 0 else jnp.logical_not(lo), prod, 0.0)
                    pe_hi = pe.astype(BF16)
                    pe_lo = (pe - pe_hi.astype(F32)).astype(BF16)
                    delta = _mm(jnp.concatenate([pe_hi, pe_lo], axis=1), ones)
                    ds = (p * (_mm_nt(dov, vw) - delta)).astype(BF16)
                    dq = dq + _mm(ds, kw)
                    dims = slice(HEAD_DIM * e, HEAD_DIM * (e + 1))
                    dk_t = dk_t + _mm(lhs_t[dims], ds)
                    dv_t = dv_t + _mm(dov_t[dims], p.astype(BF16))
                    gs = -psink * delta[:, 0:BLOCK]
                    for j in range(4):
                        tot = jnp.sum(gs[j * BLOCK:(j + 1) * BLOCK], axis=0, keepdims=True)
                        gsum = gsum + jnp.where(hlane == 8 * g + 2 * j + e, tot, 0.0)
                dkacc[HEAD_DIM * g:HEAD_DIM * (g + 1), r0:r0 + 2 * BLOCK] += dk_t
                dvacc[HEAD_DIM * g:HEAD_DIM * (g + 1), r0:r0 + 2 * BLOCK] += dv_t
                for j in range(4):
                    dqj = _rope_f32(dq[j * BLOCK:(j + 1) * BLOCK] * 0.125, cos[r0:r0 + BLOCK],
                                    -sin[r0:r0 + BLOCK], pswap)
                    dq_ref[r0:r0 + BLOCK, cols[j]] = dqj.astype(BF16)
        gs_ref[0:1, :] += gsum
        dk_all = dkacc[...]
        dv_all = dvacc[...]
        dk_last = dk_all[:, tq:tq + BLOCK] + carry[0:BLOCK, :]
        dv_last = dv_all[:, tq:tq + BLOCK] + carry[BLOCK:2 * BLOCK, :]
        carry[0:BLOCK, :] = dk_all[:, 0:BLOCK]
        carry[BLOCK:2 * BLOCK, :] = dv_all[:, 0:BLOCK]
        if nq > 1:
            dk_tile = jnp.concatenate([dk_all[:, BLOCK:tq], dk_last], axis=1)
            dv_tile = jnp.concatenate([dv_all[:, BLOCK:tq], dv_last], axis=1)
        else:
            dk_tile, dv_tile = dk_last, dv_last
        dkv_ref[:, 0:BLOCK] = _rope_f32(dk_tile.T, cos, -sin, pswap).astype(BF16)
        dkv_ref[:, BLOCK:2 * BLOCK] = dv_tile.T.astype(BF16)

    def row(b, tt):
        return b * nt + (nt - 1 - tt)

    def halo(b, tt):
        return jnp.maximum(row(b, tt) * nq - 1, 0)

    tile = pl.BlockSpec((tq, D), lambda b, tt: (row(b, tt), 0))
    return pl.pallas_call(
        body, name="attn_bwd", grid=(T // S, nt),
        in_specs=[pl.BlockSpec(memory_space=pltpu.SMEM),
                  pl.BlockSpec((tq, D), lambda b, tt: (row(b, tt), ZB_Q)),
                  pl.BlockSpec((tq, 2 * BLOCK), lambda b, tt: (row(b, tt), 0)),
                  pl.BlockSpec((BLOCK, 2 * BLOCK), lambda b, tt: (halo(b, tt), 0)),
                  tile, tile,
                  pl.BlockSpec((tq, BLOCK), lambda b, tt: (row(b, tt), 0)),
                  pl.BlockSpec((tq, BLOCK), lambda b, tt: (row(b, tt), 0)),
                  pl.BlockSpec((BLOCK, BLOCK), lambda b, tt: (halo(b, tt), 0)),
                  pl.BlockSpec((BLOCK, BLOCK), lambda b, tt: (halo(b, tt), 0)),
                  pl.BlockSpec(memory_space=pl.ANY)],
        out_specs=(pl.BlockSpec((tq, D), lambda b, tt: (row(b, tt), ZB_Q)),
                   pl.BlockSpec((tq, 2 * BLOCK), lambda b, tt: (row(b, tt), 0)),
                   pl.BlockSpec((8, BLOCK), lambda b, tt: (0, 0))),
        out_shape=(jax.ShapeDtypeStruct(dz.shape, BF16), jax.ShapeDtypeStruct((T, 2 * BLOCK), BF16),
                   jax.ShapeDtypeStruct((8, BLOCK), F32)),
        input_output_aliases={10: 0},
        scratch_shapes=[pltpu.VMEM((2 * BLOCK, BLOCK), F32), pltpu.VMEM((BLOCK, tq + BLOCK), F32),
                        pltpu.VMEM((BLOCK, tq + BLOCK), F32)],
        compiler_params=_params(("arbitrary", "arbitrary")),
    )(sinks, z, zkv, zkv, o, do, cos_t, sin_t, cos_t, sin_t, dz)


def _exchange_copies(g_ref, r1_ref, send_sems, recv_sems):
    x, y, c = _coords()
    return [pltpu.make_async_remote_copy(
        src_ref=g_ref.at[:, pl.ds(pl.multiple_of((1 - c) * HALF_ROWS, 32), HALF_ROWS), :], dst_ref=r1_ref,
        send_sem=send_sems.at[0], recv_sem=recv_sems.at[0], device_id=(x, y, 1 - c), device_id_type=MESH)]


def _chip_sum_copies(cs_ref, r2_ref, send_sems, recv_sems):
    x, y, c = _coords()
    return [pltpu.make_async_remote_copy(
        src_ref=cs_ref.at[2 * px + py], dst_ref=r2_ref.at[k], send_sem=send_sems.at[k],
        recv_sem=recv_sems.at[k], device_id=(px, py, c), device_id_type=MESH)
        for k, (px, py) in enumerate(_chip_peers(x, y))]


def _dh(dz, dz_kv, wall, x, dx1, ln_pre, tm, tile0, ntiles, gx_prev, name, copies, src, landing):
    T = x.shape[0]
    nsem = 3

    def body(*refs):
        dz_ref, kv_ref, wt_ref, x_ref, dx1_ref, g_ref, src_ref = refs[:7]
        gx_ref, glp_ref, land_ref, wbuf, send_sems, recv_sems, wsem = refs[-7:]
        i = pl.program_id(0)

        @pl.when(i == 0)
        def _():
            glp_ref[...] = jnp.zeros_like(glp_ref)
            for cp in copies(src_ref, land_ref, send_sems, recv_sems):
                cp.start()
            load = pltpu.make_async_copy(wt_ref, wbuf, wsem)
            load.start()
            load.wait()

        dh = _mm(dz_ref[...], wbuf[0:ZKV, :]) + _mm(kv_ref[...], wbuf[ZKV:IN_WIDTH, :])
        xv = x_ref[...]
        r = lax.rsqrt(jnp.mean(xv * xv, axis=-1, keepdims=True) + EPS)
        xr = xv * r
        glp_ref[...] += jnp.sum(dh * xr, axis=0, keepdims=True)
        a = dh * g_ref[...]
        gx_ref[...] = dx1_ref[...] + r * (a - xr * jnp.mean(a * xr, axis=-1, keepdims=True))

        @pl.when(i == ntiles - 1)
        def _():
            cps = copies(src_ref, land_ref, send_sems, recv_sems)
            for cp in cps:
                cp.wait_recv()
            for cp in cps:
                cp.wait_send()

    tile = pl.BlockSpec((tm, D), lambda i: (tile0 + i, 0))
    any_spec = pl.BlockSpec(memory_space=pl.ANY)
    operands = [dz, dz_kv, wall, x, dx1, ln_pre, src] + ([] if gx_prev is None else [gx_prev])
    return pl.pallas_call(
        body, name=name, grid=(ntiles,),
        in_specs=[pl.BlockSpec((tm, ZKV), lambda i: (tile0 + i, 0)),
                  pl.BlockSpec((tm, 2 * BLOCK), lambda i: (tile0 + i, 0)),
                  any_spec, tile, tile, pl.BlockSpec((1, D), lambda i: (0, 0)), any_spec]
        + ([] if gx_prev is None else [any_spec]),
        out_specs=(tile, pl.BlockSpec((1, D), lambda i: (0, 0)), any_spec),
        out_shape=(jax.ShapeDtypeStruct((T, D), F32), jax.ShapeDtypeStruct((1, D), F32), landing),
        input_output_aliases={} if gx_prev is None else {7: 0},
        scratch_shapes=[pltpu.VMEM((IN_WIDTH, D), BF16), pltpu.SemaphoreType.DMA((nsem,)),
                        pltpu.SemaphoreType.DMA((nsem,)), pltpu.SemaphoreType.DMA],
        compiler_params=_params(("arbitrary",)),
    )(*operands)


def _gwt(dz, h, gpack, tt):
    T = dz.shape[0]
    last = T // tt - 1

    def body(dz_ref, h_ref, gpack_in, gpack_ref, acc, sem):
        j = pl.program_id(0)
        t = pl.program_id(1)

        @pl.when(t == 0)
        def _():
            acc[...] = _mm_tn(dz_ref[...], h_ref[...])

        @pl.when(t > 0)
        def _():
            acc[...] += _mm_tn(dz_ref[...], h_ref[...])

        for jj in range(7):
            @pl.when((t == last) & (j == jj))
            def _(jj=jj):
                _flush_to_pack(acc, gpack_ref, WT0 + jj * D, sem)

    any_spec = pl.BlockSpec(memory_space=pl.ANY)
    return pl.pallas_call(
        body, name="gwt", grid=(7, T // tt),
        in_specs=[pl.BlockSpec((tt, D), lambda j, t: (t, j)), pl.BlockSpec((tt, D), lambda j, t: (t, 0)),
                  any_spec],
        out_specs=any_spec, out_shape=jax.ShapeDtypeStruct(gpack.shape, F32), input_output_aliases={2: 0},
        scratch_shapes=[pltpu.VMEM((D, D), F32), pltpu.SemaphoreType.DMA],
        compiler_params=_params(("arbitrary", "arbitrary")),
    )(dz, h, gpack)


def _gwt_kv(dz_kv, h, gppt, gpack, tt):
    T = dz_kv.shape[0]
    last = T // tt - 1

    def body(dz_ref, h_ref, gppt_ref, gpack_in, gpack_ref, acc, sem):
        t = pl.program_id(0)

        @pl.when(t == 0)
        def _():
            acc[...] = _mm_tn(dz_ref[...], h_ref[...])

        @pl.when(t > 0)
        def _():
            acc[...] += _mm_tn(dz_ref[...], h_ref[...])

        @pl.when(t == last)
        def _():
            _flush_to_pack(acc, gpack_ref, WT0 + ZKV, sem)
            _flush_to_pack(gppt_ref, gpack_ref, WPP0, sem)

    any_spec = pl.BlockSpec(memory_space=pl.ANY)
    return pl.pallas_call(
        body, name="gwt_kv", grid=(T // tt,),
        in_specs=[pl.BlockSpec((tt, 2 * BLOCK), lambda t: (t, 0)), pl.BlockSpec((tt, D), lambda t: (t, 0)),
                  pl.BlockSpec((PLE, D), lambda t: (0, 0)), any_spec],
        out_specs=any_spec, out_shape=jax.ShapeDtypeStruct(gpack.shape, F32), input_output_aliases={3: 0},
        scratch_shapes=[pltpu.VMEM((2 * BLOCK, D), F32), pltpu.SemaphoreType.DMA],
        compiler_params=_params(("arbitrary",)),
    )(dz_kv, h, gppt, gpack)


_BC1 = 1.0 - ADAM_B1 ** ADAM_STEP
_BC2 = 1.0 - ADAM_B2 ** ADAM_STEP


def _adamw_math(w, g, m, v):
    m = ADAM_B1 * m + (1.0 - ADAM_B1) * g
    v = ADAM_B2 * v + (1.0 - ADAM_B2) * (g * g)
    delta = -ADAM_LR * ((m / _BC1) / (jnp.sqrt(v / _BC2) + ADAM_EPS) + ADAM_WD * w)
    return delta, m, v


def _adamw_rows(g, w, m, v, rows, name):
    R, C = w.shape

    def body(g_ref, w_ref, m_ref, v_ref, go_ref, d_ref, nm_ref, nv_ref):
        gv = g_ref[...]
        d, nm, nv = _adamw_math(w_ref[...], gv, m_ref[...], v_ref[...])
        go_ref[...] = gv
        d_ref[...] = d
        nm_ref[...] = nm
        nv_ref[...] = nv

    spec = pl.BlockSpec((rows, C), lambda i: (i, 0))
    shp = jax.ShapeDtypeStruct((R, C), F32)
    return pl.pallas_call(
        body, name=name, grid=(R // rows,), in_specs=[spec] * 4, out_specs=(spec,) * 4,
        out_shape=(shp,) * 4, compiler_params=_params(("arbitrary",)),
    )(g, w, m, v)


def _adamw_square(gfin, ws, ms, vs):
    rb = 64
    nb = SQ_SHARD // rb

    def body(*refs):
        g_refs = refs[0:5]
        w_refs, m_refs, v_refs = refs[5:10], refs[10:15], refs[15:20]
        outs = refs[20:]
        for k in range(5):
            gk = g_refs[k][...]
            d, nm, nv = _adamw_math(w_refs[k][...], gk, m_refs[k][...], v_refs[k][...])
            outs[4 * k][...] = gk
            outs[4 * k + 1][...] = d
            outs[4 * k + 2][...] = nm
            outs[4 * k + 3][...] = nv

    spec = pl.BlockSpec((rb, D), lambda i: (i, 0))
    gspecs = [pl.BlockSpec((rb, D), lambda i, k=k: ((WIN_SHARD + SQ_SHARD * k) // rb + i, 0))
              for k in range(5)]
    shp = jax.ShapeDtypeStruct((SQ_SHARD, D), F32)
    res = pl.pallas_call(
        body, name="adamw_square", grid=(nb,), in_specs=gspecs + [spec] * 15, out_specs=(spec,) * 20,
        out_shape=(shp,) * 20, compiler_params=_params(("arbitrary",)),
    )(*([gfin] * 5), *ws, *ms, *vs)
    return [tuple(res[4 * k:4 * k + 4]) for k in range(5)]


def _adamw_small(gs, ws, ms, vs):
    n = len(gs)

    def body(*refs):
        outs = refs[4 * n:]
        for k in range(n):
            d, nm, nv = _adamw_math(refs[n + k][...], refs[k][...], refs[2 * n + k][...],
                                    refs[3 * n + k][...])
            outs[3 * k][...] = d
            outs[3 * k + 1][...] = nm
            outs[3 * k + 2][...] = nv

    vm = pl.BlockSpec(memory_space=pltpu.VMEM)
    shapes = []
    for w in ws:
        shapes += [jax.ShapeDtypeStruct(w.shape, F32)] * 3
    res = pl.pallas_call(
        body, name="adamw_small", in_specs=[vm] * (4 * n), out_specs=(vm,) * (3 * n),
        out_shape=tuple(shapes),
    )(*gs, *ws, *ms, *vs)
    return [tuple(res[3 * k:3 * k + 3]) for k in range(n)]


def _rope_tables(positions):
    inv = jnp.power(ROPE_THETA, -jnp.arange(0, ROPE_DIM, 2, dtype=F32) / ROPE_DIM)
    inv_h = jnp.concatenate([inv, inv, jnp.zeros((HEAD_DIM - ROPE_DIM,), F32)])
    sign_h = np.array([-1.0] * (ROPE_DIM // 2) + [1.0] * (ROPE_DIM // 2) + [0.0] * (HEAD_DIM - ROPE_DIM),
                      np.float32)
    ang = positions.astype(F32).reshape(-1, 1) * jnp.concatenate([inv_h, inv_h])[None, :]
    return jnp.cos(ang), jnp.sin(ang) * np.concatenate([sign_h, sign_h])[None, :]


def kernel(x, p, positions, w_in, ln_pre, ln_post, w_dw, b_dw, conv_ln_g, conv_ln_b, w_pw, sinks, w_br_conv, w_br_attn, w_out, w_ple_gate, w_ple_proj, loss_target, m_w_in, m_ln_pre, m_ln_post, m_w_dw, m_b_dw, m_conv_ln_g, m_conv_ln_b, m_w_pw, m_sinks, m_w_br_conv, m_w_br_attn, m_w_out, m_w_ple_gate, m_w_ple_proj, v_w_in, v_ln_pre, v_ln_post, v_w_dw, v_b_dw, v_conv_ln_g, v_conv_ln_b, v_w_pw, v_sinks, v_w_br_conv, v_w_br_attn, v_w_out, v_w_ple_gate, v_w_ple_proj):
    nb, S, _ = x.shape
    T = nb * S
    xc = lax.axis_index("x")
    yc = lax.axis_index("y")
    cc = lax.axis_index("c")
    shard = 2 * xc + yc

    sq_w = (w_pw, w_br_conv, w_br_attn, w_out, w_ple_gate)
    wdw_shard = jnp.pad(w_dw[0], ((0, 1), (0, 0)))
    wt, wdw_all = _gather_win(w_in[0].T.astype(BF16), wdw_shard)
    wdw = jnp.concatenate([wdw_all[s] for s in range(N_SHARDS)], axis=1)

    x2 = x.reshape(T, D)
    tgt = loss_target.reshape(T, D)
    p2 = p.reshape(T, PLE)
    cos_t, sin_t = _rope_tables(positions)
    sinks1 = sinks.reshape(N_HEADS)

    tm_big = min(TILE_PROJ, T)
    tm = min(TILE_TOKEN, S)
    tq = min(TILE_ATTN, S)

    tm_res = min(TILE_RESIDENT, T // 2)
    z, zkv, h, wall, wppf = _inproj(x2, ln_pre, wt, [w[0].astype(BF16) for w in sq_w],
                                    w_ple_proj[0].T.reshape(WPP_SHARD, D).astype(BF16), tm_res)
    wppt = wppf.reshape(D, PLE)
    ya, y, rstd, pw = _conv_fwd(z, wdw, b_dw, conv_ln_g, conv_ln_b, wall, S, tm)
    o = _attn_fwd(z, zkv, cos_t, sin_t, sinks1, S, tq)
    loss_p, dx1, dm, yb, g_ln_post, gpack, gw_ppt = _tail_a(x2, tgt, p2, o, ya, z, ln_post, wall, wppt, tm)

    dz, do, dc, gvec, gpack = _tail_b(dm, ya, yb, o, z, pw, y, rstd, conv_ln_g, conv_ln_b, wall, gpack, tm)
    dz, g_wdw = _conv_bwd(dc, z, wdw, dz, S, tm)
    dz, dkv, g_sinks = _attn_bwd(z, zkv, o, do, cos_t, sin_t, sinks1, dz, S, tq)
    gpack = _gwt(dz, h, gpack, min(2 * TILE_PROJ, T))
    gpack = _gwt_kv(dkv, h, gw_ppt.reshape(PLE, D), gpack, tm_big)

    cidx = jnp.reshape(cc, (1,)).astype(jnp.int32)
    scidx = jnp.stack([shard, cc]).astype(jnp.int32)
    tm_dh = tm_res
    n_dh = T // tm_dh
    n_a = max(1, n_dh // 4)
    gx, g_ln_pre_a, r1 = _dh(
        dz, dkv, wt, x2, dx1, ln_pre, tm_dh, 0, n_a, None, "dh_exchange", _exchange_copies, gpack,
        jax.ShapeDtypeStruct((N_SHARDS, HALF_ROWS, D), F32))
    cs = _chip_sum(cidx, gpack, r1)
    gx, g_ln_pre_b, r2 = _dh(
        dz, dkv, wt, x2, dx1, ln_pre, tm_dh, n_a, n_dh - n_a, gx, "dh_send", _chip_sum_copies, cs,
        jax.ShapeDtypeStruct((3, HALF_ROWS, D), BF16))
    g_ln_pre = g_ln_pre_a + g_ln_pre_b
    gfin = _swap_halves(_final_half(scidx, gpack, r1, r2))

    row37 = jnp.concatenate([g_sinks[0:1, 0:N_HEADS], loss_p, jnp.zeros((1, D - N_HEADS - 1), F32)], axis=1)
    vec = jnp.concatenate([g_wdw, g_ln_pre, g_ln_post, gvec[2:3], gvec[0:1], gvec[1:2], row37,
                           jnp.zeros((VEC_ROWS - 38, D), F32)], axis=0)
    tot = _all_reduce_small(vec)

    g_w_in, d_w_in, nm_w_in, nv_w_in = [a.T for a in _adamw_rows(
        gfin, w_in[0].T, m_w_in[0].T, v_w_in[0].T, WIN_SHARD // 8, "adamw_w_in")]
    g_w_in = g_w_in[None]
    sq_m = (m_w_pw, m_w_br_conv, m_w_br_attn, m_w_out, m_w_ple_gate)
    sq_v = (v_w_pw, v_w_br_conv, v_w_br_attn, v_w_out, v_w_ple_gate)
    sq_res = _adamw_square(gfin, [w[0] for w in sq_w], [m[0] for m in sq_m], [v[0] for v in sq_v])
    g_wpp = gfin[WIN_SHARD + 5 * SQ_SHARD:PACK_ROWS].reshape(PLE, PLE).T
    g_dw_all = tot[0:CONV_K]
    g_dw = lax.dynamic_slice_in_dim(g_dw_all, shard * PLE, PLE, axis=1)
    small_g = [g_wpp, g_dw, tot[32:33], tot[33:34], tot[34:35], tot[35:36], tot[36:37],
               tot[37:38, 0:N_HEADS]]
    small_w = [w_ple_proj[0], w_dw[0], ln_pre, ln_post, b_dw, conv_ln_g, conv_ln_b, sinks]
    small_m = [m_w_ple_proj[0], m_w_dw[0], m_ln_pre, m_ln_post, m_b_dw, m_conv_ln_g, m_conv_ln_b, m_sinks]
    small_v = [v_w_ple_proj[0], v_w_dw[0], v_ln_pre, v_ln_post, v_b_dw, v_conv_ln_g, v_conv_ln_b, v_sinks]
    small = _adamw_small(small_g, small_w, small_m, small_v)

    loss = tot[37, N_HEADS]
    grads = [g_w_in, small_g[2], small_g[3], g_dw[None], small_g[4], small_g[5], small_g[6],
             sq_res[0][0][None], small_g[7], sq_res[1][0][None], sq_res[2][0][None], sq_res[3][0][None],
             sq_res[4][0][None], g_wpp[None]]

    def triple(i):
        w_in_t = (d_w_in[None], nm_w_in[None], nv_w_in[None])
        sq = lambda k: tuple(a[None] for a in sq_res[k][1:4])
        sm = lambda k, lead: tuple(a[None] if lead else a for a in small[k])
        return [w_in_t[i], sm(2, False)[i], sm(3, False)[i], sm(1, True)[i], sm(4, False)[i],
                sm(5, False)[i], sm(6, False)[i], sq(0)[i], sm(7, False)[i], sq(1)[i], sq(2)[i], sq(3)[i],
                sq(4)[i], sm(0, True)[i]]

    return (loss, gx.reshape(nb, S, D), *grads, *triple(0), *triple(1), *triple(2))
```

```python
import functools

import jax
import jax.numpy as jnp
import numpy as np
from jax import lax
from jax.experimental import pallas as pl
from jax.experimental.pallas import tpu as pltpu

F32 = jnp.float32
BF16 = jnp.bfloat16

D = 1024
PLE = 256
N_HEADS = 16
HEAD_DIM = 64
BLOCK = 128
CONV_K = 31
ROPE_DIM = 16
ROPE_THETA = 500000.0
EPS = 1e-6
IN_WIDTH = 7424
N_SHARDS = 4

ADAM_LR = 0.001
ADAM_B1 = 0.9
ADAM_B2 = 0.999
ADAM_EPS = 1e-08
ADAM_WD = 0.01
ADAM_STEP = 10

SQ_NAMES = ("w_pw", "w_br_conv", "w_br_attn", "w_out", "w_ple_gate")
WT0 = 5 * D
WPP0 = WT0 + IN_WIDTH
WALL_ROWS = WPP0 + PLE
WIN_SHARD = IN_WIDTH // N_SHARDS
SQ_SHARD = D // N_SHARDS
WPP_SHARD = PLE * PLE // D
PACK_ROWS = WIN_SHARD + 5 * SQ_SHARD + WPP_SHARD
HALF_ROWS = PACK_ROWS // 2
VMEM_LIMIT = 56 * 1024 * 1024
MESH = pl.DeviceIdType.MESH
TILE_RESIDENT = 512
TILE_PROJ = 1024
TILE_TOKEN = 256
TILE_ATTN = 512
TAIL_PARTS = 1


ZB_AGATE, ZB_GCONV, ZB_GATTN, ZB_CGATE, ZB_CVAL, ZB_CGLU, ZB_Q = range(7)
ZKV = 7 * D
_SEGMENTS = ((0, D, ZB_CVAL * D), (D, D, ZB_CGLU * D), (2 * D, D, ZB_CGATE * D), (3 * D, D, ZB_Q * D),
             (4 * D, 2 * BLOCK, ZKV), (4 * D + 2 * BLOCK, D, ZB_AGATE * D),
             (5 * D + 2 * BLOCK, D, ZB_GCONV * D), (6 * D + 2 * BLOCK, D, ZB_GATTN * D))
_WT_CUTS = (0, 192, 640, 1216, WIN_SHARD)


def _zp_row(o):
    for a, w, zp in _SEGMENTS:
        if a <= o < a + w:
            return zp + o - a
    raise ValueError(o)


def _pieces(s):
    out = []
    for a, b in zip(_WT_CUTS[:-1], _WT_CUTS[1:]):
        first = _zp_row(WIN_SHARD * s + a)
        assert _zp_row(WIN_SHARD * s + b - 1) == first + b - a - 1
        out.append((a, b - a, WT0 + first))
    for k in range(5):
        out.append((WIN_SHARD + SQ_SHARD * k, SQ_SHARD, D * k + SQ_SHARD * s))
    out.append((WIN_SHARD + 5 * SQ_SHARD, WPP_SHARD, WPP0 + WPP_SHARD * s))
    return out


N_PIECES = len(_pieces(0))


def _wall_segments(wall0, rows):
    out = []
    for s in range(N_SHARDS):
        for pr, n, wr in _pieces(s):
            lo, hi = max(wr, wall0), min(wr + n, wall0 + rows)
            if lo < hi:
                out.append((lo - wall0, hi - lo, s, pr + lo - wr))
    assert sum(n for _, n, _, _ in out) == rows
    return out


def _sel(s, vals):
    r = jnp.int32(vals[0])
    for i in range(1, len(vals)):
        r = jnp.where(s == i, jnp.int32(vals[i]), r)
    return r


def _sig(x):
    return 1.0 / (1.0 + jnp.exp(-x))


def _mm(a, b):
    return lax.dot_general(a, b, (((1,), (0,)), ((), ())), preferred_element_type=F32)


def _mm_nt(a, b):
    return lax.dot_general(a, b, (((1,), (1,)), ((), ())), preferred_element_type=F32)


def _mm_tn(a, b):
    return lax.dot_general(a, b, (((0,), (0,)), ((), ())), preferred_element_type=F32)


def _params(sem=None):
    return pltpu.CompilerParams(dimension_semantics=sem, vmem_limit_bytes=VMEM_LIMIT)


def _flush_to_pack(acc_ref, gpack_ref, wall0, sem):
    for r, n, s, pr in _wall_segments(wall0, acc_ref.shape[0]):
        cp = pltpu.make_async_copy(acc_ref.at[pl.ds(r, n)], gpack_ref.at[s, pl.ds(pr, n)], sem)
        cp.start()
        cp.wait()


def _coords():
    return lax.axis_index("x"), lax.axis_index("y"), lax.axis_index("c")


def _chip_peers(x, y):
    return [(1 - x, y), (x, 1 - y), (1 - x, 1 - y)]


WIN_PIECES = tuple(range(len(_WT_CUTS) - 1))
SQ_PIECES = tuple(range(len(WIN_PIECES), N_PIECES))


def _gather_ops(group, src, landing, bytes_ref, stage, send_sems, recv_sems, loc_sem):
    sizes = [_pieces(0)[p][1] for p in group]
    half_rows = sum(n // 2 for n in sizes)

    def rcopy(a, b, k, dev):
        return pltpu.make_async_remote_copy(src_ref=a, dst_ref=b, send_sem=send_sems.at[k],
                                            recv_sem=recv_sems.at[k], device_id=dev, device_id_type=MESH)

    def total(k):
        x, y, c = _coords()
        rows = bytes_ref.at[pl.ds(0, half_rows)]
        return rcopy(rows, rows, k, (x, y, c))

    def send():
        x, y, c = _coords()
        s_me = 2 * x + y
        for k, (px, py) in enumerate(_chip_peers(x, y)):
            for p, n in zip(group, sizes):
                h = n // 2
                rcopy(src(p, c * h, h), landing(p, s_me, c * h, h), k, (px, py, c)).start()
        for p, n in zip(group, sizes):
            for a, b in ((src(p, 0, n), stage.at[pl.ds(0, n)]), (stage.at[pl.ds(0, n)], landing(p, s_me, 0, n))):
                cp = pltpu.make_async_copy(a, b, loc_sem)
                cp.start()
                cp.wait()

    def forward():
        x, y, c = _coords()
        for k, (px, py) in enumerate(_chip_peers(x, y)):
            total(k).wait_recv()
            for p, n in zip(group, sizes):
                rows = landing(p, 2 * px + py, c * (n // 2), n // 2)
                rcopy(rows, rows, 3 + k, (x, y, 1 - c)).start()

    def finish():
        for k in range(3):
            total(3 + k).wait_recv()
        for k in range(6):
            total(k).wait_send()

    return send, forward, finish


def _piece_rows(ref, start, off, n):
    first = start + off
    return ref.at[pl.ds(first if isinstance(first, int) else pl.multiple_of(first, 32), n)]


def _gather_win(win_t, wdw_shard, x, ln_pre, tm):
    tables = [[_pieces(s)[p][2] - WT0 for s in range(N_SHARDS)] for p in WIN_PIECES]
    T = x.shape[0]
    n_i = T // tm

    def wdw_copies(wdw_ref, wdwall_ref, send_sems, recv_sems):
        x_, y_, c_ = _coords()
        return [pltpu.make_async_remote_copy(
            src_ref=wdw_ref, dst_ref=wdwall_ref.at[2 * x_ + y_], send_sem=send_sems.at[6 + k],
            recv_sem=recv_sems.at[6 + k], device_id=(px, py, c_), device_id_type=MESH)
            for k, (px, py) in enumerate(_chip_peers(x_, y_))]

    def body(win_ref, wdw_ref, x_ref, g_ref, wt_ref, wdwall_ref, h_ref, stage, send_sems, recv_sems, loc_sems):
        i = pl.program_id(0)
        send, forward, finish = _gather_ops(
            WIN_PIECES, lambda p, off, n: _piece_rows(win_ref, _WT_CUTS[p], off, n),
            lambda p, s, off, n: _piece_rows(wt_ref, _sel(s, tables[p]), off, n),
            wt_ref, stage, send_sems, recv_sems, loc_sems.at[0])

        def own_wdw():
            x_, y_, _ = _coords()
            return pltpu.make_async_copy(wdw_ref, wdwall_ref.at[2 * x_ + y_], loc_sems.at[1])

        @pl.when(i == 0)
        def _():
            own_wdw().start()
            for cp in wdw_copies(wdw_ref, wdwall_ref, send_sems, recv_sems):
                cp.start()
            send()

        xv = x_ref[...]
        r = lax.rsqrt(jnp.mean(xv * xv, axis=-1, keepdims=True) + EPS)
        h_ref[...] = (xv * r * g_ref[...]).astype(BF16)

        @pl.when(i == n_i - 1)
        def _():
            forward()
            finish()
            cps = wdw_copies(wdw_ref, wdwall_ref, send_sems, recv_sems)
            for cp in cps:
                cp.wait_recv()
            for cp in cps:
                cp.wait_send()
            own_wdw().wait()

    any_spec = pl.BlockSpec(memory_space=pl.ANY)
    return pl.pallas_call(
        body, name="gather_win", grid=(n_i,),
        out_shape=(jax.ShapeDtypeStruct((IN_WIDTH, D), BF16), jax.ShapeDtypeStruct((N_SHARDS, 32, PLE), F32),
                   jax.ShapeDtypeStruct((T, D), BF16)),
        in_specs=[any_spec, any_spec, pl.BlockSpec((tm, D), lambda i: (i, 0)),
                  pl.BlockSpec((1, D), lambda i: (0, 0))],
        out_specs=(any_spec, any_spec, pl.BlockSpec((tm, D), lambda i: (i, 0))),
        scratch_shapes=[pltpu.VMEM((max(_pieces(0)[p][1] for p in WIN_PIECES), D), BF16),
                        pltpu.SemaphoreType.DMA((9,)), pltpu.SemaphoreType.DMA((9,)),
                        pltpu.SemaphoreType.DMA((2,))],
        compiler_params=_params(("arbitrary",)),
    )(win_t, wdw_shard, x, ln_pre)


RT = 320


def _chip_sum(cidx, gpack, r1):
    def body(c_ref, g_ref, r_ref, o_ref):
        o_ref[...] = (g_ref[...] + r_ref[...]).astype(BF16)

    nt = HALF_ROWS // RT
    return pl.pallas_call(
        body, name="chip_sum",
        grid_spec=pltpu.PrefetchScalarGridSpec(
            num_scalar_prefetch=1, grid=(N_SHARDS, nt),
            in_specs=[pl.BlockSpec((1, RT, D), lambda s, t, c: (s, c[0] * nt + t, 0)),
                      pl.BlockSpec((1, RT, D), lambda s, t, c: (s, t, 0))],
            out_specs=pl.BlockSpec((1, RT, D), lambda s, t, c: (s, t, 0))),
        out_shape=jax.ShapeDtypeStruct((N_SHARDS, HALF_ROWS, D), BF16),
        compiler_params=_params(("arbitrary", "arbitrary")),
    )(cidx, gpack, r1)


def _final_half(sc, gpack, r1, r2):
    def body(sc_ref, g_ref, r_ref, p_ref, o_ref):
        acc = g_ref[0] + r_ref[0]
        for k in range(3):
            acc = acc + p_ref[k].astype(F32)
        o_ref[...] = acc

    nt = HALF_ROWS // RT
    return pl.pallas_call(
        body, name="final_half",
        grid_spec=pltpu.PrefetchScalarGridSpec(
            num_scalar_prefetch=1, grid=(nt,),
            in_specs=[pl.BlockSpec((1, RT, D), lambda t, sc: (sc[0], sc[1] * nt + t, 0)),
                      pl.BlockSpec((1, RT, D), lambda t, sc: (sc[0], t, 0)),
                      pl.BlockSpec((3, RT, D), lambda t, sc: (0, t, 0))],
            out_specs=pl.BlockSpec((RT, D), lambda t, sc: (sc[1] * nt + t, 0))),
        out_shape=jax.ShapeDtypeStruct((PACK_ROWS, D), F32),
        compiler_params=_params(("arbitrary",)),
    )(sc, gpack, r1, r2)


VEC_ROWS = 40


def _finish_reduce(fh, vec):
    def body(f_ref, v_ref, o_ref, tot_ref, buf, send_sems, recv_sems):
        x, y, c = _coords()
        rows = pl.ds(pl.multiple_of(c * HALF_ROWS, 32), HALF_ROWS)
        swap = pltpu.make_async_remote_copy(
            src_ref=f_ref.at[rows], dst_ref=o_ref.at[rows], send_sem=send_sems.at[7],
            recv_sem=recv_sems.at[7], device_id=(x, y, 1 - c), device_id_type=MESH)
        swap.start()
        me = 4 * x + 2 * y + c
        buf[me] = v_ref[...]
        cps = []
        for r in range(1, 8):
            dx, dy, dc = (r >> 2) & 1, (r >> 1) & 1, r & 1
            peer = (1 - x if dx else x, 1 - y if dy else y, 1 - c if dc else c)
            cp = pltpu.make_async_remote_copy(
                src_ref=v_ref, dst_ref=buf.at[me], send_sem=send_sems.at[r - 1],
                recv_sem=recv_sems.at[r - 1], device_id=peer, device_id_type=MESH)
            cp.start()
            cps.append(cp)
        for cp in cps:
            cp.wait_recv()
        for cp in cps:
            cp.wait_send()
        acc = buf[0]
        for d in range(1, 8):
            acc = acc + buf[d]
        tot_ref[...] = acc
        swap.wait()

    any_spec = pl.BlockSpec(memory_space=pl.ANY)
    vm = pl.BlockSpec(memory_space=pltpu.VMEM)
    return pl.pallas_call(
        body, name="finish_reduce",
        out_shape=(jax.ShapeDtypeStruct((PACK_ROWS, D), F32), jax.ShapeDtypeStruct((VEC_ROWS, D), F32)),
        in_specs=[any_spec, vm], out_specs=(any_spec, vm), input_output_aliases={0: 0},
        scratch_shapes=[pltpu.VMEM((8, VEC_ROWS, D), F32), pltpu.SemaphoreType.DMA((8,)),
                        pltpu.SemaphoreType.DMA((8,))],
    )(fh, vec)


def _inproj(h, wt, sq_shards, wpp_shard, tm):
    T = h.shape[0]
    nsq = len(sq_shards)
    n_i = T // tm

    def body(*refs):
        h_ref, wt_ref = refs[:2]
        sq_refs = refs[2:2 + nsq]
        wpp_ref = refs[2 + nsq]
        z_ref, zkv_ref, wsq_ref, wppf_ref, wbuf, stage, send_sems, recv_sems, loc_sem, wsem = refs[3 + nsq:]
        i = pl.program_id(0)

        def src(p, off, n):
            k = p - SQ_PIECES[0]
            return _piece_rows(wpp_ref if k == nsq else sq_refs[k], 0, off, n)

        def landing(p, s, off, n):
            k = p - SQ_PIECES[0]
            if k == nsq:
                return _piece_rows(wppf_ref, WPP_SHARD * s, off, n)
            return _piece_rows(wsq_ref, D * k + SQ_SHARD * s, off, n)

        send, forward, finish = _gather_ops(SQ_PIECES, src, landing, wsq_ref, stage, send_sems, recv_sems,
                                            loc_sem)

        @pl.when(i == 0)
        def _():
            send()
            load = pltpu.make_async_copy(wt_ref, wbuf, wsem)
            load.start()
            load.wait()

        @pl.when(i == n_i // 2)
        def _():
            forward()

        h = h_ref[...]
        for j in range(7):
            z_ref[:, j * D:(j + 1) * D] = _mm_nt(h, wbuf[j * D:(j + 1) * D, :]).astype(BF16)
        zkv_ref[...] = _mm_nt(h, wbuf[ZKV:IN_WIDTH, :]).astype(BF16)

        @pl.when(i == n_i - 1)
        def _():
            finish()

    any_spec = pl.BlockSpec(memory_space=pl.ANY)
    return pl.pallas_call(
        body, name="inproj", grid=(n_i,),
        in_specs=[pl.BlockSpec((tm, D), lambda i: (i, 0))] + [any_spec] * (nsq + 2),
        out_specs=(pl.BlockSpec((tm, ZKV), lambda i: (i, 0)), pl.BlockSpec((tm, 2 * BLOCK), lambda i: (i, 0)),
                   any_spec, any_spec),
        out_shape=(jax.ShapeDtypeStruct((T, ZKV), BF16), jax.ShapeDtypeStruct((T, 2 * BLOCK), BF16),
                   jax.ShapeDtypeStruct((nsq * D, D), BF16), jax.ShapeDtypeStruct((PLE, D), BF16)),
        scratch_shapes=[pltpu.VMEM((IN_WIDTH, D), BF16), pltpu.VMEM((SQ_SHARD, D), BF16),
                        pltpu.SemaphoreType.DMA((6,)), pltpu.SemaphoreType.DMA((6,)), pltpu.SemaphoreType.DMA,
                        pltpu.SemaphoreType.DMA],
        compiler_params=_params(("arbitrary",)),
    )(h, wt, *sq_shards, wpp_shard)


HALO = 32
CONV_RC = 64
CONV_LC = 256


def _conv_taps(w_ref, src, r0, lane0, offset_of_tap):
    lanes = pl.ds(lane0, CONV_LC)
    out = None
    for b in range(8):
        taps = [k for k in range(CONV_K) if offset_of_tap(k) % 8 == b]
        if not taps:
            continue
        rows = CONV_RC + (8 if b else 0)
        vb = None
        for k in taps:
            term = w_ref[k:k + 1, lanes] * src[pl.ds(r0 + (offset_of_tap(k) - b), rows), lanes]
            vb = term if vb is None else vb + term
        vb = vb[b:b + CONV_RC] if b else vb
        out = vb if out is None else out + vb
    return out


def _conv_fwd(z, wdw, b_dw, ln_g, ln_b, wall, S, tm):
    T = z.shape[0]
    nt = S // tm
    hb = tm // HALO

    def body(cv_ref, cg_ref, cgate_ref, hcv_ref, hcg_ref, wdw_ref, bdw_ref, lng_ref, lnb_ref, wpw_ref,
             wbrc_ref, ya_ref, y_ref, rstd_ref, pw_ref, ubuf, cbuf):
        t = pl.program_id(1)
        ubuf[HALO:HALO + tm, :] = cv_ref[...].astype(F32) * _sig(cg_ref[...].astype(F32))
        hu = hcv_ref[...].astype(F32) * _sig(hcg_ref[...].astype(F32))
        ubuf[0:HALO, :] = jnp.where(t > 0, hu, 0.0)
        ubuf[HALO + tm:HALO + tm + 8, :] = jnp.zeros((8, D), F32)

        def chunk(ci, carry):
            r0 = pl.multiple_of(ci * CONV_RC, CONV_RC)
            for lg in range(D // CONV_LC):
                acc = _conv_taps(wdw_ref, ubuf, r0, lg * CONV_LC, lambda k: HALO - (CONV_K - 1) + k)
                cbuf[pl.ds(r0, CONV_RC), pl.ds(lg * CONV_LC, CONV_LC)] = acc
            return carry

        lax.fori_loop(0, tm // CONV_RC, chunk, 0)
        cc = cbuf[...] + bdw_ref[...]
        mu = jnp.mean(cc, axis=-1, keepdims=True)
        dd = cc - mu
        rstd = lax.rsqrt(jnp.mean(dd * dd, axis=-1, keepdims=True) + EPS)
        yn = dd * rstd
        y_ref[...] = yn.astype(BF16)
        rstd_ref[...] = rstd
        n = yn * lng_ref[...] + lnb_ref[...]
        s = n * _sig(n)
        pw = _mm(s.astype(BF16), wpw_ref[...])
        pw_ref[...] = pw.astype(BF16)
        gt = cgate_ref[...].astype(F32)
        ya_in = pw * (gt * _sig(gt))
        ya_ref[...] = _mm(ya_in.astype(BF16), wbrc_ref[...]).astype(BF16)

    def row(b, t):
        return b * nt + t

    def halo(b, t):
        return jnp.maximum(row(b, t) * hb - 1, 0)

    vec = pl.BlockSpec((1, D), lambda b, t: (0, 0))
    tile = lambda j: pl.BlockSpec((tm, D), lambda b, t: (row(b, t), j))
    out_tile = pl.BlockSpec((tm, D), lambda b, t: (row(b, t), 0))
    return pl.pallas_call(
        body, name="conv_fwd", grid=(T // S, nt),
        in_specs=[tile(ZB_CVAL), tile(ZB_CGLU), tile(ZB_CGATE),
                  pl.BlockSpec((HALO, D), lambda b, t: (halo(b, t), ZB_CVAL)),
                  pl.BlockSpec((HALO, D), lambda b, t: (halo(b, t), ZB_CGLU)),
                  pl.BlockSpec((32, D), lambda b, t: (0, 0)), vec, vec, vec,
                  pl.BlockSpec((D, D), lambda b, t: (0, 0)),
                  pl.BlockSpec((D, D), lambda b, t: (1, 0))],
        out_specs=(out_tile, out_tile, pl.BlockSpec((tm, 1), lambda b, t: (row(b, t), 0)), out_tile),
        out_shape=(jax.ShapeDtypeStruct((T, D), BF16), jax.ShapeDtypeStruct((T, D), BF16),
                   jax.ShapeDtypeStruct((T, 1), F32), jax.ShapeDtypeStruct((T, D), BF16)),
        scratch_shapes=[pltpu.VMEM((tm + HALO + 8, D), F32), pltpu.VMEM((tm, D), F32)],
        compiler_params=_params(("arbitrary", "arbitrary")),
    )(z, z, z, z, z, wdw, b_dw, ln_g, ln_b, wall, wall)


def _swap_matrix():
    r = lax.broadcasted_iota(jnp.int32, (BLOCK, BLOCK), 0)
    l = lax.broadcasted_iota(jnp.int32, (BLOCK, BLOCK), 1)
    lh = l & (HEAD_DIM - 1)
    half = ROPE_DIM // 2
    hit = ((lh < half) & (r == l + half)) | ((lh >= half) & (lh < ROPE_DIM) & (r == l - half))
    return jnp.where(hit, 1.0, 0.0).astype(BF16)


def _rope(tb, cos, sin, pswap):
    return tb.astype(F32) * cos + _mm(tb, pswap) * sin


def _rope_f32(tv, cos, sin, pswap):
    hi = tv.astype(BF16)
    lo = (tv - hi.astype(F32)).astype(BF16)
    return tv * cos + (_mm(hi, pswap) + _mm(lo, pswap)) * sin


def _kv_variants(kv):
    lane = lax.broadcasted_iota(jnp.int32, kv.shape, 1)
    lo = lane < HEAD_DIM
    sw = pltpu.roll(kv, HEAD_DIM, 1)
    z = jnp.zeros_like(kv)
    g0 = (jnp.where(lo, kv, z).astype(BF16), jnp.where(lo, z, sw).astype(BF16))
    g1 = (jnp.where(lo, sw, z).astype(BF16), jnp.where(lo, z, kv).astype(BF16))
    return (g0, g1)


def _band_mask(nq):
    qi = lax.broadcasted_iota(jnp.int32, (nq * BLOCK, 2 * BLOCK), 0) & (BLOCK - 1)
    sj = lax.broadcasted_iota(jnp.int32, (nq * BLOCK, 2 * BLOCK), 1)
    return (sj <= qi + BLOCK) & (sj > qi), sj


def _sink_rep(sink_ref, g, e):
    return jnp.concatenate(
        [jnp.full((BLOCK, BLOCK), sink_ref[8 * g + 2 * j + e], F32) for j in range(4)], axis=0)


def _softmax_parts(s, valid, sk):
    rows = s.shape[0]
    s = jnp.where(valid, s, -1e30)
    m = jnp.maximum(jnp.broadcast_to(jnp.max(s, axis=-1, keepdims=True), (rows, BLOCK)), sk)
    return jnp.exp(s - jnp.concatenate([m, m], axis=1)), jnp.exp(sk - m)


def _softmax_sink(s, valid, sk):
    p, ps = _softmax_parts(s, valid, sk)
    inv = 1.0 / (_mm(p.astype(BF16), jnp.ones((2 * BLOCK, BLOCK), BF16)) + ps)
    return p * jnp.concatenate([inv, inv], axis=1), ps * inv


def _attn_fwd(z, zkv, cos_t, sin_t, sinks, S, tq):
    T = z.shape[0]
    nt = S // tq
    nq = tq // BLOCK

    def body(sink_ref, q_ref, kv_ref, hkv_ref, cos_ref, sin_ref, hcos_ref, hsin_ref, o_ref):
        t = pl.program_id(1)
        cos = cos_ref[...]
        sin = sin_ref[...]
        pswap = _swap_matrix()
        kv = jnp.concatenate([hkv_ref[...], kv_ref[...]], axis=0)
        cos_k = jnp.concatenate([hcos_ref[...], cos], axis=0)
        sin_k = jnp.concatenate([hsin_ref[...], sin], axis=0)
        kx = _kv_variants(_rope(kv[:, :BLOCK], cos_k, sin_k, pswap))
        one = jnp.ones((tq + BLOCK, BLOCK), BF16)
        vx = [[jnp.concatenate([v, one], axis=1) for v in vg] for vg in _kv_variants(kv[:, BLOCK:].astype(F32))]
        band, sj = _band_mask(4)
        qs = [(_rope(q_ref[:, 128 * hp:128 * hp + 128], cos, sin, pswap) * 0.125).astype(BF16)
              for hp in range(8)]
        for n in range(nq):
            first = (t == 0) & (n == 0)
            valid = band & (jnp.logical_not(first) | (sj >= BLOCK))
            r0 = n * BLOCK
            for g in range(2):
                lhs = jnp.concatenate([qs[4 * g + j][r0:r0 + BLOCK] for j in range(4)], axis=0)
                acc = jnp.zeros((4 * BLOCK, BLOCK), F32)
                for e in range(2):
                    s = _mm_nt(lhs, kx[g][e][r0:r0 + 2 * BLOCK])
                    p, ps = _softmax_parts(s, valid, _sink_rep(sink_ref, g, e))
                    r = _mm(p.astype(BF16), vx[g][e][r0:r0 + 2 * BLOCK])
                    acc = acc + r[:, 0:BLOCK] * (1.0 / (r[:, BLOCK:2 * BLOCK] + ps))
                for j in range(4):
                    o_ref[r0:r0 + BLOCK, 128 * (4 * g + j):128 * (4 * g + j) + 128] = (
                        acc[j * BLOCK:(j + 1) * BLOCK].astype(BF16))

    def row(b, t):
        return b * nt + t

    def halo(b, t):
        return jnp.maximum(row(b, t) * nq - 1, 0)

    return pl.pallas_call(
        body, name="attn_fwd", grid=(T // S, nt),
        in_specs=[pl.BlockSpec(memory_space=pltpu.SMEM),
                  pl.BlockSpec((tq, D), lambda b, t: (row(b, t), ZB_Q)),
                  pl.BlockSpec((tq, 2 * BLOCK), lambda b, t: (row(b, t), 0)),
                  pl.BlockSpec((BLOCK, 2 * BLOCK), lambda b, t: (halo(b, t), 0)),
                  pl.BlockSpec((tq, BLOCK), lambda b, t: (row(b, t), 0)),
                  pl.BlockSpec((tq, BLOCK), lambda b, t: (row(b, t), 0)),
                  pl.BlockSpec((BLOCK, BLOCK), lambda b, t: (halo(b, t), 0)),
                  pl.BlockSpec((BLOCK, BLOCK), lambda b, t: (halo(b, t), 0))],
        out_specs=pl.BlockSpec((tq, D), lambda b, t: (row(b, t), 0)),
        out_shape=jax.ShapeDtypeStruct((T, D), BF16),
        compiler_params=_params(("arbitrary", "arbitrary")),
    )(sinks, z, zkv, zkv, cos_t, sin_t, cos_t, sin_t)


def _tail_a(x, tgt, p, o, ya, z, ln_post, wall, wppt, tm):
    T = x.shape[0]
    last = T // tm - 1

    def body(x_ref, tgt_ref, p_ref, o_ref, ya_ref, ag_ref, gc_ref, ga_ref, lnp_ref, wbra_ref, wout_ref,
             wpg_ref, wppt_ref, loss_ref, dx1_ref, dm_ref, yb_ref, glnp_ref, gpack_ref, gwpp_ref,
             acc_out, acc_pg, sem):
        i = pl.program_id(0)

        @pl.when(i == 0)
        def _():
            acc_out[...] = jnp.zeros_like(acc_out)
            acc_pg[...] = jnp.zeros_like(acc_pg)
            gwpp_ref[...] = jnp.zeros_like(gwpp_ref)
            glnp_ref[...] = jnp.zeros_like(glnp_ref)
            loss_ref[...] = jnp.zeros_like(loss_ref)

        ag = ag_ref[...].astype(F32)
        yb_in = (o_ref[...].astype(F32) * (ag * _sig(ag))).astype(BF16)
        yb = _mm(yb_in, wbra_ref[...])
        yb_ref[...] = yb.astype(BF16)
        m = (_sig(gc_ref[...].astype(F32)) * ya_ref[...].astype(F32)
             + _sig(ga_ref[...].astype(F32)) * yb).astype(BF16)
        mo = _mm(m, wout_ref[...])
        r2 = lax.rsqrt(jnp.mean(mo * mo, axis=-1, keepdims=True) + EPS)
        nrm = mo * r2
        g_post = lnp_ref[...]
        x1 = x_ref[...] + nrm * g_post
        x1b = x1.astype(BF16)
        gate = _sig(_mm(x1b, wpg_ref[...]))
        pb = p_ref[...].astype(BF16)
        pp = _mm_nt(pb, wppt_ref[...])
        err = x1 + gate * pp - tgt_ref[...]
        loss_ref[...] += 0.5 * jnp.sum(jnp.sum(err * err, axis=-1, keepdims=True) * (1.0 / D),
                                       axis=0, keepdims=True)
        dx2 = err * (1.0 / D)
        dgp = (dx2 * pp * gate * (1.0 - gate)).astype(BF16)
        dpp = (dx2 * gate).astype(BF16)
        dx1 = dx2 + _mm_nt(dgp, wpg_ref[...])
        dx1_ref[...] = dx1
        acc_pg[...] += _mm_tn(x1b, dgp)
        gwpp_ref[...] += _mm_tn(dpp, pb)
        glnp_ref[...] += jnp.sum(dx1 * nrm, axis=0, keepdims=True)
        a = dx1 * g_post
        dmo = (r2 * (a - nrm * jnp.mean(a * nrm, axis=-1, keepdims=True))).astype(BF16)
        dm_ref[...] = _mm_nt(dmo, wout_ref[...]).astype(BF16)
        acc_out[...] += _mm_tn(m, dmo)

        @pl.when(i == last)
        def _():
            _flush_to_pack(acc_out, gpack_ref, 3 * D, sem.at[0])
            _flush_to_pack(acc_pg, gpack_ref, 4 * D, sem.at[1])

    tile = pl.BlockSpec((tm, D), lambda i: (i, 0))
    ztile = lambda j: pl.BlockSpec((tm, D), lambda i: (i, j))
    wsq = lambda k: pl.BlockSpec((D, D), lambda i: (k, 0))
    const = lambda shp: pl.BlockSpec(shp, lambda i: (0, 0))
    any_spec = pl.BlockSpec(memory_space=pl.ANY)
    return pl.pallas_call(
        body, name="tail_a", grid=(T // tm,),
        in_specs=[tile, tile, pl.BlockSpec((tm, PLE), lambda i: (i, 0)), tile, tile, ztile(ZB_AGATE),
                  ztile(ZB_GCONV), ztile(ZB_GATTN), const((1, D)), wsq(2), wsq(3), wsq(4), const((D, PLE))],
        out_specs=(const((1, 1)), tile, tile, tile, const((1, D)), any_spec, const((D, PLE))),
        out_shape=(jax.ShapeDtypeStruct((1, 1), F32), jax.ShapeDtypeStruct((T, D), F32),
                   jax.ShapeDtypeStruct((T, D), BF16), jax.ShapeDtypeStruct((T, D), BF16),
                   jax.ShapeDtypeStruct((1, D), F32), jax.ShapeDtypeStruct((N_SHARDS, PACK_ROWS, D), F32),
                   jax.ShapeDtypeStruct((D, PLE), F32)),
        scratch_shapes=[pltpu.VMEM((D, D), F32), pltpu.VMEM((D, D), F32), pltpu.SemaphoreType.DMA((2,))],
        compiler_params=_params(("arbitrary",)),
    )(x, tgt, p, o, ya, z, z, z, ln_post, wall, wall, wall, wppt)


def _dsilu(v, sg):
    return sg * (1.0 + v * (1.0 - sg))


def _tail_b(dm, ya, yb, o, z, pw, y, rstd, ln_g, ln_b, wall, gpack, tm):
    T = dm.shape[0]
    last = T // tm - 1

    def body(dm_ref, ya_ref, yb_ref, o_ref, ag_ref, gc_ref, ga_ref, cgate_ref, pw_ref, y_ref, rstd_ref,
             lng_ref, lnb_ref, wpw_ref, wbrc_ref, wbra_ref, gpack_in, dg_ref, do_ref, dc_ref, gvec_ref,
             gpack_ref, acc_bra, acc_brc, acc_pw, sem):
        i = pl.program_id(0)

        @pl.when(i == 0)
        def _():
            acc_bra[...] = jnp.zeros_like(acc_bra)
            acc_brc[...] = jnp.zeros_like(acc_brc)
            acc_pw[...] = jnp.zeros_like(acc_pw)
            gvec_ref[...] = jnp.zeros_like(gvec_ref)

        g = lng_ref[...]

        def part(rs):
            dm_v = dm_ref[rs, :].astype(F32)
            sgc = _sig(gc_ref[rs, :].astype(F32))
            sga = _sig(ga_ref[rs, :].astype(F32))
            dya = (dm_v * sgc).astype(BF16)
            dyb = (dm_v * sga).astype(BF16)
            dg_ref[rs, D:2 * D] = (dm_v * ya_ref[rs, :].astype(F32) * sgc * (1.0 - sgc)).astype(BF16)
            dg_ref[rs, 2 * D:3 * D] = (dm_v * yb_ref[rs, :].astype(F32) * sga * (1.0 - sga)).astype(BF16)
            ag = ag_ref[rs, :].astype(F32)
            sag = _sig(ag)
            sa = ag * sag
            ov = o_ref[rs, :].astype(F32)
            dyb_in = _mm_nt(dyb, wbra_ref[...])
            do_ref[rs, :] = (dyb_in * sa).astype(BF16)
            dg_ref[rs, 0:D] = (dyb_in * ov * _dsilu(ag, sag)).astype(BF16)
            gt = cgate_ref[rs, :].astype(F32)
            sgt = _sig(gt)
            sgate = gt * sgt
            pw = pw_ref[rs, :].astype(F32)
            dya_in = _mm_nt(dya, wbrc_ref[...])
            dpw = (dya_in * sgate).astype(BF16)
            dg_ref[rs, 3 * D:4 * D] = (dya_in * pw * _dsilu(gt, sgt)).astype(BF16)
            yn = y_ref[rs, :].astype(F32)
            n = yn * g + lnb_ref[...]
            sn = _sig(n)
            dn = _mm_nt(dpw, wpw_ref[...]) * _dsilu(n, sn)
            dy = dn * g
            dc = rstd_ref[rs, :] * (dy - jnp.mean(dy, axis=-1, keepdims=True)
                                    - yn * jnp.mean(dy * yn, axis=-1, keepdims=True))
            dc_ref[rs, :] = dc.astype(BF16)
            sums = (jnp.sum(dn * yn, axis=0, keepdims=True), jnp.sum(dn, axis=0, keepdims=True),
                    jnp.sum(dc, axis=0, keepdims=True))
            return ((ov * sa).astype(BF16), dyb, (pw * sgate).astype(BF16), dya, (n * sn).astype(BF16), dpw,
                    sums)

        parts = [part(pl.ds(r * (tm // TAIL_PARTS), tm // TAIL_PARTS)) for r in range(TAIL_PARTS)]
        cat = lambda j: jnp.concatenate([pt[j] for pt in parts], axis=0)
        acc_bra[...] += _mm_tn(cat(0), cat(1))
        acc_brc[...] += _mm_tn(cat(2), cat(3))
        acc_pw[...] += _mm_tn(cat(4), cat(5))
        for j in range(3):
            gvec_ref[j:j + 1, :] += sum(pt[6][j] for pt in parts)

        @pl.when(i == last)
        def _():
            _flush_to_pack(acc_pw, gpack_ref, 0, sem.at[0])
            _flush_to_pack(acc_brc, gpack_ref, D, sem.at[1])
            _flush_to_pack(acc_bra, gpack_ref, 2 * D, sem.at[2])

    tile = pl.BlockSpec((tm, D), lambda i: (i, 0))
    ztile = lambda j: pl.BlockSpec((tm, D), lambda i: (i, j))
    wsq = lambda k: pl.BlockSpec((D, D), lambda i: (k, 0))
    const = lambda shp: pl.BlockSpec(shp, lambda i: (0, 0))
    any_spec = pl.BlockSpec(memory_space=pl.ANY)
    return pl.pallas_call(
        body, name="tail_b", grid=(T // tm,),
        in_specs=[tile, tile, tile, tile, ztile(ZB_AGATE), ztile(ZB_GCONV), ztile(ZB_GATTN), ztile(ZB_CGATE),
                  tile, tile, pl.BlockSpec((tm, 1), lambda i: (i, 0)), const((1, D)), const((1, D)), wsq(0),
                  wsq(1), wsq(2), any_spec],
        out_specs=(pl.BlockSpec((tm, 4 * D), lambda i: (i, 0)), tile, tile, const((8, D)), any_spec),
        out_shape=(jax.ShapeDtypeStruct((T, 7 * D), BF16), jax.ShapeDtypeStruct((T, D), BF16),
                   jax.ShapeDtypeStruct((T, D), BF16), jax.ShapeDtypeStruct((8, D), F32),
                   jax.ShapeDtypeStruct(gpack.shape, F32)),
        input_output_aliases={16: 4},
        scratch_shapes=[pltpu.VMEM((D, D), F32), pltpu.VMEM((D, D), F32), pltpu.VMEM((D, D), F32),
                        pltpu.SemaphoreType.DMA((3,))],
        compiler_params=_params(("arbitrary",)),
    )(dm, ya, yb, o, z, z, z, z, pw, y, rstd, ln_g, ln_b, wall, wall, wall, gpack)


def _conv_bwd(dc, z, wdw, dz, S, tm):
    T = dc.shape[0]
    nt = S // tm
    hb = tm // HALO
    nrows = T // HALO

    def body(dc_ref, hdc_ref, cv_ref, cg_ref, hcv_ref, hcg_ref, wdw_ref, dz_in, dz_ref, gw_ref, ubuf, dcbuf,
             dubuf, dwacc, shbuf):
        b = pl.program_id(0)
        t = pl.program_id(1)

        @pl.when((b == 0) & (t == 0))
        def _():
            dwacc[...] = jnp.zeros_like(dwacc)

        cv = cv_ref[...].astype(F32)
        sg = _sig(cg_ref[...].astype(F32))
        ubuf[HALO:HALO + tm, :] = cv * sg
        hu = hcv_ref[...].astype(F32) * _sig(hcg_ref[...].astype(F32))
        ubuf[0:HALO, :] = jnp.where(t > 0, hu, 0.0)
        ubuf[HALO + tm:HALO + tm + 8, :] = jnp.zeros((8, D), F32)
        dcbuf[0:tm, :] = dc_ref[...].astype(F32)
        dcbuf[tm:tm + HALO, :] = jnp.where(t < nt - 1, hdc_ref[...].astype(F32), 0.0)
        dcbuf[tm + HALO:tm + HALO + 8, :] = jnp.zeros((8, D), F32)

        def chunk(ci, carry):
            r0 = pl.multiple_of(ci * CONV_RC, CONV_RC)
            for lg in range(D // CONV_LC):
                l0 = lg * CONV_LC
                dubuf[pl.ds(r0, CONV_RC), pl.ds(l0, CONV_LC)] = _conv_taps(
                    wdw_ref, dcbuf, r0, l0, lambda k: CONV_K - 1 - k)
                dcc = dcbuf[pl.ds(r0, CONV_RC), pl.ds(l0, CONV_LC)]
                zero8 = jnp.zeros((8, CONV_LC), F32)
                dcz = jnp.concatenate([zero8, dcc, zero8], axis=0)
                for bb in range(8):
                    taps = [k for k in range(CONV_K) if (HALO - (CONV_K - 1) + k) % 8 == bb]
                    if not taps:
                        continue
                    rows = CONV_RC + (8 if bb else 0)
                    if bb:
                        shbuf[bb] = dcz[8 - bb:8 - bb + rows]
                    for k in taps:
                        a8 = HALO - (CONV_K - 1) + k - bb
                        dcs = shbuf[bb] if bb else dcc
                        prod = dcs * ubuf[pl.ds(r0 + a8, rows), pl.ds(l0, CONV_LC)]
                        part = prod[0:8]
                        for q in range(1, rows // 8):
                            part = part + prod[8 * q:8 * q + 8]
                        dwacc[8 * k:8 * k + 8, pl.ds(l0, CONV_LC)] += part
            return carry

        lax.fori_loop(0, tm // CONV_RC, chunk, 0)
        du = dubuf[...]
        dz_ref[:, 0:D] = (du * sg).astype(BF16)
        dz_ref[:, D:2 * D] = (du * cv * sg * (1.0 - sg)).astype(BF16)

        @pl.when((b == pl.num_programs(0) - 1) & (t == nt - 1))
        def _():
            for k in range(32):
                gw_ref[k:k + 1, :] = jnp.sum(dwacc[8 * k:8 * k + 8, :], axis=0, keepdims=True)

    def row(b, t):
        return b * nt + t

    def prev_halo(b, t):
        return jnp.maximum(row(b, t) * hb - 1, 0)

    def next_halo(b, t):
        return jnp.minimum((row(b, t) + 1) * hb, nrows - 1)

    return pl.pallas_call(
        body, name="conv_bwd", grid=(T // S, nt),
        in_specs=[pl.BlockSpec((tm, D), lambda b, t: (row(b, t), 0)),
                  pl.BlockSpec((HALO, D), lambda b, t: (next_halo(b, t), 0)),
                  pl.BlockSpec((tm, D), lambda b, t: (row(b, t), ZB_CVAL)),
                  pl.BlockSpec((tm, D), lambda b, t: (row(b, t), ZB_CGLU)),
                  pl.BlockSpec((HALO, D), lambda b, t: (prev_halo(b, t), ZB_CVAL)),
                  pl.BlockSpec((HALO, D), lambda b, t: (prev_halo(b, t), ZB_CGLU)),
                  pl.BlockSpec((32, D), lambda b, t: (0, 0)),
                  pl.BlockSpec(memory_space=pl.ANY)],
        out_specs=(pl.BlockSpec((tm, 2 * D), lambda b, t: (row(b, t), ZB_CVAL // 2)),
                   pl.BlockSpec((32, D), lambda b, t: (0, 0))),
        out_shape=(jax.ShapeDtypeStruct(dz.shape, BF16), jax.ShapeDtypeStruct((32, D), F32)),
        input_output_aliases={7: 0},
        scratch_shapes=[pltpu.VMEM((tm + HALO + 8, D), F32), pltpu.VMEM((tm + HALO + 8, D), F32),
                        pltpu.VMEM((tm, D), F32), pltpu.VMEM((8 * 32, D), F32),
                        pltpu.VMEM((8, CONV_RC + 8, CONV_LC), F32)],
        compiler_params=_params(("arbitrary", "arbitrary")),
    )(dc, dc, z, z, z, z, wdw, dz)


def _attn_bwd(z, zkv, o, do, cos_t, sin_t, sinks, dz, S, tq):
    T = z.shape[0]
    nt = S // tq
    nq = tq // BLOCK

    def body(sink_ref, q_ref, kv_ref, hkv_ref, o_ref, do_ref, cos_ref, sin_ref, hcos_ref, hsin_ref, dz_in,
             dq_ref, dkv_ref, gs_ref, carry, dkacc, dvacc):
        b = pl.program_id(0)
        tt = pl.program_id(1)
        t = nt - 1 - tt

        @pl.when((b == 0) & (tt == 0))
        def _():
            gs_ref[...] = jnp.zeros_like(gs_ref)

        @pl.when(tt == 0)
        def _():
            carry[...] = jnp.zeros_like(carry)

        cos = cos_ref[...]
        sin = sin_ref[...]
        pswap = _swap_matrix()
        kv = jnp.concatenate([hkv_ref[...], kv_ref[...]], axis=0)
        cos_k = jnp.concatenate([hcos_ref[...], cos], axis=0)
        sin_k = jnp.concatenate([hsin_ref[...], sin], axis=0)
        kx = _kv_variants(_rope(kv[:, :BLOCK], cos_k, sin_k, pswap))
        vx = _kv_variants(kv[:, BLOCK:].astype(F32))
        band, sj = _band_mask(4)
        lo = lax.broadcasted_iota(jnp.int32, (4 * BLOCK, BLOCK), 1) < HEAD_DIM
        ones = jnp.ones((2 * BLOCK, 2 * BLOCK), BF16)
        qs = [(_rope(q_ref[:, 128 * hp:128 * hp + 128], cos, sin, pswap) * 0.125).astype(BF16)
              for hp in range(8)]
        dkacc[...] = jnp.zeros_like(dkacc)
        dvacc[...] = jnp.zeros_like(dvacc)
        gsum = jnp.zeros((1, BLOCK), F32)
        hlane = lax.broadcasted_iota(jnp.int32, (1, BLOCK), 1)
        for n in range(nq):
            first = (t == 0) & (n == 0)
            valid = band & (jnp.logical_not(first) | (sj >= BLOCK))
            r0 = n * BLOCK
            for g in range(2):
                cols = [slice(128 * (4 * g + j), 128 * (4 * g + j) + 128) for j in range(4)]
                lhs = jnp.concatenate([qs[4 * g + j][r0:r0 + BLOCK] for j in range(4)], axis=0)
                dov = jnp.concatenate([do_ref[r0:r0 + BLOCK, cs] for cs in cols], axis=0)
                prod = dov.astype(F32) * jnp.concatenate(
                    [o_ref[r0:r0 + BLOCK, cs] for cs in cols], axis=0).astype(F32)
                lhs_t = lhs.T
                dov_t = dov.T
                dq = jnp.zeros((4 * BLOCK, BLOCK), F32)
                dk_t = jnp.zeros((HEAD_DIM, 2 * BLOCK), F32)
                dv_t = jnp.zeros((HEAD_DIM, 2 * BLOCK), F32)
                for e in range(2):
                    kw = kx[g][e][r0:r0 + 2 * BLOCK]
                    vw = vx[g][e][r0:r0 + 2 * BLOCK]
                    s = _mm_nt(lhs, kw)
                    p, psink = _softmax_sink(s, valid, _sink_rep(sink_ref, g, e))
                    pe = jnp.where(lo if e == 0 else jnp.logical_not(lo), prod, 0.0)
                    pe_hi = pe.astype(BF16)
                    pe_lo = (pe - pe_hi.astype(F32)).astype(BF16)
                    delta = _mm(jnp.concatenate([pe_hi, pe_lo], axis=1), ones)
                    ds = (p * (_mm_nt(dov, vw) - delta)).astype(BF16)
                    dq = dq + _mm(ds, kw)
                    dims = slice(HEAD_DIM * e, HEAD_DIM * (e + 1))
                    dk_t = dk_t + _mm(lhs_t[dims], ds)
                    dv_t = dv_t + _mm(dov_t[dims], p.astype(BF16))
                    gs = -psink * delta[:, 0:BLOCK]
                    for j in range(4):
                        tot = jnp.sum(gs[j * BLOCK:(j + 1) * BLOCK], axis=0, keepdims=True)
                        gsum = gsum + jnp.where(hlane == 8 * g + 2 * j + e, tot, 0.0)
                dkacc[HEAD_DIM * g:HEAD_DIM * (g + 1), r0:r0 + 2 * BLOCK] += dk_t
                dvacc[HEAD_DIM * g:HEAD_DIM * (g + 1), r0:r0 + 2 * BLOCK] += dv_t
                for j in range(4):
                    dqj = _rope_f32(dq[j * BLOCK:(j + 1) * BLOCK] * 0.125, cos[r0:r0 + BLOCK],
                                    -sin[r0:r0 + BLOCK], pswap)
                    dq_ref[r0:r0 + BLOCK, cols[j]] = dqj.astype(BF16)
        gs_ref[0:1, :] += gsum
        dk_all = dkacc[...]
        dv_all = dvacc[...]
        dk_last = dk_all[:, tq:tq + BLOCK] + carry[0:BLOCK, :]
        dv_last = dv_all[:, tq:tq + BLOCK] + carry[BLOCK:2 * BLOCK, :]
        carry[0:BLOCK, :] = dk_all[:, 0:BLOCK]
        carry[BLOCK:2 * BLOCK, :] = dv_all[:, 0:BLOCK]
        if nq > 1:
            dk_tile = jnp.concatenate([dk_all[:, BLOCK:tq], dk_last], axis=1)
            dv_tile = jnp.concatenate([dv_all[:, BLOCK:tq], dv_last], axis=1)
        else:
            dk_tile, dv_tile = dk_last, dv_last
        dkv_ref[:, 0:BLOCK] = _rope_f32(dk_tile.T, cos, -sin, pswap).astype(BF16)
        dkv_ref[:, BLOCK:2 * BLOCK] = dv_tile.T.astype(BF16)

    def row(b, tt):
        return b * nt + (nt - 1 - tt)

    def halo(b, tt):
        return jnp.maximum(row(b, tt) * nq - 1, 0)

    tile = pl.BlockSpec((tq, D), lambda b, tt: (row(b, tt), 0))
    return pl.pallas_call(
        body, name="attn_bwd", grid=(T // S, nt),
        in_specs=[pl.BlockSpec(memory_space=pltpu.SMEM),
                  pl.BlockSpec((tq, D), lambda b, tt: (row(b, tt), ZB_Q)),
                  pl.BlockSpec((tq, 2 * BLOCK), lambda b, tt: (row(b, tt), 0)),
                  pl.BlockSpec((BLOCK, 2 * BLOCK), lambda b, tt: (halo(b, tt), 0)),
                  tile, tile,
                  pl.BlockSpec((tq, BLOCK), lambda b, tt: (row(b, tt), 0)),
                  pl.BlockSpec((tq, BLOCK), lambda b, tt: (row(b, tt), 0)),
                  pl.BlockSpec((BLOCK, BLOCK), lambda b, tt: (halo(b, tt), 0)),
                  pl.BlockSpec((BLOCK, BLOCK), lambda b, tt: (halo(b, tt), 0)),
                  pl.BlockSpec(memory_space=pl.ANY)],
        out_specs=(pl.BlockSpec((tq, D), lambda b, tt: (row(b, tt), ZB_Q)),
                   pl.BlockSpec((tq, 2 * BLOCK), lambda b, tt: (row(b, tt), 0)),
                   pl.BlockSpec((8, BLOCK), lambda b, tt: (0, 0))),
        out_shape=(jax.ShapeDtypeStruct(dz.shape, BF16), jax.ShapeDtypeStruct((T, 2 * BLOCK), BF16),
                   jax.ShapeDtypeStruct((8, BLOCK), F32)),
        input_output_aliases={10: 0},
        scratch_shapes=[pltpu.VMEM((2 * BLOCK, BLOCK), F32), pltpu.VMEM((BLOCK, tq + BLOCK), F32),
                        pltpu.VMEM((BLOCK, tq + BLOCK), F32)],
        compiler_params=_params(("arbitrary", "arbitrary")),
    )(sinks, z, zkv, zkv, o, do, cos_t, sin_t, cos_t, sin_t, dz)


def _exchange_copies(g_ref, r1_ref, send_sems, recv_sems):
    x, y, c = _coords()
    return [pltpu.make_async_remote_copy(
        src_ref=g_ref.at[:, pl.ds(pl.multiple_of((1 - c) * HALF_ROWS, 32), HALF_ROWS), :], dst_ref=r1_ref,
        send_sem=send_sems.at[0], recv_sem=recv_sems.at[0], device_id=(x, y, 1 - c), device_id_type=MESH)]


def _chip_sum_copies(cs_ref, r2_ref, send_sems, recv_sems):
    x, y, c = _coords()
    return [pltpu.make_async_remote_copy(
        src_ref=cs_ref.at[2 * px + py], dst_ref=r2_ref.at[k], send_sem=send_sems.at[k],
        recv_sem=recv_sems.at[k], device_id=(px, py, c), device_id_type=MESH)
        for k, (px, py) in enumerate(_chip_peers(x, y))]


def _dh(dz, dz_kv, wall, x, dx1, ln_pre, tm, tile0, ntiles, gx_prev, name, copies, src, landing):
    T = x.shape[0]
    nsem = 3

    def body(*refs):
        dz_ref, kv_ref, wt_ref, x_ref, dx1_ref, g_ref, src_ref = refs[:7]
        gx_ref, glp_ref, land_ref, wbuf, send_sems, recv_sems, wsem = refs[-7:]
        i = pl.program_id(0)

        @pl.when(i == 0)
        def _():
            glp_ref[...] = jnp.zeros_like(glp_ref)
            for cp in copies(src_ref, land_ref, send_sems, recv_sems):
                cp.start()
            load = pltpu.make_async_copy(wt_ref, wbuf, wsem)
            load.start()
            load.wait()

        dh = _mm(dz_ref[...], wbuf[0:ZKV, :]) + _mm(kv_ref[...], wbuf[ZKV:IN_WIDTH, :])
        xv = x_ref[...]
        r = lax.rsqrt(jnp.mean(xv * xv, axis=-1, keepdims=True) + EPS)
        xr = xv * r
        glp_ref[...] += jnp.sum(dh * xr, axis=0, keepdims=True)
        a = dh * g_ref[...]
        gx_ref[...] = dx1_ref[...] + r * (a - xr * jnp.mean(a * xr, axis=-1, keepdims=True))

        @pl.when(i == ntiles - 1)
        def _():
            cps = copies(src_ref, land_ref, send_sems, recv_sems)
            for cp in cps:
                cp.wait_recv()
            for cp in cps:
                cp.wait_send()

    tile = pl.BlockSpec((tm, D), lambda i: (tile0 + i, 0))
    any_spec = pl.BlockSpec(memory_space=pl.ANY)
    operands = [dz, dz_kv, wall, x, dx1, ln_pre, src] + ([] if gx_prev is None else [gx_prev])
    return pl.pallas_call(
        body, name=name, grid=(ntiles,),
        in_specs=[pl.BlockSpec((tm, ZKV), lambda i: (tile0 + i, 0)),
                  pl.BlockSpec((tm, 2 * BLOCK), lambda i: (tile0 + i, 0)),
                  any_spec, tile, tile, pl.BlockSpec((1, D), lambda i: (0, 0)), any_spec]
        + ([] if gx_prev is None else [any_spec]),
        out_specs=(tile, pl.BlockSpec((1, D), lambda i: (0, 0)), any_spec),
        out_shape=(jax.ShapeDtypeStruct((T, D), F32), jax.ShapeDtypeStruct((1, D), F32), landing),
        input_output_aliases={} if gx_prev is None else {7: 0},
        scratch_shapes=[pltpu.VMEM((IN_WIDTH, D), BF16), pltpu.SemaphoreType.DMA((nsem,)),
                        pltpu.SemaphoreType.DMA((nsem,)), pltpu.SemaphoreType.DMA],
        compiler_params=_params(("arbitrary",)),
    )(*operands)


def _gwt(dz, h, gpack, tt):
    T = dz.shape[0]
    last = T // tt - 1

    def body(dz_ref, h_ref, gpack_in, gpack_ref, acc, sem):
        j = pl.program_id(0)
        t = pl.program_id(1)

        @pl.when(t == 0)
        def _():
            acc[...] = _mm_tn(dz_ref[...], h_ref[...])

        @pl.when(t > 0)
        def _():
            acc[...] += _mm_tn(dz_ref[...], h_ref[...])

        for jj in range(7):
            @pl.when((t == last) & (j == jj))
            def _(jj=jj):
                _flush_to_pack(acc, gpack_ref, WT0 + jj * D, sem)

    any_spec = pl.BlockSpec(memory_space=pl.ANY)
    return pl.pallas_call(
        body, name="gwt", grid=(7, T // tt),
        in_specs=[pl.BlockSpec((tt, D), lambda j, t: (t, j)), pl.BlockSpec((tt, D), lambda j, t: (t, 0)),
                  any_spec],
        out_specs=any_spec, out_shape=jax.ShapeDtypeStruct(gpack.shape, F32), input_output_aliases={2: 0},
        scratch_shapes=[pltpu.VMEM((D, D), F32), pltpu.SemaphoreType.DMA],
        compiler_params=_params(("arbitrary", "arbitrary")),
    )(dz, h, gpack)


def _gwt_kv(dz_kv, h, gppt, gpack, tt):
    T = dz_kv.shape[0]
    last = T // tt - 1

    def body(dz_ref, h_ref, gppt_ref, gpack_in, gpack_ref, acc, sem):
        t = pl.program_id(0)

        @pl.when(t == 0)
        def _():
            acc[...] = _mm_tn(dz_ref[...], h_ref[...])

        @pl.when(t > 0)
        def _():
            acc[...] += _mm_tn(dz_ref[...], h_ref[...])

        @pl.when(t == last)
        def _():
            _flush_to_pack(acc, gpack_ref, WT0 + ZKV, sem)
            _flush_to_pack(gppt_ref, gpack_ref, WPP0, sem)

    any_spec = pl.BlockSpec(memory_space=pl.ANY)
    return pl.pallas_call(
        body, name="gwt_kv", grid=(T // tt,),
        in_specs=[pl.BlockSpec((tt, 2 * BLOCK), lambda t: (t, 0)), pl.BlockSpec((tt, D), lambda t: (t, 0)),
                  pl.BlockSpec((PLE, D), lambda t: (0, 0)), any_spec],
        out_specs=any_spec, out_shape=jax.ShapeDtypeStruct(gpack.shape, F32), input_output_aliases={3: 0},
        scratch_shapes=[pltpu.VMEM((2 * BLOCK, D), F32), pltpu.SemaphoreType.DMA],
        compiler_params=_params(("arbitrary",)),
    )(dz_kv, h, gppt, gpack)


_BC1 = 1.0 - ADAM_B1 ** ADAM_STEP
_BC2 = 1.0 - ADAM_B2 ** ADAM_STEP


def _adamw_math(w, g, m, v):
    m = ADAM_B1 * m + (1.0 - ADAM_B1) * g
    v = ADAM_B2 * v + (1.0 - ADAM_B2) * (g * g)
    delta = -ADAM_LR * ((m / _BC1) / (jnp.sqrt(v / _BC2) + ADAM_EPS) + ADAM_WD * w)
    return delta, m, v


def _adamw_rows(g, w, m, v, rows, name):
    R, C = w.shape

    def body(g_ref, w_ref, m_ref, v_ref, go_ref, d_ref, nm_ref, nv_ref):
        gv = g_ref[...]
        d, nm, nv = _adamw_math(w_ref[...], gv, m_ref[...], v_ref[...])
        go_ref[...] = gv
        d_ref[...] = d
        nm_ref[...] = nm
        nv_ref[...] = nv

    spec = pl.BlockSpec((rows, C), lambda i: (i, 0))
    shp = jax.ShapeDtypeStruct((R, C), F32)
    return pl.pallas_call(
        body, name=name, grid=(R // rows,), in_specs=[spec] * 4, out_specs=(spec,) * 4,
        out_shape=(shp,) * 4, compiler_params=_params(("arbitrary",)),
    )(g, w, m, v)


def _adamw_square(gfin, ws, ms, vs):
    rb = 64
    nb = SQ_SHARD // rb

    def body(*refs):
        g_refs = refs[0:5]
        w_refs, m_refs, v_refs = refs[5:10], refs[10:15], refs[15:20]
        outs = refs[20:]
        for k in range(5):
            gk = g_refs[k][...]
            d, nm, nv = _adamw_math(w_refs[k][...], gk, m_refs[k][...], v_refs[k][...])
            outs[4 * k][...] = gk
            outs[4 * k + 1][...] = d
            outs[4 * k + 2][...] = nm
            outs[4 * k + 3][...] = nv

    spec = pl.BlockSpec((rb, D), lambda i: (i, 0))
    gspecs = [pl.BlockSpec((rb, D), lambda i, k=k: ((WIN_SHARD + SQ_SHARD * k) // rb + i, 0))
              for k in range(5)]
    shp = jax.ShapeDtypeStruct((SQ_SHARD, D), F32)
    res = pl.pallas_call(
        body, name="adamw_square", grid=(nb,), in_specs=gspecs + [spec] * 15, out_specs=(spec,) * 20,
        out_shape=(shp,) * 20, compiler_params=_params(("arbitrary",)),
    )(*([gfin] * 5), *ws, *ms, *vs)
    return [tuple(res[4 * k:4 * k + 4]) for k in range(5)]


def _adamw_small(gs, ws, ms, vs):
    n = len(gs)

    def body(*refs):
        outs = refs[4 * n:]
        for k in range(n):
            d, nm, nv = _adamw_math(refs[n + k][...], refs[k][...], refs[2 * n + k][...],
                                    refs[3 * n + k][...])
            outs[3 * k][...] = d
            outs[3 * k + 1][...] = nm
            outs[3 * k + 2][...] = nv

    vm = pl.BlockSpec(memory_space=pltpu.VMEM)
    shapes = []
    for w in ws:
        shapes += [jax.ShapeDtypeStruct(w.shape, F32)] * 3
    res = pl.pallas_call(
        body, name="adamw_small", in_specs=[vm] * (4 * n), out_specs=(vm,) * (3 * n),
        out_shape=tuple(shapes),
    )(*gs, *ws, *ms, *vs)
    return [tuple(res[3 * k:3 * k + 3]) for k in range(n)]


def _rope_tables(positions):
    inv = jnp.power(ROPE_THETA, -jnp.arange(0, ROPE_DIM, 2, dtype=F32) / ROPE_DIM)
    inv_h = jnp.concatenate([inv, inv, jnp.zeros((HEAD_DIM - ROPE_DIM,), F32)])
    sign_h = np.array([-1.0] * (ROPE_DIM // 2) + [1.0] * (ROPE_DIM // 2) + [0.0] * (HEAD_DIM - ROPE_DIM),
                      np.float32)
    ang = positions.astype(F32).reshape(-1, 1) * jnp.concatenate([inv_h, inv_h])[None, :]
    return jnp.cos(ang), jnp.sin(ang) * np.concatenate([sign_h, sign_h])[None, :]


def kernel(x, p, positions, w_in, ln_pre, ln_post, w_dw, b_dw, conv_ln_g, conv_ln_b, w_pw, sinks, w_br_conv, w_br_attn, w_out, w_ple_gate, w_ple_proj, loss_target, m_w_in, m_ln_pre, m_ln_post, m_w_dw, m_b_dw, m_conv_ln_g, m_conv_ln_b, m_w_pw, m_sinks, m_w_br_conv, m_w_br_attn, m_w_out, m_w_ple_gate, m_w_ple_proj, v_w_in, v_ln_pre, v_ln_post, v_w_dw, v_b_dw, v_conv_ln_g, v_conv_ln_b, v_w_pw, v_sinks, v_w_br_conv, v_w_br_attn, v_w_out, v_w_ple_gate, v_w_ple_proj):
    nb, S, _ = x.shape
    T = nb * S
    xc = lax.axis_index("x")
    yc = lax.axis_index("y")
    cc = lax.axis_index("c")
    shard = 2 * xc + yc

    sq_w = (w_pw, w_br_conv, w_br_attn, w_out, w_ple_gate)
    wdw_shard = jnp.pad(w_dw[0], ((0, 1), (0, 0)))
    x2 = x.reshape(T, D)
    tm_res = min(TILE_RESIDENT, T // 2)
    wt, wdw_all, h = _gather_win(w_in[0].T.astype(BF16), wdw_shard, x2, ln_pre, tm_res)
    wdw = jnp.concatenate([wdw_all[s] for s in range(N_SHARDS)], axis=1)

    tgt = loss_target.reshape(T, D)
    p2 = p.reshape(T, PLE)
    cos_t, sin_t = _rope_tables(positions)
    sinks1 = sinks.reshape(N_HEADS)

    tm_big = min(TILE_PROJ, T)
    tm = min(TILE_TOKEN, S)
    tq = min(TILE_ATTN, S)

    z, zkv, wall, wppf = _inproj(h, wt, [w[0].astype(BF16) for w in sq_w],
                                 w_ple_proj[0].T.reshape(WPP_SHARD, D).astype(BF16), tm_res)
    wppt = wppf.reshape(D, PLE)
    ya, y, rstd, pw = _conv_fwd(z, wdw, b_dw, conv_ln_g, conv_ln_b, wall, S, tm)
    o = _attn_fwd(z, zkv, cos_t, sin_t, sinks1, S, tq)
    loss_p, dx1, dm, yb, g_ln_post, gpack, gw_ppt = _tail_a(x2, tgt, p2, o, ya, z, ln_post, wall, wppt, tm)

    dz, do, dc, gvec, gpack = _tail_b(dm, ya, yb, o, z, pw, y, rstd, conv_ln_g, conv_ln_b, wall, gpack, tm)
    dz, g_wdw = _conv_bwd(dc, z, wdw, dz, S, tm)
    dz, dkv, g_sinks = _attn_bwd(z, zkv, o, do, cos_t, sin_t, sinks1, dz, S, tq)
    gpack = _gwt(dz, h, gpack, min(2 * TILE_PROJ, T))
    gpack = _gwt_kv(dkv, h, gw_ppt.reshape(PLE, D), gpack, tm_big)

    cidx = jnp.reshape(cc, (1,)).astype(jnp.int32)
    scidx = jnp.stack([shard, cc]).astype(jnp.int32)
    tm_dh = tm_res
    n_dh = T // tm_dh
    n_a = max(1, n_dh // 4)
    gx, g_ln_pre_a, r1 = _dh(
        dz, dkv, wt, x2, dx1, ln_pre, tm_dh, 0, n_a, None, "dh_exchange", _exchange_copies, gpack,
        jax.ShapeDtypeStruct((N_SHARDS, HALF_ROWS, D), F32))
    cs = _chip_sum(cidx, gpack, r1)
    gx, g_ln_pre_b, r2 = _dh(
        dz, dkv, wt, x2, dx1, ln_pre, tm_dh, n_a, n_dh - n_a, gx, "dh_send", _chip_sum_copies, cs,
        jax.ShapeDtypeStruct((3, HALF_ROWS, D), BF16))
    g_ln_pre = g_ln_pre_a + g_ln_pre_b
    row37 = jnp.concatenate([g_sinks[0:1, 0:N_HEADS], loss_p, jnp.zeros((1, D - N_HEADS - 1), F32)], axis=1)
    vec = jnp.concatenate([g_wdw, g_ln_pre, g_ln_post, gvec[2:3], gvec[0:1], gvec[1:2], row37,
                           jnp.zeros((VEC_ROWS - 38, D), F32)], axis=0)
    gfin, tot = _finish_reduce(_final_half(scidx, gpack, r1, r2), vec)

    g_w_in, d_w_in, nm_w_in, nv_w_in = [a.T for a in _adamw_rows(
        gfin, w_in[0].T, m_w_in[0].T, v_w_in[0].T, WIN_SHARD // 8, "adamw_w_in")]
    g_w_in = g_w_in[None]
    sq_m = (m_w_pw, m_w_br_conv, m_w_br_attn, m_w_out, m_w_ple_gate)
    sq_v = (v_w_pw, v_w_br_conv, v_w_br_attn, v_w_out, v_w_ple_gate)
    sq_res = _adamw_square(gfin, [w[0] for w in sq_w], [m[0] for m in sq_m], [v[0] for v in sq_v])
    g_wpp = gfin[WIN_SHARD + 5 * SQ_SHARD:PACK_ROWS].reshape(PLE, PLE).T
    g_dw_all = tot[0:CONV_K]
    g_dw = lax.dynamic_slice_in_dim(g_dw_all, shard * PLE, PLE, axis=1)
    small_g = [g_wpp, g_dw, tot[32:33], tot[33:34], tot[34:35], tot[35:36], tot[36:37],
               tot[37:38, 0:N_HEADS]]
    small_w = [w_ple_proj[0], w_dw[0], ln_pre, ln_post, b_dw, conv_ln_g, conv_ln_b, sinks]
    small_m = [m_w_ple_proj[0], m_w_dw[0], m_ln_pre, m_ln_post, m_b_dw, m_conv_ln_g, m_conv_ln_b, m_sinks]
    small_v = [v_w_ple_proj[0], v_w_dw[0], v_ln_pre, v_ln_post, v_b_dw, v_conv_ln_g, v_conv_ln_b, v_sinks]
    small = _adamw_small(small_g, small_w, small_m, small_v)

    loss = tot[37, N_HEADS]
    grads = [g_w_in, small_g[2], small_g[3], g_dw[None], small_g[4], small_g[5], small_g[6],
             sq_res[0][0][None], small_g[7], sq_res[1][0][None], sq_res[2][0][None], sq_res[3][0][None],
             sq_res[4][0][None], g_wpp[None]]

    def triple(i):
        w_in_t = (d_w_in[None], nm_w_in[None], nv_w_in[None])
        sq = lambda k: tuple(a[None] for a in sq_res[k][1:4])
        sm = lambda k, lead: tuple(a[None] if lead else a for a in small[k])
        return [w_in_t[i], sm(2, False)[i], sm(3, False)[i], sm(1, True)[i], sm(4, False)[i],
                sm(5, False)[i], sm(6, False)[i], sq(0)[i], sm(7, False)[i], sq(1)[i], sq(2)[i], sq(3)[i],
                sq(4)[i], sm(0, True)[i]]

    return (loss, gx.reshape(nb, S, D), *grads, *triple(0), *triple(1), *triple(2))
```

```python
import functools

import jax
import jax.numpy as jnp
import numpy as np
from jax import lax
from jax.experimental import pallas as pl
from jax.experimental.pallas import tpu as pltpu

F32 = jnp.float32
BF16 = jnp.bfloat16

D = 1024
PLE = 256
N_HEADS = 16
HEAD_DIM = 64
BLOCK = 128
CONV_K = 31
ROPE_DIM = 16
ROPE_THETA = 500000.0
EPS = 1e-6
IN_WIDTH = 7424
N_SHARDS = 4

ADAM_LR = 0.001
ADAM_B1 = 0.9
ADAM_B2 = 0.999
ADAM_EPS = 1e-08
ADAM_WD = 0.01
ADAM_STEP = 10

SQ_NAMES = ("w_pw", "w_br_conv", "w_br_attn", "w_out", "w_ple_gate")
WT0 = 5 * D
WPP0 = WT0 + IN_WIDTH
WALL_ROWS = WPP0 + PLE
WIN_SHARD = IN_WIDTH // N_SHARDS
SQ_SHARD = D // N_SHARDS
WPP_SHARD = PLE * PLE // D
PACK_ROWS = WIN_SHARD + 5 * SQ_SHARD + WPP_SHARD
HALF_ROWS = PACK_ROWS // 2
VMEM_LIMIT = 56 * 1024 * 1024
MESH = pl.DeviceIdType.MESH
TILE_RESIDENT = 512
TILE_PROJ = 1024
TILE_TOKEN = 256
TILE_ATTN = 512
TAIL_PARTS = 1


ZB_AGATE, ZB_GCONV, ZB_GATTN, ZB_CGATE, ZB_CVAL, ZB_CGLU, ZB_Q = range(7)
ZKV = 7 * D
_SEGMENTS = ((0, D, ZB_CVAL * D), (D, D, ZB_CGLU * D), (2 * D, D, ZB_CGATE * D), (3 * D, D, ZB_Q * D),
             (4 * D, 2 * BLOCK, ZKV), (4 * D + 2 * BLOCK, D, ZB_AGATE * D),
             (5 * D + 2 * BLOCK, D, ZB_GCONV * D), (6 * D + 2 * BLOCK, D, ZB_GATTN * D))
_WT_CUTS = (0, 192, 640, 1216, WIN_SHARD)


def _zp_row(o):
    for a, w, zp in _SEGMENTS:
        if a <= o < a + w:
            return zp + o - a
    raise ValueError(o)


def _pieces(s):
    out = []
    for a, b in zip(_WT_CUTS[:-1], _WT_CUTS[1:]):
        first = _zp_row(WIN_SHARD * s + a)
        assert _zp_row(WIN_SHARD * s + b - 1) == first + b - a - 1
        out.append((a, b - a, WT0 + first))
    for k in range(5):
        out.append((WIN_SHARD + SQ_SHARD * k, SQ_SHARD, D * k + SQ_SHARD * s))
    out.append((WIN_SHARD + 5 * SQ_SHARD, WPP_SHARD, WPP0 + WPP_SHARD * s))
    return out


N_PIECES = len(_pieces(0))


def _wall_segments(wall0, rows):
    out = []
    for s in range(N_SHARDS):
        for pr, n, wr in _pieces(s):
            lo, hi = max(wr, wall0), min(wr + n, wall0 + rows)
            if lo < hi:
                out.append((lo - wall0, hi - lo, s, pr + lo - wr))
    assert sum(n for _, n, _, _ in out) == rows
    return out


def _sel(s, vals):
    r = jnp.int32(vals[0])
    for i in range(1, len(vals)):
        r = jnp.where(s == i, jnp.int32(vals[i]), r)
    return r


def _sig(x):
    return 1.0 / (1.0 + jnp.exp(-x))


def _mm(a, b):
    return lax.dot_general(a, b, (((1,), (0,)), ((), ())), preferred_element_type=F32)


def _mm_nt(a, b):
    return lax.dot_general(a, b, (((1,), (1,)), ((), ())), preferred_element_type=F32)


def _mm_tn(a, b):
    return lax.dot_general(a, b, (((0,), (0,)), ((), ())), preferred_element_type=F32)


def _params(sem=None):
    return pltpu.CompilerParams(dimension_semantics=sem, vmem_limit_bytes=VMEM_LIMIT)


def _flush_to_pack(acc_ref, gpack_ref, wall0, sem):
    base = 0 if gpack_ref.shape[1] == WIN_SHARD else WIN_SHARD
    for r, n, s, pr in _wall_segments(wall0, acc_ref.shape[0]):
        assert 0 <= pr - base and pr - base + n <= gpack_ref.shape[1]
        cp = pltpu.make_async_copy(acc_ref.at[pl.ds(r, n)], gpack_ref.at[s, pl.ds(pr - base, n)], sem)
        cp.start()
        cp.wait()


def _coords():
    return lax.axis_index("x"), lax.axis_index("y"), lax.axis_index("c")


def _chip_peers(x, y):
    return [(1 - x, y), (x, 1 - y), (1 - x, 1 - y)]


WIN_PIECES = tuple(range(len(_WT_CUTS) - 1))
SQ_PIECES = tuple(range(len(WIN_PIECES), N_PIECES))


def _gather_ops(group, src, landing, bytes_ref, stage, send_sems, recv_sems, loc_sem):
    sizes = [_pieces(0)[p][1] for p in group]
    half_rows = sum(n // 2 for n in sizes)

    def rcopy(a, b, k, dev):
        return pltpu.make_async_remote_copy(src_ref=a, dst_ref=b, send_sem=send_sems.at[k],
                                            recv_sem=recv_sems.at[k], device_id=dev, device_id_type=MESH)

    def total(k):
        x, y, c = _coords()
        rows = bytes_ref.at[pl.ds(0, half_rows)]
        return rcopy(rows, rows, k, (x, y, c))

    def send():
        x, y, c = _coords()
        s_me = 2 * x + y
        for k, (px, py) in enumerate(_chip_peers(x, y)):
            for p, n in zip(group, sizes):
                h = n // 2
                rcopy(src(p, c * h, h), landing(p, s_me, c * h, h), k, (px, py, c)).start()
        for p, n in zip(group, sizes):
            for a, b in ((src(p, 0, n), stage.at[pl.ds(0, n)]), (stage.at[pl.ds(0, n)], landing(p, s_me, 0, n))):
                cp = pltpu.make_async_copy(a, b, loc_sem)
                cp.start()
                cp.wait()

    def forward():
        x, y, c = _coords()
        for k, (px, py) in enumerate(_chip_peers(x, y)):
            total(k).wait_recv()
            for p, n in zip(group, sizes):
                rows = landing(p, 2 * px + py, c * (n // 2), n // 2)
                rcopy(rows, rows, 3 + k, (x, y, 1 - c)).start()

    def finish():
        for k in range(3):
            total(3 + k).wait_recv()
        for k in range(6):
            total(k).wait_send()

    return send, forward, finish


def _piece_rows(ref, start, off, n):
    first = start + off
    return ref.at[pl.ds(first if isinstance(first, int) else pl.multiple_of(first, 32), n)]


def _gather_win(win_t, wdw_shard, x, ln_pre, tm):
    tables = [[_pieces(s)[p][2] - WT0 for s in range(N_SHARDS)] for p in WIN_PIECES]
    T = x.shape[0]
    n_i = T // tm

    def wdw_copies(wdw_ref, wdwall_ref, send_sems, recv_sems):
        x_, y_, c_ = _coords()
        return [pltpu.make_async_remote_copy(
            src_ref=wdw_ref, dst_ref=wdwall_ref.at[2 * x_ + y_], send_sem=send_sems.at[6 + k],
            recv_sem=recv_sems.at[6 + k], device_id=(px, py, c_), device_id_type=MESH)
            for k, (px, py) in enumerate(_chip_peers(x_, y_))]

    def body(win_ref, wdw_ref, x_ref, g_ref, wt_ref, wdwall_ref, h_ref, stage, send_sems, recv_sems, loc_sems):
        i = pl.program_id(0)
        send, forward, finish = _gather_ops(
            WIN_PIECES, lambda p, off, n: _piece_rows(win_ref, _WT_CUTS[p], off, n),
            lambda p, s, off, n: _piece_rows(wt_ref, _sel(s, tables[p]), off, n),
            wt_ref, stage, send_sems, recv_sems, loc_sems.at[0])

        def own_wdw():
            x_, y_, _ = _coords()
            return pltpu.make_async_copy(wdw_ref, wdwall_ref.at[2 * x_ + y_], loc_sems.at[1])

        @pl.when(i == 0)
        def _():
            own_wdw().start()
            for cp in wdw_copies(wdw_ref, wdwall_ref, send_sems, recv_sems):
                cp.start()
            send()

        xv = x_ref[...]
        r = lax.rsqrt(jnp.mean(xv * xv, axis=-1, keepdims=True) + EPS)
        h_ref[...] = (xv * r * g_ref[...]).astype(BF16)

        @pl.when(i == n_i - 1)
        def _():
            forward()
            finish()
            cps = wdw_copies(wdw_ref, wdwall_ref, send_sems, recv_sems)
            for cp in cps:
                cp.wait_recv()
            for cp in cps:
                cp.wait_send()
            own_wdw().wait()

    any_spec = pl.BlockSpec(memory_space=pl.ANY)
    return pl.pallas_call(
        body, name="gather_win", grid=(n_i,),
        out_shape=(jax.ShapeDtypeStruct((IN_WIDTH, D), BF16), jax.ShapeDtypeStruct((N_SHARDS, 32, PLE), F32),
                   jax.ShapeDtypeStruct((T, D), BF16)),
        in_specs=[any_spec, any_spec, pl.BlockSpec((tm, D), lambda i: (i, 0)),
                  pl.BlockSpec((1, D), lambda i: (0, 0))],
        out_specs=(any_spec, any_spec, pl.BlockSpec((tm, D), lambda i: (i, 0))),
        scratch_shapes=[pltpu.VMEM((max(_pieces(0)[p][1] for p in WIN_PIECES), D), BF16),
                        pltpu.SemaphoreType.DMA((9,)), pltpu.SemaphoreType.DMA((9,)),
                        pltpu.SemaphoreType.DMA((2,))],
        compiler_params=_params(("arbitrary",)),
    )(win_t, wdw_shard, x, ln_pre)


SQ_PACK = PACK_ROWS - WIN_SHARD


def _row_tile(half):
    return max(t for t in range(8, 321, 8) if half % t == 0)


def _exchange_copies(g_ref, r1_ref, send_sems, recv_sems):
    x, y, c = _coords()
    half = g_ref.shape[1] // 2
    return [pltpu.make_async_remote_copy(
        src_ref=g_ref.at[:, pl.ds(pl.multiple_of((1 - c) * half, 32), half), :], dst_ref=r1_ref,
        send_sem=send_sems.at[0], recv_sem=recv_sems.at[0], device_id=(x, y, 1 - c), device_id_type=MESH)]


def _chip_sum_copies(cs_ref, r2_ref, send_sems, recv_sems):
    x, y, c = _coords()
    return [pltpu.make_async_remote_copy(
        src_ref=cs_ref.at[2 * px + py], dst_ref=r2_ref.at[k], send_sem=send_sems.at[k],
        recv_sem=recv_sems.at[k], device_id=(px, py, c), device_id_type=MESH)
        for k, (px, py) in enumerate(_chip_peers(x, y))]


def _chip_sum(cidx, gpack, r1, name):
    half = gpack.shape[1] // 2
    rt = _row_tile(half)

    def body(c_ref, g_ref, r_ref, o_ref):
        o_ref[...] = (g_ref[...] + r_ref[...]).astype(BF16)

    nt = half // rt
    return pl.pallas_call(
        body, name=name,
        grid_spec=pltpu.PrefetchScalarGridSpec(
            num_scalar_prefetch=1, grid=(N_SHARDS, nt),
            in_specs=[pl.BlockSpec((1, rt, D), lambda s, t, c: (s, c[0] * nt + t, 0)),
                      pl.BlockSpec((1, rt, D), lambda s, t, c: (s, t, 0))],
            out_specs=pl.BlockSpec((1, rt, D), lambda s, t, c: (s, t, 0))),
        out_shape=jax.ShapeDtypeStruct((N_SHARDS, half, D), BF16),
        compiler_params=_params(("arbitrary", "arbitrary")),
    )(cidx, gpack, r1)


def _final_half(sc, gpack, r1, r2, name):
    rows = gpack.shape[1]
    half = rows // 2
    rt = _row_tile(half)

    def body(sc_ref, g_ref, r_ref, p_ref, o_ref):
        acc = g_ref[0] + r_ref[0]
        for k in range(3):
            acc = acc + p_ref[k].astype(F32)
        o_ref[...] = acc

    nt = half // rt
    return pl.pallas_call(
        body, name=name,
        grid_spec=pltpu.PrefetchScalarGridSpec(
            num_scalar_prefetch=1, grid=(nt,),
            in_specs=[pl.BlockSpec((1, rt, D), lambda t, sc: (sc[0], sc[1] * nt + t, 0)),
                      pl.BlockSpec((1, rt, D), lambda t, sc: (sc[0], t, 0)),
                      pl.BlockSpec((3, rt, D), lambda t, sc: (0, t, 0))],
            out_specs=pl.BlockSpec((rt, D), lambda t, sc: (sc[1] * nt + t, 0))),
        out_shape=jax.ShapeDtypeStruct((rows, D), F32),
        compiler_params=_params(("arbitrary",)),
    )(sc, gpack, r1, r2)


VEC_ROWS = 40


def _finish_reduce(fwt, fsq, vec):
    def body(fwt_ref, fsq_ref, v_ref, owt_ref, osq_ref, tot_ref, buf, send_sems, recv_sems):
        x, y, c = _coords()
        swaps = []
        for k, (f_ref, o_ref) in enumerate(((fwt_ref, owt_ref), (fsq_ref, osq_ref))):
            half = f_ref.shape[0] // 2
            rows = pl.ds(pl.multiple_of(c * half, 32), half)
            swaps.append(pltpu.make_async_remote_copy(
                src_ref=f_ref.at[rows], dst_ref=o_ref.at[rows], send_sem=send_sems.at[7 + k],
                recv_sem=recv_sems.at[7 + k], device_id=(x, y, 1 - c), device_id_type=MESH))
        for cp in swaps:
            cp.start()
        me = 4 * x + 2 * y + c
        buf[me] = v_ref[...]
        cps = []
        for r in range(1, 8):
            dx, dy, dc = (r >> 2) & 1, (r >> 1) & 1, r & 1
            peer = (1 - x if dx else x, 1 - y if dy else y, 1 - c if dc else c)
            cp = pltpu.make_async_remote_copy(
                src_ref=v_ref, dst_ref=buf.at[me], send_sem=send_sems.at[r - 1],
                recv_sem=recv_sems.at[r - 1], device_id=peer, device_id_type=MESH)
            cp.start()
            cps.append(cp)
        for cp in cps:
            cp.wait_recv()
        for cp in cps:
            cp.wait_send()
        acc = buf[0]
        for d in range(1, 8):
            acc = acc + buf[d]
        tot_ref[...] = acc
        for cp in swaps:
            cp.wait()

    any_spec = pl.BlockSpec(memory_space=pl.ANY)
    vm = pl.BlockSpec(memory_space=pltpu.VMEM)
    return pl.pallas_call(
        body, name="finish_reduce",
        out_shape=(jax.ShapeDtypeStruct(fwt.shape, F32), jax.ShapeDtypeStruct(fsq.shape, F32),
                   jax.ShapeDtypeStruct((VEC_ROWS, D), F32)),
        in_specs=[any_spec, any_spec, vm], out_specs=(any_spec, any_spec, vm),
        input_output_aliases={0: 0, 1: 1},
        scratch_shapes=[pltpu.VMEM((8, VEC_ROWS, D), F32), pltpu.SemaphoreType.DMA((9,)),
                        pltpu.SemaphoreType.DMA((9,))],
    )(fwt, fsq, vec)


def _inproj(h, wt, sq_shards, wpp_shard, tm):
    T = h.shape[0]
    nsq = len(sq_shards)
    n_i = T // tm

    def body(*refs):
        h_ref, wt_ref = refs[:2]
        sq_refs = refs[2:2 + nsq]
        wpp_ref = refs[2 + nsq]
        z_ref, zkv_ref, wsq_ref, wppf_ref, wbuf, stage, send_sems, recv_sems, loc_sem, wsem = refs[3 + nsq:]
        i = pl.program_id(0)

        def src(p, off, n):
            k = p - SQ_PIECES[0]
            return _piece_rows(wpp_ref if k == nsq else sq_refs[k], 0, off, n)

        def landing(p, s, off, n):
            k = p - SQ_PIECES[0]
            if k == nsq:
                return _piece_rows(wppf_ref, WPP_SHARD * s, off, n)
            return _piece_rows(wsq_ref, D * k + SQ_SHARD * s, off, n)

        send, forward, finish = _gather_ops(SQ_PIECES, src, landing, wsq_ref, stage, send_sems, recv_sems,
                                            loc_sem)

        @pl.when(i == 0)
        def _():
            send()
            load = pltpu.make_async_copy(wt_ref, wbuf, wsem)
            load.start()
            load.wait()

        @pl.when(i == n_i // 2)
        def _():
            forward()

        h = h_ref[...]
        for j in range(7):
            z_ref[:, j * D:(j + 1) * D] = _mm_nt(h, wbuf[j * D:(j + 1) * D, :]).astype(BF16)
        zkv_ref[...] = _mm_nt(h, wbuf[ZKV:IN_WIDTH, :]).astype(BF16)

        @pl.when(i == n_i - 1)
        def _():
            finish()

    any_spec = pl.BlockSpec(memory_space=pl.ANY)
    return pl.pallas_call(
        body, name="inproj", grid=(n_i,),
        in_specs=[pl.BlockSpec((tm, D), lambda i: (i, 0))] + [any_spec] * (nsq + 2),
        out_specs=(pl.BlockSpec((tm, ZKV), lambda i: (i, 0)), pl.BlockSpec((tm, 2 * BLOCK), lambda i: (i, 0)),
                   any_spec, any_spec),
        out_shape=(jax.ShapeDtypeStruct((T, ZKV), BF16), jax.ShapeDtypeStruct((T, 2 * BLOCK), BF16),
                   jax.ShapeDtypeStruct((nsq * D, D), BF16), jax.ShapeDtypeStruct((PLE, D), BF16)),
        scratch_shapes=[pltpu.VMEM((IN_WIDTH, D), BF16), pltpu.VMEM((SQ_SHARD, D), BF16),
                        pltpu.SemaphoreType.DMA((6,)), pltpu.SemaphoreType.DMA((6,)), pltpu.SemaphoreType.DMA,
                        pltpu.SemaphoreType.DMA],
        compiler_params=_params(("arbitrary",)),
    )(h, wt, *sq_shards, wpp_shard)


HALO = 32
CONV_RC = 64
CONV_LC = 256


def _conv_taps(w_ref, src, r0, lane0, offset_of_tap):
    lanes = pl.ds(lane0, CONV_LC)
    out = None
    for b in range(8):
        taps = [k for k in range(CONV_K) if offset_of_tap(k) % 8 == b]
        if not taps:
            continue
        rows = CONV_RC + (8 if b else 0)
        vb = None
        for k in taps:
            term = w_ref[k:k + 1, lanes] * src[pl.ds(r0 + (offset_of_tap(k) - b), rows), lanes]
            vb = term if vb is None else vb + term
        vb = vb[b:b + CONV_RC] if b else vb
        out = vb if out is None else out + vb
    return out


def _conv_fwd(z, wdw, b_dw, ln_g, ln_b, wall, S, tm):
    T = z.shape[0]
    nt = S // tm
    hb = tm // HALO

    def body(cv_ref, cg_ref, cgate_ref, hcv_ref, hcg_ref, wdw_ref, bdw_ref, lng_ref, lnb_ref, wpw_ref,
             wbrc_ref, ya_ref, y_ref, rstd_ref, pw_ref, ubuf, cbuf):
        t = pl.program_id(1)
        ubuf[HALO:HALO + tm, :] = cv_ref[...].astype(F32) * _sig(cg_ref[...].astype(F32))
        hu = hcv_ref[...].astype(F32) * _sig(hcg_ref[...].astype(F32))
        ubuf[0:HALO, :] = jnp.where(t > 0, hu, 0.0)
        ubuf[HALO + tm:HALO + tm + 8, :] = jnp.zeros((8, D), F32)

        def chunk(ci, carry):
            r0 = pl.multiple_of(ci * CONV_RC, CONV_RC)
            for lg in range(D // CONV_LC):
                acc = _conv_taps(wdw_ref, ubuf, r0, lg * CONV_LC, lambda k: HALO - (CONV_K - 1) + k)
                cbuf[pl.ds(r0, CONV_RC), pl.ds(lg * CONV_LC, CONV_LC)] = acc
            return carry

        lax.fori_loop(0, tm // CONV_RC, chunk, 0)
        cc = cbuf[...] + bdw_ref[...]
        mu = jnp.mean(cc, axis=-1, keepdims=True)
        dd = cc - mu
        rstd = lax.rsqrt(jnp.mean(dd * dd, axis=-1, keepdims=True) + EPS)
        yn = dd * rstd
        y_ref[...] = yn.astype(BF16)
        rstd_ref[...] = rstd
        n = yn * lng_ref[...] + lnb_ref[...]
        s = n * _sig(n)
        pw = _mm(s.astype(BF16), wpw_ref[...])
        pw_ref[...] = pw.astype(BF16)
        gt = cgate_ref[...].astype(F32)
        ya_in = pw * (gt * _sig(gt))
        ya_ref[...] = _mm(ya_in.astype(BF16), wbrc_ref[...]).astype(BF16)

    def row(b, t):
        return b * nt + t

    def halo(b, t):
        return jnp.maximum(row(b, t) * hb - 1, 0)

    vec = pl.BlockSpec((1, D), lambda b, t: (0, 0))
    tile = lambda j: pl.BlockSpec((tm, D), lambda b, t: (row(b, t), j))
    out_tile = pl.BlockSpec((tm, D), lambda b, t: (row(b, t), 0))
    return pl.pallas_call(
        body, name="conv_fwd", grid=(T // S, nt),
        in_specs=[tile(ZB_CVAL), tile(ZB_CGLU), tile(ZB_CGATE),
                  pl.BlockSpec((HALO, D), lambda b, t: (halo(b, t), ZB_CVAL)),
                  pl.BlockSpec((HALO, D), lambda b, t: (halo(b, t), ZB_CGLU)),
                  pl.BlockSpec((32, D), lambda b, t: (0, 0)), vec, vec, vec,
                  pl.BlockSpec((D, D), lambda b, t: (0, 0)),
                  pl.BlockSpec((D, D), lambda b, t: (1, 0))],
        out_specs=(out_tile, out_tile, pl.BlockSpec((tm, 1), lambda b, t: (row(b, t), 0)), out_tile),
        out_shape=(jax.ShapeDtypeStruct((T, D), BF16), jax.ShapeDtypeStruct((T, D), BF16),
                   jax.ShapeDtypeStruct((T, 1), F32), jax.ShapeDtypeStruct((T, D), BF16)),
        scratch_shapes=[pltpu.VMEM((tm + HALO + 8, D), F32), pltpu.VMEM((tm, D), F32)],
        compiler_params=_params(("arbitrary", "arbitrary")),
    )(z, z, z, z, z, wdw, b_dw, ln_g, ln_b, wall, wall)


def _swap_matrix():
    r = lax.broadcasted_iota(jnp.int32, (BLOCK, BLOCK), 0)
    l = lax.broadcasted_iota(jnp.int32, (BLOCK, BLOCK), 1)
    lh = l & (HEAD_DIM - 1)
    half = ROPE_DIM // 2
    hit = ((lh < half) & (r == l + half)) | ((lh >= half) & (lh < ROPE_DIM) & (r == l - half))
    return jnp.where(hit, 1.0, 0.0).astype(BF16)


def _rope(tb, cos, sin, pswap):
    return tb.astype(F32) * cos + _mm(tb, pswap) * sin


def _rope_f32(tv, cos, sin, pswap):
    hi = tv.astype(BF16)
    lo = (tv - hi.astype(F32)).astype(BF16)
    return tv * cos + (_mm(hi, pswap) + _mm(lo, pswap)) * sin


def _kv_variants(kv):
    lane = lax.broadcasted_iota(jnp.int32, kv.shape, 1)
    lo = lane < HEAD_DIM
    sw = pltpu.roll(kv, HEAD_DIM, 1)
    z = jnp.zeros_like(kv)
    g0 = (jnp.where(lo, kv, z).astype(BF16), jnp.where(lo, z, sw).astype(BF16))
    g1 = (jnp.where(lo, sw, z).astype(BF16), jnp.where(lo, z, kv).astype(BF16))
    return (g0, g1)


def _band_mask(nq):
    qi = lax.broadcasted_iota(jnp.int32, (nq * BLOCK, 2 * BLOCK), 0) & (BLOCK - 1)
    sj = lax.broadcasted_iota(jnp.int32, (nq * BLOCK, 2 * BLOCK), 1)
    return (sj <= qi + BLOCK) & (sj > qi), sj


def _sink_rep(sink_ref, g, e):
    return jnp.concatenate(
        [jnp.full((BLOCK, BLOCK), sink_ref[8 * g + 2 * j + e], F32) for j in range(4)], axis=0)


def _softmax_parts(s, valid, sk):
    rows = s.shape[0]
    s = jnp.where(valid, s, -1e30)
    m = jnp.maximum(jnp.broadcast_to(jnp.max(s, axis=-1, keepdims=True), (rows, BLOCK)), sk)
    return jnp.exp(s - jnp.concatenate([m, m], axis=1)), jnp.exp(sk - m)


def _softmax_sink(s, valid, sk):
    p, ps = _softmax_parts(s, valid, sk)
    inv = 1.0 / (_mm(p.astype(BF16), jnp.ones((2 * BLOCK, BLOCK), BF16)) + ps)
    return p * jnp.concatenate([inv, inv], axis=1), ps * inv


def _attn_fwd(z, zkv, cos_t, sin_t, sinks, S, tq):
    T = z.shape[0]
    nt = S // tq
    nq = tq // BLOCK

    def body(sink_ref, q_ref, kv_ref, hkv_ref, cos_ref, sin_ref, hcos_ref, hsin_ref, o_ref):
        t = pl.program_id(1)
        cos = cos_ref[...]
        sin = sin_ref[...]
        pswap = _swap_matrix()
        kv = jnp.concatenate([hkv_ref[...], kv_ref[...]], axis=0)
        cos_k = jnp.concatenate([hcos_ref[...], cos], axis=0)
        sin_k = jnp.concatenate([hsin_ref[...], sin], axis=0)
        kx = _kv_variants(_rope(kv[:, :BLOCK], cos_k, sin_k, pswap))
        one = jnp.ones((tq + BLOCK, BLOCK), BF16)
        vx = [[jnp.concatenate([v, one], axis=1) for v in vg] for vg in _kv_variants(kv[:, BLOCK:].astype(F32))]
        band, sj = _band_mask(4)
        qs = [(_rope(q_ref[:, 128 * hp:128 * hp + 128], cos, sin, pswap) * 0.125).astype(BF16)
              for hp in range(8)]
        for n in range(nq):
            first = (t == 0) & (n == 0)
            valid = band & (jnp.logical_not(first) | (sj >= BLOCK))
            r0 = n * BLOCK
            for g in range(2):
                lhs = jnp.concatenate([qs[4 * g + j][r0:r0 + BLOCK] for j in range(4)], axis=0)
                acc = jnp.zeros((4 * BLOCK, BLOCK), F32)
                for e in range(2):
                    s = _mm_nt(lhs, kx[g][e][r0:r0 + 2 * BLOCK])
                    p, ps = _softmax_parts(s, valid, _sink_rep(sink_ref, g, e))
                    r = _mm(p.astype(BF16), vx[g][e][r0:r0 + 2 * BLOCK])
                    acc = acc + r[:, 0:BLOCK] * (1.0 / (r[:, BLOCK:2 * BLOCK] + ps))
                for j in range(4):
                    o_ref[r0:r0 + BLOCK, 128 * (4 * g + j):128 * (4 * g + j) + 128] = (
                        acc[j * BLOCK:(j + 1) * BLOCK].astype(BF16))

    def row(b, t):
        return b * nt + t

    def halo(b, t):
        return jnp.maximum(row(b, t) * nq - 1, 0)

    return pl.pallas_call(
        body, name="attn_fwd", grid=(T // S, nt),
        in_specs=[pl.BlockSpec(memory_space=pltpu.SMEM),
                  pl.BlockSpec((tq, D), lambda b, t: (row(b, t), ZB_Q)),
                  pl.BlockSpec((tq, 2 * BLOCK), lambda b, t: (row(b, t), 0)),
                  pl.BlockSpec((BLOCK, 2 * BLOCK), lambda b, t: (halo(b, t), 0)),
                  pl.BlockSpec((tq, BLOCK), lambda b, t: (row(b, t), 0)),
                  pl.BlockSpec((tq, BLOCK), lambda b, t: (row(b, t), 0)),
                  pl.BlockSpec((BLOCK, BLOCK), lambda b, t: (halo(b, t), 0)),
                  pl.BlockSpec((BLOCK, BLOCK), lambda b, t: (halo(b, t), 0))],
        out_specs=pl.BlockSpec((tq, D), lambda b, t: (row(b, t), 0)),
        out_shape=jax.ShapeDtypeStruct((T, D), BF16),
        compiler_params=_params(("arbitrary", "arbitrary")),
    )(sinks, z, zkv, zkv, cos_t, sin_t, cos_t, sin_t)


def _tail_a(x, tgt, p, o, ya, z, ln_post, wall, wppt, tm):
    T = x.shape[0]
    last = T // tm - 1

    def body(x_ref, tgt_ref, p_ref, o_ref, ya_ref, ag_ref, gc_ref, ga_ref, lnp_ref, wbra_ref, wout_ref,
             wpg_ref, wppt_ref, loss_ref, dx1_ref, dm_ref, yb_ref, glnp_ref, gpack_ref, gwpp_ref,
             acc_out, acc_pg, sem):
        i = pl.program_id(0)

        @pl.when(i == 0)
        def _():
            acc_out[...] = jnp.zeros_like(acc_out)
            acc_pg[...] = jnp.zeros_like(acc_pg)
            gwpp_ref[...] = jnp.zeros_like(gwpp_ref)
            glnp_ref[...] = jnp.zeros_like(glnp_ref)
            loss_ref[...] = jnp.zeros_like(loss_ref)

        ag = ag_ref[...].astype(F32)
        yb_in = (o_ref[...].astype(F32) * (ag * _sig(ag))).astype(BF16)
        yb = _mm(yb_in, wbra_ref[...])
        yb_ref[...] = yb.astype(BF16)
        m = (_sig(gc_ref[...].astype(F32)) * ya_ref[...].astype(F32)
             + _sig(ga_ref[...].astype(F32)) * yb).astype(BF16)
        mo = _mm(m, wout_ref[...])
        r2 = lax.rsqrt(jnp.mean(mo * mo, axis=-1, keepdims=True) + EPS)
        nrm = mo * r2
        g_post = lnp_ref[...]
        x1 = x_ref[...] + nrm * g_post
        x1b = x1.astype(BF16)
        gate = _sig(_mm(x1b, wpg_ref[...]))
        pb = p_ref[...].astype(BF16)
        pp = _mm_nt(pb, wppt_ref[...])
        err = x1 + gate * pp - tgt_ref[...]
        loss_ref[...] += 0.5 * jnp.sum(jnp.sum(err * err, axis=-1, keepdims=True) * (1.0 / D),
                                       axis=0, keepdims=True)
        dx2 = err * (1.0 / D)
        dgp = (dx2 * pp * gate * (1.0 - gate)).astype(BF16)
        dpp = (dx2 * gate).astype(BF16)
        dx1 = dx2 + _mm_nt(dgp, wpg_ref[...])
        dx1_ref[...] = dx1
        acc_pg[...] += _mm_tn(x1b, dgp)
        gwpp_ref[...] += _mm_tn(dpp, pb)
        glnp_ref[...] += jnp.sum(dx1 * nrm, axis=0, keepdims=True)
        a = dx1 * g_post
        dmo = (r2 * (a - nrm * jnp.mean(a * nrm, axis=-1, keepdims=True))).astype(BF16)
        dm_ref[...] = _mm_nt(dmo, wout_ref[...]).astype(BF16)
        acc_out[...] += _mm_tn(m, dmo)

        @pl.when(i == last)
        def _():
            _flush_to_pack(acc_out, gpack_ref, 3 * D, sem.at[0])
            _flush_to_pack(acc_pg, gpack_ref, 4 * D, sem.at[1])

    tile = pl.BlockSpec((tm, D), lambda i: (i, 0))
    ztile = lambda j: pl.BlockSpec((tm, D), lambda i: (i, j))
    wsq = lambda k: pl.BlockSpec((D, D), lambda i: (k, 0))
    const = lambda shp: pl.BlockSpec(shp, lambda i: (0, 0))
    any_spec = pl.BlockSpec(memory_space=pl.ANY)
    return pl.pallas_call(
        body, name="tail_a", grid=(T // tm,),
        in_specs=[tile, tile, pl.BlockSpec((tm, PLE), lambda i: (i, 0)), tile, tile, ztile(ZB_AGATE),
                  ztile(ZB_GCONV), ztile(ZB_GATTN), const((1, D)), wsq(2), wsq(3), wsq(4), const((D, PLE))],
        out_specs=(const((1, 1)), tile, tile, tile, const((1, D)), any_spec, const((D, PLE))),
        out_shape=(jax.ShapeDtypeStruct((1, 1), F32), jax.ShapeDtypeStruct((T, D), F32),
                   jax.ShapeDtypeStruct((T, D), BF16), jax.ShapeDtypeStruct((T, D), BF16),
                   jax.ShapeDtypeStruct((1, D), F32), jax.ShapeDtypeStruct((N_SHARDS, SQ_PACK, D), F32),
                   jax.ShapeDtypeStruct((D, PLE), F32)),
        scratch_shapes=[pltpu.VMEM((D, D), F32), pltpu.VMEM((D, D), F32), pltpu.SemaphoreType.DMA((2,))],
        compiler_params=_params(("arbitrary",)),
    )(x, tgt, p, o, ya, z, z, z, ln_post, wall, wall, wall, wppt)


def _dsilu(v, sg):
    return sg * (1.0 + v * (1.0 - sg))


def _tail_b(dm, ya, yb, o, z, pw, y, rstd, ln_g, ln_b, wall, gppt, gpack, tm):
    T = dm.shape[0]
    last = T // tm - 1

    def body(dm_ref, ya_ref, yb_ref, o_ref, ag_ref, gc_ref, ga_ref, cgate_ref, pw_ref, y_ref, rstd_ref,
             lng_ref, lnb_ref, wpw_ref, wbrc_ref, wbra_ref, gppt_ref, gpack_in, dg_ref, do_ref, dc_ref,
             gvec_ref, gpack_ref, acc_bra, acc_brc, acc_pw, sem):
        i = pl.program_id(0)

        @pl.when(i == 0)
        def _():
            acc_bra[...] = jnp.zeros_like(acc_bra)
            acc_brc[...] = jnp.zeros_like(acc_brc)
            acc_pw[...] = jnp.zeros_like(acc_pw)
            gvec_ref[...] = jnp.zeros_like(gvec_ref)

        g = lng_ref[...]

        def part(rs):
            dm_v = dm_ref[rs, :].astype(F32)
            sgc = _sig(gc_ref[rs, :].astype(F32))
            sga = _sig(ga_ref[rs, :].astype(F32))
            dya = (dm_v * sgc).astype(BF16)
            dyb = (dm_v * sga).astype(BF16)
            dg_ref[rs, D:2 * D] = (dm_v * ya_ref[rs, :].astype(F32) * sgc * (1.0 - sgc)).astype(BF16)
            dg_ref[rs, 2 * D:3 * D] = (dm_v * yb_ref[rs, :].astype(F32) * sga * (1.0 - sga)).astype(BF16)
            ag = ag_ref[rs, :].astype(F32)
            sag = _sig(ag)
            sa = ag * sag
            ov = o_ref[rs, :].astype(F32)
            dyb_in = _mm_nt(dyb, wbra_ref[...])
            do_ref[rs, :] = (dyb_in * sa).astype(BF16)
            dg_ref[rs, 0:D] = (dyb_in * ov * _dsilu(ag, sag)).astype(BF16)
            gt = cgate_ref[rs, :].astype(F32)
            sgt = _sig(gt)
            sgate = gt * sgt
            pw = pw_ref[rs, :].astype(F32)
            dya_in = _mm_nt(dya, wbrc_ref[...])
            dpw = (dya_in * sgate).astype(BF16)
            dg_ref[rs, 3 * D:4 * D] = (dya_in * pw * _dsilu(gt, sgt)).astype(BF16)
            yn = y_ref[rs, :].astype(F32)
            n = yn * g + lnb_ref[...]
            sn = _sig(n)
            dn = _mm_nt(dpw, wpw_ref[...]) * _dsilu(n, sn)
            dy = dn * g
            dc = rstd_ref[rs, :] * (dy - jnp.mean(dy, axis=-1, keepdims=True)
                                    - yn * jnp.mean(dy * yn, axis=-1, keepdims=True))
            dc_ref[rs, :] = dc.astype(BF16)
            sums = (jnp.sum(dn * yn, axis=0, keepdims=True), jnp.sum(dn, axis=0, keepdims=True),
                    jnp.sum(dc, axis=0, keepdims=True))
            return ((ov * sa).astype(BF16), dyb, (pw * sgate).astype(BF16), dya, (n * sn).astype(BF16), dpw,
                    sums)

        parts = [part(pl.ds(r * (tm // TAIL_PARTS), tm // TAIL_PARTS)) for r in range(TAIL_PARTS)]
        cat = lambda j: jnp.concatenate([pt[j] for pt in parts], axis=0)
        acc_bra[...] += _mm_tn(cat(0), cat(1))
        acc_brc[...] += _mm_tn(cat(2), cat(3))
        acc_pw[...] += _mm_tn(cat(4), cat(5))
        for j in range(3):
            gvec_ref[j:j + 1, :] += sum(pt[6][j] for pt in parts)

        @pl.when(i == last)
        def _():
            _flush_to_pack(acc_pw, gpack_ref, 0, sem.at[0])
            _flush_to_pack(acc_brc, gpack_ref, D, sem.at[1])
            _flush_to_pack(acc_bra, gpack_ref, 2 * D, sem.at[2])
            _flush_to_pack(gppt_ref, gpack_ref, WPP0, sem.at[0])

    tile = pl.BlockSpec((tm, D), lambda i: (i, 0))
    ztile = lambda j: pl.BlockSpec((tm, D), lambda i: (i, j))
    wsq = lambda k: pl.BlockSpec((D, D), lambda i: (k, 0))
    const = lambda shp: pl.BlockSpec(shp, lambda i: (0, 0))
    any_spec = pl.BlockSpec(memory_space=pl.ANY)
    return pl.pallas_call(
        body, name="tail_b", grid=(T // tm,),
        in_specs=[tile, tile, tile, tile, ztile(ZB_AGATE), ztile(ZB_GCONV), ztile(ZB_GATTN), ztile(ZB_CGATE),
                  tile, tile, pl.BlockSpec((tm, 1), lambda i: (i, 0)), const((1, D)), const((1, D)), wsq(0),
                  wsq(1), wsq(2), const((PLE, D)), any_spec],
        out_specs=(pl.BlockSpec((tm, 4 * D), lambda i: (i, 0)), tile, tile, const((8, D)), any_spec),
        out_shape=(jax.ShapeDtypeStruct((T, 7 * D), BF16), jax.ShapeDtypeStruct((T, D), BF16),
                   jax.ShapeDtypeStruct((T, D), BF16), jax.ShapeDtypeStruct((8, D), F32),
                   jax.ShapeDtypeStruct(gpack.shape, F32)),
        input_output_aliases={17: 4},
        scratch_shapes=[pltpu.VMEM((D, D), F32), pltpu.VMEM((D, D), F32), pltpu.VMEM((D, D), F32),
                        pltpu.SemaphoreType.DMA((3,))],
        compiler_params=_params(("arbitrary",)),
    )(dm, ya, yb, o, z, z, z, z, pw, y, rstd, ln_g, ln_b, wall, wall, wall, gppt, gpack)


def _conv_bwd(dc, z, wdw, dz, S, tm, copies, src, landing):
    T = dc.shape[0]
    nt = S // tm
    hb = tm // HALO
    nrows = T // HALO

    def body(dc_ref, hdc_ref, cv_ref, cg_ref, hcv_ref, hcg_ref, wdw_ref, dz_in, src_ref, dz_ref, gw_ref,
             land_ref, ubuf, dcbuf, dubuf, dwacc, shbuf, send_sems, recv_sems):
        b = pl.program_id(0)
        t = pl.program_id(1)

        @pl.when((b == 0) & (t == 0))
        def _():
            dwacc[...] = jnp.zeros_like(dwacc)
            for cp in copies(src_ref, land_ref, send_sems, recv_sems):
                cp.start()

        cv = cv_ref[...].astype(F32)
        sg = _sig(cg_ref[...].astype(F32))
        ubuf[HALO:HALO + tm, :] = cv * sg
        hu = hcv_ref[...].astype(F32) * _sig(hcg_ref[...].astype(F32))
        ubuf[0:HALO, :] = jnp.where(t > 0, hu, 0.0)
        ubuf[HALO + tm:HALO + tm + 8, :] = jnp.zeros((8, D), F32)
        dcbuf[0:tm, :] = dc_ref[...].astype(F32)
        dcbuf[tm:tm + HALO, :] = jnp.where(t < nt - 1, hdc_ref[...].astype(F32), 0.0)
        dcbuf[tm + HALO:tm + HALO + 8, :] = jnp.zeros((8, D), F32)

        def chunk(ci, carry):
            r0 = pl.multiple_of(ci * CONV_RC, CONV_RC)
            for lg in range(D // CONV_LC):
                l0 = lg * CONV_LC
                dubuf[pl.ds(r0, CONV_RC), pl.ds(l0, CONV_LC)] = _conv_taps(
                    wdw_ref, dcbuf, r0, l0, lambda k: CONV_K - 1 - k)
                dcc = dcbuf[pl.ds(r0, CONV_RC), pl.ds(l0, CONV_LC)]
                zero8 = jnp.zeros((8, CONV_LC), F32)
                dcz = jnp.concatenate([zero8, dcc, zero8], axis=0)
                for bb in range(8):
                    taps = [k for k in range(CONV_K) if (HALO - (CONV_K - 1) + k) % 8 == bb]
                    if not taps:
                        continue
                    rows = CONV_RC + (8 if bb else 0)
                    if bb:
                        shbuf[bb] = dcz[8 - bb:8 - bb + rows]
                    for k in taps:
                        a8 = HALO - (CONV_K - 1) + k - bb
                        dcs = shbuf[bb] if bb else dcc
                        prod = dcs * ubuf[pl.ds(r0 + a8, rows), pl.ds(l0, CONV_LC)]
                        part = prod[0:8]
                        for q in range(1, rows // 8):
                            part = part + prod[8 * q:8 * q + 8]
                        dwacc[8 * k:8 * k + 8, pl.ds(l0, CONV_LC)] += part
            return carry

        lax.fori_loop(0, tm // CONV_RC, chunk, 0)
        du = dubuf[...]
        dz_ref[:, 0:D] = (du * sg).astype(BF16)
        dz_ref[:, D:2 * D] = (du * cv * sg * (1.0 - sg)).astype(BF16)

        @pl.when((b == pl.num_programs(0) - 1) & (t == nt - 1))
        def _():
            for k in range(32):
                gw_ref[k:k + 1, :] = jnp.sum(dwacc[8 * k:8 * k + 8, :], axis=0, keepdims=True)
            cps = copies(src_ref, land_ref, send_sems, recv_sems)
            for cp in cps:
                cp.wait_recv()
            for cp in cps:
                cp.wait_send()

    def row(b, t):
        return b * nt + t

    def prev_halo(b, t):
        return jnp.maximum(row(b, t) * hb - 1, 0)

    def next_halo(b, t):
        return jnp.minimum((row(b, t) + 1) * hb, nrows - 1)

    return pl.pallas_call(
        body, name="conv_bwd", grid=(T // S, nt),
        in_specs=[pl.BlockSpec((tm, D), lambda b, t: (row(b, t), 0)),
                  pl.BlockSpec((HALO, D), lambda b, t: (next_halo(b, t), 0)),
                  pl.BlockSpec((tm, D), lambda b, t: (row(b, t), ZB_CVAL)),
                  pl.BlockSpec((tm, D), lambda b, t: (row(b, t), ZB_CGLU)),
                  pl.BlockSpec((HALO, D), lambda b, t: (prev_halo(b, t), ZB_CVAL)),
                  pl.BlockSpec((HALO, D), lambda b, t: (prev_halo(b, t), ZB_CGLU)),
                  pl.BlockSpec((32, D), lambda b, t: (0, 0)),
                  pl.BlockSpec(memory_space=pl.ANY), pl.BlockSpec(memory_space=pl.ANY)],
        out_specs=(pl.BlockSpec((tm, 2 * D), lambda b, t: (row(b, t), ZB_CVAL // 2)),
                   pl.BlockSpec((32, D), lambda b, t: (0, 0)), pl.BlockSpec(memory_space=pl.ANY)),
        out_shape=(jax.ShapeDtypeStruct(dz.shape, BF16), jax.ShapeDtypeStruct((32, D), F32), landing),
        input_output_aliases={7: 0},
        scratch_shapes=[pltpu.VMEM((tm + HALO + 8, D), F32), pltpu.VMEM((tm + HALO + 8, D), F32),
                        pltpu.VMEM((tm, D), F32), pltpu.VMEM((8 * 32, D), F32),
                        pltpu.VMEM((8, CONV_RC + 8, CONV_LC), F32), pltpu.SemaphoreType.DMA((3,)),
                        pltpu.SemaphoreType.DMA((3,))],
        compiler_params=_params(("arbitrary", "arbitrary")),
    )(dc, dc, z, z, z, z, wdw, dz, src)


def _attn_bwd(z, zkv, o, do, cos_t, sin_t, sinks, dz, S, tq, copies, src, landing):
    T = z.shape[0]
    nt = S // tq
    nq = tq // BLOCK

    def body(sink_ref, q_ref, kv_ref, hkv_ref, o_ref, do_ref, cos_ref, sin_ref, hcos_ref, hsin_ref, dz_in,
             src_ref, dq_ref, dkv_ref, gs_ref, land_ref, carry, dkacc, dvacc, send_sems, recv_sems):
        b = pl.program_id(0)
        tt = pl.program_id(1)
        t = nt - 1 - tt

        @pl.when((b == 0) & (tt == 0))
        def _():
            gs_ref[...] = jnp.zeros_like(gs_ref)
            for cp in copies(src_ref, land_ref, send_sems, recv_sems):
                cp.start()

        @pl.when(tt == 0)
        def _():
            carry[...] = jnp.zeros_like(carry)

        cos = cos_ref[...]
        sin = sin_ref[...]
        pswap = _swap_matrix()
        kv = jnp.concatenate([hkv_ref[...], kv_ref[...]], axis=0)
        cos_k = jnp.concatenate([hcos_ref[...], cos], axis=0)
        sin_k = jnp.concatenate([hsin_ref[...], sin], axis=0)
        kx = _kv_variants(_rope(kv[:, :BLOCK], cos_k, sin_k, pswap))
        vx = _kv_variants(kv[:, BLOCK:].astype(F32))
        band, sj = _band_mask(4)
        lo = lax.broadcasted_iota(jnp.int32, (4 * BLOCK, BLOCK), 1) < HEAD_DIM
        ones = jnp.ones((2 * BLOCK, 2 * BLOCK), BF16)
        qs = [(_rope(q_ref[:, 128 * hp:128 * hp + 128], cos, sin, pswap) * 0.125).astype(BF16)
              for hp in range(8)]
        dkacc[...] = jnp.zeros_like(dkacc)
        dvacc[...] = jnp.zeros_like(dvacc)
        gsum = jnp.zeros((1, BLOCK), F32)
        hlane = lax.broadcasted_iota(jnp.int32, (1, BLOCK), 1)
        for n in range(nq):
            first = (t == 0) & (n == 0)
            valid = band & (jnp.logical_not(first) | (sj >= BLOCK))
            r0 = n * BLOCK
            for g in range(2):
                cols = [slice(128 * (4 * g + j), 128 * (4 * g + j) + 128) for j in range(4)]
                lhs = jnp.concatenate([qs[4 * g + j][r0:r0 + BLOCK] for j in range(4)], axis=0)
                dov = jnp.concatenate([do_ref[r0:r0 + BLOCK, cs] for cs in cols], axis=0)
                prod = dov.astype(F32) * jnp.concatenate(
                    [o_ref[r0:r0 + BLOCK, cs] for cs in cols], axis=0).astype(F32)
                lhs_t = lhs.T
                dov_t = dov.T
                dq = jnp.zeros((4 * BLOCK, BLOCK), F32)
                dk_t = jnp.zeros((HEAD_DIM, 2 * BLOCK), F32)
                dv_t = jnp.zeros((HEAD_DIM, 2 * BLOCK), F32)
                for e in range(2):
                    kw = kx[g][e][r0:r0 + 2 * BLOCK]
                    vw = vx[g][e][r0:r0 + 2 * BLOCK]
                    s = _mm_nt(lhs, kw)
                    p, psink = _softmax_sink(s, valid, _sink_rep(sink_ref, g, e))
                    pe = jnp.where(lo if e == 0 else jnp.logical_not(lo), prod, 0.0)
                    pe_hi = pe.astype(BF16)
                    pe_lo = (pe - pe_hi.astype(F32)).astype(BF16)
                    delta = _mm(jnp.concatenate([pe_hi, pe_lo], axis=1), ones)
                    ds = (p * (_mm_nt(dov, vw) - delta)).astype(BF16)
                    dq = dq + _mm(ds, kw)
                    dims = slice(HEAD_DIM * e, HEAD_DIM * (e + 1))
                    dk_t = dk_t + _mm(lhs_t[dims], ds)
                    dv_t = dv_t + _mm(dov_t[dims], p.astype(BF16))
                    gs = -psink * delta[:, 0:BLOCK]
                    for j in range(4):
                        tot = jnp.sum(gs[j * BLOCK:(j + 1) * BLOCK], axis=0, keepdims=True)
                        gsum = gsum + jnp.where(hlane == 8 * g + 2 * j + e, tot, 0.0)
                dkacc[HEAD_DIM * g:HEAD_DIM * (g + 1), r0:r0 + 2 * BLOCK] += dk_t
                dvacc[HEAD_DIM * g:HEAD_DIM * (g + 1), r0:r0 + 2 * BLOCK] += dv_t
                for j in range(4):
                    dqj = _rope_f32(dq[j * BLOCK:(j + 1) * BLOCK] * 0.125, cos[r0:r0 + BLOCK],
                                    -sin[r0:r0 + BLOCK], pswap)
                    dq_ref[r0:r0 + BLOCK, cols[j]] = dqj.astype(BF16)
        gs_ref[0:1, :] += gsum
        dk_all = dkacc[...]
        dv_all = dvacc[...]
        dk_last = dk_all[:, tq:tq + BLOCK] + carry[0:BLOCK, :]
        dv_last = dv_all[:, tq:tq + BLOCK] + carry[BLOCK:2 * BLOCK, :]
        carry[0:BLOCK, :] = dk_all[:, 0:BLOCK]
        carry[BLOCK:2 * BLOCK, :] = dv_all[:, 0:BLOCK]
        if nq > 1:
            dk_tile = jnp.concatenate([dk_all[:, BLOCK:tq], dk_last], axis=1)
            dv_tile = jnp.concatenate([dv_all[:, BLOCK:tq], dv_last], axis=1)
        else:
            dk_tile, dv_tile = dk_last, dv_last
        dkv_ref[:, 0:BLOCK] = _rope_f32(dk_tile.T, cos, -sin, pswap).astype(BF16)
        dkv_ref[:, BLOCK:2 * BLOCK] = dv_tile.T.astype(BF16)

        @pl.when((b == pl.num_programs(0) - 1) & (tt == nt - 1))
        def _():
            cps = copies(src_ref, land_ref, send_sems, recv_sems)
            for cp in cps:
                cp.wait_recv()
            for cp in cps:
                cp.wait_send()

    def row(b, tt):
        return b * nt + (nt - 1 - tt)

    def halo(b, tt):
        return jnp.maximum(row(b, tt) * nq - 1, 0)

    tile = pl.BlockSpec((tq, D), lambda b, tt: (row(b, tt), 0))
    return pl.pallas_call(
        body, name="attn_bwd", grid=(T // S, nt),
        in_specs=[pl.BlockSpec(memory_space=pltpu.SMEM),
                  pl.BlockSpec((tq, D), lambda b, tt: (row(b, tt), ZB_Q)),
                  pl.BlockSpec((tq, 2 * BLOCK), lambda b, tt: (row(b, tt), 0)),
                  pl.BlockSpec((BLOCK, 2 * BLOCK), lambda b, tt: (halo(b, tt), 0)),
                  tile, tile,
                  pl.BlockSpec((tq, BLOCK), lambda b, tt: (row(b, tt), 0)),
                  pl.BlockSpec((tq, BLOCK), lambda b, tt: (row(b, tt), 0)),
                  pl.BlockSpec((BLOCK, BLOCK), lambda b, tt: (halo(b, tt), 0)),
                  pl.BlockSpec((BLOCK, BLOCK), lambda b, tt: (halo(b, tt), 0)),
                  pl.BlockSpec(memory_space=pl.ANY), pl.BlockSpec(memory_space=pl.ANY)],
        out_specs=(pl.BlockSpec((tq, D), lambda b, tt: (row(b, tt), ZB_Q)),
                   pl.BlockSpec((tq, 2 * BLOCK), lambda b, tt: (row(b, tt), 0)),
                   pl.BlockSpec((8, BLOCK), lambda b, tt: (0, 0)), pl.BlockSpec(memory_space=pl.ANY)),
        out_shape=(jax.ShapeDtypeStruct(dz.shape, BF16), jax.ShapeDtypeStruct((T, 2 * BLOCK), BF16),
                   jax.ShapeDtypeStruct((8, BLOCK), F32), landing),
        input_output_aliases={10: 0},
        scratch_shapes=[pltpu.VMEM((2 * BLOCK, BLOCK), F32), pltpu.VMEM((BLOCK, tq + BLOCK), F32),
                        pltpu.VMEM((BLOCK, tq + BLOCK), F32), pltpu.SemaphoreType.DMA((3,)),
                        pltpu.SemaphoreType.DMA((3,))],
        compiler_params=_params(("arbitrary", "arbitrary")),
    )(sinks, z, zkv, zkv, o, do, cos_t, sin_t, cos_t, sin_t, dz, src)


def _dh(dz, dz_kv, wall, x, dx1, ln_pre, tm, tile0, ntiles, gx_prev, name, copies, src, landing):
    T = x.shape[0]
    nsem = 3

    def body(*refs):
        dz_ref, kv_ref, wt_ref, x_ref, dx1_ref, g_ref, src_ref = refs[:7]
        gx_ref, glp_ref, land_ref, wbuf, send_sems, recv_sems, wsem = refs[-7:]
        i = pl.program_id(0)

        @pl.when(i == 0)
        def _():
            glp_ref[...] = jnp.zeros_like(glp_ref)
            for cp in copies(src_ref, land_ref, send_sems, recv_sems):
                cp.start()
            load = pltpu.make_async_copy(wt_ref, wbuf, wsem)
            load.start()
            load.wait()

        dh = _mm(dz_ref[...], wbuf[0:ZKV, :]) + _mm(kv_ref[...], wbuf[ZKV:IN_WIDTH, :])
        xv = x_ref[...]
        r = lax.rsqrt(jnp.mean(xv * xv, axis=-1, keepdims=True) + EPS)
        xr = xv * r
        glp_ref[...] += jnp.sum(dh * xr, axis=0, keepdims=True)
        a = dh * g_ref[...]
        gx_ref[...] = dx1_ref[...] + r * (a - xr * jnp.mean(a * xr, axis=-1, keepdims=True))

        @pl.when(i == ntiles - 1)
        def _():
            cps = copies(src_ref, land_ref, send_sems, recv_sems)
            for cp in cps:
                cp.wait_recv()
            for cp in cps:
                cp.wait_send()

    tile = pl.BlockSpec((tm, D), lambda i: (tile0 + i, 0))
    any_spec = pl.BlockSpec(memory_space=pl.ANY)
    operands = [dz, dz_kv, wall, x, dx1, ln_pre, src] + ([] if gx_prev is None else [gx_prev])
    return pl.pallas_call(
        body, name=name, grid=(ntiles,),
        in_specs=[pl.BlockSpec((tm, ZKV), lambda i: (tile0 + i, 0)),
                  pl.BlockSpec((tm, 2 * BLOCK), lambda i: (tile0 + i, 0)),
                  any_spec, tile, tile, pl.BlockSpec((1, D), lambda i: (0, 0)), any_spec]
        + ([] if gx_prev is None else [any_spec]),
        out_specs=(tile, pl.BlockSpec((1, D), lambda i: (0, 0)), any_spec),
        out_shape=(jax.ShapeDtypeStruct((T, D), F32), jax.ShapeDtypeStruct((1, D), F32), landing),
        input_output_aliases={} if gx_prev is None else {7: 0},
        scratch_shapes=[pltpu.VMEM((IN_WIDTH, D), BF16), pltpu.SemaphoreType.DMA((nsem,)),
                        pltpu.SemaphoreType.DMA((nsem,)), pltpu.SemaphoreType.DMA],
        compiler_params=_params(("arbitrary",)),
    )(*operands)


def _gwt(dz, h, tt):
    T = dz.shape[0]
    last = T // tt - 1

    def body(dz_ref, h_ref, gpack_ref, acc, sem):
        j = pl.program_id(0)
        t = pl.program_id(1)

        @pl.when(t == 0)
        def _():
            acc[...] = _mm_tn(dz_ref[...], h_ref[...])

        @pl.when(t > 0)
        def _():
            acc[...] += _mm_tn(dz_ref[...], h_ref[...])

        for jj in range(7):
            @pl.when((t == last) & (j == jj))
            def _(jj=jj):
                _flush_to_pack(acc, gpack_ref, WT0 + jj * D, sem)

    any_spec = pl.BlockSpec(memory_space=pl.ANY)
    return pl.pallas_call(
        body, name="gwt", grid=(7, T // tt),
        in_specs=[pl.BlockSpec((tt, D), lambda j, t: (t, j)), pl.BlockSpec((tt, D), lambda j, t: (t, 0))],
        out_specs=any_spec, out_shape=jax.ShapeDtypeStruct((N_SHARDS, WIN_SHARD, D), F32),
        scratch_shapes=[pltpu.VMEM((D, D), F32), pltpu.SemaphoreType.DMA],
        compiler_params=_params(("arbitrary", "arbitrary")),
    )(dz, h)


def _gwt_kv(dz_kv, h, gpack, tt):
    T = dz_kv.shape[0]
    last = T // tt - 1

    def body(dz_ref, h_ref, gpack_in, gpack_ref, acc, sem):
        t = pl.program_id(0)

        @pl.when(t == 0)
        def _():
            acc[...] = _mm_tn(dz_ref[...], h_ref[...])

        @pl.when(t > 0)
        def _():
            acc[...] += _mm_tn(dz_ref[...], h_ref[...])

        @pl.when(t == last)
        def _():
            _flush_to_pack(acc, gpack_ref, WT0 + ZKV, sem)

    any_spec = pl.BlockSpec(memory_space=pl.ANY)
    return pl.pallas_call(
        body, name="gwt_kv", grid=(T // tt,),
        in_specs=[pl.BlockSpec((tt, 2 * BLOCK), lambda t: (t, 0)), pl.BlockSpec((tt, D), lambda t: (t, 0)),
                  any_spec],
        out_specs=any_spec, out_shape=jax.ShapeDtypeStruct(gpack.shape, F32), input_output_aliases={2: 0},
        scratch_shapes=[pltpu.VMEM((2 * BLOCK, D), F32), pltpu.SemaphoreType.DMA],
        compiler_params=_params(("arbitrary",)),
    )(dz_kv, h, gpack)


_BC1 = 1.0 - ADAM_B1 ** ADAM_STEP
_BC2 = 1.0 - ADAM_B2 ** ADAM_STEP


def _adamw_math(w, g, m, v):
    m = ADAM_B1 * m + (1.0 - ADAM_B1) * g
    v = ADAM_B2 * v + (1.0 - ADAM_B2) * (g * g)
    delta = -ADAM_LR * ((m / _BC1) / (jnp.sqrt(v / _BC2) + ADAM_EPS) + ADAM_WD * w)
    return delta, m, v


def _adamw_rows(g, w, m, v, rows, name):
    R, C = w.shape

    def body(g_ref, w_ref, m_ref, v_ref, go_ref, d_ref, nm_ref, nv_ref):
        gv = g_ref[...]
        d, nm, nv = _adamw_math(w_ref[...], gv, m_ref[...], v_ref[...])
        go_ref[...] = gv
        d_ref[...] = d
        nm_ref[...] = nm
        nv_ref[...] = nv

    spec = pl.BlockSpec((rows, C), lambda i: (i, 0))
    shp = jax.ShapeDtypeStruct((R, C), F32)
    return pl.pallas_call(
        body, name=name, grid=(R // rows,), in_specs=[spec] * 4, out_specs=(spec,) * 4,
        out_shape=(shp,) * 4, compiler_params=_params(("arbitrary",)),
    )(g, w, m, v)


def _adamw_square(gfin, ws, ms, vs):
    rb = 64
    nb = SQ_SHARD // rb

    def body(*refs):
        g_refs = refs[0:5]
        w_refs, m_refs, v_refs = refs[5:10], refs[10:15], refs[15:20]
        outs = refs[20:]
        for k in range(5):
            gk = g_refs[k][...]
            d, nm, nv = _adamw_math(w_refs[k][...], gk, m_refs[k][...], v_refs[k][...])
            outs[4 * k][...] = gk
            outs[4 * k + 1][...] = d
            outs[4 * k + 2][...] = nm
            outs[4 * k + 3][...] = nv

    spec = pl.BlockSpec((rb, D), lambda i: (i, 0))
    gspecs = [pl.BlockSpec((rb, D), lambda i, k=k: (SQ_SHARD * k // rb + i, 0)) for k in range(5)]
    shp = jax.ShapeDtypeStruct((SQ_SHARD, D), F32)
    res = pl.pallas_call(
        body, name="adamw_square", grid=(nb,), in_specs=gspecs + [spec] * 15, out_specs=(spec,) * 20,
        out_shape=(shp,) * 20, compiler_params=_params(("arbitrary",)),
    )(*([gfin] * 5), *ws, *ms, *vs)
    return [tuple(res[4 * k:4 * k + 4]) for k in range(5)]


def _adamw_small(gs, ws, ms, vs):
    n = len(gs)

    def body(*refs):
        outs = refs[4 * n:]
        for k in range(n):
            d, nm, nv = _adamw_math(refs[n + k][...], refs[k][...], refs[2 * n + k][...],
                                    refs[3 * n + k][...])
            outs[3 * k][...] = d
            outs[3 * k + 1][...] = nm
            outs[3 * k + 2][...] = nv

    vm = pl.BlockSpec(memory_space=pltpu.VMEM)
    shapes = []
    for w in ws:
        shapes += [jax.ShapeDtypeStruct(w.shape, F32)] * 3
    res = pl.pallas_call(
        body, name="adamw_small", in_specs=[vm] * (4 * n), out_specs=(vm,) * (3 * n),
        out_shape=tuple(shapes),
    )(*gs, *ws, *ms, *vs)
    return [tuple(res[3 * k:3 * k + 3]) for k in range(n)]


def _rope_tables(positions):
    inv = jnp.power(ROPE_THETA, -jnp.arange(0, ROPE_DIM, 2, dtype=F32) / ROPE_DIM)
    inv_h = jnp.concatenate([inv, inv, jnp.zeros((HEAD_DIM - ROPE_DIM,), F32)])
    sign_h = np.array([-1.0] * (ROPE_DIM // 2) + [1.0] * (ROPE_DIM // 2) + [0.0] * (HEAD_DIM - ROPE_DIM),
                      np.float32)
    ang = positions.astype(F32).reshape(-1, 1) * jnp.concatenate([inv_h, inv_h])[None, :]
    return jnp.cos(ang), jnp.sin(ang) * np.concatenate([sign_h, sign_h])[None, :]


def kernel(x, p, positions, w_in, ln_pre, ln_post, w_dw, b_dw, conv_ln_g, conv_ln_b, w_pw, sinks, w_br_conv, w_br_attn, w_out, w_ple_gate, w_ple_proj, loss_target, m_w_in, m_ln_pre, m_ln_post, m_w_dw, m_b_dw, m_conv_ln_g, m_conv_ln_b, m_w_pw, m_sinks, m_w_br_conv, m_w_br_attn, m_w_out, m_w_ple_gate, m_w_ple_proj, v_w_in, v_ln_pre, v_ln_post, v_w_dw, v_b_dw, v_conv_ln_g, v_conv_ln_b, v_w_pw, v_sinks, v_w_br_conv, v_w_br_attn, v_w_out, v_w_ple_gate, v_w_ple_proj):
    nb, S, _ = x.shape
    T = nb * S
    xc = lax.axis_index("x")
    yc = lax.axis_index("y")
    cc = lax.axis_index("c")
    shard = 2 * xc + yc

    sq_w = (w_pw, w_br_conv, w_br_attn, w_out, w_ple_gate)
    wdw_shard = jnp.pad(w_dw[0], ((0, 1), (0, 0)))
    x2 = x.reshape(T, D)
    tm_res = min(TILE_RESIDENT, T // 2)
    wt, wdw_all, h = _gather_win(w_in[0].T.astype(BF16), wdw_shard, x2, ln_pre, tm_res)
    wdw = jnp.concatenate([wdw_all[s] for s in range(N_SHARDS)], axis=1)

    tgt = loss_target.reshape(T, D)
    p2 = p.reshape(T, PLE)
    cos_t, sin_t = _rope_tables(positions)
    sinks1 = sinks.reshape(N_HEADS)

    tm_big = min(TILE_PROJ, T)
    tm = min(TILE_TOKEN, S)
    tq = min(TILE_ATTN, S)

    z, zkv, wall, wppf = _inproj(h, wt, [w[0].astype(BF16) for w in sq_w],
                                 w_ple_proj[0].T.reshape(WPP_SHARD, D).astype(BF16), tm_res)
    wppt = wppf.reshape(D, PLE)
    ya, y, rstd, pw = _conv_fwd(z, wdw, b_dw, conv_ln_g, conv_ln_b, wall, S, tm)
    o = _attn_fwd(z, zkv, cos_t, sin_t, sinks1, S, tq)
    loss_p, dx1, dm, yb, g_ln_post, gsq, gw_ppt = _tail_a(x2, tgt, p2, o, ya, z, ln_post, wall, wppt, tm)

    cidx = jnp.reshape(cc, (1,)).astype(jnp.int32)
    scidx = jnp.stack([shard, cc]).astype(jnp.int32)

    def landing(pack, n, dtype):
        return jax.ShapeDtypeStruct((n, pack.shape[1] // 2, D), dtype)

    dz, do, dc, gvec, gsq = _tail_b(dm, ya, yb, o, z, pw, y, rstd, conv_ln_g, conv_ln_b, wall,
                                    gw_ppt.reshape(PLE, D), gsq, tm)
    dz, g_wdw, r1_sq = _conv_bwd(dc, z, wdw, dz, S, tm, _exchange_copies, gsq, landing(gsq, N_SHARDS, F32))
    cs_sq = _chip_sum(cidx, gsq, r1_sq, "chip_sum_sq")
    dz, dkv, g_sinks, r2_sq = _attn_bwd(z, zkv, o, do, cos_t, sin_t, sinks1, dz, S, tq, _chip_sum_copies, cs_sq,
                                        landing(gsq, 3, BF16))
    gwt_pack = _gwt_kv(dkv, h, _gwt(dz, h, min(2 * TILE_PROJ, T)), tm_big)

    tm_dh = tm_res
    n_dh = T // tm_dh
    n_a = max(1, n_dh // 4)
    gx, g_ln_pre_a, r1_wt = _dh(
        dz, dkv, wt, x2, dx1, ln_pre, tm_dh, 0, n_a, None, "dh_exchange", _exchange_copies, gwt_pack,
        landing(gwt_pack, N_SHARDS, F32))
    cs_wt = _chip_sum(cidx, gwt_pack, r1_wt, "chip_sum_wt")
    gx, g_ln_pre_b, r2_wt = _dh(
        dz, dkv, wt, x2, dx1, ln_pre, tm_dh, n_a, n_dh - n_a, gx, "dh_send", _chip_sum_copies, cs_wt,
        landing(gwt_pack, 3, BF16))
    g_ln_pre = g_ln_pre_a + g_ln_pre_b
    row37 = jnp.concatenate([g_sinks[0:1, 0:N_HEADS], loss_p, jnp.zeros((1, D - N_HEADS - 1), F32)], axis=1)
    vec = jnp.concatenate([g_wdw, g_ln_pre, g_ln_post, gvec[2:3], gvec[0:1], gvec[1:2], row37,
                           jnp.zeros((VEC_ROWS - 38, D), F32)], axis=0)
    gfin_wt, gfin_sq, tot = _finish_reduce(_final_half(scidx, gwt_pack, r1_wt, r2_wt, "final_half_wt"),
                                           _final_half(scidx, gsq, r1_sq, r2_sq, "final_half_sq"), vec)

    g_w_in, d_w_in, nm_w_in, nv_w_in = [a.T for a in _adamw_rows(
        gfin_wt, w_in[0].T, m_w_in[0].T, v_w_in[0].T, WIN_SHARD // 8, "adamw_w_in")]
    g_w_in = g_w_in[None]
    sq_m = (m_w_pw, m_w_br_conv, m_w_br_attn, m_w_out, m_w_ple_gate)
    sq_v = (v_w_pw, v_w_br_conv, v_w_br_attn, v_w_out, v_w_ple_gate)
    sq_res = _adamw_square(gfin_sq, [w[0] for w in sq_w], [m[0] for m in sq_m], [v[0] for v in sq_v])
    g_wpp = gfin_sq[5 * SQ_SHARD:SQ_PACK].reshape(PLE, PLE).T
    g_dw_all = tot[0:CONV_K]
    g_dw = lax.dynamic_slice_in_dim(g_dw_all, shard * PLE, PLE, axis=1)
    small_g = [g_wpp, g_dw, tot[32:33], tot[33:34], tot[34:35], tot[35:36], tot[36:37],
               tot[37:38, 0:N_HEADS]]
    small_w = [w_ple_proj[0], w_dw[0], ln_pre, ln_post, b_dw, conv_ln_g, conv_ln_b, sinks]
    small_m = [m_w_ple_proj[0], m_w_dw[0], m_ln_pre, m_ln_post, m_b_dw, m_conv_ln_g, m_conv_ln_b, m_sinks]
    small_v = [v_w_ple_proj[0], v_w_dw[0], v_ln_pre, v_ln_post, v_b_dw, v_conv_ln_g, v_conv_ln_b, v_sinks]
    small = _adamw_small(small_g, small_w, small_m, small_v)

    loss = tot[37, N_HEADS]
    grads = [g_w_in, small_g[2], small_g[3], g_dw[None], small_g[4], small_g[5], small_g[6],
             sq_res[0][0][None], small_g[7], sq_res[1][0][None], sq_res[2][0][None], sq_res[3][0][None],
             sq_res[4][0][None], g_wpp[None]]

    def triple(i):
        w_in_t = (d_w_in[None], nm_w_in[None], nv_w_in[None])
        sq = lambda k: tuple(a[None] for a in sq_res[k][1:4])
        sm = lambda k, lead: tuple(a[None] if lead else a for a in small[k])
        return [w_in_t[i], sm(2, False)[i], sm(3, False)[i], sm(1, True)[i], sm(4, False)[i],
                sm(5, False)[i], sm(6, False)[i], sq(0)[i], sm(7, False)[i], sq(1)[i], sq(2)[i], sq(3)[i],
                sq(4)[i], sm(0, True)[i]]

    return (loss, gx.reshape(nb, S, D), *grads, *triple(0), *triple(1), *triple(2))
```

```python
import functools

import jax
import jax.numpy as jnp
import numpy as np
from jax import lax
from jax.experimental import pallas as pl
from jax.experimental.pallas import tpu as pltpu

F32 = jnp.float32
BF16 = jnp.bfloat16

D = 1024
PLE = 256
N_HEADS = 16
HEAD_DIM = 64
BLOCK = 128
CONV_K = 31
ROPE_DIM = 16
ROPE_THETA = 500000.0
EPS = 1e-6
IN_WIDTH = 7424
N_SHARDS = 4

ADAM_LR = 0.001
ADAM_B1 = 0.9
ADAM_B2 = 0.999
ADAM_EPS = 1e-08
ADAM_WD = 0.01
ADAM_STEP = 10

SQ_NAMES = ("w_pw", "w_br_conv", "w_br_attn", "w_out", "w_ple_gate")
WT0 = 5 * D
WPP0 = WT0 + IN_WIDTH
WALL_ROWS = WPP0 + PLE
WIN_SHARD = IN_WIDTH // N_SHARDS
SQ_SHARD = D // N_SHARDS
WPP_SHARD = PLE * PLE // D
PACK_ROWS = WIN_SHARD + 5 * SQ_SHARD + WPP_SHARD
HALF_ROWS = PACK_ROWS // 2
VMEM_LIMIT = 56 * 1024 * 1024
MESH = pl.DeviceIdType.MESH
TILE_RESIDENT = 512
TILE_PROJ = 1024
TILE_TOKEN = 256
TILE_ATTN = 512
TAIL_PARTS = 1


ZB_AGATE, ZB_GCONV, ZB_GATTN, ZB_CGATE, ZB_CVAL, ZB_CGLU, ZB_Q = range(7)
ZKV = 7 * D
_SEGMENTS = ((0, D, ZB_CVAL * D), (D, D, ZB_CGLU * D), (2 * D, D, ZB_CGATE * D), (3 * D, D, ZB_Q * D),
             (4 * D, 2 * BLOCK, ZKV), (4 * D + 2 * BLOCK, D, ZB_AGATE * D),
             (5 * D + 2 * BLOCK, D, ZB_GCONV * D), (6 * D + 2 * BLOCK, D, ZB_GATTN * D))
_WT_CUTS = (0, 192, 640, 1216, WIN_SHARD)


def _zp_row(o):
    for a, w, zp in _SEGMENTS:
        if a <= o < a + w:
            return zp + o - a
    raise ValueError(o)


def _pieces(s):
    out = []
    for a, b in zip(_WT_CUTS[:-1], _WT_CUTS[1:]):
        first = _zp_row(WIN_SHARD * s + a)
        assert _zp_row(WIN_SHARD * s + b - 1) == first + b - a - 1
        out.append((a, b - a, WT0 + first))
    for k in range(5):
        out.append((WIN_SHARD + SQ_SHARD * k, SQ_SHARD, D * k + SQ_SHARD * s))
    out.append((WIN_SHARD + 5 * SQ_SHARD, WPP_SHARD, WPP0 + WPP_SHARD * s))
    return out


N_PIECES = len(_pieces(0))


def _wall_segments(wall0, rows):
    out = []
    for s in range(N_SHARDS):
        for pr, n, wr in _pieces(s):
            lo, hi = max(wr, wall0), min(wr + n, wall0 + rows)
            if lo < hi:
                out.append((lo - wall0, hi - lo, s, pr + lo - wr))
    assert sum(n for _, n, _, _ in out) == rows
    return out


def _sel(s, vals):
    r = jnp.int32(vals[0])
    for i in range(1, len(vals)):
        r = jnp.where(s == i, jnp.int32(vals[i]), r)
    return r


def _sig(x):
    return 1.0 / (1.0 + jnp.exp(-x))


def _mm(a, b):
    return lax.dot_general(a, b, (((1,), (0,)), ((), ())), preferred_element_type=F32)


def _mm_nt(a, b):
    return lax.dot_general(a, b, (((1,), (1,)), ((), ())), preferred_element_type=F32)


def _mm_tn(a, b):
    return lax.dot_general(a, b, (((0,), (0,)), ((), ())), preferred_element_type=F32)


def _params(sem=None):
    return pltpu.CompilerParams(dimension_semantics=sem, vmem_limit_bytes=VMEM_LIMIT)


def _flush_to_pack(acc_ref, gpack_ref, wall0, sem):
    base = 0 if gpack_ref.shape[1] == WIN_SHARD else WIN_SHARD
    for r, n, s, pr in _wall_segments(wall0, acc_ref.shape[0]):
        assert 0 <= pr - base and pr - base + n <= gpack_ref.shape[1]
        cp = pltpu.make_async_copy(acc_ref.at[pl.ds(r, n)], gpack_ref.at[s, pl.ds(pr - base, n)], sem)
        cp.start()
        cp.wait()


def _coords():
    return lax.axis_index("x"), lax.axis_index("y"), lax.axis_index("c")


def _chip_peers(x, y):
    return [(1 - x, y), (x, 1 - y), (1 - x, 1 - y)]


WIN_PIECES = tuple(range(len(_WT_CUTS) - 1))
SQ_PIECES = tuple(range(len(WIN_PIECES), N_PIECES))


def _gather_ops(group, src, landing, bytes_ref, stage, send_sems, recv_sems, loc_sem):
    sizes = [_pieces(0)[p][1] for p in group]
    half_rows = sum(n // 2 for n in sizes)

    def rcopy(a, b, k, dev):
        return pltpu.make_async_remote_copy(src_ref=a, dst_ref=b, send_sem=send_sems.at[k],
                                            recv_sem=recv_sems.at[k], device_id=dev, device_id_type=MESH)

    def total(k):
        x, y, c = _coords()
        rows = bytes_ref.at[pl.ds(0, half_rows)]
        return rcopy(rows, rows, k, (x, y, c))

    def send():
        x, y, c = _coords()
        s_me = 2 * x + y
        for k, (px, py) in enumerate(_chip_peers(x, y)):
            for p, n in zip(group, sizes):
                h = n // 2
                rcopy(src(p, c * h, h), landing(p, s_me, c * h, h), k, (px, py, c)).start()
        for p, n in zip(group, sizes):
            for a, b in ((src(p, 0, n), stage.at[pl.ds(0, n)]), (stage.at[pl.ds(0, n)], landing(p, s_me, 0, n))):
                cp = pltpu.make_async_copy(a, b, loc_sem)
                cp.start()
                cp.wait()

    def forward():
        x, y, c = _coords()
        for k, (px, py) in enumerate(_chip_peers(x, y)):
            total(k).wait_recv()
            for p, n in zip(group, sizes):
                rows = landing(p, 2 * px + py, c * (n // 2), n // 2)
                rcopy(rows, rows, 3 + k, (x, y, 1 - c)).start()

    def finish():
        for k in range(3):
            total(3 + k).wait_recv()
        for k in range(6):
            total(k).wait_send()

    return send, forward, finish


def _piece_rows(ref, start, off, n):
    first = start + off
    return ref.at[pl.ds(first if isinstance(first, int) else pl.multiple_of(first, 32), n)]


def _gather_win(win_t, wdw_shard, x, ln_pre, pos, inv_sign, tm):
    tables = [[_pieces(s)[p][2] - WT0 for s in range(N_SHARDS)] for p in WIN_PIECES]
    T = x.shape[0]
    n_i = T // tm

    def wdw_copies(wdw_ref, wdwall_ref, send_sems, recv_sems):
        x_, y_, c_ = _coords()
        return [pltpu.make_async_remote_copy(
            src_ref=wdw_ref, dst_ref=wdwall_ref.at[2 * x_ + y_], send_sem=send_sems.at[6 + k],
            recv_sem=recv_sems.at[6 + k], device_id=(px, py, c_), device_id_type=MESH)
            for k, (px, py) in enumerate(_chip_peers(x_, y_))]

    def body(win_ref, wdw_ref, x_ref, g_ref, pos_ref, is_ref, wt_ref, wdwall_ref, h_ref, cos_ref, sin_ref,
             stage, send_sems, recv_sems, loc_sems):
        i = pl.program_id(0)
        send, forward, finish = _gather_ops(
            WIN_PIECES, lambda p, off, n: _piece_rows(win_ref, _WT_CUTS[p], off, n),
            lambda p, s, off, n: _piece_rows(wt_ref, _sel(s, tables[p]), off, n),
            wt_ref, stage, send_sems, recv_sems, loc_sems.at[0])

        def own_wdw():
            x_, y_, _ = _coords()
            return pltpu.make_async_copy(wdw_ref, wdwall_ref.at[2 * x_ + y_], loc_sems.at[1])

        @pl.when(i == 0)
        def _():
            own_wdw().start()
            for cp in wdw_copies(wdw_ref, wdwall_ref, send_sems, recv_sems):
                cp.start()
            send()

        xv = x_ref[...]
        r = lax.rsqrt(jnp.mean(xv * xv, axis=-1, keepdims=True) + EPS)
        h_ref[...] = (xv * r * g_ref[...]).astype(BF16)
        ang = pos_ref[...] * is_ref[0:1, :]
        cos_ref[...] = jnp.cos(ang)
        sin_ref[...] = jnp.sin(ang) * is_ref[1:2, :]

        @pl.when(i == n_i - 1)
        def _():
            forward()
            finish()
            cps = wdw_copies(wdw_ref, wdwall_ref, send_sems, recv_sems)
            for cp in cps:
                cp.wait_recv()
            for cp in cps:
                cp.wait_send()
            own_wdw().wait()

    any_spec = pl.BlockSpec(memory_space=pl.ANY)
    return pl.pallas_call(
        body, name="gather_win", grid=(n_i,),
        out_shape=(jax.ShapeDtypeStruct((IN_WIDTH, D), BF16), jax.ShapeDtypeStruct((N_SHARDS, 32, PLE), F32),
                   jax.ShapeDtypeStruct((T, D), BF16), jax.ShapeDtypeStruct((T, BLOCK), F32),
                   jax.ShapeDtypeStruct((T, BLOCK), F32)),
        in_specs=[any_spec, any_spec, pl.BlockSpec((tm, D), lambda i: (i, 0)),
                  pl.BlockSpec((1, D), lambda i: (0, 0)), pl.BlockSpec((tm, 1), lambda i: (i, 0)),
                  pl.BlockSpec((8, BLOCK), lambda i: (0, 0))],
        out_specs=(any_spec, any_spec, pl.BlockSpec((tm, D), lambda i: (i, 0)),
                   pl.BlockSpec((tm, BLOCK), lambda i: (i, 0)), pl.BlockSpec((tm, BLOCK), lambda i: (i, 0))),
        scratch_shapes=[pltpu.VMEM((max(_pieces(0)[p][1] for p in WIN_PIECES), D), BF16),
                        pltpu.SemaphoreType.DMA((9,)), pltpu.SemaphoreType.DMA((9,)),
                        pltpu.SemaphoreType.DMA((2,))],
        compiler_params=_params(("arbitrary",)),
    )(win_t, wdw_shard, x, ln_pre, pos, inv_sign)


SQ_PACK = PACK_ROWS - WIN_SHARD


def _row_tile(half):
    return max(t for t in range(8, 321, 8) if half % t == 0)


def _exchange_copies(g_ref, r1_ref, send_sems, recv_sems):
    x, y, c = _coords()
    half = g_ref.shape[1] // 2
    return [pltpu.make_async_remote_copy(
        src_ref=g_ref.at[:, pl.ds(pl.multiple_of((1 - c) * half, 32), half), :], dst_ref=r1_ref,
        send_sem=send_sems.at[0], recv_sem=recv_sems.at[0], device_id=(x, y, 1 - c), device_id_type=MESH)]


def _chip_sum_copies(cs_ref, r2_ref, send_sems, recv_sems):
    x, y, c = _coords()
    return [pltpu.make_async_remote_copy(
        src_ref=cs_ref.at[2 * px + py], dst_ref=r2_ref.at[k], send_sem=send_sems.at[k],
        recv_sem=recv_sems.at[k], device_id=(px, py, c), device_id_type=MESH)
        for k, (px, py) in enumerate(_chip_peers(x, y))]


def _chip_sum(cidx, gpack, r1, name):
    half = gpack.shape[1] // 2
    rt = _row_tile(half)

    def body(c_ref, g_ref, r_ref, o_ref):
        o_ref[...] = (g_ref[...] + r_ref[...]).astype(BF16)

    nt = half // rt
    return pl.pallas_call(
        body, name=name,
        grid_spec=pltpu.PrefetchScalarGridSpec(
            num_scalar_prefetch=1, grid=(N_SHARDS, nt),
            in_specs=[pl.BlockSpec((1, rt, D), lambda s, t, c: (s, c[0] * nt + t, 0)),
                      pl.BlockSpec((1, rt, D), lambda s, t, c: (s, t, 0))],
            out_specs=pl.BlockSpec((1, rt, D), lambda s, t, c: (s, t, 0))),
        out_shape=jax.ShapeDtypeStruct((N_SHARDS, half, D), BF16),
        compiler_params=_params(("arbitrary", "arbitrary")),
    )(cidx, gpack, r1)


def _final_half(sc, gpack, r1, r2, name):
    rows = gpack.shape[1]
    half = rows // 2
    rt = _row_tile(half)

    def body(sc_ref, g_ref, r_ref, p_ref, o_ref):
        acc = g_ref[0] + r_ref[0]
        for k in range(3):
            acc = acc + p_ref[k].astype(F32)
        o_ref[...] = acc

    nt = half // rt
    return pl.pallas_call(
        body, name=name,
        grid_spec=pltpu.PrefetchScalarGridSpec(
            num_scalar_prefetch=1, grid=(nt,),
            in_specs=[pl.BlockSpec((1, rt, D), lambda t, sc: (sc[0], sc[1] * nt + t, 0)),
                      pl.BlockSpec((1, rt, D), lambda t, sc: (sc[0], t, 0)),
                      pl.BlockSpec((3, rt, D), lambda t, sc: (0, t, 0))],
            out_specs=pl.BlockSpec((rt, D), lambda t, sc: (sc[1] * nt + t, 0))),
        out_shape=jax.ShapeDtypeStruct((rows, D), F32),
        compiler_params=_params(("arbitrary",)),
    )(sc, gpack, r1, r2)


VEC_ROWS = 40


def _finish_reduce(fwt, fsq, vec):
    def body(fwt_ref, fsq_ref, v_ref, owt_ref, osq_ref, tot_ref, buf, send_sems, recv_sems):
        x, y, c = _coords()
        swaps = []
        for k, (f_ref, o_ref) in enumerate(((fwt_ref, owt_ref), (fsq_ref, osq_ref))):
            half = f_ref.shape[0] // 2
            rows = pl.ds(pl.multiple_of(c * half, 32), half)
            swaps.append(pltpu.make_async_remote_copy(
                src_ref=f_ref.at[rows], dst_ref=o_ref.at[rows], send_sem=send_sems.at[7 + k],
                recv_sem=recv_sems.at[7 + k], device_id=(x, y, 1 - c), device_id_type=MESH))
        for cp in swaps:
            cp.start()
        me = 4 * x + 2 * y + c
        buf[me] = v_ref[...]
        cps = []
        for r in range(1, 8):
            dx, dy, dc = (r >> 2) & 1, (r >> 1) & 1, r & 1
            peer = (1 - x if dx else x, 1 - y if dy else y, 1 - c if dc else c)
            cp = pltpu.make_async_remote_copy(
                src_ref=v_ref, dst_ref=buf.at[me], send_sem=send_sems.at[r - 1],
                recv_sem=recv_sems.at[r - 1], device_id=peer, device_id_type=MESH)
            cp.start()
            cps.append(cp)
        for cp in cps:
            cp.wait_recv()
        for cp in cps:
            cp.wait_send()
        acc = buf[0]
        for d in range(1, 8):
            acc = acc + buf[d]
        tot_ref[...] = acc
        for cp in swaps:
            cp.wait()

    any_spec = pl.BlockSpec(memory_space=pl.ANY)
    vm = pl.BlockSpec(memory_space=pltpu.VMEM)
    return pl.pallas_call(
        body, name="finish_reduce",
        out_shape=(jax.ShapeDtypeStruct(fwt.shape, F32), jax.ShapeDtypeStruct(fsq.shape, F32),
                   jax.ShapeDtypeStruct((VEC_ROWS, D), F32)),
        in_specs=[any_spec, any_spec, vm], out_specs=(any_spec, any_spec, vm),
        input_output_aliases={0: 0, 1: 1},
        scratch_shapes=[pltpu.VMEM((8, VEC_ROWS, D), F32), pltpu.SemaphoreType.DMA((9,)),
                        pltpu.SemaphoreType.DMA((9,))],
    )(fwt, fsq, vec)


def _inproj(h, wt, sq_shards, wpp_shard, tm):
    T = h.shape[0]
    nsq = len(sq_shards)
    n_i = T // tm

    def body(*refs):
        h_ref, wt_ref = refs[:2]
        sq_refs = refs[2:2 + nsq]
        wpp_ref = refs[2 + nsq]
        z_ref, zkv_ref, wsq_ref, wppf_ref, wbuf, stage, send_sems, recv_sems, loc_sem, wsem = refs[3 + nsq:]
        i = pl.program_id(0)

        def src(p, off, n):
            k = p - SQ_PIECES[0]
            return _piece_rows(wpp_ref if k == nsq else sq_refs[k], 0, off, n)

        def landing(p, s, off, n):
            k = p - SQ_PIECES[0]
            if k == nsq:
                return _piece_rows(wppf_ref, WPP_SHARD * s, off, n)
            return _piece_rows(wsq_ref, D * k + SQ_SHARD * s, off, n)

        send, forward, finish = _gather_ops(SQ_PIECES, src, landing, wsq_ref, stage, send_sems, recv_sems,
                                            loc_sem)

        def project(before_block):
            h = h_ref[...]
            for j in range(7):
                before_block(j)
                z_ref[:, j * D:(j + 1) * D] = _mm_nt(h, wbuf[j * D:(j + 1) * D, :]).astype(BF16)
            before_block(7)
            zkv_ref[...] = _mm_nt(h, wbuf[ZKV:IN_WIDTH, :]).astype(BF16)

        @pl.when(i == 0)
        def _():
            cuts = [j * D for j in range(8)] + [IN_WIDTH]
            loads = [pltpu.make_async_copy(wt_ref.at[pl.ds(a, b - a)], wbuf.at[pl.ds(a, b - a)], wsem.at[j])
                     for j, (a, b) in enumerate(zip(cuts[:-1], cuts[1:]))]
            for load in loads:
                load.start()
            send()
            project(lambda j: loads[j].wait())

        @pl.when(i == n_i // 2)
        def _():
            forward()

        @pl.when(i > 0)
        def _():
            project(lambda j: None)

        @pl.when(i == n_i - 1)
        def _():
            finish()

    any_spec = pl.BlockSpec(memory_space=pl.ANY)
    return pl.pallas_call(
        body, name="inproj", grid=(n_i,),
        in_specs=[pl.BlockSpec((tm, D), lambda i: (i, 0))] + [any_spec] * (nsq + 2),
        out_specs=(pl.BlockSpec((tm, ZKV), lambda i: (i, 0)), pl.BlockSpec((tm, 2 * BLOCK), lambda i: (i, 0)),
                   any_spec, any_spec),
        out_shape=(jax.ShapeDtypeStruct((T, ZKV), BF16), jax.ShapeDtypeStruct((T, 2 * BLOCK), BF16),
                   jax.ShapeDtypeStruct((nsq * D, D), BF16), jax.ShapeDtypeStruct((PLE, D), BF16)),
        scratch_shapes=[pltpu.VMEM((IN_WIDTH, D), BF16), pltpu.VMEM((SQ_SHARD, D), BF16),
                        pltpu.SemaphoreType.DMA((6,)), pltpu.SemaphoreType.DMA((6,)), pltpu.SemaphoreType.DMA,
                        pltpu.SemaphoreType.DMA((8,))],
        compiler_params=_params(("arbitrary",)),
    )(h, wt, *sq_shards, wpp_shard)


HALO = 32
CONV_RC = 64
CONV_LC = 256


def _conv_taps(w_ref, src, r0, lane0, offset_of_tap):
    lanes = pl.ds(lane0, CONV_LC)
    out = None
    for b in range(8):
        taps = [k for k in range(CONV_K) if offset_of_tap(k) % 8 == b]
        if not taps:
            continue
        rows = CONV_RC + (8 if b else 0)
        vb = None
        for k in taps:
            term = w_ref[k:k + 1, lanes] * src[pl.ds(r0 + (offset_of_tap(k) - b), rows), lanes]
            vb = term if vb is None else vb + term
        vb = vb[b:b + CONV_RC] if b else vb
        out = vb if out is None else out + vb
    return out


def _conv_fwd(z, wdw, b_dw, ln_g, ln_b, wall, S, tm):
    T = z.shape[0]
    nt = S // tm
    hb = tm // HALO

    def body(cv_ref, cg_ref, cgate_ref, hcv_ref, hcg_ref, wdw_ref, bdw_ref, lng_ref, lnb_ref, wpw_ref,
             wbrc_ref, ya_ref, y_ref, rstd_ref, pw_ref, ubuf, cbuf):
        t = pl.program_id(1)
        ubuf[HALO:HALO + tm, :] = cv_ref[...].astype(F32) * _sig(cg_ref[...].astype(F32))
        hu = hcv_ref[...].astype(F32) * _sig(hcg_ref[...].astype(F32))
        ubuf[0:HALO, :] = jnp.where(t > 0, hu, 0.0)
        ubuf[HALO + tm:HALO + tm + 8, :] = jnp.zeros((8, D), F32)

        def chunk(ci, carry):
            r0 = pl.multiple_of(ci * CONV_RC, CONV_RC)
            for lg in range(D // CONV_LC):
                acc = _conv_taps(wdw_ref, ubuf, r0, lg * CONV_LC, lambda k: HALO - (CONV_K - 1) + k)
                cbuf[pl.ds(r0, CONV_RC), pl.ds(lg * CONV_LC, CONV_LC)] = acc
            return carry

        lax.fori_loop(0, tm // CONV_RC, chunk, 0)
        cc = cbuf[...] + bdw_ref[...]
        mu = jnp.mean(cc, axis=-1, keepdims=True)
        dd = cc - mu
        rstd = lax.rsqrt(jnp.mean(dd * dd, axis=-1, keepdims=True) + EPS)
        yn = dd * rstd
        y_ref[...] = yn.astype(BF16)
        rstd_ref[...] = rstd
        n = yn * lng_ref[...] + lnb_ref[...]
        s = n * _sig(n)
        pw = _mm(s.astype(BF16), wpw_ref[...])
        pw_ref[...] = pw.astype(BF16)
        gt = cgate_ref[...].astype(F32)
        ya_in = pw * (gt * _sig(gt))
        ya_ref[...] = _mm(ya_in.astype(BF16), wbrc_ref[...]).astype(BF16)

    def row(b, t):
        return b * nt + t

    def halo(b, t):
        return jnp.maximum(row(b, t) * hb - 1, 0)

    vec = pl.BlockSpec((1, D), lambda b, t: (0, 0))
    tile = lambda j: pl.BlockSpec((tm, D), lambda b, t: (row(b, t), j))
    out_tile = pl.BlockSpec((tm, D), lambda b, t: (row(b, t), 0))
    return pl.pallas_call(
        body, name="conv_fwd", grid=(T // S, nt),
        in_specs=[tile(ZB_CVAL), tile(ZB_CGLU), tile(ZB_CGATE),
                  pl.BlockSpec((HALO, D), lambda b, t: (halo(b, t), ZB_CVAL)),
                  pl.BlockSpec((HALO, D), lambda b, t: (halo(b, t), ZB_CGLU)),
                  pl.BlockSpec((32, D), lambda b, t: (0, 0)), vec, vec, vec,
                  pl.BlockSpec((D, D), lambda b, t: (0, 0)),
                  pl.BlockSpec((D, D), lambda b, t: (1, 0))],
        out_specs=(out_tile, out_tile, pl.BlockSpec((tm, 1), lambda b, t: (row(b, t), 0)), out_tile),
        out_shape=(jax.ShapeDtypeStruct((T, D), BF16), jax.ShapeDtypeStruct((T, D), BF16),
                   jax.ShapeDtypeStruct((T, 1), F32), jax.ShapeDtypeStruct((T, D), BF16)),
        scratch_shapes=[pltpu.VMEM((tm + HALO + 8, D), F32), pltpu.VMEM((tm, D), F32)],
        compiler_params=_params(("arbitrary", "arbitrary")),
    )(z, z, z, z, z, wdw, b_dw, ln_g, ln_b, wall, wall)


def _swap_matrix():
    r = lax.broadcasted_iota(jnp.int32, (BLOCK, BLOCK), 0)
    l = lax.broadcasted_iota(jnp.int32, (BLOCK, BLOCK), 1)
    lh = l & (HEAD_DIM - 1)
    half = ROPE_DIM // 2
    hit = ((lh < half) & (r == l + half)) | ((lh >= half) & (lh < ROPE_DIM) & (r == l - half))
    return jnp.where(hit, 1.0, 0.0).astype(BF16)


def _rope(tb, cos, sin, pswap):
    return tb.astype(F32) * cos + _mm(tb, pswap) * sin


def _rope_f32(tv, cos, sin, pswap):
    hi = tv.astype(BF16)
    lo = (tv - hi.astype(F32)).astype(BF16)
    return tv * cos + (_mm(hi, pswap) + _mm(lo, pswap)) * sin


def _kv_variants(kv):
    lane = lax.broadcasted_iota(jnp.int32, kv.shape, 1)
    lo = lane < HEAD_DIM
    sw = pltpu.roll(kv, HEAD_DIM, 1)
    z = jnp.zeros_like(kv)
    g0 = (jnp.where(lo, kv, z).astype(BF16), jnp.where(lo, z, sw).astype(BF16))
    g1 = (jnp.where(lo, sw, z).astype(BF16), jnp.where(lo, z, kv).astype(BF16))
    return (g0, g1)


def _band_mask(nq):
    qi = lax.broadcasted_iota(jnp.int32, (nq * BLOCK, 2 * BLOCK), 0) & (BLOCK - 1)
    sj = lax.broadcasted_iota(jnp.int32, (nq * BLOCK, 2 * BLOCK), 1)
    return (sj <= qi + BLOCK) & (sj > qi), sj


def _sink_rep(sink_ref, g, e):
    return jnp.concatenate(
        [jnp.full((BLOCK, BLOCK), sink_ref[8 * g + 2 * j + e], F32) for j in range(4)], axis=0)


def _softmax_parts(s, valid, sk):
    rows = s.shape[0]
    s = jnp.where(valid, s, -1e30)
    m = jnp.maximum(jnp.broadcast_to(jnp.max(s, axis=-1, keepdims=True), (rows, BLOCK)), sk)
    return jnp.exp(s - jnp.concatenate([m, m], axis=1)), jnp.exp(sk - m)


def _softmax_sink(s, valid, sk):
    p, ps = _softmax_parts(s, valid, sk)
    inv = 1.0 / (_mm(p.astype(BF16), jnp.ones((2 * BLOCK, BLOCK), BF16)) + ps)
    return p * jnp.concatenate([inv, inv], axis=1), ps * inv


def _attn_fwd(z, zkv, cos_t, sin_t, sinks, S, tq):
    T = z.shape[0]
    nt = S // tq
    nq = tq // BLOCK

    def body(sink_ref, q_ref, kv_ref, hkv_ref, cos_ref, sin_ref, hcos_ref, hsin_ref, o_ref):
        t = pl.program_id(1)
        cos = cos_ref[...]
        sin = sin_ref[...]
        pswap = _swap_matrix()
        kv = jnp.concatenate([hkv_ref[...], kv_ref[...]], axis=0)
        cos_k = jnp.concatenate([hcos_ref[...], cos], axis=0)
        sin_k = jnp.concatenate([hsin_ref[...], sin], axis=0)
        kx = _kv_variants(_rope(kv[:, :BLOCK], cos_k, sin_k, pswap))
        one = jnp.ones((tq + BLOCK, BLOCK), BF16)
        vx = [[jnp.concatenate([v, one], axis=1) for v in vg] for vg in _kv_variants(kv[:, BLOCK:].astype(F32))]
        band, sj = _band_mask(4)
        qs = [(_rope(q_ref[:, 128 * hp:128 * hp + 128], cos, sin, pswap) * 0.125).astype(BF16)
              for hp in range(8)]
        for n in range(nq):
            first = (t == 0) & (n == 0)
            valid = band & (jnp.logical_not(first) | (sj >= BLOCK))
            r0 = n * BLOCK
            for g in range(2):
                lhs = jnp.concatenate([qs[4 * g + j][r0:r0 + BLOCK] for j in range(4)], axis=0)
                acc = jnp.zeros((4 * BLOCK, BLOCK), F32)
                for e in range(2):
                    s = _mm_nt(lhs, kx[g][e][r0:r0 + 2 * BLOCK])
                    p, ps = _softmax_parts(s, valid, _sink_rep(sink_ref, g, e))
                    r = _mm(p.astype(BF16), vx[g][e][r0:r0 + 2 * BLOCK])
                    acc = acc + r[:, 0:BLOCK] * (1.0 / (r[:, BLOCK:2 * BLOCK] + ps))
                for j in range(4):
                    o_ref[r0:r0 + BLOCK, 128 * (4 * g + j):128 * (4 * g + j) + 128] = (
                        acc[j * BLOCK:(j + 1) * BLOCK].astype(BF16))

    def row(b, t):
        return b * nt + t

    def halo(b, t):
        return jnp.maximum(row(b, t) * nq - 1, 0)

    return pl.pallas_call(
        body, name="attn_fwd", grid=(T // S, nt),
        in_specs=[pl.BlockSpec(memory_space=pltpu.SMEM),
                  pl.BlockSpec((tq, D), lambda b, t: (row(b, t), ZB_Q)),
                  pl.BlockSpec((tq, 2 * BLOCK), lambda b, t: (row(b, t), 0)),
                  pl.BlockSpec((BLOCK, 2 * BLOCK), lambda b, t: (halo(b, t), 0)),
                  pl.BlockSpec((tq, BLOCK), lambda b, t: (row(b, t), 0)),
                  pl.BlockSpec((tq, BLOCK), lambda b, t: (row(b, t), 0)),
                  pl.BlockSpec((BLOCK, BLOCK), lambda b, t: (halo(b, t), 0)),
                  pl.BlockSpec((BLOCK, BLOCK), lambda b, t: (halo(b, t), 0))],
        out_specs=pl.BlockSpec((tq, D), lambda b, t: (row(b, t), 0)),
        out_shape=jax.ShapeDtypeStruct((T, D), BF16),
        compiler_params=_params(("arbitrary", "arbitrary")),
    )(sinks, z, zkv, zkv, cos_t, sin_t, cos_t, sin_t)


def _tail_a(x, tgt, p, o, ya, z, ln_post, wall, wppt, tm):
    T = x.shape[0]
    last = T // tm - 1

    def body(x_ref, tgt_ref, p_ref, o_ref, ya_ref, ag_ref, gc_ref, ga_ref, lnp_ref, wbra_ref, wout_ref,
             wpg_ref, wppt_ref, loss_ref, dx1_ref, dm_ref, yb_ref, glnp_ref, gpack_ref, gwpp_ref,
             acc_out, acc_pg, sem):
        i = pl.program_id(0)

        @pl.when(i == 0)
        def _():
            acc_out[...] = jnp.zeros_like(acc_out)
            acc_pg[...] = jnp.zeros_like(acc_pg)
            gwpp_ref[...] = jnp.zeros_like(gwpp_ref)
            glnp_ref[...] = jnp.zeros_like(glnp_ref)
            loss_ref[...] = jnp.zeros_like(loss_ref)

        ag = ag_ref[...].astype(F32)
        yb_in = (o_ref[...].astype(F32) * (ag * _sig(ag))).astype(BF16)
        yb = _mm(yb_in, wbra_ref[...])
        yb_ref[...] = yb.astype(BF16)
        m = (_sig(gc_ref[...].astype(F32)) * ya_ref[...].astype(F32)
             + _sig(ga_ref[...].astype(F32)) * yb).astype(BF16)
        mo = _mm(m, wout_ref[...])
        r2 = lax.rsqrt(jnp.mean(mo * mo, axis=-1, keepdims=True) + EPS)
        nrm = mo * r2
        g_post = lnp_ref[...]
        x1 = x_ref[...] + nrm * g_post
        x1b = x1.astype(BF16)
        gate = _sig(_mm(x1b, wpg_ref[...]))
        pb = p_ref[...].astype(BF16)
        pp = _mm_nt(pb, wppt_ref[...])
        err = x1 + gate * pp - tgt_ref[...]
        loss_ref[...] += 0.5 * jnp.sum(jnp.sum(err * err, axis=-1, keepdims=True) * (1.0 / D),
                                       axis=0, keepdims=True)
        dx2 = err * (1.0 / D)
        dgp = (dx2 * pp * gate * (1.0 - gate)).astype(BF16)
        dpp = (dx2 * gate).astype(BF16)
        dx1 = dx2 + _mm_nt(dgp, wpg_ref[...])
        dx1_ref[...] = dx1
        acc_pg[...] += _mm_tn(x1b, dgp)
        gwpp_ref[...] += _mm_tn(dpp, pb)
        glnp_ref[...] += jnp.sum(dx1 * nrm, axis=0, keepdims=True)
        a = dx1 * g_post
        dmo = (r2 * (a - nrm * jnp.mean(a * nrm, axis=-1, keepdims=True))).astype(BF16)
        dm_ref[...] = _mm_nt(dmo, wout_ref[...]).astype(BF16)
        acc_out[...] += _mm_tn(m, dmo)

        @pl.when(i == last)
        def _():
            _flush_to_pack(acc_out, gpack_ref, 3 * D, sem.at[0])
            _flush_to_pack(acc_pg, gpack_ref, 4 * D, sem.at[1])

    tile = pl.BlockSpec((tm, D), lambda i: (i, 0))
    ztile = lambda j: pl.BlockSpec((tm, D), lambda i: (i, j))
    wsq = lambda k: pl.BlockSpec((D, D), lambda i: (k, 0))
    const = lambda shp: pl.BlockSpec(shp, lambda i: (0, 0))
    any_spec = pl.BlockSpec(memory_space=pl.ANY)
    return pl.pallas_call(
        body, name="tail_a", grid=(T // tm,),
        in_specs=[tile, tile, pl.BlockSpec((tm, PLE), lambda i: (i, 0)), tile, tile, ztile(ZB_AGATE),
                  ztile(ZB_GCONV), ztile(ZB_GATTN), const((1, D)), wsq(2), wsq(3), wsq(4), const((D, PLE))],
        out_specs=(const((1, 1)), tile, tile, tile, const((1, D)), any_spec, const((D, PLE))),
        out_shape=(jax.ShapeDtypeStruct((1, 1), F32), jax.ShapeDtypeStruct((T, D), F32),
                   jax.ShapeDtypeStruct((T, D), BF16), jax.ShapeDtypeStruct((T, D), BF16),
                   jax.ShapeDtypeStruct((1, D), F32), jax.ShapeDtypeStruct((N_SHARDS, SQ_PACK, D), F32),
                   jax.ShapeDtypeStruct((D, PLE), F32)),
        scratch_shapes=[pltpu.VMEM((D, D), F32), pltpu.VMEM((D, D), F32), pltpu.SemaphoreType.DMA((2,))],
        compiler_params=_params(("arbitrary",)),
    )(x, tgt, p, o, ya, z, z, z, ln_post, wall, wall, wall, wppt)


def _dsilu(v, sg):
    return sg * (1.0 + v * (1.0 - sg))


def _tail_b(dm, ya, yb, o, z, pw, y, rstd, ln_g, ln_b, wall, gppt, gpack, tm):
    T = dm.shape[0]
    last = T // tm - 1

    def body(dm_ref, ya_ref, yb_ref, o_ref, ag_ref, gc_ref, ga_ref, cgate_ref, pw_ref, y_ref, rstd_ref,
             lng_ref, lnb_ref, wpw_ref, wbrc_ref, wbra_ref, gppt_ref, gpack_in, dg_ref, do_ref, dc_ref,
             gvec_ref, gpack_ref, acc_bra, acc_brc, acc_pw, sem):
        i = pl.program_id(0)

        @pl.when(i == 0)
        def _():
            acc_bra[...] = jnp.zeros_like(acc_bra)
            acc_brc[...] = jnp.zeros_like(acc_brc)
            acc_pw[...] = jnp.zeros_like(acc_pw)
            gvec_ref[...] = jnp.zeros_like(gvec_ref)

        g = lng_ref[...]

        def part(rs):
            dm_v = dm_ref[rs, :].astype(F32)
            sgc = _sig(gc_ref[rs, :].astype(F32))
            sga = _sig(ga_ref[rs, :].astype(F32))
            dya = (dm_v * sgc).astype(BF16)
            dyb = (dm_v * sga).astype(BF16)
            dg_ref[rs, D:2 * D] = (dm_v * ya_ref[rs, :].astype(F32) * sgc * (1.0 - sgc)).astype(BF16)
            dg_ref[rs, 2 * D:3 * D] = (dm_v * yb_ref[rs, :].astype(F32) * sga * (1.0 - sga)).astype(BF16)
            ag = ag_ref[rs, :].astype(F32)
            sag = _sig(ag)
            sa = ag * sag
            ov = o_ref[rs, :].astype(F32)
            dyb_in = _mm_nt(dyb, wbra_ref[...])
            do_ref[rs, :] = (dyb_in * sa).astype(BF16)
            dg_ref[rs, 0:D] = (dyb_in * ov * _dsilu(ag, sag)).astype(BF16)
            gt = cgate_ref[rs, :].astype(F32)
            sgt = _sig(gt)
            sgate = gt * sgt
            pw = pw_ref[rs, :].astype(F32)
            dya_in = _mm_nt(dya, wbrc_ref[...])
            dpw = (dya_in * sgate).astype(BF16)
            dg_ref[rs, 3 * D:4 * D] = (dya_in * pw * _dsilu(gt, sgt)).astype(BF16)
            yn = y_ref[rs, :].astype(F32)
            n = yn * g + lnb_ref[...]
            sn = _sig(n)
            dn = _mm_nt(dpw, wpw_ref[...]) * _dsilu(n, sn)
            dy = dn * g
            dc = rstd_ref[rs, :] * (dy - jnp.mean(dy, axis=-1, keepdims=True)
                                    - yn * jnp.mean(dy * yn, axis=-1, keepdims=True))
            dc_ref[rs, :] = dc.astype(BF16)
            sums = (jnp.sum(dn * yn, axis=0, keepdims=True), jnp.sum(dn, axis=0, keepdims=True),
                    jnp.sum(dc, axis=0, keepdims=True))
            return ((ov * sa).astype(BF16), dyb, (pw * sgate).astype(BF16), dya, (n * sn).astype(BF16), dpw,
                    sums)

        parts = [part(pl.ds(r * (tm // TAIL_PARTS), tm // TAIL_PARTS)) for r in range(TAIL_PARTS)]
        cat = lambda j: jnp.concatenate([pt[j] for pt in parts], axis=0)
        acc_bra[...] += _mm_tn(cat(0), cat(1))
        acc_brc[...] += _mm_tn(cat(2), cat(3))
        acc_pw[...] += _mm_tn(cat(4), cat(5))
        for j in range(3):
            gvec_ref[j:j + 1, :] += sum(pt[6][j] for pt in parts)

        @pl.when(i == last)
        def _():
            _flush_to_pack(acc_pw, gpack_ref, 0, sem.at[0])
            _flush_to_pack(acc_brc, gpack_ref, D, sem.at[1])
            _flush_to_pack(acc_bra, gpack_ref, 2 * D, sem.at[2])
            _flush_to_pack(gppt_ref, gpack_ref, WPP0, sem.at[0])

    tile = pl.BlockSpec((tm, D), lambda i: (i, 0))
    ztile = lambda j: pl.BlockSpec((tm, D), lambda i: (i, j))
    wsq = lambda k: pl.BlockSpec((D, D), lambda i: (k, 0))
    const = lambda shp: pl.BlockSpec(shp, lambda i: (0, 0))
    any_spec = pl.BlockSpec(memory_space=pl.ANY)
    return pl.pallas_call(
        body, name="tail_b", grid=(T // tm,),
        in_specs=[tile, tile, tile, tile, ztile(ZB_AGATE), ztile(ZB_GCONV), ztile(ZB_GATTN), ztile(ZB_CGATE),
                  tile, tile, pl.BlockSpec((tm, 1), lambda i: (i, 0)), const((1, D)), const((1, D)), wsq(0),
                  wsq(1), wsq(2), const((PLE, D)), any_spec],
        out_specs=(pl.BlockSpec((tm, 4 * D), lambda i: (i, 0)), tile, tile, const((8, D)), any_spec),
        out_shape=(jax.ShapeDtypeStruct((T, 7 * D), BF16), jax.ShapeDtypeStruct((T, D), BF16),
                   jax.ShapeDtypeStruct((T, D), BF16), jax.ShapeDtypeStruct((8, D), F32),
                   jax.ShapeDtypeStruct(gpack.shape, F32)),
        input_output_aliases={17: 4},
        scratch_shapes=[pltpu.VMEM((D, D), F32), pltpu.VMEM((D, D), F32), pltpu.VMEM((D, D), F32),
                        pltpu.SemaphoreType.DMA((3,))],
        compiler_params=_params(("arbitrary",)),
    )(dm, ya, yb, o, z, z, z, z, pw, y, rstd, ln_g, ln_b, wall, wall, wall, gppt, gpack)


def _conv_bwd(dc, z, wdw, dz, S, tm, copies, src, landing):
    T = dc.shape[0]
    nt = S // tm
    hb = tm // HALO
    nrows = T // HALO

    def body(dc_ref, hdc_ref, cv_ref, cg_ref, hcv_ref, hcg_ref, wdw_ref, dz_in, src_ref, dz_ref, gw_ref,
             land_ref, ubuf, dcbuf, dubuf, dwacc, shbuf, send_sems, recv_sems):
        b = pl.program_id(0)
        t = pl.program_id(1)

        @pl.when((b == 0) & (t == 0))
        def _():
            dwacc[...] = jnp.zeros_like(dwacc)
            for cp in copies(src_ref, land_ref, send_sems, recv_sems):
                cp.start()

        cv = cv_ref[...].astype(F32)
        sg = _sig(cg_ref[...].astype(F32))
        ubuf[HALO:HALO + tm, :] = cv * sg
        hu = hcv_ref[...].astype(F32) * _sig(hcg_ref[...].astype(F32))
        ubuf[0:HALO, :] = jnp.where(t > 0, hu, 0.0)
        ubuf[HALO + tm:HALO + tm + 8, :] = jnp.zeros((8, D), F32)
        dcbuf[0:tm, :] = dc_ref[...].astype(F32)
        dcbuf[tm:tm + HALO, :] = jnp.where(t < nt - 1, hdc_ref[...].astype(F32), 0.0)
        dcbuf[tm + HALO:tm + HALO + 8, :] = jnp.zeros((8, D), F32)

        def chunk(ci, carry):
            r0 = pl.multiple_of(ci * CONV_RC, CONV_RC)
            for lg in range(D // CONV_LC):
                l0 = lg * CONV_LC
                dubuf[pl.ds(r0, CONV_RC), pl.ds(l0, CONV_LC)] = _conv_taps(
                    wdw_ref, dcbuf, r0, l0, lambda k: CONV_K - 1 - k)
                dcc = dcbuf[pl.ds(r0, CONV_RC), pl.ds(l0, CONV_LC)]
                zero8 = jnp.zeros((8, CONV_LC), F32)
                dcz = jnp.concatenate([zero8, dcc, zero8], axis=0)
                for bb in range(8):
                    taps = [k for k in range(CONV_K) if (HALO - (CONV_K - 1) + k) % 8 == bb]
                    if not taps:
                        continue
                    rows = CONV_RC + (8 if bb else 0)
                    if bb:
                        shbuf[bb] = dcz[8 - bb:8 - bb + rows]
                    for k in taps:
                        a8 = HALO - (CONV_K - 1) + k - bb
                        dcs = shbuf[bb] if bb else dcc
                        prod = dcs * ubuf[pl.ds(r0 + a8, rows), pl.ds(l0, CONV_LC)]
                        part = prod[0:8]
                        for q in range(1, rows // 8):
                            part = part + prod[8 * q:8 * q + 8]
                        dwacc[8 * k:8 * k + 8, pl.ds(l0, CONV_LC)] += part
            return carry

        lax.fori_loop(0, tm // CONV_RC, chunk, 0)
        du = dubuf[...]
        dz_ref[:, 0:D] = (du * sg).astype(BF16)
        dz_ref[:, D:2 * D] = (du * cv * sg * (1.0 - sg)).astype(BF16)

        @pl.when((b == pl.num_programs(0) - 1) & (t == nt - 1))
        def _():
            for k in range(32):
                gw_ref[k:k + 1, :] = jnp.sum(dwacc[8 * k:8 * k + 8, :], axis=0, keepdims=True)
            cps = copies(src_ref, land_ref, send_sems, recv_sems)
            for cp in cps:
                cp.wait_recv()
            for cp in cps:
                cp.wait_send()

    def row(b, t):
        return b * nt + t

    def prev_halo(b, t):
        return jnp.maximum(row(b, t) * hb - 1, 0)

    def next_halo(b, t):
        return jnp.minimum((row(b, t) + 1) * hb, nrows - 1)

    return pl.pallas_call(
        body, name="conv_bwd", grid=(T // S, nt),
        in_specs=[pl.BlockSpec((tm, D), lambda b, t: (row(b, t), 0)),
                  pl.BlockSpec((HALO, D), lambda b, t: (next_halo(b, t), 0)),
                  pl.BlockSpec((tm, D), lambda b, t: (row(b, t), ZB_CVAL)),
                  pl.BlockSpec((tm, D), lambda b, t: (row(b, t), ZB_CGLU)),
                  pl.BlockSpec((HALO, D), lambda b, t: (prev_halo(b, t), ZB_CVAL)),
                  pl.BlockSpec((HALO, D), lambda b, t: (prev_halo(b, t), ZB_CGLU)),
                  pl.BlockSpec((32, D), lambda b, t: (0, 0)),
                  pl.BlockSpec(memory_space=pl.ANY), pl.BlockSpec(memory_space=pl.ANY)],
        out_specs=(pl.BlockSpec((tm, 2 * D), lambda b, t: (row(b, t), ZB_CVAL // 2)),
                   pl.BlockSpec((32, D), lambda b, t: (0, 0)), pl.BlockSpec(memory_space=pl.ANY)),
        out_shape=(jax.ShapeDtypeStruct(dz.shape, BF16), jax.ShapeDtypeStruct((32, D), F32), landing),
        input_output_aliases={7: 0},
        scratch_shapes=[pltpu.VMEM((tm + HALO + 8, D), F32), pltpu.VMEM((tm + HALO + 8, D), F32),
                        pltpu.VMEM((tm, D), F32), pltpu.VMEM((8 * 32, D), F32),
                        pltpu.VMEM((8, CONV_RC + 8, CONV_LC), F32), pltpu.SemaphoreType.DMA((3,)),
                        pltpu.SemaphoreType.DMA((3,))],
        compiler_params=_params(("arbitrary", "arbitrary")),
    )(dc, dc, z, z, z, z, wdw, dz, src)


def _attn_bwd(z, zkv, o, do, cos_t, sin_t, sinks, dz, S, tq, copies, src, landing):
    T = z.shape[0]
    nt = S // tq
    nq = tq // BLOCK

    def body(sink_ref, q_ref, kv_ref, hkv_ref, o_ref, do_ref, cos_ref, sin_ref, hcos_ref, hsin_ref, dz_in,
             src_ref, dq_ref, dkv_ref, gs_ref, land_ref, carry, dkacc, dvacc, send_sems, recv_sems):
        b = pl.program_id(0)
        tt = pl.program_id(1)
        t = nt - 1 - tt

        @pl.when((b == 0) & (tt == 0))
        def _():
            gs_ref[...] = jnp.zeros_like(gs_ref)
            for cp in copies(src_ref, land_ref, send_sems, recv_sems):
                cp.start()

        @pl.when(tt == 0)
        def _():
            carry[...] = jnp.zeros_like(carry)

        cos = cos_ref[...]
        sin = sin_ref[...]
        pswap = _swap_matrix()
        kv = jnp.concatenate([hkv_ref[...], kv_ref[...]], axis=0)
        cos_k = jnp.concatenate([hcos_ref[...], cos], axis=0)
        sin_k = jnp.concatenate([hsin_ref[...], sin], axis=0)
        kx = _kv_variants(_rope(kv[:, :BLOCK], cos_k, sin_k, pswap))
        vx = _kv_variants(kv[:, BLOCK:].astype(F32))
        band, sj = _band_mask(4)
        lo = lax.broadcasted_iota(jnp.int32, (4 * BLOCK, BLOCK), 1) < HEAD_DIM
        ones = jnp.ones((2 * BLOCK, 2 * BLOCK), BF16)
        qs = [(_rope(q_ref[:, 128 * hp:128 * hp + 128], cos, sin, pswap) * 0.125).astype(BF16)
              for hp in range(8)]
        dkacc[...] = jnp.zeros_like(dkacc)
        dvacc[...] = jnp.zeros_like(dvacc)
        gsum = jnp.zeros((1, BLOCK), F32)
        hlane = lax.broadcasted_iota(jnp.int32, (1, BLOCK), 1)
        for n in range(nq):
            first = (t == 0) & (n == 0)
            valid = band & (jnp.logical_not(first) | (sj >= BLOCK))
            r0 = n * BLOCK
            for g in range(2):
                cols = [slice(128 * (4 * g + j), 128 * (4 * g + j) + 128) for j in range(4)]
                lhs = jnp.concatenate([qs[4 * g + j][r0:r0 + BLOCK] for j in range(4)], axis=0)
                dov = jnp.concatenate([do_ref[r0:r0 + BLOCK, cs] for cs in cols], axis=0)
                prod = dov.astype(F32) * jnp.concatenate(
                    [o_ref[r0:r0 + BLOCK, cs] for cs in cols], axis=0).astype(F32)
                lhs_t = lhs.T
                dov_t = dov.T
                dq = jnp.zeros((4 * BLOCK, BLOCK), F32)
                dk_t = jnp.zeros((HEAD_DIM, 2 * BLOCK), F32)
                dv_t = jnp.zeros((HEAD_DIM, 2 * BLOCK), F32)
                for e in range(2):
                    kw = kx[g][e][r0:r0 + 2 * BLOCK]
                    vw = vx[g][e][r0:r0 + 2 * BLOCK]
                    s = _mm_nt(lhs, kw)
                    p, psink = _softmax_sink(s, valid, _sink_rep(sink_ref, g, e))
                    pe = jnp.where(lo if e == 0 else jnp.logical_not(lo), prod, 0.0)
                    pe_hi = pe.astype(BF16)
                    pe_lo = (pe - pe_hi.astype(F32)).astype(BF16)
                    delta = _mm(jnp.concatenate([pe_hi, pe_lo], axis=1), ones)
                    ds = (p * (_mm_nt(dov, vw) - delta)).astype(BF16)
                    dq = dq + _mm(ds, kw)
                    dims = slice(HEAD_DIM * e, HEAD_DIM * (e + 1))
                    dk_t = dk_t + _mm(lhs_t[dims], ds)
                    dv_t = dv_t + _mm(dov_t[dims], p.astype(BF16))
                    gs = -psink * delta[:, 0:BLOCK]
                    for j in range(4):
                        tot = jnp.sum(gs[j * BLOCK:(j + 1) * BLOCK], axis=0, keepdims=True)
                        gsum = gsum + jnp.where(hlane == 8 * g + 2 * j + e, tot, 0.0)
                dkacc[HEAD_DIM * g:HEAD_DIM * (g + 1), r0:r0 + 2 * BLOCK] += dk_t
                dvacc[HEAD_DIM * g:HEAD_DIM * (g + 1), r0:r0 + 2 * BLOCK] += dv_t
                for j in range(4):
                    dqj = _rope_f32(dq[j * BLOCK:(j + 1) * BLOCK] * 0.125, cos[r0:r0 + BLOCK],
                                    -sin[r0:r0 + BLOCK], pswap)
                    dq_ref[r0:r0 + BLOCK, cols[j]] = dqj.astype(BF16)
        gs_ref[0:1, :] += gsum
        dk_all = dkacc[...]
        dv_all = dvacc[...]
        dk_last = dk_all[:, tq:tq + BLOCK] + carry[0:BLOCK, :]
        dv_last = dv_all[:, tq:tq + BLOCK] + carry[BLOCK:2 * BLOCK, :]
        carry[0:BLOCK, :] = dk_all[:, 0:BLOCK]
        carry[BLOCK:2 * BLOCK, :] = dv_all[:, 0:BLOCK]
        if nq > 1:
            dk_tile = jnp.concatenate([dk_all[:, BLOCK:tq], dk_last], axis=1)
            dv_tile = jnp.concatenate([dv_all[:, BLOCK:tq], dv_last], axis=1)
        else:
            dk_tile, dv_tile = dk_last, dv_last
        dkv_ref[:, 0:BLOCK] = _rope_f32(dk_tile.T, cos, -sin, pswap).astype(BF16)
        dkv_ref[:, BLOCK:2 * BLOCK] = dv_tile.T.astype(BF16)

        @pl.when((b == pl.num_programs(0) - 1) & (tt == nt - 1))
        def _():
            cps = copies(src_ref, land_ref, send_sems, recv_sems)
            for cp in cps:
                cp.wait_recv()
            for cp in cps:
                cp.wait_send()

    def row(b, tt):
        return b * nt + (nt - 1 - tt)

    def halo(b, tt):
        return jnp.maximum(row(b, tt) * nq - 1, 0)

    tile = pl.BlockSpec((tq, D), lambda b, tt: (row(b, tt), 0))
    return pl.pallas_call(
        body, name="attn_bwd", grid=(T // S, nt),
        in_specs=[pl.BlockSpec(memory_space=pltpu.SMEM),
                  pl.BlockSpec((tq, D), lambda b, tt: (row(b, tt), ZB_Q)),
                  pl.BlockSpec((tq, 2 * BLOCK), lambda b, tt: (row(b, tt), 0)),
                  pl.BlockSpec((BLOCK, 2 * BLOCK), lambda b, tt: (halo(b, tt), 0)),
                  tile, tile,
                  pl.BlockSpec((tq, BLOCK), lambda b, tt: (row(b, tt), 0)),
                  pl.BlockSpec((tq, BLOCK), lambda b, tt: (row(b, tt), 0)),
                  pl.BlockSpec((BLOCK, BLOCK), lambda b, tt: (halo(b, tt), 0)),
                  pl.BlockSpec((BLOCK, BLOCK), lambda b, tt: (halo(b, tt), 0)),
                  pl.BlockSpec(memory_space=pl.ANY), pl.BlockSpec(memory_space=pl.ANY)],
        out_specs=(pl.BlockSpec((tq, D), lambda b, tt: (row(b, tt), ZB_Q)),
                   pl.BlockSpec((tq, 2 * BLOCK), lambda b, tt: (row(b, tt), 0)),
                   pl.BlockSpec((8, BLOCK), lambda b, tt: (0, 0)), pl.BlockSpec(memory_space=pl.ANY)),
        out_shape=(jax.ShapeDtypeStruct(dz.shape, BF16), jax.ShapeDtypeStruct((T, 2 * BLOCK), BF16),
                   jax.ShapeDtypeStruct((8, BLOCK), F32), landing),
        input_output_aliases={10: 0},
        scratch_shapes=[pltpu.VMEM((2 * BLOCK, BLOCK), F32), pltpu.VMEM((BLOCK, tq + BLOCK), F32),
                        pltpu.VMEM((BLOCK, tq + BLOCK), F32), pltpu.SemaphoreType.DMA((3,)),
                        pltpu.SemaphoreType.DMA((3,))],
        compiler_params=_params(("arbitrary", "arbitrary")),
    )(sinks, z, zkv, zkv, o, do, cos_t, sin_t, cos_t, sin_t, dz, src)


def _dh(dz, dz_kv, wall, x, dx1, ln_pre, tm, tile0, ntiles, gx_prev, name, copies, src, landing):
    T = x.shape[0]
    nsem = 3

    def body(*refs):
        dz_ref, kv_ref, wt_ref, x_ref, dx1_ref, g_ref, src_ref = refs[:7]
        gx_ref, glp_ref, land_ref, wbuf, send_sems, recv_sems, wsem = refs[-7:]
        i = pl.program_id(0)

        @pl.when(i == 0)
        def _():
            glp_ref[...] = jnp.zeros_like(glp_ref)
            for cp in copies(src_ref, land_ref, send_sems, recv_sems):
                cp.start()
            load = pltpu.make_async_copy(wt_ref, wbuf, wsem)
            load.start()
            load.wait()

        dh = _mm(dz_ref[...], wbuf[0:ZKV, :]) + _mm(kv_ref[...], wbuf[ZKV:IN_WIDTH, :])
        xv = x_ref[...]
        r = lax.rsqrt(jnp.mean(xv * xv, axis=-1, keepdims=True) + EPS)
        xr = xv * r
        glp_ref[...] += jnp.sum(dh * xr, axis=0, keepdims=True)
        a = dh * g_ref[...]
        gx_ref[...] = dx1_ref[...] + r * (a - xr * jnp.mean(a * xr, axis=-1, keepdims=True))

        @pl.when(i == ntiles - 1)
        def _():
            cps = copies(src_ref, land_ref, send_sems, recv_sems)
            for cp in cps:
                cp.wait_recv()
            for cp in cps:
                cp.wait_send()

    tile = pl.BlockSpec((tm, D), lambda i: (tile0 + i, 0))
    any_spec = pl.BlockSpec(memory_space=pl.ANY)
    operands = [dz, dz_kv, wall, x, dx1, ln_pre, src] + ([] if gx_prev is None else [gx_prev])
    return pl.pallas_call(
        body, name=name, grid=(ntiles,),
        in_specs=[pl.BlockSpec((tm, ZKV), lambda i: (tile0 + i, 0)),
                  pl.BlockSpec((tm, 2 * BLOCK), lambda i: (tile0 + i, 0)),
                  any_spec, tile, tile, pl.BlockSpec((1, D), lambda i: (0, 0)), any_spec]
        + ([] if gx_prev is None else [any_spec]),
        out_specs=(tile, pl.BlockSpec((1, D), lambda i: (0, 0)), any_spec),
        out_shape=(jax.ShapeDtypeStruct((T, D), F32), jax.ShapeDtypeStruct((1, D), F32), landing),
        input_output_aliases={} if gx_prev is None else {7: 0},
        scratch_shapes=[pltpu.VMEM((IN_WIDTH, D), BF16), pltpu.SemaphoreType.DMA((nsem,)),
                        pltpu.SemaphoreType.DMA((nsem,)), pltpu.SemaphoreType.DMA],
        compiler_params=_params(("arbitrary",)),
    )(*operands)


def _gwt(dz, dz_kv, h, tt):
    T = dz.shape[0]
    nt = T // tt
    last = nt - 1
    kv = 2 * BLOCK

    def body(dz_ref, dzkv_ref, h_ref, gpack_ref, acc, sem):
        j = pl.program_id(0)
        t = pl.program_id(1)

        @pl.when((j < 7) & (t == 0))
        def _():
            acc[...] = _mm_tn(dz_ref[...], h_ref[...])

        @pl.when((j < 7) & (t > 0))
        def _():
            acc[...] += _mm_tn(dz_ref[...], h_ref[...])

        @pl.when((j == 7) & (t == 0))
        def _():
            acc[0:kv, :] = _mm_tn(dzkv_ref[...], h_ref[...])

        @pl.when((j == 7) & (t > 0))
        def _():
            acc[0:kv, :] += _mm_tn(dzkv_ref[...], h_ref[...])

        for jj in range(7):
            @pl.when((t == last) & (j == jj))
            def _(jj=jj):
                _flush_to_pack(acc, gpack_ref, WT0 + jj * D, sem)

        @pl.when((t == last) & (j == 7))
        def _():
            _flush_to_pack(acc.at[pl.ds(0, kv)], gpack_ref, WT0 + ZKV, sem)

    return pl.pallas_call(
        body, name="gwt", grid=(8, nt),
        in_specs=[pl.BlockSpec((tt, D), lambda j, t: (jnp.where(j == 7, last, t), jnp.minimum(j, 6))),
                  pl.BlockSpec((tt, kv), lambda j, t: (jnp.where(j == 7, t, 0), 0)),
                  pl.BlockSpec((tt, D), lambda j, t: (t, 0))],
        out_specs=pl.BlockSpec(memory_space=pl.ANY),
        out_shape=jax.ShapeDtypeStruct((N_SHARDS, WIN_SHARD, D), F32),
        scratch_shapes=[pltpu.VMEM((D, D), F32), pltpu.SemaphoreType.DMA],
        compiler_params=_params(("arbitrary", "arbitrary")),
    )(dz, dz_kv, h)


_BC1 = 1.0 - ADAM_B1 ** ADAM_STEP
_BC2 = 1.0 - ADAM_B2 ** ADAM_STEP


def _adamw_math(w, g, m, v):
    m = ADAM_B1 * m + (1.0 - ADAM_B1) * g
    v = ADAM_B2 * v + (1.0 - ADAM_B2) * (g * g)
    delta = -ADAM_LR * ((m / _BC1) / (jnp.sqrt(v / _BC2) + ADAM_EPS) + ADAM_WD * w)
    return delta, m, v


def _adamw_rows(g, w, m, v, rows, name):
    R, C = w.shape

    def body(g_ref, w_ref, m_ref, v_ref, go_ref, d_ref, nm_ref, nv_ref):
        gv = g_ref[...]
        d, nm, nv = _adamw_math(w_ref[...], gv, m_ref[...], v_ref[...])
        go_ref[...] = gv
        d_ref[...] = d
        nm_ref[...] = nm
        nv_ref[...] = nv

    spec = pl.BlockSpec((rows, C), lambda i: (i, 0))
    shp = jax.ShapeDtypeStruct((R, C), F32)
    return pl.pallas_call(
        body, name=name, grid=(R // rows,), in_specs=[spec] * 4, out_specs=(spec,) * 4,
        out_shape=(shp,) * 4, compiler_params=_params(("arbitrary",)),
    )(g, w, m, v)


def _adamw_square(gfin, ws, ms, vs):
    rb = 64
    nb = SQ_SHARD // rb

    def body(*refs):
        g_refs = refs[0:5]
        w_refs, m_refs, v_refs = refs[5:10], refs[10:15], refs[15:20]
        outs = refs[20:]
        for k in range(5):
            gk = g_refs[k][...]
            d, nm, nv = _adamw_math(w_refs[k][...], gk, m_refs[k][...], v_refs[k][...])
            outs[4 * k][...] = gk
            outs[4 * k + 1][...] = d
            outs[4 * k + 2][...] = nm
            outs[4 * k + 3][...] = nv

    spec = pl.BlockSpec((rb, D), lambda i: (i, 0))
    gspecs = [pl.BlockSpec((rb, D), lambda i, k=k: (SQ_SHARD * k // rb + i, 0)) for k in range(5)]
    shp = jax.ShapeDtypeStruct((SQ_SHARD, D), F32)
    res = pl.pallas_call(
        body, name="adamw_square", grid=(nb,), in_specs=gspecs + [spec] * 15, out_specs=(spec,) * 20,
        out_shape=(shp,) * 20, compiler_params=_params(("arbitrary",)),
    )(*([gfin] * 5), *ws, *ms, *vs)
    return [tuple(res[4 * k:4 * k + 4]) for k in range(5)]


def _adamw_small(gs, ws, ms, vs):
    n = len(gs)

    def body(*refs):
        outs = refs[4 * n:]
        for k in range(n):
            d, nm, nv = _adamw_math(refs[n + k][...], refs[k][...], refs[2 * n + k][...],
                                    refs[3 * n + k][...])
            outs[3 * k][...] = d
            outs[3 * k + 1][...] = nm
            outs[3 * k + 2][...] = nv

    vm = pl.BlockSpec(memory_space=pltpu.VMEM)
    shapes = []
    for w in ws:
        shapes += [jax.ShapeDtypeStruct(w.shape, F32)] * 3
    res = pl.pallas_call(
        body, name="adamw_small", in_specs=[vm] * (4 * n), out_specs=(vm,) * (3 * n),
        out_shape=tuple(shapes),
    )(*gs, *ws, *ms, *vs)
    return [tuple(res[3 * k:3 * k + 3]) for k in range(n)]


def _rope_lanes():
    inv = jnp.power(ROPE_THETA, -jnp.arange(0, ROPE_DIM, 2, dtype=F32) / ROPE_DIM)
    inv_h = jnp.concatenate([inv, inv, jnp.zeros((HEAD_DIM - ROPE_DIM,), F32)])
    sign_h = np.array([-1.0] * (ROPE_DIM // 2) + [1.0] * (ROPE_DIM // 2) + [0.0] * (HEAD_DIM - ROPE_DIM),
                      np.float32)
    rows = jnp.stack([jnp.concatenate([inv_h, inv_h]), jnp.asarray(np.concatenate([sign_h, sign_h]))])
    return jnp.concatenate([rows, jnp.zeros((6, BLOCK), F32)], axis=0)


def kernel(x, p, positions, w_in, ln_pre, ln_post, w_dw, b_dw, conv_ln_g, conv_ln_b, w_pw, sinks, w_br_conv, w_br_attn, w_out, w_ple_gate, w_ple_proj, loss_target, m_w_in, m_ln_pre, m_ln_post, m_w_dw, m_b_dw, m_conv_ln_g, m_conv_ln_b, m_w_pw, m_sinks, m_w_br_conv, m_w_br_attn, m_w_out, m_w_ple_gate, m_w_ple_proj, v_w_in, v_ln_pre, v_ln_post, v_w_dw, v_b_dw, v_conv_ln_g, v_conv_ln_b, v_w_pw, v_sinks, v_w_br_conv, v_w_br_attn, v_w_out, v_w_ple_gate, v_w_ple_proj):
    nb, S, _ = x.shape
    T = nb * S
    xc = lax.axis_index("x")
    yc = lax.axis_index("y")
    cc = lax.axis_index("c")
    shard = 2 * xc + yc

    sq_w = (w_pw, w_br_conv, w_br_attn, w_out, w_ple_gate)
    wdw_shard = jnp.pad(w_dw[0], ((0, 1), (0, 0)))
    x2 = x.reshape(T, D)
    tm_res = min(TILE_RESIDENT, T // 2)
    wt, wdw_all, h, cos_t, sin_t = _gather_win(w_in[0].T.astype(BF16), wdw_shard, x2, ln_pre,
                                               positions.astype(F32).reshape(T, 1), _rope_lanes(), tm_res)
    wdw = jnp.concatenate([wdw_all[s] for s in range(N_SHARDS)], axis=1)

    tgt = loss_target.reshape(T, D)
    p2 = p.reshape(T, PLE)
    sinks1 = sinks.reshape(N_HEADS)

    tm = min(TILE_TOKEN, S)
    tq = min(TILE_ATTN, S)

    z, zkv, wall, wppf = _inproj(h, wt, [w[0].astype(BF16) for w in sq_w],
                                 w_ple_proj[0].T.reshape(WPP_SHARD, D).astype(BF16), tm_res)
    wppt = wppf.reshape(D, PLE)
    ya, y, rstd, pw = _conv_fwd(z, wdw, b_dw, conv_ln_g, conv_ln_b, wall, S, tm)
    o = _attn_fwd(z, zkv, cos_t, sin_t, sinks1, S, tq)
    loss_p, dx1, dm, yb, g_ln_post, gsq, gw_ppt = _tail_a(x2, tgt, p2, o, ya, z, ln_post, wall, wppt, tm)

    cidx = jnp.reshape(cc, (1,)).astype(jnp.int32)
    scidx = jnp.stack([shard, cc]).astype(jnp.int32)

    def landing(pack, n, dtype):
        return jax.ShapeDtypeStruct((n, pack.shape[1] // 2, D), dtype)

    dz, do, dc, gvec, gsq = _tail_b(dm, ya, yb, o, z, pw, y, rstd, conv_ln_g, conv_ln_b, wall,
                                    gw_ppt.reshape(PLE, D), gsq, tm)
    dz, g_wdw, r1_sq = _conv_bwd(dc, z, wdw, dz, S, tm, _exchange_copies, gsq, landing(gsq, N_SHARDS, F32))
    cs_sq = _chip_sum(cidx, gsq, r1_sq, "chip_sum_sq")
    dz, dkv, g_sinks, r2_sq = _attn_bwd(z, zkv, o, do, cos_t, sin_t, sinks1, dz, S, tq, _chip_sum_copies, cs_sq,
                                        landing(gsq, 3, BF16))
    gwt_pack = _gwt(dz, dkv, h, min(2 * TILE_PROJ, T))

    tm_dh = tm_res
    n_dh = T // tm_dh
    n_a = max(1, n_dh // 4)
    gx, g_ln_pre_a, r1_wt = _dh(
        dz, dkv, wt, x2, dx1, ln_pre, tm_dh, 0, n_a, None, "dh_exchange", _exchange_copies, gwt_pack,
        landing(gwt_pack, N_SHARDS, F32))
    cs_wt = _chip_sum(cidx, gwt_pack, r1_wt, "chip_sum_wt")
    gx, g_ln_pre_b, r2_wt = _dh(
        dz, dkv, wt, x2, dx1, ln_pre, tm_dh, n_a, n_dh - n_a, gx, "dh_send", _chip_sum_copies, cs_wt,
        landing(gwt_pack, 3, BF16))
    g_ln_pre = g_ln_pre_a + g_ln_pre_b
    row37 = jnp.concatenate([g_sinks[0:1, 0:N_HEADS], loss_p, jnp.zeros((1, D - N_HEADS - 1), F32)], axis=1)
    vec = jnp.concatenate([g_wdw, g_ln_pre, g_ln_post, gvec[2:3], gvec[0:1], gvec[1:2], row37,
                           jnp.zeros((VEC_ROWS - 38, D), F32)], axis=0)
    gfin_wt, gfin_sq, tot = _finish_reduce(_final_half(scidx, gwt_pack, r1_wt, r2_wt, "final_half_wt"),
                                           _final_half(scidx, gsq, r1_sq, r2_sq, "final_half_sq"), vec)

    g_w_in, d_w_in, nm_w_in, nv_w_in = [a.T for a in _adamw_rows(
        gfin_wt, w_in[0].T, m_w_in[0].T, v_w_in[0].T, WIN_SHARD // 8, "adamw_w_in")]
    g_w_in = g_w_in[None]
    sq_m = (m_w_pw, m_w_br_conv, m_w_br_attn, m_w_out, m_w_ple_gate)
    sq_v = (v_w_pw, v_w_br_conv, v_w_br_attn, v_w_out, v_w_ple_gate)
    sq_res = _adamw_square(gfin_sq, [w[0] for w in sq_w], [m[0] for m in sq_m], [v[0] for v in sq_v])
    g_wpp = gfin_sq[5 * SQ_SHARD:SQ_PACK].reshape(PLE, PLE).T
    g_dw_all = tot[0:CONV_K]
    g_dw = lax.dynamic_slice_in_dim(g_dw_all, shard * PLE, PLE, axis=1)
    small_g = [g_wpp, g_dw, tot[32:33], tot[33:34], tot[34:35], tot[35:36], tot[36:37],
               tot[37:38, 0:N_HEADS]]
    small_w = [w_ple_proj[0], w_dw[0], ln_pre, ln_post, b_dw, conv_ln_g, conv_ln_b, sinks]
    small_m = [m_w_ple_proj[0], m_w_dw[0], m_ln_pre, m_ln_post, m_b_dw, m_conv_ln_g, m_conv_ln_b, m_sinks]
    small_v = [v_w_ple_proj[0], v_w_dw[0], v_ln_pre, v_ln_post, v_b_dw, v_conv_ln_g, v_conv_ln_b, v_sinks]
    small = _adamw_small(small_g, small_w, small_m, small_v)

    loss = tot[37, N_HEADS]
    grads = [g_w_in, small_g[2], small_g[3], g_dw[None], small_g[4], small_g[5], small_g[6],
             sq_res[0][0][None], small_g[7], sq_res[1][0][None], sq_res[2][0][None], sq_res[3][0][None],
             sq_res[4][0][None], g_wpp[None]]

    def triple(i):
        w_in_t = (d_w_in[None], nm_w_in[None], nv_w_in[None])
        sq = lambda k: tuple(a[None] for a in sq_res[k][1:4])
        sm = lambda k, lead: tuple(a[None] if lead else a for a in small[k])
        return [w_in_t[i], sm(2, False)[i], sm(3, False)[i], sm(1, True)[i], sm(4, False)[i],
                sm(5, False)[i], sm(6, False)[i], sq(0)[i], sm(7, False)[i], sq(1)[i], sq(2)[i], sq(3)[i],
                sq(4)[i], sm(0, True)[i]]

    return (loss, gx.reshape(nb, S, D), *grads, *triple(0), *triple(1), *triple(2))
```

```python
import functools

import jax
import jax.numpy as jnp
import numpy as np
from jax import lax
from jax.experimental import pallas as pl
from jax.experimental.pallas import tpu as pltpu

F32 = jnp.float32
BF16 = jnp.bfloat16

D = 1024
PLE = 256
N_HEADS = 16
HEAD_DIM = 64
BLOCK = 128
CONV_K = 31
ROPE_DIM = 16
ROPE_THETA = 500000.0
EPS = 1e-6
IN_WIDTH = 7424
N_SHARDS = 4

ADAM_LR = 0.001
ADAM_B1 = 0.9
ADAM_B2 = 0.999
ADAM_EPS = 1e-08
ADAM_WD = 0.01
ADAM_STEP = 10

SQ_NAMES = ("w_pw", "w_br_conv", "w_br_attn", "w_out", "w_ple_gate")
WT0 = 5 * D
WPP0 = WT0 + IN_WIDTH
WALL_ROWS = WPP0 + PLE
WIN_SHARD = IN_WIDTH // N_SHARDS
SQ_SHARD = D // N_SHARDS
WPP_SHARD = PLE * PLE // D
PACK_ROWS = WIN_SHARD + 5 * SQ_SHARD + WPP_SHARD
HALF_ROWS = PACK_ROWS // 2
VMEM_LIMIT = 56 * 1024 * 1024
MESH = pl.DeviceIdType.MESH
TILE_RESIDENT = 512
TILE_PROJ = 1024
TILE_TOKEN = 256
TILE_ATTN = 512
TAIL_PARTS = 1


ZB_AGATE, ZB_GCONV, ZB_GATTN, ZB_CGATE, ZB_CVAL, ZB_CGLU, ZB_Q = range(7)
ZKV = 7 * D
_SEGMENTS = ((0, D, ZB_CVAL * D), (D, D, ZB_CGLU * D), (2 * D, D, ZB_CGATE * D), (3 * D, D, ZB_Q * D),
             (4 * D, 2 * BLOCK, ZKV), (4 * D + 2 * BLOCK, D, ZB_AGATE * D),
             (5 * D + 2 * BLOCK, D, ZB_GCONV * D), (6 * D + 2 * BLOCK, D, ZB_GATTN * D))
_WT_CUTS = (0, 192, 640, 1216, WIN_SHARD)


def _zp_row(o):
    for a, w, zp in _SEGMENTS:
        if a <= o < a + w:
            return zp + o - a
    raise ValueError(o)


def _pieces(s):
    out = []
    for a, b in zip(_WT_CUTS[:-1], _WT_CUTS[1:]):
        first = _zp_row(WIN_SHARD * s + a)
        assert _zp_row(WIN_SHARD * s + b - 1) == first + b - a - 1
        out.append((a, b - a, WT0 + first))
    for k in range(5):
        out.append((WIN_SHARD + SQ_SHARD * k, SQ_SHARD, D * k + SQ_SHARD * s))
    out.append((WIN_SHARD + 5 * SQ_SHARD, WPP_SHARD, WPP0 + WPP_SHARD * s))
    return out


N_PIECES = len(_pieces(0))


def _wall_segments(wall0, rows):
    out = []
    for s in range(N_SHARDS):
        for pr, n, wr in _pieces(s):
            lo, hi = max(wr, wall0), min(wr + n, wall0 + rows)
            if lo < hi:
                out.append((lo - wall0, hi - lo, s, pr + lo - wr))
    assert sum(n for _, n, _, _ in out) == rows
    return out


def _sel(s, vals):
    r = jnp.int32(vals[0])
    for i in range(1, len(vals)):
        r = jnp.where(s == i, jnp.int32(vals[i]), r)
    return r


def _sig(x):
    return 1.0 / (1.0 + jnp.exp(-x))


def _mm(a, b):
    return lax.dot_general(a, b, (((1,), (0,)), ((), ())), preferred_element_type=F32)


def _mm_nt(a, b):
    return lax.dot_general(a, b, (((1,), (1,)), ((), ())), preferred_element_type=F32)


def _mm_tn(a, b):
    return lax.dot_general(a, b, (((0,), (0,)), ((), ())), preferred_element_type=F32)


def _params(sem=None):
    return pltpu.CompilerParams(dimension_semantics=sem, vmem_limit_bytes=VMEM_LIMIT)


def _flush_to_pack(acc_ref, gpack_ref, wall0, sem):
    base = 0 if gpack_ref.shape[1] == WIN_SHARD else WIN_SHARD
    for r, n, s, pr in _wall_segments(wall0, acc_ref.shape[0]):
        assert 0 <= pr - base and pr - base + n <= gpack_ref.shape[1]
        cp = pltpu.make_async_copy(acc_ref.at[pl.ds(r, n)], gpack_ref.at[s, pl.ds(pr - base, n)], sem)
        cp.start()
        cp.wait()


def _coords():
    return lax.axis_index("x"), lax.axis_index("y"), lax.axis_index("c")


def _chip_peers(x, y):
    return [(1 - x, y), (x, 1 - y), (1 - x, 1 - y)]


WIN_PIECES = tuple(range(len(_WT_CUTS) - 1))
SQ_PIECES = tuple(range(len(WIN_PIECES), N_PIECES))


def _gather_ops(group, src, landing, bytes_ref, stage, send_sems, recv_sems, loc_sem):
    sizes = [_pieces(0)[p][1] for p in group]
    half_rows = sum(n // 2 for n in sizes)

    def rcopy(a, b, k, dev):
        return pltpu.make_async_remote_copy(src_ref=a, dst_ref=b, send_sem=send_sems.at[k],
                                            recv_sem=recv_sems.at[k], device_id=dev, device_id_type=MESH)

    def total(k):
        x, y, c = _coords()
        rows = bytes_ref.at[pl.ds(0, half_rows)]
        return rcopy(rows, rows, k, (x, y, c))

    def send():
        x, y, c = _coords()
        s_me = 2 * x + y
        for k, (px, py) in enumerate(_chip_peers(x, y)):
            for p, n in zip(group, sizes):
                h = n // 2
                rcopy(src(p, c * h, h), landing(p, s_me, c * h, h), k, (px, py, c)).start()
        for p, n in zip(group, sizes):
            for a, b in ((src(p, 0, n), stage.at[pl.ds(0, n)]), (stage.at[pl.ds(0, n)], landing(p, s_me, 0, n))):
                cp = pltpu.make_async_copy(a, b, loc_sem)
                cp.start()
                cp.wait()

    def forward():
        x, y, c = _coords()
        for k, (px, py) in enumerate(_chip_peers(x, y)):
            total(k).wait_recv()
            for p, n in zip(group, sizes):
                rows = landing(p, 2 * px + py, c * (n // 2), n // 2)
                rcopy(rows, rows, 3 + k, (x, y, 1 - c)).start()

    def finish():
        for k in range(3):
            total(3 + k).wait_recv()
        for k in range(6):
            total(k).wait_send()

    return send, forward, finish


def _piece_rows(ref, start, off, n):
    first = start + off
    return ref.at[pl.ds(first if isinstance(first, int) else pl.multiple_of(first, 32), n)]


def _prenorm(x, ln_pre, pos, inv_sign, tm):
    T = x.shape[0]

    def body(x_ref, g_ref, pos_ref, is_ref, h_ref, cos_ref, sin_ref):
        xv = x_ref[...]
        r = lax.rsqrt(jnp.mean(xv * xv, axis=-1, keepdims=True) + EPS)
        h_ref[...] = (xv * r * g_ref[...]).astype(BF16)
        ang = pos_ref[...] * is_ref[0:1, :]
        cos_ref[...] = jnp.cos(ang)
        sin_ref[...] = jnp.sin(ang) * is_ref[1:2, :]

    return pl.pallas_call(
        body, name="prenorm", grid=(T // tm,),
        out_shape=(jax.ShapeDtypeStruct((T, D), BF16), jax.ShapeDtypeStruct((T, BLOCK), F32),
                   jax.ShapeDtypeStruct((T, BLOCK), F32)),
        in_specs=[pl.BlockSpec((tm, D), lambda i: (i, 0)), pl.BlockSpec((1, D), lambda i: (0, 0)),
                  pl.BlockSpec((tm, 1), lambda i: (i, 0)), pl.BlockSpec((8, BLOCK), lambda i: (0, 0))],
        out_specs=(pl.BlockSpec((tm, D), lambda i: (i, 0)), pl.BlockSpec((tm, BLOCK), lambda i: (i, 0)),
                   pl.BlockSpec((tm, BLOCK), lambda i: (i, 0))),
        compiler_params=_params(("arbitrary",)),
    )(x, ln_pre, pos, inv_sign)


SQ_PACK = PACK_ROWS - WIN_SHARD


def _row_tile(half):
    return max(t for t in range(8, 321, 8) if half % t == 0)


def _exchange_copies(g_ref, r1_ref, send_sems, recv_sems):
    x, y, c = _coords()
    half = g_ref.shape[1] // 2
    return [pltpu.make_async_remote_copy(
        src_ref=g_ref.at[:, pl.ds(pl.multiple_of((1 - c) * half, 32), half), :], dst_ref=r1_ref,
        send_sem=send_sems.at[0], recv_sem=recv_sems.at[0], device_id=(x, y, 1 - c), device_id_type=MESH)]


def _chip_sum_copies(cs_ref, r2_ref, send_sems, recv_sems):
    x, y, c = _coords()
    return [pltpu.make_async_remote_copy(
        src_ref=cs_ref.at[2 * px + py], dst_ref=r2_ref.at[k], send_sem=send_sems.at[k],
        recv_sem=recv_sems.at[k], device_id=(px, py, c), device_id_type=MESH)
        for k, (px, py) in enumerate(_chip_peers(x, y))]


def _chip_sum(cidx, gpack, r1, name):
    half = gpack.shape[1] // 2
    rt = _row_tile(half)

    def body(c_ref, g_ref, r_ref, o_ref):
        o_ref[...] = (g_ref[...] + r_ref[...]).astype(BF16)

    nt = half // rt
    return pl.pallas_call(
        body, name=name,
        grid_spec=pltpu.PrefetchScalarGridSpec(
            num_scalar_prefetch=1, grid=(N_SHARDS, nt),
            in_specs=[pl.BlockSpec((1, rt, D), lambda s, t, c: (s, c[0] * nt + t, 0)),
                      pl.BlockSpec((1, rt, D), lambda s, t, c: (s, t, 0))],
            out_specs=pl.BlockSpec((1, rt, D), lambda s, t, c: (s, t, 0))),
        out_shape=jax.ShapeDtypeStruct((N_SHARDS, half, D), BF16),
        compiler_params=_params(("arbitrary", "arbitrary")),
    )(cidx, gpack, r1)


def _final_half(sc, gpack, r1, r2, name):
    rows = gpack.shape[1]
    half = rows // 2
    rt = _row_tile(half)

    def body(sc_ref, g_ref, r_ref, p_ref, o_ref):
        acc = g_ref[0] + r_ref[0]
        for k in range(3):
            acc = acc + p_ref[k].astype(F32)
        o_ref[...] = acc

    nt = half // rt
    return pl.pallas_call(
        body, name=name,
        grid_spec=pltpu.PrefetchScalarGridSpec(
            num_scalar_prefetch=1, grid=(nt,),
            in_specs=[pl.BlockSpec((1, rt, D), lambda t, sc: (sc[0], sc[1] * nt + t, 0)),
                      pl.BlockSpec((1, rt, D), lambda t, sc: (sc[0], t, 0)),
                      pl.BlockSpec((3, rt, D), lambda t, sc: (0, t, 0))],
            out_specs=pl.BlockSpec((rt, D), lambda t, sc: (sc[1] * nt + t, 0))),
        out_shape=jax.ShapeDtypeStruct((rows, D), F32),
        compiler_params=_params(("arbitrary",)),
    )(sc, gpack, r1, r2)


VEC_ROWS = 40


def _finish_reduce(fwt, fsq, vec):
    def body(fwt_ref, fsq_ref, v_ref, owt_ref, osq_ref, tot_ref, buf, send_sems, recv_sems):
        x, y, c = _coords()
        swaps = []
        for k, (f_ref, o_ref) in enumerate(((fwt_ref, owt_ref), (fsq_ref, osq_ref))):
            half = f_ref.shape[0] // 2
            rows = pl.ds(pl.multiple_of(c * half, 32), half)
            swaps.append(pltpu.make_async_remote_copy(
                src_ref=f_ref.at[rows], dst_ref=o_ref.at[rows], send_sem=send_sems.at[7 + k],
                recv_sem=recv_sems.at[7 + k], device_id=(x, y, 1 - c), device_id_type=MESH))
        for cp in swaps:
            cp.start()
        me = 4 * x + 2 * y + c
        buf[me] = v_ref[...]
        cps = []
        for r in range(1, 8):
            dx, dy, dc = (r >> 2) & 1, (r >> 1) & 1, r & 1
            peer = (1 - x if dx else x, 1 - y if dy else y, 1 - c if dc else c)
            cp = pltpu.make_async_remote_copy(
                src_ref=v_ref, dst_ref=buf.at[me], send_sem=send_sems.at[r - 1],
                recv_sem=recv_sems.at[r - 1], device_id=peer, device_id_type=MESH)
            cp.start()
            cps.append(cp)
        for cp in cps:
            cp.wait_recv()
        for cp in cps:
            cp.wait_send()
        acc = buf[0]
        for d in range(1, 8):
            acc = acc + buf[d]
        tot_ref[...] = acc
        for cp in swaps:
            cp.wait()

    any_spec = pl.BlockSpec(memory_space=pl.ANY)
    vm = pl.BlockSpec(memory_space=pltpu.VMEM)
    return pl.pallas_call(
        body, name="finish_reduce",
        out_shape=(jax.ShapeDtypeStruct(fwt.shape, F32), jax.ShapeDtypeStruct(fsq.shape, F32),
                   jax.ShapeDtypeStruct((VEC_ROWS, D), F32)),
        in_specs=[any_spec, any_spec, vm], out_specs=(any_spec, any_spec, vm),
        input_output_aliases={0: 0, 1: 1},
        scratch_shapes=[pltpu.VMEM((8, VEC_ROWS, D), F32), pltpu.SemaphoreType.DMA((9,)),
                        pltpu.SemaphoreType.DMA((9,))],
    )(fwt, fsq, vec)


SOLO_ROWS = WIN_SHARD - BLOCK // 2


def _solo_ranges(s):
    lo = 0 if s % 2 == 0 else BLOCK // 2
    cut = []
    for a, n, wr in _pieces(s)[:len(WIN_PIECES)]:
        b0, b1 = max(a, lo), min(a + n, lo + SOLO_ROWS)
        if b0 < b1:
            cut.append((wr - WT0 + b0 - a, b1 - b0))
    out = []
    for z0, n in sorted(cut):
        if out and out[-1][0] + out[-1][1] == z0:
            out[-1] = (out[-1][0], out[-1][1] + n)
        else:
            out.append((z0, n))
    out = [r for z0, n in out for r in (((z0, ZKV - z0), (ZKV, z0 + n - ZKV)) if z0 < ZKV < z0 + n else ((z0, n),))]
    assert all(z0 % BLOCK == 0 and n % BLOCK == 0 for z0, n in out) and sum(n for _, n in out) == SOLO_ROWS
    return out


def _shared_tile(pair):
    z0 = _zp_row(WIN_SHARD * (2 * pair) + SOLO_ROWS)
    assert z0 % BLOCK == 0 and _zp_row(WIN_SHARD * (2 * pair + 1)) == z0 + BLOCK // 2
    return z0


def _inproj(h, win_t, wdw_shard, sq_shards, wpp_shard, tm):
    T = h.shape[0]
    nsq = len(sq_shards)
    n_t = T // tm
    assert n_t >= 2
    tables = [[_pieces(s)[p][2] - WT0 for s in range(N_SHARDS)] for p in WIN_PIECES]
    sizes = [_pieces(0)[p][1] for p in WIN_PIECES]
    half_rows = sum(n // 2 for n in sizes)
    relation_of_pass = {1: 1, 2: 0, 3: 2}

    def body(*refs):
        h_ref, win_ref, wdw_ref = refs[:3]
        sq_refs = refs[3:3 + nsq]
        wpp_ref = refs[3 + nsq]
        (z_ref, zkv_ref, wt_ref, wdwall_ref, wsq_ref, wppf_ref, wbuf, stage, stage_sh, sq_stage, wsend, wrecv,
         sqsend, sqrecv, loc_sems, out_sems, sh_sems) = refs[4 + nsq:]
        p = pl.program_id(0)
        t = pl.program_id(1)
        x, y, c = _coords()
        s_me = 2 * x + y
        peers = _chip_peers(x, y)
        shard = jnp.bitwise_xor(s_me, p)
        first, last = t == 0, t == n_t - 1

        def sq_src(q, off, n):
            k = q - SQ_PIECES[0]
            return _piece_rows(wpp_ref if k == nsq else sq_refs[k], 0, off, n)

        def sq_landing(q, s, off, n):
            k = q - SQ_PIECES[0]
            if k == nsq:
                return _piece_rows(wppf_ref, WPP_SHARD * s, off, n)
            return _piece_rows(wsq_ref, D * k + SQ_SHARD * s, off, n)

        sq_send, sq_forward, sq_finish = _gather_ops(SQ_PIECES, sq_src, sq_landing, wsq_ref, sq_stage, sqsend,
                                                     sqrecv, loc_sems.at[2])

        def rcopy(a, b, k, dev):
            return pltpu.make_async_remote_copy(src_ref=a, dst_ref=b, send_sem=wsend.at[k], recv_sem=wrecv.at[k],
                                                device_id=dev, device_id_type=MESH)

        def total(k):
            rows = wt_ref.at[pl.ds(0, half_rows)]
            return rcopy(rows, rows, k, (x, y, c))

        def in_hbm(q, s, off, n):
            return _piece_rows(wt_ref, _sel(s, tables[q]), off, n)

        def in_vmem(q, s):
            return _piece_rows(wbuf, _sel(s, tables[q]), 0, sizes[q])

        def send_to(k):
            px, py = peers[k]
            for q, n in zip(WIN_PIECES, sizes):
                rcopy(_piece_rows(win_ref, _WT_CUTS[q], c * (n // 2), n // 2), in_hbm(q, s_me, c * (n // 2), n // 2),
                      k, (px, py, c)).start()

        def forward_from(k):
            px, py = peers[k]
            total(k).wait_recv()
            for q, n in zip(WIN_PIECES, sizes):
                rows = in_hbm(q, 2 * px + py, c * (n // 2), n // 2)
                rcopy(rows, rows, 3 + k, (x, y, 1 - c)).start()
            total(3 + k).wait_recv()

        def shard_total(a, b, sem):
            return pltpu.make_async_copy(a.at[pl.ds(0, WIN_SHARD)], b.at[pl.ds(0, WIN_SHARD)], sem)

        def wdw_copies():
            return [pltpu.make_async_remote_copy(
                src_ref=wdw_ref, dst_ref=wdwall_ref.at[s_me], send_sem=wsend.at[6 + k], recv_sem=wrecv.at[6 + k],
                device_id=(px, py, c), device_id_type=MESH) for k, (px, py) in enumerate(peers)]

        def own_wdw():
            return pltpu.make_async_copy(wdw_ref, wdwall_ref.at[s_me], loc_sems.at[3])

        @pl.when((p == 0) & first)
        def _():
            send_to(0)
            send_to(1)
            own_wdw().start()
            for cp in wdw_copies():
                cp.start()
            for q in WIN_PIECES:
                pltpu.make_async_copy(_piece_rows(win_ref, _WT_CUTS[q], 0, sizes[q]), in_vmem(q, s_me),
                                      loc_sems.at[0]).start()
            shard_total(win_ref, wbuf, loc_sems.at[0]).wait()
            for q in WIN_PIECES:
                pltpu.make_async_copy(in_vmem(q, s_me), in_hbm(q, s_me, 0, sizes[q]), loc_sems.at[1]).start()

        for pp, k in relation_of_pass.items():
            @pl.when((p == pp - 1) & last)
            def _(k=k):
                forward_from(k)
                px, py = peers[k]
                for q in WIN_PIECES:
                    pltpu.make_async_copy(in_hbm(q, 2 * px + py, 0, sizes[q]), in_vmem(q, 2 * px + py),
                                          loc_sems.at[0]).start()

            @pl.when((p == pp) & first)
            def _(pp=pp):
                shard_total(wt_ref, wbuf, loc_sems.at[0]).wait()
                if pp == 1:
                    total(0).wait_send()
                    total(1).wait_send()
                    send_to(2)
                if pp == 2:
                    sq_send()

        @pl.when((p == 3) & (t == n_t // 2))
        def _():
            sq_forward()

        step = p * n_t + t
        slot = step % 2
        rows = pl.ds(pl.multiple_of(t * tm, tm), tm)

        def out_total(sl):
            return pltpu.make_async_copy(stage.at[sl], stage.at[sl], out_sems.at[sl])

        def sh_copy(sl, z0):
            return pltpu.make_async_copy(stage_sh.at[sl], z_ref.at[rows, pl.ds(z0, BLOCK)], sh_sems.at[sl])

        @pl.when(step >= 2)
        def _():
            out_total(slot).wait()

        @pl.when((step >= 2) & (((step - 2) // n_t) % 2 == 1))
        def _():
            sh_copy(slot, 0).wait()

        for s in range(N_SHARDS):
            @pl.when(shard == s)
            def _(s=s):
                off = 0
                for z0, n in _solo_ranges(s):
                    stage[slot, :, off:off + n] = _mm_nt(h_ref[...], wbuf[z0:z0 + n, :]).astype(BF16)
                    dst = zkv_ref.at[rows] if z0 == ZKV else z_ref.at[rows, pl.ds(z0, n)]
                    pltpu.make_async_copy(stage.at[slot, :, pl.ds(off, n)], dst, out_sems.at[slot]).start()
                    off += n

        @pl.when(p % 2 == 1)
        def _():
            z0 = pl.multiple_of(jnp.where(shard < 2, _shared_tile(0), _shared_tile(1)), BLOCK)
            stage_sh[slot] = _mm_nt(h_ref[...], wbuf[pl.ds(z0, BLOCK), :]).astype(BF16)
            sh_copy(slot, z0).start()

        @pl.when((p == 3) & last)
        def _():
            sq_finish()
            for k in (2, 3, 4, 5):
                total(k).wait_send()
            shard_total(wbuf, wt_ref, loc_sems.at[1]).wait()
            cps = wdw_copies()
            for cp in cps:
                cp.wait_recv()
            for cp in cps:
                cp.wait_send()
            own_wdw().wait()
            for sl in range(2):
                out_total(sl).wait()
                sh_copy(sl, 0).wait()

    any_spec = pl.BlockSpec(memory_space=pl.ANY)
    return pl.pallas_call(
        body, name="inproj", grid=(N_SHARDS, n_t),
        in_specs=[pl.BlockSpec((tm, D), lambda p, t: (t, 0))] + [any_spec] * (nsq + 3),
        out_specs=(any_spec,) * 6,
        out_shape=(jax.ShapeDtypeStruct((T, ZKV), BF16), jax.ShapeDtypeStruct((T, 2 * BLOCK), BF16),
                   jax.ShapeDtypeStruct((IN_WIDTH, D), BF16), jax.ShapeDtypeStruct((N_SHARDS, 32, PLE), F32),
                   jax.ShapeDtypeStruct((nsq * D, D), BF16), jax.ShapeDtypeStruct((PLE, D), BF16)),
        scratch_shapes=[pltpu.VMEM((IN_WIDTH, D), BF16), pltpu.VMEM((2, tm, SOLO_ROWS), BF16),
                        pltpu.VMEM((2, tm, BLOCK), BF16), pltpu.VMEM((SQ_SHARD, D), BF16),
                        pltpu.SemaphoreType.DMA((9,)), pltpu.SemaphoreType.DMA((9,)),
                        pltpu.SemaphoreType.DMA((6,)), pltpu.SemaphoreType.DMA((6,)),
                        pltpu.SemaphoreType.DMA((4,)), pltpu.SemaphoreType.DMA((2,)),
                        pltpu.SemaphoreType.DMA((2,))],
        compiler_params=_params(("arbitrary", "arbitrary")),
    )(h, win_t, wdw_shard, *sq_shards, wpp_shard)


HALO = 32
CONV_RC = 64
CONV_LC = 256


def _conv_taps(w_ref, src, r0, lane0, offset_of_tap):
    lanes = pl.ds(lane0, CONV_LC)
    out = None
    for b in range(8):
        taps = [k for k in range(CONV_K) if offset_of_tap(k) % 8 == b]
        if not taps:
            continue
        rows = CONV_RC + (8 if b else 0)
        vb = None
        for k in taps:
            term = w_ref[k:k + 1, lanes] * src[pl.ds(r0 + (offset_of_tap(k) - b), rows), lanes]
            vb = term if vb is None else vb + term
        vb = vb[b:b + CONV_RC] if b else vb
        out = vb if out is None else out + vb
    return out


def _conv_fwd(z, wdw, b_dw, ln_g, ln_b, wall, S, tm):
    T = z.shape[0]
    nt = S // tm
    hb = tm // HALO

    def body(cv_ref, cg_ref, cgate_ref, hcv_ref, hcg_ref, wdw_ref, bdw_ref, lng_ref, lnb_ref, wpw_ref,
             wbrc_ref, ya_ref, y_ref, rstd_ref, pw_ref, ubuf, cbuf):
        t = pl.program_id(1)
        ubuf[HALO:HALO + tm, :] = cv_ref[...].astype(F32) * _sig(cg_ref[...].astype(F32))
        hu = hcv_ref[...].astype(F32) * _sig(hcg_ref[...].astype(F32))
        ubuf[0:HALO, :] = jnp.where(t > 0, hu, 0.0)
        ubuf[HALO + tm:HALO + tm + 8, :] = jnp.zeros((8, D), F32)

        def chunk(ci, carry):
            r0 = pl.multiple_of(ci * CONV_RC, CONV_RC)
            for lg in range(D // CONV_LC):
                acc = _conv_taps(wdw_ref, ubuf, r0, lg * CONV_LC, lambda k: HALO - (CONV_K - 1) + k)
                cbuf[pl.ds(r0, CONV_RC), pl.ds(lg * CONV_LC, CONV_LC)] = acc
            return carry

        lax.fori_loop(0, tm // CONV_RC, chunk, 0)
        cc = cbuf[...] + bdw_ref[...]
        mu = jnp.mean(cc, axis=-1, keepdims=True)
        dd = cc - mu
        rstd = lax.rsqrt(jnp.mean(dd * dd, axis=-1, keepdims=True) + EPS)
        yn = dd * rstd
        y_ref[...] = yn.astype(BF16)
        rstd_ref[...] = rstd
        n = yn * lng_ref[...] + lnb_ref[...]
        s = n * _sig(n)
        pw = _mm(s.astype(BF16), wpw_ref[...])
        pw_ref[...] = pw.astype(BF16)
        gt = cgate_ref[...].astype(F32)
        ya_in = pw * (gt * _sig(gt))
        ya_ref[...] = _mm(ya_in.astype(BF16), wbrc_ref[...]).astype(BF16)

    def row(b, t):
        return b * nt + t

    def halo(b, t):
        return jnp.maximum(row(b, t) * hb - 1, 0)

    vec = pl.BlockSpec((1, D), lambda b, t: (0, 0))
    tile = lambda j: pl.BlockSpec((tm, D), lambda b, t: (row(b, t), j))
    out_tile = pl.BlockSpec((tm, D), lambda b, t: (row(b, t), 0))
    return pl.pallas_call(
        body, name="conv_fwd", grid=(T // S, nt),
        in_specs=[tile(ZB_CVAL), tile(ZB_CGLU), tile(ZB_CGATE),
                  pl.BlockSpec((HALO, D), lambda b, t: (halo(b, t), ZB_CVAL)),
                  pl.BlockSpec((HALO, D), lambda b, t: (halo(b, t), ZB_CGLU)),
                  pl.BlockSpec((32, D), lambda b, t: (0, 0)), vec, vec, vec,
                  pl.BlockSpec((D, D), lambda b, t: (0, 0)),
                  pl.BlockSpec((D, D), lambda b, t: (1, 0))],
        out_specs=(out_tile, out_tile, pl.BlockSpec((tm, 1), lambda b, t: (row(b, t), 0)), out_tile),
        out_shape=(jax.ShapeDtypeStruct((T, D), BF16), jax.ShapeDtypeStruct((T, D), BF16),
                   jax.ShapeDtypeStruct((T, 1), F32), jax.ShapeDtypeStruct((T, D), BF16)),
        scratch_shapes=[pltpu.VMEM((tm + HALO + 8, D), F32), pltpu.VMEM((tm, D), F32)],
        compiler_params=_params(("arbitrary", "arbitrary")),
    )(z, z, z, z, z, wdw, b_dw, ln_g, ln_b, wall, wall)


def _swap_matrix():
    r = lax.broadcasted_iota(jnp.int32, (BLOCK, BLOCK), 0)
    l = lax.broadcasted_iota(jnp.int32, (BLOCK, BLOCK), 1)
    lh = l & (HEAD_DIM - 1)
    half = ROPE_DIM // 2
    hit = ((lh < half) & (r == l + half)) | ((lh >= half) & (lh < ROPE_DIM) & (r == l - half))
    return jnp.where(hit, 1.0, 0.0).astype(BF16)


def _rope(tb, cos, sin, pswap):
    return tb.astype(F32) * cos + _mm(tb, pswap) * sin


def _rope_f32(tv, cos, sin, pswap):
    hi = tv.astype(BF16)
    lo = (tv - hi.astype(F32)).astype(BF16)
    return tv * cos + (_mm(hi, pswap) + _mm(lo, pswap)) * sin


def _kv_variants(kv):
    lane = lax.broadcasted_iota(jnp.int32, kv.shape, 1)
    lo = lane < HEAD_DIM
    sw = pltpu.roll(kv, HEAD_DIM, 1)
    z = jnp.zeros_like(kv)
    g0 = (jnp.where(lo, kv, z).astype(BF16), jnp.where(lo, z, sw).astype(BF16))
    g1 = (jnp.where(lo, sw, z).astype(BF16), jnp.where(lo, z, kv).astype(BF16))
    return (g0, g1)


def _band_mask(nq):
    qi = lax.broadcasted_iota(jnp.int32, (nq * BLOCK, 2 * BLOCK), 0) & (BLOCK - 1)
    sj = lax.broadcasted_iota(jnp.int32, (nq * BLOCK, 2 * BLOCK), 1)
    return (sj <= qi + BLOCK) & (sj > qi), sj


def _sink_rep(sink_ref, g, e):
    return jnp.concatenate(
        [jnp.full((BLOCK, BLOCK), sink_ref[8 * g + 2 * j + e], F32) for j in range(4)], axis=0)


def _softmax_parts(s, valid, sk):
    rows = s.shape[0]
    s = jnp.where(valid, s, -1e30)
    m = jnp.maximum(jnp.broadcast_to(jnp.max(s, axis=-1, keepdims=True), (rows, BLOCK)), sk)
    return jnp.exp(s - jnp.concatenate([m, m], axis=1)), jnp.exp(sk - m)


def _softmax_sink(s, valid, sk):
    p, ps = _softmax_parts(s, valid, sk)
    inv = 1.0 / (_mm(p.astype(BF16), jnp.ones((2 * BLOCK, BLOCK), BF16)) + ps)
    return p * jnp.concatenate([inv, inv], axis=1), ps * inv


def _attn_fwd(z, zkv, cos_t, sin_t, sinks, S, tq):
    T = z.shape[0]
    nt = S // tq
    nq = tq // BLOCK

    def body(sink_ref, q_ref, kv_ref, hkv_ref, cos_ref, sin_ref, hcos_ref, hsin_ref, o_ref):
        t = pl.program_id(1)
        cos = cos_ref[...]
        sin = sin_ref[...]
        pswap = _swap_matrix()
        kv = jnp.concatenate([hkv_ref[...], kv_ref[...]], axis=0)
        cos_k = jnp.concatenate([hcos_ref[...], cos], axis=0)
        sin_k = jnp.concatenate([hsin_ref[...], sin], axis=0)
        kx = _kv_variants(_rope(kv[:, :BLOCK], cos_k, sin_k, pswap))
        one = jnp.ones((tq + BLOCK, BLOCK), BF16)
        vx = [[jnp.concatenate([v, one], axis=1) for v in vg] for vg in _kv_variants(kv[:, BLOCK:].astype(F32))]
        band, sj = _band_mask(4)
        qs = [(_rope(q_ref[:, 128 * hp:128 * hp + 128], cos, sin, pswap) * 0.125).astype(BF16)
              for hp in range(8)]
        for n in range(nq):
            first = (t == 0) & (n == 0)
            valid = band & (jnp.logical_not(first) | (sj >= BLOCK))
            r0 = n * BLOCK
            for g in range(2):
                lhs = jnp.concatenate([qs[4 * g + j][r0:r0 + BLOCK] for j in range(4)], axis=0)
                acc = jnp.zeros((4 * BLOCK, BLOCK), F32)
                for e in range(2):
                    s = _mm_nt(lhs, kx[g][e][r0:r0 + 2 * BLOCK])
                    p, ps = _softmax_parts(s, valid, _sink_rep(sink_ref, g, e))
                    r = _mm(p.astype(BF16), vx[g][e][r0:r0 + 2 * BLOCK])
                    acc = acc + r[:, 0:BLOCK] * (1.0 / (r[:, BLOCK:2 * BLOCK] + ps))
                for j in range(4):
                    o_ref[r0:r0 + BLOCK, 128 * (4 * g + j):128 * (4 * g + j) + 128] = (
                        acc[j * BLOCK:(j + 1) * BLOCK].astype(BF16))

    def row(b, t):
        return b * nt + t

    def halo(b, t):
        return jnp.maximum(row(b, t) * nq - 1, 0)

    return pl.pallas_call(
        body, name="attn_fwd", grid=(T // S, nt),
        in_specs=[pl.BlockSpec(memory_space=pltpu.SMEM),
                  pl.BlockSpec((tq, D), lambda b, t: (row(b, t), ZB_Q)),
                  pl.BlockSpec((tq, 2 * BLOCK), lambda b, t: (row(b, t), 0)),
                  pl.BlockSpec((BLOCK, 2 * BLOCK), lambda b, t: (halo(b, t), 0)),
                  pl.BlockSpec((tq, BLOCK), lambda b, t: (row(b, t), 0)),
                  pl.BlockSpec((tq, BLOCK), lambda b, t: (row(b, t), 0)),
                  pl.BlockSpec((BLOCK, BLOCK), lambda b, t: (halo(b, t), 0)),
                  pl.BlockSpec((BLOCK, BLOCK), lambda b, t: (halo(b, t), 0))],
        out_specs=pl.BlockSpec((tq, D), lambda b, t: (row(b, t), 0)),
        out_shape=jax.ShapeDtypeStruct((T, D), BF16),
        compiler_params=_params(("arbitrary", "arbitrary")),
    )(sinks, z, zkv, zkv, cos_t, sin_t, cos_t, sin_t)


def _tail_a(x, tgt, p, o, ya, z, ln_post, wall, wppt, tm):
    T = x.shape[0]
    last = T // tm - 1

    def body(x_ref, tgt_ref, p_ref, o_ref, ya_ref, ag_ref, gc_ref, ga_ref, lnp_ref, wbra_ref, wout_ref,
             wpg_ref, wppt_ref, loss_ref, dx1_ref, dm_ref, yb_ref, glnp_ref, gpack_ref, gwpp_ref,
             acc_out, acc_pg, sem):
        i = pl.program_id(0)

        @pl.when(i == 0)
        def _():
            acc_out[...] = jnp.zeros_like(acc_out)
            acc_pg[...] = jnp.zeros_like(acc_pg)
            gwpp_ref[...] = jnp.zeros_like(gwpp_ref)
            glnp_ref[...] = jnp.zeros_like(glnp_ref)
            loss_ref[...] = jnp.zeros_like(loss_ref)

        ag = ag_ref[...].astype(F32)
        yb_in = (o_ref[...].astype(F32) * (ag * _sig(ag))).astype(BF16)
        yb = _mm(yb_in, wbra_ref[...])
        yb_ref[...] = yb.astype(BF16)
        m = (_sig(gc_ref[...].astype(F32)) * ya_ref[...].astype(F32)
             + _sig(ga_ref[...].astype(F32)) * yb).astype(BF16)
        mo = _mm(m, wout_ref[...])
        r2 = lax.rsqrt(jnp.mean(mo * mo, axis=-1, keepdims=True) + EPS)
        nrm = mo * r2
        g_post = lnp_ref[...]
        x1 = x_ref[...] + nrm * g_post
        x1b = x1.astype(BF16)
        gate = _sig(_mm(x1b, wpg_ref[...]))
        pb = p_ref[...].astype(BF16)
        pp = _mm_nt(pb, wppt_ref[...])
        err = x1 + gate * pp - tgt_ref[...]
        loss_ref[...] += 0.5 * jnp.sum(jnp.sum(err * err, axis=-1, keepdims=True) * (1.0 / D),
                                       axis=0, keepdims=True)
        dx2 = err * (1.0 / D)
        dgp = (dx2 * pp * gate * (1.0 - gate)).astype(BF16)
        dpp = (dx2 * gate).astype(BF16)
        dx1 = dx2 + _mm_nt(dgp, wpg_ref[...])
        dx1_ref[...] = dx1
        acc_pg[...] += _mm_tn(x1b, dgp)
        gwpp_ref[...] += _mm_tn(dpp, pb)
        glnp_ref[...] += jnp.sum(dx1 * nrm, axis=0, keepdims=True)
        a = dx1 * g_post
        dmo = (r2 * (a - nrm * jnp.mean(a * nrm, axis=-1, keepdims=True))).astype(BF16)
        dm_ref[...] = _mm_nt(dmo, wout_ref[...]).astype(BF16)
        acc_out[...] += _mm_tn(m, dmo)

        @pl.when(i == last)
        def _():
            _flush_to_pack(acc_out, gpack_ref, 3 * D, sem.at[0])
            _flush_to_pack(acc_pg, gpack_ref, 4 * D, sem.at[1])

    tile = pl.BlockSpec((tm, D), lambda i: (i, 0))
    ztile = lambda j: pl.BlockSpec((tm, D), lambda i: (i, j))
    wsq = lambda k: pl.BlockSpec((D, D), lambda i: (k, 0))
    const = lambda shp: pl.BlockSpec(shp, lambda i: (0, 0))
    any_spec = pl.BlockSpec(memory_space=pl.ANY)
    return pl.pallas_call(
        body, name="tail_a", grid=(T // tm,),
        in_specs=[tile, tile, pl.BlockSpec((tm, PLE), lambda i: (i, 0)), tile, tile, ztile(ZB_AGATE),
                  ztile(ZB_GCONV), ztile(ZB_GATTN), const((1, D)), wsq(2), wsq(3), wsq(4), const((D, PLE))],
        out_specs=(const((1, 1)), tile, tile, tile, const((1, D)), any_spec, const((D, PLE))),
        out_shape=(jax.ShapeDtypeStruct((1, 1), F32), jax.ShapeDtypeStruct((T, D), F32),
                   jax.ShapeDtypeStruct((T, D), BF16), jax.ShapeDtypeStruct((T, D), BF16),
                   jax.ShapeDtypeStruct((1, D), F32), jax.ShapeDtypeStruct((N_SHARDS, SQ_PACK, D), F32),
                   jax.ShapeDtypeStruct((D, PLE), F32)),
        scratch_shapes=[pltpu.VMEM((D, D), F32), pltpu.VMEM((D, D), F32), pltpu.SemaphoreType.DMA((2,))],
        compiler_params=_params(("arbitrary",)),
    )(x, tgt, p, o, ya, z, z, z, ln_post, wall, wall, wall, wppt)


def _dsilu(v, sg):
    return sg * (1.0 + v * (1.0 - sg))


def _tail_b(dm, ya, yb, o, z, pw, y, rstd, ln_g, ln_b, wall, gppt, gpack, tm):
    T = dm.shape[0]
    last = T // tm - 1

    def body(dm_ref, ya_ref, yb_ref, o_ref, ag_ref, gc_ref, ga_ref, cgate_ref, pw_ref, y_ref, rstd_ref,
             lng_ref, lnb_ref, wpw_ref, wbrc_ref, wbra_ref, gppt_ref, gpack_in, dg_ref, do_ref, dc_ref,
             gvec_ref, gpack_ref, acc_bra, acc_brc, acc_pw, sem):
        i = pl.program_id(0)

        @pl.when(i == 0)
        def _():
            acc_bra[...] = jnp.zeros_like(acc_bra)
            acc_brc[...] = jnp.zeros_like(acc_brc)
            acc_pw[...] = jnp.zeros_like(acc_pw)
            gvec_ref[...] = jnp.zeros_like(gvec_ref)

        g = lng_ref[...]

        def part(rs):
            dm_v = dm_ref[rs, :].astype(F32)
            sgc = _sig(gc_ref[rs, :].astype(F32))
            sga = _sig(ga_ref[rs, :].astype(F32))
            dya = (dm_v * sgc).astype(BF16)
            dyb = (dm_v * sga).astype(BF16)
            dg_ref[rs, D:2 * D] = (dm_v * ya_ref[rs, :].astype(F32) * sgc * (1.0 - sgc)).astype(BF16)
            dg_ref[rs, 2 * D:3 * D] = (dm_v * yb_ref[rs, :].astype(F32) * sga * (1.0 - sga)).astype(BF16)
            ag = ag_ref[rs, :].astype(F32)
            sag = _sig(ag)
            sa = ag * sag
            ov = o_ref[rs, :].astype(F32)
            dyb_in = _mm_nt(dyb, wbra_ref[...])
            do_ref[rs, :] = (dyb_in * sa).astype(BF16)
            dg_ref[rs, 0:D] = (dyb_in * ov * _dsilu(ag, sag)).astype(BF16)
            gt = cgate_ref[rs, :].astype(F32)
            sgt = _sig(gt)
            sgate = gt * sgt
            pw = pw_ref[rs, :].astype(F32)
            dya_in = _mm_nt(dya, wbrc_ref[...])
            dpw = (dya_in * sgate).astype(BF16)
            dg_ref[rs, 3 * D:4 * D] = (dya_in * pw * _dsilu(gt, sgt)).astype(BF16)
            yn = y_ref[rs, :].astype(F32)
            n = yn * g + lnb_ref[...]
            sn = _sig(n)
            dn = _mm_nt(dpw, wpw_ref[...]) * _dsilu(n, sn)
            dy = dn * g
            dc = rstd_ref[rs, :] * (dy - jnp.mean(dy, axis=-1, keepdims=True)
                                    - yn * jnp.mean(dy * yn, axis=-1, keepdims=True))
            dc_ref[rs, :] = dc.astype(BF16)
            sums = (jnp.sum(dn * yn, axis=0, keepdims=True), jnp.sum(dn, axis=0, keepdims=True),
                    jnp.sum(dc, axis=0, keepdims=True))
            return ((ov * sa).astype(BF16), dyb, (pw * sgate).astype(BF16), dya, (n * sn).astype(BF16), dpw,
                    sums)

        parts = [part(pl.ds(r * (tm // TAIL_PARTS), tm // TAIL_PARTS)) for r in range(TAIL_PARTS)]
        cat = lambda j: jnp.concatenate([pt[j] for pt in parts], axis=0)
        acc_bra[...] += _mm_tn(cat(0), cat(1))
        acc_brc[...] += _mm_tn(cat(2), cat(3))
        acc_pw[...] += _mm_tn(cat(4), cat(5))
        for j in range(3):
            gvec_ref[j:j + 1, :] += sum(pt[6][j] for pt in parts)

        @pl.when(i == last)
        def _():
            _flush_to_pack(acc_pw, gpack_ref, 0, sem.at[0])
            _flush_to_pack(acc_brc, gpack_ref, D, sem.at[1])
            _flush_to_pack(acc_bra, gpack_ref, 2 * D, sem.at[2])
            _flush_to_pack(gppt_ref, gpack_ref, WPP0, sem.at[0])

    tile = pl.BlockSpec((tm, D), lambda i: (i, 0))
    ztile = lambda j: pl.BlockSpec((tm, D), lambda i: (i, j))
    wsq = lambda k: pl.BlockSpec((D, D), lambda i: (k, 0))
    const = lambda shp: pl.BlockSpec(shp, lambda i: (0, 0))
    any_spec = pl.BlockSpec(memory_space=pl.ANY)
    return pl.pallas_call(
        body, name="tail_b", grid=(T // tm,),
        in_specs=[tile, tile, tile, tile, ztile(ZB_AGATE), ztile(ZB_GCONV), ztile(ZB_GATTN), ztile(ZB_CGATE),
                  tile, tile, pl.BlockSpec((tm, 1), lambda i: (i, 0)), const((1, D)), const((1, D)), wsq(0),
                  wsq(1), wsq(2), const((PLE, D)), any_spec],
        out_specs=(pl.BlockSpec((tm, 4 * D), lambda i: (i, 0)), tile, tile, const((8, D)), any_spec),
        out_shape=(jax.ShapeDtypeStruct((T, 7 * D), BF16), jax.ShapeDtypeStruct((T, D), BF16),
                   jax.ShapeDtypeStruct((T, D), BF16), jax.ShapeDtypeStruct((8, D), F32),
                   jax.ShapeDtypeStruct(gpack.shape, F32)),
        input_output_aliases={17: 4},
        scratch_shapes=[pltpu.VMEM((D, D), F32), pltpu.VMEM((D, D), F32), pltpu.VMEM((D, D), F32),
                        pltpu.SemaphoreType.DMA((3,))],
        compiler_params=_params(("arbitrary",)),
    )(dm, ya, yb, o, z, z, z, z, pw, y, rstd, ln_g, ln_b, wall, wall, wall, gppt, gpack)


def _conv_bwd(dc, z, wdw, dz, S, tm, copies, src, landing):
    T = dc.shape[0]
    nt = S // tm
    hb = tm // HALO
    nrows = T // HALO

    def body(dc_ref, hdc_ref, cv_ref, cg_ref, hcv_ref, hcg_ref, wdw_ref, dz_in, src_ref, dz_ref, gw_ref,
             land_ref, ubuf, dcbuf, dubuf, dwacc, shbuf, send_sems, recv_sems):
        b = pl.program_id(0)
        t = pl.program_id(1)

        @pl.when((b == 0) & (t == 0))
        def _():
            dwacc[...] = jnp.zeros_like(dwacc)
            for cp in copies(src_ref, land_ref, send_sems, recv_sems):
                cp.start()

        cv = cv_ref[...].astype(F32)
        sg = _sig(cg_ref[...].astype(F32))
        ubuf[HALO:HALO + tm, :] = cv * sg
        hu = hcv_ref[...].astype(F32) * _sig(hcg_ref[...].astype(F32))
        ubuf[0:HALO, :] = jnp.where(t > 0, hu, 0.0)
        ubuf[HALO + tm:HALO + tm + 8, :] = jnp.zeros((8, D), F32)
        dcbuf[0:tm, :] = dc_ref[...].astype(F32)
        dcbuf[tm:tm + HALO, :] = jnp.where(t < nt - 1, hdc_ref[...].astype(F32), 0.0)
        dcbuf[tm + HALO:tm + HALO + 8, :] = jnp.zeros((8, D), F32)

        def chunk(ci, carry):
            r0 = pl.multiple_of(ci * CONV_RC, CONV_RC)
            for lg in range(D // CONV_LC):
                l0 = lg * CONV_LC
                dubuf[pl.ds(r0, CONV_RC), pl.ds(l0, CONV_LC)] = _conv_taps(
                    wdw_ref, dcbuf, r0, l0, lambda k: CONV_K - 1 - k)
                dcc = dcbuf[pl.ds(r0, CONV_RC), pl.ds(l0, CONV_LC)]
                zero8 = jnp.zeros((8, CONV_LC), F32)
                dcz = jnp.concatenate([zero8, dcc, zero8], axis=0)
                for bb in range(8):
                    taps = [k for k in range(CONV_K) if (HALO - (CONV_K - 1) + k) % 8 == bb]
                    if not taps:
                        continue
                    rows = CONV_RC + (8 if bb else 0)
                    if bb:
                        shbuf[bb] = dcz[8 - bb:8 - bb + rows]
                    for k in taps:
                        a8 = HALO - (CONV_K - 1) + k - bb
                        dcs = shbuf[bb] if bb else dcc
                        prod = dcs * ubuf[pl.ds(r0 + a8, rows), pl.ds(l0, CONV_LC)]
                        part = prod[0:8]
                        for q in range(1, rows // 8):
                            part = part + prod[8 * q:8 * q + 8]
                        dwacc[8 * k:8 * k + 8, pl.ds(l0, CONV_LC)] += part
            return carry

        lax.fori_loop(0, tm // CONV_RC, chunk, 0)
        du = dubuf[...]
        dz_ref[:, 0:D] = (du * sg).astype(BF16)
        dz_ref[:, D:2 * D] = (du * cv * sg * (1.0 - sg)).astype(BF16)

        @pl.when((b == pl.num_programs(0) - 1) & (t == nt - 1))
        def _():
            for k in range(32):
                gw_ref[k:k + 1, :] = jnp.sum(dwacc[8 * k:8 * k + 8, :], axis=0, keepdims=True)
            cps = copies(src_ref, land_ref, send_sems, recv_sems)
            for cp in cps:
                cp.wait_recv()
            for cp in cps:
                cp.wait_send()

    def row(b, t):
        return b * nt + t

    def prev_halo(b, t):
        return jnp.maximum(row(b, t) * hb - 1, 0)

    def next_halo(b, t):
        return jnp.minimum((row(b, t) + 1) * hb, nrows - 1)

    return pl.pallas_call(
        body, name="conv_bwd", grid=(T // S, nt),
        in_specs=[pl.BlockSpec((tm, D), lambda b, t: (row(b, t), 0)),
                  pl.BlockSpec((HALO, D), lambda b, t: (next_halo(b, t), 0)),
                  pl.BlockSpec((tm, D), lambda b, t: (row(b, t), ZB_CVAL)),
                  pl.BlockSpec((tm, D), lambda b, t: (row(b, t), ZB_CGLU)),
                  pl.BlockSpec((HALO, D), lambda b, t: (prev_halo(b, t), ZB_CVAL)),
                  pl.BlockSpec((HALO, D), lambda b, t: (prev_halo(b, t), ZB_CGLU)),
                  pl.BlockSpec((32, D), lambda b, t: (0, 0)),
                  pl.BlockSpec(memory_space=pl.ANY), pl.BlockSpec(memory_space=pl.ANY)],
        out_specs=(pl.BlockSpec((tm, 2 * D), lambda b, t: (row(b, t), ZB_CVAL // 2)),
                   pl.BlockSpec((32, D), lambda b, t: (0, 0)), pl.BlockSpec(memory_space=pl.ANY)),
        out_shape=(jax.ShapeDtypeStruct(dz.shape, BF16), jax.ShapeDtypeStruct((32, D), F32), landing),
        input_output_aliases={7: 0},
        scratch_shapes=[pltpu.VMEM((tm + HALO + 8, D), F32), pltpu.VMEM((tm + HALO + 8, D), F32),
                        pltpu.VMEM((tm, D), F32), pltpu.VMEM((8 * 32, D), F32),
                        pltpu.VMEM((8, CONV_RC + 8, CONV_LC), F32), pltpu.SemaphoreType.DMA((3,)),
                        pltpu.SemaphoreType.DMA((3,))],
        compiler_params=_params(("arbitrary", "arbitrary")),
    )(dc, dc, z, z, z, z, wdw, dz, src)


def _attn_bwd(z, zkv, o, do, cos_t, sin_t, sinks, dz, S, tq, copies, src, landing):
    T = z.shape[0]
    nt = S // tq
    nq = tq // BLOCK

    def body(sink_ref, q_ref, kv_ref, hkv_ref, o_ref, do_ref, cos_ref, sin_ref, hcos_ref, hsin_ref, dz_in,
             src_ref, dq_ref, dkv_ref, gs_ref, land_ref, carry, dkacc, dvacc, send_sems, recv_sems):
        b = pl.program_id(0)
        tt = pl.program_id(1)
        t = nt - 1 - tt

        @pl.when((b == 0) & (tt == 0))
        def _():
            gs_ref[...] = jnp.zeros_like(gs_ref)
            for cp in copies(src_ref, land_ref, send_sems, recv_sems):
                cp.start()

        @pl.when(tt == 0)
        def _():
            carry[...] = jnp.zeros_like(carry)

        cos = cos_ref[...]
        sin = sin_ref[...]
        pswap = _swap_matrix()
        kv = jnp.concatenate([hkv_ref[...], kv_ref[...]], axis=0)
        cos_k = jnp.concatenate([hcos_ref[...], cos], axis=0)
        sin_k = jnp.concatenate([hsin_ref[...], sin], axis=0)
        kx = _kv_variants(_rope(kv[:, :BLOCK], cos_k, sin_k, pswap))
        vx = _kv_variants(kv[:, BLOCK:].astype(F32))
        band, sj = _band_mask(4)
        lo = lax.broadcasted_iota(jnp.int32, (4 * BLOCK, BLOCK), 1) < HEAD_DIM
        ones = jnp.ones((2 * BLOCK, 2 * BLOCK), BF16)
        qs = [(_rope(q_ref[:, 128 * hp:128 * hp + 128], cos, sin, pswap) * 0.125).astype(BF16)
              for hp in range(8)]
        dkacc[...] = jnp.zeros_like(dkacc)
        dvacc[...] = jnp.zeros_like(dvacc)
        gsum = jnp.zeros((1, BLOCK), F32)
        hlane = lax.broadcasted_iota(jnp.int32, (1, BLOCK), 1)
        for n in range(nq):
            first = (t == 0) & (n == 0)
            valid = band & (jnp.logical_not(first) | (sj >= BLOCK))
            r0 = n * BLOCK
            for g in range(2):
                cols = [slice(128 * (4 * g + j), 128 * (4 * g + j) + 128) for j in range(4)]
                lhs = jnp.concatenate([qs[4 * g + j][r0:r0 + BLOCK] for j in range(4)], axis=0)
                dov = jnp.concatenate([do_ref[r0:r0 + BLOCK, cs] for cs in cols], axis=0)
                prod = dov.astype(F32) * jnp.concatenate(
                    [o_ref[r0:r0 + BLOCK, cs] for cs in cols], axis=0).astype(F32)
                lhs_t = lhs.T
                dov_t = dov.T
                dq = jnp.zeros((4 * BLOCK, BLOCK), F32)
                dk_t = jnp.zeros((HEAD_DIM, 2 * BLOCK), F32)
                dv_t = jnp.zeros((HEAD_DIM, 2 * BLOCK), F32)
                for e in range(2):
                    kw = kx[g][e][r0:r0 + 2 * BLOCK]
                    vw = vx[g][e][r0:r0 + 2 * BLOCK]
                    s = _mm_nt(lhs, kw)
                    p, psink = _softmax_sink(s, valid, _sink_rep(sink_ref, g, e))
                    pe = jnp.where(lo if e == 0 else jnp.logical_not(lo), prod, 0.0)
                    pe_hi = pe.astype(BF16)
                    pe_lo = (pe - pe_hi.astype(F32)).astype(BF16)
                    delta = _mm(jnp.concatenate([pe_hi, pe_lo], axis=1), ones)
                    ds = (p * (_mm_nt(dov, vw) - delta)).astype(BF16)
                    dq = dq + _mm(ds, kw)
                    dims = slice(HEAD_DIM * e, HEAD_DIM * (e + 1))
                    dk_t = dk_t + _mm(lhs_t[dims], ds)
                    dv_t = dv_t + _mm(dov_t[dims], p.astype(BF16))
                    gs = -psink * delta[:, 0:BLOCK]
                    for j in range(4):
                        tot = jnp.sum(gs[j * BLOCK:(j + 1) * BLOCK], axis=0, keepdims=True)
                        gsum = gsum + jnp.where(hlane == 8 * g + 2 * j + e, tot, 0.0)
                dkacc[HEAD_DIM * g:HEAD_DIM * (g + 1), r0:r0 + 2 * BLOCK] += dk_t
                dvacc[HEAD_DIM * g:HEAD_DIM * (g + 1), r0:r0 + 2 * BLOCK] += dv_t
                for j in range(4):
                    dqj = _rope_f32(dq[j * BLOCK:(j + 1) * BLOCK] * 0.125, cos[r0:r0 + BLOCK],
                                    -sin[r0:r0 + BLOCK], pswap)
                    dq_ref[r0:r0 + BLOCK, cols[j]] = dqj.astype(BF16)
        gs_ref[0:1, :] += gsum
        dk_all = dkacc[...]
        dv_all = dvacc[...]
        dk_last = dk_all[:, tq:tq + BLOCK] + carry[0:BLOCK, :]
        dv_last = dv_all[:, tq:tq + BLOCK] + carry[BLOCK:2 * BLOCK, :]
        carry[0:BLOCK, :] = dk_all[:, 0:BLOCK]
        carry[BLOCK:2 * BLOCK, :] = dv_all[:, 0:BLOCK]
        if nq > 1:
            dk_tile = jnp.concatenate([dk_all[:, BLOCK:tq], dk_last], axis=1)
            dv_tile = jnp.concatenate([dv_all[:, BLOCK:tq], dv_last], axis=1)
        else:
            dk_tile, dv_tile = dk_last, dv_last
        dkv_ref[:, 0:BLOCK] = _rope_f32(dk_tile.T, cos, -sin, pswap).astype(BF16)
        dkv_ref[:, BLOCK:2 * BLOCK] = dv_tile.T.astype(BF16)

        @pl.when((b == pl.num_programs(0) - 1) & (tt == nt - 1))
        def _():
            cps = copies(src_ref, land_ref, send_sems, recv_sems)
            for cp in cps:
                cp.wait_recv()
            for cp in cps:
                cp.wait_send()

    def row(b, tt):
        return b * nt + (nt - 1 - tt)

    def halo(b, tt):
        return jnp.maximum(row(b, tt) * nq - 1, 0)

    tile = pl.BlockSpec((tq, D), lambda b, tt: (row(b, tt), 0))
    return pl.pallas_call(
        body, name="attn_bwd", grid=(T // S, nt),
        in_specs=[pl.BlockSpec(memory_space=pltpu.SMEM),
                  pl.BlockSpec((tq, D), lambda b, tt: (row(b, tt), ZB_Q)),
                  pl.BlockSpec((tq, 2 * BLOCK), lambda b, tt: (row(b, tt), 0)),
                  pl.BlockSpec((BLOCK, 2 * BLOCK), lambda b, tt: (halo(b, tt), 0)),
                  tile, tile,
                  pl.BlockSpec((tq, BLOCK), lambda b, tt: (row(b, tt), 0)),
                  pl.BlockSpec((tq, BLOCK), lambda b, tt: (row(b, tt), 0)),
                  pl.BlockSpec((BLOCK, BLOCK), lambda b, tt: (halo(b, tt), 0)),
                  pl.BlockSpec((BLOCK, BLOCK), lambda b, tt: (halo(b, tt), 0)),
                  pl.BlockSpec(memory_space=pl.ANY), pl.BlockSpec(memory_space=pl.ANY)],
        out_specs=(pl.BlockSpec((tq, D), lambda b, tt: (row(b, tt), ZB_Q)),
                   pl.BlockSpec((tq, 2 * BLOCK), lambda b, tt: (row(b, tt), 0)),
                   pl.BlockSpec((8, BLOCK), lambda b, tt: (0, 0)), pl.BlockSpec(memory_space=pl.ANY)),
        out_shape=(jax.ShapeDtypeStruct(dz.shape, BF16), jax.ShapeDtypeStruct((T, 2 * BLOCK), BF16),
                   jax.ShapeDtypeStruct((8, BLOCK), F32), landing),
        input_output_aliases={10: 0},
        scratch_shapes=[pltpu.VMEM((2 * BLOCK, BLOCK), F32), pltpu.VMEM((BLOCK, tq + BLOCK), F32),
                        pltpu.VMEM((BLOCK, tq + BLOCK), F32), pltpu.SemaphoreType.DMA((3,)),
                        pltpu.SemaphoreType.DMA((3,))],
        compiler_params=_params(("arbitrary", "arbitrary")),
    )(sinks, z, zkv, zkv, o, do, cos_t, sin_t, cos_t, sin_t, dz, src)


def _dh(dz, dz_kv, wall, x, dx1, ln_pre, tm, tile0, ntiles, gx_prev, name, copies, src, landing):
    T = x.shape[0]
    nsem = 3

    def body(*refs):
        dz_ref, kv_ref, wt_ref, x_ref, dx1_ref, g_ref, src_ref = refs[:7]
        gx_ref, glp_ref, land_ref, wbuf, send_sems, recv_sems, wsem = refs[-7:]
        i = pl.program_id(0)

        @pl.when(i == 0)
        def _():
            glp_ref[...] = jnp.zeros_like(glp_ref)
            for cp in copies(src_ref, land_ref, send_sems, recv_sems):
                cp.start()
            load = pltpu.make_async_copy(wt_ref, wbuf, wsem)
            load.start()
            load.wait()

        dh = _mm(dz_ref[...], wbuf[0:ZKV, :]) + _mm(kv_ref[...], wbuf[ZKV:IN_WIDTH, :])
        xv = x_ref[...]
        r = lax.rsqrt(jnp.mean(xv * xv, axis=-1, keepdims=True) + EPS)
        xr = xv * r
        glp_ref[...] += jnp.sum(dh * xr, axis=0, keepdims=True)
        a = dh * g_ref[...]
        gx_ref[...] = dx1_ref[...] + r * (a - xr * jnp.mean(a * xr, axis=-1, keepdims=True))

        @pl.when(i == ntiles - 1)
        def _():
            cps = copies(src_ref, land_ref, send_sems, recv_sems)
            for cp in cps:
                cp.wait_recv()
            for cp in cps:
                cp.wait_send()

    tile = pl.BlockSpec((tm, D), lambda i: (tile0 + i, 0))
    any_spec = pl.BlockSpec(memory_space=pl.ANY)
    operands = [dz, dz_kv, wall, x, dx1, ln_pre, src] + ([] if gx_prev is None else [gx_prev])
    return pl.pallas_call(
        body, name=name, grid=(ntiles,),
        in_specs=[pl.BlockSpec((tm, ZKV), lambda i: (tile0 + i, 0)),
                  pl.BlockSpec((tm, 2 * BLOCK), lambda i: (tile0 + i, 0)),
                  any_spec, tile, tile, pl.BlockSpec((1, D), lambda i: (0, 0)), any_spec]
        + ([] if gx_prev is None else [any_spec]),
        out_specs=(tile, pl.BlockSpec((1, D), lambda i: (0, 0)), any_spec),
        out_shape=(jax.ShapeDtypeStruct((T, D), F32), jax.ShapeDtypeStruct((1, D), F32), landing),
        input_output_aliases={} if gx_prev is None else {7: 0},
        scratch_shapes=[pltpu.VMEM((IN_WIDTH, D), BF16), pltpu.SemaphoreType.DMA((nsem,)),
                        pltpu.SemaphoreType.DMA((nsem,)), pltpu.SemaphoreType.DMA],
        compiler_params=_params(("arbitrary",)),
    )(*operands)


def _gwt(dz, dz_kv, h, tt):
    T = dz.shape[0]
    nt = T // tt
    last = nt - 1
    kv = 2 * BLOCK

    def body(dz_ref, dzkv_ref, h_ref, gpack_ref, acc, sem):
        j = pl.program_id(0)
        t = pl.program_id(1)

        @pl.when((j < 7) & (t == 0))
        def _():
            acc[...] = _mm_tn(dz_ref[...], h_ref[...])

        @pl.when((j < 7) & (t > 0))
        def _():
            acc[...] += _mm_tn(dz_ref[...], h_ref[...])

        @pl.when((j == 7) & (t == 0))
        def _():
            acc[0:kv, :] = _mm_tn(dzkv_ref[...], h_ref[...])

        @pl.when((j == 7) & (t > 0))
        def _():
            acc[0:kv, :] += _mm_tn(dzkv_ref[...], h_ref[...])

        for jj in range(7):
            @pl.when((t == last) & (j == jj))
            def _(jj=jj):
                _flush_to_pack(acc, gpack_ref, WT0 + jj * D, sem)

        @pl.when((t == last) & (j == 7))
        def _():
            _flush_to_pack(acc.at[pl.ds(0, kv)], gpack_ref, WT0 + ZKV, sem)

    return pl.pallas_call(
        body, name="gwt", grid=(8, nt),
        in_specs=[pl.BlockSpec((tt, D), lambda j, t: (jnp.where(j == 7, last, t), jnp.minimum(j, 6))),
                  pl.BlockSpec((tt, kv), lambda j, t: (jnp.where(j == 7, t, 0), 0)),
                  pl.BlockSpec((tt, D), lambda j, t: (t, 0))],
        out_specs=pl.BlockSpec(memory_space=pl.ANY),
        out_shape=jax.ShapeDtypeStruct((N_SHARDS, WIN_SHARD, D), F32),
        scratch_shapes=[pltpu.VMEM((D, D), F32), pltpu.SemaphoreType.DMA],
        compiler_params=_params(("arbitrary", "arbitrary")),
    )(dz, dz_kv, h)


_BC1 = 1.0 - ADAM_B1 ** ADAM_STEP
_BC2 = 1.0 - ADAM_B2 ** ADAM_STEP


def _adamw_math(w, g, m, v):
    m = ADAM_B1 * m + (1.0 - ADAM_B1) * g
    v = ADAM_B2 * v + (1.0 - ADAM_B2) * (g * g)
    delta = -ADAM_LR * ((m / _BC1) / (jnp.sqrt(v / _BC2) + ADAM_EPS) + ADAM_WD * w)
    return delta, m, v


def _adamw_rows(g, w, m, v, rows, name):
    R, C = w.shape

    def body(g_ref, w_ref, m_ref, v_ref, go_ref, d_ref, nm_ref, nv_ref):
        gv = g_ref[...]
        d, nm, nv = _adamw_math(w_ref[...], gv, m_ref[...], v_ref[...])
        go_ref[...] = gv
        d_ref[...] = d
        nm_ref[...] = nm
        nv_ref[...] = nv

    spec = pl.BlockSpec((rows, C), lambda i: (i, 0))
    shp = jax.ShapeDtypeStruct((R, C), F32)
    return pl.pallas_call(
        body, name=name, grid=(R // rows,), in_specs=[spec] * 4, out_specs=(spec,) * 4,
        out_shape=(shp,) * 4, compiler_params=_params(("arbitrary",)),
    )(g, w, m, v)


def _adamw_square(gfin, ws, ms, vs):
    rb = 64
    nb = SQ_SHARD // rb

    def body(*refs):
        g_refs = refs[0:5]
        w_refs, m_refs, v_refs = refs[5:10], refs[10:15], refs[15:20]
        outs = refs[20:]
        for k in range(5):
            gk = g_refs[k][...]
            d, nm, nv = _adamw_math(w_refs[k][...], gk, m_refs[k][...], v_refs[k][...])
            outs[4 * k][...] = gk
            outs[4 * k + 1][...] = d
            outs[4 * k + 2][...] = nm
            outs[4 * k + 3][...] = nv

    spec = pl.BlockSpec((rb, D), lambda i: (i, 0))
    gspecs = [pl.BlockSpec((rb, D), lambda i, k=k: (SQ_SHARD * k // rb + i, 0)) for k in range(5)]
    shp = jax.ShapeDtypeStruct((SQ_SHARD, D), F32)
    res = pl.pallas_call(
        body, name="adamw_square", grid=(nb,), in_specs=gspecs + [spec] * 15, out_specs=(spec,) * 20,
        out_shape=(shp,) * 20, compiler_params=_params(("arbitrary",)),
    )(*([gfin] * 5), *ws, *ms, *vs)
    return [tuple(res[4 * k:4 * k + 4]) for k in range(5)]


def _adamw_small(gs, ws, ms, vs):
    n = len(gs)

    def body(*refs):
        outs = refs[4 * n:]
        for k in range(n):
            d, nm, nv = _adamw_math(refs[n + k][...], refs[k][...], refs[2 * n + k][...],
                                    refs[3 * n + k][...])
            outs[3 * k][...] = d
            outs[3 * k + 1][...] = nm
            outs[3 * k + 2][...] = nv

    vm = pl.BlockSpec(memory_space=pltpu.VMEM)
    shapes = []
    for w in ws:
        shapes += [jax.ShapeDtypeStruct(w.shape, F32)] * 3
    res = pl.pallas_call(
        body, name="adamw_small", in_specs=[vm] * (4 * n), out_specs=(vm,) * (3 * n),
        out_shape=tuple(shapes),
    )(*gs, *ws, *ms, *vs)
    return [tuple(res[3 * k:3 * k + 3]) for k in range(n)]


def _rope_lanes():
    inv = jnp.power(ROPE_THETA, -jnp.arange(0, ROPE_DIM, 2, dtype=F32) / ROPE_DIM)
    inv_h = jnp.concatenate([inv, inv, jnp.zeros((HEAD_DIM - ROPE_DIM,), F32)])
    sign_h = np.array([-1.0] * (ROPE_DIM // 2) + [1.0] * (ROPE_DIM // 2) + [0.0] * (HEAD_DIM - ROPE_DIM),
                      np.float32)
    rows = jnp.stack([jnp.concatenate([inv_h, inv_h]), jnp.asarray(np.concatenate([sign_h, sign_h]))])
    return jnp.concatenate([rows, jnp.zeros((6, BLOCK), F32)], axis=0)


def kernel(x, p, positions, w_in, ln_pre, ln_post, w_dw, b_dw, conv_ln_g, conv_ln_b, w_pw, sinks, w_br_conv, w_br_attn, w_out, w_ple_gate, w_ple_proj, loss_target, m_w_in, m_ln_pre, m_ln_post, m_w_dw, m_b_dw, m_conv_ln_g, m_conv_ln_b, m_w_pw, m_sinks, m_w_br_conv, m_w_br_attn, m_w_out, m_w_ple_gate, m_w_ple_proj, v_w_in, v_ln_pre, v_ln_post, v_w_dw, v_b_dw, v_conv_ln_g, v_conv_ln_b, v_w_pw, v_sinks, v_w_br_conv, v_w_br_attn, v_w_out, v_w_ple_gate, v_w_ple_proj):
    nb, S, _ = x.shape
    T = nb * S
    xc = lax.axis_index("x")
    yc = lax.axis_index("y")
    cc = lax.axis_index("c")
    shard = 2 * xc + yc

    sq_w = (w_pw, w_br_conv, w_br_attn, w_out, w_ple_gate)
    wdw_shard = jnp.pad(w_dw[0], ((0, 1), (0, 0)))
    x2 = x.reshape(T, D)
    tm_res = min(TILE_RESIDENT, T // 2)
    h, cos_t, sin_t = _prenorm(x2, ln_pre, positions.astype(F32).reshape(T, 1), _rope_lanes(), tm_res)

    tgt = loss_target.reshape(T, D)
    p2 = p.reshape(T, PLE)
    sinks1 = sinks.reshape(N_HEADS)

    tm = min(TILE_TOKEN, S)
    tq = min(TILE_ATTN, S)

    z, zkv, wt, wdw_all, wall, wppf = _inproj(
        h, w_in[0].T.astype(BF16), wdw_shard, [w[0].astype(BF16) for w in sq_w],
        w_ple_proj[0].T.reshape(WPP_SHARD, D).astype(BF16), min(TILE_PROJ, T // 2))
    wdw = jnp.concatenate([wdw_all[s] for s in range(N_SHARDS)], axis=1)
    wppt = wppf.reshape(D, PLE)
    ya, y, rstd, pw = _conv_fwd(z, wdw, b_dw, conv_ln_g, conv_ln_b, wall, S, tm)
    o = _attn_fwd(z, zkv, cos_t, sin_t, sinks1, S, tq)
    loss_p, dx1, dm, yb, g_ln_post, gsq, gw_ppt = _tail_a(x2, tgt, p2, o, ya, z, ln_post, wall, wppt, tm)

    cidx = jnp.reshape(cc, (1,)).astype(jnp.int32)
    scidx = jnp.stack([shard, cc]).astype(jnp.int32)

    def landing(pack, n, dtype):
        return jax.ShapeDtypeStruct((n, pack.shape[1] // 2, D), dtype)

    dz, do, dc, gvec, gsq = _tail_b(dm, ya, yb, o, z, pw, y, rstd, conv_ln_g, conv_ln_b, wall,
                                    gw_ppt.reshape(PLE, D), gsq, tm)
    dz, g_wdw, r1_sq = _conv_bwd(dc, z, wdw, dz, S, tm, _exchange_copies, gsq, landing(gsq, N_SHARDS, F32))
    cs_sq = _chip_sum(cidx, gsq, r1_sq, "chip_sum_sq")
    dz, dkv, g_sinks, r2_sq = _attn_bwd(z, zkv, o, do, cos_t, sin_t, sinks1, dz, S, tq, _chip_sum_copies, cs_sq,
                                        landing(gsq, 3, BF16))
    gwt_pack = _gwt(dz, dkv, h, min(2 * TILE_PROJ, T))

    tm_dh = tm_res
    n_dh = T // tm_dh
    n_a = max(1, n_dh // 4)
    gx, g_ln_pre_a, r1_wt = _dh(
        dz, dkv, wt, x2, dx1, ln_pre, tm_dh, 0, n_a, None, "dh_exchange", _exchange_copies, gwt_pack,
        landing(gwt_pack, N_SHARDS, F32))
    cs_wt = _chip_sum(cidx, gwt_pack, r1_wt, "chip_sum_wt")
    gx, g_ln_pre_b, r2_wt = _dh(
        dz, dkv, wt, x2, dx1, ln_pre, tm_dh, n_a, n_dh - n_a, gx, "dh_send", _chip_sum_copies, cs_wt,
        landing(gwt_pack, 3, BF16))
    g_ln_pre = g_ln_pre_a + g_ln_pre_b
    row37 = jnp.concatenate([g_sinks[0:1, 0:N_HEADS], loss_p, jnp.zeros((1, D - N_HEADS - 1), F32)], axis=1)
    vec = jnp.concatenate([g_wdw, g_ln_pre, g_ln_post, gvec[2:3], gvec[0:1], gvec[1:2], row37,
                           jnp.zeros((VEC_ROWS - 38, D), F32)], axis=0)
    gfin_wt, gfin_sq, tot = _finish_reduce(_final_half(scidx, gwt_pack, r1_wt, r2_wt, "final_half_wt"),
                                           _final_half(scidx, gsq, r1_sq, r2_sq, "final_half_sq"), vec)

    g_w_in, d_w_in, nm_w_in, nv_w_in = [a.T for a in _adamw_rows(
        gfin_wt, w_in[0].T, m_w_in[0].T, v_w_in[0].T, WIN_SHARD // 8, "adamw_w_in")]
    g_w_in = g_w_in[None]
    sq_m = (m_w_pw, m_w_br_conv, m_w_br_attn, m_w_out, m_w_ple_gate)
    sq_v = (v_w_pw, v_w_br_conv, v_w_br_attn, v_w_out, v_w_ple_gate)
    sq_res = _adamw_square(gfin_sq, [w[0] for w in sq_w], [m[0] for m in sq_m], [v[0] for v in sq_v])
    g_wpp = gfin_sq[5 * SQ_SHARD:SQ_PACK].reshape(PLE, PLE).T
    g_dw_all = tot[0:CONV_K]
    g_dw = lax.dynamic_slice_in_dim(g_dw_all, shard * PLE, PLE, axis=1)
    small_g = [g_wpp, g_dw, tot[32:33], tot[33:34], tot[34:35], tot[35:36], tot[36:37],
               tot[37:38, 0:N_HEADS]]
    small_w = [w_ple_proj[0], w_dw[0], ln_pre, ln_post, b_dw, conv_ln_g, conv_ln_b, sinks]
    small_m = [m_w_ple_proj[0], m_w_dw[0], m_ln_pre, m_ln_post, m_b_dw, m_conv_ln_g, m_conv_ln_b, m_sinks]
    small_v = [v_w_ple_proj[0], v_w_dw[0], v_ln_pre, v_ln_post, v_b_dw, v_conv_ln_g, v_conv_ln_b, v_sinks]
    small = _adamw_small(small_g, small_w, small_m, small_v)

    loss = tot[37, N_HEADS]
    grads = [g_w_in, small_g[2], small_g[3], g_dw[None], small_g[4], small_g[5], small_g[6],
             sq_res[0][0][None], small_g[7], sq_res[1][0][None], sq_res[2][0][None], sq_res[3][0][None],
             sq_res[4][0][None], g_wpp[None]]

    def triple(i):
        w_in_t = (d_w_in[None], nm_w_in[None], nv_w_in[None])
        sq = lambda k: tuple(a[None] for a in sq_res[k][1:4])
        sm = lambda k, lead: tuple(a[None] if lead else a for a in small[k])
        return [w_in_t[i], sm(2, False)[i], sm(3, False)[i], sm(1, True)[i], sm(4, False)[i],
                sm(5, False)[i], sm(6, False)[i], sq(0)[i], sm(7, False)[i], sq(1)[i], sq(2)[i], sq(3)[i],
                sq(4)[i], sm(0, True)[i]]

    return (loss, gx.reshape(nb, S, D), *grads, *triple(0), *triple(1), *triple(2))
```

```python
import functools

import jax
import jax.numpy as jnp
import numpy as np
from jax import lax
from jax.experimental import pallas as pl
from jax.experimental.pallas import tpu as pltpu

F32 = jnp.float32
BF16 = jnp.bfloat16

D = 1024
PLE = 256
N_HEADS = 16
HEAD_DIM = 64
BLOCK = 128
CONV_K = 31
ROPE_DIM = 16
ROPE_THETA = 500000.0
EPS = 1e-6
IN_WIDTH = 7424
N_SHARDS = 4

ADAM_LR = 0.001
ADAM_B1 = 0.9
ADAM_B2 = 0.999
ADAM_EPS = 1e-08
ADAM_WD = 0.01
ADAM_STEP = 10

SQ_NAMES = ("w_pw", "w_br_conv", "w_br_attn", "w_out", "w_ple_gate")
WT0 = 5 * D
WPP0 = WT0 + IN_WIDTH
WALL_ROWS = WPP0 + PLE
WIN_SHARD = IN_WIDTH // N_SHARDS
SQ_SHARD = D // N_SHARDS
WPP_SHARD = PLE * PLE // D
PACK_ROWS = WIN_SHARD + 5 * SQ_SHARD + WPP_SHARD
HALF_ROWS = PACK_ROWS // 2
VMEM_LIMIT = 56 * 1024 * 1024
MESH = pl.DeviceIdType.MESH
TILE_RESIDENT = 512
TILE_PROJ = 1024
TILE_TOKEN = 256
TILE_ATTN = 512
TAIL_PARTS = 1


ZB_AGATE, ZB_GCONV, ZB_GATTN, ZB_CGATE, ZB_CVAL, ZB_CGLU, ZB_Q = range(7)
ZKV = 7 * D
_SEGMENTS = ((0, D, ZB_CVAL * D), (D, D, ZB_CGLU * D), (2 * D, D, ZB_CGATE * D), (3 * D, D, ZB_Q * D),
             (4 * D, 2 * BLOCK, ZKV), (4 * D + 2 * BLOCK, D, ZB_AGATE * D),
             (5 * D + 2 * BLOCK, D, ZB_GCONV * D), (6 * D + 2 * BLOCK, D, ZB_GATTN * D))
_WT_CUTS = (0, 192, 640, 1216, WIN_SHARD)


def _zp_row(o):
    for a, w, zp in _SEGMENTS:
        if a <= o < a + w:
            return zp + o - a
    raise ValueError(o)


def _pieces(s):
    out = []
    for a, b in zip(_WT_CUTS[:-1], _WT_CUTS[1:]):
        first = _zp_row(WIN_SHARD * s + a)
        assert _zp_row(WIN_SHARD * s + b - 1) == first + b - a - 1
        out.append((a, b - a, WT0 + first))
    for k in range(5):
        out.append((WIN_SHARD + SQ_SHARD * k, SQ_SHARD, D * k + SQ_SHARD * s))
    out.append((WIN_SHARD + 5 * SQ_SHARD, WPP_SHARD, WPP0 + WPP_SHARD * s))
    return out


N_PIECES = len(_pieces(0))


def _wall_segments(wall0, rows):
    out = []
    for s in range(N_SHARDS):
        for pr, n, wr in _pieces(s):
            lo, hi = max(wr, wall0), min(wr + n, wall0 + rows)
            if lo < hi:
                out.append((lo - wall0, hi - lo, s, pr + lo - wr))
    assert sum(n for _, n, _, _ in out) == rows
    return out


def _sel(s, vals):
    r = jnp.int32(vals[0])
    for i in range(1, len(vals)):
        r = jnp.where(s == i, jnp.int32(vals[i]), r)
    return r


def _sig(x):
    return 1.0 / (1.0 + jnp.exp(-x))


def _mm(a, b):
    return lax.dot_general(a, b, (((1,), (0,)), ((), ())), preferred_element_type=F32)


def _mm_nt(a, b):
    return lax.dot_general(a, b, (((1,), (1,)), ((), ())), preferred_element_type=F32)


def _mm_tn(a, b):
    return lax.dot_general(a, b, (((0,), (0,)), ((), ())), preferred_element_type=F32)


def _params(sem=None):
    return pltpu.CompilerParams(dimension_semantics=sem, vmem_limit_bytes=VMEM_LIMIT)


def _flush_to_pack(acc_ref, gpack_ref, wall0, sem):
    base = 0 if gpack_ref.shape[1] == WIN_SHARD else WIN_SHARD
    for r, n, s, pr in _wall_segments(wall0, acc_ref.shape[0]):
        assert 0 <= pr - base and pr - base + n <= gpack_ref.shape[1]
        cp = pltpu.make_async_copy(acc_ref.at[pl.ds(r, n)], gpack_ref.at[s, pl.ds(pr - base, n)], sem)
        cp.start()
        cp.wait()


def _coords():
    return lax.axis_index("x"), lax.axis_index("y"), lax.axis_index("c")


def _chip_peers(x, y):
    return [(1 - x, y), (x, 1 - y), (1 - x, 1 - y)]


WIN_PIECES = tuple(range(len(_WT_CUTS) - 1))
SQ_PIECES = tuple(range(len(WIN_PIECES), N_PIECES))


def _gather_ops(group, src, landing, bytes_ref, stage, send_sems, recv_sems, loc_sem):
    sizes = [_pieces(0)[p][1] for p in group]
    half_rows = sum(n // 2 for n in sizes)

    def rcopy(a, b, k, dev):
        return pltpu.make_async_remote_copy(src_ref=a, dst_ref=b, send_sem=send_sems.at[k],
                                            recv_sem=recv_sems.at[k], device_id=dev, device_id_type=MESH)

    def total(k):
        x, y, c = _coords()
        rows = bytes_ref.at[pl.ds(0, half_rows)]
        return rcopy(rows, rows, k, (x, y, c))

    def send():
        x, y, c = _coords()
        s_me = 2 * x + y
        for k, (px, py) in enumerate(_chip_peers(x, y)):
            for p, n in zip(group, sizes):
                h = n // 2
                rcopy(src(p, c * h, h), landing(p, s_me, c * h, h), k, (px, py, c)).start()
        for p, n in zip(group, sizes):
            for a, b in ((src(p, 0, n), stage.at[pl.ds(0, n)]), (stage.at[pl.ds(0, n)], landing(p, s_me, 0, n))):
                cp = pltpu.make_async_copy(a, b, loc_sem)
                cp.start()
                cp.wait()

    def forward():
        x, y, c = _coords()
        for k, (px, py) in enumerate(_chip_peers(x, y)):
            total(k).wait_recv()
            for p, n in zip(group, sizes):
                rows = landing(p, 2 * px + py, c * (n // 2), n // 2)
                rcopy(rows, rows, 3 + k, (x, y, 1 - c)).start()

    def finish():
        for k in range(3):
            total(3 + k).wait_recv()
        for k in range(6):
            total(k).wait_send()

    return send, forward, finish


def _piece_rows(ref, start, off, n):
    first = start + off
    return ref.at[pl.ds(first if isinstance(first, int) else pl.multiple_of(first, 32), n)]


ROPE_ROWS = 16


def _prenorm(x, ln_pre, pos, freq, spread, tm):
    T = x.shape[0]

    def to_lanes(v, e):
        out = None
        for _ in range(3):
            part = v.astype(BF16)
            term = _mm_tn(part, e)
            out = term if out is None else out + term
            v = v - part.astype(F32)
        return out

    def body(x_ref, g_ref, pos_ref, f_ref, e_ref, h_ref, cos_ref, sin_ref):
        xv = x_ref[...]
        r = lax.rsqrt(jnp.mean(xv * xv, axis=-1, keepdims=True) + EPS)
        h_ref[...] = (xv * r * g_ref[...]).astype(BF16)
        ang = f_ref[...] * pos_ref[...]
        cos_ref[...] = to_lanes(jnp.cos(ang), e_ref[0]) + e_ref[2, 0:1, :].astype(F32)
        sin_ref[...] = to_lanes(jnp.sin(ang), e_ref[1])

    return pl.pallas_call(
        body, name="prenorm", grid=(T // tm,),
        out_shape=(jax.ShapeDtypeStruct((T, D), BF16), jax.ShapeDtypeStruct((T, BLOCK), F32),
                   jax.ShapeDtypeStruct((T, BLOCK), F32)),
        in_specs=[pl.BlockSpec((tm, D), lambda i: (i, 0)), pl.BlockSpec((1, D), lambda i: (0, 0)),
                  pl.BlockSpec((1, tm), lambda i: (0, i)), pl.BlockSpec((ROPE_ROWS, 1), lambda i: (0, 0)),
                  pl.BlockSpec((3, ROPE_ROWS, BLOCK), lambda i: (0, 0, 0))],
        out_specs=(pl.BlockSpec((tm, D), lambda i: (i, 0)), pl.BlockSpec((tm, BLOCK), lambda i: (i, 0)),
                   pl.BlockSpec((tm, BLOCK), lambda i: (i, 0))),
        compiler_params=_params(("arbitrary",)),
    )(x, ln_pre, pos, freq, spread)


SQ_PACK = PACK_ROWS - WIN_SHARD


def _row_tile(half):
    return max(t for t in range(8, 321, 8) if half % t == 0)


def _exchange_copies(g_ref, r1_ref, send_sems, recv_sems):
    x, y, c = _coords()
    half = g_ref.shape[1] // 2
    return [pltpu.make_async_remote_copy(
        src_ref=g_ref.at[:, pl.ds(pl.multiple_of((1 - c) * half, 32), half), :], dst_ref=r1_ref,
        send_sem=send_sems.at[0], recv_sem=recv_sems.at[0], device_id=(x, y, 1 - c), device_id_type=MESH)]


def _chip_sum_copies(cs_ref, r2_ref, send_sems, recv_sems):
    x, y, c = _coords()
    return [pltpu.make_async_remote_copy(
        src_ref=cs_ref.at[2 * px + py], dst_ref=r2_ref.at[k], send_sem=send_sems.at[k],
        recv_sem=recv_sems.at[k], device_id=(px, py, c), device_id_type=MESH)
        for k, (px, py) in enumerate(_chip_peers(x, y))]


def _chip_sum(cidx, gpack, r1, name):
    half = gpack.shape[1] // 2
    rt = _row_tile(half)

    def body(c_ref, g_ref, r_ref, o_ref):
        o_ref[...] = (g_ref[...] + r_ref[...]).astype(BF16)

    nt = half // rt
    return pl.pallas_call(
        body, name=name,
        grid_spec=pltpu.PrefetchScalarGridSpec(
            num_scalar_prefetch=1, grid=(N_SHARDS, nt),
            in_specs=[pl.BlockSpec((1, rt, D), lambda s, t, c: (s, c[0] * nt + t, 0)),
                      pl.BlockSpec((1, rt, D), lambda s, t, c: (s, t, 0))],
            out_specs=pl.BlockSpec((1, rt, D), lambda s, t, c: (s, t, 0))),
        out_shape=jax.ShapeDtypeStruct((N_SHARDS, half, D), BF16),
        compiler_params=_params(("arbitrary", "arbitrary")),
    )(cidx, gpack, r1)


def _final_half(sc, gpack, r1, r2, name):
    rows = gpack.shape[1]
    half = rows // 2
    rt = _row_tile(half)

    def body(sc_ref, g_ref, r_ref, p_ref, o_ref):
        acc = g_ref[0] + r_ref[0]
        for k in range(3):
            acc = acc + p_ref[k].astype(F32)
        o_ref[...] = acc

    nt = half // rt
    return pl.pallas_call(
        body, name=name,
        grid_spec=pltpu.PrefetchScalarGridSpec(
            num_scalar_prefetch=1, grid=(nt,),
            in_specs=[pl.BlockSpec((1, rt, D), lambda t, sc: (sc[0], sc[1] * nt + t, 0)),
                      pl.BlockSpec((1, rt, D), lambda t, sc: (sc[0], t, 0)),
                      pl.BlockSpec((3, rt, D), lambda t, sc: (0, t, 0))],
            out_specs=pl.BlockSpec((rt, D), lambda t, sc: (sc[1] * nt + t, 0))),
        out_shape=jax.ShapeDtypeStruct((rows, D), F32),
        compiler_params=_params(("arbitrary",)),
    )(sc, gpack, r1, r2)


VEC_ROWS = 40


def _finish_reduce(fwt, fsq, vec):
    def body(fwt_ref, fsq_ref, v_ref, owt_ref, osq_ref, tot_ref, buf, send_sems, recv_sems):
        x, y, c = _coords()
        swaps = []
        for k, (f_ref, o_ref) in enumerate(((fwt_ref, owt_ref), (fsq_ref, osq_ref))):
            half = f_ref.shape[0] // 2
            rows = pl.ds(pl.multiple_of(c * half, 32), half)
            swaps.append(pltpu.make_async_remote_copy(
                src_ref=f_ref.at[rows], dst_ref=o_ref.at[rows], send_sem=send_sems.at[7 + k],
                recv_sem=recv_sems.at[7 + k], device_id=(x, y, 1 - c), device_id_type=MESH))
        for cp in swaps:
            cp.start()
        me = 4 * x + 2 * y + c
        buf[me] = v_ref[...]
        cps = []
        for r in range(1, 8):
            dx, dy, dc = (r >> 2) & 1, (r >> 1) & 1, r & 1
            peer = (1 - x if dx else x, 1 - y if dy else y, 1 - c if dc else c)
            cp = pltpu.make_async_remote_copy(
                src_ref=v_ref, dst_ref=buf.at[me], send_sem=send_sems.at[r - 1],
                recv_sem=recv_sems.at[r - 1], device_id=peer, device_id_type=MESH)
            cp.start()
            cps.append(cp)
        for cp in cps:
            cp.wait_recv()
        for cp in cps:
            cp.wait_send()
        acc = buf[0]
        for d in range(1, 8):
            acc = acc + buf[d]
        tot_ref[...] = acc
        for cp in swaps:
            cp.wait()

    any_spec = pl.BlockSpec(memory_space=pl.ANY)
    vm = pl.BlockSpec(memory_space=pltpu.VMEM)
    return pl.pallas_call(
        body, name="finish_reduce",
        out_shape=(jax.ShapeDtypeStruct(fwt.shape, F32), jax.ShapeDtypeStruct(fsq.shape, F32),
                   jax.ShapeDtypeStruct((VEC_ROWS, D), F32)),
        in_specs=[any_spec, any_spec, vm], out_specs=(any_spec, any_spec, vm),
        input_output_aliases={0: 0, 1: 1},
        scratch_shapes=[pltpu.VMEM((8, VEC_ROWS, D), F32), pltpu.SemaphoreType.DMA((9,)),
                        pltpu.SemaphoreType.DMA((9,))],
    )(fwt, fsq, vec)


SOLO_ROWS = WIN_SHARD - BLOCK // 2


def _solo_first(s):
    return 0 if s % 2 == 0 else BLOCK // 2


def _solo_segments(s):
    lo = _solo_first(s)
    out = []
    for a, n, wr in _pieces(s)[:len(WIN_PIECES)]:
        b0, b1 = max(a, lo), min(a + n, lo + SOLO_ROWS)
        if b0 >= b1:
            continue
        z0 = wr - WT0 + b0 - a
        if out and out[-1][0] + out[-1][1] == b0 - lo and out[-1][2] + out[-1][1] == z0:
            out[-1] = (out[-1][0], out[-1][1] + b1 - b0, out[-1][2])
        else:
            out.append((b0 - lo, b1 - b0, z0))
    out = [r for o, n, z0 in out for r in
           (((o, ZKV - z0, z0), (o + ZKV - z0, z0 + n - ZKV, ZKV)) if z0 < ZKV < z0 + n else ((o, n, z0),))]
    assert all(v % BLOCK == 0 for seg in out for v in seg) and sum(n for _, n, _ in out) == SOLO_ROWS
    return out


def _shared_tile(pair):
    z0 = _zp_row(WIN_SHARD * (2 * pair) + SOLO_ROWS)
    assert z0 % BLOCK == 0 and _zp_row(WIN_SHARD * (2 * pair + 1)) == z0 + BLOCK // 2
    return z0


def _inproj(h, win_t, wdw_shard, sq_shards, wpp_shard, tm):
    T = h.shape[0]
    nsq = len(sq_shards)
    n_t = T // tm
    assert n_t >= 2
    tables = [[_pieces(s)[p][2] - WT0 for s in range(N_SHARDS)] for p in WIN_PIECES]
    sizes = [_pieces(0)[p][1] for p in WIN_PIECES]
    half_rows = sum(n // 2 for n in sizes)
    relation_of_pass = {1: 1, 2: 0, 3: 2}

    def body(*refs):
        h_ref, win_ref, wdw_ref = refs[:3]
        sq_refs = refs[3:3 + nsq]
        wpp_ref = refs[3 + nsq]
        (z_ref, zkv_ref, wt_ref, wdwall_ref, wsq_ref, wppf_ref, wbuf, stage, stage_sh, sq_stage, wsend, wrecv,
         sqsend, sqrecv, loc_sems, out_sems, sh_sems) = refs[4 + nsq:]
        p = pl.program_id(0)
        t = pl.program_id(1)
        x, y, c = _coords()
        s_me = 2 * x + y
        peers = _chip_peers(x, y)
        shard = jnp.bitwise_xor(s_me, p)
        first, last = t == 0, t == n_t - 1

        def sq_src(q, off, n):
            k = q - SQ_PIECES[0]
            return _piece_rows(wpp_ref if k == nsq else sq_refs[k], 0, off, n)

        def sq_landing(q, s, off, n):
            k = q - SQ_PIECES[0]
            if k == nsq:
                return _piece_rows(wppf_ref, WPP_SHARD * s, off, n)
            return _piece_rows(wsq_ref, D * k + SQ_SHARD * s, off, n)

        sq_send, sq_forward, sq_finish = _gather_ops(SQ_PIECES, sq_src, sq_landing, wsq_ref, sq_stage, sqsend,
                                                     sqrecv, loc_sems.at[2])

        def rcopy(a, b, k, dev):
            return pltpu.make_async_remote_copy(src_ref=a, dst_ref=b, send_sem=wsend.at[k], recv_sem=wrecv.at[k],
                                                device_id=dev, device_id_type=MESH)

        def total(k):
            rows = wt_ref.at[pl.ds(0, half_rows)]
            return rcopy(rows, rows, k, (x, y, c))

        def in_hbm(q, s, off, n):
            return _piece_rows(wt_ref, _sel(s, tables[q]), off, n)

        def in_vmem(q, s):
            return _piece_rows(wbuf, WIN_SHARD * s + _WT_CUTS[q], 0, sizes[q])

        def send_to(k):
            px, py = peers[k]
            for q, n in zip(WIN_PIECES, sizes):
                rcopy(_piece_rows(win_ref, _WT_CUTS[q], c * (n // 2), n // 2), in_hbm(q, s_me, c * (n // 2), n // 2),
                      k, (px, py, c)).start()

        def forward_from(k):
            px, py = peers[k]
            total(k).wait_recv()
            for q, n in zip(WIN_PIECES, sizes):
                rows = in_hbm(q, 2 * px + py, c * (n // 2), n // 2)
                rcopy(rows, rows, 3 + k, (x, y, 1 - c)).start()
            total(3 + k).wait_recv()

        def shard_total(a, b, sem):
            return pltpu.make_async_copy(a.at[pl.ds(0, WIN_SHARD)], b.at[pl.ds(0, WIN_SHARD)], sem)

        def wdw_copies():
            return [pltpu.make_async_remote_copy(
                src_ref=wdw_ref, dst_ref=wdwall_ref.at[s_me], send_sem=wsend.at[6 + k], recv_sem=wrecv.at[6 + k],
                device_id=(px, py, c), device_id_type=MESH) for k, (px, py) in enumerate(peers)]

        def own_wdw():
            return pltpu.make_async_copy(wdw_ref, wdwall_ref.at[s_me], loc_sems.at[3])

        @pl.when((p == 0) & first)
        def _():
            send_to(0)
            send_to(1)
            own_wdw().start()
            for cp in wdw_copies():
                cp.start()
            for q in WIN_PIECES:
                pltpu.make_async_copy(_piece_rows(win_ref, _WT_CUTS[q], 0, sizes[q]), in_vmem(q, s_me),
                                      loc_sems.at[0]).start()
            shard_total(win_ref, wbuf, loc_sems.at[0]).wait()
            for q in WIN_PIECES:
                pltpu.make_async_copy(in_vmem(q, s_me), in_hbm(q, s_me, 0, sizes[q]), loc_sems.at[1]).start()

        for pp, k in relation_of_pass.items():
            @pl.when((p == pp - 1) & last)
            def _(k=k):
                forward_from(k)
                px, py = peers[k]
                for q in WIN_PIECES:
                    pltpu.make_async_copy(in_hbm(q, 2 * px + py, 0, sizes[q]), in_vmem(q, 2 * px + py),
                                          loc_sems.at[0]).start()

            @pl.when((p == pp) & first)
            def _(pp=pp):
                shard_total(wt_ref, wbuf, loc_sems.at[0]).wait()
                if pp == 1:
                    total(0).wait_send()
                    total(1).wait_send()
                    send_to(2)
                if pp == 2:
                    sq_send()

        @pl.when((p == 3) & (t == n_t // 2))
        def _():
            sq_forward()

        step = p * n_t + t
        slot = step % 2
        rows = pl.ds(pl.multiple_of(t * tm, tm), tm)

        def out_total(sl):
            return pltpu.make_async_copy(stage.at[sl], stage.at[sl], out_sems.at[sl])

        def sh_copy(sl, z0):
            return pltpu.make_async_copy(stage_sh.at[sl], z_ref.at[rows, pl.ds(z0, BLOCK)], sh_sems.at[sl])

        @pl.when(step >= 2)
        def _():
            out_total(slot).wait()

        @pl.when((step >= 2) & (((step - 2) // n_t) % 2 == 1))
        def _():
            sh_copy(slot, 0).wait()

        solo0 = pl.multiple_of(WIN_SHARD * shard + (BLOCK // 2) * (shard % 2), BLOCK // 2)
        stage[slot] = _mm_nt(h_ref[...], wbuf[pl.ds(solo0, SOLO_ROWS), :]).astype(BF16)
        for s in range(N_SHARDS):
            @pl.when(shard == s)
            def _(s=s):
                for off, n, z0 in _solo_segments(s):
                    dst = zkv_ref.at[rows] if z0 == ZKV else z_ref.at[rows, pl.ds(z0, n)]
                    pltpu.make_async_copy(stage.at[slot, :, pl.ds(off, n)], dst, out_sems.at[slot]).start()

        @pl.when(p % 2 == 1)
        def _():
            pair = shard // 2
            w0 = pl.multiple_of(2 * WIN_SHARD * pair + SOLO_ROWS, BLOCK // 2)
            z0 = pl.multiple_of(jnp.where(pair == 0, _shared_tile(0), _shared_tile(1)), BLOCK)
            stage_sh[slot] = _mm_nt(h_ref[...], wbuf[pl.ds(w0, BLOCK), :]).astype(BF16)
            sh_copy(slot, z0).start()

        @pl.when((p == 3) & last)
        def _():
            sq_finish()
            for k in (2, 3, 4, 5):
                total(k).wait_send()
            shard_total(wbuf, wt_ref, loc_sems.at[1]).wait()
            cps = wdw_copies()
            for cp in cps:
                cp.wait_recv()
            for cp in cps:
                cp.wait_send()
            own_wdw().wait()
            for sl in range(2):
                out_total(sl).wait()
                sh_copy(sl, 0).wait()

    any_spec = pl.BlockSpec(memory_space=pl.ANY)
    return pl.pallas_call(
        body, name="inproj", grid=(N_SHARDS, n_t),
        in_specs=[pl.BlockSpec((tm, D), lambda p, t: (t, 0))] + [any_spec] * (nsq + 3),
        out_specs=(any_spec,) * 6,
        out_shape=(jax.ShapeDtypeStruct((T, ZKV), BF16), jax.ShapeDtypeStruct((T, 2 * BLOCK), BF16),
                   jax.ShapeDtypeStruct((IN_WIDTH, D), BF16), jax.ShapeDtypeStruct((N_SHARDS, 32, PLE), F32),
                   jax.ShapeDtypeStruct((nsq * D, D), BF16), jax.ShapeDtypeStruct((PLE, D), BF16)),
        scratch_shapes=[pltpu.VMEM((IN_WIDTH, D), BF16), pltpu.VMEM((2, tm, SOLO_ROWS), BF16),
                        pltpu.VMEM((2, tm, BLOCK), BF16), pltpu.VMEM((SQ_SHARD, D), BF16),
                        pltpu.SemaphoreType.DMA((9,)), pltpu.SemaphoreType.DMA((9,)),
                        pltpu.SemaphoreType.DMA((6,)), pltpu.SemaphoreType.DMA((6,)),
                        pltpu.SemaphoreType.DMA((4,)), pltpu.SemaphoreType.DMA((2,)),
                        pltpu.SemaphoreType.DMA((2,))],
        compiler_params=_params(("arbitrary", "arbitrary")),
    )(h, win_t, wdw_shard, *sq_shards, wpp_shard)


HALO = 32
CONV_RC = 64
CONV_LC = 256


def _conv_taps(w_ref, src, r0, lane0, offset_of_tap):
    lanes = pl.ds(lane0, CONV_LC)
    out = None
    for b in range(8):
        taps = [k for k in range(CONV_K) if offset_of_tap(k) % 8 == b]
        if not taps:
            continue
        rows = CONV_RC + (8 if b else 0)
        vb = None
        for k in taps:
            term = w_ref[k:k + 1, lanes] * src[pl.ds(r0 + (offset_of_tap(k) - b), rows), lanes]
            vb = term if vb is None else vb + term
        vb = vb[b:b + CONV_RC] if b else vb
        out = vb if out is None else out + vb
    return out


def _conv_fwd(z, wdw, b_dw, ln_g, ln_b, wall, S, tm):
    T = z.shape[0]
    nt = S // tm
    hb = tm // HALO

    def body(cv_ref, cg_ref, cgate_ref, hcv_ref, hcg_ref, wdw_ref, bdw_ref, lng_ref, lnb_ref, wpw_ref,
             wbrc_ref, ya_ref, y_ref, rstd_ref, pw_ref, ubuf, cbuf):
        t = pl.program_id(1)
        ubuf[HALO:HALO + tm, :] = cv_ref[...].astype(F32) * _sig(cg_ref[...].astype(F32))
        hu = hcv_ref[...].astype(F32) * _sig(hcg_ref[...].astype(F32))
        ubuf[0:HALO, :] = jnp.where(t > 0, hu, 0.0)
        ubuf[HALO + tm:HALO + tm + 8, :] = jnp.zeros((8, D), F32)

        def chunk(ci, carry):
            r0 = pl.multiple_of(ci * CONV_RC, CONV_RC)
            for lg in range(D // CONV_LC):
                acc = _conv_taps(wdw_ref, ubuf, r0, lg * CONV_LC, lambda k: HALO - (CONV_K - 1) + k)
                cbuf[pl.ds(r0, CONV_RC), pl.ds(lg * CONV_LC, CONV_LC)] = acc
            return carry

        lax.fori_loop(0, tm // CONV_RC, chunk, 0)
        cc = cbuf[...] + bdw_ref[...]
        mu = jnp.mean(cc, axis=-1, keepdims=True)
        dd = cc - mu
        rstd = lax.rsqrt(jnp.mean(dd * dd, axis=-1, keepdims=True) + EPS)
        yn = dd * rstd
        y_ref[...] = yn.astype(BF16)
        rstd_ref[...] = rstd
        n = yn * lng_ref[...] + lnb_ref[...]
        s = n * _sig(n)
        pw = _mm(s.astype(BF16), wpw_ref[...])
        pw_ref[...] = pw.astype(BF16)
        gt = cgate_ref[...].astype(F32)
        ya_in = pw * (gt * _sig(gt))
        ya_ref[...] = _mm(ya_in.astype(BF16), wbrc_ref[...]).astype(BF16)

    def row(b, t):
        return b * nt + t

    def halo(b, t):
        return jnp.maximum(row(b, t) * hb - 1, 0)

    vec = pl.BlockSpec((1, D), lambda b, t: (0, 0))
    tile = lambda j: pl.BlockSpec((tm, D), lambda b, t: (row(b, t), j))
    out_tile = pl.BlockSpec((tm, D), lambda b, t: (row(b, t), 0))
    return pl.pallas_call(
        body, name="conv_fwd", grid=(T // S, nt),
        in_specs=[tile(ZB_CVAL), tile(ZB_CGLU), tile(ZB_CGATE),
                  pl.BlockSpec((HALO, D), lambda b, t: (halo(b, t), ZB_CVAL)),
                  pl.BlockSpec((HALO, D), lambda b, t: (halo(b, t), ZB_CGLU)),
                  pl.BlockSpec((32, D), lambda b, t: (0, 0)), vec, vec, vec,
                  pl.BlockSpec((D, D), lambda b, t: (0, 0)),
                  pl.BlockSpec((D, D), lambda b, t: (1, 0))],
        out_specs=(out_tile, out_tile, pl.BlockSpec((tm, 1), lambda b, t: (row(b, t), 0)), out_tile),
        out_shape=(jax.ShapeDtypeStruct((T, D), BF16), jax.ShapeDtypeStruct((T, D), BF16),
                   jax.ShapeDtypeStruct((T, 1), F32), jax.ShapeDtypeStruct((T, D), BF16)),
        scratch_shapes=[pltpu.VMEM((tm + HALO + 8, D), F32), pltpu.VMEM((tm, D), F32)],
        compiler_params=_params(("arbitrary", "arbitrary")),
    )(z, z, z, z, z, wdw, b_dw, ln_g, ln_b, wall, wall)


def _swap_matrix():
    r = lax.broadcasted_iota(jnp.int32, (BLOCK, BLOCK), 0)
    l = lax.broadcasted_iota(jnp.int32, (BLOCK, BLOCK), 1)
    lh = l & (HEAD_DIM - 1)
    half = ROPE_DIM // 2
    hit = ((lh < half) & (r == l + half)) | ((lh >= half) & (lh < ROPE_DIM) & (r == l - half))
    return jnp.where(hit, 1.0, 0.0).astype(BF16)


def _rope(tb, cos, sin, pswap):
    return tb.astype(F32) * cos + _mm(tb, pswap) * sin


def _rope_f32(tv, cos, sin, pswap):
    hi = tv.astype(BF16)
    lo = (tv - hi.astype(F32)).astype(BF16)
    return tv * cos + (_mm(hi, pswap) + _mm(lo, pswap)) * sin


def _kv_variants(kv):
    lane = lax.broadcasted_iota(jnp.int32, kv.shape, 1)
    lo = lane < HEAD_DIM
    sw = pltpu.roll(kv, HEAD_DIM, 1)
    z = jnp.zeros_like(kv)
    g0 = (jnp.where(lo, kv, z).astype(BF16), jnp.where(lo, z, sw).astype(BF16))
    g1 = (jnp.where(lo, sw, z).astype(BF16), jnp.where(lo, z, kv).astype(BF16))
    return (g0, g1)


def _band_mask(nq):
    qi = lax.broadcasted_iota(jnp.int32, (nq * BLOCK, 2 * BLOCK), 0) & (BLOCK - 1)
    sj = lax.broadcasted_iota(jnp.int32, (nq * BLOCK, 2 * BLOCK), 1)
    return (sj <= qi + BLOCK) & (sj > qi), sj


def _sink_rep(sink_ref, g, e):
    return jnp.concatenate(
        [jnp.full((BLOCK, BLOCK), sink_ref[8 * g + 2 * j + e], F32) for j in range(4)], axis=0)


def _softmax_parts(s, valid, sk):
    rows = s.shape[0]
    s = jnp.where(valid, s, -1e30)
    m = jnp.maximum(jnp.broadcast_to(jnp.max(s, axis=-1, keepdims=True), (rows, BLOCK)), sk)
    return jnp.exp(s - jnp.concatenate([m, m], axis=1)), jnp.exp(sk - m)


def _softmax_sink(s, valid, sk):
    p, ps = _softmax_parts(s, valid, sk)
    inv = 1.0 / (_mm(p.astype(BF16), jnp.ones((2 * BLOCK, BLOCK), BF16)) + ps)
    return p * jnp.concatenate([inv, inv], axis=1), ps * inv


def _attn_fwd(z, zkv, cos_t, sin_t, sinks, S, tq):
    T = z.shape[0]
    nt = S // tq
    nq = tq // BLOCK

    def body(sink_ref, q_ref, kv_ref, hkv_ref, cos_ref, sin_ref, hcos_ref, hsin_ref, o_ref):
        t = pl.program_id(1)
        cos = cos_ref[...]
        sin = sin_ref[...]
        pswap = _swap_matrix()
        kv = jnp.concatenate([hkv_ref[...], kv_ref[...]], axis=0)
        cos_k = jnp.concatenate([hcos_ref[...], cos], axis=0)
        sin_k = jnp.concatenate([hsin_ref[...], sin], axis=0)
        kx = _kv_variants(_rope(kv[:, :BLOCK], cos_k, sin_k, pswap))
        one = jnp.ones((tq + BLOCK, BLOCK), BF16)
        vx = [[jnp.concatenate([v, one], axis=1) for v in vg] for vg in _kv_variants(kv[:, BLOCK:].astype(F32))]
        band, sj = _band_mask(4)
        qs = [(_rope(q_ref[:, 128 * hp:128 * hp + 128], cos, sin, pswap) * 0.125).astype(BF16)
              for hp in range(8)]
        for n in range(nq):
            first = (t == 0) & (n == 0)
            valid = band & (jnp.logical_not(first) | (sj >= BLOCK))
            r0 = n * BLOCK
            for g in range(2):
                lhs = jnp.concatenate([qs[4 * g + j][r0:r0 + BLOCK] for j in range(4)], axis=0)
                acc = jnp.zeros((4 * BLOCK, BLOCK), F32)
                for e in range(2):
                    s = _mm_nt(lhs, kx[g][e][r0:r0 + 2 * BLOCK])
                    p, ps = _softmax_parts(s, valid, _sink_rep(sink_ref, g, e))
                    r = _mm(p.astype(BF16), vx[g][e][r0:r0 + 2 * BLOCK])
                    acc = acc + r[:, 0:BLOCK] * (1.0 / (r[:, BLOCK:2 * BLOCK] + ps))
                for j in range(4):
                    o_ref[r0:r0 + BLOCK, 128 * (4 * g + j):128 * (4 * g + j) + 128] = (
                        acc[j * BLOCK:(j + 1) * BLOCK].astype(BF16))

    def row(b, t):
        return b * nt + t

    def halo(b, t):
        return jnp.maximum(row(b, t) * nq - 1, 0)

    return pl.pallas_call(
        body, name="attn_fwd", grid=(T // S, nt),
        in_specs=[pl.BlockSpec(memory_space=pltpu.SMEM),
                  pl.BlockSpec((tq, D), lambda b, t: (row(b, t), ZB_Q)),
                  pl.BlockSpec((tq, 2 * BLOCK), lambda b, t: (row(b, t), 0)),
                  pl.BlockSpec((BLOCK, 2 * BLOCK), lambda b, t: (halo(b, t), 0)),
                  pl.BlockSpec((tq, BLOCK), lambda b, t: (row(b, t), 0)),
                  pl.BlockSpec((tq, BLOCK), lambda b, t: (row(b, t), 0)),
                  pl.BlockSpec((BLOCK, BLOCK), lambda b, t: (halo(b, t), 0)),
                  pl.BlockSpec((BLOCK, BLOCK), lambda b, t: (halo(b, t), 0))],
        out_specs=pl.BlockSpec((tq, D), lambda b, t: (row(b, t), 0)),
        out_shape=jax.ShapeDtypeStruct((T, D), BF16),
        compiler_params=_params(("arbitrary", "arbitrary")),
    )(sinks, z, zkv, zkv, cos_t, sin_t, cos_t, sin_t)


def _tail_a(x, tgt, p, o, ya, z, ln_post, wall, wppt, tm):
    T = x.shape[0]
    last = T // tm - 1

    def body(x_ref, tgt_ref, p_ref, o_ref, ya_ref, ag_ref, gc_ref, ga_ref, lnp_ref, wbra_ref, wout_ref,
             wpg_ref, wppt_ref, loss_ref, dx1_ref, dm_ref, yb_ref, glnp_ref, gpack_ref, gwpp_ref,
             acc_out, acc_pg, sem):
        i = pl.program_id(0)

        @pl.when(i == 0)
        def _():
            acc_out[...] = jnp.zeros_like(acc_out)
            acc_pg[...] = jnp.zeros_like(acc_pg)
            gwpp_ref[...] = jnp.zeros_like(gwpp_ref)
            glnp_ref[...] = jnp.zeros_like(glnp_ref)
            loss_ref[...] = jnp.zeros_like(loss_ref)

        ag = ag_ref[...].astype(F32)
        yb_in = (o_ref[...].astype(F32) * (ag * _sig(ag))).astype(BF16)
        yb = _mm(yb_in, wbra_ref[...])
        yb_ref[...] = yb.astype(BF16)
        m = (_sig(gc_ref[...].astype(F32)) * ya_ref[...].astype(F32)
             + _sig(ga_ref[...].astype(F32)) * yb).astype(BF16)
        mo = _mm(m, wout_ref[...])
        r2 = lax.rsqrt(jnp.mean(mo * mo, axis=-1, keepdims=True) + EPS)
        nrm = mo * r2
        g_post = lnp_ref[...]
        x1 = x_ref[...] + nrm * g_post
        x1b = x1.astype(BF16)
        gate = _sig(_mm(x1b, wpg_ref[...]))
        pb = p_ref[...].astype(BF16)
        pp = _mm_nt(pb, wppt_ref[...])
        err = x1 + gate * pp - tgt_ref[...]
        loss_ref[...] += 0.5 * jnp.sum(jnp.sum(err * err, axis=-1, keepdims=True) * (1.0 / D),
                                       axis=0, keepdims=True)
        dx2 = err * (1.0 / D)
        dgp = (dx2 * pp * gate * (1.0 - gate)).astype(BF16)
        dpp = (dx2 * gate).astype(BF16)
        dx1 = dx2 + _mm_nt(dgp, wpg_ref[...])
        dx1_ref[...] = dx1
        acc_pg[...] += _mm_tn(x1b, dgp)
        gwpp_ref[...] += _mm_tn(dpp, pb)
        glnp_ref[...] += jnp.sum(dx1 * nrm, axis=0, keepdims=True)
        a = dx1 * g_post
        dmo = (r2 * (a - nrm * jnp.mean(a * nrm, axis=-1, keepdims=True))).astype(BF16)
        dm_ref[...] = _mm_nt(dmo, wout_ref[...]).astype(BF16)
        acc_out[...] += _mm_tn(m, dmo)

        @pl.when(i == last)
        def _():
            _flush_to_pack(acc_out, gpack_ref, 3 * D, sem.at[0])
            _flush_to_pack(acc_pg, gpack_ref, 4 * D, sem.at[1])

    tile = pl.BlockSpec((tm, D), lambda i: (i, 0))
    ztile = lambda j: pl.BlockSpec((tm, D), lambda i: (i, j))
    wsq = lambda k: pl.BlockSpec((D, D), lambda i: (k, 0))
    const = lambda shp: pl.BlockSpec(shp, lambda i: (0, 0))
    any_spec = pl.BlockSpec(memory_space=pl.ANY)
    return pl.pallas_call(
        body, name="tail_a", grid=(T // tm,),
        in_specs=[tile, tile, pl.BlockSpec((tm, PLE), lambda i: (i, 0)), tile, tile, ztile(ZB_AGATE),
                  ztile(ZB_GCONV), ztile(ZB_GATTN), const((1, D)), wsq(2), wsq(3), wsq(4), const((D, PLE))],
        out_specs=(const((1, 1)), tile, tile, tile, const((1, D)), any_spec, const((D, PLE))),
        out_shape=(jax.ShapeDtypeStruct((1, 1), F32), jax.ShapeDtypeStruct((T, D), F32),
                   jax.ShapeDtypeStruct((T, D), BF16), jax.ShapeDtypeStruct((T, D), BF16),
                   jax.ShapeDtypeStruct((1, D), F32), jax.ShapeDtypeStruct((N_SHARDS, SQ_PACK, D), F32),
                   jax.ShapeDtypeStruct((D, PLE), F32)),
        scratch_shapes=[pltpu.VMEM((D, D), F32), pltpu.VMEM((D, D), F32), pltpu.SemaphoreType.DMA((2,))],
        compiler_params=_params(("arbitrary",)),
    )(x, tgt, p, o, ya, z, z, z, ln_post, wall, wall, wall, wppt)


def _dsilu(v, sg):
    return sg * (1.0 + v * (1.0 - sg))


def _tail_b(dm, ya, yb, o, z, pw, y, rstd, ln_g, ln_b, wall, gppt, gpack, tm):
    T = dm.shape[0]
    last = T // tm - 1

    def body(dm_ref, ya_ref, yb_ref, o_ref, ag_ref, gc_ref, ga_ref, cgate_ref, pw_ref, y_ref, rstd_ref,
             lng_ref, lnb_ref, wpw_ref, wbrc_ref, wbra_ref, gppt_ref, gpack_in, dg_ref, do_ref, dc_ref,
             gvec_ref, gpack_ref, acc_bra, acc_brc, acc_pw, sem):
        i = pl.program_id(0)

        @pl.when(i == 0)
        def _():
            acc_bra[...] = jnp.zeros_like(acc_bra)
            acc_brc[...] = jnp.zeros_like(acc_brc)
            acc_pw[...] = jnp.zeros_like(acc_pw)
            gvec_ref[...] = jnp.zeros_like(gvec_ref)

        g = lng_ref[...]

        def part(rs):
            dm_v = dm_ref[rs, :].astype(F32)
            sgc = _sig(gc_ref[rs, :].astype(F32))
            sga = _sig(ga_ref[rs, :].astype(F32))
            dya = (dm_v * sgc).astype(BF16)
            dyb = (dm_v * sga).astype(BF16)
            dg_ref[rs, D:2 * D] = (dm_v * ya_ref[rs, :].astype(F32) * sgc * (1.0 - sgc)).astype(BF16)
            dg_ref[rs, 2 * D:3 * D] = (dm_v * yb_ref[rs, :].astype(F32) * sga * (1.0 - sga)).astype(BF16)
            ag = ag_ref[rs, :].astype(F32)
            sag = _sig(ag)
            sa = ag * sag
            ov = o_ref[rs, :].astype(F32)
            dyb_in = _mm_nt(dyb, wbra_ref[...])
            do_ref[rs, :] = (dyb_in * sa).astype(BF16)
            dg_ref[rs, 0:D] = (dyb_in * ov * _dsilu(ag, sag)).astype(BF16)
            gt = cgate_ref[rs, :].astype(F32)
            sgt = _sig(gt)
            sgate = gt * sgt
            pw = pw_ref[rs, :].astype(F32)
            dya_in = _mm_nt(dya, wbrc_ref[...])
            dpw = (dya_in * sgate).astype(BF16)
            dg_ref[rs, 3 * D:4 * D] = (dya_in * pw * _dsilu(gt, sgt)).astype(BF16)
            yn = y_ref[rs, :].astype(F32)
            n = yn * g + lnb_ref[...]
            sn = _sig(n)
            dn = _mm_nt(dpw, wpw_ref[...]) * _dsilu(n, sn)
            dy = dn * g
            dc = rstd_ref[rs, :] * (dy - jnp.mean(dy, axis=-1, keepdims=True)
                                    - yn * jnp.mean(dy * yn, axis=-1, keepdims=True))
            dc_ref[rs, :] = dc.astype(BF16)
            sums = (jnp.sum(dn * yn, axis=0, keepdims=True), jnp.sum(dn, axis=0, keepdims=True),
                    jnp.sum(dc, axis=0, keepdims=True))
            return ((ov * sa).astype(BF16), dyb, (pw * sgate).astype(BF16), dya, (n * sn).astype(BF16), dpw,
                    sums)

        parts = [part(pl.ds(r * (tm // TAIL_PARTS), tm // TAIL_PARTS)) for r in range(TAIL_PARTS)]
        cat = lambda j: jnp.concatenate([pt[j] for pt in parts], axis=0)
        acc_bra[...] += _mm_tn(cat(0), cat(1))
        acc_brc[...] += _mm_tn(cat(2), cat(3))
        acc_pw[...] += _mm_tn(cat(4), cat(5))
        for j in range(3):
            gvec_ref[j:j + 1, :] += sum(pt[6][j] for pt in parts)

        @pl.when(i == last)
        def _():
            _flush_to_pack(acc_pw, gpack_ref, 0, sem.at[0])
            _flush_to_pack(acc_brc, gpack_ref, D, sem.at[1])
            _flush_to_pack(acc_bra, gpack_ref, 2 * D, sem.at[2])
            _flush_to_pack(gppt_ref, gpack_ref, WPP0, sem.at[0])

    tile = pl.BlockSpec((tm, D), lambda i: (i, 0))
    ztile = lambda j: pl.BlockSpec((tm, D), lambda i: (i, j))
    wsq = lambda k: pl.BlockSpec((D, D), lambda i: (k, 0))
    const = lambda shp: pl.BlockSpec(shp, lambda i: (0, 0))
    any_spec = pl.BlockSpec(memory_space=pl.ANY)
    return pl.pallas_call(
        body, name="tail_b", grid=(T // tm,),
        in_specs=[tile, tile, tile, tile, ztile(ZB_AGATE), ztile(ZB_GCONV), ztile(ZB_GATTN), ztile(ZB_CGATE),
                  tile, tile, pl.BlockSpec((tm, 1), lambda i: (i, 0)), const((1, D)), const((1, D)), wsq(0),
                  wsq(1), wsq(2), const((PLE, D)), any_spec],
        out_specs=(pl.BlockSpec((tm, 4 * D), lambda i: (i, 0)), tile, tile, const((8, D)), any_spec),
        out_shape=(jax.ShapeDtypeStruct((T, 7 * D), BF16), jax.ShapeDtypeStruct((T, D), BF16),
                   jax.ShapeDtypeStruct((T, D), BF16), jax.ShapeDtypeStruct((8, D), F32),
                   jax.ShapeDtypeStruct(gpack.shape, F32)),
        input_output_aliases={17: 4},
        scratch_shapes=[pltpu.VMEM((D, D), F32), pltpu.VMEM((D, D), F32), pltpu.VMEM((D, D), F32),
                        pltpu.SemaphoreType.DMA((3,))],
        compiler_params=_params(("arbitrary",)),
    )(dm, ya, yb, o, z, z, z, z, pw, y, rstd, ln_g, ln_b, wall, wall, wall, gppt, gpack)


def _conv_bwd(dc, z, wdw, dz, S, tm, copies, src, landing):
    T = dc.shape[0]
    nt = S // tm
    hb = tm // HALO
    nrows = T // HALO

    def body(dc_ref, hdc_ref, cv_ref, cg_ref, hcv_ref, hcg_ref, wdw_ref, dz_in, src_ref, dz_ref, gw_ref,
             land_ref, ubuf, dcbuf, dubuf, dwacc, shbuf, send_sems, recv_sems):
        b = pl.program_id(0)
        t = pl.program_id(1)

        @pl.when((b == 0) & (t == 0))
        def _():
            dwacc[...] = jnp.zeros_like(dwacc)
            for cp in copies(src_ref, land_ref, send_sems, recv_sems):
                cp.start()

        cv = cv_ref[...].astype(F32)
        sg = _sig(cg_ref[...].astype(F32))
        ubuf[HALO:HALO + tm, :] = cv * sg
        hu = hcv_ref[...].astype(F32) * _sig(hcg_ref[...].astype(F32))
        ubuf[0:HALO, :] = jnp.where(t > 0, hu, 0.0)
        ubuf[HALO + tm:HALO + tm + 8, :] = jnp.zeros((8, D), F32)
        dcbuf[0:tm, :] = dc_ref[...].astype(F32)
        dcbuf[tm:tm + HALO, :] = jnp.where(t < nt - 1, hdc_ref[...].astype(F32), 0.0)
        dcbuf[tm + HALO:tm + HALO + 8, :] = jnp.zeros((8, D), F32)

        def chunk(ci, carry):
            r0 = pl.multiple_of(ci * CONV_RC, CONV_RC)
            for lg in range(D // CONV_LC):
                l0 = lg * CONV_LC
                dubuf[pl.ds(r0, CONV_RC), pl.ds(l0, CONV_LC)] = _conv_taps(
                    wdw_ref, dcbuf, r0, l0, lambda k: CONV_K - 1 - k)
                dcc = dcbuf[pl.ds(r0, CONV_RC), pl.ds(l0, CONV_LC)]
                zero8 = jnp.zeros((8, CONV_LC), F32)
                dcz = jnp.concatenate([zero8, dcc, zero8], axis=0)
                for bb in range(8):
                    taps = [k for k in range(CONV_K) if (HALO - (CONV_K - 1) + k) % 8 == bb]
                    if not taps:
                        continue
                    rows = CONV_RC + (8 if bb else 0)
                    if bb:
                        shbuf[bb] = dcz[8 - bb:8 - bb + rows]
                    for k in taps:
                        a8 = HALO - (CONV_K - 1) + k - bb
                        dcs = shbuf[bb] if bb else dcc
                        prod = dcs * ubuf[pl.ds(r0 + a8, rows), pl.ds(l0, CONV_LC)]
                        part = prod[0:8]
                        for q in range(1, rows // 8):
                            part = part + prod[8 * q:8 * q + 8]
                        dwacc[8 * k:8 * k + 8, pl.ds(l0, CONV_LC)] += part
            return carry

        lax.fori_loop(0, tm // CONV_RC, chunk, 0)
        du = dubuf[...]
        dz_ref[:, 0:D] = (du * sg).astype(BF16)
        dz_ref[:, D:2 * D] = (du * cv * sg * (1.0 - sg)).astype(BF16)

        @pl.when((b == pl.num_programs(0) - 1) & (t == nt - 1))
        def _():
            for k in range(32):
                gw_ref[k:k + 1, :] = jnp.sum(dwacc[8 * k:8 * k + 8, :], axis=0, keepdims=True)
            cps = copies(src_ref, land_ref, send_sems, recv_sems)
            for cp in cps:
                cp.wait_recv()
            for cp in cps:
                cp.wait_send()

    def row(b, t):
        return b * nt + t

    def prev_halo(b, t):
        return jnp.maximum(row(b, t) * hb - 1, 0)

    def next_halo(b, t):
        return jnp.minimum((row(b, t) + 1) * hb, nrows - 1)

    return pl.pallas_call(
        body, name="conv_bwd", grid=(T // S, nt),
        in_specs=[pl.BlockSpec((tm, D), lambda b, t: (row(b, t), 0)),
                  pl.BlockSpec((HALO, D), lambda b, t: (next_halo(b, t), 0)),
                  pl.BlockSpec((tm, D), lambda b, t: (row(b, t), ZB_CVAL)),
                  pl.BlockSpec((tm, D), lambda b, t: (row(b, t), ZB_CGLU)),
                  pl.BlockSpec((HALO, D), lambda b, t: (prev_halo(b, t), ZB_CVAL)),
                  pl.BlockSpec((HALO, D), lambda b, t: (prev_halo(b, t), ZB_CGLU)),
                  pl.BlockSpec((32, D), lambda b, t: (0, 0)),
                  pl.BlockSpec(memory_space=pl.ANY), pl.BlockSpec(memory_space=pl.ANY)],
        out_specs=(pl.BlockSpec((tm, 2 * D), lambda b, t: (row(b, t), ZB_CVAL // 2)),
                   pl.BlockSpec((32, D), lambda b, t: (0, 0)), pl.BlockSpec(memory_space=pl.ANY)),
        out_shape=(jax.ShapeDtypeStruct(dz.shape, BF16), jax.ShapeDtypeStruct((32, D), F32), landing),
        input_output_aliases={7: 0},
        scratch_shapes=[pltpu.VMEM((tm + HALO + 8, D), F32), pltpu.VMEM((tm + HALO + 8, D), F32),
                        pltpu.VMEM((tm, D), F32), pltpu.VMEM((8 * 32, D), F32),
                        pltpu.VMEM((8, CONV_RC + 8, CONV_LC), F32), pltpu.SemaphoreType.DMA((3,)),
                        pltpu.SemaphoreType.DMA((3,))],
        compiler_params=_params(("arbitrary", "arbitrary")),
    )(dc, dc, z, z, z, z, wdw, dz, src)


def _attn_bwd(z, zkv, o, do, cos_t, sin_t, sinks, dz, S, tq, copies, src, landing):
    T = z.shape[0]
    nt = S // tq
    nq = tq // BLOCK

    def body(sink_ref, q_ref, kv_ref, hkv_ref, o_ref, do_ref, cos_ref, sin_ref, hcos_ref, hsin_ref, dz_in,
             src_ref, dq_ref, dkv_ref, gs_ref, land_ref, carry, dkacc, dvacc, send_sems, recv_sems):
        b = pl.program_id(0)
        tt = pl.program_id(1)
        t = nt - 1 - tt

        @pl.when((b == 0) & (tt == 0))
        def _():
            gs_ref[...] = jnp.zeros_like(gs_ref)
            for cp in copies(src_ref, land_ref, send_sems, recv_sems):
                cp.start()

        @pl.when(tt == 0)
        def _():
            carry[...] = jnp.zeros_like(carry)

        cos = cos_ref[...]
        sin = sin_ref[...]
        pswap = _swap_matrix()
        kv = jnp.concatenate([hkv_ref[...], kv_ref[...]], axis=0)
        cos_k = jnp.concatenate([hcos_ref[...], cos], axis=0)
        sin_k = jnp.concatenate([hsin_ref[...], sin], axis=0)
        kx = _kv_variants(_rope(kv[:, :BLOCK], cos_k, sin_k, pswap))
        vx = _kv_variants(kv[:, BLOCK:].astype(F32))
        band, sj = _band_mask(4)
        lo = lax.broadcasted_iota(jnp.int32, (4 * BLOCK, BLOCK), 1) < HEAD_DIM
        ones = jnp.ones((2 * BLOCK, 2 * BLOCK), BF16)
        qs = [(_rope(q_ref[:, 128 * hp:128 * hp + 128], cos, sin, pswap) * 0.125).astype(BF16)
              for hp in range(8)]
        dkacc[...] = jnp.zeros_like(dkacc)
        dvacc[...] = jnp.zeros_like(dvacc)
        gsum = jnp.zeros((1, BLOCK), F32)
        hlane = lax.broadcasted_iota(jnp.int32, (1, BLOCK), 1)
        for n in range(nq):
            first = (t == 0) & (n == 0)
            valid = band & (jnp.logical_not(first) | (sj >= BLOCK))
            r0 = n * BLOCK
            for g in range(2):
                cols = [slice(128 * (4 * g + j), 128 * (4 * g + j) + 128) for j in range(4)]
                lhs = jnp.concatenate([qs[4 * g + j][r0:r0 + BLOCK] for j in range(4)], axis=0)
                dov = jnp.concatenate([do_ref[r0:r0 + BLOCK, cs] for cs in cols], axis=0)
                prod = dov.astype(F32) * jnp.concatenate(
                    [o_ref[r0:r0 + BLOCK, cs] for cs in cols], axis=0).astype(F32)
                lhs_t = lhs.T
                dov_t = dov.T
                dq = jnp.zeros((4 * BLOCK, BLOCK), F32)
                dk_t = jnp.zeros((HEAD_DIM, 2 * BLOCK), F32)
                dv_t = jnp.zeros((HEAD_DIM, 2 * BLOCK), F32)
                for e in range(2):
                    kw = kx[g][e][r0:r0 + 2 * BLOCK]
                    vw = vx[g][e][r0:r0 + 2 * BLOCK]
                    s = _mm_nt(lhs, kw)
                    p, psink = _softmax_sink(s, valid, _sink_rep(sink_ref, g, e))
                    pe = jnp.where(lo if e == 0 else jnp.logical_not(lo), prod, 0.0)
                    pe_hi = pe.astype(BF16)
                    pe_lo = (pe - pe_hi.astype(F32)).astype(BF16)
                    delta = _mm(jnp.concatenate([pe_hi, pe_lo], axis=1), ones)
                    ds = (p * (_mm_nt(dov, vw) - delta)).astype(BF16)
                    dq = dq + _mm(ds, kw)
                    dims = slice(HEAD_DIM * e, HEAD_DIM * (e + 1))
                    dk_t = dk_t + _mm(lhs_t[dims], ds)
                    dv_t = dv_t + _mm(dov_t[dims], p.astype(BF16))
                    gs = -psink * delta[:, 0:BLOCK]
                    for j in range(4):
                        tot = jnp.sum(gs[j * BLOCK:(j + 1) * BLOCK], axis=0, keepdims=True)
                        gsum = gsum + jnp.where(hlane == 8 * g + 2 * j + e, tot, 0.0)
                dkacc[HEAD_DIM * g:HEAD_DIM * (g + 1), r0:r0 + 2 * BLOCK] += dk_t
                dvacc[HEAD_DIM * g:HEAD_DIM * (g + 1), r0:r0 + 2 * BLOCK] += dv_t
                for j in range(4):
                    dqj = _rope_f32(dq[j * BLOCK:(j + 1) * BLOCK] * 0.125, cos[r0:r0 + BLOCK],
                                    -sin[r0:r0 + BLOCK], pswap)
                    dq_ref[r0:r0 + BLOCK, cols[j]] = dqj.astype(BF16)
        gs_ref[0:1, :] += gsum
        dk_all = dkacc[...]
        dv_all = dvacc[...]
        dk_last = dk_all[:, tq:tq + BLOCK] + carry[0:BLOCK, :]
        dv_last = dv_all[:, tq:tq + BLOCK] + carry[BLOCK:2 * BLOCK, :]
        carry[0:BLOCK, :] = dk_all[:, 0:BLOCK]
        carry[BLOCK:2 * BLOCK, :] = dv_all[:, 0:BLOCK]
        if nq > 1:
            dk_tile = jnp.concatenate([dk_all[:, BLOCK:tq], dk_last], axis=1)
            dv_tile = jnp.concatenate([dv_all[:, BLOCK:tq], dv_last], axis=1)
        else:
            dk_tile, dv_tile = dk_last, dv_last
        dkv_ref[:, 0:BLOCK] = _rope_f32(dk_tile.T, cos, -sin, pswap).astype(BF16)
        dkv_ref[:, BLOCK:2 * BLOCK] = dv_tile.T.astype(BF16)

        @pl.when((b == pl.num_programs(0) - 1) & (tt == nt - 1))
        def _():
            cps = copies(src_ref, land_ref, send_sems, recv_sems)
            for cp in cps:
                cp.wait_recv()
            for cp in cps:
                cp.wait_send()

    def row(b, tt):
        return b * nt + (nt - 1 - tt)

    def halo(b, tt):
        return jnp.maximum(row(b, tt) * nq - 1, 0)

    tile = pl.BlockSpec((tq, D), lambda b, tt: (row(b, tt), 0))
    return pl.pallas_call(
        body, name="attn_bwd", grid=(T // S, nt),
        in_specs=[pl.BlockSpec(memory_space=pltpu.SMEM),
                  pl.BlockSpec((tq, D), lambda b, tt: (row(b, tt), ZB_Q)),
                  pl.BlockSpec((tq, 2 * BLOCK), lambda b, tt: (row(b, tt), 0)),
                  pl.BlockSpec((BLOCK, 2 * BLOCK), lambda b, tt: (halo(b, tt), 0)),
                  tile, tile,
                  pl.BlockSpec((tq, BLOCK), lambda b, tt: (row(b, tt), 0)),
                  pl.BlockSpec((tq, BLOCK), lambda b, tt: (row(b, tt), 0)),
                  pl.BlockSpec((BLOCK, BLOCK), lambda b, tt: (halo(b, tt), 0)),
                  pl.BlockSpec((BLOCK, BLOCK), lambda b, tt: (halo(b, tt), 0)),
                  pl.BlockSpec(memory_space=pl.ANY), pl.BlockSpec(memory_space=pl.ANY)],
        out_specs=(pl.BlockSpec((tq, D), lambda b, tt: (row(b, tt), ZB_Q)),
                   pl.BlockSpec((tq, 2 * BLOCK), lambda b, tt: (row(b, tt), 0)),
                   pl.BlockSpec((8, BLOCK), lambda b, tt: (0, 0)), pl.BlockSpec(memory_space=pl.ANY)),
        out_shape=(jax.ShapeDtypeStruct(dz.shape, BF16), jax.ShapeDtypeStruct((T, 2 * BLOCK), BF16),
                   jax.ShapeDtypeStruct((8, BLOCK), F32), landing),
        input_output_aliases={10: 0},
        scratch_shapes=[pltpu.VMEM((2 * BLOCK, BLOCK), F32), pltpu.VMEM((BLOCK, tq + BLOCK), F32),
                        pltpu.VMEM((BLOCK, tq + BLOCK), F32), pltpu.SemaphoreType.DMA((3,)),
                        pltpu.SemaphoreType.DMA((3,))],
        compiler_params=_params(("arbitrary", "arbitrary")),
    )(sinks, z, zkv, zkv, o, do, cos_t, sin_t, cos_t, sin_t, dz, src)


def _dh(dz, dz_kv, wall, x, dx1, ln_pre, tm, tile0, ntiles, gx_prev, name, copies, src, landing):
    T = x.shape[0]
    nsem = 3

    def body(*refs):
        dz_ref, kv_ref, wt_ref, x_ref, dx1_ref, g_ref, src_ref = refs[:7]
        gx_ref, glp_ref, land_ref, wbuf, send_sems, recv_sems, wsem = refs[-7:]
        i = pl.program_id(0)

        @pl.when(i == 0)
        def _():
            glp_ref[...] = jnp.zeros_like(glp_ref)
            for cp in copies(src_ref, land_ref, send_sems, recv_sems):
                cp.start()
            load = pltpu.make_async_copy(wt_ref, wbuf, wsem)
            load.start()
            load.wait()

        dh = _mm(dz_ref[...], wbuf[0:ZKV, :]) + _mm(kv_ref[...], wbuf[ZKV:IN_WIDTH, :])
        xv = x_ref[...]
        r = lax.rsqrt(jnp.mean(xv * xv, axis=-1, keepdims=True) + EPS)
        xr = xv * r
        glp_ref[...] += jnp.sum(dh * xr, axis=0, keepdims=True)
        a = dh * g_ref[...]
        gx_ref[...] = dx1_ref[...] + r * (a - xr * jnp.mean(a * xr, axis=-1, keepdims=True))

        @pl.when(i == ntiles - 1)
        def _():
            cps = copies(src_ref, land_ref, send_sems, recv_sems)
            for cp in cps:
                cp.wait_recv()
            for cp in cps:
                cp.wait_send()

    tile = pl.BlockSpec((tm, D), lambda i: (tile0 + i, 0))
    any_spec = pl.BlockSpec(memory_space=pl.ANY)
    operands = [dz, dz_kv, wall, x, dx1, ln_pre, src] + ([] if gx_prev is None else [gx_prev])
    return pl.pallas_call(
        body, name=name, grid=(ntiles,),
        in_specs=[pl.BlockSpec((tm, ZKV), lambda i: (tile0 + i, 0)),
                  pl.BlockSpec((tm, 2 * BLOCK), lambda i: (tile0 + i, 0)),
                  any_spec, tile, tile, pl.BlockSpec((1, D), lambda i: (0, 0)), any_spec]
        + ([] if gx_prev is None else [any_spec]),
        out_specs=(tile, pl.BlockSpec((1, D), lambda i: (0, 0)), any_spec),
        out_shape=(jax.ShapeDtypeStruct((T, D), F32), jax.ShapeDtypeStruct((1, D), F32), landing),
        input_output_aliases={} if gx_prev is None else {7: 0},
        scratch_shapes=[pltpu.VMEM((IN_WIDTH, D), BF16), pltpu.SemaphoreType.DMA((nsem,)),
                        pltpu.SemaphoreType.DMA((nsem,)), pltpu.SemaphoreType.DMA],
        compiler_params=_params(("arbitrary",)),
    )(*operands)


def _gwt(dz, dz_kv, h, tt):
    T = dz.shape[0]
    nt = T // tt
    last = nt - 1
    kv = 2 * BLOCK

    def body(dz_ref, dzkv_ref, h_ref, gpack_ref, acc, sem):
        j = pl.program_id(0)
        t = pl.program_id(1)

        @pl.when((j < 7) & (t == 0))
        def _():
            acc[...] = _mm_tn(dz_ref[...], h_ref[...])

        @pl.when((j < 7) & (t > 0))
        def _():
            acc[...] += _mm_tn(dz_ref[...], h_ref[...])

        @pl.when((j == 7) & (t == 0))
        def _():
            acc[0:kv, :] = _mm_tn(dzkv_ref[...], h_ref[...])

        @pl.when((j == 7) & (t > 0))
        def _():
            acc[0:kv, :] += _mm_tn(dzkv_ref[...], h_ref[...])

        for jj in range(7):
            @pl.when((t == last) & (j == jj))
            def _(jj=jj):
                _flush_to_pack(acc, gpack_ref, WT0 + jj * D, sem)

        @pl.when((t == last) & (j == 7))
        def _():
            _flush_to_pack(acc.at[pl.ds(0, kv)], gpack_ref, WT0 + ZKV, sem)

    return pl.pallas_call(
        body, name="gwt", grid=(8, nt),
        in_specs=[pl.BlockSpec((tt, D), lambda j, t: (jnp.where(j == 7, last, t), jnp.minimum(j, 6))),
                  pl.BlockSpec((tt, kv), lambda j, t: (jnp.where(j == 7, t, 0), 0)),
                  pl.BlockSpec((tt, D), lambda j, t: (t, 0))],
        out_specs=pl.BlockSpec(memory_space=pl.ANY),
        out_shape=jax.ShapeDtypeStruct((N_SHARDS, WIN_SHARD, D), F32),
        scratch_shapes=[pltpu.VMEM((D, D), F32), pltpu.SemaphoreType.DMA],
        compiler_params=_params(("arbitrary", "arbitrary")),
    )(dz, dz_kv, h)


_BC1 = 1.0 - ADAM_B1 ** ADAM_STEP
_BC2 = 1.0 - ADAM_B2 ** ADAM_STEP


def _adamw_math(w, g, m, v):
    m = ADAM_B1 * m + (1.0 - ADAM_B1) * g
    v = ADAM_B2 * v + (1.0 - ADAM_B2) * (g * g)
    delta = -ADAM_LR * ((m / _BC1) / (jnp.sqrt(v / _BC2) + ADAM_EPS) + ADAM_WD * w)
    return delta, m, v


def _adamw_rows(g, w, m, v, rows, name):
    R, C = w.shape

    def body(g_ref, w_ref, m_ref, v_ref, go_ref, d_ref, nm_ref, nv_ref):
        gv = g_ref[...]
        d, nm, nv = _adamw_math(w_ref[...], gv, m_ref[...], v_ref[...])
        go_ref[...] = gv
        d_ref[...] = d
        nm_ref[...] = nm
        nv_ref[...] = nv

    spec = pl.BlockSpec((rows, C), lambda i: (i, 0))
    shp = jax.ShapeDtypeStruct((R, C), F32)
    return pl.pallas_call(
        body, name=name, grid=(R // rows,), in_specs=[spec] * 4, out_specs=(spec,) * 4,
        out_shape=(shp,) * 4, compiler_params=_params(("arbitrary",)),
    )(g, w, m, v)


def _adamw_square(gfin, ws, ms, vs):
    rb = 64
    nb = SQ_SHARD // rb

    def body(*refs):
        g_refs = refs[0:5]
        w_refs, m_refs, v_refs = refs[5:10], refs[10:15], refs[15:20]
        outs = refs[20:]
        for k in range(5):
            gk = g_refs[k][...]
            d, nm, nv = _adamw_math(w_refs[k][...], gk, m_refs[k][...], v_refs[k][...])
            outs[4 * k][...] = gk
            outs[4 * k + 1][...] = d
            outs[4 * k + 2][...] = nm
            outs[4 * k + 3][...] = nv

    spec = pl.BlockSpec((rb, D), lambda i: (i, 0))
    gspecs = [pl.BlockSpec((rb, D), lambda i, k=k: (SQ_SHARD * k // rb + i, 0)) for k in range(5)]
    shp = jax.ShapeDtypeStruct((SQ_SHARD, D), F32)
    res = pl.pallas_call(
        body, name="adamw_square", grid=(nb,), in_specs=gspecs + [spec] * 15, out_specs=(spec,) * 20,
        out_shape=(shp,) * 20, compiler_params=_params(("arbitrary",)),
    )(*([gfin] * 5), *ws, *ms, *vs)
    return [tuple(res[4 * k:4 * k + 4]) for k in range(5)]


def _adamw_small(gs, ws, ms, vs):
    n = len(gs)

    def body(*refs):
        outs = refs[4 * n:]
        for k in range(n):
            d, nm, nv = _adamw_math(refs[n + k][...], refs[k][...], refs[2 * n + k][...],
                                    refs[3 * n + k][...])
            outs[3 * k][...] = d
            outs[3 * k + 1][...] = nm
            outs[3 * k + 2][...] = nv

    vm = pl.BlockSpec(memory_space=pltpu.VMEM)
    shapes = []
    for w in ws:
        shapes += [jax.ShapeDtypeStruct(w.shape, F32)] * 3
    res = pl.pallas_call(
        body, name="adamw_small", in_specs=[vm] * (4 * n), out_specs=(vm,) * (3 * n),
        out_shape=tuple(shapes),
    )(*gs, *ws, *ms, *vs)
    return [tuple(res[3 * k:3 * k + 3]) for k in range(n)]


def _rope_constants():
    half = ROPE_DIM // 2
    inv = jnp.power(ROPE_THETA, -jnp.arange(0, ROPE_DIM, 2, dtype=F32) / ROPE_DIM)
    freq = jnp.concatenate([inv, jnp.zeros((ROPE_ROWS - half,), F32)]).reshape(ROPE_ROWS, 1)
    spread = np.zeros((3, ROPE_ROWS, BLOCK), np.float32)
    for lane in range(BLOCK):
        d = lane % HEAD_DIM
        if d < ROPE_DIM:
            spread[0, d % half, lane] = 1.0
            spread[1, d % half, lane] = -1.0 if d < half else 1.0
        else:
            spread[2, 0, lane] = 1.0
    return freq, jnp.asarray(spread, BF16)


def kernel(x, p, positions, w_in, ln_pre, ln_post, w_dw, b_dw, conv_ln_g, conv_ln_b, w_pw, sinks, w_br_conv, w_br_attn, w_out, w_ple_gate, w_ple_proj, loss_target, m_w_in, m_ln_pre, m_ln_post, m_w_dw, m_b_dw, m_conv_ln_g, m_conv_ln_b, m_w_pw, m_sinks, m_w_br_conv, m_w_br_attn, m_w_out, m_w_ple_gate, m_w_ple_proj, v_w_in, v_ln_pre, v_ln_post, v_w_dw, v_b_dw, v_conv_ln_g, v_conv_ln_b, v_w_pw, v_sinks, v_w_br_conv, v_w_br_attn, v_w_out, v_w_ple_gate, v_w_ple_proj):
    nb, S, _ = x.shape
    T = nb * S
    xc = lax.axis_index("x")
    yc = lax.axis_index("y")
    cc = lax.axis_index("c")
    shard = 2 * xc + yc

    sq_w = (w_pw, w_br_conv, w_br_attn, w_out, w_ple_gate)
    wdw_shard = jnp.pad(w_dw[0], ((0, 1), (0, 0)))
    x2 = x.reshape(T, D)
    tm_res = min(TILE_RESIDENT, T // 2)
    h, cos_t, sin_t = _prenorm(x2, ln_pre, positions.astype(F32).reshape(1, T), *_rope_constants(), tm_res)

    tgt = loss_target.reshape(T, D)
    p2 = p.reshape(T, PLE)
    sinks1 = sinks.reshape(N_HEADS)

    tm = min(TILE_TOKEN, S)
    tq = min(TILE_ATTN, S)

    z, zkv, wt, wdw_all, wall, wppf = _inproj(
        h, w_in[0].T.astype(BF16), wdw_shard, [w[0].astype(BF16) for w in sq_w],
        w_ple_proj[0].T.reshape(WPP_SHARD, D).astype(BF16), min(TILE_PROJ, T // 2))
    wdw = jnp.concatenate([wdw_all[s] for s in range(N_SHARDS)], axis=1)
    wppt = wppf.reshape(D, PLE)
    ya, y, rstd, pw = _conv_fwd(z, wdw, b_dw, conv_ln_g, conv_ln_b, wall, S, tm)
    o = _attn_fwd(z, zkv, cos_t, sin_t, sinks1, S, tq)
    loss_p, dx1, dm, yb, g_ln_post, gsq, gw_ppt = _tail_a(x2, tgt, p2, o, ya, z, ln_post, wall, wppt, tm)

    cidx = jnp.reshape(cc, (1,)).astype(jnp.int32)
    scidx = jnp.stack([shard, cc]).astype(jnp.int32)

    def landing(pack, n, dtype):
        return jax.ShapeDtypeStruct((n, pack.shape[1] // 2, D), dtype)

    dz, do, dc, gvec, gsq = _tail_b(dm, ya, yb, o, z, pw, y, rstd, conv_ln_g, conv_ln_b, wall,
                                    gw_ppt.reshape(PLE, D), gsq, tm)
    dz, g_wdw, r1_sq = _conv_bwd(dc, z, wdw, dz, S, tm, _exchange_copies, gsq, landing(gsq, N_SHARDS, F32))
    cs_sq = _chip_sum(cidx, gsq, r1_sq, "chip_sum_sq")
    dz, dkv, g_sinks, r2_sq = _attn_bwd(z, zkv, o, do, cos_t, sin_t, sinks1, dz, S, tq, _chip_sum_copies, cs_sq,
                                        landing(gsq, 3, BF16))
    gwt_pack = _gwt(dz, dkv, h, min(2 * TILE_PROJ, T))

    tm_dh = tm_res
    n_dh = T // tm_dh
    n_a = max(1, n_dh // 4)
    gx, g_ln_pre_a, r1_wt = _dh(
        dz, dkv, wt, x2, dx1, ln_pre, tm_dh, 0, n_a, None, "dh_exchange", _exchange_copies, gwt_pack,
        landing(gwt_pack, N_SHARDS, F32))
    cs_wt = _chip_sum(cidx, gwt_pack, r1_wt, "chip_sum_wt")
    gx, g_ln_pre_b, r2_wt = _dh(
        dz, dkv, wt, x2, dx1, ln_pre, tm_dh, n_a, n_dh - n_a, gx, "dh_send", _chip_sum_copies, cs_wt,
        landing(gwt_pack, 3, BF16))
    g_ln_pre = g_ln_pre_a + g_ln_pre_b
    row37 = jnp.concatenate([g_sinks[0:1, 0:N_HEADS], loss_p, jnp.zeros((1, D - N_HEADS - 1), F32)], axis=1)
    vec = jnp.concatenate([g_wdw, g_ln_pre, g_ln_post, gvec[2:3], gvec[0:1], gvec[1:2], row37,
                           jnp.zeros((VEC_ROWS - 38, D), F32)], axis=0)
    gfin_wt, gfin_sq, tot = _finish_reduce(_final_half(scidx, gwt_pack, r1_wt, r2_wt, "final_half_wt"),
                                           _final_half(scidx, gsq, r1_sq, r2_sq, "final_half_sq"), vec)

    g_w_in, d_w_in, nm_w_in, nv_w_in = [a.T for a in _adamw_rows(
        gfin_wt, w_in[0].T, m_w_in[0].T, v_w_in[0].T, WIN_SHARD // 8, "adamw_w_in")]
    g_w_in = g_w_in[None]
    sq_m = (m_w_pw, m_w_br_conv, m_w_br_attn, m_w_out, m_w_ple_gate)
    sq_v = (v_w_pw, v_w_br_conv, v_w_br_attn, v_w_out, v_w_ple_gate)
    sq_res = _adamw_square(gfin_sq, [w[0] for w in sq_w], [m[0] for m in sq_m], [v[0] for v in sq_v])
    g_wpp = gfin_sq[5 * SQ_SHARD:SQ_PACK].reshape(PLE, PLE).T
    g_dw_all = tot[0:CONV_K]
    g_dw = lax.dynamic_slice_in_dim(g_dw_all, shard * PLE, PLE, axis=1)
    small_g = [g_wpp, g_dw, tot[32:33], tot[33:34], tot[34:35], tot[35:36], tot[36:37],
               tot[37:38, 0:N_HEADS]]
    small_w = [w_ple_proj[0], w_dw[0], ln_pre, ln_post, b_dw, conv_ln_g, conv_ln_b, sinks]
    small_m = [m_w_ple_proj[0], m_w_dw[0], m_ln_pre, m_ln_post, m_b_dw, m_conv_ln_g, m_conv_ln_b, m_sinks]
    small_v = [v_w_ple_proj[0], v_w_dw[0], v_ln_pre, v_ln_post, v_b_dw, v_conv_ln_g, v_conv_ln_b, v_sinks]
    small = _adamw_small(small_g, small_w, small_m, small_v)

    loss = tot[37, N_HEADS]
    grads = [g_w_in, small_g[2], small_g[3], g_dw[None], small_g[4], small_g[5], small_g[6],
             sq_res[0][0][None], small_g[7], sq_res[1][0][None], sq_res[2][0][None], sq_res[3][0][None],
             sq_res[4][0][None], g_wpp[None]]

    def triple(i):
        w_in_t = (d_w_in[None], nm_w_in[None], nv_w_in[None])
        sq = lambda k: tuple(a[None] for a in sq_res[k][1:4])
        sm = lambda k, lead: tuple(a[None] if lead else a for a in small[k])
        return [w_in_t[i], sm(2, False)[i], sm(3, False)[i], sm(1, True)[i], sm(4, False)[i],
                sm(5, False)[i], sm(6, False)[i], sq(0)[i], sm(7, False)[i], sq(1)[i], sq(2)[i], sq(3)[i],
                sq(4)[i], sm(0, True)[i]]

    return (loss, gx.reshape(nb, S, D), *grads, *triple(0), *triple(1), *triple(2))
```

```python
import functools

import jax
import jax.numpy as jnp
import numpy as np
from jax import lax
from jax.experimental import pallas as pl
from jax.experimental.pallas import tpu as pltpu

F32 = jnp.float32
BF16 = jnp.bfloat16

D = 1024
PLE = 256
N_HEADS = 16
HEAD_DIM = 64
BLOCK = 128
CONV_K = 31
ROPE_DIM = 16
ROPE_THETA = 500000.0
EPS = 1e-6
IN_WIDTH = 7424
N_SHARDS = 4

ADAM_LR = 0.001
ADAM_B1 = 0.9
ADAM_B2 = 0.999
ADAM_EPS = 1e-08
ADAM_WD = 0.01
ADAM_STEP = 10

SQ_NAMES = ("w_pw", "w_br_conv", "w_br_attn", "w_out", "w_ple_gate")
WT0 = 5 * D
WPP0 = WT0 + IN_WIDTH
WALL_ROWS = WPP0 + PLE
WIN_SHARD = IN_WIDTH // N_SHARDS
SQ_SHARD = D // N_SHARDS
WPP_SHARD = PLE * PLE // D
PACK_ROWS = WIN_SHARD + 5 * SQ_SHARD + WPP_SHARD
HALF_ROWS = PACK_ROWS // 2
VMEM_LIMIT = 56 * 1024 * 1024
MESH = pl.DeviceIdType.MESH
TILE_RESIDENT = 512
TILE_PROJ = 1024
TILE_TOKEN = 256
TILE_ATTN = 512
TAIL_PARTS = 1


ZB_AGATE, ZB_GCONV, ZB_GATTN, ZB_CGATE, ZB_CVAL, ZB_CGLU, ZB_Q = range(7)
ZKV = 7 * D
_SEGMENTS = ((0, D, ZB_CVAL * D), (D, D, ZB_CGLU * D), (2 * D, D, ZB_CGATE * D), (3 * D, D, ZB_Q * D),
             (4 * D, 2 * BLOCK, ZKV), (4 * D + 2 * BLOCK, D, ZB_AGATE * D),
             (5 * D + 2 * BLOCK, D, ZB_GCONV * D), (6 * D + 2 * BLOCK, D, ZB_GATTN * D))
_WT_CUTS = (0, 192, 640, 1216, WIN_SHARD)


def _zp_row(o):
    for a, w, zp in _SEGMENTS:
        if a <= o < a + w:
            return zp + o - a
    raise ValueError(o)


def _pieces(s):
    out = []
    for a, b in zip(_WT_CUTS[:-1], _WT_CUTS[1:]):
        first = _zp_row(WIN_SHARD * s + a)
        assert _zp_row(WIN_SHARD * s + b - 1) == first + b - a - 1
        out.append((a, b - a, WT0 + first))
    for k in range(5):
        out.append((WIN_SHARD + SQ_SHARD * k, SQ_SHARD, D * k + SQ_SHARD * s))
    out.append((WIN_SHARD + 5 * SQ_SHARD, WPP_SHARD, WPP0 + WPP_SHARD * s))
    return out


N_PIECES = len(_pieces(0))


def _wall_segments(wall0, rows):
    out = []
    for s in range(N_SHARDS):
        for pr, n, wr in _pieces(s):
            lo, hi = max(wr, wall0), min(wr + n, wall0 + rows)
            if lo < hi:
                out.append((lo - wall0, hi - lo, s, pr + lo - wr))
    assert sum(n for _, n, _, _ in out) == rows
    return out


def _sel(s, vals):
    r = jnp.int32(vals[0])
    for i in range(1, len(vals)):
        r = jnp.where(s == i, jnp.int32(vals[i]), r)
    return r


def _sig(x):
    return 1.0 / (1.0 + jnp.exp(-x))


def _mm(a, b):
    return lax.dot_general(a, b, (((1,), (0,)), ((), ())), preferred_element_type=F32)


def _mm_nt(a, b):
    return lax.dot_general(a, b, (((1,), (1,)), ((), ())), preferred_element_type=F32)


def _mm_tn(a, b):
    return lax.dot_general(a, b, (((0,), (0,)), ((), ())), preferred_element_type=F32)


def _params(sem=None):
    return pltpu.CompilerParams(dimension_semantics=sem, vmem_limit_bytes=VMEM_LIMIT)


def _flush_to_pack(acc_ref, gpack_ref, wall0, sem):
    base = 0 if gpack_ref.shape[1] == WIN_SHARD else WIN_SHARD
    for r, n, s, pr in _wall_segments(wall0, acc_ref.shape[0]):
        assert 0 <= pr - base and pr - base + n <= gpack_ref.shape[1]
        cp = pltpu.make_async_copy(acc_ref.at[pl.ds(r, n)], gpack_ref.at[s, pl.ds(pr - base, n)], sem)
        cp.start()
        cp.wait()


def _coords():
    return lax.axis_index("x"), lax.axis_index("y"), lax.axis_index("c")


def _chip_peers(x, y):
    return [(1 - x, y), (x, 1 - y), (1 - x, 1 - y)]


WIN_PIECES = tuple(range(len(_WT_CUTS) - 1))
SQ_PIECES = tuple(range(len(WIN_PIECES), N_PIECES))


def _gather_ops(group, src, landing, bytes_ref, stage, send_sems, recv_sems, loc_sem):
    sizes = [_pieces(0)[p][1] for p in group]
    half_rows = sum(n // 2 for n in sizes)

    def rcopy(a, b, k, dev):
        return pltpu.make_async_remote_copy(src_ref=a, dst_ref=b, send_sem=send_sems.at[k],
                                            recv_sem=recv_sems.at[k], device_id=dev, device_id_type=MESH)

    def total(k):
        x, y, c = _coords()
        rows = bytes_ref.at[pl.ds(0, half_rows)]
        return rcopy(rows, rows, k, (x, y, c))

    def send():
        x, y, c = _coords()
        s_me = 2 * x + y
        for k, (px, py) in enumerate(_chip_peers(x, y)):
            for p, n in zip(group, sizes):
                h = n // 2
                rcopy(src(p, c * h, h), landing(p, s_me, c * h, h), k, (px, py, c)).start()
        for p, n in zip(group, sizes):
            for a, b in ((src(p, 0, n), stage.at[pl.ds(0, n)]), (stage.at[pl.ds(0, n)], landing(p, s_me, 0, n))):
                cp = pltpu.make_async_copy(a, b, loc_sem)
                cp.start()
                cp.wait()

    def forward():
        x, y, c = _coords()
        for k, (px, py) in enumerate(_chip_peers(x, y)):
            total(k).wait_recv()
            for p, n in zip(group, sizes):
                rows = landing(p, 2 * px + py, c * (n // 2), n // 2)
                rcopy(rows, rows, 3 + k, (x, y, 1 - c)).start()

    def finish():
        for k in range(3):
            total(3 + k).wait_recv()
        for k in range(6):
            total(k).wait_send()

    return send, forward, finish


def _piece_rows(ref, start, off, n):
    first = start + off
    return ref.at[pl.ds(first if isinstance(first, int) else pl.multiple_of(first, 32), n)]


GROUP_CONV = SQ_PIECES[0:2]
GROUP_TAIL = SQ_PIECES[2:]


def _group_shapes(group):
    n_sq = sum(1 for q in group if q != N_PIECES - 1)
    return [jax.ShapeDtypeStruct((n_sq * D, D), BF16)] + (
        [jax.ShapeDtypeStruct((PLE, D), BF16)] if N_PIECES - 1 in group else [])


def _group_scratch():
    return [pltpu.VMEM((SQ_SHARD, D), BF16), pltpu.SemaphoreType.DMA((6,)), pltpu.SemaphoreType.DMA((6,)),
            pltpu.SemaphoreType.DMA]


def _group_gather(group, shard_refs, out_refs, scratch, step, n_steps):
    wall_ref = out_refs[0]
    stage, send_sems, recv_sems, loc_sem = scratch

    def src(q, off, n):
        return _piece_rows(shard_refs[group.index(q)], 0, off, n)

    def landing(q, s, off, n):
        if q == N_PIECES - 1:
            return _piece_rows(out_refs[1], WPP_SHARD * s, off, n)
        return _piece_rows(wall_ref, D * group.index(q) + SQ_SHARD * s, off, n)

    send, forward, finish = _gather_ops(group, src, landing, wall_ref, stage, send_sems, recv_sems, loc_sem)
    pl.when(step == 0)(send)
    pl.when(step == n_steps // 2)(forward)
    return finish


ROPE_ROWS = 16


def _prenorm(x, ln_pre, pos, freq, spread, tm):
    T = x.shape[0]

    def to_lanes(v, e):
        out = None
        for _ in range(3):
            part = v.astype(BF16)
            term = _mm_tn(part, e)
            out = term if out is None else out + term
            v = v - part.astype(F32)
        return out

    def body(x_ref, g_ref, pos_ref, f_ref, e_ref, h_ref, cos_ref, sin_ref):
        xv = x_ref[...]
        r = lax.rsqrt(jnp.mean(xv * xv, axis=-1, keepdims=True) + EPS)
        h_ref[...] = (xv * r * g_ref[...]).astype(BF16)
        ang = f_ref[...] * pos_ref[...]
        cos_ref[...] = to_lanes(jnp.cos(ang), e_ref[0]) + e_ref[2, 0:1, :].astype(F32)
        sin_ref[...] = to_lanes(jnp.sin(ang), e_ref[1])

    return pl.pallas_call(
        body, name="prenorm", grid=(T // tm,),
        out_shape=(jax.ShapeDtypeStruct((T, D), BF16), jax.ShapeDtypeStruct((T, BLOCK), F32),
                   jax.ShapeDtypeStruct((T, BLOCK), F32)),
        in_specs=[pl.BlockSpec((tm, D), lambda i: (i, 0)), pl.BlockSpec((1, D), lambda i: (0, 0)),
                  pl.BlockSpec((1, tm), lambda i: (0, i)), pl.BlockSpec((ROPE_ROWS, 1), lambda i: (0, 0)),
                  pl.BlockSpec((3, ROPE_ROWS, BLOCK), lambda i: (0, 0, 0))],
        out_specs=(pl.BlockSpec((tm, D), lambda i: (i, 0)), pl.BlockSpec((tm, BLOCK), lambda i: (i, 0)),
                   pl.BlockSpec((tm, BLOCK), lambda i: (i, 0))),
        compiler_params=_params(("arbitrary",)),
    )(x, ln_pre, pos, freq, spread)


SQ_PACK = PACK_ROWS - WIN_SHARD


def _row_tile(half):
    return max(t for t in range(8, 321, 8) if half % t == 0)


def _exchange_copies(g_ref, r1_ref, send_sems, recv_sems):
    x, y, c = _coords()
    half = g_ref.shape[1] // 2
    return [pltpu.make_async_remote_copy(
        src_ref=g_ref.at[:, pl.ds(pl.multiple_of((1 - c) * half, 32), half), :], dst_ref=r1_ref,
        send_sem=send_sems.at[0], recv_sem=recv_sems.at[0], device_id=(x, y, 1 - c), device_id_type=MESH)]


def _chip_sum_copies(cs_ref, r2_ref, send_sems, recv_sems):
    x, y, c = _coords()
    return [pltpu.make_async_remote_copy(
        src_ref=cs_ref.at[2 * px + py], dst_ref=r2_ref.at[k], send_sem=send_sems.at[k],
        recv_sem=recv_sems.at[k], device_id=(px, py, c), device_id_type=MESH)
        for k, (px, py) in enumerate(_chip_peers(x, y))]


def _chip_sum(cidx, gpack, r1, name):
    half = gpack.shape[1] // 2
    rt = _row_tile(half)

    def body(c_ref, g_ref, r_ref, o_ref):
        o_ref[...] = (g_ref[...] + r_ref[...]).astype(BF16)

    nt = half // rt
    return pl.pallas_call(
        body, name=name,
        grid_spec=pltpu.PrefetchScalarGridSpec(
            num_scalar_prefetch=1, grid=(N_SHARDS, nt),
            in_specs=[pl.BlockSpec((1, rt, D), lambda s, t, c: (s, c[0] * nt + t, 0)),
                      pl.BlockSpec((1, rt, D), lambda s, t, c: (s, t, 0))],
            out_specs=pl.BlockSpec((1, rt, D), lambda s, t, c: (s, t, 0))),
        out_shape=jax.ShapeDtypeStruct((N_SHARDS, half, D), BF16),
        compiler_params=_params(("arbitrary", "arbitrary")),
    )(cidx, gpack, r1)


def _final_half(sc, gpack, r1, r2, name):
    rows = gpack.shape[1]
    half = rows // 2
    rt = _row_tile(half)

    def body(sc_ref, g_ref, r_ref, p_ref, o_ref):
        acc = g_ref[0] + r_ref[0]
        for k in range(3):
            acc = acc + p_ref[k].astype(F32)
        o_ref[...] = acc

    nt = half // rt
    return pl.pallas_call(
        body, name=name,
        grid_spec=pltpu.PrefetchScalarGridSpec(
            num_scalar_prefetch=1, grid=(nt,),
            in_specs=[pl.BlockSpec((1, rt, D), lambda t, sc: (sc[0], sc[1] * nt + t, 0)),
                      pl.BlockSpec((1, rt, D), lambda t, sc: (sc[0], t, 0)),
                      pl.BlockSpec((3, rt, D), lambda t, sc: (0, t, 0))],
            out_specs=pl.BlockSpec((rt, D), lambda t, sc: (sc[1] * nt + t, 0))),
        out_shape=jax.ShapeDtypeStruct((rows, D), F32),
        compiler_params=_params(("arbitrary",)),
    )(sc, gpack, r1, r2)


VEC_ROWS = 40


def _finish_reduce(fwt, fsq, vec):
    def body(fwt_ref, fsq_ref, v_ref, owt_ref, osq_ref, tot_ref, buf, send_sems, recv_sems):
        x, y, c = _coords()
        swaps = []
        for k, (f_ref, o_ref) in enumerate(((fwt_ref, owt_ref), (fsq_ref, osq_ref))):
            half = f_ref.shape[0] // 2
            rows = pl.ds(pl.multiple_of(c * half, 32), half)
            swaps.append(pltpu.make_async_remote_copy(
                src_ref=f_ref.at[rows], dst_ref=o_ref.at[rows], send_sem=send_sems.at[7 + k],
                recv_sem=recv_sems.at[7 + k], device_id=(x, y, 1 - c), device_id_type=MESH))
        for cp in swaps:
            cp.start()
        me = 4 * x + 2 * y + c
        buf[me] = v_ref[...]
        cps = []
        for r in range(1, 8):
            dx, dy, dc = (r >> 2) & 1, (r >> 1) & 1, r & 1
            peer = (1 - x if dx else x, 1 - y if dy else y, 1 - c if dc else c)
            cp = pltpu.make_async_remote_copy(
                src_ref=v_ref, dst_ref=buf.at[me], send_sem=send_sems.at[r - 1],
                recv_sem=recv_sems.at[r - 1], device_id=peer, device_id_type=MESH)
            cp.start()
            cps.append(cp)
        for cp in cps:
            cp.wait_recv()
        for cp in cps:
            cp.wait_send()
        acc = buf[0]
        for d in range(1, 8):
            acc = acc + buf[d]
        tot_ref[...] = acc
        for cp in swaps:
            cp.wait()

    any_spec = pl.BlockSpec(memory_space=pl.ANY)
    vm = pl.BlockSpec(memory_space=pltpu.VMEM)
    return pl.pallas_call(
        body, name="finish_reduce",
        out_shape=(jax.ShapeDtypeStruct(fwt.shape, F32), jax.ShapeDtypeStruct(fsq.shape, F32),
                   jax.ShapeDtypeStruct((VEC_ROWS, D), F32)),
        in_specs=[any_spec, any_spec, vm], out_specs=(any_spec, any_spec, vm),
        input_output_aliases={0: 0, 1: 1},
        scratch_shapes=[pltpu.VMEM((8, VEC_ROWS, D), F32), pltpu.SemaphoreType.DMA((9,)),
                        pltpu.SemaphoreType.DMA((9,))],
    )(fwt, fsq, vec)


SOLO_ROWS = WIN_SHARD - BLOCK // 2


def _solo_first(s):
    return 0 if s % 2 == 0 else BLOCK // 2


def _solo_segments(s):
    lo = _solo_first(s)
    out = []
    for a, n, wr in _pieces(s)[:len(WIN_PIECES)]:
        b0, b1 = max(a, lo), min(a + n, lo + SOLO_ROWS)
        if b0 >= b1:
            continue
        z0 = wr - WT0 + b0 - a
        if out and out[-1][0] + out[-1][1] == b0 - lo and out[-1][2] + out[-1][1] == z0:
            out[-1] = (out[-1][0], out[-1][1] + b1 - b0, out[-1][2])
        else:
            out.append((b0 - lo, b1 - b0, z0))
    out = [r for o, n, z0 in out for r in
           (((o, ZKV - z0, z0), (o + ZKV - z0, z0 + n - ZKV, ZKV)) if z0 < ZKV < z0 + n else ((o, n, z0),))]
    assert all(v % BLOCK == 0 for seg in out for v in seg) and sum(n for _, n, _ in out) == SOLO_ROWS
    return out


def _shared_tile(pair):
    z0 = _zp_row(WIN_SHARD * (2 * pair) + SOLO_ROWS)
    assert z0 % BLOCK == 0 and _zp_row(WIN_SHARD * (2 * pair + 1)) == z0 + BLOCK // 2
    return z0


def _inproj(h, win_t, wdw_shard, tm):
    T = h.shape[0]
    n_t = T // tm
    assert n_t >= 2
    tables = [[_pieces(s)[p][2] - WT0 for s in range(N_SHARDS)] for p in WIN_PIECES]
    sizes = [_pieces(0)[p][1] for p in WIN_PIECES]
    half_rows = sum(n // 2 for n in sizes)
    relation_of_pass = {1: 1, 2: 0, 3: 2}

    def body(h_ref, win_ref, wdw_ref, z_ref, zkv_ref, wt_ref, wdwall_ref, wbuf, stage, stage_sh, wsend, wrecv,
             loc_sems, out_sems, sh_sems):
        p = pl.program_id(0)
        t = pl.program_id(1)
        x, y, c = _coords()
        s_me = 2 * x + y
        peers = _chip_peers(x, y)
        shard = jnp.bitwise_xor(s_me, p)
        first, last = t == 0, t == n_t - 1

        def rcopy(a, b, k, dev):
            return pltpu.make_async_remote_copy(src_ref=a, dst_ref=b, send_sem=wsend.at[k], recv_sem=wrecv.at[k],
                                                device_id=dev, device_id_type=MESH)

        def total(k):
            rows = wt_ref.at[pl.ds(0, half_rows)]
            return rcopy(rows, rows, k, (x, y, c))

        def in_hbm(q, s, off, n):
            return _piece_rows(wt_ref, _sel(s, tables[q]), off, n)

        def in_vmem(q, s):
            return _piece_rows(wbuf, WIN_SHARD * s + _WT_CUTS[q], 0, sizes[q])

        def send_to(k):
            px, py = peers[k]
            for q, n in zip(WIN_PIECES, sizes):
                rcopy(_piece_rows(win_ref, _WT_CUTS[q], c * (n // 2), n // 2), in_hbm(q, s_me, c * (n // 2), n // 2),
                      k, (px, py, c)).start()

        def forward_from(k):
            px, py = peers[k]
            total(k).wait_recv()
            for q, n in zip(WIN_PIECES, sizes):
                rows = in_hbm(q, 2 * px + py, c * (n // 2), n // 2)
                rcopy(rows, rows, 3 + k, (x, y, 1 - c)).start()
            total(3 + k).wait_recv()

        def shard_total(a, b, sem):
            return pltpu.make_async_copy(a.at[pl.ds(0, WIN_SHARD)], b.at[pl.ds(0, WIN_SHARD)], sem)

        def wdw_copies():
            return [pltpu.make_async_remote_copy(
                src_ref=wdw_ref, dst_ref=wdwall_ref.at[s_me], send_sem=wsend.at[6 + k], recv_sem=wrecv.at[6 + k],
                device_id=(px, py, c), device_id_type=MESH) for k, (px, py) in enumerate(peers)]

        def own_wdw():
            return pltpu.make_async_copy(wdw_ref, wdwall_ref.at[s_me], loc_sems.at[2])

        @pl.when((p == 0) & first)
        def _():
            send_to(0)
            send_to(1)
            own_wdw().start()
            for cp in wdw_copies():
                cp.start()
            for q in WIN_PIECES:
                pltpu.make_async_copy(_piece_rows(win_ref, _WT_CUTS[q], 0, sizes[q]), in_vmem(q, s_me),
                                      loc_sems.at[0]).start()
            shard_total(win_ref, wbuf, loc_sems.at[0]).wait()
            for q in WIN_PIECES:
                pltpu.make_async_copy(in_vmem(q, s_me), in_hbm(q, s_me, 0, sizes[q]), loc_sems.at[1]).start()

        for pp, k in relation_of_pass.items():
            @pl.when((p == pp - 1) & last)
            def _(k=k):
                forward_from(k)
                px, py = peers[k]
                for q in WIN_PIECES:
                    pltpu.make_async_copy(in_hbm(q, 2 * px + py, 0, sizes[q]), in_vmem(q, 2 * px + py),
                                          loc_sems.at[0]).start()

            @pl.when((p == pp) & first)
            def _(pp=pp):
                shard_total(wt_ref, wbuf, loc_sems.at[0]).wait()
                if pp == 1:
                    total(0).wait_send()
                    total(1).wait_send()
                    send_to(2)

        step = p * n_t + t
        slot = step % 2
        rows = pl.ds(pl.multiple_of(t * tm, tm), tm)

        def out_total(sl):
            return pltpu.make_async_copy(stage.at[sl], stage.at[sl], out_sems.at[sl])

        def sh_copy(sl, z0):
            return pltpu.make_async_copy(stage_sh.at[sl], z_ref.at[rows, pl.ds(z0, BLOCK)], sh_sems.at[sl])

        @pl.when(step >= 2)
        def _():
            out_total(slot).wait()

        @pl.when((step >= 2) & (((step - 2) // n_t) % 2 == 1))
        def _():
            sh_copy(slot, 0).wait()

        solo0 = pl.multiple_of(WIN_SHARD * shard + (BLOCK // 2) * (shard % 2), BLOCK // 2)
        stage[slot] = _mm_nt(h_ref[...], wbuf[pl.ds(solo0, SOLO_ROWS), :]).astype(BF16)
        for s in range(N_SHARDS):
            @pl.when(shard == s)
            def _(s=s):
                for off, n, z0 in _solo_segments(s):
                    dst = zkv_ref.at[rows] if z0 == ZKV else z_ref.at[rows, pl.ds(z0, n)]
                    pltpu.make_async_copy(stage.at[slot, :, pl.ds(off, n)], dst, out_sems.at[slot]).start()

        @pl.when(p % 2 == 1)
        def _():
            pair = shard // 2
            w0 = pl.multiple_of(2 * WIN_SHARD * pair + SOLO_ROWS, BLOCK // 2)
            z0 = pl.multiple_of(jnp.where(pair == 0, _shared_tile(0), _shared_tile(1)), BLOCK)
            stage_sh[slot] = _mm_nt(h_ref[...], wbuf[pl.ds(w0, BLOCK), :]).astype(BF16)
            sh_copy(slot, z0).start()

        @pl.when((p == 3) & last)
        def _():
            for k in (2, 3, 4, 5):
                total(k).wait_send()
            shard_total(wbuf, wt_ref, loc_sems.at[1]).wait()
            cps = wdw_copies()
            for cp in cps:
                cp.wait_recv()
            for cp in cps:
                cp.wait_send()
            own_wdw().wait()
            for sl in range(2):
                out_total(sl).wait()
                sh_copy(sl, 0).wait()

    any_spec = pl.BlockSpec(memory_space=pl.ANY)
    return pl.pallas_call(
        body, name="inproj", grid=(N_SHARDS, n_t),
        in_specs=[pl.BlockSpec((tm, D), lambda p, t: (t, 0)), any_spec, any_spec],
        out_specs=(any_spec,) * 4,
        out_shape=(jax.ShapeDtypeStruct((T, ZKV), BF16), jax.ShapeDtypeStruct((T, 2 * BLOCK), BF16),
                   jax.ShapeDtypeStruct((IN_WIDTH, D), BF16), jax.ShapeDtypeStruct((N_SHARDS, 32, PLE), F32)),
        scratch_shapes=[pltpu.VMEM((IN_WIDTH, D), BF16), pltpu.VMEM((2, tm, SOLO_ROWS), BF16),
                        pltpu.VMEM((2, tm, BLOCK), BF16),
                        pltpu.SemaphoreType.DMA((9,)), pltpu.SemaphoreType.DMA((9,)),
                        pltpu.SemaphoreType.DMA((3,)), pltpu.SemaphoreType.DMA((2,)),
                        pltpu.SemaphoreType.DMA((2,))],
        compiler_params=_params(("arbitrary", "arbitrary")),
    )(h, win_t, wdw_shard)


HALO = 32
CONV_RC = 64
CONV_LC = 256


def _conv_taps(w_ref, src, r0, lane0, offset_of_tap):
    lanes = pl.ds(lane0, CONV_LC)
    out = None
    for b in range(8):
        taps = [k for k in range(CONV_K) if offset_of_tap(k) % 8 == b]
        if not taps:
            continue
        rows = CONV_RC + (8 if b else 0)
        vb = None
        for k in taps:
            term = w_ref[k:k + 1, lanes] * src[pl.ds(r0 + (offset_of_tap(k) - b), rows), lanes]
            vb = term if vb is None else vb + term
        vb = vb[b:b + CONV_RC] if b else vb
        out = vb if out is None else out + vb
    return out


def _conv_fwd(z, wdw, b_dw, ln_g, ln_b, wall, S, tm, group, shards):
    T = z.shape[0]
    nt = S // tm
    hb = tm // HALO
    gathered = _group_shapes(group)

    def body(cv_ref, cg_ref, cgate_ref, hcv_ref, hcg_ref, wdw_ref, bdw_ref, lng_ref, lnb_ref, wpw_ref,
             wbrc_ref, *rest):
        shard_refs, rest = rest[:len(group)], rest[len(group):]
        ya_ref, y_ref, rstd_ref, pw_ref = rest[:4]
        gather_refs, (ubuf, cbuf), gather_scratch = rest[4:4 + len(gathered)], rest[-6:-4], rest[-4:]
        t = pl.program_id(1)
        step = pl.program_id(0) * nt + t
        finish_gather = _group_gather(group, shard_refs, gather_refs, gather_scratch, step, T // tm)
        ubuf[HALO:HALO + tm, :] = cv_ref[...].astype(F32) * _sig(cg_ref[...].astype(F32))
        hu = hcv_ref[...].astype(F32) * _sig(hcg_ref[...].astype(F32))
        ubuf[0:HALO, :] = jnp.where(t > 0, hu, 0.0)
        ubuf[HALO + tm:HALO + tm + 8, :] = jnp.zeros((8, D), F32)

        def chunk(ci, carry):
            r0 = pl.multiple_of(ci * CONV_RC, CONV_RC)
            for lg in range(D // CONV_LC):
                acc = _conv_taps(wdw_ref, ubuf, r0, lg * CONV_LC, lambda k: HALO - (CONV_K - 1) + k)
                cbuf[pl.ds(r0, CONV_RC), pl.ds(lg * CONV_LC, CONV_LC)] = acc
            return carry

        lax.fori_loop(0, tm // CONV_RC, chunk, 0)
        cc = cbuf[...] + bdw_ref[...]
        mu = jnp.mean(cc, axis=-1, keepdims=True)
        dd = cc - mu
        rstd = lax.rsqrt(jnp.mean(dd * dd, axis=-1, keepdims=True) + EPS)
        yn = dd * rstd
        y_ref[...] = yn.astype(BF16)
        rstd_ref[...] = rstd
        n = yn * lng_ref[...] + lnb_ref[...]
        s = n * _sig(n)
        pw = _mm(s.astype(BF16), wpw_ref[...])
        pw_ref[...] = pw.astype(BF16)
        gt = cgate_ref[...].astype(F32)
        ya_in = pw * (gt * _sig(gt))
        ya_ref[...] = _mm(ya_in.astype(BF16), wbrc_ref[...]).astype(BF16)
        pl.when(step == T // tm - 1)(finish_gather)

    def row(b, t):
        return b * nt + t

    def halo(b, t):
        return jnp.maximum(row(b, t) * hb - 1, 0)

    vec = pl.BlockSpec((1, D), lambda b, t: (0, 0))
    tile = lambda j: pl.BlockSpec((tm, D), lambda b, t: (row(b, t), j))
    out_tile = pl.BlockSpec((tm, D), lambda b, t: (row(b, t), 0))
    any_spec = pl.BlockSpec(memory_space=pl.ANY)
    return pl.pallas_call(
        body, name="conv_fwd", grid=(T // S, nt),
        in_specs=[tile(ZB_CVAL), tile(ZB_CGLU), tile(ZB_CGATE),
                  pl.BlockSpec((HALO, D), lambda b, t: (halo(b, t), ZB_CVAL)),
                  pl.BlockSpec((HALO, D), lambda b, t: (halo(b, t), ZB_CGLU)),
                  pl.BlockSpec((32, D), lambda b, t: (0, 0)), vec, vec, vec,
                  pl.BlockSpec((D, D), lambda b, t: (0, 0)),
                  pl.BlockSpec((D, D), lambda b, t: (1, 0))] + [any_spec] * len(group),
        out_specs=(out_tile, out_tile, pl.BlockSpec((tm, 1), lambda b, t: (row(b, t), 0)), out_tile)
        + (any_spec,) * len(gathered),
        out_shape=[jax.ShapeDtypeStruct((T, D), BF16), jax.ShapeDtypeStruct((T, D), BF16),
                   jax.ShapeDtypeStruct((T, 1), F32), jax.ShapeDtypeStruct((T, D), BF16)] + gathered,
        scratch_shapes=[pltpu.VMEM((tm + HALO + 8, D), F32), pltpu.VMEM((tm, D), F32)] + _group_scratch(),
        compiler_params=_params(("arbitrary", "arbitrary")),
    )(z, z, z, z, z, wdw, b_dw, ln_g, ln_b, wall, wall, *shards)


def _swap_matrix():
    r = lax.broadcasted_iota(jnp.int32, (BLOCK, BLOCK), 0)
    l = lax.broadcasted_iota(jnp.int32, (BLOCK, BLOCK), 1)
    lh = l & (HEAD_DIM - 1)
    half = ROPE_DIM // 2
    hit = ((lh < half) & (r == l + half)) | ((lh >= half) & (lh < ROPE_DIM) & (r == l - half))
    return jnp.where(hit, 1.0, 0.0).astype(BF16)


def _rope(tb, cos, sin, pswap):
    return tb.astype(F32) * cos + _mm(tb, pswap) * sin


def _rope_f32(tv, cos, sin, pswap):
    hi = tv.astype(BF16)
    lo = (tv - hi.astype(F32)).astype(BF16)
    return tv * cos + (_mm(hi, pswap) + _mm(lo, pswap)) * sin


def _kv_variants(kv):
    lane = lax.broadcasted_iota(jnp.int32, kv.shape, 1)
    lo = lane < HEAD_DIM
    sw = pltpu.roll(kv, HEAD_DIM, 1)
    z = jnp.zeros_like(kv)
    g0 = (jnp.where(lo, kv, z).astype(BF16), jnp.where(lo, z, sw).astype(BF16))
    g1 = (jnp.where(lo, sw, z).astype(BF16), jnp.where(lo, z, kv).astype(BF16))
    return (g0, g1)


def _band_mask(nq):
    qi = lax.broadcasted_iota(jnp.int32, (nq * BLOCK, 2 * BLOCK), 0) & (BLOCK - 1)
    sj = lax.broadcasted_iota(jnp.int32, (nq * BLOCK, 2 * BLOCK), 1)
    return (sj <= qi + BLOCK) & (sj > qi), sj


def _sink_rep(sink_ref, g, e):
    return jnp.concatenate(
        [jnp.full((BLOCK, BLOCK), sink_ref[8 * g + 2 * j + e], F32) for j in range(4)], axis=0)


def _softmax_parts(s, valid, sk):
    rows = s.shape[0]
    s = jnp.where(valid, s, -1e30)
    m = jnp.maximum(jnp.broadcast_to(jnp.max(s, axis=-1, keepdims=True), (rows, BLOCK)), sk)
    return jnp.exp(s - jnp.concatenate([m, m], axis=1)), jnp.exp(sk - m)


def _softmax_sink(s, valid, sk):
    p, ps = _softmax_parts(s, valid, sk)
    inv = 1.0 / (_mm(p.astype(BF16), jnp.ones((2 * BLOCK, BLOCK), BF16)) + ps)
    return p * jnp.concatenate([inv, inv], axis=1), ps * inv


def _attn_fwd(z, zkv, cos_t, sin_t, sinks, S, tq, group, shards):
    T = z.shape[0]
    nt = S // tq
    nq = tq // BLOCK
    gathered = _group_shapes(group)

    def body(sink_ref, q_ref, kv_ref, hkv_ref, cos_ref, sin_ref, hcos_ref, hsin_ref, *rest):
        shard_refs, o_ref = rest[:len(group)], rest[len(group)]
        t = pl.program_id(1)
        step = pl.program_id(0) * nt + t
        finish_gather = _group_gather(group, shard_refs, rest[len(group) + 1:-4], rest[-4:], step, T // tq)
        cos = cos_ref[...]
        sin = sin_ref[...]
        pswap = _swap_matrix()
        kv = jnp.concatenate([hkv_ref[...], kv_ref[...]], axis=0)
        cos_k = jnp.concatenate([hcos_ref[...], cos], axis=0)
        sin_k = jnp.concatenate([hsin_ref[...], sin], axis=0)
        kx = _kv_variants(_rope(kv[:, :BLOCK], cos_k, sin_k, pswap))
        one = jnp.ones((tq + BLOCK, BLOCK), BF16)
        vx = [[jnp.concatenate([v, one], axis=1) for v in vg] for vg in _kv_variants(kv[:, BLOCK:].astype(F32))]
        band, sj = _band_mask(4)
        qs = [(_rope(q_ref[:, 128 * hp:128 * hp + 128], cos, sin, pswap) * 0.125).astype(BF16)
              for hp in range(8)]
        for n in range(nq):
            first = (t == 0) & (n == 0)
            valid = band & (jnp.logical_not(first) | (sj >= BLOCK))
            r0 = n * BLOCK
            for g in range(2):
                lhs = jnp.concatenate([qs[4 * g + j][r0:r0 + BLOCK] for j in range(4)], axis=0)
                acc = jnp.zeros((4 * BLOCK, BLOCK), F32)
                for e in range(2):
                    s = _mm_nt(lhs, kx[g][e][r0:r0 + 2 * BLOCK])
                    p, ps = _softmax_parts(s, valid, _sink_rep(sink_ref, g, e))
                    r = _mm(p.astype(BF16), vx[g][e][r0:r0 + 2 * BLOCK])
                    acc = acc + r[:, 0:BLOCK] * (1.0 / (r[:, BLOCK:2 * BLOCK] + ps))
                for j in range(4):
                    o_ref[r0:r0 + BLOCK, 128 * (4 * g + j):128 * (4 * g + j) + 128] = (
                        acc[j * BLOCK:(j + 1) * BLOCK].astype(BF16))
        pl.when(step == T // tq - 1)(finish_gather)

    def row(b, t):
        return b * nt + t

    def halo(b, t):
        return jnp.maximum(row(b, t) * nq - 1, 0)

    any_spec = pl.BlockSpec(memory_space=pl.ANY)
    return pl.pallas_call(
        body, name="attn_fwd", grid=(T // S, nt),
        in_specs=[pl.BlockSpec(memory_space=pltpu.SMEM),
                  pl.BlockSpec((tq, D), lambda b, t: (row(b, t), ZB_Q)),
                  pl.BlockSpec((tq, 2 * BLOCK), lambda b, t: (row(b, t), 0)),
                  pl.BlockSpec((BLOCK, 2 * BLOCK), lambda b, t: (halo(b, t), 0)),
                  pl.BlockSpec((tq, BLOCK), lambda b, t: (row(b, t), 0)),
                  pl.BlockSpec((tq, BLOCK), lambda b, t: (row(b, t), 0)),
                  pl.BlockSpec((BLOCK, BLOCK), lambda b, t: (halo(b, t), 0)),
                  pl.BlockSpec((BLOCK, BLOCK), lambda b, t: (halo(b, t), 0))] + [any_spec] * len(group),
        out_specs=(pl.BlockSpec((tq, D), lambda b, t: (row(b, t), 0)),) + (any_spec,) * len(gathered),
        out_shape=[jax.ShapeDtypeStruct((T, D), BF16)] + gathered,
        scratch_shapes=_group_scratch(),
        compiler_params=_params(("arbitrary", "arbitrary")),
    )(sinks, z, zkv, zkv, cos_t, sin_t, cos_t, sin_t, *shards)


def _tail_a(x, tgt, p, o, ya, z, ln_post, wall_b, wppt, tm):
    T = x.shape[0]
    last = T // tm - 1

    def body(x_ref, tgt_ref, p_ref, o_ref, ya_ref, ag_ref, gc_ref, ga_ref, lnp_ref, wbra_ref, wout_ref,
             wpg_ref, wppt_ref, loss_ref, dx1_ref, dm_ref, yb_ref, glnp_ref, gpack_ref, gwpp_ref,
             acc_out, acc_pg, sem):
        i = pl.program_id(0)

        @pl.when(i == 0)
        def _():
            acc_out[...] = jnp.zeros_like(acc_out)
            acc_pg[...] = jnp.zeros_like(acc_pg)
            gwpp_ref[...] = jnp.zeros_like(gwpp_ref)
            glnp_ref[...] = jnp.zeros_like(glnp_ref)
            loss_ref[...] = jnp.zeros_like(loss_ref)

        ag = ag_ref[...].astype(F32)
        yb_in = (o_ref[...].astype(F32) * (ag * _sig(ag))).astype(BF16)
        yb = _mm(yb_in, wbra_ref[...])
        yb_ref[...] = yb.astype(BF16)
        m = (_sig(gc_ref[...].astype(F32)) * ya_ref[...].astype(F32)
             + _sig(ga_ref[...].astype(F32)) * yb).astype(BF16)
        mo = _mm(m, wout_ref[...])
        r2 = lax.rsqrt(jnp.mean(mo * mo, axis=-1, keepdims=True) + EPS)
        nrm = mo * r2
        g_post = lnp_ref[...]
        x1 = x_ref[...] + nrm * g_post
        x1b = x1.astype(BF16)
        gate = _sig(_mm(x1b, wpg_ref[...]))
        pb = p_ref[...].astype(BF16)
        pp = _mm_nt(pb, wppt_ref[...])
        err = x1 + gate * pp - tgt_ref[...]
        loss_ref[...] += 0.5 * jnp.sum(jnp.sum(err * err, axis=-1, keepdims=True) * (1.0 / D),
                                       axis=0, keepdims=True)
        dx2 = err * (1.0 / D)
        dgp = (dx2 * pp * gate * (1.0 - gate)).astype(BF16)
        dpp = (dx2 * gate).astype(BF16)
        dx1 = dx2 + _mm_nt(dgp, wpg_ref[...])
        dx1_ref[...] = dx1
        acc_pg[...] += _mm_tn(x1b, dgp)
        gwpp_ref[...] += _mm_tn(dpp, pb)
        glnp_ref[...] += jnp.sum(dx1 * nrm, axis=0, keepdims=True)
        a = dx1 * g_post
        dmo = (r2 * (a - nrm * jnp.mean(a * nrm, axis=-1, keepdims=True))).astype(BF16)
        dm_ref[...] = _mm_nt(dmo, wout_ref[...]).astype(BF16)
        acc_out[...] += _mm_tn(m, dmo)

        @pl.when(i == last)
        def _():
            _flush_to_pack(acc_out, gpack_ref, 3 * D, sem.at[0])
            _flush_to_pack(acc_pg, gpack_ref, 4 * D, sem.at[1])

    tile = pl.BlockSpec((tm, D), lambda i: (i, 0))
    ztile = lambda j: pl.BlockSpec((tm, D), lambda i: (i, j))
    wsq = lambda k: pl.BlockSpec((D, D), lambda i: (k, 0))
    const = lambda shp: pl.BlockSpec(shp, lambda i: (0, 0))
    any_spec = pl.BlockSpec(memory_space=pl.ANY)
    return pl.pallas_call(
        body, name="tail_a", grid=(T // tm,),
        in_specs=[tile, tile, pl.BlockSpec((tm, PLE), lambda i: (i, 0)), tile, tile, ztile(ZB_AGATE),
                  ztile(ZB_GCONV), ztile(ZB_GATTN), const((1, D)), wsq(0), wsq(1), wsq(2), const((D, PLE))],
        out_specs=(const((1, 1)), tile, tile, tile, const((1, D)), any_spec, const((D, PLE))),
        out_shape=(jax.ShapeDtypeStruct((1, 1), F32), jax.ShapeDtypeStruct((T, D), F32),
                   jax.ShapeDtypeStruct((T, D), BF16), jax.ShapeDtypeStruct((T, D), BF16),
                   jax.ShapeDtypeStruct((1, D), F32), jax.ShapeDtypeStruct((N_SHARDS, SQ_PACK, D), F32),
                   jax.ShapeDtypeStruct((D, PLE), F32)),
        scratch_shapes=[pltpu.VMEM((D, D), F32), pltpu.VMEM((D, D), F32), pltpu.SemaphoreType.DMA((2,))],
        compiler_params=_params(("arbitrary",)),
    )(x, tgt, p, o, ya, z, z, z, ln_post, wall_b, wall_b, wall_b, wppt)


def _dsilu(v, sg):
    return sg * (1.0 + v * (1.0 - sg))


def _tail_b(dm, ya, yb, o, z, pw, y, rstd, ln_g, ln_b, wall_a, wall_b, gppt, gpack, tm):
    T = dm.shape[0]
    last = T // tm - 1

    def body(dm_ref, ya_ref, yb_ref, o_ref, ag_ref, gc_ref, ga_ref, cgate_ref, pw_ref, y_ref, rstd_ref,
             lng_ref, lnb_ref, wpw_ref, wbrc_ref, wbra_ref, gppt_ref, gpack_in, dg_ref, do_ref, dc_ref,
             gvec_ref, gpack_ref, acc_bra, acc_brc, acc_pw, sem):
        i = pl.program_id(0)

        @pl.when(i == 0)
        def _():
            acc_bra[...] = jnp.zeros_like(acc_bra)
            acc_brc[...] = jnp.zeros_like(acc_brc)
            acc_pw[...] = jnp.zeros_like(acc_pw)
            gvec_ref[...] = jnp.zeros_like(gvec_ref)

        g = lng_ref[...]

        def part(rs):
            dm_v = dm_ref[rs, :].astype(F32)
            sgc = _sig(gc_ref[rs, :].astype(F32))
            sga = _sig(ga_ref[rs, :].astype(F32))
            dya = (dm_v * sgc).astype(BF16)
            dyb = (dm_v * sga).astype(BF16)
            dg_ref[rs, D:2 * D] = (dm_v * ya_ref[rs, :].astype(F32) * sgc * (1.0 - sgc)).astype(BF16)
            dg_ref[rs, 2 * D:3 * D] = (dm_v * yb_ref[rs, :].astype(F32) * sga * (1.0 - sga)).astype(BF16)
            ag = ag_ref[rs, :].astype(F32)
            sag = _sig(ag)
            sa = ag * sag
            ov = o_ref[rs, :].astype(F32)
            dyb_in = _mm_nt(dyb, wbra_ref[...])
            do_ref[rs, :] = (dyb_in * sa).astype(BF16)
            dg_ref[rs, 0:D] = (dyb_in * ov * _dsilu(ag, sag)).astype(BF16)
            gt = cgate_ref[rs, :].astype(F32)
            sgt = _sig(gt)
            sgate = gt * sgt
            pw = pw_ref[rs, :].astype(F32)
            dya_in = _mm_nt(dya, wbrc_ref[...])
            dpw = (dya_in * sgate).astype(BF16)
            dg_ref[rs, 3 * D:4 * D] = (dya_in * pw * _dsilu(gt, sgt)).astype(BF16)
            yn = y_ref[rs, :].astype(F32)
            n = yn * g + lnb_ref[...]
            sn = _sig(n)
            dn = _mm_nt(dpw, wpw_ref[...]) * _dsilu(n, sn)
            dy = dn * g
            dc = rstd_ref[rs, :] * (dy - jnp.mean(dy, axis=-1, keepdims=True)
                                    - yn * jnp.mean(dy * yn, axis=-1, keepdims=True))
            dc_ref[rs, :] = dc.astype(BF16)
            sums = (jnp.sum(dn * yn, axis=0, keepdims=True), jnp.sum(dn, axis=0, keepdims=True),
                    jnp.sum(dc, axis=0, keepdims=True))
            return ((ov * sa).astype(BF16), dyb, (pw * sgate).astype(BF16), dya, (n * sn).astype(BF16), dpw,
                    sums)

        parts = [part(pl.ds(r * (tm // TAIL_PARTS), tm // TAIL_PARTS)) for r in range(TAIL_PARTS)]
        cat = lambda j: jnp.concatenate([pt[j] for pt in parts], axis=0)
        acc_bra[...] += _mm_tn(cat(0), cat(1))
        acc_brc[...] += _mm_tn(cat(2), cat(3))
        acc_pw[...] += _mm_tn(cat(4), cat(5))
        for j in range(3):
            gvec_ref[j:j + 1, :] += sum(pt[6][j] for pt in parts)

        @pl.when(i == last)
        def _():
            _flush_to_pack(acc_pw, gpack_ref, 0, sem.at[0])
            _flush_to_pack(acc_brc, gpack_ref, D, sem.at[1])
            _flush_to_pack(acc_bra, gpack_ref, 2 * D, sem.at[2])
            _flush_to_pack(gppt_ref, gpack_ref, WPP0, sem.at[0])

    tile = pl.BlockSpec((tm, D), lambda i: (i, 0))
    ztile = lambda j: pl.BlockSpec((tm, D), lambda i: (i, j))
    wsq = lambda k: pl.BlockSpec((D, D), lambda i: (k, 0))
    const = lambda shp: pl.BlockSpec(shp, lambda i: (0, 0))
    any_spec = pl.BlockSpec(memory_space=pl.ANY)
    return pl.pallas_call(
        body, name="tail_b", grid=(T // tm,),
        in_specs=[tile, tile, tile, tile, ztile(ZB_AGATE), ztile(ZB_GCONV), ztile(ZB_GATTN), ztile(ZB_CGATE),
                  tile, tile, pl.BlockSpec((tm, 1), lambda i: (i, 0)), const((1, D)), const((1, D)), wsq(0),
                  wsq(1), wsq(0), const((PLE, D)), any_spec],
        out_specs=(pl.BlockSpec((tm, 4 * D), lambda i: (i, 0)), tile, tile, const((8, D)), any_spec),
        out_shape=(jax.ShapeDtypeStruct((T, 7 * D), BF16), jax.ShapeDtypeStruct((T, D), BF16),
                   jax.ShapeDtypeStruct((T, D), BF16), jax.ShapeDtypeStruct((8, D), F32),
                   jax.ShapeDtypeStruct(gpack.shape, F32)),
        input_output_aliases={17: 4},
        scratch_shapes=[pltpu.VMEM((D, D), F32), pltpu.VMEM((D, D), F32), pltpu.VMEM((D, D), F32),
                        pltpu.SemaphoreType.DMA((3,))],
        compiler_params=_params(("arbitrary",)),
    )(dm, ya, yb, o, z, z, z, z, pw, y, rstd, ln_g, ln_b, wall_a, wall_a, wall_b, gppt, gpack)


def _conv_bwd(dc, z, wdw, dz, S, tm, copies, src, landing):
    T = dc.shape[0]
    nt = S // tm
    hb = tm // HALO
    nrows = T // HALO

    def body(dc_ref, hdc_ref, cv_ref, cg_ref, hcv_ref, hcg_ref, wdw_ref, dz_in, src_ref, dz_ref, gw_ref,
             land_ref, ubuf, dcbuf, dubuf, dwacc, shbuf, send_sems, recv_sems):
        b = pl.program_id(0)
        t = pl.program_id(1)

        @pl.when((b == 0) & (t == 0))
        def _():
            dwacc[...] = jnp.zeros_like(dwacc)
            for cp in copies(src_ref, land_ref, send_sems, recv_sems):
                cp.start()

        cv = cv_ref[...].astype(F32)
        sg = _sig(cg_ref[...].astype(F32))
        ubuf[HALO:HALO + tm, :] = cv * sg
        hu = hcv_ref[...].astype(F32) * _sig(hcg_ref[...].astype(F32))
        ubuf[0:HALO, :] = jnp.where(t > 0, hu, 0.0)
        ubuf[HALO + tm:HALO + tm + 8, :] = jnp.zeros((8, D), F32)
        dcbuf[0:tm, :] = dc_ref[...].astype(F32)
        dcbuf[tm:tm + HALO, :] = jnp.where(t < nt - 1, hdc_ref[...].astype(F32), 0.0)
        dcbuf[tm + HALO:tm + HALO + 8, :] = jnp.zeros((8, D), F32)

        def chunk(ci, carry):
            r0 = pl.multiple_of(ci * CONV_RC, CONV_RC)
            for lg in range(D // CONV_LC):
                l0 = lg * CONV_LC
                dubuf[pl.ds(r0, CONV_RC), pl.ds(l0, CONV_LC)] = _conv_taps(
                    wdw_ref, dcbuf, r0, l0, lambda k: CONV_K - 1 - k)
                dcc = dcbuf[pl.ds(r0, CONV_RC), pl.ds(l0, CONV_LC)]
                zero8 = jnp.zeros((8, CONV_LC), F32)
                dcz = jnp.concatenate([zero8, dcc, zero8], axis=0)
                for bb in range(8):
                    taps = [k for k in range(CONV_K) if (HALO - (CONV_K - 1) + k) % 8 == bb]
                    if not taps:
                        continue
                    rows = CONV_RC + (8 if bb else 0)
                    if bb:
                        shbuf[bb] = dcz[8 - bb:8 - bb + rows]
                    for k in taps:
                        a8 = HALO - (CONV_K - 1) + k - bb
                        dcs = shbuf[bb] if bb else dcc
                        prod = dcs * ubuf[pl.ds(r0 + a8, rows), pl.ds(l0, CONV_LC)]
                        part = prod[0:8]
                        for q in range(1, rows // 8):
                            part = part + prod[8 * q:8 * q + 8]
                        dwacc[8 * k:8 * k + 8, pl.ds(l0, CONV_LC)] += part
            return carry

        lax.fori_loop(0, tm // CONV_RC, chunk, 0)
        du = dubuf[...]
        dz_ref[:, 0:D] = (du * sg).astype(BF16)
        dz_ref[:, D:2 * D] = (du * cv * sg * (1.0 - sg)).astype(BF16)

        @pl.when((b == pl.num_programs(0) - 1) & (t == nt - 1))
        def _():
            for k in range(32):
                gw_ref[k:k + 1, :] = jnp.sum(dwacc[8 * k:8 * k + 8, :], axis=0, keepdims=True)
            cps = copies(src_ref, land_ref, send_sems, recv_sems)
            for cp in cps:
                cp.wait_recv()
            for cp in cps:
                cp.wait_send()

    def row(b, t):
        return b * nt + t

    def prev_halo(b, t):
        return jnp.maximum(row(b, t) * hb - 1, 0)

    def next_halo(b, t):
        return jnp.minimum((row(b, t) + 1) * hb, nrows - 1)

    return pl.pallas_call(
        body, name="conv_bwd", grid=(T // S, nt),
        in_specs=[pl.BlockSpec((tm, D), lambda b, t: (row(b, t), 0)),
                  pl.BlockSpec((HALO, D), lambda b, t: (next_halo(b, t), 0)),
                  pl.BlockSpec((tm, D), lambda b, t: (row(b, t), ZB_CVAL)),
                  pl.BlockSpec((tm, D), lambda b, t: (row(b, t), ZB_CGLU)),
                  pl.BlockSpec((HALO, D), lambda b, t: (prev_halo(b, t), ZB_CVAL)),
                  pl.BlockSpec((HALO, D), lambda b, t: (prev_halo(b, t), ZB_CGLU)),
                  pl.BlockSpec((32, D), lambda b, t: (0, 0)),
                  pl.BlockSpec(memory_space=pl.ANY), pl.BlockSpec(memory_space=pl.ANY)],
        out_specs=(pl.BlockSpec((tm, 2 * D), lambda b, t: (row(b, t), ZB_CVAL // 2)),
                   pl.BlockSpec((32, D), lambda b, t: (0, 0)), pl.BlockSpec(memory_space=pl.ANY)),
        out_shape=(jax.ShapeDtypeStruct(dz.shape, BF16), jax.ShapeDtypeStruct((32, D), F32), landing),
        input_output_aliases={7: 0},
        scratch_shapes=[pltpu.VMEM((tm + HALO + 8, D), F32), pltpu.VMEM((tm + HALO + 8, D), F32),
                        pltpu.VMEM((tm, D), F32), pltpu.VMEM((8 * 32, D), F32),
                        pltpu.VMEM((8, CONV_RC + 8, CONV_LC), F32), pltpu.SemaphoreType.DMA((3,)),
                        pltpu.SemaphoreType.DMA((3,))],
        compiler_params=_params(("arbitrary", "arbitrary")),
    )(dc, dc, z, z, z, z, wdw, dz, src)


def _attn_bwd(z, zkv, o, do, cos_t, sin_t, sinks, dz, S, tq, copies, src, landing):
    T = z.shape[0]
    nt = S // tq
    nq = tq // BLOCK

    def body(sink_ref, q_ref, kv_ref, hkv_ref, o_ref, do_ref, cos_ref, sin_ref, hcos_ref, hsin_ref, dz_in,
             src_ref, dq_ref, dkv_ref, gs_ref, land_ref, carry, dkacc, dvacc, send_sems, recv_sems):
        b = pl.program_id(0)
        tt = pl.program_id(1)
        t = nt - 1 - tt

        @pl.when((b == 0) & (tt == 0))
        def _():
            gs_ref[...] = jnp.zeros_like(gs_ref)
            for cp in copies(src_ref, land_ref, send_sems, recv_sems):
                cp.start()

        @pl.when(tt == 0)
        def _():
            carry[...] = jnp.zeros_like(carry)

        cos = cos_ref[...]
        sin = sin_ref[...]
        pswap = _swap_matrix()
        kv = jnp.concatenate([hkv_ref[...], kv_ref[...]], axis=0)
        cos_k = jnp.concatenate([hcos_ref[...], cos], axis=0)
        sin_k = jnp.concatenate([hsin_ref[...], sin], axis=0)
        kx = _kv_variants(_rope(kv[:, :BLOCK], cos_k, sin_k, pswap))
        vx = _kv_variants(kv[:, BLOCK:].astype(F32))
        band, sj = _band_mask(4)
        lo = lax.broadcasted_iota(jnp.int32, (4 * BLOCK, BLOCK), 1) < HEAD_DIM
        ones = jnp.ones((2 * BLOCK, 2 * BLOCK), BF16)
        qs = [(_rope(q_ref[:, 128 * hp:128 * hp + 128], cos, sin, pswap) * 0.125).astype(BF16)
              for hp in range(8)]
        dkacc[...] = jnp.zeros_like(dkacc)
        dvacc[...] = jnp.zeros_like(dvacc)
        gsum = jnp.zeros((1, BLOCK), F32)
        hlane = lax.broadcasted_iota(jnp.int32, (1, BLOCK), 1)
        for n in range(nq):
            first = (t == 0) & (n == 0)
            valid = band & (jnp.logical_not(first) | (sj >= BLOCK))
            r0 = n * BLOCK
            for g in range(2):
                cols = [slice(128 * (4 * g + j), 128 * (4 * g + j) + 128) for j in range(4)]
                lhs = jnp.concatenate([qs[4 * g + j][r0:r0 + BLOCK] for j in range(4)], axis=0)
                dov = jnp.concatenate([do_ref[r0:r0 + BLOCK, cs] for cs in cols], axis=0)
                prod = dov.astype(F32) * jnp.concatenate(
                    [o_ref[r0:r0 + BLOCK, cs] for cs in cols], axis=0).astype(F32)
                lhs_t = lhs.T
                dov_t = dov.T
                dq = jnp.zeros((4 * BLOCK, BLOCK), F32)
                dk_t = jnp.zeros((HEAD_DIM, 2 * BLOCK), F32)
                dv_t = jnp.zeros((HEAD_DIM, 2 * BLOCK), F32)
                for e in range(2):
                    kw = kx[g][e][r0:r0 + 2 * BLOCK]
                    vw = vx[g][e][r0:r0 + 2 * BLOCK]
                    s = _mm_nt(lhs, kw)
                    p, psink = _softmax_sink(s, valid, _sink_rep(sink_ref, g, e))
                    pe = jnp.where(lo if e == 0 else jnp.logical_not(lo), prod, 0.0)
                    pe_hi = pe.astype(BF16)
                    pe_lo = (pe - pe_hi.astype(F32)).astype(BF16)
                    delta = _mm(jnp.concatenate([pe_hi, pe_lo], axis=1), ones)
                    ds = (p * (_mm_nt(dov, vw) - delta)).astype(BF16)
                    dq = dq + _mm(ds, kw)
                    dims = slice(HEAD_DIM * e, HEAD_DIM * (e + 1))
                    dk_t = dk_t + _mm(lhs_t[dims], ds)
                    dv_t = dv_t + _mm(dov_t[dims], p.astype(BF16))
                    gs = -psink * delta[:, 0:BLOCK]
                    for j in range(4):
                        tot = jnp.sum(gs[j * BLOCK:(j + 1) * BLOCK], axis=0, keepdims=True)
                        gsum = gsum + jnp.where(hlane == 8 * g + 2 * j + e, tot, 0.0)
                dkacc[HEAD_DIM * g:HEAD_DIM * (g + 1), r0:r0 + 2 * BLOCK] += dk_t
                dvacc[HEAD_DIM * g:HEAD_DIM * (g + 1), r0:r0 + 2 * BLOCK] += dv_t
                for j in range(4):
                    dqj = _rope_f32(dq[j * BLOCK:(j + 1) * BLOCK] * 0.125, cos[r0:r0 + BLOCK],
                                    -sin[r0:r0 + BLOCK], pswap)
                    dq_ref[r0:r0 + BLOCK, cols[j]] = dqj.astype(BF16)
        gs_ref[0:1, :] += gsum
        dk_all = dkacc[...]
        dv_all = dvacc[...]
        dk_last = dk_all[:, tq:tq + BLOCK] + carry[0:BLOCK, :]
        dv_last = dv_all[:, tq:tq + BLOCK] + carry[BLOCK:2 * BLOCK, :]
        carry[0:BLOCK, :] = dk_all[:, 0:BLOCK]
        carry[BLOCK:2 * BLOCK, :] = dv_all[:, 0:BLOCK]
        if nq > 1:
            dk_tile = jnp.concatenate([dk_all[:, BLOCK:tq], dk_last], axis=1)
            dv_tile = jnp.concatenate([dv_all[:, BLOCK:tq], dv_last], axis=1)
        else:
            dk_tile, dv_tile = dk_last, dv_last
        dkv_ref[:, 0:BLOCK] = _rope_f32(dk_tile.T, cos, -sin, pswap).astype(BF16)
        dkv_ref[:, BLOCK:2 * BLOCK] = dv_tile.T.astype(BF16)

        @pl.when((b == pl.num_programs(0) - 1) & (tt == nt - 1))
        def _():
            cps = copies(src_ref, land_ref, send_sems, recv_sems)
            for cp in cps:
                cp.wait_recv()
            for cp in cps:
                cp.wait_send()

    def row(b, tt):
        return b * nt + (nt - 1 - tt)

    def halo(b, tt):
        return jnp.maximum(row(b, tt) * nq - 1, 0)

    tile = pl.BlockSpec((tq, D), lambda b, tt: (row(b, tt), 0))
    return pl.pallas_call(
        body, name="attn_bwd", grid=(T // S, nt),
        in_specs=[pl.BlockSpec(memory_space=pltpu.SMEM),
                  pl.BlockSpec((tq, D), lambda b, tt: (row(b, tt), ZB_Q)),
                  pl.BlockSpec((tq, 2 * BLOCK), lambda b, tt: (row(b, tt), 0)),
                  pl.BlockSpec((BLOCK, 2 * BLOCK), lambda b, tt: (halo(b, tt), 0)),
                  tile, tile,
                  pl.BlockSpec((tq, BLOCK), lambda b, tt: (row(b, tt), 0)),
                  pl.BlockSpec((tq, BLOCK), lambda b, tt: (row(b, tt), 0)),
                  pl.BlockSpec((BLOCK, BLOCK), lambda b, tt: (halo(b, tt), 0)),
                  pl.BlockSpec((BLOCK, BLOCK), lambda b, tt: (halo(b, tt), 0)),
                  pl.BlockSpec(memory_space=pl.ANY), pl.BlockSpec(memory_space=pl.ANY)],
        out_specs=(pl.BlockSpec((tq, D), lambda b, tt: (row(b, tt), ZB_Q)),
                   pl.BlockSpec((tq, 2 * BLOCK), lambda b, tt: (row(b, tt), 0)),
                   pl.BlockSpec((8, BLOCK), lambda b, tt: (0, 0)), pl.BlockSpec(memory_space=pl.ANY)),
        out_shape=(jax.ShapeDtypeStruct(dz.shape, BF16), jax.ShapeDtypeStruct((T, 2 * BLOCK), BF16),
                   jax.ShapeDtypeStruct((8, BLOCK), F32), landing),
        input_output_aliases={10: 0},
        scratch_shapes=[pltpu.VMEM((2 * BLOCK, BLOCK), F32), pltpu.VMEM((BLOCK, tq + BLOCK), F32),
                        pltpu.VMEM((BLOCK, tq + BLOCK), F32), pltpu.SemaphoreType.DMA((3,)),
                        pltpu.SemaphoreType.DMA((3,))],
        compiler_params=_params(("arbitrary", "arbitrary")),
    )(sinks, z, zkv, zkv, o, do, cos_t, sin_t, cos_t, sin_t, dz, src)


def _dh(dz, dz_kv, wall, x, dx1, ln_pre, tm, tile0, ntiles, gx_prev, name, copies, src, landing):
    T = x.shape[0]
    nsem = 3

    def body(*refs):
        dz_ref, kv_ref, wt_ref, x_ref, dx1_ref, g_ref, src_ref = refs[:7]
        gx_ref, glp_ref, land_ref, wbuf, send_sems, recv_sems, wsem = refs[-7:]
        i = pl.program_id(0)

        @pl.when(i == 0)
        def _():
            glp_ref[...] = jnp.zeros_like(glp_ref)
            for cp in copies(src_ref, land_ref, send_sems, recv_sems):
                cp.start()
            load = pltpu.make_async_copy(wt_ref, wbuf, wsem)
            load.start()
            load.wait()

        dh = _mm(dz_ref[...], wbuf[0:ZKV, :]) + _mm(kv_ref[...], wbuf[ZKV:IN_WIDTH, :])
        xv = x_ref[...]
        r = lax.rsqrt(jnp.mean(xv * xv, axis=-1, keepdims=True) + EPS)
        xr = xv * r
        glp_ref[...] += jnp.sum(dh * xr, axis=0, keepdims=True)
        a = dh * g_ref[...]
        gx_ref[...] = dx1_ref[...] + r * (a - xr * jnp.mean(a * xr, axis=-1, keepdims=True))

        @pl.when(i == ntiles - 1)
        def _():
            cps = copies(src_ref, land_ref, send_sems, recv_sems)
            for cp in cps:
                cp.wait_recv()
            for cp in cps:
                cp.wait_send()

    tile = pl.BlockSpec((tm, D), lambda i: (tile0 + i, 0))
    any_spec = pl.BlockSpec(memory_space=pl.ANY)
    operands = [dz, dz_kv, wall, x, dx1, ln_pre, src] + ([] if gx_prev is None else [gx_prev])
    return pl.pallas_call(
        body, name=name, grid=(ntiles,),
        in_specs=[pl.BlockSpec((tm, ZKV), lambda i: (tile0 + i, 0)),
                  pl.BlockSpec((tm, 2 * BLOCK), lambda i: (tile0 + i, 0)),
                  any_spec, tile, tile, pl.BlockSpec((1, D), lambda i: (0, 0)), any_spec]
        + ([] if gx_prev is None else [any_spec]),
        out_specs=(tile, pl.BlockSpec((1, D), lambda i: (0, 0)), any_spec),
        out_shape=(jax.ShapeDtypeStruct((T, D), F32), jax.ShapeDtypeStruct((1, D), F32), landing),
        input_output_aliases={} if gx_prev is None else {7: 0},
        scratch_shapes=[pltpu.VMEM((IN_WIDTH, D), BF16), pltpu.SemaphoreType.DMA((nsem,)),
                        pltpu.SemaphoreType.DMA((nsem,)), pltpu.SemaphoreType.DMA],
        compiler_params=_params(("arbitrary",)),
    )(*operands)


def _gwt(dz, dz_kv, h, tt):
    T = dz.shape[0]
    nt = T // tt
    last = nt - 1
    kv = 2 * BLOCK

    def body(dz_ref, dzkv_ref, h_ref, gpack_ref, acc, sem):
        j = pl.program_id(0)
        t = pl.program_id(1)

        @pl.when((j < 7) & (t == 0))
        def _():
            acc[...] = _mm_tn(dz_ref[...], h_ref[...])

        @pl.when((j < 7) & (t > 0))
        def _():
            acc[...] += _mm_tn(dz_ref[...], h_ref[...])

        @pl.when((j == 7) & (t == 0))
        def _():
            acc[0:kv, :] = _mm_tn(dzkv_ref[...], h_ref[...])

        @pl.when((j == 7) & (t > 0))
        def _():
            acc[0:kv, :] += _mm_tn(dzkv_ref[...], h_ref[...])

        for jj in range(7):
            @pl.when((t == last) & (j == jj))
            def _(jj=jj):
                _flush_to_pack(acc, gpack_ref, WT0 + jj * D, sem)

        @pl.when((t == last) & (j == 7))
        def _():
            _flush_to_pack(acc.at[pl.ds(0, kv)], gpack_ref, WT0 + ZKV, sem)

    return pl.pallas_call(
        body, name="gwt", grid=(8, nt),
        in_specs=[pl.BlockSpec((tt, D), lambda j, t: (jnp.where(j == 7, last, t), jnp.minimum(j, 6))),
                  pl.BlockSpec((tt, kv), lambda j, t: (jnp.where(j == 7, t, 0), 0)),
                  pl.BlockSpec((tt, D), lambda j, t: (t, 0))],
        out_specs=pl.BlockSpec(memory_space=pl.ANY),
        out_shape=jax.ShapeDtypeStruct((N_SHARDS, WIN_SHARD, D), F32),
        scratch_shapes=[pltpu.VMEM((D, D), F32), pltpu.SemaphoreType.DMA],
        compiler_params=_params(("arbitrary", "arbitrary")),
    )(dz, dz_kv, h)


_BC1 = 1.0 - ADAM_B1 ** ADAM_STEP
_BC2 = 1.0 - ADAM_B2 ** ADAM_STEP


def _adamw_math(w, g, m, v):
    m = ADAM_B1 * m + (1.0 - ADAM_B1) * g
    v = ADAM_B2 * v + (1.0 - ADAM_B2) * (g * g)
    delta = -ADAM_LR * ((m / _BC1) / (jnp.sqrt(v / _BC2) + ADAM_EPS) + ADAM_WD * w)
    return delta, m, v


def _adamw_rows(g, w, m, v, rows, name):
    R, C = w.shape

    def body(g_ref, w_ref, m_ref, v_ref, go_ref, d_ref, nm_ref, nv_ref):
        gv = g_ref[...]
        d, nm, nv = _adamw_math(w_ref[...], gv, m_ref[...], v_ref[...])
        go_ref[...] = gv
        d_ref[...] = d
        nm_ref[...] = nm
        nv_ref[...] = nv

    spec = pl.BlockSpec((rows, C), lambda i: (i, 0))
    shp = jax.ShapeDtypeStruct((R, C), F32)
    return pl.pallas_call(
        body, name=name, grid=(R // rows,), in_specs=[spec] * 4, out_specs=(spec,) * 4,
        out_shape=(shp,) * 4, compiler_params=_params(("arbitrary",)),
    )(g, w, m, v)


def _adamw_square(gfin, ws, ms, vs):
    rb = 64
    nb = SQ_SHARD // rb

    def body(*refs):
        g_refs = refs[0:5]
        w_refs, m_refs, v_refs = refs[5:10], refs[10:15], refs[15:20]
        outs = refs[20:]
        for k in range(5):
            gk = g_refs[k][...]
            d, nm, nv = _adamw_math(w_refs[k][...], gk, m_refs[k][...], v_refs[k][...])
            outs[4 * k][...] = gk
            outs[4 * k + 1][...] = d
            outs[4 * k + 2][...] = nm
            outs[4 * k + 3][...] = nv

    spec = pl.BlockSpec((rb, D), lambda i: (i, 0))
    gspecs = [pl.BlockSpec((rb, D), lambda i, k=k: (SQ_SHARD * k // rb + i, 0)) for k in range(5)]
    shp = jax.ShapeDtypeStruct((SQ_SHARD, D), F32)
    res = pl.pallas_call(
        body, name="adamw_square", grid=(nb,), in_specs=gspecs + [spec] * 15, out_specs=(spec,) * 20,
        out_shape=(shp,) * 20, compiler_params=_params(("arbitrary",)),
    )(*([gfin] * 5), *ws, *ms, *vs)
    return [tuple(res[4 * k:4 * k + 4]) for k in range(5)]


def _adamw_small(gs, ws, ms, vs):
    n = len(gs)

    def body(*refs):
        outs = refs[4 * n:]
        for k in range(n):
            d, nm, nv = _adamw_math(refs[n + k][...], refs[k][...], refs[2 * n + k][...],
                                    refs[3 * n + k][...])
            outs[3 * k][...] = d
            outs[3 * k + 1][...] = nm
            outs[3 * k + 2][...] = nv

    vm = pl.BlockSpec(memory_space=pltpu.VMEM)
    shapes = []
    for w in ws:
        shapes += [jax.ShapeDtypeStruct(w.shape, F32)] * 3
    res = pl.pallas_call(
        body, name="adamw_small", in_specs=[vm] * (4 * n), out_specs=(vm,) * (3 * n),
        out_shape=tuple(shapes),
    )(*gs, *ws, *ms, *vs)
    return [tuple(res[3 * k:3 * k + 3]) for k in range(n)]


def _rope_constants():
    half = ROPE_DIM // 2
    inv = jnp.power(ROPE_THETA, -jnp.arange(0, ROPE_DIM, 2, dtype=F32) / ROPE_DIM)
    freq = jnp.concatenate([inv, jnp.zeros((ROPE_ROWS - half,), F32)]).reshape(ROPE_ROWS, 1)
    spread = np.zeros((3, ROPE_ROWS, BLOCK), np.float32)
    for lane in range(BLOCK):
        d = lane % HEAD_DIM
        if d < ROPE_DIM:
            spread[0, d % half, lane] = 1.0
            spread[1, d % half, lane] = -1.0 if d < half else 1.0
        else:
            spread[2, 0, lane] = 1.0
    return freq, jnp.asarray(spread, BF16)


def kernel(x, p, positions, w_in, ln_pre, ln_post, w_dw, b_dw, conv_ln_g, conv_ln_b, w_pw, sinks, w_br_conv, w_br_attn, w_out, w_ple_gate, w_ple_proj, loss_target, m_w_in, m_ln_pre, m_ln_post, m_w_dw, m_b_dw, m_conv_ln_g, m_conv_ln_b, m_w_pw, m_sinks, m_w_br_conv, m_w_br_attn, m_w_out, m_w_ple_gate, m_w_ple_proj, v_w_in, v_ln_pre, v_ln_post, v_w_dw, v_b_dw, v_conv_ln_g, v_conv_ln_b, v_w_pw, v_sinks, v_w_br_conv, v_w_br_attn, v_w_out, v_w_ple_gate, v_w_ple_proj):
    nb, S, _ = x.shape
    T = nb * S
    xc = lax.axis_index("x")
    yc = lax.axis_index("y")
    cc = lax.axis_index("c")
    shard = 2 * xc + yc

    sq_w = (w_pw, w_br_conv, w_br_attn, w_out, w_ple_gate)
    wdw_shard = jnp.pad(w_dw[0], ((0, 1), (0, 0)))
    x2 = x.reshape(T, D)
    tm_res = min(TILE_RESIDENT, T // 2)
    h, cos_t, sin_t = _prenorm(x2, ln_pre, positions.astype(F32).reshape(1, T), *_rope_constants(), tm_res)

    tgt = loss_target.reshape(T, D)
    p2 = p.reshape(T, PLE)
    sinks1 = sinks.reshape(N_HEADS)

    tm = min(TILE_TOKEN, S)
    tq = min(TILE_ATTN, S)

    z, zkv, wt, wdw_all = _inproj(h, w_in[0].T.astype(BF16), wdw_shard, min(TILE_PROJ, T // 2))
    wdw = jnp.concatenate([wdw_all[s] for s in range(N_SHARDS)], axis=1)
    sq_shards = [w[0].astype(BF16) for w in sq_w] + [w_ple_proj[0].T.reshape(WPP_SHARD, D).astype(BF16)]
    o, wall_a = _attn_fwd(z, zkv, cos_t, sin_t, sinks1, S, tq, GROUP_CONV, sq_shards[0:2])
    ya, y, rstd, pw, wall_b, wppf = _conv_fwd(z, wdw, b_dw, conv_ln_g, conv_ln_b, wall_a, S, tm, GROUP_TAIL,
                                              sq_shards[2:])
    wppt = wppf.reshape(D, PLE)
    loss_p, dx1, dm, yb, g_ln_post, gsq, gw_ppt = _tail_a(x2, tgt, p2, o, ya, z, ln_post, wall_b, wppt, tm)

    cidx = jnp.reshape(cc, (1,)).astype(jnp.int32)
    scidx = jnp.stack([shard, cc]).astype(jnp.int32)

    def landing(pack, n, dtype):
        return jax.ShapeDtypeStruct((n, pack.shape[1] // 2, D), dtype)

    dz, do, dc, gvec, gsq = _tail_b(dm, ya, yb, o, z, pw, y, rstd, conv_ln_g, conv_ln_b, wall_a, wall_b,
                                    gw_ppt.reshape(PLE, D), gsq, tm)
    dz, g_wdw, r1_sq = _conv_bwd(dc, z, wdw, dz, S, tm, _exchange_copies, gsq, landing(gsq, N_SHARDS, F32))
    cs_sq = _chip_sum(cidx, gsq, r1_sq, "chip_sum_sq")
    dz, dkv, g_sinks, r2_sq = _attn_bwd(z, zkv, o, do, cos_t, sin_t, sinks1, dz, S, tq, _chip_sum_copies, cs_sq,
                                        landing(gsq, 3, BF16))
    gwt_pack = _gwt(dz, dkv, h, min(2 * TILE_PROJ, T))

    tm_dh = tm_res
    n_dh = T // tm_dh
    n_a = max(1, n_dh // 4)
    gx, g_ln_pre_a, r1_wt = _dh(
        dz, dkv, wt, x2, dx1, ln_pre, tm_dh, 0, n_a, None, "dh_exchange", _exchange_copies, gwt_pack,
        landing(gwt_pack, N_SHARDS, F32))
    cs_wt = _chip_sum(cidx, gwt_pack, r1_wt, "chip_sum_wt")
    gx, g_ln_pre_b, r2_wt = _dh(
        dz, dkv, wt, x2, dx1, ln_pre, tm_dh, n_a, n_dh - n_a, gx, "dh_send", _chip_sum_copies, cs_wt,
        landing(gwt_pack, 3, BF16))
    g_ln_pre = g_ln_pre_a + g_ln_pre_b
    row37 = jnp.concatenate([g_sinks[0:1, 0:N_HEADS], loss_p, jnp.zeros((1, D - N_HEADS - 1), F32)], axis=1)
    vec = jnp.concatenate([g_wdw, g_ln_pre, g_ln_post, gvec[2:3], gvec[0:1], gvec[1:2], row37,
                           jnp.zeros((VEC_ROWS - 38, D), F32)], axis=0)
    gfin_wt, gfin_sq, tot = _finish_reduce(_final_half(scidx, gwt_pack, r1_wt, r2_wt, "final_half_wt"),
                                           _final_half(scidx, gsq, r1_sq, r2_sq, "final_half_sq"), vec)

    g_w_in, d_w_in, nm_w_in, nv_w_in = [a.T for a in _adamw_rows(
        gfin_wt, w_in[0].T, m_w_in[0].T, v_w_in[0].T, WIN_SHARD // 8, "adamw_w_in")]
    g_w_in = g_w_in[None]
    sq_m = (m_w_pw, m_w_br_conv, m_w_br_attn, m_w_out, m_w_ple_gate)
    sq_v = (v_w_pw, v_w_br_conv, v_w_br_attn, v_w_out, v_w_ple_gate)
    sq_res = _adamw_square(gfin_sq, [w[0] for w in sq_w], [m[0] for m in sq_m], [v[0] for v in sq_v])
    g_wpp = gfin_sq[5 * SQ_SHARD:SQ_PACK].reshape(PLE, PLE).T
    g_dw_all = tot[0:CONV_K]
    g_dw = lax.dynamic_slice_in_dim(g_dw_all, shard * PLE, PLE, axis=1)
    small_g = [g_wpp, g_dw, tot[32:33], tot[33:34], tot[34:35], tot[35:36], tot[36:37],
               tot[37:38, 0:N_HEADS]]
    small_w = [w_ple_proj[0], w_dw[0], ln_pre, ln_post, b_dw, conv_ln_g, conv_ln_b, sinks]
    small_m = [m_w_ple_proj[0], m_w_dw[0], m_ln_pre, m_ln_post, m_b_dw, m_conv_ln_g, m_conv_ln_b, m_sinks]
    small_v = [v_w_ple_proj[0], v_w_dw[0], v_ln_pre, v_ln_post, v_b_dw, v_conv_ln_g, v_conv_ln_b, v_sinks]
    small = _adamw_small(small_g, small_w, small_m, small_v)

    loss = tot[37, N_HEADS]
    grads = [g_w_in, small_g[2], small_g[3], g_dw[None], small_g[4], small_g[5], small_g[6],
             sq_res[0][0][None], small_g[7], sq_res[1][0][None], sq_res[2][0][None], sq_res[3][0][None],
             sq_res[4][0][None], g_wpp[None]]

    def triple(i):
        w_in_t = (d_w_in[None], nm_w_in[None], nv_w_in[None])
        sq = lambda k: tuple(a[None] for a in sq_res[k][1:4])
        sm = lambda k, lead: tuple(a[None] if lead else a for a in small[k])
        return [w_in_t[i], sm(2, False)[i], sm(3, False)[i], sm(1, True)[i], sm(4, False)[i],
                sm(5, False)[i], sm(6, False)[i], sq(0)[i], sm(7, False)[i], sq(1)[i], sq(2)[i], sq(3)[i],
                sq(4)[i], sm(0, True)[i]]

    return (loss, gx.reshape(nb, S, D), *grads, *triple(0), *triple(1), *triple(2))
```

```python
import functools

import jax
import jax.numpy as jnp
import numpy as np
from jax import lax
from jax.experimental import pallas as pl
from jax.experimental.pallas import tpu as pltpu

F32 = jnp.float32
BF16 = jnp.bfloat16

D = 1024
PLE = 256
N_HEADS = 16
HEAD_DIM = 64
BLOCK = 128
CONV_K = 31
ROPE_DIM = 16
ROPE_THETA = 500000.0
EPS = 1e-6
IN_WIDTH = 7424
N_SHARDS = 4

ADAM_LR = 0.001
ADAM_B1 = 0.9
ADAM_B2 = 0.999
ADAM_EPS = 1e-08
ADAM_WD = 0.01
ADAM_STEP = 10

SQ_NAMES = ("w_pw", "w_br_conv", "w_br_attn", "w_out", "w_ple_gate")
WT0 = 5 * D
WPP0 = WT0 + IN_WIDTH
WALL_ROWS = WPP0 + PLE
WIN_SHARD = IN_WIDTH // N_SHARDS
SQ_SHARD = D // N_SHARDS
WPP_SHARD = PLE * PLE // D
PACK_ROWS = WIN_SHARD + 5 * SQ_SHARD + WPP_SHARD
HALF_ROWS = PACK_ROWS // 2
VMEM_LIMIT = 56 * 1024 * 1024
MESH = pl.DeviceIdType.MESH
TILE_RESIDENT = 512
TILE_PROJ = 1024
TILE_TOKEN = 256
TILE_ATTN = 512
TAIL_PARTS = 1


ZB_AGATE, ZB_GCONV, ZB_GATTN, ZB_CGATE, ZB_CVAL, ZB_CGLU, ZB_Q = range(7)
ZKV = 7 * D
_SEGMENTS = ((0, D, ZB_CVAL * D), (D, D, ZB_CGLU * D), (2 * D, D, ZB_CGATE * D), (3 * D, D, ZB_Q * D),
             (4 * D, 2 * BLOCK, ZKV), (4 * D + 2 * BLOCK, D, ZB_AGATE * D),
             (5 * D + 2 * BLOCK, D, ZB_GCONV * D), (6 * D + 2 * BLOCK, D, ZB_GATTN * D))
_WT_CUTS = (0, 192, 640, 1216, WIN_SHARD)


def _zp_row(o):
    for a, w, zp in _SEGMENTS:
        if a <= o < a + w:
            return zp + o - a
    raise ValueError(o)


def _pieces(s):
    out = []
    for a, b in zip(_WT_CUTS[:-1], _WT_CUTS[1:]):
        first = _zp_row(WIN_SHARD * s + a)
        assert _zp_row(WIN_SHARD * s + b - 1) == first + b - a - 1
        out.append((a, b - a, WT0 + first))
    for k in range(5):
        out.append((WIN_SHARD + SQ_SHARD * k, SQ_SHARD, D * k + SQ_SHARD * s))
    out.append((WIN_SHARD + 5 * SQ_SHARD, WPP_SHARD, WPP0 + WPP_SHARD * s))
    return out


N_PIECES = len(_pieces(0))


def _wall_segments(wall0, rows):
    out = []
    for s in range(N_SHARDS):
        for pr, n, wr in _pieces(s):
            lo, hi = max(wr, wall0), min(wr + n, wall0 + rows)
            if lo < hi:
                out.append((lo - wall0, hi - lo, s, pr + lo - wr))
    assert sum(n for _, n, _, _ in out) == rows
    return out


def _sel(s, vals):
    r = jnp.int32(vals[0])
    for i in range(1, len(vals)):
        r = jnp.where(s == i, jnp.int32(vals[i]), r)
    return r


def _sig(x):
    return 1.0 / (1.0 + jnp.exp(-x))


def _mm(a, b):
    return lax.dot_general(a, b, (((1,), (0,)), ((), ())), preferred_element_type=F32)


def _mm_nt(a, b):
    return lax.dot_general(a, b, (((1,), (1,)), ((), ())), preferred_element_type=F32)


def _mm_tn(a, b):
    return lax.dot_general(a, b, (((0,), (0,)), ((), ())), preferred_element_type=F32)


def _params(sem=None):
    return pltpu.CompilerParams(dimension_semantics=sem, vmem_limit_bytes=VMEM_LIMIT)


def _flush_to_pack(acc_ref, gpack_ref, wall0, sem):
    for cp in _pack_copies(acc_ref, gpack_ref, wall0, sem):
        cp.start()
        cp.wait()


def _flush_all(items, gpack_ref):
    for acc_ref, wall0, sem in items:
        for cp in _pack_copies(acc_ref, gpack_ref, wall0, sem):
            cp.start()
    for acc_ref, _, sem in items:
        pltpu.make_async_copy(acc_ref, gpack_ref.at[0, pl.ds(0, acc_ref.shape[0])], sem).wait()


def _pack_copies(acc_ref, gpack_ref, wall0, sem):
    base = 0 if gpack_ref.shape[1] == WIN_SHARD else WIN_SHARD
    out = []
    for r, n, s, pr in _wall_segments(wall0, acc_ref.shape[0]):
        assert 0 <= pr - base and pr - base + n <= gpack_ref.shape[1]
        out.append(pltpu.make_async_copy(acc_ref.at[pl.ds(r, n)], gpack_ref.at[s, pl.ds(pr - base, n)], sem))
    return out


def _coords():
    return lax.axis_index("x"), lax.axis_index("y"), lax.axis_index("c")


def _chip_peers(x, y):
    return [(1 - x, y), (x, 1 - y), (1 - x, 1 - y)]


WIN_PIECES = tuple(range(len(_WT_CUTS) - 1))
SQ_PIECES = tuple(range(len(WIN_PIECES), N_PIECES))


def _gather_ops(group, src, landing, bytes_ref, stage, send_sems, recv_sems, loc_sem):
    sizes = [_pieces(0)[p][1] for p in group]
    half_rows = sum(n // 2 for n in sizes)
    starts = [sum(sizes[:i]) for i in range(len(sizes))]

    def rcopy(a, b, k, dev):
        return pltpu.make_async_remote_copy(src_ref=a, dst_ref=b, send_sem=send_sems.at[k],
                                            recv_sem=recv_sems.at[k], device_id=dev, device_id_type=MESH)

    def total(k):
        x, y, c = _coords()
        rows = bytes_ref.at[pl.ds(0, half_rows)]
        return rcopy(rows, rows, k, (x, y, c))

    def own_total():
        rows = stage.at[pl.ds(0, sum(sizes))]
        return pltpu.make_async_copy(rows, rows, loc_sem)

    def send():
        x, y, c = _coords()
        s_me = 2 * x + y
        for k, (px, py) in enumerate(_chip_peers(x, y)):
            for p, n in zip(group, sizes):
                h = n // 2
                rcopy(src(p, c * h, h), landing(p, s_me, c * h, h), k, (px, py, c)).start()
        for p, n, r in zip(group, sizes, starts):
            pltpu.make_async_copy(src(p, 0, n), stage.at[pl.ds(r, n)], loc_sem).start()

    def forward():
        x, y, c = _coords()
        own_total().wait()
        for p, n, r in zip(group, sizes, starts):
            pltpu.make_async_copy(stage.at[pl.ds(r, n)], landing(p, 2 * x + y, 0, n), loc_sem).start()
        for k, (px, py) in enumerate(_chip_peers(x, y)):
            total(k).wait_recv()
            for p, n in zip(group, sizes):
                rows = landing(p, 2 * px + py, c * (n // 2), n // 2)
                rcopy(rows, rows, 3 + k, (x, y, 1 - c)).start()

    def finish():
        own_total().wait()
        for k in range(3):
            total(3 + k).wait_recv()
        for k in range(6):
            total(k).wait_send()

    return send, forward, finish


def _piece_rows(ref, start, off, n):
    first = start + off
    return ref.at[pl.ds(first if isinstance(first, int) else pl.multiple_of(first, 32), n)]


GROUP_CONV = SQ_PIECES[0:2]
GROUP_TAIL = SQ_PIECES[2:]


def _group_shapes(group):
    n_sq = sum(1 for q in group if q != N_PIECES - 1)
    return [jax.ShapeDtypeStruct((n_sq * D, D), BF16)] + (
        [jax.ShapeDtypeStruct((PLE, D), BF16)] if N_PIECES - 1 in group else [])


def _group_scratch(group):
    return [pltpu.VMEM((sum(_pieces(0)[q][1] for q in group), D), BF16), pltpu.SemaphoreType.DMA((6,)),
            pltpu.SemaphoreType.DMA((6,)), pltpu.SemaphoreType.DMA]


def _group_gather(group, shard_refs, out_refs, scratch, step, n_steps):
    wall_ref = out_refs[0]
    stage, send_sems, recv_sems, loc_sem = scratch

    def src(q, off, n):
        return _piece_rows(shard_refs[group.index(q)], 0, off, n)

    def landing(q, s, off, n):
        if q == N_PIECES - 1:
            return _piece_rows(out_refs[1], WPP_SHARD * s, off, n)
        return _piece_rows(wall_ref, D * group.index(q) + SQ_SHARD * s, off, n)

    send, forward, finish = _gather_ops(group, src, landing, wall_ref, stage, send_sems, recv_sems, loc_sem)
    pl.when(step == 0)(send)
    pl.when(step == n_steps // 2)(forward)
    return finish


ROPE_ROWS = 16


def _prenorm(x, ln_pre, pos, freq, spread, tm):
    T = x.shape[0]

    def to_lanes(v, e):
        out = None
        for _ in range(3):
            part = v.astype(BF16)
            term = _mm_tn(part, e)
            out = term if out is None else out + term
            v = v - part.astype(F32)
        return out

    def body(x_ref, g_ref, pos_ref, f_ref, e_ref, h_ref, cos_ref, sin_ref):
        xv = x_ref[...]
        r = lax.rsqrt(jnp.mean(xv * xv, axis=-1, keepdims=True) + EPS)
        h_ref[...] = (xv * r * g_ref[...]).astype(BF16)
        ang = f_ref[...] * pos_ref[...]
        cos_ref[...] = to_lanes(jnp.cos(ang), e_ref[0]) + e_ref[2, 0:1, :].astype(F32)
        sin_ref[...] = to_lanes(jnp.sin(ang), e_ref[1])

    return pl.pallas_call(
        body, name="prenorm", grid=(T // tm,),
        out_shape=(jax.ShapeDtypeStruct((T, D), BF16), jax.ShapeDtypeStruct((T, BLOCK), F32),
                   jax.ShapeDtypeStruct((T, BLOCK), F32)),
        in_specs=[pl.BlockSpec((tm, D), lambda i: (i, 0)), pl.BlockSpec((1, D), lambda i: (0, 0)),
                  pl.BlockSpec((1, tm), lambda i: (0, i)), pl.BlockSpec((ROPE_ROWS, 1), lambda i: (0, 0)),
                  pl.BlockSpec((3, ROPE_ROWS, BLOCK), lambda i: (0, 0, 0))],
        out_specs=(pl.BlockSpec((tm, D), lambda i: (i, 0)), pl.BlockSpec((tm, BLOCK), lambda i: (i, 0)),
                   pl.BlockSpec((tm, BLOCK), lambda i: (i, 0))),
        compiler_params=_params(("arbitrary",)),
    )(x, ln_pre, pos, freq, spread)


SQ_PACK = PACK_ROWS - WIN_SHARD


def _row_tile(half):
    return max(t for t in range(8, 321, 8) if half % t == 0)


def _exchange_copies(g_ref, r1_ref, send_sems, recv_sems):
    x, y, c = _coords()
    half = g_ref.shape[1] // 2
    return [pltpu.make_async_remote_copy(
        src_ref=g_ref.at[:, pl.ds(pl.multiple_of((1 - c) * half, 32), half), :], dst_ref=r1_ref,
        send_sem=send_sems.at[0], recv_sem=recv_sems.at[0], device_id=(x, y, 1 - c), device_id_type=MESH)]


def _chip_sum_copies(cs_ref, r2_ref, send_sems, recv_sems):
    x, y, c = _coords()
    return [pltpu.make_async_remote_copy(
        src_ref=cs_ref.at[2 * px + py], dst_ref=r2_ref.at[k], send_sem=send_sems.at[k],
        recv_sem=recv_sems.at[k], device_id=(px, py, c), device_id_type=MESH)
        for k, (px, py) in enumerate(_chip_peers(x, y))]


def _chip_sum(cidx, gpack, r1, name):
    half = gpack.shape[1] // 2
    rt = _row_tile(half)

    def body(c_ref, g_ref, r_ref, o_ref):
        o_ref[...] = (g_ref[...] + r_ref[...]).astype(BF16)

    nt = half // rt
    return pl.pallas_call(
        body, name=name,
        grid_spec=pltpu.PrefetchScalarGridSpec(
            num_scalar_prefetch=1, grid=(N_SHARDS, nt),
            in_specs=[pl.BlockSpec((1, rt, D), lambda s, t, c: (s, c[0] * nt + t, 0)),
                      pl.BlockSpec((1, rt, D), lambda s, t, c: (s, t, 0))],
            out_specs=pl.BlockSpec((1, rt, D), lambda s, t, c: (s, t, 0))),
        out_shape=jax.ShapeDtypeStruct((N_SHARDS, half, D), BF16),
        compiler_params=_params(("arbitrary", "arbitrary")),
    )(cidx, gpack, r1)


def _final_half(sc, gpack, r1, r2, name):
    rows = gpack.shape[1]
    half = rows // 2
    rt = _row_tile(half)

    def body(sc_ref, g_ref, r_ref, p_ref, o_ref):
        acc = g_ref[0] + r_ref[0]
        for k in range(3):
            acc = acc + p_ref[k].astype(F32)
        o_ref[...] = acc

    nt = half // rt
    return pl.pallas_call(
        body, name=name,
        grid_spec=pltpu.PrefetchScalarGridSpec(
            num_scalar_prefetch=1, grid=(nt,),
            in_specs=[pl.BlockSpec((1, rt, D), lambda t, sc: (sc[0], sc[1] * nt + t, 0)),
                      pl.BlockSpec((1, rt, D), lambda t, sc: (sc[0], t, 0)),
                      pl.BlockSpec((3, rt, D), lambda t, sc: (0, t, 0))],
            out_specs=pl.BlockSpec((rt, D), lambda t, sc: (sc[1] * nt + t, 0))),
        out_shape=jax.ShapeDtypeStruct((rows, D), F32),
        compiler_params=_params(("arbitrary",)),
    )(sc, gpack, r1, r2)


VEC_ROWS = 40


def _finish_reduce(fwt, fsq, vec):
    def body(fwt_ref, fsq_ref, v_ref, owt_ref, osq_ref, tot_ref, buf, send_sems, recv_sems):
        x, y, c = _coords()
        swaps = []
        for k, (f_ref, o_ref) in enumerate(((fwt_ref, owt_ref), (fsq_ref, osq_ref))):
            half = f_ref.shape[0] // 2
            rows = pl.ds(pl.multiple_of(c * half, 32), half)
            swaps.append(pltpu.make_async_remote_copy(
                src_ref=f_ref.at[rows], dst_ref=o_ref.at[rows], send_sem=send_sems.at[7 + k],
                recv_sem=recv_sems.at[7 + k], device_id=(x, y, 1 - c), device_id_type=MESH))
        for cp in swaps:
            cp.start()
        me = 4 * x + 2 * y + c
        buf[me] = v_ref[...]
        cps = []
        for r in range(1, 8):
            dx, dy, dc = (r >> 2) & 1, (r >> 1) & 1, r & 1
            peer = (1 - x if dx else x, 1 - y if dy else y, 1 - c if dc else c)
            cp = pltpu.make_async_remote_copy(
                src_ref=v_ref, dst_ref=buf.at[me], send_sem=send_sems.at[r - 1],
                recv_sem=recv_sems.at[r - 1], device_id=peer, device_id_type=MESH)
            cp.start()
            cps.append(cp)
        for cp in cps:
            cp.wait_recv()
        for cp in cps:
            cp.wait_send()
        acc = buf[0]
        for d in range(1, 8):
            acc = acc + buf[d]
        tot_ref[...] = acc
        for cp in swaps:
            cp.wait()

    any_spec = pl.BlockSpec(memory_space=pl.ANY)
    vm = pl.BlockSpec(memory_space=pltpu.VMEM)
    return pl.pallas_call(
        body, name="finish_reduce",
        out_shape=(jax.ShapeDtypeStruct(fwt.shape, F32), jax.ShapeDtypeStruct(fsq.shape, F32),
                   jax.ShapeDtypeStruct((VEC_ROWS, D), F32)),
        in_specs=[any_spec, any_spec, vm], out_specs=(any_spec, any_spec, vm),
        input_output_aliases={0: 0, 1: 1},
        scratch_shapes=[pltpu.VMEM((8, VEC_ROWS, D), F32), pltpu.SemaphoreType.DMA((9,)),
                        pltpu.SemaphoreType.DMA((9,))],
    )(fwt, fsq, vec)


SOLO_ROWS = WIN_SHARD - BLOCK // 2


def _solo_first(s):
    return 0 if s % 2 == 0 else BLOCK // 2


def _solo_segments(s):
    lo = _solo_first(s)
    out = []
    for a, n, wr in _pieces(s)[:len(WIN_PIECES)]:
        b0, b1 = max(a, lo), min(a + n, lo + SOLO_ROWS)
        if b0 >= b1:
            continue
        z0 = wr - WT0 + b0 - a
        if out and out[-1][0] + out[-1][1] == b0 - lo and out[-1][2] + out[-1][1] == z0:
            out[-1] = (out[-1][0], out[-1][1] + b1 - b0, out[-1][2])
        else:
            out.append((b0 - lo, b1 - b0, z0))
    out = [r for o, n, z0 in out for r in
           (((o, ZKV - z0, z0), (o + ZKV - z0, z0 + n - ZKV, ZKV)) if z0 < ZKV < z0 + n else ((o, n, z0),))]
    assert all(v % BLOCK == 0 for seg in out for v in seg) and sum(n for _, n, _ in out) == SOLO_ROWS
    return out


def _shared_tile(pair):
    z0 = _zp_row(WIN_SHARD * (2 * pair) + SOLO_ROWS)
    assert z0 % BLOCK == 0 and _zp_row(WIN_SHARD * (2 * pair + 1)) == z0 + BLOCK // 2
    return z0


def _inproj(h, win_t, wdw_shard, tm):
    T = h.shape[0]
    n_t = T // tm
    assert n_t >= 2
    tables = [[_pieces(s)[p][2] - WT0 for s in range(N_SHARDS)] for p in WIN_PIECES]
    sizes = [_pieces(0)[p][1] for p in WIN_PIECES]
    half_rows = sum(n // 2 for n in sizes)
    relation_of_pass = {1: 1, 2: 0, 3: 2}

    def body(h_ref, win_ref, wdw_ref, z_ref, zkv_ref, wt_ref, wdwall_ref, wbuf, stage, stage_sh, wsend, wrecv,
             loc_sems, out_sems, sh_sems):
        p = pl.program_id(0)
        t = pl.program_id(1)
        x, y, c = _coords()
        s_me = 2 * x + y
        peers = _chip_peers(x, y)
        shard = jnp.bitwise_xor(s_me, p)
        first, last = t == 0, t == n_t - 1

        def rcopy(a, b, k, dev):
            return pltpu.make_async_remote_copy(src_ref=a, dst_ref=b, send_sem=wsend.at[k], recv_sem=wrecv.at[k],
                                                device_id=dev, device_id_type=MESH)

        def total(k):
            rows = wt_ref.at[pl.ds(0, half_rows)]
            return rcopy(rows, rows, k, (x, y, c))

        def in_hbm(q, s, off, n):
            return _piece_rows(wt_ref, _sel(s, tables[q]), off, n)

        def in_vmem(q, s):
            return _piece_rows(wbuf, WIN_SHARD * s + _WT_CUTS[q], 0, sizes[q])

        def send_to(k):
            px, py = peers[k]
            for q, n in zip(WIN_PIECES, sizes):
                rcopy(_piece_rows(win_ref, _WT_CUTS[q], c * (n // 2), n // 2), in_hbm(q, s_me, c * (n // 2), n // 2),
                      k, (px, py, c)).start()

        def forward_from(k):
            px, py = peers[k]
            total(k).wait_recv()
            for q, n in zip(WIN_PIECES, sizes):
                rows = in_hbm(q, 2 * px + py, c * (n // 2), n // 2)
                rcopy(rows, rows, 3 + k, (x, y, 1 - c)).start()
            total(3 + k).wait_recv()

        def shard_total(a, b, sem):
            return pltpu.make_async_copy(a.at[pl.ds(0, WIN_SHARD)], b.at[pl.ds(0, WIN_SHARD)], sem)

        def wdw_copies():
            return [pltpu.make_async_remote_copy(
                src_ref=wdw_ref, dst_ref=wdwall_ref.at[s_me], send_sem=wsend.at[6 + k], recv_sem=wrecv.at[6 + k],
                device_id=(px, py, c), device_id_type=MESH) for k, (px, py) in enumerate(peers)]

        def own_wdw():
            return pltpu.make_async_copy(wdw_ref, wdwall_ref.at[s_me], loc_sems.at[2])

        @pl.when((p == 0) & first)
        def _():
            send_to(0)
            send_to(1)
            own_wdw().start()
            for cp in wdw_copies():
                cp.start()
            for q in WIN_PIECES:
                pltpu.make_async_copy(_piece_rows(win_ref, _WT_CUTS[q], 0, sizes[q]), in_vmem(q, s_me),
                                      loc_sems.at[0]).start()
            shard_total(win_ref, wbuf, loc_sems.at[0]).wait()
            for q in WIN_PIECES:
                pltpu.make_async_copy(in_vmem(q, s_me), in_hbm(q, s_me, 0, sizes[q]), loc_sems.at[1]).start()

        for pp, k in relation_of_pass.items():
            @pl.when((p == pp - 1) & last)
            def _(k=k):
                forward_from(k)
                px, py = peers[k]
                for q in WIN_PIECES:
                    pltpu.make_async_copy(in_hbm(q, 2 * px + py, 0, sizes[q]), in_vmem(q, 2 * px + py),
                                          loc_sems.at[0]).start()

            @pl.when((p == pp) & first)
            def _(pp=pp):
                shard_total(wt_ref, wbuf, loc_sems.at[0]).wait()
                if pp == 1:
                    total(0).wait_send()
                    total(1).wait_send()
                    send_to(2)

        step = p * n_t + t
        slot = step % 2
        rows = pl.ds(pl.multiple_of(t * tm, tm), tm)

        def out_total(sl):
            return pltpu.make_async_copy(stage.at[sl], stage.at[sl], out_sems.at[sl])

        def sh_copy(sl, z0):
            return pltpu.make_async_copy(stage_sh.at[sl], z_ref.at[rows, pl.ds(z0, BLOCK)], sh_sems.at[sl])

        @pl.when(step >= 2)
        def _():
            out_total(slot).wait()

        @pl.when((step >= 2) & (((step - 2) // n_t) % 2 == 1))
        def _():
            sh_copy(slot, 0).wait()

        solo0 = pl.multiple_of(WIN_SHARD * shard + (BLOCK // 2) * (shard % 2), BLOCK // 2)
        stage[slot] = _mm_nt(h_ref[...], wbuf[pl.ds(solo0, SOLO_ROWS), :]).astype(BF16)
        for s in range(N_SHARDS):
            @pl.when(shard == s)
            def _(s=s):
                for off, n, z0 in _solo_segments(s):
                    dst = zkv_ref.at[rows] if z0 == ZKV else z_ref.at[rows, pl.ds(z0, n)]
                    pltpu.make_async_copy(stage.at[slot, :, pl.ds(off, n)], dst, out_sems.at[slot]).start()

        @pl.when(p % 2 == 1)
        def _():
            pair = shard // 2
            w0 = pl.multiple_of(2 * WIN_SHARD * pair + SOLO_ROWS, BLOCK // 2)
            z0 = pl.multiple_of(jnp.where(pair == 0, _shared_tile(0), _shared_tile(1)), BLOCK)
            stage_sh[slot] = _mm_nt(h_ref[...], wbuf[pl.ds(w0, BLOCK), :]).astype(BF16)
            sh_copy(slot, z0).start()

        @pl.when((p == 3) & last)
        def _():
            for k in (2, 3, 4, 5):
                total(k).wait_send()
            shard_total(wbuf, wt_ref, loc_sems.at[1]).wait()
            cps = wdw_copies()
            for cp in cps:
                cp.wait_recv()
            for cp in cps:
                cp.wait_send()
            own_wdw().wait()
            for sl in range(2):
                out_total(sl).wait()
                sh_copy(sl, 0).wait()

    any_spec = pl.BlockSpec(memory_space=pl.ANY)
    return pl.pallas_call(
        body, name="inproj", grid=(N_SHARDS, n_t),
        in_specs=[pl.BlockSpec((tm, D), lambda p, t: (t, 0)), any_spec, any_spec],
        out_specs=(any_spec,) * 4,
        out_shape=(jax.ShapeDtypeStruct((T, ZKV), BF16), jax.ShapeDtypeStruct((T, 2 * BLOCK), BF16),
                   jax.ShapeDtypeStruct((IN_WIDTH, D), BF16), jax.ShapeDtypeStruct((N_SHARDS, 32, PLE), F32)),
        scratch_shapes=[pltpu.VMEM((IN_WIDTH, D), BF16), pltpu.VMEM((2, tm, SOLO_ROWS), BF16),
                        pltpu.VMEM((2, tm, BLOCK), BF16),
                        pltpu.SemaphoreType.DMA((9,)), pltpu.SemaphoreType.DMA((9,)),
                        pltpu.SemaphoreType.DMA((3,)), pltpu.SemaphoreType.DMA((2,)),
                        pltpu.SemaphoreType.DMA((2,))],
        compiler_params=_params(("arbitrary", "arbitrary")),
    )(h, win_t, wdw_shard)


HALO = 32
CONV_RC = 64
CONV_LC = 256


def _conv_taps(w_ref, src, r0, lane0, offset_of_tap):
    lanes = pl.ds(lane0, CONV_LC)
    out = None
    for b in range(8):
        taps = [k for k in range(CONV_K) if offset_of_tap(k) % 8 == b]
        if not taps:
            continue
        rows = CONV_RC + (8 if b else 0)
        vb = None
        for k in taps:
            term = w_ref[k:k + 1, lanes] * src[pl.ds(r0 + (offset_of_tap(k) - b), rows), lanes]
            vb = term if vb is None else vb + term
        vb = vb[b:b + CONV_RC] if b else vb
        out = vb if out is None else out + vb
    return out


def _conv_fwd(z, wdw, b_dw, ln_g, ln_b, wall, S, tm, group, shards):
    T = z.shape[0]
    nt = S // tm
    hb = tm // HALO
    gathered = _group_shapes(group)

    def body(cv_ref, cg_ref, cgate_ref, hcv_ref, hcg_ref, wdw_ref, bdw_ref, lng_ref, lnb_ref, wpw_ref,
             wbrc_ref, *rest):
        shard_refs, rest = rest[:len(group)], rest[len(group):]
        ya_ref, y_ref, rstd_ref, pw_ref = rest[:4]
        gather_refs, (ubuf, cbuf), gather_scratch = rest[4:4 + len(gathered)], rest[-6:-4], rest[-4:]
        t = pl.program_id(1)
        step = pl.program_id(0) * nt + t
        finish_gather = _group_gather(group, shard_refs, gather_refs, gather_scratch, step, T // tm)
        ubuf[HALO:HALO + tm, :] = cv_ref[...].astype(F32) * _sig(cg_ref[...].astype(F32))
        hu = hcv_ref[...].astype(F32) * _sig(hcg_ref[...].astype(F32))
        ubuf[0:HALO, :] = jnp.where(t > 0, hu, 0.0)
        ubuf[HALO + tm:HALO + tm + 8, :] = jnp.zeros((8, D), F32)

        def chunk(ci, carry):
            r0 = pl.multiple_of(ci * CONV_RC, CONV_RC)
            for lg in range(D // CONV_LC):
                acc = _conv_taps(wdw_ref, ubuf, r0, lg * CONV_LC, lambda k: HALO - (CONV_K - 1) + k)
                cbuf[pl.ds(r0, CONV_RC), pl.ds(lg * CONV_LC, CONV_LC)] = acc
            return carry

        lax.fori_loop(0, tm // CONV_RC, chunk, 0)
        cc = cbuf[...] + bdw_ref[...]
        mu = jnp.mean(cc, axis=-1, keepdims=True)
        dd = cc - mu
        rstd = lax.rsqrt(jnp.mean(dd * dd, axis=-1, keepdims=True) + EPS)
        yn = dd * rstd
        y_ref[...] = yn.astype(BF16)
        rstd_ref[...] = rstd
        n = yn * lng_ref[...] + lnb_ref[...]
        s = n * _sig(n)
        pw = _mm(s.astype(BF16), wpw_ref[...])
        pw_ref[...] = pw.astype(BF16)
        gt = cgate_ref[...].astype(F32)
        ya_in = pw * (gt * _sig(gt))
        ya_ref[...] = _mm(ya_in.astype(BF16), wbrc_ref[...]).astype(BF16)
        pl.when(step == T // tm - 1)(finish_gather)

    def row(b, t):
        return b * nt + t

    def halo(b, t):
        return jnp.maximum(row(b, t) * hb - 1, 0)

    vec = pl.BlockSpec((1, D), lambda b, t: (0, 0))
    tile = lambda j: pl.BlockSpec((tm, D), lambda b, t: (row(b, t), j))
    out_tile = pl.BlockSpec((tm, D), lambda b, t: (row(b, t), 0))
    any_spec = pl.BlockSpec(memory_space=pl.ANY)
    return pl.pallas_call(
        body, name="conv_fwd", grid=(T // S, nt),
        in_specs=[tile(ZB_CVAL), tile(ZB_CGLU), tile(ZB_CGATE),
                  pl.BlockSpec((HALO, D), lambda b, t: (halo(b, t), ZB_CVAL)),
                  pl.BlockSpec((HALO, D), lambda b, t: (halo(b, t), ZB_CGLU)),
                  pl.BlockSpec((32, D), lambda b, t: (0, 0)), vec, vec, vec,
                  pl.BlockSpec((D, D), lambda b, t: (0, 0)),
                  pl.BlockSpec((D, D), lambda b, t: (1, 0))] + [any_spec] * len(group),
        out_specs=(out_tile, out_tile, pl.BlockSpec((tm, 1), lambda b, t: (row(b, t), 0)), out_tile)
        + (any_spec,) * len(gathered),
        out_shape=[jax.ShapeDtypeStruct((T, D), BF16), jax.ShapeDtypeStruct((T, D), BF16),
                   jax.ShapeDtypeStruct((T, 1), F32), jax.ShapeDtypeStruct((T, D), BF16)] + gathered,
        scratch_shapes=[pltpu.VMEM((tm + HALO + 8, D), F32), pltpu.VMEM((tm, D), F32)] + _group_scratch(group),
        compiler_params=_params(("arbitrary", "arbitrary")),
    )(z, z, z, z, z, wdw, b_dw, ln_g, ln_b, wall, wall, *shards)


def _swap_matrix():
    r = lax.broadcasted_iota(jnp.int32, (BLOCK, BLOCK), 0)
    l = lax.broadcasted_iota(jnp.int32, (BLOCK, BLOCK), 1)
    lh = l & (HEAD_DIM - 1)
    half = ROPE_DIM // 2
    hit = ((lh < half) & (r == l + half)) | ((lh >= half) & (lh < ROPE_DIM) & (r == l - half))
    return jnp.where(hit, 1.0, 0.0).astype(BF16)


def _rope(tb, cos, sin, pswap):
    return tb.astype(F32) * cos + _mm(tb, pswap) * sin


def _rope_f32(tv, cos, sin, pswap):
    hi = tv.astype(BF16)
    lo = (tv - hi.astype(F32)).astype(BF16)
    return tv * cos + (_mm(hi, pswap) + _mm(lo, pswap)) * sin


def _kv_variants(kv):
    lane = lax.broadcasted_iota(jnp.int32, kv.shape, 1)
    lo = lane < HEAD_DIM
    sw = pltpu.roll(kv, HEAD_DIM, 1)
    z = jnp.zeros_like(kv)
    g0 = (jnp.where(lo, kv, z).astype(BF16), jnp.where(lo, z, sw).astype(BF16))
    g1 = (jnp.where(lo, sw, z).astype(BF16), jnp.where(lo, z, kv).astype(BF16))
    return (g0, g1)


def _band_mask(nq):
    qi = lax.broadcasted_iota(jnp.int32, (nq * BLOCK, 2 * BLOCK), 0) & (BLOCK - 1)
    sj = lax.broadcasted_iota(jnp.int32, (nq * BLOCK, 2 * BLOCK), 1)
    return (sj <= qi + BLOCK) & (sj > qi), sj


def _sink_rep(sink_ref, g, e):
    return jnp.concatenate(
        [jnp.full((BLOCK, BLOCK), sink_ref[8 * g + 2 * j + e], F32) for j in range(4)], axis=0)


def _softmax_parts(s, valid, sk):
    rows = s.shape[0]
    s = jnp.where(valid, s, -1e30)
    m = jnp.maximum(jnp.broadcast_to(jnp.max(s, axis=-1, keepdims=True), (rows, BLOCK)), sk)
    return jnp.exp(s - jnp.concatenate([m, m], axis=1)), jnp.exp(sk - m)


def _softmax_sink(s, valid, sk):
    p, ps = _softmax_parts(s, valid, sk)
    inv = 1.0 / (_mm(p.astype(BF16), jnp.ones((2 * BLOCK, BLOCK), BF16)) + ps)
    return p * jnp.concatenate([inv, inv], axis=1), ps * inv


def _attn_fwd(z, zkv, cos_t, sin_t, sinks, S, tq, group, shards):
    T = z.shape[0]
    nt = S // tq
    nq = tq // BLOCK
    gathered = _group_shapes(group)

    def body(sink_ref, q_ref, kv_ref, hkv_ref, cos_ref, sin_ref, hcos_ref, hsin_ref, *rest):
        shard_refs, o_ref = rest[:len(group)], rest[len(group)]
        t = pl.program_id(1)
        step = pl.program_id(0) * nt + t
        finish_gather = _group_gather(group, shard_refs, rest[len(group) + 1:-4], rest[-4:], step, T // tq)
        cos = cos_ref[...]
        sin = sin_ref[...]
        pswap = _swap_matrix()
        kv = jnp.concatenate([hkv_ref[...], kv_ref[...]], axis=0)
        cos_k = jnp.concatenate([hcos_ref[...], cos], axis=0)
        sin_k = jnp.concatenate([hsin_ref[...], sin], axis=0)
        kx = _kv_variants(_rope(kv[:, :BLOCK], cos_k, sin_k, pswap))
        one = jnp.ones((tq + BLOCK, BLOCK), BF16)
        vx = [[jnp.concatenate([v, one], axis=1) for v in vg] for vg in _kv_variants(kv[:, BLOCK:].astype(F32))]
        band, sj = _band_mask(4)
        qs = [(_rope(q_ref[:, 128 * hp:128 * hp + 128], cos, sin, pswap) * 0.125).astype(BF16)
              for hp in range(8)]
        for n in range(nq):
            first = (t == 0) & (n == 0)
            valid = band & (jnp.logical_not(first) | (sj >= BLOCK))
            r0 = n * BLOCK
            for g in range(2):
                lhs = jnp.concatenate([qs[4 * g + j][r0:r0 + BLOCK] for j in range(4)], axis=0)
                acc = jnp.zeros((4 * BLOCK, BLOCK), F32)
                for e in range(2):
                    s = _mm_nt(lhs, kx[g][e][r0:r0 + 2 * BLOCK])
                    p, ps = _softmax_parts(s, valid, _sink_rep(sink_ref, g, e))
                    r = _mm(p.astype(BF16), vx[g][e][r0:r0 + 2 * BLOCK])
                    acc = acc + r[:, 0:BLOCK] * (1.0 / (r[:, BLOCK:2 * BLOCK] + ps))
                for j in range(4):
                    o_ref[r0:r0 + BLOCK, 128 * (4 * g + j):128 * (4 * g + j) + 128] = (
                        acc[j * BLOCK:(j + 1) * BLOCK].astype(BF16))
        pl.when(step == T // tq - 1)(finish_gather)

    def row(b, t):
        return b * nt + t

    def halo(b, t):
        return jnp.maximum(row(b, t) * nq - 1, 0)

    any_spec = pl.BlockSpec(memory_space=pl.ANY)
    return pl.pallas_call(
        body, name="attn_fwd", grid=(T // S, nt),
        in_specs=[pl.BlockSpec(memory_space=pltpu.SMEM),
                  pl.BlockSpec((tq, D), lambda b, t: (row(b, t), ZB_Q)),
                  pl.BlockSpec((tq, 2 * BLOCK), lambda b, t: (row(b, t), 0)),
                  pl.BlockSpec((BLOCK, 2 * BLOCK), lambda b, t: (halo(b, t), 0)),
                  pl.BlockSpec((tq, BLOCK), lambda b, t: (row(b, t), 0)),
                  pl.BlockSpec((tq, BLOCK), lambda b, t: (row(b, t), 0)),
                  pl.BlockSpec((BLOCK, BLOCK), lambda b, t: (halo(b, t), 0)),
                  pl.BlockSpec((BLOCK, BLOCK), lambda b, t: (halo(b, t), 0))] + [any_spec] * len(group),
        out_specs=(pl.BlockSpec((tq, D), lambda b, t: (row(b, t), 0)),) + (any_spec,) * len(gathered),
        out_shape=[jax.ShapeDtypeStruct((T, D), BF16)] + gathered,
        scratch_shapes=_group_scratch(group),
        compiler_params=_params(("arbitrary", "arbitrary")),
    )(sinks, z, zkv, zkv, cos_t, sin_t, cos_t, sin_t, *shards)


def _tail_a(x, tgt, p, o, ya, z, ln_post, wall_b, wppt, tm):
    T = x.shape[0]
    last = T // tm - 1

    def body(x_ref, tgt_ref, p_ref, o_ref, ya_ref, ag_ref, gc_ref, ga_ref, lnp_ref, wbra_ref, wout_ref,
             wpg_ref, wppt_ref, loss_ref, dx1_ref, dm_ref, yb_ref, glnp_ref, gpack_ref, gwpp_ref,
             acc_out, acc_pg, sem):
        i = pl.program_id(0)

        @pl.when(i == 0)
        def _():
            acc_out[...] = jnp.zeros_like(acc_out)
            acc_pg[...] = jnp.zeros_like(acc_pg)
            gwpp_ref[...] = jnp.zeros_like(gwpp_ref)
            glnp_ref[...] = jnp.zeros_like(glnp_ref)
            loss_ref[...] = jnp.zeros_like(loss_ref)

        ag = ag_ref[...].astype(F32)
        yb_in = (o_ref[...].astype(F32) * (ag * _sig(ag))).astype(BF16)
        yb = _mm(yb_in, wbra_ref[...])
        yb_ref[...] = yb.astype(BF16)
        m = (_sig(gc_ref[...].astype(F32)) * ya_ref[...].astype(F32)
             + _sig(ga_ref[...].astype(F32)) * yb).astype(BF16)
        mo = _mm(m, wout_ref[...])
        r2 = lax.rsqrt(jnp.mean(mo * mo, axis=-1, keepdims=True) + EPS)
        nrm = mo * r2
        g_post = lnp_ref[...]
        x1 = x_ref[...] + nrm * g_post
        x1b = x1.astype(BF16)
        gate = _sig(_mm(x1b, wpg_ref[...]))
        pb = p_ref[...].astype(BF16)
        pp = _mm_nt(pb, wppt_ref[...])
        err = x1 + gate * pp - tgt_ref[...]
        loss_ref[...] += 0.5 * jnp.sum(jnp.sum(err * err, axis=-1, keepdims=True) * (1.0 / D),
                                       axis=0, keepdims=True)
        dx2 = err * (1.0 / D)
        dgp = (dx2 * pp * gate * (1.0 - gate)).astype(BF16)
        dpp = (dx2 * gate).astype(BF16)
        dx1 = dx2 + _mm_nt(dgp, wpg_ref[...])
        dx1_ref[...] = dx1
        acc_pg[...] += _mm_tn(x1b, dgp)
        gwpp_ref[...] += _mm_tn(dpp, pb)
        glnp_ref[...] += jnp.sum(dx1 * nrm, axis=0, keepdims=True)
        a = dx1 * g_post
        dmo = (r2 * (a - nrm * jnp.mean(a * nrm, axis=-1, keepdims=True))).astype(BF16)
        dm_ref[...] = _mm_nt(dmo, wout_ref[...]).astype(BF16)
        acc_out[...] += _mm_tn(m, dmo)

        @pl.when(i == last)
        def _():
            _flush_all([(acc_out, 3 * D, sem.at[0]), (acc_pg, 4 * D, sem.at[1])], gpack_ref)

    tile = pl.BlockSpec((tm, D), lambda i: (i, 0))
    ztile = lambda j: pl.BlockSpec((tm, D), lambda i: (i, j))
    wsq = lambda k: pl.BlockSpec((D, D), lambda i: (k, 0))
    const = lambda shp: pl.BlockSpec(shp, lambda i: (0, 0))
    any_spec = pl.BlockSpec(memory_space=pl.ANY)
    return pl.pallas_call(
        body, name="tail_a", grid=(T // tm,),
        in_specs=[tile, tile, pl.BlockSpec((tm, PLE), lambda i: (i, 0)), tile, tile, ztile(ZB_AGATE),
                  ztile(ZB_GCONV), ztile(ZB_GATTN), const((1, D)), wsq(0), wsq(1), wsq(2), const((D, PLE))],
        out_specs=(const((1, 1)), tile, tile, tile, const((1, D)), any_spec, const((D, PLE))),
        out_shape=(jax.ShapeDtypeStruct((1, 1), F32), jax.ShapeDtypeStruct((T, D), F32),
                   jax.ShapeDtypeStruct((T, D), BF16), jax.ShapeDtypeStruct((T, D), BF16),
                   jax.ShapeDtypeStruct((1, D), F32), jax.ShapeDtypeStruct((N_SHARDS, SQ_PACK, D), F32),
                   jax.ShapeDtypeStruct((D, PLE), F32)),
        scratch_shapes=[pltpu.VMEM((D, D), F32), pltpu.VMEM((D, D), F32), pltpu.SemaphoreType.DMA((2,))],
        compiler_params=_params(("arbitrary",)),
    )(x, tgt, p, o, ya, z, z, z, ln_post, wall_b, wall_b, wall_b, wppt)


def _dsilu(v, sg):
    return sg * (1.0 + v * (1.0 - sg))


def _tail_b(dm, ya, yb, o, z, pw, y, rstd, ln_g, ln_b, wall_a, wall_b, gppt, gpack, tm):
    T = dm.shape[0]
    last = T // tm - 1

    def body(dm_ref, ya_ref, yb_ref, o_ref, ag_ref, gc_ref, ga_ref, cgate_ref, pw_ref, y_ref, rstd_ref,
             lng_ref, lnb_ref, wpw_ref, wbrc_ref, wbra_ref, gppt_ref, gpack_in, dg_ref, do_ref, dc_ref,
             gvec_ref, gpack_ref, acc_bra, acc_brc, acc_pw, sem):
        i = pl.program_id(0)

        @pl.when(i == 0)
        def _():
            acc_bra[...] = jnp.zeros_like(acc_bra)
            acc_brc[...] = jnp.zeros_like(acc_brc)
            acc_pw[...] = jnp.zeros_like(acc_pw)
            gvec_ref[...] = jnp.zeros_like(gvec_ref)

        g = lng_ref[...]

        def part(rs):
            dm_v = dm_ref[rs, :].astype(F32)
            sgc = _sig(gc_ref[rs, :].astype(F32))
            sga = _sig(ga_ref[rs, :].astype(F32))
            dya = (dm_v * sgc).astype(BF16)
            dyb = (dm_v * sga).astype(BF16)
            dg_ref[rs, D:2 * D] = (dm_v * ya_ref[rs, :].astype(F32) * sgc * (1.0 - sgc)).astype(BF16)
            dg_ref[rs, 2 * D:3 * D] = (dm_v * yb_ref[rs, :].astype(F32) * sga * (1.0 - sga)).astype(BF16)
            ag = ag_ref[rs, :].astype(F32)
            sag = _sig(ag)
            sa = ag * sag
            ov = o_ref[rs, :].astype(F32)
            dyb_in = _mm_nt(dyb, wbra_ref[...])
            do_ref[rs, :] = (dyb_in * sa).astype(BF16)
            dg_ref[rs, 0:D] = (dyb_in * ov * _dsilu(ag, sag)).astype(BF16)
            gt = cgate_ref[rs, :].astype(F32)
            sgt = _sig(gt)
            sgate = gt * sgt
            pw = pw_ref[rs, :].astype(F32)
            dya_in = _mm_nt(dya, wbrc_ref[...])
            dpw = (dya_in * sgate).astype(BF16)
            dg_ref[rs, 3 * D:4 * D] = (dya_in * pw * _dsilu(gt, sgt)).astype(BF16)
            yn = y_ref[rs, :].astype(F32)
            n = yn * g + lnb_ref[...]
            sn = _sig(n)
            dn = _mm_nt(dpw, wpw_ref[...]) * _dsilu(n, sn)
            dy = dn * g
            dc = rstd_ref[rs, :] * (dy - jnp.mean(dy, axis=-1, keepdims=True)
                                    - yn * jnp.mean(dy * yn, axis=-1, keepdims=True))
            dc_ref[rs, :] = dc.astype(BF16)
            sums = (jnp.sum(dn * yn, axis=0, keepdims=True), jnp.sum(dn, axis=0, keepdims=True),
                    jnp.sum(dc, axis=0, keepdims=True))
            return ((ov * sa).astype(BF16), dyb, (pw * sgate).astype(BF16), dya, (n * sn).astype(BF16), dpw,
                    sums)

        parts = [part(pl.ds(r * (tm // TAIL_PARTS), tm // TAIL_PARTS)) for r in range(TAIL_PARTS)]
        cat = lambda j: jnp.concatenate([pt[j] for pt in parts], axis=0)
        acc_bra[...] += _mm_tn(cat(0), cat(1))
        acc_brc[...] += _mm_tn(cat(2), cat(3))
        acc_pw[...] += _mm_tn(cat(4), cat(5))
        for j in range(3):
            gvec_ref[j:j + 1, :] += sum(pt[6][j] for pt in parts)

        @pl.when(i == last)
        def _():
            _flush_all([(acc_pw, 0, sem.at[0]), (acc_brc, D, sem.at[1]), (acc_bra, 2 * D, sem.at[2]),
                        (gppt_ref, WPP0, sem.at[3])], gpack_ref)

    tile = pl.BlockSpec((tm, D), lambda i: (i, 0))
    ztile = lambda j: pl.BlockSpec((tm, D), lambda i: (i, j))
    wsq = lambda k: pl.BlockSpec((D, D), lambda i: (k, 0))
    const = lambda shp: pl.BlockSpec(shp, lambda i: (0, 0))
    any_spec = pl.BlockSpec(memory_space=pl.ANY)
    return pl.pallas_call(
        body, name="tail_b", grid=(T // tm,),
        in_specs=[tile, tile, tile, tile, ztile(ZB_AGATE), ztile(ZB_GCONV), ztile(ZB_GATTN), ztile(ZB_CGATE),
                  tile, tile, pl.BlockSpec((tm, 1), lambda i: (i, 0)), const((1, D)), const((1, D)), wsq(0),
                  wsq(1), wsq(0), const((PLE, D)), any_spec],
        out_specs=(pl.BlockSpec((tm, 4 * D), lambda i: (i, 0)), tile, tile, const((8, D)), any_spec),
        out_shape=(jax.ShapeDtypeStruct((T, 7 * D), BF16), jax.ShapeDtypeStruct((T, D), BF16),
                   jax.ShapeDtypeStruct((T, D), BF16), jax.ShapeDtypeStruct((8, D), F32),
                   jax.ShapeDtypeStruct(gpack.shape, F32)),
        input_output_aliases={17: 4},
        scratch_shapes=[pltpu.VMEM((D, D), F32), pltpu.VMEM((D, D), F32), pltpu.VMEM((D, D), F32),
                        pltpu.SemaphoreType.DMA((4,))],
        compiler_params=_params(("arbitrary",)),
    )(dm, ya, yb, o, z, z, z, z, pw, y, rstd, ln_g, ln_b, wall_a, wall_a, wall_b, gppt, gpack)


def _conv_bwd(dc, z, wdw, dz, S, tm, copies, src, landing):
    T = dc.shape[0]
    nt = S // tm
    hb = tm // HALO
    nrows = T // HALO

    def body(dc_ref, hdc_ref, cv_ref, cg_ref, hcv_ref, hcg_ref, wdw_ref, dz_in, src_ref, dz_ref, gw_ref,
             land_ref, ubuf, dcbuf, dubuf, dwacc, shbuf, send_sems, recv_sems):
        b = pl.program_id(0)
        t = pl.program_id(1)

        @pl.when((b == 0) & (t == 0))
        def _():
            dwacc[...] = jnp.zeros_like(dwacc)
            for cp in copies(src_ref, land_ref, send_sems, recv_sems):
                cp.start()

        cv = cv_ref[...].astype(F32)
        sg = _sig(cg_ref[...].astype(F32))
        ubuf[HALO:HALO + tm, :] = cv * sg
        hu = hcv_ref[...].astype(F32) * _sig(hcg_ref[...].astype(F32))
        ubuf[0:HALO, :] = jnp.where(t > 0, hu, 0.0)
        ubuf[HALO + tm:HALO + tm + 8, :] = jnp.zeros((8, D), F32)
        dcbuf[0:tm, :] = dc_ref[...].astype(F32)
        dcbuf[tm:tm + HALO, :] = jnp.where(t < nt - 1, hdc_ref[...].astype(F32), 0.0)
        dcbuf[tm + HALO:tm + HALO + 8, :] = jnp.zeros((8, D), F32)

        def chunk(ci, carry):
            r0 = pl.multiple_of(ci * CONV_RC, CONV_RC)
            for lg in range(D // CONV_LC):
                l0 = lg * CONV_LC
                dubuf[pl.ds(r0, CONV_RC), pl.ds(l0, CONV_LC)] = _conv_taps(
                    wdw_ref, dcbuf, r0, l0, lambda k: CONV_K - 1 - k)
                dcc = dcbuf[pl.ds(r0, CONV_RC), pl.ds(l0, CONV_LC)]
                zero8 = jnp.zeros((8, CONV_LC), F32)
                dcz = jnp.concatenate([zero8, dcc, zero8], axis=0)
                for bb in range(8):
                    taps = [k for k in range(CONV_K) if (HALO - (CONV_K - 1) + k) % 8 == bb]
                    if not taps:
                        continue
                    rows = CONV_RC + (8 if bb else 0)
                    if bb:
                        shbuf[bb] = dcz[8 - bb:8 - bb + rows]
                    for k in taps:
                        a8 = HALO - (CONV_K - 1) + k - bb
                        dcs = shbuf[bb] if bb else dcc
                        prod = dcs * ubuf[pl.ds(r0 + a8, rows), pl.ds(l0, CONV_LC)]
                        part = prod[0:8]
                        for q in range(1, rows // 8):
                            part = part + prod[8 * q:8 * q + 8]
                        dwacc[8 * k:8 * k + 8, pl.ds(l0, CONV_LC)] += part
            return carry

        lax.fori_loop(0, tm // CONV_RC, chunk, 0)
        du = dubuf[...]
        dz_ref[:, 0:D] = (du * sg).astype(BF16)
        dz_ref[:, D:2 * D] = (du * cv * sg * (1.0 - sg)).astype(BF16)

        @pl.when((b == pl.num_programs(0) - 1) & (t == nt - 1))
        def _():
            for k in range(32):
                gw_ref[k:k + 1, :] = jnp.sum(dwacc[8 * k:8 * k + 8, :], axis=0, keepdims=True)
            cps = copies(src_ref, land_ref, send_sems, recv_sems)
            for cp in cps:
                cp.wait_recv()
            for cp in cps:
                cp.wait_send()

    def row(b, t):
        return b * nt + t

    def prev_halo(b, t):
        return jnp.maximum(row(b, t) * hb - 1, 0)

    def next_halo(b, t):
        return jnp.minimum((row(b, t) + 1) * hb, nrows - 1)

    return pl.pallas_call(
        body, name="conv_bwd", grid=(T // S, nt),
        in_specs=[pl.BlockSpec((tm, D), lambda b, t: (row(b, t), 0)),
                  pl.BlockSpec((HALO, D), lambda b, t: (next_halo(b, t), 0)),
                  pl.BlockSpec((tm, D), lambda b, t: (row(b, t), ZB_CVAL)),
                  pl.BlockSpec((tm, D), lambda b, t: (row(b, t), ZB_CGLU)),
                  pl.BlockSpec((HALO, D), lambda b, t: (prev_halo(b, t), ZB_CVAL)),
                  pl.BlockSpec((HALO, D), lambda b, t: (prev_halo(b, t), ZB_CGLU)),
                  pl.BlockSpec((32, D), lambda b, t: (0, 0)),
                  pl.BlockSpec(memory_space=pl.ANY), pl.BlockSpec(memory_space=pl.ANY)],
        out_specs=(pl.BlockSpec((tm, 2 * D), lambda b, t: (row(b, t), ZB_CVAL // 2)),
                   pl.BlockSpec((32, D), lambda b, t: (0, 0)), pl.BlockSpec(memory_space=pl.ANY)),
        out_shape=(jax.ShapeDtypeStruct(dz.shape, BF16), jax.ShapeDtypeStruct((32, D), F32), landing),
        input_output_aliases={7: 0},
        scratch_shapes=[pltpu.VMEM((tm + HALO + 8, D), F32), pltpu.VMEM((tm + HALO + 8, D), F32),
                        pltpu.VMEM((tm, D), F32), pltpu.VMEM((8 * 32, D), F32),
                        pltpu.VMEM((8, CONV_RC + 8, CONV_LC), F32), pltpu.SemaphoreType.DMA((3,)),
                        pltpu.SemaphoreType.DMA((3,))],
        compiler_params=_params(("arbitrary", "arbitrary")),
    )(dc, dc, z, z, z, z, wdw, dz, src)


def _attn_bwd(z, zkv, o, do, cos_t, sin_t, sinks, dz, S, tq, copies, src, landing):
    T = z.shape[0]
    nt = S // tq
    nq = tq // BLOCK

    def body(sink_ref, q_ref, kv_ref, hkv_ref, o_ref, do_ref, cos_ref, sin_ref, hcos_ref, hsin_ref, dz_in,
             src_ref, dq_ref, dkv_ref, gs_ref, land_ref, carry, dkacc, dvacc, send_sems, recv_sems):
        b = pl.program_id(0)
        tt = pl.program_id(1)
        t = nt - 1 - tt

        @pl.when((b == 0) & (tt == 0))
        def _():
            gs_ref[...] = jnp.zeros_like(gs_ref)
            for cp in copies(src_ref, land_ref, send_sems, recv_sems):
                cp.start()

        @pl.when(tt == 0)
        def _():
            carry[...] = jnp.zeros_like(carry)

        cos = cos_ref[...]
        sin = sin_ref[...]
        pswap = _swap_matrix()
        kv = jnp.concatenate([hkv_ref[...], kv_ref[...]], axis=0)
        cos_k = jnp.concatenate([hcos_ref[...], cos], axis=0)
        sin_k = jnp.concatenate([hsin_ref[...], sin], axis=0)
        kx = _kv_variants(_rope(kv[:, :BLOCK], cos_k, sin_k, pswap))
        vx = _kv_variants(kv[:, BLOCK:].astype(F32))
        band, sj = _band_mask(4)
        lo = lax.broadcasted_iota(jnp.int32, (4 * BLOCK, BLOCK), 1) < HEAD_DIM
        ones = jnp.ones((2 * BLOCK, 2 * BLOCK), BF16)
        qs = [(_rope(q_ref[:, 128 * hp:128 * hp + 128], cos, sin, pswap) * 0.125).astype(BF16)
              for hp in range(8)]
        dkacc[...] = jnp.zeros_like(dkacc)
        dvacc[...] = jnp.zeros_like(dvacc)
        gsum = jnp.zeros((1, BLOCK), F32)
        hlane = lax.broadcasted_iota(jnp.int32, (1, BLOCK), 1)
        for n in range(nq):
            first = (t == 0) & (n == 0)
            valid = band & (jnp.logical_not(first) | (sj >= BLOCK))
            r0 = n * BLOCK
            for g in range(2):
                cols = [slice(128 * (4 * g + j), 128 * (4 * g + j) + 128) for j in range(4)]
                lhs = jnp.concatenate([qs[4 * g + j][r0:r0 + BLOCK] for j in range(4)], axis=0)
                dov = jnp.concatenate([do_ref[r0:r0 + BLOCK, cs] for cs in cols], axis=0)
                prod = dov.astype(F32) * jnp.concatenate(
                    [o_ref[r0:r0 + BLOCK, cs] for cs in cols], axis=0).astype(F32)
                lhs_t = lhs.T
                dov_t = dov.T
                dq = jnp.zeros((4 * BLOCK, BLOCK), F32)
                dk_t = jnp.zeros((HEAD_DIM, 2 * BLOCK), F32)
                dv_t = jnp.zeros((HEAD_DIM, 2 * BLOCK), F32)
                for e in range(2):
                    kw = kx[g][e][r0:r0 + 2 * BLOCK]
                    vw = vx[g][e][r0:r0 + 2 * BLOCK]
                    s = _mm_nt(lhs, kw)
                    p, psink = _softmax_sink(s, valid, _sink_rep(sink_ref, g, e))
                    pe = jnp.where(lo if e == 0 else jnp.logical_not(lo), prod, 0.0)
                    pe_hi = pe.astype(BF16)
                    pe_lo = (pe - pe_hi.astype(F32)).astype(BF16)
                    delta = _mm(jnp.concatenate([pe_hi, pe_lo], axis=1), ones)
                    ds = (p * (_mm_nt(dov, vw) - delta)).astype(BF16)
                    dq = dq + _mm(ds, kw)
                    dims = slice(HEAD_DIM * e, HEAD_DIM * (e + 1))
                    dk_t = dk_t + _mm(lhs_t[dims], ds)
                    dv_t = dv_t + _mm(dov_t[dims], p.astype(BF16))
                    gs = -psink * delta[:, 0:BLOCK]
                    for j in range(4):
                        tot = jnp.sum(gs[j * BLOCK:(j + 1) * BLOCK], axis=0, keepdims=True)
                        gsum = gsum + jnp.where(hlane == 8 * g + 2 * j + e, tot, 0.0)
                dkacc[HEAD_DIM * g:HEAD_DIM * (g + 1), r0:r0 + 2 * BLOCK] += dk_t
                dvacc[HEAD_DIM * g:HEAD_DIM * (g + 1), r0:r0 + 2 * BLOCK] += dv_t
                for j in range(4):
                    dqj = _rope_f32(dq[j * BLOCK:(j + 1) * BLOCK] * 0.125, cos[r0:r0 + BLOCK],
                                    -sin[r0:r0 + BLOCK], pswap)
                    dq_ref[r0:r0 + BLOCK, cols[j]] = dqj.astype(BF16)
        gs_ref[0:1, :] += gsum
        dk_all = dkacc[...]
        dv_all = dvacc[...]
        dk_last = dk_all[:, tq:tq + BLOCK] + carry[0:BLOCK, :]
        dv_last = dv_all[:, tq:tq + BLOCK] + carry[BLOCK:2 * BLOCK, :]
        carry[0:BLOCK, :] = dk_all[:, 0:BLOCK]
        carry[BLOCK:2 * BLOCK, :] = dv_all[:, 0:BLOCK]
        if nq > 1:
            dk_tile = jnp.concatenate([dk_all[:, BLOCK:tq], dk_last], axis=1)
            dv_tile = jnp.concatenate([dv_all[:, BLOCK:tq], dv_last], axis=1)
        else:
            dk_tile, dv_tile = dk_last, dv_last
        dkv_ref[:, 0:BLOCK] = _rope_f32(dk_tile.T, cos, -sin, pswap).astype(BF16)
        dkv_ref[:, BLOCK:2 * BLOCK] = dv_tile.T.astype(BF16)

        @pl.when((b == pl.num_programs(0) - 1) & (tt == nt - 1))
        def _():
            cps = copies(src_ref, land_ref, send_sems, recv_sems)
            for cp in cps:
                cp.wait_recv()
            for cp in cps:
                cp.wait_send()

    def row(b, tt):
        return b * nt + (nt - 1 - tt)

    def halo(b, tt):
        return jnp.maximum(row(b, tt) * nq - 1, 0)

    tile = pl.BlockSpec((tq, D), lambda b, tt: (row(b, tt), 0))
    return pl.pallas_call(
        body, name="attn_bwd", grid=(T // S, nt),
        in_specs=[pl.BlockSpec(memory_space=pltpu.SMEM),
                  pl.BlockSpec((tq, D), lambda b, tt: (row(b, tt), ZB_Q)),
                  pl.BlockSpec((tq, 2 * BLOCK), lambda b, tt: (row(b, tt), 0)),
                  pl.BlockSpec((BLOCK, 2 * BLOCK), lambda b, tt: (halo(b, tt), 0)),
                  tile, tile,
                  pl.BlockSpec((tq, BLOCK), lambda b, tt: (row(b, tt), 0)),
                  pl.BlockSpec((tq, BLOCK), lambda b, tt: (row(b, tt), 0)),
                  pl.BlockSpec((BLOCK, BLOCK), lambda b, tt: (halo(b, tt), 0)),
                  pl.BlockSpec((BLOCK, BLOCK), lambda b, tt: (halo(b, tt), 0)),
                  pl.BlockSpec(memory_space=pl.ANY), pl.BlockSpec(memory_space=pl.ANY)],
        out_specs=(pl.BlockSpec((tq, D), lambda b, tt: (row(b, tt), ZB_Q)),
                   pl.BlockSpec((tq, 2 * BLOCK), lambda b, tt: (row(b, tt), 0)),
                   pl.BlockSpec((8, BLOCK), lambda b, tt: (0, 0)), pl.BlockSpec(memory_space=pl.ANY)),
        out_shape=(jax.ShapeDtypeStruct(dz.shape, BF16), jax.ShapeDtypeStruct((T, 2 * BLOCK), BF16),
                   jax.ShapeDtypeStruct((8, BLOCK), F32), landing),
        input_output_aliases={10: 0},
        scratch_shapes=[pltpu.VMEM((2 * BLOCK, BLOCK), F32), pltpu.VMEM((BLOCK, tq + BLOCK), F32),
                        pltpu.VMEM((BLOCK, tq + BLOCK), F32), pltpu.SemaphoreType.DMA((3,)),
                        pltpu.SemaphoreType.DMA((3,))],
        compiler_params=_params(("arbitrary", "arbitrary")),
    )(sinks, z, zkv, zkv, o, do, cos_t, sin_t, cos_t, sin_t, dz, src)


def _dh(dz, dz_kv, wall, x, dx1, ln_pre, tm, tile0, ntiles, gx_prev, name, copies, src, landing):
    T = x.shape[0]
    nsem = 3

    def body(*refs):
        dz_ref, kv_ref, wt_ref, x_ref, dx1_ref, g_ref, src_ref = refs[:7]
        gx_ref, glp_ref, land_ref, wbuf, send_sems, recv_sems, wsem = refs[-7:]
        i = pl.program_id(0)

        @pl.when(i == 0)
        def _():
            glp_ref[...] = jnp.zeros_like(glp_ref)
            for cp in copies(src_ref, land_ref, send_sems, recv_sems):
                cp.start()
            load = pltpu.make_async_copy(wt_ref, wbuf, wsem)
            load.start()
            load.wait()

        dh = _mm(dz_ref[...], wbuf[0:ZKV, :]) + _mm(kv_ref[...], wbuf[ZKV:IN_WIDTH, :])
        xv = x_ref[...]
        r = lax.rsqrt(jnp.mean(xv * xv, axis=-1, keepdims=True) + EPS)
        xr = xv * r
        glp_ref[...] += jnp.sum(dh * xr, axis=0, keepdims=True)
        a = dh * g_ref[...]
        gx_ref[...] = dx1_ref[...] + r * (a - xr * jnp.mean(a * xr, axis=-1, keepdims=True))

        @pl.when(i == ntiles - 1)
        def _():
            cps = copies(src_ref, land_ref, send_sems, recv_sems)
            for cp in cps:
                cp.wait_recv()
            for cp in cps:
                cp.wait_send()

    tile = pl.BlockSpec((tm, D), lambda i: (tile0 + i, 0))
    any_spec = pl.BlockSpec(memory_space=pl.ANY)
    operands = [dz, dz_kv, wall, x, dx1, ln_pre, src] + ([] if gx_prev is None else [gx_prev])
    return pl.pallas_call(
        body, name=name, grid=(ntiles,),
        in_specs=[pl.BlockSpec((tm, ZKV), lambda i: (tile0 + i, 0)),
                  pl.BlockSpec((tm, 2 * BLOCK), lambda i: (tile0 + i, 0)),
                  any_spec, tile, tile, pl.BlockSpec((1, D), lambda i: (0, 0)), any_spec]
        + ([] if gx_prev is None else [any_spec]),
        out_specs=(tile, pl.BlockSpec((1, D), lambda i: (0, 0)), any_spec),
        out_shape=(jax.ShapeDtypeStruct((T, D), F32), jax.ShapeDtypeStruct((1, D), F32), landing),
        input_output_aliases={} if gx_prev is None else {7: 0},
        scratch_shapes=[pltpu.VMEM((IN_WIDTH, D), BF16), pltpu.SemaphoreType.DMA((nsem,)),
                        pltpu.SemaphoreType.DMA((nsem,)), pltpu.SemaphoreType.DMA],
        compiler_params=_params(("arbitrary",)),
    )(*operands)


def _gwt(dz, dz_kv, h, tt):
    T = dz.shape[0]
    nt = T // tt
    last = nt - 1
    kv = 2 * BLOCK

    def body(dz_ref, dzkv_ref, h_ref, gpack_ref, hbuf, acc, hsems, sems):
        j = pl.program_id(0)
        t = pl.program_id(1)
        slot = j % 2
        rows = pl.ds(pl.multiple_of(t * tt, tt), tt)

        def h_load(i):
            return pltpu.make_async_copy(h_ref.at[pl.ds(i * tt, tt)], hbuf.at[pl.ds(i * tt, tt)], hsems.at[i])

        @pl.when((j == 0) & (t == 0))
        def _():
            for i in range(nt):
                h_load(i).start()

        for i in range(nt):
            pl.when((j == 0) & (t == i))(h_load(i).wait)

        @pl.when((j < 7) & (t == 0))
        def _():
            acc[slot] = _mm_tn(dz_ref[...], hbuf[rows, :])

        @pl.when((j < 7) & (t > 0))
        def _():
            acc[slot] += _mm_tn(dz_ref[...], hbuf[rows, :])

        @pl.when((j == 7) & (t == 0))
        def _():
            acc[1, 0:kv, :] = _mm_tn(dzkv_ref[...], hbuf[rows, :])

        @pl.when((j == 7) & (t > 0))
        def _():
            acc[1, 0:kv, :] += _mm_tn(dzkv_ref[...], hbuf[rows, :])

        def block_total(sl):
            return pltpu.make_async_copy(acc.at[sl], gpack_ref.at[0, pl.ds(0, D)], sems.at[sl])

        for jj in range(8):
            @pl.when((t == last) & (j == jj))
            def _(jj=jj):
                if jj >= 1:
                    block_total((jj - 1) % 2).wait()
                if jj == 7:
                    _flush_to_pack(acc.at[1, pl.ds(0, kv)], gpack_ref, WT0 + ZKV, sems.at[1])
                else:
                    for cp in _pack_copies(acc.at[jj % 2], gpack_ref, WT0 + jj * D, sems.at[jj % 2]):
                        cp.start()

    any_spec = pl.BlockSpec(memory_space=pl.ANY)
    return pl.pallas_call(
        body, name="gwt", grid=(8, nt),
        in_specs=[pl.BlockSpec((tt, D), lambda j, t: (jnp.where(j == 7, last, t), jnp.minimum(j, 6))),
                  pl.BlockSpec((tt, kv), lambda j, t: (jnp.where(j == 7, t, 0), 0)), any_spec],
        out_specs=any_spec,
        out_shape=jax.ShapeDtypeStruct((N_SHARDS, WIN_SHARD, D), F32),
        scratch_shapes=[pltpu.VMEM((T, D), BF16), pltpu.VMEM((2, D, D), F32), pltpu.SemaphoreType.DMA((nt,)),
                        pltpu.SemaphoreType.DMA((2,))],
        compiler_params=_params(("arbitrary", "arbitrary")),
    )(dz, dz_kv, h)


_BC1 = 1.0 - ADAM_B1 ** ADAM_STEP
_BC2 = 1.0 - ADAM_B2 ** ADAM_STEP


def _adamw_math(w, g, m, v):
    m = ADAM_B1 * m + (1.0 - ADAM_B1) * g
    v = ADAM_B2 * v + (1.0 - ADAM_B2) * (g * g)
    delta = -ADAM_LR * ((m / _BC1) / (jnp.sqrt(v / _BC2) + ADAM_EPS) + ADAM_WD * w)
    return delta, m, v


def _adamw_rows(g, w, m, v, rows, name):
    R, C = w.shape

    def body(g_ref, w_ref, m_ref, v_ref, go_ref, d_ref, nm_ref, nv_ref):
        gv = g_ref[...]
        d, nm, nv = _adamw_math(w_ref[...], gv, m_ref[...], v_ref[...])
        go_ref[...] = gv
        d_ref[...] = d
        nm_ref[...] = nm
        nv_ref[...] = nv

    spec = pl.BlockSpec((rows, C), lambda i: (i, 0))
    shp = jax.ShapeDtypeStruct((R, C), F32)
    return pl.pallas_call(
        body, name=name, grid=(R // rows,), in_specs=[spec] * 4, out_specs=(spec,) * 4,
        out_shape=(shp,) * 4, compiler_params=_params(("arbitrary",)),
    )(g, w, m, v)


def _adamw_square(gfin, ws, ms, vs):
    rb = 64
    nb = SQ_SHARD // rb

    def body(*refs):
        g_refs = refs[0:5]
        w_refs, m_refs, v_refs = refs[5:10], refs[10:15], refs[15:20]
        outs = refs[20:]
        for k in range(5):
            gk = g_refs[k][...]
            d, nm, nv = _adamw_math(w_refs[k][...], gk, m_refs[k][...], v_refs[k][...])
            outs[4 * k][...] = gk
            outs[4 * k + 1][...] = d
            outs[4 * k + 2][...] = nm
            outs[4 * k + 3][...] = nv

    spec = pl.BlockSpec((rb, D), lambda i: (i, 0))
    gspecs = [pl.BlockSpec((rb, D), lambda i, k=k: (SQ_SHARD * k // rb + i, 0)) for k in range(5)]
    shp = jax.ShapeDtypeStruct((SQ_SHARD, D), F32)
    res = pl.pallas_call(
        body, name="adamw_square", grid=(nb,), in_specs=gspecs + [spec] * 15, out_specs=(spec,) * 20,
        out_shape=(shp,) * 20, compiler_params=_params(("arbitrary",)),
    )(*([gfin] * 5), *ws, *ms, *vs)
    return [tuple(res[4 * k:4 * k + 4]) for k in range(5)]


def _adamw_small(gs, ws, ms, vs):
    n = len(gs)

    def body(*refs):
        outs = refs[4 * n:]
        for k in range(n):
            d, nm, nv = _adamw_math(refs[n + k][...], refs[k][...], refs[2 * n + k][...],
                                    refs[3 * n + k][...])
            outs[3 * k][...] = d
            outs[3 * k + 1][...] = nm
            outs[3 * k + 2][...] = nv

    vm = pl.BlockSpec(memory_space=pltpu.VMEM)
    shapes = []
    for w in ws:
        shapes += [jax.ShapeDtypeStruct(w.shape, F32)] * 3
    res = pl.pallas_call(
        body, name="adamw_small", in_specs=[vm] * (4 * n), out_specs=(vm,) * (3 * n),
        out_shape=tuple(shapes),
    )(*gs, *ws, *ms, *vs)
    return [tuple(res[3 * k:3 * k + 3]) for k in range(n)]


def _rope_constants():
    half = ROPE_DIM // 2
    inv = jnp.power(ROPE_THETA, -jnp.arange(0, ROPE_DIM, 2, dtype=F32) / ROPE_DIM)
    freq = jnp.concatenate([inv, jnp.zeros((ROPE_ROWS - half,), F32)]).reshape(ROPE_ROWS, 1)
    spread = np.zeros((3, ROPE_ROWS, BLOCK), np.float32)
    for lane in range(BLOCK):
        d = lane % HEAD_DIM
        if d < ROPE_DIM:
            spread[0, d % half, lane] = 1.0
            spread[1, d % half, lane] = -1.0 if d < half else 1.0
        else:
            spread[2, 0, lane] = 1.0
    return freq, jnp.asarray(spread, BF16)


def kernel(x, p, positions, w_in, ln_pre, ln_post, w_dw, b_dw, conv_ln_g, conv_ln_b, w_pw, sinks, w_br_conv, w_br_attn, w_out, w_ple_gate, w_ple_proj, loss_target, m_w_in, m_ln_pre, m_ln_post, m_w_dw, m_b_dw, m_conv_ln_g, m_conv_ln_b, m_w_pw, m_sinks, m_w_br_conv, m_w_br_attn, m_w_out, m_w_ple_gate, m_w_ple_proj, v_w_in, v_ln_pre, v_ln_post, v_w_dw, v_b_dw, v_conv_ln_g, v_conv_ln_b, v_w_pw, v_sinks, v_w_br_conv, v_w_br_attn, v_w_out, v_w_ple_gate, v_w_ple_proj):
    nb, S, _ = x.shape
    T = nb * S
    xc = lax.axis_index("x")
    yc = lax.axis_index("y")
    cc = lax.axis_index("c")
    shard = 2 * xc + yc

    sq_w = (w_pw, w_br_conv, w_br_attn, w_out, w_ple_gate)
    wdw_shard = jnp.pad(w_dw[0], ((0, 1), (0, 0)))
    x2 = x.reshape(T, D)
    tm_res = min(TILE_RESIDENT, T // 2)
    h, cos_t, sin_t = _prenorm(x2, ln_pre, positions.astype(F32).reshape(1, T), *_rope_constants(), tm_res)

    tgt = loss_target.reshape(T, D)
    p2 = p.reshape(T, PLE)
    sinks1 = sinks.reshape(N_HEADS)

    tm = min(TILE_TOKEN, S)
    tq = min(TILE_ATTN, S)

    z, zkv, wt, wdw_all = _inproj(h, w_in[0].T.astype(BF16), wdw_shard, min(TILE_PROJ, T // 2))
    wdw = jnp.concatenate([wdw_all[s] for s in range(N_SHARDS)], axis=1)
    sq_shards = [w[0].astype(BF16) for w in sq_w] + [w_ple_proj[0].T.reshape(WPP_SHARD, D).astype(BF16)]
    o, wall_a = _attn_fwd(z, zkv, cos_t, sin_t, sinks1, S, tq, GROUP_CONV, sq_shards[0:2])
    ya, y, rstd, pw, wall_b, wppf = _conv_fwd(z, wdw, b_dw, conv_ln_g, conv_ln_b, wall_a, S, tm, GROUP_TAIL,
                                              sq_shards[2:])
    wppt = wppf.reshape(D, PLE)
    loss_p, dx1, dm, yb, g_ln_post, gsq, gw_ppt = _tail_a(x2, tgt, p2, o, ya, z, ln_post, wall_b, wppt, tm)

    cidx = jnp.reshape(cc, (1,)).astype(jnp.int32)
    scidx = jnp.stack([shard, cc]).astype(jnp.int32)

    def landing(pack, n, dtype):
        return jax.ShapeDtypeStruct((n, pack.shape[1] // 2, D), dtype)

    dz, do, dc, gvec, gsq = _tail_b(dm, ya, yb, o, z, pw, y, rstd, conv_ln_g, conv_ln_b, wall_a, wall_b,
                                    gw_ppt.reshape(PLE, D), gsq, tm)
    dz, g_wdw, r1_sq = _conv_bwd(dc, z, wdw, dz, S, tm, _exchange_copies, gsq, landing(gsq, N_SHARDS, F32))
    cs_sq = _chip_sum(cidx, gsq, r1_sq, "chip_sum_sq")
    dz, dkv, g_sinks, r2_sq = _attn_bwd(z, zkv, o, do, cos_t, sin_t, sinks1, dz, S, tq, _chip_sum_copies, cs_sq,
                                        landing(gsq, 3, BF16))
    gwt_pack = _gwt(dz, dkv, h, min(2 * TILE_PROJ, T))

    tm_dh = tm_res
    n_dh = T // tm_dh
    n_a = max(1, n_dh // 4)
    gx, g_ln_pre_a, r1_wt = _dh(
        dz, dkv, wt, x2, dx1, ln_pre, tm_dh, 0, n_a, None, "dh_exchange", _exchange_copies, gwt_pack,
        landing(gwt_pack, N_SHARDS, F32))
    cs_wt = _chip_sum(cidx, gwt_pack, r1_wt, "chip_sum_wt")
    gx, g_ln_pre_b, r2_wt = _dh(
        dz, dkv, wt, x2, dx1, ln_pre, tm_dh, n_a, n_dh - n_a, gx, "dh_send", _chip_sum_copies, cs_wt,
        landing(gwt_pack, 3, BF16))
    g_ln_pre = g_ln_pre_a + g_ln_pre_b
    row37 = jnp.concatenate([g_sinks[0:1, 0:N_HEADS], loss_p, jnp.zeros((1, D - N_HEADS - 1), F32)], axis=1)
    vec = jnp.concatenate([g_wdw, g_ln_pre, g_ln_post, gvec[2:3], gvec[0:1], gvec[1:2], row37,
                           jnp.zeros((VEC_ROWS - 38, D), F32)], axis=0)
    gfin_wt, gfin_sq, tot = _finish_reduce(_final_half(scidx, gwt_pack, r1_wt, r2_wt, "final_half_wt"),
                                           _final_half(scidx, gsq, r1_sq, r2_sq, "final_half_sq"), vec)

    g_w_in, d_w_in, nm_w_in, nv_w_in = [a.T for a in _adamw_rows(
        gfin_wt, w_in[0].T, m_w_in[0].T, v_w_in[0].T, WIN_SHARD // 8, "adamw_w_in")]
    g_w_in = g_w_in[None]
    sq_m = (m_w_pw, m_w_br_conv, m_w_br_attn, m_w_out, m_w_ple_gate)
    sq_v = (v_w_pw, v_w_br_conv, v_w_br_attn, v_w_out, v_w_ple_gate)
    sq_res = _adamw_square(gfin_sq, [w[0] for w in sq_w], [m[0] for m in sq_m], [v[0] for v in sq_v])
    g_wpp = gfin_sq[5 * SQ_SHARD:SQ_PACK].reshape(PLE, PLE).T
    g_dw_all = tot[0:CONV_K]
    g_dw = lax.dynamic_slice_in_dim(g_dw_all, shard * PLE, PLE, axis=1)
    small_g = [g_wpp, g_dw, tot[32:33], tot[33:34], tot[34:35], tot[35:36], tot[36:37],
               tot[37:38, 0:N_HEADS]]
    small_w = [w_ple_proj[0], w_dw[0], ln_pre, ln_post, b_dw, conv_ln_g, conv_ln_b, sinks]
    small_m = [m_w_ple_proj[0], m_w_dw[0], m_ln_pre, m_ln_post, m_b_dw, m_conv_ln_g, m_conv_ln_b, m_sinks]
    small_v = [v_w_ple_proj[0], v_w_dw[0], v_ln_pre, v_ln_post, v_b_dw, v_conv_ln_g, v_conv_ln_b, v_sinks]
    small = _adamw_small(small_g, small_w, small_m, small_v)

    loss = tot[37, N_HEADS]
    grads = [g_w_in, small_g[2], small_g[3], g_dw[None], small_g[4], small_g[5], small_g[6],
             sq_res[0][0][None], small_g[7], sq_res[1][0][None], sq_res[2][0][None], sq_res[3][0][None],
             sq_res[4][0][None], g_wpp[None]]

    def triple(i):
        w_in_t = (d_w_in[None], nm_w_in[None], nv_w_in[None])
        sq = lambda k: tuple(a[None] for a in sq_res[k][1:4])
        sm = lambda k, lead: tuple(a[None] if lead else a for a in small[k])
        return [w_in_t[i], sm(2, False)[i], sm(3, False)[i], sm(1, True)[i], sm(4, False)[i],
                sm(5, False)[i], sm(6, False)[i], sq(0)[i], sm(7, False)[i], sq(1)[i], sq(2)[i], sq(3)[i],
                sq(4)[i], sm(0, True)[i]]

    return (loss, gx.reshape(nb, S, D), *grads, *triple(0), *triple(1), *triple(2))
```

```python
import functools

import jax
import jax.numpy as jnp
import numpy as np
from jax import lax
from jax.experimental import pallas as pl
from jax.experimental.pallas import tpu as pltpu

F32 = jnp.float32
BF16 = jnp.bfloat16

D = 1024
PLE = 256
N_HEADS = 16
HEAD_DIM = 64
BLOCK = 128
CONV_K = 31
ROPE_DIM = 16
ROPE_THETA = 500000.0
EPS = 1e-6
IN_WIDTH = 7424
N_SHARDS = 4

ADAM_LR = 0.001
ADAM_B1 = 0.9
ADAM_B2 = 0.999
ADAM_EPS = 1e-08
ADAM_WD = 0.01
ADAM_STEP = 10

SQ_NAMES = ("w_pw", "w_br_conv", "w_br_attn", "w_out", "w_ple_gate")
WT0 = 5 * D
WPP0 = WT0 + IN_WIDTH
WALL_ROWS = WPP0 + PLE
WIN_SHARD = IN_WIDTH // N_SHARDS
SQ_SHARD = D // N_SHARDS
WPP_SHARD = PLE * PLE // D
PACK_ROWS = WIN_SHARD + 5 * SQ_SHARD + WPP_SHARD
HALF_ROWS = PACK_ROWS // 2
VMEM_LIMIT = 56 * 1024 * 1024
MESH = pl.DeviceIdType.MESH
TILE_RESIDENT = 512
TILE_PROJ = 1024
TILE_TOKEN = 256
TILE_ATTN = 512
TAIL_PARTS = 1


ZB_AGATE, ZB_GCONV, ZB_GATTN, ZB_CGATE, ZB_CVAL, ZB_CGLU, ZB_Q = range(7)
ZKV = 7 * D
_SEGMENTS = ((0, D, ZB_CVAL * D), (D, D, ZB_CGLU * D), (2 * D, D, ZB_CGATE * D), (3 * D, D, ZB_Q * D),
             (4 * D, 2 * BLOCK, ZKV), (4 * D + 2 * BLOCK, D, ZB_AGATE * D),
             (5 * D + 2 * BLOCK, D, ZB_GCONV * D), (6 * D + 2 * BLOCK, D, ZB_GATTN * D))
_WT_CUTS = (0, 192, 640, 1216, WIN_SHARD)


def _zp_row(o):
    for a, w, zp in _SEGMENTS:
        if a <= o < a + w:
            return zp + o - a
    raise ValueError(o)


def _pieces(s):
    out = []
    for a, b in zip(_WT_CUTS[:-1], _WT_CUTS[1:]):
        first = _zp_row(WIN_SHARD * s + a)
        assert _zp_row(WIN_SHARD * s + b - 1) == first + b - a - 1
        out.append((a, b - a, WT0 + first))
    for k in range(5):
        out.append((WIN_SHARD + SQ_SHARD * k, SQ_SHARD, D * k + SQ_SHARD * s))
    out.append((WIN_SHARD + 5 * SQ_SHARD, WPP_SHARD, WPP0 + WPP_SHARD * s))
    return out


N_PIECES = len(_pieces(0))


def _wall_segments(wall0, rows):
    out = []
    for s in range(N_SHARDS):
        for pr, n, wr in _pieces(s):
            lo, hi = max(wr, wall0), min(wr + n, wall0 + rows)
            if lo < hi:
                out.append((lo - wall0, hi - lo, s, pr + lo - wr))
    assert sum(n for _, n, _, _ in out) == rows
    return out


def _sel(s, vals):
    r = jnp.int32(vals[0])
    for i in range(1, len(vals)):
        r = jnp.where(s == i, jnp.int32(vals[i]), r)
    return r


def _sig(x):
    return 1.0 / (1.0 + jnp.exp(-x))


def _mm(a, b):
    return lax.dot_general(a, b, (((1,), (0,)), ((), ())), preferred_element_type=F32)


def _mm_nt(a, b):
    return lax.dot_general(a, b, (((1,), (1,)), ((), ())), preferred_element_type=F32)


def _mm_tn(a, b):
    return lax.dot_general(a, b, (((0,), (0,)), ((), ())), preferred_element_type=F32)


def _params(sem=None):
    return pltpu.CompilerParams(dimension_semantics=sem, vmem_limit_bytes=VMEM_LIMIT)


def _flush_to_pack(acc_ref, gpack_ref, wall0, sem):
    for cp in _pack_copies(acc_ref, gpack_ref, wall0, sem):
        cp.start()
        cp.wait()


def _flush_all(items, gpack_ref):
    for acc_ref, wall0, sem in items:
        for cp in _pack_copies(acc_ref, gpack_ref, wall0, sem):
            cp.start()
    for acc_ref, _, sem in items:
        pltpu.make_async_copy(acc_ref, gpack_ref.at[0, pl.ds(0, acc_ref.shape[0])], sem).wait()


def _pack_copies(acc_ref, gpack_ref, wall0, sem):
    base = 0 if gpack_ref.shape[1] == WIN_SHARD else WIN_SHARD
    out = []
    for r, n, s, pr in _wall_segments(wall0, acc_ref.shape[0]):
        assert 0 <= pr - base and pr - base + n <= gpack_ref.shape[1]
        out.append(pltpu.make_async_copy(acc_ref.at[pl.ds(r, n)], gpack_ref.at[s, pl.ds(pr - base, n)], sem))
    return out


def _coords():
    return lax.axis_index("x"), lax.axis_index("y"), lax.axis_index("c")


def _chip_peers(x, y):
    return [(1 - x, y), (x, 1 - y), (1 - x, 1 - y)]


WIN_PIECES = tuple(range(len(_WT_CUTS) - 1))
SQ_PIECES = tuple(range(len(WIN_PIECES), N_PIECES))


def _gather_ops(group, src, landing, bytes_ref, stage, send_sems, recv_sems, loc_sem):
    sizes = [_pieces(0)[p][1] for p in group]
    half_rows = sum(n // 2 for n in sizes)
    starts = [sum(sizes[:i]) for i in range(len(sizes))]

    def rcopy(a, b, k, dev):
        return pltpu.make_async_remote_copy(src_ref=a, dst_ref=b, send_sem=send_sems.at[k],
                                            recv_sem=recv_sems.at[k], device_id=dev, device_id_type=MESH)

    def total(k):
        x, y, c = _coords()
        rows = bytes_ref.at[pl.ds(0, half_rows)]
        return rcopy(rows, rows, k, (x, y, c))

    def own_total():
        rows = stage.at[pl.ds(0, sum(sizes))]
        return pltpu.make_async_copy(rows, rows, loc_sem)

    def send():
        x, y, c = _coords()
        s_me = 2 * x + y
        for k, (px, py) in enumerate(_chip_peers(x, y)):
            for p, n in zip(group, sizes):
                h = n // 2
                rcopy(src(p, c * h, h), landing(p, s_me, c * h, h), k, (px, py, c)).start()
        for p, n, r in zip(group, sizes, starts):
            pltpu.make_async_copy(src(p, 0, n), stage.at[pl.ds(r, n)], loc_sem).start()

    def forward():
        x, y, c = _coords()
        own_total().wait()
        for p, n, r in zip(group, sizes, starts):
            pltpu.make_async_copy(stage.at[pl.ds(r, n)], landing(p, 2 * x + y, 0, n), loc_sem).start()
        for k, (px, py) in enumerate(_chip_peers(x, y)):
            total(k).wait_recv()
            for p, n in zip(group, sizes):
                rows = landing(p, 2 * px + py, c * (n // 2), n // 2)
                rcopy(rows, rows, 3 + k, (x, y, 1 - c)).start()

    def finish():
        own_total().wait()
        for k in range(3):
            total(3 + k).wait_recv()
        for k in range(6):
            total(k).wait_send()

    return send, forward, finish


def _piece_rows(ref, start, off, n):
    first = start + off
    return ref.at[pl.ds(first if isinstance(first, int) else pl.multiple_of(first, 32), n)]


GROUP_CONV = SQ_PIECES[0:2]
GROUP_TAIL = SQ_PIECES[2:]


def _group_shapes(group):
    n_sq = sum(1 for q in group if q != N_PIECES - 1)
    return [jax.ShapeDtypeStruct((n_sq * D, D), BF16)] + (
        [jax.ShapeDtypeStruct((PLE, D), BF16)] if N_PIECES - 1 in group else [])


def _group_scratch(group):
    return [pltpu.VMEM((sum(_pieces(0)[q][1] for q in group), D), BF16), pltpu.SemaphoreType.DMA((6,)),
            pltpu.SemaphoreType.DMA((6,)), pltpu.SemaphoreType.DMA]


def _group_gather(group, shard_refs, out_refs, scratch, step, n_steps):
    wall_ref = out_refs[0]
    stage, send_sems, recv_sems, loc_sem = scratch

    def src(q, off, n):
        return _piece_rows(shard_refs[group.index(q)], 0, off, n)

    def landing(q, s, off, n):
        if q == N_PIECES - 1:
            return _piece_rows(out_refs[1], WPP_SHARD * s, off, n)
        return _piece_rows(wall_ref, D * group.index(q) + SQ_SHARD * s, off, n)

    send, forward, finish = _gather_ops(group, src, landing, wall_ref, stage, send_sems, recv_sems, loc_sem)
    pl.when(step == 0)(send)
    pl.when(step == n_steps // 2)(forward)
    return finish


ROPE_ROWS = 16


def _prenorm(x, ln_pre, pos, freq, spread, tm):
    T = x.shape[0]

    def to_lanes(v, e):
        out = None
        for _ in range(3):
            part = v.astype(BF16)
            term = _mm_tn(part, e)
            out = term if out is None else out + term
            v = v - part.astype(F32)
        return out

    def body(x_ref, g_ref, pos_ref, f_ref, e_ref, h_ref, cos_ref, sin_ref):
        xv = x_ref[...]
        r = lax.rsqrt(jnp.mean(xv * xv, axis=-1, keepdims=True) + EPS)
        h_ref[...] = (xv * r * g_ref[...]).astype(BF16)
        ang = f_ref[...] * pos_ref[...]
        cos_ref[...] = to_lanes(jnp.cos(ang), e_ref[0]) + e_ref[2, 0:1, :].astype(F32)
        sin_ref[...] = to_lanes(jnp.sin(ang), e_ref[1])

    return pl.pallas_call(
        body, name="prenorm", grid=(T // tm,),
        out_shape=(jax.ShapeDtypeStruct((T, D), BF16), jax.ShapeDtypeStruct((T, BLOCK), F32),
                   jax.ShapeDtypeStruct((T, BLOCK), F32)),
        in_specs=[pl.BlockSpec((tm, D), lambda i: (i, 0)), pl.BlockSpec((1, D), lambda i: (0, 0)),
                  pl.BlockSpec((1, tm), lambda i: (0, i)), pl.BlockSpec((ROPE_ROWS, 1), lambda i: (0, 0)),
                  pl.BlockSpec((3, ROPE_ROWS, BLOCK), lambda i: (0, 0, 0))],
        out_specs=(pl.BlockSpec((tm, D), lambda i: (i, 0)), pl.BlockSpec((tm, BLOCK), lambda i: (i, 0)),
                   pl.BlockSpec((tm, BLOCK), lambda i: (i, 0))),
        compiler_params=_params(("arbitrary",)),
    )(x, ln_pre, pos, freq, spread)


SQ_PACK = PACK_ROWS - WIN_SHARD


def _row_tile(half):
    return max(t for t in range(8, 321, 8) if half % t == 0)


def _exchange_copies(g_ref, r1_ref, send_sems, recv_sems):
    x, y, c = _coords()
    half = g_ref.shape[1] // 2
    return [pltpu.make_async_remote_copy(
        src_ref=g_ref.at[:, pl.ds(pl.multiple_of((1 - c) * half, 32), half), :], dst_ref=r1_ref,
        send_sem=send_sems.at[0], recv_sem=recv_sems.at[0], device_id=(x, y, 1 - c), device_id_type=MESH)]


def _chip_sum_copies(cs_ref, r2_ref, send_sems, recv_sems):
    x, y, c = _coords()
    return [pltpu.make_async_remote_copy(
        src_ref=cs_ref.at[2 * px + py], dst_ref=r2_ref.at[k], send_sem=send_sems.at[k],
        recv_sem=recv_sems.at[k], device_id=(px, py, c), device_id_type=MESH)
        for k, (px, py) in enumerate(_chip_peers(x, y))]


def _chip_sum(cidx, gpack, r1, name):
    half = gpack.shape[1] // 2
    rt = _row_tile(half)

    def body(c_ref, g_ref, r_ref, o_ref):
        o_ref[...] = (g_ref[...] + r_ref[...]).astype(BF16)

    nt = half // rt
    return pl.pallas_call(
        body, name=name,
        grid_spec=pltpu.PrefetchScalarGridSpec(
            num_scalar_prefetch=1, grid=(N_SHARDS, nt),
            in_specs=[pl.BlockSpec((1, rt, D), lambda s, t, c: (s, c[0] * nt + t, 0)),
                      pl.BlockSpec((1, rt, D), lambda s, t, c: (s, t, 0))],
            out_specs=pl.BlockSpec((1, rt, D), lambda s, t, c: (s, t, 0))),
        out_shape=jax.ShapeDtypeStruct((N_SHARDS, half, D), BF16),
        compiler_params=_params(("arbitrary", "arbitrary")),
    )(cidx, gpack, r1)


def _final_half(sc, gpack, r1, r2, name):
    rows = gpack.shape[1]
    half = rows // 2
    rt = _row_tile(half)

    def body(sc_ref, g_ref, r_ref, p_ref, o_ref):
        acc = g_ref[0] + r_ref[0]
        for k in range(3):
            acc = acc + p_ref[k].astype(F32)
        o_ref[...] = acc

    nt = half // rt
    return pl.pallas_call(
        body, name=name,
        grid_spec=pltpu.PrefetchScalarGridSpec(
            num_scalar_prefetch=1, grid=(nt,),
            in_specs=[pl.BlockSpec((1, rt, D), lambda t, sc: (sc[0], sc[1] * nt + t, 0)),
                      pl.BlockSpec((1, rt, D), lambda t, sc: (sc[0], t, 0)),
                      pl.BlockSpec((3, rt, D), lambda t, sc: (0, t, 0))],
            out_specs=pl.BlockSpec((rt, D), lambda t, sc: (sc[1] * nt + t, 0))),
        out_shape=jax.ShapeDtypeStruct((rows, D), F32),
        compiler_params=_params(("arbitrary",)),
    )(sc, gpack, r1, r2)


VEC_ROWS = 40


def _finish_reduce(fwt, fsq, vec):
    def body(fwt_ref, fsq_ref, v_ref, owt_ref, osq_ref, tot_ref, buf, send_sems, recv_sems):
        x, y, c = _coords()
        swaps = []
        for k, (f_ref, o_ref) in enumerate(((fwt_ref, owt_ref), (fsq_ref, osq_ref))):
            half = f_ref.shape[0] // 2
            rows = pl.ds(pl.multiple_of(c * half, 32), half)
            swaps.append(pltpu.make_async_remote_copy(
                src_ref=f_ref.at[rows], dst_ref=o_ref.at[rows], send_sem=send_sems.at[7 + k],
                recv_sem=recv_sems.at[7 + k], device_id=(x, y, 1 - c), device_id_type=MESH))
        for cp in swaps:
            cp.start()
        me = 4 * x + 2 * y + c
        buf[me] = v_ref[...]
        cps = []
        for r in range(1, 8):
            dx, dy, dc = (r >> 2) & 1, (r >> 1) & 1, r & 1
            peer = (1 - x if dx else x, 1 - y if dy else y, 1 - c if dc else c)
            cp = pltpu.make_async_remote_copy(
                src_ref=v_ref, dst_ref=buf.at[me], send_sem=send_sems.at[r - 1],
                recv_sem=recv_sems.at[r - 1], device_id=peer, device_id_type=MESH)
            cp.start()
            cps.append(cp)
        for cp in cps:
            cp.wait_recv()
        for cp in cps:
            cp.wait_send()
        acc = buf[0]
        for d in range(1, 8):
            acc = acc + buf[d]
        tot_ref[...] = acc
        for cp in swaps:
            cp.wait()

    any_spec = pl.BlockSpec(memory_space=pl.ANY)
    vm = pl.BlockSpec(memory_space=pltpu.VMEM)
    return pl.pallas_call(
        body, name="finish_reduce",
        out_shape=(jax.ShapeDtypeStruct(fwt.shape, F32), jax.ShapeDtypeStruct(fsq.shape, F32),
                   jax.ShapeDtypeStruct((VEC_ROWS, D), F32)),
        in_specs=[any_spec, any_spec, vm], out_specs=(any_spec, any_spec, vm),
        input_output_aliases={0: 0, 1: 1},
        scratch_shapes=[pltpu.VMEM((8, VEC_ROWS, D), F32), pltpu.SemaphoreType.DMA((9,)),
                        pltpu.SemaphoreType.DMA((9,))],
    )(fwt, fsq, vec)


SOLO_ROWS = WIN_SHARD - BLOCK // 2


def _solo_first(s):
    return 0 if s % 2 == 0 else BLOCK // 2


def _solo_segments(s):
    lo = _solo_first(s)
    out = []
    for a, n, wr in _pieces(s)[:len(WIN_PIECES)]:
        b0, b1 = max(a, lo), min(a + n, lo + SOLO_ROWS)
        if b0 >= b1:
            continue
        z0 = wr - WT0 + b0 - a
        if out and out[-1][0] + out[-1][1] == b0 - lo and out[-1][2] + out[-1][1] == z0:
            out[-1] = (out[-1][0], out[-1][1] + b1 - b0, out[-1][2])
        else:
            out.append((b0 - lo, b1 - b0, z0))
    out = [r for o, n, z0 in out for r in
           (((o, ZKV - z0, z0), (o + ZKV - z0, z0 + n - ZKV, ZKV)) if z0 < ZKV < z0 + n else ((o, n, z0),))]
    assert all(v % BLOCK == 0 for seg in out for v in seg) and sum(n for _, n, _ in out) == SOLO_ROWS
    return out


def _shared_tile(pair):
    z0 = _zp_row(WIN_SHARD * (2 * pair) + SOLO_ROWS)
    assert z0 % BLOCK == 0 and _zp_row(WIN_SHARD * (2 * pair + 1)) == z0 + BLOCK // 2
    return z0


def _inproj(h, win_t, wdw_shard, tm):
    T = h.shape[0]
    n_t = T // tm
    assert n_t >= 2
    tables = [[_pieces(s)[p][2] - WT0 for s in range(N_SHARDS)] for p in WIN_PIECES]
    sizes = [_pieces(0)[p][1] for p in WIN_PIECES]
    half_rows = sum(n // 2 for n in sizes)
    relation_of_pass = {1: 1, 2: 0, 3: 2}

    def body(h_ref, win_ref, wdw_ref, z_ref, zkv_ref, wt_ref, wdwall_ref, wbuf, stage, stage_sh, wsend, wrecv,
             loc_sems, out_sems, sh_sems):
        p = pl.program_id(0)
        t = pl.program_id(1)
        x, y, c = _coords()
        s_me = 2 * x + y
        peers = _chip_peers(x, y)
        shard = jnp.bitwise_xor(s_me, p)
        first, last = t == 0, t == n_t - 1

        def rcopy(a, b, k, dev):
            return pltpu.make_async_remote_copy(src_ref=a, dst_ref=b, send_sem=wsend.at[k], recv_sem=wrecv.at[k],
                                                device_id=dev, device_id_type=MESH)

        def total(k):
            rows = wt_ref.at[pl.ds(0, half_rows)]
            return rcopy(rows, rows, k, (x, y, c))

        def in_hbm(q, s, off, n):
            return _piece_rows(wt_ref, _sel(s, tables[q]), off, n)

        def in_vmem(q, s):
            return _piece_rows(wbuf, WIN_SHARD * s + _WT_CUTS[q], 0, sizes[q])

        def send_to(k):
            px, py = peers[k]
            for q, n in zip(WIN_PIECES, sizes):
                rcopy(_piece_rows(win_ref, _WT_CUTS[q], c * (n // 2), n // 2), in_hbm(q, s_me, c * (n // 2), n // 2),
                      k, (px, py, c)).start()

        def forward_from(k):
            px, py = peers[k]
            total(k).wait_recv()
            for q, n in zip(WIN_PIECES, sizes):
                rows = in_hbm(q, 2 * px + py, c * (n // 2), n // 2)
                rcopy(rows, rows, 3 + k, (x, y, 1 - c)).start()
            total(3 + k).wait_recv()

        def shard_total(a, b, sem):
            return pltpu.make_async_copy(a.at[pl.ds(0, WIN_SHARD)], b.at[pl.ds(0, WIN_SHARD)], sem)

        def wdw_copies():
            return [pltpu.make_async_remote_copy(
                src_ref=wdw_ref, dst_ref=wdwall_ref.at[s_me], send_sem=wsend.at[6 + k], recv_sem=wrecv.at[6 + k],
                device_id=(px, py, c), device_id_type=MESH) for k, (px, py) in enumerate(peers)]

        def own_wdw():
            return pltpu.make_async_copy(wdw_ref, wdwall_ref.at[s_me], loc_sems.at[2])

        @pl.when((p == 0) & first)
        def _():
            send_to(0)
            send_to(1)
            own_wdw().start()
            for cp in wdw_copies():
                cp.start()
            for q in WIN_PIECES:
                pltpu.make_async_copy(_piece_rows(win_ref, _WT_CUTS[q], 0, sizes[q]), in_vmem(q, s_me),
                                      loc_sems.at[0]).start()
            shard_total(win_ref, wbuf, loc_sems.at[0]).wait()
            for q in WIN_PIECES:
                pltpu.make_async_copy(in_vmem(q, s_me), in_hbm(q, s_me, 0, sizes[q]), loc_sems.at[1]).start()

        for pp, k in relation_of_pass.items():
            @pl.when((p == pp - 1) & last)
            def _(k=k):
                forward_from(k)
                px, py = peers[k]
                for q in WIN_PIECES:
                    pltpu.make_async_copy(in_hbm(q, 2 * px + py, 0, sizes[q]), in_vmem(q, 2 * px + py),
                                          loc_sems.at[0]).start()

            @pl.when((p == pp) & first)
            def _(pp=pp):
                shard_total(wt_ref, wbuf, loc_sems.at[0]).wait()
                if pp == 1:
                    total(0).wait_send()
                    total(1).wait_send()
                    send_to(2)

        step = p * n_t + t
        slot = step % 2
        rows = pl.ds(pl.multiple_of(t * tm, tm), tm)

        def out_total(sl):
            return pltpu.make_async_copy(stage.at[sl], stage.at[sl], out_sems.at[sl])

        def sh_copy(sl, z0):
            return pltpu.make_async_copy(stage_sh.at[sl], z_ref.at[rows, pl.ds(z0, BLOCK)], sh_sems.at[sl])

        @pl.when(step >= 2)
        def _():
            out_total(slot).wait()

        @pl.when((step >= 2) & (((step - 2) // n_t) % 2 == 1))
        def _():
            sh_copy(slot, 0).wait()

        solo0 = pl.multiple_of(WIN_SHARD * shard + (BLOCK // 2) * (shard % 2), BLOCK // 2)
        stage[slot] = _mm_nt(h_ref[...], wbuf[pl.ds(solo0, SOLO_ROWS), :]).astype(BF16)
        for s in range(N_SHARDS):
            @pl.when(shard == s)
            def _(s=s):
                for off, n, z0 in _solo_segments(s):
                    dst = zkv_ref.at[rows] if z0 == ZKV else z_ref.at[rows, pl.ds(z0, n)]
                    pltpu.make_async_copy(stage.at[slot, :, pl.ds(off, n)], dst, out_sems.at[slot]).start()

        @pl.when(p % 2 == 1)
        def _():
            pair = shard // 2
            w0 = pl.multiple_of(2 * WIN_SHARD * pair + SOLO_ROWS, BLOCK // 2)
            z0 = pl.multiple_of(jnp.where(pair == 0, _shared_tile(0), _shared_tile(1)), BLOCK)
            stage_sh[slot] = _mm_nt(h_ref[...], wbuf[pl.ds(w0, BLOCK), :]).astype(BF16)
            sh_copy(slot, z0).start()

        @pl.when((p == 3) & last)
        def _():
            for k in (2, 3, 4, 5):
                total(k).wait_send()
            shard_total(wbuf, wt_ref, loc_sems.at[1]).wait()
            cps = wdw_copies()
            for cp in cps:
                cp.wait_recv()
            for cp in cps:
                cp.wait_send()
            own_wdw().wait()
            for sl in range(2):
                out_total(sl).wait()
                sh_copy(sl, 0).wait()

    any_spec = pl.BlockSpec(memory_space=pl.ANY)
    return pl.pallas_call(
        body, name="inproj", grid=(N_SHARDS, n_t),
        in_specs=[pl.BlockSpec((tm, D), lambda p, t: (t, 0)), any_spec, any_spec],
        out_specs=(any_spec,) * 4,
        out_shape=(jax.ShapeDtypeStruct((T, ZKV), BF16), jax.ShapeDtypeStruct((T, 2 * BLOCK), BF16),
                   jax.ShapeDtypeStruct((IN_WIDTH, D), BF16), jax.ShapeDtypeStruct((N_SHARDS, 32, PLE), F32)),
        scratch_shapes=[pltpu.VMEM((IN_WIDTH, D), BF16), pltpu.VMEM((2, tm, SOLO_ROWS), BF16),
                        pltpu.VMEM((2, tm, BLOCK), BF16),
                        pltpu.SemaphoreType.DMA((9,)), pltpu.SemaphoreType.DMA((9,)),
                        pltpu.SemaphoreType.DMA((3,)), pltpu.SemaphoreType.DMA((2,)),
                        pltpu.SemaphoreType.DMA((2,))],
        compiler_params=_params(("arbitrary", "arbitrary")),
    )(h, win_t, wdw_shard)


HALO = 32
CONV_RC = 64
CONV_LC = 256


def _conv_taps(w_ref, src, r0, lane0, offset_of_tap):
    lanes = pl.ds(lane0, CONV_LC)
    out = None
    for b in range(8):
        taps = [k for k in range(CONV_K) if offset_of_tap(k) % 8 == b]
        if not taps:
            continue
        rows = CONV_RC + (8 if b else 0)
        vb = None
        for k in taps:
            term = w_ref[k:k + 1, lanes] * src[pl.ds(r0 + (offset_of_tap(k) - b), rows), lanes]
            vb = term if vb is None else vb + term
        vb = vb[b:b + CONV_RC] if b else vb
        out = vb if out is None else out + vb
    return out


def _conv_fwd(z, wdw, b_dw, ln_g, ln_b, wall, S, tm, group, shards):
    T = z.shape[0]
    nt = S // tm
    hb = tm // HALO
    gathered = _group_shapes(group)

    def body(cv_ref, cg_ref, cgate_ref, hcv_ref, hcg_ref, wdw_ref, bdw_ref, lng_ref, lnb_ref, wpw_ref,
             wbrc_ref, *rest):
        shard_refs, rest = rest[:len(group)], rest[len(group):]
        ya_ref, y_ref, rstd_ref, pw_ref = rest[:4]
        gather_refs, (ubuf, cbuf), gather_scratch = rest[4:4 + len(gathered)], rest[-6:-4], rest[-4:]
        t = pl.program_id(1)
        step = pl.program_id(0) * nt + t
        finish_gather = _group_gather(group, shard_refs, gather_refs, gather_scratch, step, T // tm)
        ubuf[HALO:HALO + tm, :] = cv_ref[...].astype(F32) * _sig(cg_ref[...].astype(F32))
        hu = hcv_ref[...].astype(F32) * _sig(hcg_ref[...].astype(F32))
        ubuf[0:HALO, :] = jnp.where(t > 0, hu, 0.0)
        ubuf[HALO + tm:HALO + tm + 8, :] = jnp.zeros((8, D), F32)

        def chunk(ci, carry):
            r0 = pl.multiple_of(ci * CONV_RC, CONV_RC)
            for lg in range(D // CONV_LC):
                acc = _conv_taps(wdw_ref, ubuf, r0, lg * CONV_LC, lambda k: HALO - (CONV_K - 1) + k)
                cbuf[pl.ds(r0, CONV_RC), pl.ds(lg * CONV_LC, CONV_LC)] = acc
            return carry

        lax.fori_loop(0, tm // CONV_RC, chunk, 0)
        cc = cbuf[...] + bdw_ref[...]
        mu = jnp.mean(cc, axis=-1, keepdims=True)
        dd = cc - mu
        rstd = lax.rsqrt(jnp.mean(dd * dd, axis=-1, keepdims=True) + EPS)
        yn = dd * rstd
        y_ref[...] = yn.astype(BF16)
        rstd_ref[...] = rstd
        n = yn * lng_ref[...] + lnb_ref[...]
        s = n * _sig(n)
        pw = _mm(s.astype(BF16), wpw_ref[...])
        pw_ref[...] = pw.astype(BF16)
        gt = cgate_ref[...].astype(F32)
        ya_in = pw * (gt * _sig(gt))
        ya_ref[...] = _mm(ya_in.astype(BF16), wbrc_ref[...]).astype(BF16)
        pl.when(step == T // tm - 1)(finish_gather)

    def row(b, t):
        return b * nt + t

    def halo(b, t):
        return jnp.maximum(row(b, t) * hb - 1, 0)

    vec = pl.BlockSpec((1, D), lambda b, t: (0, 0))
    tile = lambda j: pl.BlockSpec((tm, D), lambda b, t: (row(b, t), j))
    out_tile = pl.BlockSpec((tm, D), lambda b, t: (row(b, t), 0))
    any_spec = pl.BlockSpec(memory_space=pl.ANY)
    return pl.pallas_call(
        body, name="conv_fwd", grid=(T // S, nt),
        in_specs=[tile(ZB_CVAL), tile(ZB_CGLU), tile(ZB_CGATE),
                  pl.BlockSpec((HALO, D), lambda b, t: (halo(b, t), ZB_CVAL)),
                  pl.BlockSpec((HALO, D), lambda b, t: (halo(b, t), ZB_CGLU)),
                  pl.BlockSpec((32, D), lambda b, t: (0, 0)), vec, vec, vec,
                  pl.BlockSpec((D, D), lambda b, t: (0, 0)),
                  pl.BlockSpec((D, D), lambda b, t: (1, 0))] + [any_spec] * len(group),
        out_specs=(out_tile, out_tile, pl.BlockSpec((tm, 1), lambda b, t: (row(b, t), 0)), out_tile)
        + (any_spec,) * len(gathered),
        out_shape=[jax.ShapeDtypeStruct((T, D), BF16), jax.ShapeDtypeStruct((T, D), BF16),
                   jax.ShapeDtypeStruct((T, 1), F32), jax.ShapeDtypeStruct((T, D), BF16)] + gathered,
        scratch_shapes=[pltpu.VMEM((tm + HALO + 8, D), F32), pltpu.VMEM((tm, D), F32)] + _group_scratch(group),
        compiler_params=_params(("arbitrary", "arbitrary")),
    )(z, z, z, z, z, wdw, b_dw, ln_g, ln_b, wall, wall, *shards)


def _swap_matrix():
    r = lax.broadcasted_iota(jnp.int32, (BLOCK, BLOCK), 0)
    l = lax.broadcasted_iota(jnp.int32, (BLOCK, BLOCK), 1)
    lh = l & (HEAD_DIM - 1)
    half = ROPE_DIM // 2
    hit = ((lh < half) & (r == l + half)) | ((lh >= half) & (lh < ROPE_DIM) & (r == l - half))
    return jnp.where(hit, 1.0, 0.0).astype(BF16)


def _rope(tb, cos, sin, pswap):
    return tb.astype(F32) * cos + _mm(tb, pswap) * sin


def _rope_f32(tv, cos, sin, pswap):
    hi = tv.astype(BF16)
    lo = (tv - hi.astype(F32)).astype(BF16)
    return tv * cos + (_mm(hi, pswap) + _mm(lo, pswap)) * sin


def _kv_variants(kv):
    lane = lax.broadcasted_iota(jnp.int32, kv.shape, 1)
    lo = lane < HEAD_DIM
    sw = pltpu.roll(kv, HEAD_DIM, 1)
    z = jnp.zeros_like(kv)
    g0 = (jnp.where(lo, kv, z).astype(BF16), jnp.where(lo, z, sw).astype(BF16))
    g1 = (jnp.where(lo, sw, z).astype(BF16), jnp.where(lo, z, kv).astype(BF16))
    return (g0, g1)


def _band_mask(nq):
    qi = lax.broadcasted_iota(jnp.int32, (nq * BLOCK, 2 * BLOCK), 0) & (BLOCK - 1)
    sj = lax.broadcasted_iota(jnp.int32, (nq * BLOCK, 2 * BLOCK), 1)
    return (sj <= qi + BLOCK) & (sj > qi), sj


def _sink_rep(sink_ref, g, e):
    return jnp.concatenate(
        [jnp.full((BLOCK, BLOCK), sink_ref[8 * g + 2 * j + e], F32) for j in range(4)], axis=0)


def _softmax_parts(s, valid, sk):
    rows = s.shape[0]
    s = jnp.where(valid, s, -1e30)
    m = jnp.maximum(jnp.broadcast_to(jnp.max(s, axis=-1, keepdims=True), (rows, BLOCK)), sk)
    return jnp.exp(s - jnp.concatenate([m, m], axis=1)), jnp.exp(sk - m)


def _softmax_sink(s, valid, sk):
    p, ps = _softmax_parts(s, valid, sk)
    inv = 1.0 / (_mm(p.astype(BF16), jnp.ones((2 * BLOCK, BLOCK), BF16)) + ps)
    return p * jnp.concatenate([inv, inv], axis=1), ps * inv


def _attn_fwd(z, zkv, cos_t, sin_t, sinks, S, tq, group, shards):
    T = z.shape[0]
    nt = S // tq
    nq = tq // BLOCK
    gathered = _group_shapes(group)

    def body(sink_ref, q_ref, kv_ref, hkv_ref, cos_ref, sin_ref, hcos_ref, hsin_ref, *rest):
        shard_refs, o_ref = rest[:len(group)], rest[len(group)]
        t = pl.program_id(1)
        step = pl.program_id(0) * nt + t
        finish_gather = _group_gather(group, shard_refs, rest[len(group) + 1:-4], rest[-4:], step, T // tq)
        cos = cos_ref[...]
        sin = sin_ref[...]
        pswap = _swap_matrix()
        kv = jnp.concatenate([hkv_ref[...], kv_ref[...]], axis=0)
        cos_k = jnp.concatenate([hcos_ref[...], cos], axis=0)
        sin_k = jnp.concatenate([hsin_ref[...], sin], axis=0)
        kx = _kv_variants(_rope(kv[:, :BLOCK], cos_k, sin_k, pswap))
        one = jnp.ones((tq + BLOCK, BLOCK), BF16)
        vx = [[jnp.concatenate([v, one], axis=1) for v in vg] for vg in _kv_variants(kv[:, BLOCK:].astype(F32))]
        band, sj = _band_mask(4)
        qs = [(_rope(q_ref[:, 128 * hp:128 * hp + 128], cos, sin, pswap) * 0.125).astype(BF16)
              for hp in range(8)]
        for n in range(nq):
            first = (t == 0) & (n == 0)
            valid = band & (jnp.logical_not(first) | (sj >= BLOCK))
            r0 = n * BLOCK
            for g in range(2):
                lhs = jnp.concatenate([qs[4 * g + j][r0:r0 + BLOCK] for j in range(4)], axis=0)
                acc = jnp.zeros((4 * BLOCK, BLOCK), F32)
                for e in range(2):
                    s = _mm_nt(lhs, kx[g][e][r0:r0 + 2 * BLOCK])
                    p, ps = _softmax_parts(s, valid, _sink_rep(sink_ref, g, e))
                    r = _mm(p.astype(BF16), vx[g][e][r0:r0 + 2 * BLOCK])
                    acc = acc + r[:, 0:BLOCK] * (1.0 / (r[:, BLOCK:2 * BLOCK] + ps))
                for j in range(4):
                    o_ref[r0:r0 + BLOCK, 128 * (4 * g + j):128 * (4 * g + j) + 128] = (
                        acc[j * BLOCK:(j + 1) * BLOCK].astype(BF16))
        pl.when(step == T // tq - 1)(finish_gather)

    def row(b, t):
        return b * nt + t

    def halo(b, t):
        return jnp.maximum(row(b, t) * nq - 1, 0)

    any_spec = pl.BlockSpec(memory_space=pl.ANY)
    return pl.pallas_call(
        body, name="attn_fwd", grid=(T // S, nt),
        in_specs=[pl.BlockSpec(memory_space=pltpu.SMEM),
                  pl.BlockSpec((tq, D), lambda b, t: (row(b, t), ZB_Q)),
                  pl.BlockSpec((tq, 2 * BLOCK), lambda b, t: (row(b, t), 0)),
                  pl.BlockSpec((BLOCK, 2 * BLOCK), lambda b, t: (halo(b, t), 0)),
                  pl.BlockSpec((tq, BLOCK), lambda b, t: (row(b, t), 0)),
                  pl.BlockSpec((tq, BLOCK), lambda b, t: (row(b, t), 0)),
                  pl.BlockSpec((BLOCK, BLOCK), lambda b, t: (halo(b, t), 0)),
                  pl.BlockSpec((BLOCK, BLOCK), lambda b, t: (halo(b, t), 0))] + [any_spec] * len(group),
        out_specs=(pl.BlockSpec((tq, D), lambda b, t: (row(b, t), 0)),) + (any_spec,) * len(gathered),
        out_shape=[jax.ShapeDtypeStruct((T, D), BF16)] + gathered,
        scratch_shapes=_group_scratch(group),
        compiler_params=_params(("arbitrary", "arbitrary")),
    )(sinks, z, zkv, zkv, cos_t, sin_t, cos_t, sin_t, *shards)


def _tail_a(x, tgt, p, o, ya, z, ln_post, wall_b, wppt, tm):
    T = x.shape[0]
    last = T // tm - 1

    def body(x_ref, tgt_ref, p_ref, o_ref, ya_ref, ag_ref, gc_ref, ga_ref, lnp_ref, wbra_ref, wout_ref,
             wpg_ref, wppt_ref, loss_ref, dx1_ref, dm_ref, yb_ref, glnp_ref, gpack_ref, gwpp_ref,
             acc_out, acc_pg, sem):
        i = pl.program_id(0)

        @pl.when(i == 0)
        def _():
            acc_out[...] = jnp.zeros_like(acc_out)
            acc_pg[...] = jnp.zeros_like(acc_pg)
            gwpp_ref[...] = jnp.zeros_like(gwpp_ref)
            glnp_ref[...] = jnp.zeros_like(glnp_ref)
            loss_ref[...] = jnp.zeros_like(loss_ref)

        ag = ag_ref[...].astype(F32)
        yb_in = (o_ref[...].astype(F32) * (ag * _sig(ag))).astype(BF16)
        yb = _mm(yb_in, wbra_ref[...])
        yb_ref[...] = yb.astype(BF16)
        m = (_sig(gc_ref[...].astype(F32)) * ya_ref[...].astype(F32)
             + _sig(ga_ref[...].astype(F32)) * yb).astype(BF16)
        mo = _mm(m, wout_ref[...])
        r2 = lax.rsqrt(jnp.mean(mo * mo, axis=-1, keepdims=True) + EPS)
        nrm = mo * r2
        g_post = lnp_ref[...]
        x1 = x_ref[...] + nrm * g_post
        x1b = x1.astype(BF16)
        gate = _sig(_mm(x1b, wpg_ref[...]))
        pb = p_ref[...].astype(BF16)
        pp = _mm_nt(pb, wppt_ref[...])
        err = x1 + gate * pp - tgt_ref[...]
        loss_ref[...] += 0.5 * jnp.sum(jnp.sum(err * err, axis=-1, keepdims=True) * (1.0 / D),
                                       axis=0, keepdims=True)
        dx2 = err * (1.0 / D)
        dgp = (dx2 * pp * gate * (1.0 - gate)).astype(BF16)
        dpp = (dx2 * gate).astype(BF16)
        dx1 = dx2 + _mm_nt(dgp, wpg_ref[...])
        dx1_ref[...] = dx1
        acc_pg[...] += _mm_tn(x1b, dgp)
        gwpp_ref[...] += _mm_tn(dpp, pb)
        glnp_ref[...] += jnp.sum(dx1 * nrm, axis=0, keepdims=True)
        a = dx1 * g_post
        dmo = (r2 * (a - nrm * jnp.mean(a * nrm, axis=-1, keepdims=True))).astype(BF16)
        dm_ref[...] = _mm_nt(dmo, wout_ref[...]).astype(BF16)
        acc_out[...] += _mm_tn(m, dmo)

        @pl.when(i == last)
        def _():
            _flush_all([(acc_out, 3 * D, sem.at[0]), (acc_pg, 4 * D, sem.at[1])], gpack_ref)

    tile = pl.BlockSpec((tm, D), lambda i: (i, 0))
    ztile = lambda j: pl.BlockSpec((tm, D), lambda i: (i, j))
    wsq = lambda k: pl.BlockSpec((D, D), lambda i: (k, 0))
    const = lambda shp: pl.BlockSpec(shp, lambda i: (0, 0))
    any_spec = pl.BlockSpec(memory_space=pl.ANY)
    return pl.pallas_call(
        body, name="tail_a", grid=(T // tm,),
        in_specs=[tile, tile, pl.BlockSpec((tm, PLE), lambda i: (i, 0)), tile, tile, ztile(ZB_AGATE),
                  ztile(ZB_GCONV), ztile(ZB_GATTN), const((1, D)), wsq(0), wsq(1), wsq(2), const((D, PLE))],
        out_specs=(const((1, 1)), tile, tile, tile, const((1, D)), any_spec, const((D, PLE))),
        out_shape=(jax.ShapeDtypeStruct((1, 1), F32), jax.ShapeDtypeStruct((T, D), F32),
                   jax.ShapeDtypeStruct((T, D), BF16), jax.ShapeDtypeStruct((T, D), BF16),
                   jax.ShapeDtypeStruct((1, D), F32), jax.ShapeDtypeStruct((N_SHARDS, SQ_PACK, D), F32),
                   jax.ShapeDtypeStruct((D, PLE), F32)),
        scratch_shapes=[pltpu.VMEM((D, D), F32), pltpu.VMEM((D, D), F32), pltpu.SemaphoreType.DMA((2,))],
        compiler_params=_params(("arbitrary",)),
    )(x, tgt, p, o, ya, z, z, z, ln_post, wall_b, wall_b, wall_b, wppt)


def _dsilu(v, sg):
    return sg * (1.0 + v * (1.0 - sg))


def _tail_b(dm, ya, yb, o, z, pw, y, rstd, ln_g, ln_b, wall_a, wall_b, gppt, gpack, tm):
    T = dm.shape[0]
    last = T // tm - 1

    def body(dm_ref, ya_ref, yb_ref, o_ref, ag_ref, gc_ref, ga_ref, cgate_ref, pw_ref, y_ref, rstd_ref,
             lng_ref, lnb_ref, wpw_ref, wbrc_ref, wbra_ref, gppt_ref, gpack_in, dg_ref, do_ref, dc_ref,
             gvec_ref, gpack_ref, acc_bra, acc_brc, acc_pw, sem):
        i = pl.program_id(0)

        @pl.when(i == 0)
        def _():
            acc_bra[...] = jnp.zeros_like(acc_bra)
            acc_brc[...] = jnp.zeros_like(acc_brc)
            acc_pw[...] = jnp.zeros_like(acc_pw)
            gvec_ref[...] = jnp.zeros_like(gvec_ref)

        g = lng_ref[...]

        def part(rs):
            dm_v = dm_ref[rs, :].astype(F32)
            sgc = _sig(gc_ref[rs, :].astype(F32))
            sga = _sig(ga_ref[rs, :].astype(F32))
            dya = (dm_v * sgc).astype(BF16)
            dyb = (dm_v * sga).astype(BF16)
            dg_ref[rs, D:2 * D] = (dm_v * ya_ref[rs, :].astype(F32) * sgc * (1.0 - sgc)).astype(BF16)
            dg_ref[rs, 2 * D:3 * D] = (dm_v * yb_ref[rs, :].astype(F32) * sga * (1.0 - sga)).astype(BF16)
            ag = ag_ref[rs, :].astype(F32)
            sag = _sig(ag)
            sa = ag * sag
            ov = o_ref[rs, :].astype(F32)
            dyb_in = _mm_nt(dyb, wbra_ref[...])
            do_ref[rs, :] = (dyb_in * sa).astype(BF16)
            dg_ref[rs, 0:D] = (dyb_in * ov * _dsilu(ag, sag)).astype(BF16)
            gt = cgate_ref[rs, :].astype(F32)
            sgt = _sig(gt)
            sgate = gt * sgt
            pw = pw_ref[rs, :].astype(F32)
            dya_in = _mm_nt(dya, wbrc_ref[...])
            dpw = (dya_in * sgate).astype(BF16)
            dg_ref[rs, 3 * D:4 * D] = (dya_in * pw * _dsilu(gt, sgt)).astype(BF16)
            yn = y_ref[rs, :].astype(F32)
            n = yn * g + lnb_ref[...]
            sn = _sig(n)
            dn = _mm_nt(dpw, wpw_ref[...]) * _dsilu(n, sn)
            dy = dn * g
            dc = rstd_ref[rs, :] * (dy - jnp.mean(dy, axis=-1, keepdims=True)
                                    - yn * jnp.mean(dy * yn, axis=-1, keepdims=True))
            dc_ref[rs, :] = dc.astype(BF16)
            sums = (jnp.sum(dn * yn, axis=0, keepdims=True), jnp.sum(dn, axis=0, keepdims=True),
                    jnp.sum(dc, axis=0, keepdims=True))
            return ((ov * sa).astype(BF16), dyb, (pw * sgate).astype(BF16), dya, (n * sn).astype(BF16), dpw,
                    sums)

        parts = [part(pl.ds(r * (tm // TAIL_PARTS), tm // TAIL_PARTS)) for r in range(TAIL_PARTS)]
        cat = lambda j: jnp.concatenate([pt[j] for pt in parts], axis=0)
        acc_bra[...] += _mm_tn(cat(0), cat(1))
        acc_brc[...] += _mm_tn(cat(2), cat(3))
        acc_pw[...] += _mm_tn(cat(4), cat(5))
        for j in range(3):
            gvec_ref[j:j + 1, :] += sum(pt[6][j] for pt in parts)

        @pl.when(i == last)
        def _():
            _flush_all([(acc_pw, 0, sem.at[0]), (acc_brc, D, sem.at[1]), (acc_bra, 2 * D, sem.at[2]),
                        (gppt_ref, WPP0, sem.at[3])], gpack_ref)

    tile = pl.BlockSpec((tm, D), lambda i: (i, 0))
    ztile = lambda j: pl.BlockSpec((tm, D), lambda i: (i, j))
    wsq = lambda k: pl.BlockSpec((D, D), lambda i: (k, 0))
    const = lambda shp: pl.BlockSpec(shp, lambda i: (0, 0))
    any_spec = pl.BlockSpec(memory_space=pl.ANY)
    return pl.pallas_call(
        body, name="tail_b", grid=(T // tm,),
        in_specs=[tile, tile, tile, tile, ztile(ZB_AGATE), ztile(ZB_GCONV), ztile(ZB_GATTN), ztile(ZB_CGATE),
                  tile, tile, pl.BlockSpec((tm, 1), lambda i: (i, 0)), const((1, D)), const((1, D)), wsq(0),
                  wsq(1), wsq(0), const((PLE, D)), any_spec],
        out_specs=(pl.BlockSpec((tm, 4 * D), lambda i: (i, 0)), tile, tile, const((8, D)), any_spec),
        out_shape=(jax.ShapeDtypeStruct((T, 7 * D), BF16), jax.ShapeDtypeStruct((T, D), BF16),
                   jax.ShapeDtypeStruct((T, D), BF16), jax.ShapeDtypeStruct((8, D), F32),
                   jax.ShapeDtypeStruct(gpack.shape, F32)),
        input_output_aliases={17: 4},
        scratch_shapes=[pltpu.VMEM((D, D), F32), pltpu.VMEM((D, D), F32), pltpu.VMEM((D, D), F32),
                        pltpu.SemaphoreType.DMA((4,))],
        compiler_params=_params(("arbitrary",)),
    )(dm, ya, yb, o, z, z, z, z, pw, y, rstd, ln_g, ln_b, wall_a, wall_a, wall_b, gppt, gpack)


def _conv_bwd(dc, z, wdw, dz, S, tm, copies, src, landing):
    T = dc.shape[0]
    nt = S // tm
    hb = tm // HALO
    nrows = T // HALO

    def body(dc_ref, hdc_ref, cv_ref, cg_ref, hcv_ref, hcg_ref, wdw_ref, dz_in, src_ref, dz_ref, gw_ref,
             land_ref, ubuf, dcbuf, dubuf, dwacc, shbuf, send_sems, recv_sems):
        b = pl.program_id(0)
        t = pl.program_id(1)

        @pl.when((b == 0) & (t == 0))
        def _():
            dwacc[...] = jnp.zeros_like(dwacc)
            for cp in copies(src_ref, land_ref, send_sems, recv_sems):
                cp.start()

        cv = cv_ref[...].astype(F32)
        sg = _sig(cg_ref[...].astype(F32))
        ubuf[HALO:HALO + tm, :] = cv * sg
        hu = hcv_ref[...].astype(F32) * _sig(hcg_ref[...].astype(F32))
        ubuf[0:HALO, :] = jnp.where(t > 0, hu, 0.0)
        ubuf[HALO + tm:HALO + tm + 8, :] = jnp.zeros((8, D), F32)
        dcbuf[0:tm, :] = dc_ref[...].astype(F32)
        dcbuf[tm:tm + HALO, :] = jnp.where(t < nt - 1, hdc_ref[...].astype(F32), 0.0)
        dcbuf[tm + HALO:tm + HALO + 8, :] = jnp.zeros((8, D), F32)

        def chunk(ci, carry):
            r0 = pl.multiple_of(ci * CONV_RC, CONV_RC)
            for lg in range(D // CONV_LC):
                l0 = lg * CONV_LC
                dubuf[pl.ds(r0, CONV_RC), pl.ds(l0, CONV_LC)] = _conv_taps(
                    wdw_ref, dcbuf, r0, l0, lambda k: CONV_K - 1 - k)
                dcc = dcbuf[pl.ds(r0, CONV_RC), pl.ds(l0, CONV_LC)]
                zero8 = jnp.zeros((8, CONV_LC), F32)
                dcz = jnp.concatenate([zero8, dcc, zero8], axis=0)
                for bb in range(8):
                    taps = [k for k in range(CONV_K) if (HALO - (CONV_K - 1) + k) % 8 == bb]
                    if not taps:
                        continue
                    rows = CONV_RC + (8 if bb else 0)
                    if bb:
                        shbuf[bb] = dcz[8 - bb:8 - bb + rows]
                    for k in taps:
                        a8 = HALO - (CONV_K - 1) + k - bb
                        dcs = shbuf[bb] if bb else dcc
                        prod = dcs * ubuf[pl.ds(r0 + a8, rows), pl.ds(l0, CONV_LC)]
                        part = prod[0:8]
                        for q in range(1, rows // 8):
                            part = part + prod[8 * q:8 * q + 8]
                        dwacc[8 * k:8 * k + 8, pl.ds(l0, CONV_LC)] += part
            return carry

        lax.fori_loop(0, tm // CONV_RC, chunk, 0)
        du = dubuf[...]
        dz_ref[:, 0:D] = (du * sg).astype(BF16)
        dz_ref[:, D:2 * D] = (du * cv * sg * (1.0 - sg)).astype(BF16)

        @pl.when((b == pl.num_programs(0) - 1) & (t == nt - 1))
        def _():
            for k in range(32):
                gw_ref[k:k + 1, :] = jnp.sum(dwacc[8 * k:8 * k + 8, :], axis=0, keepdims=True)
            cps = copies(src_ref, land_ref, send_sems, recv_sems)
            for cp in cps:
                cp.wait_recv()
            for cp in cps:
                cp.wait_send()

    def row(b, t):
        return b * nt + t

    def prev_halo(b, t):
        return jnp.maximum(row(b, t) * hb - 1, 0)

    def next_halo(b, t):
        return jnp.minimum((row(b, t) + 1) * hb, nrows - 1)

    return pl.pallas_call(
        body, name="conv_bwd", grid=(T // S, nt),
        in_specs=[pl.BlockSpec((tm, D), lambda b, t: (row(b, t), 0)),
                  pl.BlockSpec((HALO, D), lambda b, t: (next_halo(b, t), 0)),
                  pl.BlockSpec((tm, D), lambda b, t: (row(b, t), ZB_CVAL)),
                  pl.BlockSpec((tm, D), lambda b, t: (row(b, t), ZB_CGLU)),
                  pl.BlockSpec((HALO, D), lambda b, t: (prev_halo(b, t), ZB_CVAL)),
                  pl.BlockSpec((HALO, D), lambda b, t: (prev_halo(b, t), ZB_CGLU)),
                  pl.BlockSpec((32, D), lambda b, t: (0, 0)),
                  pl.BlockSpec(memory_space=pl.ANY), pl.BlockSpec(memory_space=pl.ANY)],
        out_specs=(pl.BlockSpec((tm, 2 * D), lambda b, t: (row(b, t), ZB_CVAL // 2)),
                   pl.BlockSpec((32, D), lambda b, t: (0, 0)), pl.BlockSpec(memory_space=pl.ANY)),
        out_shape=(jax.ShapeDtypeStruct(dz.shape, BF16), jax.ShapeDtypeStruct((32, D), F32), landing),
        input_output_aliases={7: 0},
        scratch_shapes=[pltpu.VMEM((tm + HALO + 8, D), F32), pltpu.VMEM((tm + HALO + 8, D), F32),
                        pltpu.VMEM((tm, D), F32), pltpu.VMEM((8 * 32, D), F32),
                        pltpu.VMEM((8, CONV_RC + 8, CONV_LC), F32), pltpu.SemaphoreType.DMA((3,)),
                        pltpu.SemaphoreType.DMA((3,))],
        compiler_params=_params(("arbitrary", "arbitrary")),
    )(dc, dc, z, z, z, z, wdw, dz, src)


def _attn_bwd(z, zkv, o, do, cos_t, sin_t, sinks, dz, S, tq, copies, src, landing):
    T = z.shape[0]
    nt = S // tq
    nq = tq // BLOCK

    def body(sink_ref, q_ref, kv_ref, hkv_ref, o_ref, do_ref, cos_ref, sin_ref, hcos_ref, hsin_ref, dz_in,
             src_ref, dq_ref, dkv_ref, gs_ref, land_ref, carry, dkacc, dvacc, send_sems, recv_sems):
        b = pl.program_id(0)
        tt = pl.program_id(1)
        t = nt - 1 - tt

        @pl.when((b == 0) & (tt == 0))
        def _():
            gs_ref[...] = jnp.zeros_like(gs_ref)
            for cp in copies(src_ref, land_ref, send_sems, recv_sems):
                cp.start()

        @pl.when(tt == 0)
        def _():
            carry[...] = jnp.zeros_like(carry)

        cos = cos_ref[...]
        sin = sin_ref[...]
        pswap = _swap_matrix()
        kv = jnp.concatenate([hkv_ref[...], kv_ref[...]], axis=0)
        cos_k = jnp.concatenate([hcos_ref[...], cos], axis=0)
        sin_k = jnp.concatenate([hsin_ref[...], sin], axis=0)
        kx = _kv_variants(_rope(kv[:, :BLOCK], cos_k, sin_k, pswap))
        vx = _kv_variants(kv[:, BLOCK:].astype(F32))
        band, sj = _band_mask(4)
        lo = lax.broadcasted_iota(jnp.int32, (4 * BLOCK, BLOCK), 1) < HEAD_DIM
        ones = jnp.ones((2 * BLOCK, 2 * BLOCK), BF16)
        qs = [(_rope(q_ref[:, 128 * hp:128 * hp + 128], cos, sin, pswap) * 0.125).astype(BF16)
              for hp in range(8)]
        dkacc[...] = jnp.zeros_like(dkacc)
        dvacc[...] = jnp.zeros_like(dvacc)
        gsum = jnp.zeros((1, BLOCK), F32)
        hlane = lax.broadcasted_iota(jnp.int32, (1, BLOCK), 1)
        for n in range(nq):
            first = (t == 0) & (n == 0)
            valid = band & (jnp.logical_not(first) | (sj >= BLOCK))
            r0 = n * BLOCK
            for g in range(2):
                cols = [slice(128 * (4 * g + j), 128 * (4 * g + j) + 128) for j in range(4)]
                lhs = jnp.concatenate([qs[4 * g + j][r0:r0 + BLOCK] for j in range(4)], axis=0)
                dov = jnp.concatenate([do_ref[r0:r0 + BLOCK, cs] for cs in cols], axis=0)
                prod = dov.astype(F32) * jnp.concatenate(
                    [o_ref[r0:r0 + BLOCK, cs] for cs in cols], axis=0).astype(F32)
                lhs_t = lhs.T
                dov_t = dov.T
                dq = jnp.zeros((4 * BLOCK, BLOCK), F32)
                dk_t = jnp.zeros((HEAD_DIM, 2 * BLOCK), F32)
                dv_t = jnp.zeros((HEAD_DIM, 2 * BLOCK), F32)
                for e in range(2):
                    kw = kx[g][e][r0:r0 + 2 * BLOCK]
                    vw = vx[g][e][r0:r0 + 2 * BLOCK]
                    s = _mm_nt(lhs, kw)
                    p, psink = _softmax_sink(s, valid, _sink_rep(sink_ref, g, e))
                    pe = jnp.where(lo if e == 0 else jnp.logical_not(lo), prod, 0.0)
                    pe_hi = pe.astype(BF16)
                    pe_lo = (pe - pe_hi.astype(F32)).astype(BF16)
                    delta = _mm(jnp.concatenate([pe_hi, pe_lo], axis=1), ones)
                    ds = (p * (_mm_nt(dov, vw) - delta)).astype(BF16)
                    dq = dq + _mm(ds, kw)
                    dims = slice(HEAD_DIM * e, HEAD_DIM * (e + 1))
                    dk_t = dk_t + _mm(lhs_t[dims], ds)
                    dv_t = dv_t + _mm(dov_t[dims], p.astype(BF16))
                    gs = -psink * delta[:, 0:BLOCK]
                    for j in range(4):
                        tot = jnp.sum(gs[j * BLOCK:(j + 1) * BLOCK], axis=0, keepdims=True)
                        gsum = gsum + jnp.where(hlane == 8 * g + 2 * j + e, tot, 0.0)
                dkacc[HEAD_DIM * g:HEAD_DIM * (g + 1), r0:r0 + 2 * BLOCK] += dk_t
                dvacc[HEAD_DIM * g:HEAD_DIM * (g + 1), r0:r0 + 2 * BLOCK] += dv_t
                for j in range(4):
                    dqj = _rope_f32(dq[j * BLOCK:(j + 1) * BLOCK] * 0.125, cos[r0:r0 + BLOCK],
                                    -sin[r0:r0 + BLOCK], pswap)
                    dq_ref[r0:r0 + BLOCK, cols[j]] = dqj.astype(BF16)
        gs_ref[0:1, :] += gsum
        dk_all = dkacc[...]
        dv_all = dvacc[...]
        dk_last = dk_all[:, tq:tq + BLOCK] + carry[0:BLOCK, :]
        dv_last = dv_all[:, tq:tq + BLOCK] + carry[BLOCK:2 * BLOCK, :]
        carry[0:BLOCK, :] = dk_all[:, 0:BLOCK]
        carry[BLOCK:2 * BLOCK, :] = dv_all[:, 0:BLOCK]
        if nq > 1:
            dk_tile = jnp.concatenate([dk_all[:, BLOCK:tq], dk_last], axis=1)
            dv_tile = jnp.concatenate([dv_all[:, BLOCK:tq], dv_last], axis=1)
        else:
            dk_tile, dv_tile = dk_last, dv_last
        dkv_ref[:, 0:BLOCK] = _rope_f32(dk_tile.T, cos, -sin, pswap).astype(BF16)
        dkv_ref[:, BLOCK:2 * BLOCK] = dv_tile.T.astype(BF16)

        @pl.when((b == pl.num_programs(0) - 1) & (tt == nt - 1))
        def _():
            cps = copies(src_ref, land_ref, send_sems, recv_sems)
            for cp in cps:
                cp.wait_recv()
            for cp in cps:
                cp.wait_send()

    def row(b, tt):
        return b * nt + (nt - 1 - tt)

    def halo(b, tt):
        return jnp.maximum(row(b, tt) * nq - 1, 0)

    tile = pl.BlockSpec((tq, D), lambda b, tt: (row(b, tt), 0))
    return pl.pallas_call(
        body, name="attn_bwd", grid=(T // S, nt),
        in_specs=[pl.BlockSpec(memory_space=pltpu.SMEM),
                  pl.BlockSpec((tq, D), lambda b, tt: (row(b, tt), ZB_Q)),
                  pl.BlockSpec((tq, 2 * BLOCK), lambda b, tt: (row(b, tt), 0)),
                  pl.BlockSpec((BLOCK, 2 * BLOCK), lambda b, tt: (halo(b, tt), 0)),
                  tile, tile,
                  pl.BlockSpec((tq, BLOCK), lambda b, tt: (row(b, tt), 0)),
                  pl.BlockSpec((tq, BLOCK), lambda b, tt: (row(b, tt), 0)),
                  pl.BlockSpec((BLOCK, BLOCK), lambda b, tt: (halo(b, tt), 0)),
                  pl.BlockSpec((BLOCK, BLOCK), lambda b, tt: (halo(b, tt), 0)),
                  pl.BlockSpec(memory_space=pl.ANY), pl.BlockSpec(memory_space=pl.ANY)],
        out_specs=(pl.BlockSpec((tq, D), lambda b, tt: (row(b, tt), ZB_Q)),
                   pl.BlockSpec((tq, 2 * BLOCK), lambda b, tt: (row(b, tt), 0)),
                   pl.BlockSpec((8, BLOCK), lambda b, tt: (0, 0)), pl.BlockSpec(memory_space=pl.ANY)),
        out_shape=(jax.ShapeDtypeStruct(dz.shape, BF16), jax.ShapeDtypeStruct((T, 2 * BLOCK), BF16),
                   jax.ShapeDtypeStruct((8, BLOCK), F32), landing),
        input_output_aliases={10: 0},
        scratch_shapes=[pltpu.VMEM((2 * BLOCK, BLOCK), F32), pltpu.VMEM((BLOCK, tq + BLOCK), F32),
                        pltpu.VMEM((BLOCK, tq + BLOCK), F32), pltpu.SemaphoreType.DMA((3,)),
                        pltpu.SemaphoreType.DMA((3,))],
        compiler_params=_params(("arbitrary", "arbitrary")),
    )(sinks, z, zkv, zkv, o, do, cos_t, sin_t, cos_t, sin_t, dz, src)


def _dh(dz, dz_kv, wall, x, dx1, ln_pre, tm, tile0, ntiles, gx_prev, name, copies, src, landing):
    T = x.shape[0]
    nsem = 3

    def body(*refs):
        dz_ref, kv_ref, wt_ref, x_ref, dx1_ref, g_ref, src_ref = refs[:7]
        gx_ref, glp_ref, land_ref, wbuf, send_sems, recv_sems, wsem = refs[-7:]
        i = pl.program_id(0)

        @pl.when(i == 0)
        def _():
            glp_ref[...] = jnp.zeros_like(glp_ref)
            for cp in copies(src_ref, land_ref, send_sems, recv_sems):
                cp.start()
            load = pltpu.make_async_copy(wt_ref, wbuf, wsem)
            load.start()
            load.wait()

        dh = _mm(dz_ref[...], wbuf[0:ZKV, :]) + _mm(kv_ref[...], wbuf[ZKV:IN_WIDTH, :])
        xv = x_ref[...]
        r = lax.rsqrt(jnp.mean(xv * xv, axis=-1, keepdims=True) + EPS)
        xr = xv * r
        glp_ref[...] += jnp.sum(dh * xr, axis=0, keepdims=True)
        a = dh * g_ref[...]
        gx_ref[...] = dx1_ref[...] + r * (a - xr * jnp.mean(a * xr, axis=-1, keepdims=True))

        @pl.when(i == ntiles - 1)
        def _():
            cps = copies(src_ref, land_ref, send_sems, recv_sems)
            for cp in cps:
                cp.wait_recv()
            for cp in cps:
                cp.wait_send()

    tile = pl.BlockSpec((tm, D), lambda i: (tile0 + i, 0))
    any_spec = pl.BlockSpec(memory_space=pl.ANY)
    operands = [dz, dz_kv, wall, x, dx1, ln_pre, src] + ([] if gx_prev is None else [gx_prev])
    return pl.pallas_call(
        body, name=name, grid=(ntiles,),
        in_specs=[pl.BlockSpec((tm, ZKV), lambda i: (tile0 + i, 0)),
                  pl.BlockSpec((tm, 2 * BLOCK), lambda i: (tile0 + i, 0)),
                  any_spec, tile, tile, pl.BlockSpec((1, D), lambda i: (0, 0)), any_spec]
        + ([] if gx_prev is None else [any_spec]),
        out_specs=(tile, pl.BlockSpec((1, D), lambda i: (0, 0)), any_spec),
        out_shape=(jax.ShapeDtypeStruct((T, D), F32), jax.ShapeDtypeStruct((1, D), F32), landing),
        input_output_aliases={} if gx_prev is None else {7: 0},
        scratch_shapes=[pltpu.VMEM((IN_WIDTH, D), BF16), pltpu.SemaphoreType.DMA((nsem,)),
                        pltpu.SemaphoreType.DMA((nsem,)), pltpu.SemaphoreType.DMA],
        compiler_params=_params(("arbitrary",)),
    )(*operands)


def _gwt(dz, dz_kv, h, tt):
    T = dz.shape[0]
    nt = T // tt
    last = nt - 1
    kv = 2 * BLOCK
    half = WIN_SHARD // 2

    def body(dz_ref, dzkv_ref, h_ref, gpack_ref, r1_ref, hbuf, acc, hsems, sems, send_sems, recv_sems):
        j = pl.program_id(0)
        t = pl.program_id(1)
        slot = j % 2
        rows = pl.ds(pl.multiple_of(t * tt, tt), tt)
        x, y, c = _coords()

        def exchange(jj):
            wall0, n_rows = (WT0 + jj * D, D) if jj < 7 else (WT0 + ZKV, kv)
            for _, n, s, pr in _wall_segments(wall0, n_rows):
                for hb in range(2):
                    lo, hi = max(pr, hb * half), min(pr + n, (hb + 1) * half)
                    if lo < hi:
                        cp = pltpu.make_async_remote_copy(
                            src_ref=gpack_ref.at[s, pl.ds(lo, hi - lo)],
                            dst_ref=r1_ref.at[s, pl.ds(lo - hb * half, hi - lo)], send_sem=send_sems.at[0],
                            recv_sem=recv_sems.at[0], device_id=(x, y, 1 - c), device_id_type=MESH)
                        pl.when(c == 1 - hb)(cp.start)

        def h_load(i):
            return pltpu.make_async_copy(h_ref.at[pl.ds(i * tt, tt)], hbuf.at[pl.ds(i * tt, tt)], hsems.at[i])

        @pl.when((j == 0) & (t == 0))
        def _():
            for i in range(nt):
                h_load(i).start()

        for i in range(nt):
            pl.when((j == 0) & (t == i))(h_load(i).wait)

        @pl.when((j < 7) & (t == 0))
        def _():
            acc[slot] = _mm_tn(dz_ref[...], hbuf[rows, :])

        @pl.when((j < 7) & (t > 0))
        def _():
            acc[slot] += _mm_tn(dz_ref[...], hbuf[rows, :])

        @pl.when((j == 7) & (t == 0))
        def _():
            acc[1, 0:kv, :] = _mm_tn(dzkv_ref[...], hbuf[rows, :])

        @pl.when((j == 7) & (t > 0))
        def _():
            acc[1, 0:kv, :] += _mm_tn(dzkv_ref[...], hbuf[rows, :])

        def block_total(sl):
            return pltpu.make_async_copy(acc.at[sl], gpack_ref.at[0, pl.ds(0, D)], sems.at[sl])

        for jj in range(8):
            @pl.when((t == last) & (j == jj))
            def _(jj=jj):
                if jj >= 1:
                    block_total((jj - 1) % 2).wait()
                    exchange(jj - 1)
                if jj == 7:
                    _flush_to_pack(acc.at[1, pl.ds(0, kv)], gpack_ref, WT0 + ZKV, sems.at[1])
                    exchange(7)
                    whole = _exchange_copies(gpack_ref, r1_ref, send_sems, recv_sems)[0]
                    whole.wait_recv()
                    whole.wait_send()
                else:
                    for cp in _pack_copies(acc.at[jj % 2], gpack_ref, WT0 + jj * D, sems.at[jj % 2]):
                        cp.start()

    any_spec = pl.BlockSpec(memory_space=pl.ANY)
    return pl.pallas_call(
        body, name="gwt", grid=(8, nt),
        in_specs=[pl.BlockSpec((tt, D), lambda j, t: (jnp.where(j == 7, last, t), jnp.minimum(j, 6))),
                  pl.BlockSpec((tt, kv), lambda j, t: (jnp.where(j == 7, t, 0), 0)), any_spec],
        out_specs=(any_spec, any_spec),
        out_shape=(jax.ShapeDtypeStruct((N_SHARDS, WIN_SHARD, D), F32),
                   jax.ShapeDtypeStruct((N_SHARDS, half, D), F32)),
        scratch_shapes=[pltpu.VMEM((T, D), BF16), pltpu.VMEM((2, D, D), F32), pltpu.SemaphoreType.DMA((nt,)),
                        pltpu.SemaphoreType.DMA((2,)), pltpu.SemaphoreType.DMA((1,)),
                        pltpu.SemaphoreType.DMA((1,))],
        compiler_params=_params(("arbitrary", "arbitrary")),
    )(dz, dz_kv, h)


_BC1 = 1.0 - ADAM_B1 ** ADAM_STEP
_BC2 = 1.0 - ADAM_B2 ** ADAM_STEP


def _adamw_math(w, g, m, v):
    m = ADAM_B1 * m + (1.0 - ADAM_B1) * g
    v = ADAM_B2 * v + (1.0 - ADAM_B2) * (g * g)
    delta = -ADAM_LR * ((m / _BC1) / (jnp.sqrt(v / _BC2) + ADAM_EPS) + ADAM_WD * w)
    return delta, m, v


def _adamw_rows(g, w, m, v, rows, name):
    R, C = w.shape

    def body(g_ref, w_ref, m_ref, v_ref, go_ref, d_ref, nm_ref, nv_ref):
        gv = g_ref[...]
        d, nm, nv = _adamw_math(w_ref[...], gv, m_ref[...], v_ref[...])
        go_ref[...] = gv
        d_ref[...] = d
        nm_ref[...] = nm
        nv_ref[...] = nv

    spec = pl.BlockSpec((rows, C), lambda i: (i, 0))
    shp = jax.ShapeDtypeStruct((R, C), F32)
    return pl.pallas_call(
        body, name=name, grid=(R // rows,), in_specs=[spec] * 4, out_specs=(spec,) * 4,
        out_shape=(shp,) * 4, compiler_params=_params(("arbitrary",)),
    )(g, w, m, v)


def _adamw_square(gfin, ws, ms, vs):
    rb = 64
    nb = SQ_SHARD // rb

    def body(*refs):
        g_refs = refs[0:5]
        w_refs, m_refs, v_refs = refs[5:10], refs[10:15], refs[15:20]
        outs = refs[20:]
        for k in range(5):
            gk = g_refs[k][...]
            d, nm, nv = _adamw_math(w_refs[k][...], gk, m_refs[k][...], v_refs[k][...])
            outs[4 * k][...] = gk
            outs[4 * k + 1][...] = d
            outs[4 * k + 2][...] = nm
            outs[4 * k + 3][...] = nv

    spec = pl.BlockSpec((rb, D), lambda i: (i, 0))
    gspecs = [pl.BlockSpec((rb, D), lambda i, k=k: (SQ_SHARD * k // rb + i, 0)) for k in range(5)]
    shp = jax.ShapeDtypeStruct((SQ_SHARD, D), F32)
    res = pl.pallas_call(
        body, name="adamw_square", grid=(nb,), in_specs=gspecs + [spec] * 15, out_specs=(spec,) * 20,
        out_shape=(shp,) * 20, compiler_params=_params(("arbitrary",)),
    )(*([gfin] * 5), *ws, *ms, *vs)
    return [tuple(res[4 * k:4 * k + 4]) for k in range(5)]


def _adamw_small(gs, ws, ms, vs):
    n = len(gs)

    def body(*refs):
        outs = refs[4 * n:]
        for k in range(n):
            d, nm, nv = _adamw_math(refs[n + k][...], refs[k][...], refs[2 * n + k][...],
                                    refs[3 * n + k][...])
            outs[3 * k][...] = d
            outs[3 * k + 1][...] = nm
            outs[3 * k + 2][...] = nv

    vm = pl.BlockSpec(memory_space=pltpu.VMEM)
    shapes = []
    for w in ws:
        shapes += [jax.ShapeDtypeStruct(w.shape, F32)] * 3
    res = pl.pallas_call(
        body, name="adamw_small", in_specs=[vm] * (4 * n), out_specs=(vm,) * (3 * n),
        out_shape=tuple(shapes),
    )(*gs, *ws, *ms, *vs)
    return [tuple(res[3 * k:3 * k + 3]) for k in range(n)]


def _rope_constants():
    half = ROPE_DIM // 2
    inv = jnp.power(ROPE_THETA, -jnp.arange(0, ROPE_DIM, 2, dtype=F32) / ROPE_DIM)
    freq = jnp.concatenate([inv, jnp.zeros((ROPE_ROWS - half,), F32)]).reshape(ROPE_ROWS, 1)
    spread = np.zeros((3, ROPE_ROWS, BLOCK), np.float32)
    for lane in range(BLOCK):
        d = lane % HEAD_DIM
        if d < ROPE_DIM:
            spread[0, d % half, lane] = 1.0
            spread[1, d % half, lane] = -1.0 if d < half else 1.0
        else:
            spread[2, 0, lane] = 1.0
    return freq, jnp.asarray(spread, BF16)


def kernel(x, p, positions, w_in, ln_pre, ln_post, w_dw, b_dw, conv_ln_g, conv_ln_b, w_pw, sinks, w_br_conv, w_br_attn, w_out, w_ple_gate, w_ple_proj, loss_target, m_w_in, m_ln_pre, m_ln_post, m_w_dw, m_b_dw, m_conv_ln_g, m_conv_ln_b, m_w_pw, m_sinks, m_w_br_conv, m_w_br_attn, m_w_out, m_w_ple_gate, m_w_ple_proj, v_w_in, v_ln_pre, v_ln_post, v_w_dw, v_b_dw, v_conv_ln_g, v_conv_ln_b, v_w_pw, v_sinks, v_w_br_conv, v_w_br_attn, v_w_out, v_w_ple_gate, v_w_ple_proj):
    nb, S, _ = x.shape
    T = nb * S
    xc = lax.axis_index("x")
    yc = lax.axis_index("y")
    cc = lax.axis_index("c")
    shard = 2 * xc + yc

    sq_w = (w_pw, w_br_conv, w_br_attn, w_out, w_ple_gate)
    wdw_shard = jnp.pad(w_dw[0], ((0, 1), (0, 0)))
    x2 = x.reshape(T, D)
    tm_res = min(TILE_RESIDENT, T // 2)
    h, cos_t, sin_t = _prenorm(x2, ln_pre, positions.astype(F32).reshape(1, T), *_rope_constants(), tm_res)

    tgt = loss_target.reshape(T, D)
    p2 = p.reshape(T, PLE)
    sinks1 = sinks.reshape(N_HEADS)

    tm = min(TILE_TOKEN, S)
    tq = min(TILE_ATTN, S)

    z, zkv, wt, wdw_all = _inproj(h, w_in[0].T.astype(BF16), wdw_shard, min(TILE_PROJ, T // 2))
    wdw = jnp.concatenate([wdw_all[s] for s in range(N_SHARDS)], axis=1)
    sq_shards = [w[0].astype(BF16) for w in sq_w] + [w_ple_proj[0].T.reshape(WPP_SHARD, D).astype(BF16)]
    o, wall_a = _attn_fwd(z, zkv, cos_t, sin_t, sinks1, S, tq, GROUP_CONV, sq_shards[0:2])
    ya, y, rstd, pw, wall_b, wppf = _conv_fwd(z, wdw, b_dw, conv_ln_g, conv_ln_b, wall_a, S, tm, GROUP_TAIL,
                                              sq_shards[2:])
    wppt = wppf.reshape(D, PLE)
    loss_p, dx1, dm, yb, g_ln_post, gsq, gw_ppt = _tail_a(x2, tgt, p2, o, ya, z, ln_post, wall_b, wppt, tm)

    cidx = jnp.reshape(cc, (1,)).astype(jnp.int32)
    scidx = jnp.stack([shard, cc]).astype(jnp.int32)

    def landing(pack, n, dtype):
        return jax.ShapeDtypeStruct((n, pack.shape[1] // 2, D), dtype)

    dz, do, dc, gvec, gsq = _tail_b(dm, ya, yb, o, z, pw, y, rstd, conv_ln_g, conv_ln_b, wall_a, wall_b,
                                    gw_ppt.reshape(PLE, D), gsq, tm)
    dz, g_wdw, r1_sq = _conv_bwd(dc, z, wdw, dz, S, tm, _exchange_copies, gsq, landing(gsq, N_SHARDS, F32))
    cs_sq = _chip_sum(cidx, gsq, r1_sq, "chip_sum_sq")
    dz, dkv, g_sinks, r2_sq = _attn_bwd(z, zkv, o, do, cos_t, sin_t, sinks1, dz, S, tq, _chip_sum_copies, cs_sq,
                                        landing(gsq, 3, BF16))
    gwt_pack, r1_wt = _gwt(dz, dkv, h, min(2 * TILE_PROJ, T))
    cs_wt = _chip_sum(cidx, gwt_pack, r1_wt, "chip_sum_wt")
    gx, g_ln_pre, r2_wt = _dh(dz, dkv, wt, x2, dx1, ln_pre, tm_res, 0, T // tm_res, None, "dh", _chip_sum_copies,
                              cs_wt, landing(gwt_pack, 3, BF16))
    row37 = jnp.concatenate([g_sinks[0:1, 0:N_HEADS], loss_p, jnp.zeros((1, D - N_HEADS - 1), F32)], axis=1)
    vec = jnp.concatenate([g_wdw, g_ln_pre, g_ln_post, gvec[2:3], gvec[0:1], gvec[1:2], row37,
                           jnp.zeros((VEC_ROWS - 38, D), F32)], axis=0)
    gfin_wt, gfin_sq, tot = _finish_reduce(_final_half(scidx, gwt_pack, r1_wt, r2_wt, "final_half_wt"),
                                           _final_half(scidx, gsq, r1_sq, r2_sq, "final_half_sq"), vec)

    g_w_in, d_w_in, nm_w_in, nv_w_in = [a.T for a in _adamw_rows(
        gfin_wt, w_in[0].T, m_w_in[0].T, v_w_in[0].T, WIN_SHARD // 8, "adamw_w_in")]
    g_w_in = g_w_in[None]
    sq_m = (m_w_pw, m_w_br_conv, m_w_br_attn, m_w_out, m_w_ple_gate)
    sq_v = (v_w_pw, v_w_br_conv, v_w_br_attn, v_w_out, v_w_ple_gate)
    sq_res = _adamw_square(gfin_sq, [w[0] for w in sq_w], [m[0] for m in sq_m], [v[0] for v in sq_v])
    g_wpp = gfin_sq[5 * SQ_SHARD:SQ_PACK].reshape(PLE, PLE).T
    g_dw_all = tot[0:CONV_K]
    g_dw = lax.dynamic_slice_in_dim(g_dw_all, shard * PLE, PLE, axis=1)
    small_g = [g_wpp, g_dw, tot[32:33], tot[33:34], tot[34:35], tot[35:36], tot[36:37],
               tot[37:38, 0:N_HEADS]]
    small_w = [w_ple_proj[0], w_dw[0], ln_pre, ln_post, b_dw, conv_ln_g, conv_ln_b, sinks]
    small_m = [m_w_ple_proj[0], m_w_dw[0], m_ln_pre, m_ln_post, m_b_dw, m_conv_ln_g, m_conv_ln_b, m_sinks]
    small_v = [v_w_ple_proj[0], v_w_dw[0], v_ln_pre, v_ln_post, v_b_dw, v_conv_ln_g, v_conv_ln_b, v_sinks]
    small = _adamw_small(small_g, small_w, small_m, small_v)

    loss = tot[37, N_HEADS]
    grads = [g_w_in, small_g[2], small_g[3], g_dw[None], small_g[4], small_g[5], small_g[6],
             sq_res[0][0][None], small_g[7], sq_res[1][0][None], sq_res[2][0][None], sq_res[3][0][None],
             sq_res[4][0][None], g_wpp[None]]

    def triple(i):
        w_in_t = (d_w_in[None], nm_w_in[None], nv_w_in[None])
        sq = lambda k: tuple(a[None] for a in sq_res[k][1:4])
        sm = lambda k, lead: tuple(a[None] if lead else a for a in small[k])
        return [w_in_t[i], sm(2, False)[i], sm(3, False)[i], sm(1, True)[i], sm(4, False)[i],
                sm(5, False)[i], sm(6, False)[i], sq(0)[i], sm(7, False)[i], sq(1)[i], sq(2)[i], sq(3)[i],
                sq(4)[i], sm(0, True)[i]]

    return (loss, gx.reshape(nb, S, D), *grads, *triple(0), *triple(1), *triple(2))
```

```python
import functools

import jax
import jax.numpy as jnp
import numpy as np
from jax import lax
from jax.experimental import pallas as pl
from jax.experimental.pallas import tpu as pltpu

F32 = jnp.float32
BF16 = jnp.bfloat16

D = 1024
PLE = 256
N_HEADS = 16
HEAD_DIM = 64
BLOCK = 128
CONV_K = 31
ROPE_DIM = 16
ROPE_THETA = 500000.0
EPS = 1e-6
IN_WIDTH = 7424
N_SHARDS = 4

ADAM_LR = 0.001
ADAM_B1 = 0.9
ADAM_B2 = 0.999
ADAM_EPS = 1e-08
ADAM_WD = 0.01
ADAM_STEP = 10

SQ_NAMES = ("w_pw", "w_br_conv", "w_br_attn", "w_out", "w_ple_gate")
WT0 = 5 * D
WPP0 = WT0 + IN_WIDTH
WALL_ROWS = WPP0 + PLE
WIN_SHARD = IN_WIDTH // N_SHARDS
SQ_SHARD = D // N_SHARDS
WPP_SHARD = PLE * PLE // D
PACK_ROWS = WIN_SHARD + 5 * SQ_SHARD + WPP_SHARD
HALF_ROWS = PACK_ROWS // 2
VMEM_LIMIT = 56 * 1024 * 1024
MESH = pl.DeviceIdType.MESH
TILE_RESIDENT = 512
TILE_PROJ = 1024
TILE_TOKEN = 256
TILE_ATTN = 512
TAIL_PARTS = 1


ZB_AGATE, ZB_GCONV, ZB_GATTN, ZB_CGATE, ZB_CVAL, ZB_CGLU, ZB_Q = range(7)
ZKV = 7 * D
_SEGMENTS = ((0, D, ZB_CVAL * D), (D, D, ZB_CGLU * D), (2 * D, D, ZB_CGATE * D), (3 * D, D, ZB_Q * D),
             (4 * D, 2 * BLOCK, ZKV), (4 * D + 2 * BLOCK, D, ZB_AGATE * D),
             (5 * D + 2 * BLOCK, D, ZB_GCONV * D), (6 * D + 2 * BLOCK, D, ZB_GATTN * D))
_WT_CUTS = (0, 192, 640, 1216, WIN_SHARD)


def _zp_row(o):
    for a, w, zp in _SEGMENTS:
        if a <= o < a + w:
            return zp + o - a
    raise ValueError(o)


def _pieces(s):
    out = []
    for a, b in zip(_WT_CUTS[:-1], _WT_CUTS[1:]):
        first = _zp_row(WIN_SHARD * s + a)
        assert _zp_row(WIN_SHARD * s + b - 1) == first + b - a - 1
        out.append((a, b - a, WT0 + first))
    for k in range(5):
        out.append((WIN_SHARD + SQ_SHARD * k, SQ_SHARD, D * k + SQ_SHARD * s))
    out.append((WIN_SHARD + 5 * SQ_SHARD, WPP_SHARD, WPP0 + WPP_SHARD * s))
    return out


N_PIECES = len(_pieces(0))


def _wall_segments(wall0, rows):
    out = []
    for s in range(N_SHARDS):
        for pr, n, wr in _pieces(s):
            lo, hi = max(wr, wall0), min(wr + n, wall0 + rows)
            if lo < hi:
                out.append((lo - wall0, hi - lo, s, pr + lo - wr))
    assert sum(n for _, n, _, _ in out) == rows
    return out


def _sel(s, vals):
    r = jnp.int32(vals[0])
    for i in range(1, len(vals)):
        r = jnp.where(s == i, jnp.int32(vals[i]), r)
    return r


def _sig(x):
    return 1.0 / (1.0 + jnp.exp(-x))


def _mm(a, b):
    return lax.dot_general(a, b, (((1,), (0,)), ((), ())), preferred_element_type=F32)


def _mm_nt(a, b):
    return lax.dot_general(a, b, (((1,), (1,)), ((), ())), preferred_element_type=F32)


def _mm_tn(a, b):
    return lax.dot_general(a, b, (((0,), (0,)), ((), ())), preferred_element_type=F32)


def _params(sem=None):
    return pltpu.CompilerParams(dimension_semantics=sem, vmem_limit_bytes=VMEM_LIMIT)


def _flush_to_pack(acc_ref, gpack_ref, wall0, sem):
    for cp in _pack_copies(acc_ref, gpack_ref, wall0, sem):
        cp.start()
        cp.wait()


def _flush_all(items, gpack_ref):
    for acc_ref, wall0, sem in items:
        for cp in _pack_copies(acc_ref, gpack_ref, wall0, sem):
            cp.start()
    for acc_ref, _, sem in items:
        pltpu.make_async_copy(acc_ref, gpack_ref.at[0, pl.ds(0, acc_ref.shape[0])], sem).wait()


def _pack_copies(acc_ref, gpack_ref, wall0, sem):
    base = 0 if gpack_ref.shape[1] == WIN_SHARD else WIN_SHARD
    out = []
    for r, n, s, pr in _wall_segments(wall0, acc_ref.shape[0]):
        assert 0 <= pr - base and pr - base + n <= gpack_ref.shape[1]
        out.append(pltpu.make_async_copy(acc_ref.at[pl.ds(r, n)], gpack_ref.at[s, pl.ds(pr - base, n)], sem))
    return out


def _coords():
    return lax.axis_index("x"), lax.axis_index("y"), lax.axis_index("c")


def _chip_peers(x, y):
    return [(1 - x, y), (x, 1 - y), (1 - x, 1 - y)]


WIN_PIECES = tuple(range(len(_WT_CUTS) - 1))
SQ_PIECES = tuple(range(len(WIN_PIECES), N_PIECES))


def _gather_ops(group, src, landing, bytes_ref, stage, send_sems, recv_sems, loc_sem):
    sizes = [_pieces(0)[p][1] for p in group]
    half_rows = sum(n // 2 for n in sizes)
    starts = [sum(sizes[:i]) for i in range(len(sizes))]

    def rcopy(a, b, k, dev):
        return pltpu.make_async_remote_copy(src_ref=a, dst_ref=b, send_sem=send_sems.at[k],
                                            recv_sem=recv_sems.at[k], device_id=dev, device_id_type=MESH)

    def total(k):
        x, y, c = _coords()
        rows = bytes_ref.at[pl.ds(0, half_rows)]
        return rcopy(rows, rows, k, (x, y, c))

    def own_total():
        rows = stage.at[pl.ds(0, sum(sizes))]
        return pltpu.make_async_copy(rows, rows, loc_sem)

    def send():
        x, y, c = _coords()
        s_me = 2 * x + y
        for k, (px, py) in enumerate(_chip_peers(x, y)):
            for p, n in zip(group, sizes):
                h = n // 2
                rcopy(src(p, c * h, h), landing(p, s_me, c * h, h), k, (px, py, c)).start()
        for p, n, r in zip(group, sizes, starts):
            pltpu.make_async_copy(src(p, 0, n), stage.at[pl.ds(r, n)], loc_sem).start()

    def forward():
        x, y, c = _coords()
        own_total().wait()
        for p, n, r in zip(group, sizes, starts):
            pltpu.make_async_copy(stage.at[pl.ds(r, n)], landing(p, 2 * x + y, 0, n), loc_sem).start()
        for k, (px, py) in enumerate(_chip_peers(x, y)):
            total(k).wait_recv()
            for p, n in zip(group, sizes):
                rows = landing(p, 2 * px + py, c * (n // 2), n // 2)
                rcopy(rows, rows, 3 + k, (x, y, 1 - c)).start()

    def finish():
        own_total().wait()
        for k in range(3):
            total(3 + k).wait_recv()
        for k in range(6):
            total(k).wait_send()

    return send, forward, finish


def _piece_rows(ref, start, off, n):
    first = start + off
    return ref.at[pl.ds(first if isinstance(first, int) else pl.multiple_of(first, 32), n)]


GROUP_CONV = SQ_PIECES[0:2]
GROUP_TAIL = SQ_PIECES[2:]


def _group_shapes(group):
    n_sq = sum(1 for q in group if q != N_PIECES - 1)
    return [jax.ShapeDtypeStruct((n_sq * D, D), BF16)] + (
        [jax.ShapeDtypeStruct((PLE, D), BF16)] if N_PIECES - 1 in group else [])


def _group_scratch(group):
    return [pltpu.VMEM((sum(_pieces(0)[q][1] for q in group), D), BF16), pltpu.SemaphoreType.DMA((6,)),
            pltpu.SemaphoreType.DMA((6,)), pltpu.SemaphoreType.DMA]


def _group_gather(group, shard_refs, out_refs, scratch, step, n_steps):
    wall_ref = out_refs[0]
    stage, send_sems, recv_sems, loc_sem = scratch

    def src(q, off, n):
        return _piece_rows(shard_refs[group.index(q)], 0, off, n)

    def landing(q, s, off, n):
        if q == N_PIECES - 1:
            return _piece_rows(out_refs[1], WPP_SHARD * s, off, n)
        return _piece_rows(wall_ref, D * group.index(q) + SQ_SHARD * s, off, n)

    send, forward, finish = _gather_ops(group, src, landing, wall_ref, stage, send_sems, recv_sems, loc_sem)
    pl.when(step == 0)(send)
    pl.when(step == n_steps // 2)(forward)
    return finish


ROPE_ROWS = 16


def _rope_tables(pos, freq, spread_ref):
    def to_lanes(v, e):
        out = None
        for _ in range(3):
            part = v.astype(BF16)
            term = _mm_tn(part, e)
            out = term if out is None else out + term
            v = v - part.astype(F32)
        return out

    ang = freq * pos
    return (to_lanes(jnp.cos(ang), spread_ref[0]) + spread_ref[2, 0:1, :].astype(F32),
            to_lanes(jnp.sin(ang), spread_ref[1]))


SQ_PACK = PACK_ROWS - WIN_SHARD


def _row_tile(half):
    return max(t for t in range(8, 321, 8) if half % t == 0)


def _exchange_copies(g_ref, r1_ref, send_sems, recv_sems):
    x, y, c = _coords()
    half = g_ref.shape[1] // 2
    return [pltpu.make_async_remote_copy(
        src_ref=g_ref.at[:, pl.ds(pl.multiple_of((1 - c) * half, 32), half), :], dst_ref=r1_ref,
        send_sem=send_sems.at[0], recv_sem=recv_sems.at[0], device_id=(x, y, 1 - c), device_id_type=MESH)]


def _chip_sum_copies(cs_ref, r2_ref, send_sems, recv_sems):
    x, y, c = _coords()
    return [pltpu.make_async_remote_copy(
        src_ref=cs_ref.at[2 * px + py], dst_ref=r2_ref.at[k], send_sem=send_sems.at[k],
        recv_sem=recv_sems.at[k], device_id=(px, py, c), device_id_type=MESH)
        for k, (px, py) in enumerate(_chip_peers(x, y))]


def _chip_sum(cidx, gpack, r1, name):
    half = gpack.shape[1] // 2
    rt = _row_tile(half)

    def body(c_ref, g_ref, r_ref, o_ref):
        o_ref[...] = (g_ref[...] + r_ref[...]).astype(BF16)

    nt = half // rt
    return pl.pallas_call(
        body, name=name,
        grid_spec=pltpu.PrefetchScalarGridSpec(
            num_scalar_prefetch=1, grid=(N_SHARDS, nt),
            in_specs=[pl.BlockSpec((1, rt, D), lambda s, t, c: (s, c[0] * nt + t, 0)),
                      pl.BlockSpec((1, rt, D), lambda s, t, c: (s, t, 0))],
            out_specs=pl.BlockSpec((1, rt, D), lambda s, t, c: (s, t, 0))),
        out_shape=jax.ShapeDtypeStruct((N_SHARDS, half, D), BF16),
        compiler_params=_params(("arbitrary", "arbitrary")),
    )(cidx, gpack, r1)


def _final_half(sc, gpack, r1, r2, name):
    rows = gpack.shape[1]
    half = rows // 2
    rt = _row_tile(half)

    def body(sc_ref, g_ref, r_ref, p_ref, o_ref):
        acc = g_ref[0] + r_ref[0]
        for k in range(3):
            acc = acc + p_ref[k].astype(F32)
        o_ref[...] = acc

    nt = half // rt
    return pl.pallas_call(
        body, name=name,
        grid_spec=pltpu.PrefetchScalarGridSpec(
            num_scalar_prefetch=1, grid=(nt,),
            in_specs=[pl.BlockSpec((1, rt, D), lambda t, sc: (sc[0], sc[1] * nt + t, 0)),
                      pl.BlockSpec((1, rt, D), lambda t, sc: (sc[0], t, 0)),
                      pl.BlockSpec((3, rt, D), lambda t, sc: (0, t, 0))],
            out_specs=pl.BlockSpec((rt, D), lambda t, sc: (sc[1] * nt + t, 0))),
        out_shape=jax.ShapeDtypeStruct((rows, D), F32),
        compiler_params=_params(("arbitrary",)),
    )(sc, gpack, r1, r2)


VEC_ROWS = 40


def _finish_reduce(fwt, fsq, vec):
    def body(fwt_ref, fsq_ref, v_ref, owt_ref, osq_ref, tot_ref, buf, send_sems, recv_sems):
        x, y, c = _coords()
        swaps = []
        for k, (f_ref, o_ref) in enumerate(((fwt_ref, owt_ref), (fsq_ref, osq_ref))):
            half = f_ref.shape[0] // 2
            rows = pl.ds(pl.multiple_of(c * half, 32), half)
            swaps.append(pltpu.make_async_remote_copy(
                src_ref=f_ref.at[rows], dst_ref=o_ref.at[rows], send_sem=send_sems.at[7 + k],
                recv_sem=recv_sems.at[7 + k], device_id=(x, y, 1 - c), device_id_type=MESH))
        for cp in swaps:
            cp.start()
        me = 4 * x + 2 * y + c
        buf[me] = v_ref[...]
        cps = []
        for r in range(1, 8):
            dx, dy, dc = (r >> 2) & 1, (r >> 1) & 1, r & 1
            peer = (1 - x if dx else x, 1 - y if dy else y, 1 - c if dc else c)
            cp = pltpu.make_async_remote_copy(
                src_ref=v_ref, dst_ref=buf.at[me], send_sem=send_sems.at[r - 1],
                recv_sem=recv_sems.at[r - 1], device_id=peer, device_id_type=MESH)
            cp.start()
            cps.append(cp)
        for cp in cps:
            cp.wait_recv()
        for cp in cps:
            cp.wait_send()
        acc = buf[0]
        for d in range(1, 8):
            acc = acc + buf[d]
        tot_ref[...] = acc
        for cp in swaps:
            cp.wait()

    any_spec = pl.BlockSpec(memory_space=pl.ANY)
    vm = pl.BlockSpec(memory_space=pltpu.VMEM)
    return pl.pallas_call(
        body, name="finish_reduce",
        out_shape=(jax.ShapeDtypeStruct(fwt.shape, F32), jax.ShapeDtypeStruct(fsq.shape, F32),
                   jax.ShapeDtypeStruct((VEC_ROWS, D), F32)),
        in_specs=[any_spec, any_spec, vm], out_specs=(any_spec, any_spec, vm),
        input_output_aliases={0: 0, 1: 1},
        scratch_shapes=[pltpu.VMEM((8, VEC_ROWS, D), F32), pltpu.SemaphoreType.DMA((9,)),
                        pltpu.SemaphoreType.DMA((9,))],
    )(fwt, fsq, vec)


SOLO_ROWS = WIN_SHARD - BLOCK // 2


def _solo_first(s):
    return 0 if s % 2 == 0 else BLOCK // 2


def _solo_segments(s):
    lo = _solo_first(s)
    out = []
    for a, n, wr in _pieces(s)[:len(WIN_PIECES)]:
        b0, b1 = max(a, lo), min(a + n, lo + SOLO_ROWS)
        if b0 >= b1:
            continue
        z0 = wr - WT0 + b0 - a
        if out and out[-1][0] + out[-1][1] == b0 - lo and out[-1][2] + out[-1][1] == z0:
            out[-1] = (out[-1][0], out[-1][1] + b1 - b0, out[-1][2])
        else:
            out.append((b0 - lo, b1 - b0, z0))
    out = [r for o, n, z0 in out for r in
           (((o, ZKV - z0, z0), (o + ZKV - z0, z0 + n - ZKV, ZKV)) if z0 < ZKV < z0 + n else ((o, n, z0),))]
    assert all(v % BLOCK == 0 for seg in out for v in seg) and sum(n for _, n, _ in out) == SOLO_ROWS
    return out


def _shared_tile(pair):
    z0 = _zp_row(WIN_SHARD * (2 * pair) + SOLO_ROWS)
    assert z0 % BLOCK == 0 and _zp_row(WIN_SHARD * (2 * pair + 1)) == z0 + BLOCK // 2
    return z0


def _inproj(x, ln_pre, pos, freq, spread, win_t, wdw_shard, tm):
    T = x.shape[0]
    n_t = T // tm
    assert n_t >= 2 and n_t % 2 == 0
    tables = [[_pieces(s)[p][2] - WT0 for s in range(N_SHARDS)] for p in WIN_PIECES]
    sizes = [_pieces(0)[p][1] for p in WIN_PIECES]
    half_rows = sum(n // 2 for n in sizes)
    relation_of_pass = {1: 1, 2: 0, 3: 2}

    def body(x_ref, g_ref, pos_ref, f_ref, e_ref, win_ref, wdw_ref, z_ref, zkv_ref, wt_ref, wdwall_ref, h_ref,
             cos_ref, sin_ref, wbuf, stage, stage_sh, hbuf, wsend, wrecv, loc_sems, out_sems, sh_sems, h_sems):
        p = pl.program_id(0)
        t = pl.program_id(1)
        x, y, c = _coords()
        s_me = 2 * x + y
        peers = _chip_peers(x, y)
        shard = jnp.bitwise_xor(s_me, p)
        first, last = t == 0, t == n_t - 1

        def rcopy(a, b, k, dev):
            return pltpu.make_async_remote_copy(src_ref=a, dst_ref=b, send_sem=wsend.at[k], recv_sem=wrecv.at[k],
                                                device_id=dev, device_id_type=MESH)

        def total(k):
            rows = wt_ref.at[pl.ds(0, half_rows)]
            return rcopy(rows, rows, k, (x, y, c))

        def in_hbm(q, s, off, n):
            return _piece_rows(wt_ref, _sel(s, tables[q]), off, n)

        def in_vmem(q, s):
            return _piece_rows(wbuf, WIN_SHARD * s + _WT_CUTS[q], 0, sizes[q])

        def send_to(k):
            px, py = peers[k]
            for q, n in zip(WIN_PIECES, sizes):
                rcopy(_piece_rows(win_ref, _WT_CUTS[q], c * (n // 2), n // 2), in_hbm(q, s_me, c * (n // 2), n // 2),
                      k, (px, py, c)).start()

        def forward_from(k):
            px, py = peers[k]
            total(k).wait_recv()
            for q, n in zip(WIN_PIECES, sizes):
                rows = in_hbm(q, 2 * px + py, c * (n // 2), n // 2)
                rcopy(rows, rows, 3 + k, (x, y, 1 - c)).start()

        def shard_total(a, b, sem):
            return pltpu.make_async_copy(a.at[pl.ds(0, WIN_SHARD)], b.at[pl.ds(0, WIN_SHARD)], sem)

        def wdw_copies():
            return [pltpu.make_async_remote_copy(
                src_ref=wdw_ref, dst_ref=wdwall_ref.at[s_me], send_sem=wsend.at[6 + k], recv_sem=wrecv.at[6 + k],
                device_id=(px, py, c), device_id_type=MESH) for k, (px, py) in enumerate(peers)]

        def own_wdw():
            return pltpu.make_async_copy(wdw_ref, wdwall_ref.at[s_me], loc_sems.at[2])

        @pl.when((p == 0) & first)
        def _():
            send_to(0)
            send_to(1)
            own_wdw().start()
            for cp in wdw_copies():
                cp.start()
            for q in WIN_PIECES:
                pltpu.make_async_copy(_piece_rows(win_ref, _WT_CUTS[q], 0, sizes[q]), in_vmem(q, s_me),
                                      loc_sems.at[0]).start()
            shard_total(win_ref, wbuf, loc_sems.at[0]).wait()
            for q in WIN_PIECES:
                pltpu.make_async_copy(in_vmem(q, s_me), in_hbm(q, s_me, 0, sizes[q]), loc_sems.at[1]).start()

        for pp, k in relation_of_pass.items():
            pl.when((p == pp - 1) & (t == n_t - 2))(functools.partial(forward_from, k))

            @pl.when((p == pp - 1) & last)
            def _(k=k):
                total(3 + k).wait_recv()
                px, py = peers[k]
                for q in WIN_PIECES:
                    pltpu.make_async_copy(in_hbm(q, 2 * px + py, 0, sizes[q]), in_vmem(q, 2 * px + py),
                                          loc_sems.at[0]).start()

            @pl.when((p == pp) & first)
            def _(pp=pp):
                shard_total(wt_ref, wbuf, loc_sems.at[0]).wait()
                if pp == 1:
                    total(0).wait_send()
                    total(1).wait_send()
                    send_to(2)

        step = p * n_t + t
        slot = step % 2
        rows = pl.ds(pl.multiple_of(t * tm, tm), tm)

        def out_total(sl):
            return pltpu.make_async_copy(stage.at[sl], stage.at[sl], out_sems.at[sl])

        def sh_copy(sl, z0):
            return pltpu.make_async_copy(stage_sh.at[sl], z_ref.at[rows, pl.ds(z0, BLOCK)], sh_sems.at[sl])

        @pl.when(step >= 2)
        def _():
            out_total(slot).wait()

        @pl.when((step >= 2) & (((step - 2) // n_t) % 2 == 1))
        def _():
            sh_copy(slot, 0).wait()

        def h_out(sl):
            return pltpu.make_async_copy(hbuf.at[sl], h_ref.at[rows], h_sems.at[sl])

        def h_in(sl, tile):
            return pltpu.make_async_copy(h_ref.at[pl.ds(pl.multiple_of(tile * tm, tm), tm)], hbuf.at[sl],
                                         h_sems.at[sl])

        @pl.when((p == 0) & (t >= 2))
        def _():
            h_out(slot).wait()

        @pl.when(p == 0)
        def _():
            xv = x_ref[...]
            r = lax.rsqrt(jnp.mean(xv * xv, axis=-1, keepdims=True) + EPS)
            hbuf[slot] = (xv * r * g_ref[...]).astype(BF16)
            h_out(slot).start()
            cos, sin = _rope_tables(pos_ref[...], f_ref[...], e_ref)
            cos_ref[...] = cos
            sin_ref[...] = sin

        @pl.when((p == 1) & first)
        def _():
            h_out(0).wait()
            h_out(1).wait()
            h_in(0, 0).start()

        @pl.when(p >= 1)
        def _():
            h_in(slot, t).wait()

        @pl.when((p >= 1) & (step < N_SHARDS * n_t - 1))
        def _():
            h_in(1 - slot, jnp.where(last, 0, t + 1)).start()

        solo0 = pl.multiple_of(WIN_SHARD * shard + (BLOCK // 2) * (shard % 2), BLOCK // 2)
        stage[slot] = _mm_nt(hbuf[slot], wbuf[pl.ds(solo0, SOLO_ROWS), :]).astype(BF16)
        for s in range(N_SHARDS):
            @pl.when(shard == s)
            def _(s=s):
                for off, n, z0 in _solo_segments(s):
                    dst = zkv_ref.at[rows] if z0 == ZKV else z_ref.at[rows, pl.ds(z0, n)]
                    pltpu.make_async_copy(stage.at[slot, :, pl.ds(off, n)], dst, out_sems.at[slot]).start()

        @pl.when(p % 2 == 1)
        def _():
            pair = shard // 2
            w0 = pl.multiple_of(2 * WIN_SHARD * pair + SOLO_ROWS, BLOCK // 2)
            z0 = pl.multiple_of(jnp.where(pair == 0, _shared_tile(0), _shared_tile(1)), BLOCK)
            stage_sh[slot] = _mm_nt(hbuf[slot], wbuf[pl.ds(w0, BLOCK), :]).astype(BF16)
            sh_copy(slot, z0).start()

        @pl.when((p == 3) & last)
        def _():
            for k in (2, 3, 4, 5):
                total(k).wait_send()
            shard_total(wbuf, wt_ref, loc_sems.at[1]).wait()
            cps = wdw_copies()
            for cp in cps:
                cp.wait_recv()
            for cp in cps:
                cp.wait_send()
            own_wdw().wait()
            for sl in range(2):
                out_total(sl).wait()
                sh_copy(sl, 0).wait()

    def in_pass0(p, t):
        return jnp.where(p == 0, t, n_t - 1)

    any_spec = pl.BlockSpec(memory_space=pl.ANY)
    return pl.pallas_call(
        body, name="inproj", grid=(N_SHARDS, n_t),
        in_specs=[pl.BlockSpec((tm, D), lambda p, t: (in_pass0(p, t), 0)), pl.BlockSpec((1, D), lambda p, t: (0, 0)),
                  pl.BlockSpec((1, tm), lambda p, t: (0, in_pass0(p, t))),
                  pl.BlockSpec((ROPE_ROWS, 1), lambda p, t: (0, 0)),
                  pl.BlockSpec((3, ROPE_ROWS, BLOCK), lambda p, t: (0, 0, 0)), any_spec, any_spec],
        out_specs=(any_spec,) * 5 + (pl.BlockSpec((tm, BLOCK), lambda p, t: (in_pass0(p, t), 0)),) * 2,
        out_shape=(jax.ShapeDtypeStruct((T, ZKV), BF16), jax.ShapeDtypeStruct((T, 2 * BLOCK), BF16),
                   jax.ShapeDtypeStruct((IN_WIDTH, D), BF16), jax.ShapeDtypeStruct((N_SHARDS, 32, PLE), F32),
                   jax.ShapeDtypeStruct((T, D), BF16), jax.ShapeDtypeStruct((T, BLOCK), F32),
                   jax.ShapeDtypeStruct((T, BLOCK), F32)),
        scratch_shapes=[pltpu.VMEM((IN_WIDTH, D), BF16), pltpu.VMEM((2, tm, SOLO_ROWS), BF16),
                        pltpu.VMEM((2, tm, BLOCK), BF16), pltpu.VMEM((2, tm, D), BF16),
                        pltpu.SemaphoreType.DMA((9,)), pltpu.SemaphoreType.DMA((9,)),
                        pltpu.SemaphoreType.DMA((3,)), pltpu.SemaphoreType.DMA((2,)),
                        pltpu.SemaphoreType.DMA((2,)), pltpu.SemaphoreType.DMA((2,))],
        compiler_params=_params(("arbitrary", "arbitrary")),
    )(x, ln_pre, pos, freq, spread, win_t, wdw_shard)


HALO = 32
CONV_RC = 64
CONV_LC = 256


def _conv_taps(w_ref, src, r0, lane0, offset_of_tap):
    lanes = pl.ds(lane0, CONV_LC)
    out = None
    for b in range(8):
        taps = [k for k in range(CONV_K) if offset_of_tap(k) % 8 == b]
        if not taps:
            continue
        rows = CONV_RC + (8 if b else 0)
        vb = None
        for k in taps:
            term = w_ref[k:k + 1, lanes] * src[pl.ds(r0 + (offset_of_tap(k) - b), rows), lanes]
            vb = term if vb is None else vb + term
        vb = vb[b:b + CONV_RC] if b else vb
        out = vb if out is None else out + vb
    return out


def _conv_fwd(z, wdw, b_dw, ln_g, ln_b, wall, S, tm, group, shards):
    T = z.shape[0]
    nt = S // tm
    hb = tm // HALO
    gathered = _group_shapes(group)

    def body(cv_ref, cg_ref, cgate_ref, hcv_ref, hcg_ref, wdw_ref, bdw_ref, lng_ref, lnb_ref, wpw_ref,
             wbrc_ref, *rest):
        shard_refs, rest = rest[:len(group)], rest[len(group):]
        ya_ref, y_ref, rstd_ref, pw_ref = rest[:4]
        gather_refs, (ubuf, cbuf), gather_scratch = rest[4:4 + len(gathered)], rest[-6:-4], rest[-4:]
        t = pl.program_id(1)
        step = pl.program_id(0) * nt + t
        finish_gather = _group_gather(group, shard_refs, gather_refs, gather_scratch, step, T // tm)
        ubuf[HALO:HALO + tm, :] = cv_ref[...].astype(F32) * _sig(cg_ref[...].astype(F32))
        hu = hcv_ref[...].astype(F32) * _sig(hcg_ref[...].astype(F32))
        ubuf[0:HALO, :] = jnp.where(t > 0, hu, 0.0)
        ubuf[HALO + tm:HALO + tm + 8, :] = jnp.zeros((8, D), F32)

        def chunk(ci, carry):
            r0 = pl.multiple_of(ci * CONV_RC, CONV_RC)
            for lg in range(D // CONV_LC):
                acc = _conv_taps(wdw_ref, ubuf, r0, lg * CONV_LC, lambda k: HALO - (CONV_K - 1) + k)
                cbuf[pl.ds(r0, CONV_RC), pl.ds(lg * CONV_LC, CONV_LC)] = acc
            return carry

        lax.fori_loop(0, tm // CONV_RC, chunk, 0)
        cc = cbuf[...] + bdw_ref[...]
        mu = jnp.mean(cc, axis=-1, keepdims=True)
        dd = cc - mu
        rstd = lax.rsqrt(jnp.mean(dd * dd, axis=-1, keepdims=True) + EPS)
        yn = dd * rstd
        y_ref[...] = yn.astype(BF16)
        rstd_ref[...] = rstd
        n = yn * lng_ref[...] + lnb_ref[...]
        s = n * _sig(n)
        pw = _mm(s.astype(BF16), wpw_ref[...])
        pw_ref[...] = pw.astype(BF16)
        gt = cgate_ref[...].astype(F32)
        ya_in = pw * (gt * _sig(gt))
        ya_ref[...] = _mm(ya_in.astype(BF16), wbrc_ref[...]).astype(BF16)
        pl.when(step == T // tm - 1)(finish_gather)

    def row(b, t):
        return b * nt + t

    def halo(b, t):
        return jnp.maximum(row(b, t) * hb - 1, 0)

    vec = pl.BlockSpec((1, D), lambda b, t: (0, 0))
    tile = lambda j: pl.BlockSpec((tm, D), lambda b, t: (row(b, t), j))
    out_tile = pl.BlockSpec((tm, D), lambda b, t: (row(b, t), 0))
    any_spec = pl.BlockSpec(memory_space=pl.ANY)
    return pl.pallas_call(
        body, name="conv_fwd", grid=(T // S, nt),
        in_specs=[tile(ZB_CVAL), tile(ZB_CGLU), tile(ZB_CGATE),
                  pl.BlockSpec((HALO, D), lambda b, t: (halo(b, t), ZB_CVAL)),
                  pl.BlockSpec((HALO, D), lambda b, t: (halo(b, t), ZB_CGLU)),
                  pl.BlockSpec((32, D), lambda b, t: (0, 0)), vec, vec, vec,
                  pl.BlockSpec((D, D), lambda b, t: (0, 0)),
                  pl.BlockSpec((D, D), lambda b, t: (1, 0))] + [any_spec] * len(group),
        out_specs=(out_tile, out_tile, pl.BlockSpec((tm, 1), lambda b, t: (row(b, t), 0)), out_tile)
        + (any_spec,) * len(gathered),
        out_shape=[jax.ShapeDtypeStruct((T, D), BF16), jax.ShapeDtypeStruct((T, D), BF16),
                   jax.ShapeDtypeStruct((T, 1), F32), jax.ShapeDtypeStruct((T, D), BF16)] + gathered,
        scratch_shapes=[pltpu.VMEM((tm + HALO + 8, D), F32), pltpu.VMEM((tm, D), F32)] + _group_scratch(group),
        compiler_params=_params(("arbitrary", "arbitrary")),
    )(z, z, z, z, z, wdw, b_dw, ln_g, ln_b, wall, wall, *shards)


def _swap_matrix():
    r = lax.broadcasted_iota(jnp.int32, (BLOCK, BLOCK), 0)
    l = lax.broadcasted_iota(jnp.int32, (BLOCK, BLOCK), 1)
    lh = l & (HEAD_DIM - 1)
    half = ROPE_DIM // 2
    hit = ((lh < half) & (r == l + half)) | ((lh >= half) & (lh < ROPE_DIM) & (r == l - half))
    return jnp.where(hit, 1.0, 0.0).astype(BF16)


def _rope(tb, cos, sin, pswap):
    return tb.astype(F32) * cos + _mm(tb, pswap) * sin


def _rope_f32(tv, cos, sin, pswap):
    hi = tv.astype(BF16)
    lo = (tv - hi.astype(F32)).astype(BF16)
    return tv * cos + (_mm(hi, pswap) + _mm(lo, pswap)) * sin


def _kv_variants(kv):
    lane = lax.broadcasted_iota(jnp.int32, kv.shape, 1)
    lo = lane < HEAD_DIM
    sw = pltpu.roll(kv, HEAD_DIM, 1)
    z = jnp.zeros_like(kv)
    g0 = (jnp.where(lo, kv, z).astype(BF16), jnp.where(lo, z, sw).astype(BF16))
    g1 = (jnp.where(lo, sw, z).astype(BF16), jnp.where(lo, z, kv).astype(BF16))
    return (g0, g1)


def _band_mask(nq):
    qi = lax.broadcasted_iota(jnp.int32, (nq * BLOCK, 2 * BLOCK), 0) & (BLOCK - 1)
    sj = lax.broadcasted_iota(jnp.int32, (nq * BLOCK, 2 * BLOCK), 1)
    return (sj <= qi + BLOCK) & (sj > qi), sj


def _sink_rep(sink_ref, g, e):
    return jnp.concatenate(
        [jnp.full((BLOCK, BLOCK), sink_ref[8 * g + 2 * j + e], F32) for j in range(4)], axis=0)


def _softmax_parts(s, valid, sk):
    rows = s.shape[0]
    s = jnp.where(valid, s, -1e30)
    m = jnp.maximum(jnp.broadcast_to(jnp.max(s, axis=-1, keepdims=True), (rows, BLOCK)), sk)
    return jnp.exp(s - jnp.concatenate([m, m], axis=1)), jnp.exp(sk - m)


def _softmax_sink(s, valid, sk):
    p, ps = _softmax_parts(s, valid, sk)
    inv = 1.0 / (_mm(p.astype(BF16), jnp.ones((2 * BLOCK, BLOCK), BF16)) + ps)
    return p * jnp.concatenate([inv, inv], axis=1), ps * inv


def _attn_fwd(z, zkv, cos_t, sin_t, sinks, S, tq, group, shards):
    T = z.shape[0]
    nt = S // tq
    nq = tq // BLOCK
    gathered = _group_shapes(group)

    def body(sink_ref, q_ref, kv_ref, hkv_ref, cos_ref, sin_ref, hcos_ref, hsin_ref, *rest):
        shard_refs, o_ref = rest[:len(group)], rest[len(group)]
        t = pl.program_id(1)
        step = pl.program_id(0) * nt + t
        finish_gather = _group_gather(group, shard_refs, rest[len(group) + 1:-4], rest[-4:], step, T // tq)
        cos = cos_ref[...]
        sin = sin_ref[...]
        pswap = _swap_matrix()
        kv = jnp.concatenate([hkv_ref[...], kv_ref[...]], axis=0)
        cos_k = jnp.concatenate([hcos_ref[...], cos], axis=0)
        sin_k = jnp.concatenate([hsin_ref[...], sin], axis=0)
        kx = _kv_variants(_rope(kv[:, :BLOCK], cos_k, sin_k, pswap))
        one = jnp.ones((tq + BLOCK, BLOCK), BF16)
        vx = [[jnp.concatenate([v, one], axis=1) for v in vg] for vg in _kv_variants(kv[:, BLOCK:].astype(F32))]
        band, sj = _band_mask(4)
        qs = [(_rope(q_ref[:, 128 * hp:128 * hp + 128], cos, sin, pswap) * 0.125).astype(BF16)
              for hp in range(8)]
        for n in range(nq):
            first = (t == 0) & (n == 0)
            valid = band & (jnp.logical_not(first) | (sj >= BLOCK))
            r0 = n * BLOCK
            for g in range(2):
                lhs = jnp.concatenate([qs[4 * g + j][r0:r0 + BLOCK] for j in range(4)], axis=0)
                acc = jnp.zeros((4 * BLOCK, BLOCK), F32)
                for e in range(2):
                    s = _mm_nt(lhs, kx[g][e][r0:r0 + 2 * BLOCK])
                    p, ps = _softmax_parts(s, valid, _sink_rep(sink_ref, g, e))
                    r = _mm(p.astype(BF16), vx[g][e][r0:r0 + 2 * BLOCK])
                    acc = acc + r[:, 0:BLOCK] * (1.0 / (r[:, BLOCK:2 * BLOCK] + ps))
                for j in range(4):
                    o_ref[r0:r0 + BLOCK, 128 * (4 * g + j):128 * (4 * g + j) + 128] = (
                        acc[j * BLOCK:(j + 1) * BLOCK].astype(BF16))
        pl.when(step == T // tq - 1)(finish_gather)

    def row(b, t):
        return b * nt + t

    def halo(b, t):
        return jnp.maximum(row(b, t) * nq - 1, 0)

    any_spec = pl.BlockSpec(memory_space=pl.ANY)
    return pl.pallas_call(
        body, name="attn_fwd", grid=(T // S, nt),
        in_specs=[pl.BlockSpec(memory_space=pltpu.SMEM),
                  pl.BlockSpec((tq, D), lambda b, t: (row(b, t), ZB_Q)),
                  pl.BlockSpec((tq, 2 * BLOCK), lambda b, t: (row(b, t), 0)),
                  pl.BlockSpec((BLOCK, 2 * BLOCK), lambda b, t: (halo(b, t), 0)),
                  pl.BlockSpec((tq, BLOCK), lambda b, t: (row(b, t), 0)),
                  pl.BlockSpec((tq, BLOCK), lambda b, t: (row(b, t), 0)),
                  pl.BlockSpec((BLOCK, BLOCK), lambda b, t: (halo(b, t), 0)),
                  pl.BlockSpec((BLOCK, BLOCK), lambda b, t: (halo(b, t), 0))] + [any_spec] * len(group),
        out_specs=(pl.BlockSpec((tq, D), lambda b, t: (row(b, t), 0)),) + (any_spec,) * len(gathered),
        out_shape=[jax.ShapeDtypeStruct((T, D), BF16)] + gathered,
        scratch_shapes=_group_scratch(group),
        compiler_params=_params(("arbitrary", "arbitrary")),
    )(sinks, z, zkv, zkv, cos_t, sin_t, cos_t, sin_t, *shards)


def _tail_a(x, tgt, p, o, ya, z, ln_post, wall_b, wppt, tm):
    T = x.shape[0]
    last = T // tm - 1

    def body(x_ref, tgt_ref, p_ref, o_ref, ya_ref, ag_ref, gc_ref, ga_ref, lnp_ref, wbra_ref, wout_ref,
             wpg_ref, wppt_ref, loss_ref, dx1_ref, dm_ref, yb_ref, glnp_ref, gpack_ref, gwpp_ref,
             acc_out, acc_pg, sem):
        i = pl.program_id(0)

        @pl.when(i == 0)
        def _():
            acc_out[...] = jnp.zeros_like(acc_out)
            acc_pg[...] = jnp.zeros_like(acc_pg)
            gwpp_ref[...] = jnp.zeros_like(gwpp_ref)
            glnp_ref[...] = jnp.zeros_like(glnp_ref)
            loss_ref[...] = jnp.zeros_like(loss_ref)

        ag = ag_ref[...].astype(F32)
        yb_in = (o_ref[...].astype(F32) * (ag * _sig(ag))).astype(BF16)
        yb = _mm(yb_in, wbra_ref[...])
        yb_ref[...] = yb.astype(BF16)
        m = (_sig(gc_ref[...].astype(F32)) * ya_ref[...].astype(F32)
             + _sig(ga_ref[...].astype(F32)) * yb).astype(BF16)
        mo = _mm(m, wout_ref[...])
        r2 = lax.rsqrt(jnp.mean(mo * mo, axis=-1, keepdims=True) + EPS)
        nrm = mo * r2
        g_post = lnp_ref[...]
        x1 = x_ref[...] + nrm * g_post
        x1b = x1.astype(BF16)
        gate = _sig(_mm(x1b, wpg_ref[...]))
        pb = p_ref[...].astype(BF16)
        pp = _mm_nt(pb, wppt_ref[...])
        err = x1 + gate * pp - tgt_ref[...]
        loss_ref[...] += 0.5 * jnp.sum(jnp.sum(err * err, axis=-1, keepdims=True) * (1.0 / D),
                                       axis=0, keepdims=True)
        dx2 = err * (1.0 / D)
        dgp = (dx2 * pp * gate * (1.0 - gate)).astype(BF16)
        dpp = (dx2 * gate).astype(BF16)
        dx1 = dx2 + _mm_nt(dgp, wpg_ref[...])
        dx1_ref[...] = dx1
        acc_pg[...] += _mm_tn(x1b, dgp)
        gwpp_ref[...] += _mm_tn(dpp, pb)
        glnp_ref[...] += jnp.sum(dx1 * nrm, axis=0, keepdims=True)
        a = dx1 * g_post
        dmo = (r2 * (a - nrm * jnp.mean(a * nrm, axis=-1, keepdims=True))).astype(BF16)
        dm_ref[...] = _mm_nt(dmo, wout_ref[...]).astype(BF16)
        acc_out[...] += _mm_tn(m, dmo)

        @pl.when(i == last)
        def _():
            _flush_all([(acc_out, 3 * D, sem.at[0]), (acc_pg, 4 * D, sem.at[1])], gpack_ref)

    tile = pl.BlockSpec((tm, D), lambda i: (i, 0))
    ztile = lambda j: pl.BlockSpec((tm, D), lambda i: (i, j))
    wsq = lambda k: pl.BlockSpec((D, D), lambda i: (k, 0))
    const = lambda shp: pl.BlockSpec(shp, lambda i: (0, 0))
    any_spec = pl.BlockSpec(memory_space=pl.ANY)
    return pl.pallas_call(
        body, name="tail_a", grid=(T // tm,),
        in_specs=[tile, tile, pl.BlockSpec((tm, PLE), lambda i: (i, 0)), tile, tile, ztile(ZB_AGATE),
                  ztile(ZB_GCONV), ztile(ZB_GATTN), const((1, D)), wsq(0), wsq(1), wsq(2), const((D, PLE))],
        out_specs=(const((1, 1)), tile, tile, tile, const((1, D)), any_spec, const((D, PLE))),
        out_shape=(jax.ShapeDtypeStruct((1, 1), F32), jax.ShapeDtypeStruct((T, D), F32),
                   jax.ShapeDtypeStruct((T, D), BF16), jax.ShapeDtypeStruct((T, D), BF16),
                   jax.ShapeDtypeStruct((1, D), F32), jax.ShapeDtypeStruct((N_SHARDS, SQ_PACK, D), F32),
                   jax.ShapeDtypeStruct((D, PLE), F32)),
        scratch_shapes=[pltpu.VMEM((D, D), F32), pltpu.VMEM((D, D), F32), pltpu.SemaphoreType.DMA((2,))],
        compiler_params=_params(("arbitrary",)),
    )(x, tgt, p, o, ya, z, z, z, ln_post, wall_b, wall_b, wall_b, wppt)


def _dsilu(v, sg):
    return sg * (1.0 + v * (1.0 - sg))


def _tail_b(dm, ya, yb, o, z, pw, y, rstd, ln_g, ln_b, wall_a, wall_b, gppt, gpack, tm):
    T = dm.shape[0]
    last = T // tm - 1

    def body(dm_ref, ya_ref, yb_ref, o_ref, ag_ref, gc_ref, ga_ref, cgate_ref, pw_ref, y_ref, rstd_ref,
             lng_ref, lnb_ref, wpw_ref, wbrc_ref, wbra_ref, gppt_ref, gpack_in, dg_ref, do_ref, dc_ref,
             gvec_ref, gpack_ref, acc_bra, acc_brc, acc_pw, sem):
        i = pl.program_id(0)

        @pl.when(i == 0)
        def _():
            acc_bra[...] = jnp.zeros_like(acc_bra)
            acc_brc[...] = jnp.zeros_like(acc_brc)
            acc_pw[...] = jnp.zeros_like(acc_pw)
            gvec_ref[...] = jnp.zeros_like(gvec_ref)

        g = lng_ref[...]

        def part(rs):
            dm_v = dm_ref[rs, :].astype(F32)
            sgc = _sig(gc_ref[rs, :].astype(F32))
            sga = _sig(ga_ref[rs, :].astype(F32))
            dya = (dm_v * sgc).astype(BF16)
            dyb = (dm_v * sga).astype(BF16)
            dg_ref[rs, D:2 * D] = (dm_v * ya_ref[rs, :].astype(F32) * sgc * (1.0 - sgc)).astype(BF16)
            dg_ref[rs, 2 * D:3 * D] = (dm_v * yb_ref[rs, :].astype(F32) * sga * (1.0 - sga)).astype(BF16)
            ag = ag_ref[rs, :].astype(F32)
            sag = _sig(ag)
            sa = ag * sag
            ov = o_ref[rs, :].astype(F32)
            dyb_in = _mm_nt(dyb, wbra_ref[...])
            do_ref[rs, :] = (dyb_in * sa).astype(BF16)
            dg_ref[rs, 0:D] = (dyb_in * ov * _dsilu(ag, sag)).astype(BF16)
            gt = cgate_ref[rs, :].astype(F32)
            sgt = _sig(gt)
            sgate = gt * sgt
            pw = pw_ref[rs, :].astype(F32)
            dya_in = _mm_nt(dya, wbrc_ref[...])
            dpw = (dya_in * sgate).astype(BF16)
            dg_ref[rs, 3 * D:4 * D] = (dya_in * pw * _dsilu(gt, sgt)).astype(BF16)
            yn = y_ref[rs, :].astype(F32)
            n = yn * g + lnb_ref[...]
            sn = _sig(n)
            dn = _mm_nt(dpw, wpw_ref[...]) * _dsilu(n, sn)
            dy = dn * g
            dc = rstd_ref[rs, :] * (dy - jnp.mean(dy, axis=-1, keepdims=True)
                                    - yn * jnp.mean(dy * yn, axis=-1, keepdims=True))
            dc_ref[rs, :] = dc.astype(BF16)
            sums = (jnp.sum(dn * yn, axis=0, keepdims=True), jnp.sum(dn, axis=0, keepdims=True),
                    jnp.sum(dc, axis=0, keepdims=True))
            return ((ov * sa).astype(BF16), dyb, (pw * sgate).astype(BF16), dya, (n * sn).astype(BF16), dpw,
                    sums)

        parts = [part(pl.ds(r * (tm // TAIL_PARTS), tm // TAIL_PARTS)) for r in range(TAIL_PARTS)]
        cat = lambda j: jnp.concatenate([pt[j] for pt in parts], axis=0)
        acc_bra[...] += _mm_tn(cat(0), cat(1))
        acc_brc[...] += _mm_tn(cat(2), cat(3))
        acc_pw[...] += _mm_tn(cat(4), cat(5))
        for j in range(3):
            gvec_ref[j:j + 1, :] += sum(pt[6][j] for pt in parts)

        @pl.when(i == last)
        def _():
            _flush_all([(acc_pw, 0, sem.at[0]), (acc_brc, D, sem.at[1]), (acc_bra, 2 * D, sem.at[2]),
                        (gppt_ref, WPP0, sem.at[3])], gpack_ref)

    tile = pl.BlockSpec((tm, D), lambda i: (i, 0))
    ztile = lambda j: pl.BlockSpec((tm, D), lambda i: (i, j))
    wsq = lambda k: pl.BlockSpec((D, D), lambda i: (k, 0))
    const = lambda shp: pl.BlockSpec(shp, lambda i: (0, 0))
    any_spec = pl.BlockSpec(memory_space=pl.ANY)
    return pl.pallas_call(
        body, name="tail_b", grid=(T // tm,),
        in_specs=[tile, tile, tile, tile, ztile(ZB_AGATE), ztile(ZB_GCONV), ztile(ZB_GATTN), ztile(ZB_CGATE),
                  tile, tile, pl.BlockSpec((tm, 1), lambda i: (i, 0)), const((1, D)), const((1, D)), wsq(0),
                  wsq(1), wsq(0), const((PLE, D)), any_spec],
        out_specs=(pl.BlockSpec((tm, 4 * D), lambda i: (i, 0)), tile, tile, const((8, D)), any_spec),
        out_shape=(jax.ShapeDtypeStruct((T, 7 * D), BF16), jax.ShapeDtypeStruct((T, D), BF16),
                   jax.ShapeDtypeStruct((T, D), BF16), jax.ShapeDtypeStruct((8, D), F32),
                   jax.ShapeDtypeStruct(gpack.shape, F32)),
        input_output_aliases={17: 4},
        scratch_shapes=[pltpu.VMEM((D, D), F32), pltpu.VMEM((D, D), F32), pltpu.VMEM((D, D), F32),
                        pltpu.SemaphoreType.DMA((4,))],
        compiler_params=_params(("arbitrary",)),
    )(dm, ya, yb, o, z, z, z, z, pw, y, rstd, ln_g, ln_b, wall_a, wall_a, wall_b, gppt, gpack)


def _conv_bwd(dc, z, wdw, dz, S, tm, copies, src, landing):
    T = dc.shape[0]
    nt = S // tm
    hb = tm // HALO
    nrows = T // HALO

    def body(dc_ref, hdc_ref, cv_ref, cg_ref, hcv_ref, hcg_ref, wdw_ref, dz_in, src_ref, dz_ref, gw_ref,
             land_ref, ubuf, dcbuf, dubuf, dwacc, shbuf, send_sems, recv_sems):
        b = pl.program_id(0)
        t = pl.program_id(1)

        @pl.when((b == 0) & (t == 0))
        def _():
            dwacc[...] = jnp.zeros_like(dwacc)
            for cp in copies(src_ref, land_ref, send_sems, recv_sems):
                cp.start()

        cv = cv_ref[...].astype(F32)
        sg = _sig(cg_ref[...].astype(F32))
        ubuf[HALO:HALO + tm, :] = cv * sg
        hu = hcv_ref[...].astype(F32) * _sig(hcg_ref[...].astype(F32))
        ubuf[0:HALO, :] = jnp.where(t > 0, hu, 0.0)
        ubuf[HALO + tm:HALO + tm + 8, :] = jnp.zeros((8, D), F32)
        dcbuf[0:tm, :] = dc_ref[...].astype(F32)
        dcbuf[tm:tm + HALO, :] = jnp.where(t < nt - 1, hdc_ref[...].astype(F32), 0.0)
        dcbuf[tm + HALO:tm + HALO + 8, :] = jnp.zeros((8, D), F32)

        def chunk(ci, carry):
            r0 = pl.multiple_of(ci * CONV_RC, CONV_RC)
            for lg in range(D // CONV_LC):
                l0 = lg * CONV_LC
                dubuf[pl.ds(r0, CONV_RC), pl.ds(l0, CONV_LC)] = _conv_taps(
                    wdw_ref, dcbuf, r0, l0, lambda k: CONV_K - 1 - k)
                dcc = dcbuf[pl.ds(r0, CONV_RC), pl.ds(l0, CONV_LC)]
                zero8 = jnp.zeros((8, CONV_LC), F32)
                dcz = jnp.concatenate([zero8, dcc, zero8], axis=0)
                for bb in range(8):
                    taps = [k for k in range(CONV_K) if (HALO - (CONV_K - 1) + k) % 8 == bb]
                    if not taps:
                        continue
                    rows = CONV_RC + (8 if bb else 0)
                    if bb:
                        shbuf[bb] = dcz[8 - bb:8 - bb + rows]
                    for k in taps:
                        a8 = HALO - (CONV_K - 1) + k - bb
                        dcs = shbuf[bb] if bb else dcc
                        prod = dcs * ubuf[pl.ds(r0 + a8, rows), pl.ds(l0, CONV_LC)]
                        part = prod[0:8]
                        for q in range(1, rows // 8):
                            part = part + prod[8 * q:8 * q + 8]
                        dwacc[8 * k:8 * k + 8, pl.ds(l0, CONV_LC)] += part
            return carry

        lax.fori_loop(0, tm // CONV_RC, chunk, 0)
        du = dubuf[...]
        dz_ref[:, 0:D] = (du * sg).astype(BF16)
        dz_ref[:, D:2 * D] = (du * cv * sg * (1.0 - sg)).astype(BF16)

        @pl.when((b == pl.num_programs(0) - 1) & (t == nt - 1))
        def _():
            for k in range(32):
                gw_ref[k:k + 1, :] = jnp.sum(dwacc[8 * k:8 * k + 8, :], axis=0, keepdims=True)
            cps = copies(src_ref, land_ref, send_sems, recv_sems)
            for cp in cps:
                cp.wait_recv()
            for cp in cps:
                cp.wait_send()

    def row(b, t):
        return b * nt + t

    def prev_halo(b, t):
        return jnp.maximum(row(b, t) * hb - 1, 0)

    def next_halo(b, t):
        return jnp.minimum((row(b, t) + 1) * hb, nrows - 1)

    return pl.pallas_call(
        body, name="conv_bwd", grid=(T // S, nt),
        in_specs=[pl.BlockSpec((tm, D), lambda b, t: (row(b, t), 0)),
                  pl.BlockSpec((HALO, D), lambda b, t: (next_halo(b, t), 0)),
                  pl.BlockSpec((tm, D), lambda b, t: (row(b, t), ZB_CVAL)),
                  pl.BlockSpec((tm, D), lambda b, t: (row(b, t), ZB_CGLU)),
                  pl.BlockSpec((HALO, D), lambda b, t: (prev_halo(b, t), ZB_CVAL)),
                  pl.BlockSpec((HALO, D), lambda b, t: (prev_halo(b, t), ZB_CGLU)),
                  pl.BlockSpec((32, D), lambda b, t: (0, 0)),
                  pl.BlockSpec(memory_space=pl.ANY), pl.BlockSpec(memory_space=pl.ANY)],
        out_specs=(pl.BlockSpec((tm, 2 * D), lambda b, t: (row(b, t), ZB_CVAL // 2)),
                   pl.BlockSpec((32, D), lambda b, t: (0, 0)), pl.BlockSpec(memory_space=pl.ANY)),
        out_shape=(jax.ShapeDtypeStruct(dz.shape, BF16), jax.ShapeDtypeStruct((32, D), F32), landing),
        input_output_aliases={7: 0},
        scratch_shapes=[pltpu.VMEM((tm + HALO + 8, D), F32), pltpu.VMEM((tm + HALO + 8, D), F32),
                        pltpu.VMEM((tm, D), F32), pltpu.VMEM((8 * 32, D), F32),
                        pltpu.VMEM((8, CONV_RC + 8, CONV_LC), F32), pltpu.SemaphoreType.DMA((3,)),
                        pltpu.SemaphoreType.DMA((3,))],
        compiler_params=_params(("arbitrary", "arbitrary")),
    )(dc, dc, z, z, z, z, wdw, dz, src)


def _attn_bwd(z, zkv, o, do, cos_t, sin_t, sinks, dz, S, tq, copies, src, landing):
    T = z.shape[0]
    nt = S // tq
    nq = tq // BLOCK

    def body(sink_ref, q_ref, kv_ref, hkv_ref, o_ref, do_ref, cos_ref, sin_ref, hcos_ref, hsin_ref, dz_in,
             src_ref, dq_ref, dkv_ref, gs_ref, land_ref, carry, dkacc, dvacc, send_sems, recv_sems):
        b = pl.program_id(0)
        tt = pl.program_id(1)
        t = nt - 1 - tt

        @pl.when((b == 0) & (tt == 0))
        def _():
            gs_ref[...] = jnp.zeros_like(gs_ref)
            for cp in copies(src_ref, land_ref, send_sems, recv_sems):
                cp.start()

        @pl.when(tt == 0)
        def _():
            carry[...] = jnp.zeros_like(carry)

        cos = cos_ref[...]
        sin = sin_ref[...]
        pswap = _swap_matrix()
        kv = jnp.concatenate([hkv_ref[...], kv_ref[...]], axis=0)
        cos_k = jnp.concatenate([hcos_ref[...], cos], axis=0)
        sin_k = jnp.concatenate([hsin_ref[...], sin], axis=0)
        kx = _kv_variants(_rope(kv[:, :BLOCK], cos_k, sin_k, pswap))
        vx = _kv_variants(kv[:, BLOCK:].astype(F32))
        band, sj = _band_mask(4)
        lo = lax.broadcasted_iota(jnp.int32, (4 * BLOCK, BLOCK), 1) < HEAD_DIM
        ones = jnp.ones((2 * BLOCK, 2 * BLOCK), BF16)
        qs = [(_rope(q_ref[:, 128 * hp:128 * hp + 128], cos, sin, pswap) * 0.125).astype(BF16)
              for hp in range(8)]
        dkacc[...] = jnp.zeros_like(dkacc)
        dvacc[...] = jnp.zeros_like(dvacc)
        gsum = jnp.zeros((1, BLOCK), F32)
        hlane = lax.broadcasted_iota(jnp.int32, (1, BLOCK), 1)
        for n in range(nq):
            first = (t == 0) & (n == 0)
            valid = band & (jnp.logical_not(first) | (sj >= BLOCK))
            r0 = n * BLOCK
            for g in range(2):
                cols = [slice(128 * (4 * g + j), 128 * (4 * g + j) + 128) for j in range(4)]
                lhs = jnp.concatenate([qs[4 * g + j][r0:r0 + BLOCK] for j in range(4)], axis=0)
                dov = jnp.concatenate([do_ref[r0:r0 + BLOCK, cs] for cs in cols], axis=0)
                prod = dov.astype(F32) * jnp.concatenate(
                    [o_ref[r0:r0 + BLOCK, cs] for cs in cols], axis=0).astype(F32)
                lhs_t = lhs.T
                dov_t = dov.T
                dq = jnp.zeros((4 * BLOCK, BLOCK), F32)
                dk_t = jnp.zeros((HEAD_DIM, 2 * BLOCK), F32)
                dv_t = jnp.zeros((HEAD_DIM, 2 * BLOCK), F32)
                for e in range(2):
                    kw = kx[g][e][r0:r0 + 2 * BLOCK]
                    vw = vx[g][e][r0:r0 + 2 * BLOCK]
                    s = _mm_nt(lhs, kw)
                    p, psink = _softmax_sink(s, valid, _sink_rep(sink_ref, g, e))
                    pe = jnp.where(lo if e == 0 else jnp.logical_not(lo), prod, 0.0)
                    pe_hi = pe.astype(BF16)
                    pe_lo = (pe - pe_hi.astype(F32)).astype(BF16)
                    delta = _mm(jnp.concatenate([pe_hi, pe_lo], axis=1), ones)
                    ds = (p * (_mm_nt(dov, vw) - delta)).astype(BF16)
                    dq = dq + _mm(ds, kw)
                    dims = slice(HEAD_DIM * e, HEAD_DIM * (e + 1))
                    dk_t = dk_t + _mm(lhs_t[dims], ds)
                    dv_t = dv_t + _mm(dov_t[dims], p.astype(BF16))
                    gs = -psink * delta[:, 0:BLOCK]
                    for j in range(4):
                        tot = jnp.sum(gs[j * BLOCK:(j + 1) * BLOCK], axis=0, keepdims=True)
                        gsum = gsum + jnp.where(hlane == 8 * g + 2 * j + e, tot, 0.0)
                dkacc[HEAD_DIM * g:HEAD_DIM * (g + 1), r0:r0 + 2 * BLOCK] += dk_t
                dvacc[HEAD_DIM * g:HEAD_DIM * (g + 1), r0:r0 + 2 * BLOCK] += dv_t
                for j in range(4):
                    dqj = _rope_f32(dq[j * BLOCK:(j + 1) * BLOCK] * 0.125, cos[r0:r0 + BLOCK],
                                    -sin[r0:r0 + BLOCK], pswap)
                    dq_ref[r0:r0 + BLOCK, cols[j]] = dqj.astype(BF16)
        gs_ref[0:1, :] += gsum
        dk_all = dkacc[...]
        dv_all = dvacc[...]
        dk_last = dk_all[:, tq:tq + BLOCK] + carry[0:BLOCK, :]
        dv_last = dv_all[:, tq:tq + BLOCK] + carry[BLOCK:2 * BLOCK, :]
        carry[0:BLOCK, :] = dk_all[:, 0:BLOCK]
        carry[BLOCK:2 * BLOCK, :] = dv_all[:, 0:BLOCK]
        if nq > 1:
            dk_tile = jnp.concatenate([dk_all[:, BLOCK:tq], dk_last], axis=1)
            dv_tile = jnp.concatenate([dv_all[:, BLOCK:tq], dv_last], axis=1)
        else:
            dk_tile, dv_tile = dk_last, dv_last
        dkv_ref[:, 0:BLOCK] = _rope_f32(dk_tile.T, cos, -sin, pswap).astype(BF16)
        dkv_ref[:, BLOCK:2 * BLOCK] = dv_tile.T.astype(BF16)

        @pl.when((b == pl.num_programs(0) - 1) & (tt == nt - 1))
        def _():
            cps = copies(src_ref, land_ref, send_sems, recv_sems)
            for cp in cps:
                cp.wait_recv()
            for cp in cps:
                cp.wait_send()

    def row(b, tt):
        return b * nt + (nt - 1 - tt)

    def halo(b, tt):
        return jnp.maximum(row(b, tt) * nq - 1, 0)

    tile = pl.BlockSpec((tq, D), lambda b, tt: (row(b, tt), 0))
    return pl.pallas_call(
        body, name="attn_bwd", grid=(T // S, nt),
        in_specs=[pl.BlockSpec(memory_space=pltpu.SMEM),
                  pl.BlockSpec((tq, D), lambda b, tt: (row(b, tt), ZB_Q)),
                  pl.BlockSpec((tq, 2 * BLOCK), lambda b, tt: (row(b, tt), 0)),
                  pl.BlockSpec((BLOCK, 2 * BLOCK), lambda b, tt: (halo(b, tt), 0)),
                  tile, tile,
                  pl.BlockSpec((tq, BLOCK), lambda b, tt: (row(b, tt), 0)),
                  pl.BlockSpec((tq, BLOCK), lambda b, tt: (row(b, tt), 0)),
                  pl.BlockSpec((BLOCK, BLOCK), lambda b, tt: (halo(b, tt), 0)),
                  pl.BlockSpec((BLOCK, BLOCK), lambda b, tt: (halo(b, tt), 0)),
                  pl.BlockSpec(memory_space=pl.ANY), pl.BlockSpec(memory_space=pl.ANY)],
        out_specs=(pl.BlockSpec((tq, D), lambda b, tt: (row(b, tt), ZB_Q)),
                   pl.BlockSpec((tq, 2 * BLOCK), lambda b, tt: (row(b, tt), 0)),
                   pl.BlockSpec((8, BLOCK), lambda b, tt: (0, 0)), pl.BlockSpec(memory_space=pl.ANY)),
        out_shape=(jax.ShapeDtypeStruct(dz.shape, BF16), jax.ShapeDtypeStruct((T, 2 * BLOCK), BF16),
                   jax.ShapeDtypeStruct((8, BLOCK), F32), landing),
        input_output_aliases={10: 0},
        scratch_shapes=[pltpu.VMEM((2 * BLOCK, BLOCK), F32), pltpu.VMEM((BLOCK, tq + BLOCK), F32),
                        pltpu.VMEM((BLOCK, tq + BLOCK), F32), pltpu.SemaphoreType.DMA((3,)),
                        pltpu.SemaphoreType.DMA((3,))],
        compiler_params=_params(("arbitrary", "arbitrary")),
    )(sinks, z, zkv, zkv, o, do, cos_t, sin_t, cos_t, sin_t, dz, src)


def _dh(dz, dz_kv, wall, x, dx1, ln_pre, tm, tile0, ntiles, gx_prev, name, copies, src, landing):
    T = x.shape[0]
    nsem = 3

    def body(*refs):
        dz_ref, kv_ref, wt_ref, x_ref, dx1_ref, g_ref, src_ref = refs[:7]
        gx_ref, glp_ref, land_ref, wbuf, send_sems, recv_sems, wsem = refs[-7:]
        i = pl.program_id(0)

        @pl.when(i == 0)
        def _():
            glp_ref[...] = jnp.zeros_like(glp_ref)
            for cp in copies(src_ref, land_ref, send_sems, recv_sems):
                cp.start()
            load = pltpu.make_async_copy(wt_ref, wbuf, wsem)
            load.start()
            load.wait()

        dh = _mm(dz_ref[...], wbuf[0:ZKV, :]) + _mm(kv_ref[...], wbuf[ZKV:IN_WIDTH, :])
        xv = x_ref[...]
        r = lax.rsqrt(jnp.mean(xv * xv, axis=-1, keepdims=True) + EPS)
        xr = xv * r
        glp_ref[...] += jnp.sum(dh * xr, axis=0, keepdims=True)
        a = dh * g_ref[...]
        gx_ref[...] = dx1_ref[...] + r * (a - xr * jnp.mean(a * xr, axis=-1, keepdims=True))

        @pl.when(i == ntiles - 1)
        def _():
            cps = copies(src_ref, land_ref, send_sems, recv_sems)
            for cp in cps:
                cp.wait_recv()
            for cp in cps:
                cp.wait_send()

    tile = pl.BlockSpec((tm, D), lambda i: (tile0 + i, 0))
    any_spec = pl.BlockSpec(memory_space=pl.ANY)
    operands = [dz, dz_kv, wall, x, dx1, ln_pre, src] + ([] if gx_prev is None else [gx_prev])
    return pl.pallas_call(
        body, name=name, grid=(ntiles,),
        in_specs=[pl.BlockSpec((tm, ZKV), lambda i: (tile0 + i, 0)),
                  pl.BlockSpec((tm, 2 * BLOCK), lambda i: (tile0 + i, 0)),
                  any_spec, tile, tile, pl.BlockSpec((1, D), lambda i: (0, 0)), any_spec]
        + ([] if gx_prev is None else [any_spec]),
        out_specs=(tile, pl.BlockSpec((1, D), lambda i: (0, 0)), any_spec),
        out_shape=(jax.ShapeDtypeStruct((T, D), F32), jax.ShapeDtypeStruct((1, D), F32), landing),
        input_output_aliases={} if gx_prev is None else {7: 0},
        scratch_shapes=[pltpu.VMEM((IN_WIDTH, D), BF16), pltpu.SemaphoreType.DMA((nsem,)),
                        pltpu.SemaphoreType.DMA((nsem,)), pltpu.SemaphoreType.DMA],
        compiler_params=_params(("arbitrary",)),
    )(*operands)


def _gwt(dz, dz_kv, h, tt):
    T = dz.shape[0]
    nt = T // tt
    last = nt - 1
    kv = 2 * BLOCK
    half = WIN_SHARD // 2

    def body(dz_ref, dzkv_ref, h_ref, gpack_ref, r1_ref, hbuf, acc, hsems, sems, send_sems, recv_sems):
        j = pl.program_id(0)
        t = pl.program_id(1)
        slot = j % 2
        rows = pl.ds(pl.multiple_of(t * tt, tt), tt)
        x, y, c = _coords()

        def exchange(jj):
            wall0, n_rows = (WT0 + jj * D, D) if jj < 7 else (WT0 + ZKV, kv)
            for _, n, s, pr in _wall_segments(wall0, n_rows):
                for hb in range(2):
                    lo, hi = max(pr, hb * half), min(pr + n, (hb + 1) * half)
                    if lo < hi:
                        cp = pltpu.make_async_remote_copy(
                            src_ref=gpack_ref.at[s, pl.ds(lo, hi - lo)],
                            dst_ref=r1_ref.at[s, pl.ds(lo - hb * half, hi - lo)], send_sem=send_sems.at[0],
                            recv_sem=recv_sems.at[0], device_id=(x, y, 1 - c), device_id_type=MESH)
                        pl.when(c == 1 - hb)(cp.start)

        def h_load(i):
            return pltpu.make_async_copy(h_ref.at[pl.ds(i * tt, tt)], hbuf.at[pl.ds(i * tt, tt)], hsems.at[i])

        @pl.when((j == 0) & (t == 0))
        def _():
            for i in range(nt):
                h_load(i).start()

        for i in range(nt):
            pl.when((j == 0) & (t == i))(h_load(i).wait)

        @pl.when((j < 7) & (t == 0))
        def _():
            acc[slot] = _mm_tn(dz_ref[...], hbuf[rows, :])

        @pl.when((j < 7) & (t > 0))
        def _():
            acc[slot] += _mm_tn(dz_ref[...], hbuf[rows, :])

        @pl.when((j == 7) & (t == 0))
        def _():
            acc[1, 0:kv, :] = _mm_tn(dzkv_ref[...], hbuf[rows, :])

        @pl.when((j == 7) & (t > 0))
        def _():
            acc[1, 0:kv, :] += _mm_tn(dzkv_ref[...], hbuf[rows, :])

        def block_total(sl):
            return pltpu.make_async_copy(acc.at[sl], gpack_ref.at[0, pl.ds(0, D)], sems.at[sl])

        for jj in range(8):
            @pl.when((t == last) & (j == jj))
            def _(jj=jj):
                if jj >= 1:
                    block_total((jj - 1) % 2).wait()
                    exchange(jj - 1)
                if jj == 7:
                    _flush_to_pack(acc.at[1, pl.ds(0, kv)], gpack_ref, WT0 + ZKV, sems.at[1])
                    exchange(7)
                    whole = _exchange_copies(gpack_ref, r1_ref, send_sems, recv_sems)[0]
                    whole.wait_recv()
                    whole.wait_send()
                else:
                    for cp in _pack_copies(acc.at[jj % 2], gpack_ref, WT0 + jj * D, sems.at[jj % 2]):
                        cp.start()

    any_spec = pl.BlockSpec(memory_space=pl.ANY)
    return pl.pallas_call(
        body, name="gwt", grid=(8, nt),
        in_specs=[pl.BlockSpec((tt, D), lambda j, t: (jnp.where(j == 7, last, t), jnp.minimum(j, 6))),
                  pl.BlockSpec((tt, kv), lambda j, t: (jnp.where(j == 7, t, 0), 0)), any_spec],
        out_specs=(any_spec, any_spec),
        out_shape=(jax.ShapeDtypeStruct((N_SHARDS, WIN_SHARD, D), F32),
                   jax.ShapeDtypeStruct((N_SHARDS, half, D), F32)),
        scratch_shapes=[pltpu.VMEM((T, D), BF16), pltpu.VMEM((2, D, D), F32), pltpu.SemaphoreType.DMA((nt,)),
                        pltpu.SemaphoreType.DMA((2,)), pltpu.SemaphoreType.DMA((1,)),
                        pltpu.SemaphoreType.DMA((1,))],
        compiler_params=_params(("arbitrary", "arbitrary")),
    )(dz, dz_kv, h)


_BC1 = 1.0 - ADAM_B1 ** ADAM_STEP
_BC2 = 1.0 - ADAM_B2 ** ADAM_STEP


def _adamw_math(w, g, m, v):
    m = ADAM_B1 * m + (1.0 - ADAM_B1) * g
    v = ADAM_B2 * v + (1.0 - ADAM_B2) * (g * g)
    delta = -ADAM_LR * ((m / _BC1) / (jnp.sqrt(v / _BC2) + ADAM_EPS) + ADAM_WD * w)
    return delta, m, v


def _adamw_rows(g, w, m, v, rows, name):
    R, C = w.shape

    def body(g_ref, w_ref, m_ref, v_ref, go_ref, d_ref, nm_ref, nv_ref):
        gv = g_ref[...]
        d, nm, nv = _adamw_math(w_ref[...], gv, m_ref[...], v_ref[...])
        go_ref[...] = gv
        d_ref[...] = d
        nm_ref[...] = nm
        nv_ref[...] = nv

    spec = pl.BlockSpec((rows, C), lambda i: (i, 0))
    shp = jax.ShapeDtypeStruct((R, C), F32)
    return pl.pallas_call(
        body, name=name, grid=(R // rows,), in_specs=[spec] * 4, out_specs=(spec,) * 4,
        out_shape=(shp,) * 4, compiler_params=_params(("arbitrary",)),
    )(g, w, m, v)


def _adamw_square(gfin, ws, ms, vs):
    rb = 64
    nb = SQ_SHARD // rb

    def body(*refs):
        g_refs = refs[0:5]
        w_refs, m_refs, v_refs = refs[5:10], refs[10:15], refs[15:20]
        outs = refs[20:]
        for k in range(5):
            gk = g_refs[k][...]
            d, nm, nv = _adamw_math(w_refs[k][...], gk, m_refs[k][...], v_refs[k][...])
            outs[4 * k][...] = gk
            outs[4 * k + 1][...] = d
            outs[4 * k + 2][...] = nm
            outs[4 * k + 3][...] = nv

    spec = pl.BlockSpec((rb, D), lambda i: (i, 0))
    gspecs = [pl.BlockSpec((rb, D), lambda i, k=k: (SQ_SHARD * k // rb + i, 0)) for k in range(5)]
    shp = jax.ShapeDtypeStruct((SQ_SHARD, D), F32)
    res = pl.pallas_call(
        body, name="adamw_square", grid=(nb,), in_specs=gspecs + [spec] * 15, out_specs=(spec,) * 20,
        out_shape=(shp,) * 20, compiler_params=_params(("arbitrary",)),
    )(*([gfin] * 5), *ws, *ms, *vs)
    return [tuple(res[4 * k:4 * k + 4]) for k in range(5)]


def _adamw_small(gs, ws, ms, vs):
    n = len(gs)

    def body(*refs):
        outs = refs[4 * n:]
        for k in range(n):
            d, nm, nv = _adamw_math(refs[n + k][...], refs[k][...], refs[2 * n + k][...],
                                    refs[3 * n + k][...])
            outs[3 * k][...] = d
            outs[3 * k + 1][...] = nm
            outs[3 * k + 2][...] = nv

    vm = pl.BlockSpec(memory_space=pltpu.VMEM)
    shapes = []
    for w in ws:
        shapes += [jax.ShapeDtypeStruct(w.shape, F32)] * 3
    res = pl.pallas_call(
        body, name="adamw_small", in_specs=[vm] * (4 * n), out_specs=(vm,) * (3 * n),
        out_shape=tuple(shapes),
    )(*gs, *ws, *ms, *vs)
    return [tuple(res[3 * k:3 * k + 3]) for k in range(n)]


def _rope_constants():
    half = ROPE_DIM // 2
    inv = jnp.power(ROPE_THETA, -jnp.arange(0, ROPE_DIM, 2, dtype=F32) / ROPE_DIM)
    freq = jnp.concatenate([inv, jnp.zeros((ROPE_ROWS - half,), F32)]).reshape(ROPE_ROWS, 1)
    spread = np.zeros((3, ROPE_ROWS, BLOCK), np.float32)
    for lane in range(BLOCK):
        d = lane % HEAD_DIM
        if d < ROPE_DIM:
            spread[0, d % half, lane] = 1.0
            spread[1, d % half, lane] = -1.0 if d < half else 1.0
        else:
            spread[2, 0, lane] = 1.0
    return freq, jnp.asarray(spread, BF16)


def kernel(x, p, positions, w_in, ln_pre, ln_post, w_dw, b_dw, conv_ln_g, conv_ln_b, w_pw, sinks, w_br_conv, w_br_attn, w_out, w_ple_gate, w_ple_proj, loss_target, m_w_in, m_ln_pre, m_ln_post, m_w_dw, m_b_dw, m_conv_ln_g, m_conv_ln_b, m_w_pw, m_sinks, m_w_br_conv, m_w_br_attn, m_w_out, m_w_ple_gate, m_w_ple_proj, v_w_in, v_ln_pre, v_ln_post, v_w_dw, v_b_dw, v_conv_ln_g, v_conv_ln_b, v_w_pw, v_sinks, v_w_br_conv, v_w_br_attn, v_w_out, v_w_ple_gate, v_w_ple_proj):
    nb, S, _ = x.shape
    T = nb * S
    xc = lax.axis_index("x")
    yc = lax.axis_index("y")
    cc = lax.axis_index("c")
    shard = 2 * xc + yc

    sq_w = (w_pw, w_br_conv, w_br_attn, w_out, w_ple_gate)
    wdw_shard = jnp.pad(w_dw[0], ((0, 1), (0, 0)))
    x2 = x.reshape(T, D)
    tm_res = min(TILE_RESIDENT, T // 2)

    tgt = loss_target.reshape(T, D)
    p2 = p.reshape(T, PLE)
    sinks1 = sinks.reshape(N_HEADS)

    tm = min(TILE_TOKEN, S)
    tq = min(TILE_ATTN, S)

    z, zkv, wt, wdw_all, h, cos_t, sin_t = _inproj(
        x2, ln_pre, positions.astype(F32).reshape(1, T), *_rope_constants(), w_in[0].T.astype(BF16), wdw_shard,
        min(TILE_PROJ, T // 2))
    wdw = jnp.concatenate([wdw_all[s] for s in range(N_SHARDS)], axis=1)
    sq_shards = [w[0].astype(BF16) for w in sq_w] + [w_ple_proj[0].T.reshape(WPP_SHARD, D).astype(BF16)]
    o, wall_a = _attn_fwd(z, zkv, cos_t, sin_t, sinks1, S, tq, GROUP_CONV, sq_shards[0:2])
    ya, y, rstd, pw, wall_b, wppf = _conv_fwd(z, wdw, b_dw, conv_ln_g, conv_ln_b, wall_a, S, tm, GROUP_TAIL,
                                              sq_shards[2:])
    wppt = wppf.reshape(D, PLE)
    loss_p, dx1, dm, yb, g_ln_post, gsq, gw_ppt = _tail_a(x2, tgt, p2, o, ya, z, ln_post, wall_b, wppt, tm)

    cidx = jnp.reshape(cc, (1,)).astype(jnp.int32)
    scidx = jnp.stack([shard, cc]).astype(jnp.int32)

    def landing(pack, n, dtype):
        return jax.ShapeDtypeStruct((n, pack.shape[1] // 2, D), dtype)

    dz, do, dc, gvec, gsq = _tail_b(dm, ya, yb, o, z, pw, y, rstd, conv_ln_g, conv_ln_b, wall_a, wall_b,
                                    gw_ppt.reshape(PLE, D), gsq, tm)
    dz, g_wdw, r1_sq = _conv_bwd(dc, z, wdw, dz, S, tm, _exchange_copies, gsq, landing(gsq, N_SHARDS, F32))
    cs_sq = _chip_sum(cidx, gsq, r1_sq, "chip_sum_sq")
    dz, dkv, g_sinks, r2_sq = _attn_bwd(z, zkv, o, do, cos_t, sin_t, sinks1, dz, S, tq, _chip_sum_copies, cs_sq,
                                        landing(gsq, 3, BF16))
    gwt_pack, r1_wt = _gwt(dz, dkv, h, min(2 * TILE_PROJ, T))
    cs_wt = _chip_sum(cidx, gwt_pack, r1_wt, "chip_sum_wt")
    gx, g_ln_pre, r2_wt = _dh(dz, dkv, wt, x2, dx1, ln_pre, tm_res, 0, T // tm_res, None, "dh", _chip_sum_copies,
                              cs_wt, landing(gwt_pack, 3, BF16))
    row37 = jnp.concatenate([g_sinks[0:1, 0:N_HEADS], loss_p, jnp.zeros((1, D - N_HEADS - 1), F32)], axis=1)
    vec = jnp.concatenate([g_wdw, g_ln_pre, g_ln_post, gvec[2:3], gvec[0:1], gvec[1:2], row37,
                           jnp.zeros((VEC_ROWS - 38, D), F32)], axis=0)
    gfin_wt, gfin_sq, tot = _finish_reduce(_final_half(scidx, gwt_pack, r1_wt, r2_wt, "final_half_wt"),
                                           _final_half(scidx, gsq, r1_sq, r2_sq, "final_half_sq"), vec)

    g_w_in, d_w_in, nm_w_in, nv_w_in = [a.T for a in _adamw_rows(
        gfin_wt, w_in[0].T, m_w_in[0].T, v_w_in[0].T, WIN_SHARD // 8, "adamw_w_in")]
    g_w_in = g_w_in[None]
    sq_m = (m_w_pw, m_w_br_conv, m_w_br_attn, m_w_out, m_w_ple_gate)
    sq_v = (v_w_pw, v_w_br_conv, v_w_br_attn, v_w_out, v_w_ple_gate)
    sq_res = _adamw_square(gfin_sq, [w[0] for w in sq_w], [m[0] for m in sq_m], [v[0] for v in sq_v])
    g_wpp = gfin_sq[5 * SQ_SHARD:SQ_PACK].reshape(PLE, PLE).T
    g_dw_all = tot[0:CONV_K]
    g_dw = lax.dynamic_slice_in_dim(g_dw_all, shard * PLE, PLE, axis=1)
    small_g = [g_wpp, g_dw, tot[32:33], tot[33:34], tot[34:35], tot[35:36], tot[36:37],
               tot[37:38, 0:N_HEADS]]
    small_w = [w_ple_proj[0], w_dw[0], ln_pre, ln_post, b_dw, conv_ln_g, conv_ln_b, sinks]
    small_m = [m_w_ple_proj[0], m_w_dw[0], m_ln_pre, m_ln_post, m_b_dw, m_conv_ln_g, m_conv_ln_b, m_sinks]
    small_v = [v_w_ple_proj[0], v_w_dw[0], v_ln_pre, v_ln_post, v_b_dw, v_conv_ln_g, v_conv_ln_b, v_sinks]
    small = _adamw_small(small_g, small_w, small_m, small_v)

    loss = tot[37, N_HEADS]
    grads = [g_w_in, small_g[2], small_g[3], g_dw[None], small_g[4], small_g[5], small_g[6],
             sq_res[0][0][None], small_g[7], sq_res[1][0][None], sq_res[2][0][None], sq_res[3][0][None],
             sq_res[4][0][None], g_wpp[None]]

    def triple(i):
        w_in_t = (d_w_in[None], nm_w_in[None], nv_w_in[None])
        sq = lambda k: tuple(a[None] for a in sq_res[k][1:4])
        sm = lambda k, lead: tuple(a[None] if lead else a for a in small[k])
        return [w_in_t[i], sm(2, False)[i], sm(3, False)[i], sm(1, True)[i], sm(4, False)[i],
                sm(5, False)[i], sm(6, False)[i], sq(0)[i], sm(7, False)[i], sq(1)[i], sq(2)[i], sq(3)[i],
                sq(4)[i], sm(0, True)[i]]

    return (loss, gx.reshape(nb, S, D), *grads, *triple(0), *triple(1), *triple(2))
```

```python
import functools

import jax
import jax.numpy as jnp
import numpy as np
from jax import lax
from jax.experimental import pallas as pl
from jax.experimental.pallas import tpu as pltpu

F32 = jnp.float32
BF16 = jnp.bfloat16

D = 1024
PLE = 256
N_HEADS = 16
HEAD_DIM = 64
BLOCK = 128
CONV_K = 31
ROPE_DIM = 16
ROPE_THETA = 500000.0
EPS = 1e-6
IN_WIDTH = 7424
N_SHARDS = 4

ADAM_LR = 0.001
ADAM_B1 = 0.9
ADAM_B2 = 0.999
ADAM_EPS = 1e-08
ADAM_WD = 0.01
ADAM_STEP = 10

SQ_NAMES = ("w_pw", "w_br_conv", "w_br_attn", "w_out", "w_ple_gate")
WT0 = 5 * D
WPP0 = WT0 + IN_WIDTH
WALL_ROWS = WPP0 + PLE
WIN_SHARD = IN_WIDTH // N_SHARDS
SQ_SHARD = D // N_SHARDS
WPP_SHARD = PLE * PLE // D
PACK_ROWS = WIN_SHARD + 5 * SQ_SHARD + WPP_SHARD
HALF_ROWS = PACK_ROWS // 2
VMEM_LIMIT = 56 * 1024 * 1024
MESH = pl.DeviceIdType.MESH
TILE_RESIDENT = 512
TILE_PROJ = 1024
TILE_TOKEN = 256
TILE_CONV = 512
TILE_ATTN = 512
TAIL_PARTS = 1


ZB_AGATE, ZB_GCONV, ZB_GATTN, ZB_CGATE, ZB_CVAL, ZB_CGLU, ZB_Q = range(7)
ZKV = 7 * D
_SEGMENTS = ((0, D, ZB_CVAL * D), (D, D, ZB_CGLU * D), (2 * D, D, ZB_CGATE * D), (3 * D, D, ZB_Q * D),
             (4 * D, 2 * BLOCK, ZKV), (4 * D + 2 * BLOCK, D, ZB_AGATE * D),
             (5 * D + 2 * BLOCK, D, ZB_GCONV * D), (6 * D + 2 * BLOCK, D, ZB_GATTN * D))
_WT_CUTS = (0, 192, 640, 1216, WIN_SHARD)


def _zp_row(o):
    for a, w, zp in _SEGMENTS:
        if a <= o < a + w:
            return zp + o - a
    raise ValueError(o)


def _pieces(s):
    out = []
    for a, b in zip(_WT_CUTS[:-1], _WT_CUTS[1:]):
        first = _zp_row(WIN_SHARD * s + a)
        assert _zp_row(WIN_SHARD * s + b - 1) == first + b - a - 1
        out.append((a, b - a, WT0 + first))
    for k in range(5):
        out.append((WIN_SHARD + SQ_SHARD * k, SQ_SHARD, D * k + SQ_SHARD * s))
    out.append((WIN_SHARD + 5 * SQ_SHARD, WPP_SHARD, WPP0 + WPP_SHARD * s))
    return out


N_PIECES = len(_pieces(0))


def _wall_segments(wall0, rows):
    out = []
    for s in range(N_SHARDS):
        for pr, n, wr in _pieces(s):
            lo, hi = max(wr, wall0), min(wr + n, wall0 + rows)
            if lo < hi:
                out.append((lo - wall0, hi - lo, s, pr + lo - wr))
    assert sum(n for _, n, _, _ in out) == rows
    return out


def _sel(s, vals):
    r = jnp.int32(vals[0])
    for i in range(1, len(vals)):
        r = jnp.where(s == i, jnp.int32(vals[i]), r)
    return r


def _sig(x):
    return 1.0 / (1.0 + jnp.exp(-x))


def _mm(a, b):
    return lax.dot_general(a, b, (((1,), (0,)), ((), ())), preferred_element_type=F32)


def _mm_nt(a, b):
    return lax.dot_general(a, b, (((1,), (1,)), ((), ())), preferred_element_type=F32)


def _mm_tn(a, b):
    return lax.dot_general(a, b, (((0,), (0,)), ((), ())), preferred_element_type=F32)


def _params(sem=None):
    return pltpu.CompilerParams(dimension_semantics=sem, vmem_limit_bytes=VMEM_LIMIT)


def _flush_to_pack(acc_ref, gpack_ref, wall0, sem):
    for cp in _pack_copies(acc_ref, gpack_ref, wall0, sem):
        cp.start()
        cp.wait()


def _flush_all(items, gpack_ref):
    for acc_ref, wall0, sem in items:
        for cp in _pack_copies(acc_ref, gpack_ref, wall0, sem):
            cp.start()
    for acc_ref, _, sem in items:
        pltpu.make_async_copy(acc_ref, gpack_ref.at[0, pl.ds(0, acc_ref.shape[0])], sem).wait()


def _pack_copies(acc_ref, gpack_ref, wall0, sem):
    base = 0 if gpack_ref.shape[1] == WIN_SHARD else WIN_SHARD
    out = []
    for r, n, s, pr in _wall_segments(wall0, acc_ref.shape[0]):
        assert 0 <= pr - base and pr - base + n <= gpack_ref.shape[1]
        out.append(pltpu.make_async_copy(acc_ref.at[pl.ds(r, n)], gpack_ref.at[s, pl.ds(pr - base, n)], sem))
    return out


def _coords():
    return lax.axis_index("x"), lax.axis_index("y"), lax.axis_index("c")


def _chip_peers(x, y):
    return [(1 - x, y), (x, 1 - y), (1 - x, 1 - y)]


WIN_PIECES = tuple(range(len(_WT_CUTS) - 1))
SQ_PIECES = tuple(range(len(WIN_PIECES), N_PIECES))


def _gather_ops(group, src, landing, bytes_ref, stage, send_sems, recv_sems, loc_sem):
    sizes = [_pieces(0)[p][1] for p in group]
    half_rows = sum(n // 2 for n in sizes)
    starts = [sum(sizes[:i]) for i in range(len(sizes))]

    def rcopy(a, b, k, dev):
        return pltpu.make_async_remote_copy(src_ref=a, dst_ref=b, send_sem=send_sems.at[k],
                                            recv_sem=recv_sems.at[k], device_id=dev, device_id_type=MESH)

    def total(k):
        x, y, c = _coords()
        rows = bytes_ref.at[pl.ds(0, half_rows)]
        return rcopy(rows, rows, k, (x, y, c))

    def own_total():
        rows = stage.at[pl.ds(0, sum(sizes))]
        return pltpu.make_async_copy(rows, rows, loc_sem)

    def send():
        x, y, c = _coords()
        s_me = 2 * x + y
        for k, (px, py) in enumerate(_chip_peers(x, y)):
            for p, n in zip(group, sizes):
                h = n // 2
                rcopy(src(p, c * h, h), landing(p, s_me, c * h, h), k, (px, py, c)).start()
        for p, n, r in zip(group, sizes, starts):
            pltpu.make_async_copy(src(p, 0, n), stage.at[pl.ds(r, n)], loc_sem).start()

    def forward():
        x, y, c = _coords()
        own_total().wait()
        for p, n, r in zip(group, sizes, starts):
            pltpu.make_async_copy(stage.at[pl.ds(r, n)], landing(p, 2 * x + y, 0, n), loc_sem).start()
        for k, (px, py) in enumerate(_chip_peers(x, y)):
            total(k).wait_recv()
            for p, n in zip(group, sizes):
                rows = landing(p, 2 * px + py, c * (n // 2), n // 2)
                rcopy(rows, rows, 3 + k, (x, y, 1 - c)).start()

    def finish():
        own_total().wait()
        for k in range(3):
            total(3 + k).wait_recv()
        for k in range(6):
            total(k).wait_send()

    return send, forward, finish


def _piece_rows(ref, start, off, n):
    first = start + off
    return ref.at[pl.ds(first if isinstance(first, int) else pl.multiple_of(first, 32), n)]


GROUP_CONV = SQ_PIECES[0:2]
GROUP_TAIL = SQ_PIECES[2:]


def _group_shapes(group):
    n_sq = sum(1 for q in group if q != N_PIECES - 1)
    return [jax.ShapeDtypeStruct((n_sq * D, D), BF16)] + (
        [jax.ShapeDtypeStruct((PLE, D), BF16)] if N_PIECES - 1 in group else [])


def _group_scratch(group):
    return [pltpu.VMEM((sum(_pieces(0)[q][1] for q in group), D), BF16), pltpu.SemaphoreType.DMA((6,)),
            pltpu.SemaphoreType.DMA((6,)), pltpu.SemaphoreType.DMA]


def _group_gather(group, shard_refs, out_refs, scratch, step, n_steps):
    wall_ref = out_refs[0]
    stage, send_sems, recv_sems, loc_sem = scratch

    def src(q, off, n):
        return _piece_rows(shard_refs[group.index(q)], 0, off, n)

    def landing(q, s, off, n):
        if q == N_PIECES - 1:
            return _piece_rows(out_refs[1], WPP_SHARD * s, off, n)
        return _piece_rows(wall_ref, D * group.index(q) + SQ_SHARD * s, off, n)

    send, forward, finish = _gather_ops(group, src, landing, wall_ref, stage, send_sems, recv_sems, loc_sem)
    pl.when(step == 0)(send)
    pl.when(step == n_steps // 2)(forward)
    return finish


ROPE_ROWS = 16


def _rope_tables(pos, freq, spread_ref):
    def to_lanes(v, e):
        out = None
        for _ in range(3):
            part = v.astype(BF16)
            term = _mm_tn(part, e)
            out = term if out is None else out + term
            v = v - part.astype(F32)
        return out

    ang = freq * pos
    return (to_lanes(jnp.cos(ang), spread_ref[0]) + spread_ref[2, 0:1, :].astype(F32),
            to_lanes(jnp.sin(ang), spread_ref[1]))


SQ_PACK = PACK_ROWS - WIN_SHARD


def _row_tile(half):
    return max(t for t in range(8, 321, 8) if half % t == 0)


def _exchange_copies(g_ref, r1_ref, send_sems, recv_sems):
    x, y, c = _coords()
    half = g_ref.shape[1] // 2
    return [pltpu.make_async_remote_copy(
        src_ref=g_ref.at[:, pl.ds(pl.multiple_of((1 - c) * half, 32), half), :], dst_ref=r1_ref,
        send_sem=send_sems.at[0], recv_sem=recv_sems.at[0], device_id=(x, y, 1 - c), device_id_type=MESH)]


def _chip_sum_copies(cs_ref, r2_ref, send_sems, recv_sems):
    x, y, c = _coords()
    return [pltpu.make_async_remote_copy(
        src_ref=cs_ref.at[2 * px + py], dst_ref=r2_ref.at[k], send_sem=send_sems.at[k],
        recv_sem=recv_sems.at[k], device_id=(px, py, c), device_id_type=MESH)
        for k, (px, py) in enumerate(_chip_peers(x, y))]


def _chip_sum(cidx, gpack, r1, name):
    half = gpack.shape[1] // 2
    rt = _row_tile(half)

    def body(c_ref, g_ref, r_ref, o_ref):
        o_ref[...] = (g_ref[...] + r_ref[...]).astype(BF16)

    nt = half // rt
    return pl.pallas_call(
        body, name=name,
        grid_spec=pltpu.PrefetchScalarGridSpec(
            num_scalar_prefetch=1, grid=(N_SHARDS, nt),
            in_specs=[pl.BlockSpec((1, rt, D), lambda s, t, c: (s, c[0] * nt + t, 0)),
                      pl.BlockSpec((1, rt, D), lambda s, t, c: (s, t, 0))],
            out_specs=pl.BlockSpec((1, rt, D), lambda s, t, c: (s, t, 0))),
        out_shape=jax.ShapeDtypeStruct((N_SHARDS, half, D), BF16),
        compiler_params=_params(("arbitrary", "arbitrary")),
    )(cidx, gpack, r1)


def _final_half(sc, gpack, r1, r2, name):
    rows = gpack.shape[1]
    half = rows // 2
    rt = _row_tile(half)

    def body(sc_ref, g_ref, r_ref, p_ref, o_ref):
        acc = g_ref[0] + r_ref[0]
        for k in range(3):
            acc = acc + p_ref[k].astype(F32)
        o_ref[...] = acc

    nt = half // rt
    return pl.pallas_call(
        body, name=name,
        grid_spec=pltpu.PrefetchScalarGridSpec(
            num_scalar_prefetch=1, grid=(nt,),
            in_specs=[pl.BlockSpec((1, rt, D), lambda t, sc: (sc[0], sc[1] * nt + t, 0)),
                      pl.BlockSpec((1, rt, D), lambda t, sc: (sc[0], t, 0)),
                      pl.BlockSpec((3, rt, D), lambda t, sc: (0, t, 0))],
            out_specs=pl.BlockSpec((rt, D), lambda t, sc: (sc[1] * nt + t, 0))),
        out_shape=jax.ShapeDtypeStruct((rows, D), F32),
        compiler_params=_params(("arbitrary",)),
    )(sc, gpack, r1, r2)


VEC_ROWS = 40


def _finish_reduce(fwt, fsq, vec):
    def body(fwt_ref, fsq_ref, v_ref, owt_ref, osq_ref, tot_ref, buf, send_sems, recv_sems):
        x, y, c = _coords()
        swaps = []
        for k, (f_ref, o_ref) in enumerate(((fwt_ref, owt_ref), (fsq_ref, osq_ref))):
            half = f_ref.shape[0] // 2
            rows = pl.ds(pl.multiple_of(c * half, 32), half)
            swaps.append(pltpu.make_async_remote_copy(
                src_ref=f_ref.at[rows], dst_ref=o_ref.at[rows], send_sem=send_sems.at[7 + k],
                recv_sem=recv_sems.at[7 + k], device_id=(x, y, 1 - c), device_id_type=MESH))
        for cp in swaps:
            cp.start()
        me = 4 * x + 2 * y + c
        buf[me] = v_ref[...]
        cps = []
        for r in range(1, 8):
            dx, dy, dc = (r >> 2) & 1, (r >> 1) & 1, r & 1
            peer = (1 - x if dx else x, 1 - y if dy else y, 1 - c if dc else c)
            cp = pltpu.make_async_remote_copy(
                src_ref=v_ref, dst_ref=buf.at[me], send_sem=send_sems.at[r - 1],
                recv_sem=recv_sems.at[r - 1], device_id=peer, device_id_type=MESH)
            cp.start()
            cps.append(cp)
        for cp in cps:
            cp.wait_recv()
        for cp in cps:
            cp.wait_send()
        acc = buf[0]
        for d in range(1, 8):
            acc = acc + buf[d]
        tot_ref[...] = acc
        for cp in swaps:
            cp.wait()

    any_spec = pl.BlockSpec(memory_space=pl.ANY)
    vm = pl.BlockSpec(memory_space=pltpu.VMEM)
    return pl.pallas_call(
        body, name="finish_reduce",
        out_shape=(jax.ShapeDtypeStruct(fwt.shape, F32), jax.ShapeDtypeStruct(fsq.shape, F32),
                   jax.ShapeDtypeStruct((VEC_ROWS, D), F32)),
        in_specs=[any_spec, any_spec, vm], out_specs=(any_spec, any_spec, vm),
        input_output_aliases={0: 0, 1: 1},
        scratch_shapes=[pltpu.VMEM((8, VEC_ROWS, D), F32), pltpu.SemaphoreType.DMA((9,)),
                        pltpu.SemaphoreType.DMA((9,))],
    )(fwt, fsq, vec)


SOLO_ROWS = WIN_SHARD - BLOCK // 2


def _solo_first(s):
    return 0 if s % 2 == 0 else BLOCK // 2


def _solo_segments(s):
    lo = _solo_first(s)
    out = []
    for a, n, wr in _pieces(s)[:len(WIN_PIECES)]:
        b0, b1 = max(a, lo), min(a + n, lo + SOLO_ROWS)
        if b0 >= b1:
            continue
        z0 = wr - WT0 + b0 - a
        if out and out[-1][0] + out[-1][1] == b0 - lo and out[-1][2] + out[-1][1] == z0:
            out[-1] = (out[-1][0], out[-1][1] + b1 - b0, out[-1][2])
        else:
            out.append((b0 - lo, b1 - b0, z0))
    out = [r for o, n, z0 in out for r in
           (((o, ZKV - z0, z0), (o + ZKV - z0, z0 + n - ZKV, ZKV)) if z0 < ZKV < z0 + n else ((o, n, z0),))]
    assert all(v % BLOCK == 0 for seg in out for v in seg) and sum(n for _, n, _ in out) == SOLO_ROWS
    return out


def _shared_tile(pair):
    z0 = _zp_row(WIN_SHARD * (2 * pair) + SOLO_ROWS)
    assert z0 % BLOCK == 0 and _zp_row(WIN_SHARD * (2 * pair + 1)) == z0 + BLOCK // 2
    return z0


def _inproj(x, ln_pre, pos, freq, spread, win_t, wdw_shard, tm):
    T = x.shape[0]
    n_t = T // tm
    assert n_t >= 2 and n_t % 2 == 0
    tables = [[_pieces(s)[p][2] - WT0 for s in range(N_SHARDS)] for p in WIN_PIECES]
    sizes = [_pieces(0)[p][1] for p in WIN_PIECES]
    half_rows = sum(n // 2 for n in sizes)
    relation_of_pass = {1: 1, 2: 0, 3: 2}

    def body(x_ref, g_ref, pos_ref, f_ref, e_ref, win_ref, wdw_ref, z_ref, zkv_ref, wt_ref, wdwall_ref, h_ref,
             cos_ref, sin_ref, wbuf, stage, stage_sh, hbuf, wsend, wrecv, loc_sems, out_sems, sh_sems, h_sems):
        p = pl.program_id(0)
        t = pl.program_id(1)
        x, y, c = _coords()
        s_me = 2 * x + y
        peers = _chip_peers(x, y)
        shard = jnp.bitwise_xor(s_me, p)
        first, last = t == 0, t == n_t - 1

        def rcopy(a, b, k, dev):
            return pltpu.make_async_remote_copy(src_ref=a, dst_ref=b, send_sem=wsend.at[k], recv_sem=wrecv.at[k],
                                                device_id=dev, device_id_type=MESH)

        def total(k):
            rows = wt_ref.at[pl.ds(0, half_rows)]
            return rcopy(rows, rows, k, (x, y, c))

        def in_hbm(q, s, off, n):
            return _piece_rows(wt_ref, _sel(s, tables[q]), off, n)

        def in_vmem(q, s):
            return _piece_rows(wbuf, WIN_SHARD * s + _WT_CUTS[q], 0, sizes[q])

        def send_to(k):
            px, py = peers[k]
            for q, n in zip(WIN_PIECES, sizes):
                rcopy(_piece_rows(win_ref, _WT_CUTS[q], c * (n // 2), n // 2), in_hbm(q, s_me, c * (n // 2), n // 2),
                      k, (px, py, c)).start()

        def forward_from(k):
            px, py = peers[k]
            total(k).wait_recv()
            for q, n in zip(WIN_PIECES, sizes):
                rows = in_hbm(q, 2 * px + py, c * (n // 2), n // 2)
                rcopy(rows, rows, 3 + k, (x, y, 1 - c)).start()

        def shard_total(a, b, sem):
            return pltpu.make_async_copy(a.at[pl.ds(0, WIN_SHARD)], b.at[pl.ds(0, WIN_SHARD)], sem)

        def wdw_copies():
            return [pltpu.make_async_remote_copy(
                src_ref=wdw_ref, dst_ref=wdwall_ref.at[s_me], send_sem=wsend.at[6 + k], recv_sem=wrecv.at[6 + k],
                device_id=(px, py, c), device_id_type=MESH) for k, (px, py) in enumerate(peers)]

        def own_wdw():
            return pltpu.make_async_copy(wdw_ref, wdwall_ref.at[s_me], loc_sems.at[2])

        @pl.when((p == 0) & first)
        def _():
            send_to(0)
            send_to(1)
            own_wdw().start()
            for cp in wdw_copies():
                cp.start()
            for q in WIN_PIECES:
                pltpu.make_async_copy(_piece_rows(win_ref, _WT_CUTS[q], 0, sizes[q]), in_vmem(q, s_me),
                                      loc_sems.at[0]).start()

        for pp, k in relation_of_pass.items():
            pl.when((p == pp - 1) & (t == n_t - 2))(functools.partial(forward_from, k))

            @pl.when((p == pp - 1) & last)
            def _(k=k):
                total(3 + k).wait_recv()
                px, py = peers[k]
                for q in WIN_PIECES:
                    pltpu.make_async_copy(in_hbm(q, 2 * px + py, 0, sizes[q]), in_vmem(q, 2 * px + py),
                                          loc_sems.at[0]).start()

            @pl.when((p == pp) & first)
            def _(pp=pp):
                shard_total(wt_ref, wbuf, loc_sems.at[0]).wait()
                if pp == 1:
                    total(0).wait_send()
                    total(1).wait_send()
                    send_to(2)

        step = p * n_t + t
        slot = step % 2
        rows = pl.ds(pl.multiple_of(t * tm, tm), tm)

        def out_total(sl):
            return pltpu.make_async_copy(stage.at[sl], stage.at[sl], out_sems.at[sl])

        def sh_copy(sl, z0):
            return pltpu.make_async_copy(stage_sh.at[sl], z_ref.at[rows, pl.ds(z0, BLOCK)], sh_sems.at[sl])

        @pl.when(step >= 2)
        def _():
            out_total(slot).wait()

        @pl.when((step >= 2) & (((step - 2) // n_t) % 2 == 1))
        def _():
            sh_copy(slot, 0).wait()

        def h_out(sl):
            return pltpu.make_async_copy(hbuf.at[sl], h_ref.at[rows], h_sems.at[sl])

        def h_in(sl, tile):
            return pltpu.make_async_copy(h_ref.at[pl.ds(pl.multiple_of(tile * tm, tm), tm)], hbuf.at[sl],
                                         h_sems.at[sl])

        @pl.when((p == 0) & (t >= 2))
        def _():
            h_out(slot).wait()

        @pl.when(p == 0)
        def _():
            xv = x_ref[...]
            r = lax.rsqrt(jnp.mean(xv * xv, axis=-1, keepdims=True) + EPS)
            hbuf[slot] = (xv * r * g_ref[...]).astype(BF16)
            h_out(slot).start()
            cos, sin = _rope_tables(pos_ref[...], f_ref[...], e_ref)
            cos_ref[...] = cos
            sin_ref[...] = sin

        @pl.when((p == 1) & first)
        def _():
            h_out(0).wait()
            h_out(1).wait()
            h_in(0, 0).start()

        @pl.when(p >= 1)
        def _():
            h_in(slot, t).wait()

        @pl.when((p >= 1) & (step < N_SHARDS * n_t - 1))
        def _():
            h_in(1 - slot, jnp.where(last, 0, t + 1)).start()

        @pl.when((p == 0) & first)
        def _():
            shard_total(win_ref, wbuf, loc_sems.at[0]).wait()
            for q in WIN_PIECES:
                pltpu.make_async_copy(in_vmem(q, s_me), in_hbm(q, s_me, 0, sizes[q]), loc_sems.at[1]).start()

        solo0 = pl.multiple_of(WIN_SHARD * shard + (BLOCK // 2) * (shard % 2), BLOCK // 2)
        stage[slot] = _mm_nt(hbuf[slot], wbuf[pl.ds(solo0, SOLO_ROWS), :]).astype(BF16)
        for s in range(N_SHARDS):
            @pl.when(shard == s)
            def _(s=s):
                for off, n, z0 in _solo_segments(s):
                    dst = zkv_ref.at[rows] if z0 == ZKV else z_ref.at[rows, pl.ds(z0, n)]
                    pltpu.make_async_copy(stage.at[slot, :, pl.ds(off, n)], dst, out_sems.at[slot]).start()

        @pl.when(p % 2 == 1)
        def _():
            pair = shard // 2
            w0 = pl.multiple_of(2 * WIN_SHARD * pair + SOLO_ROWS, BLOCK // 2)
            z0 = pl.multiple_of(jnp.where(pair == 0, _shared_tile(0), _shared_tile(1)), BLOCK)
            stage_sh[slot] = _mm_nt(hbuf[slot], wbuf[pl.ds(w0, BLOCK), :]).astype(BF16)
            sh_copy(slot, z0).start()

        @pl.when((p == 3) & last)
        def _():
            for k in (2, 3, 4, 5):
                total(k).wait_send()
            shard_total(wbuf, wt_ref, loc_sems.at[1]).wait()
            cps = wdw_copies()
            for cp in cps:
                cp.wait_recv()
            for cp in cps:
                cp.wait_send()
            own_wdw().wait()
            for sl in range(2):
                out_total(sl).wait()
                sh_copy(sl, 0).wait()

    def in_pass0(p, t):
        return jnp.where(p == 0, t, n_t - 1)

    any_spec = pl.BlockSpec(memory_space=pl.ANY)
    return pl.pallas_call(
        body, name="inproj", grid=(N_SHARDS, n_t),
        in_specs=[pl.BlockSpec((tm, D), lambda p, t: (in_pass0(p, t), 0)), pl.BlockSpec((1, D), lambda p, t: (0, 0)),
                  pl.BlockSpec((1, tm), lambda p, t: (0, in_pass0(p, t))),
                  pl.BlockSpec((ROPE_ROWS, 1), lambda p, t: (0, 0)),
                  pl.BlockSpec((3, ROPE_ROWS, BLOCK), lambda p, t: (0, 0, 0)), any_spec, any_spec],
        out_specs=(any_spec,) * 5 + (pl.BlockSpec((tm, BLOCK), lambda p, t: (in_pass0(p, t), 0)),) * 2,
        out_shape=(jax.ShapeDtypeStruct((T, ZKV), BF16), jax.ShapeDtypeStruct((T, 2 * BLOCK), BF16),
                   jax.ShapeDtypeStruct((IN_WIDTH, D), BF16), jax.ShapeDtypeStruct((N_SHARDS, 32, PLE), F32),
                   jax.ShapeDtypeStruct((T, D), BF16), jax.ShapeDtypeStruct((T, BLOCK), F32),
                   jax.ShapeDtypeStruct((T, BLOCK), F32)),
        scratch_shapes=[pltpu.VMEM((IN_WIDTH, D), BF16), pltpu.VMEM((2, tm, SOLO_ROWS), BF16),
                        pltpu.VMEM((2, tm, BLOCK), BF16), pltpu.VMEM((2, tm, D), BF16),
                        pltpu.SemaphoreType.DMA((9,)), pltpu.SemaphoreType.DMA((9,)),
                        pltpu.SemaphoreType.DMA((3,)), pltpu.SemaphoreType.DMA((2,)),
                        pltpu.SemaphoreType.DMA((2,)), pltpu.SemaphoreType.DMA((2,))],
        compiler_params=_params(("arbitrary", "arbitrary")),
    )(x, ln_pre, pos, freq, spread, win_t, wdw_shard)


HALO = 32
CONV_RC = 64
CONV_LC = 256


def _conv_taps(w_ref, src, r0, lane0, offset_of_tap):
    lanes = pl.ds(lane0, CONV_LC)
    out = None
    for b in range(8):
        taps = [k for k in range(CONV_K) if offset_of_tap(k) % 8 == b]
        if not taps:
            continue
        rows = CONV_RC + (8 if b else 0)
        vb = None
        for k in taps:
            term = w_ref[k:k + 1, lanes] * src[pl.ds(r0 + (offset_of_tap(k) - b), rows), lanes]
            vb = term if vb is None else vb + term
        vb = vb[b:b + CONV_RC] if b else vb
        out = vb if out is None else out + vb
    return out


def _conv_fwd(z, wdw, b_dw, ln_g, ln_b, wall, S, tm, group, shards):
    T = z.shape[0]
    nt = S // tm
    hb = tm // HALO
    gathered = _group_shapes(group)

    def body(cv_ref, cg_ref, cgate_ref, hcv_ref, hcg_ref, wdw_ref, bdw_ref, lng_ref, lnb_ref, wpw_ref,
             wbrc_ref, *rest):
        shard_refs, rest = rest[:len(group)], rest[len(group):]
        ya_ref, y_ref, rstd_ref, pw_ref = rest[:4]
        gather_refs, (ubuf, cbuf), gather_scratch = rest[4:4 + len(gathered)], rest[-6:-4], rest[-4:]
        t = pl.program_id(1)
        step = pl.program_id(0) * nt + t
        finish_gather = _group_gather(group, shard_refs, gather_refs, gather_scratch, step, T // tm)
        ubuf[HALO:HALO + tm, :] = cv_ref[...].astype(F32) * _sig(cg_ref[...].astype(F32))
        hu = hcv_ref[...].astype(F32) * _sig(hcg_ref[...].astype(F32))
        ubuf[0:HALO, :] = jnp.where(t > 0, hu, 0.0)
        ubuf[HALO + tm:HALO + tm + 8, :] = jnp.zeros((8, D), F32)

        def chunk(ci, carry):
            r0 = pl.multiple_of(ci * CONV_RC, CONV_RC)
            for lg in range(D // CONV_LC):
                acc = _conv_taps(wdw_ref, ubuf, r0, lg * CONV_LC, lambda k: HALO - (CONV_K - 1) + k)
                cbuf[pl.ds(r0, CONV_RC), pl.ds(lg * CONV_LC, CONV_LC)] = acc
            return carry

        lax.fori_loop(0, tm // CONV_RC, chunk, 0)
        cc = cbuf[...] + bdw_ref[...]
        mu = jnp.mean(cc, axis=-1, keepdims=True)
        dd = cc - mu
        rstd = lax.rsqrt(jnp.mean(dd * dd, axis=-1, keepdims=True) + EPS)
        yn = dd * rstd
        y_ref[...] = yn.astype(BF16)
        rstd_ref[...] = rstd
        n = yn * lng_ref[...] + lnb_ref[...]
        s = n * _sig(n)
        pw = _mm(s.astype(BF16), wpw_ref[...])
        pw_ref[...] = pw.astype(BF16)
        gt = cgate_ref[...].astype(F32)
        ya_in = pw * (gt * _sig(gt))
        ya_ref[...] = _mm(ya_in.astype(BF16), wbrc_ref[...]).astype(BF16)
        pl.when(step == T // tm - 1)(finish_gather)

    def row(b, t):
        return b * nt + t

    def halo(b, t):
        return jnp.maximum(row(b, t) * hb - 1, 0)

    vec = pl.BlockSpec((1, D), lambda b, t: (0, 0))
    tile = lambda j: pl.BlockSpec((tm, D), lambda b, t: (row(b, t), j))
    out_tile = pl.BlockSpec((tm, D), lambda b, t: (row(b, t), 0))
    any_spec = pl.BlockSpec(memory_space=pl.ANY)
    return pl.pallas_call(
        body, name="conv_fwd", grid=(T // S, nt),
        in_specs=[tile(ZB_CVAL), tile(ZB_CGLU), tile(ZB_CGATE),
                  pl.BlockSpec((HALO, D), lambda b, t: (halo(b, t), ZB_CVAL)),
                  pl.BlockSpec((HALO, D), lambda b, t: (halo(b, t), ZB_CGLU)),
                  pl.BlockSpec((32, D), lambda b, t: (0, 0)), vec, vec, vec,
                  pl.BlockSpec((D, D), lambda b, t: (0, 0)),
                  pl.BlockSpec((D, D), lambda b, t: (1, 0))] + [any_spec] * len(group),
        out_specs=(out_tile, out_tile, pl.BlockSpec((tm, 1), lambda b, t: (row(b, t), 0)), out_tile)
        + (any_spec,) * len(gathered),
        out_shape=[jax.ShapeDtypeStruct((T, D), BF16), jax.ShapeDtypeStruct((T, D), BF16),
                   jax.ShapeDtypeStruct((T, 1), F32), jax.ShapeDtypeStruct((T, D), BF16)] + gathered,
        scratch_shapes=[pltpu.VMEM((tm + HALO + 8, D), F32), pltpu.VMEM((tm, D), F32)] + _group_scratch(group),
        compiler_params=_params(("arbitrary", "arbitrary")),
    )(z, z, z, z, z, wdw, b_dw, ln_g, ln_b, wall, wall, *shards)


def _swap_matrix():
    r = lax.broadcasted_iota(jnp.int32, (BLOCK, BLOCK), 0)
    l = lax.broadcasted_iota(jnp.int32, (BLOCK, BLOCK), 1)
    lh = l & (HEAD_DIM - 1)
    half = ROPE_DIM // 2
    hit = ((lh < half) & (r == l + half)) | ((lh >= half) & (lh < ROPE_DIM) & (r == l - half))
    return jnp.where(hit, 1.0, 0.0).astype(BF16)


def _rope(tb, cos, sin, pswap):
    return tb.astype(F32) * cos + _mm(tb, pswap) * sin


def _rope_f32(tv, cos, sin, pswap):
    hi = tv.astype(BF16)
    lo = (tv - hi.astype(F32)).astype(BF16)
    return tv * cos + (_mm(hi, pswap) + _mm(lo, pswap)) * sin


def _kv_variants(kv):
    lane = lax.broadcasted_iota(jnp.int32, kv.shape, 1)
    lo = lane < HEAD_DIM
    sw = pltpu.roll(kv, HEAD_DIM, 1)
    z = jnp.zeros_like(kv)
    g0 = (jnp.where(lo, kv, z).astype(BF16), jnp.where(lo, z, sw).astype(BF16))
    g1 = (jnp.where(lo, sw, z).astype(BF16), jnp.where(lo, z, kv).astype(BF16))
    return (g0, g1)


def _band_mask(nq):
    qi = lax.broadcasted_iota(jnp.int32, (nq * BLOCK, 2 * BLOCK), 0) & (BLOCK - 1)
    sj = lax.broadcasted_iota(jnp.int32, (nq * BLOCK, 2 * BLOCK), 1)
    return (sj <= qi + BLOCK) & (sj > qi), sj


def _sink_rep(sink_ref, g, e):
    return jnp.concatenate(
        [jnp.full((BLOCK, BLOCK), sink_ref[8 * g + 2 * j + e], F32) for j in range(4)], axis=0)


def _softmax_parts(s, valid, sk):
    rows = s.shape[0]
    s = jnp.where(valid, s, -1e30)
    m = jnp.maximum(jnp.broadcast_to(jnp.max(s, axis=-1, keepdims=True), (rows, BLOCK)), sk)
    return jnp.exp(s - jnp.concatenate([m, m], axis=1)), jnp.exp(sk - m)


def _softmax_sink(s, valid, sk):
    p, ps = _softmax_parts(s, valid, sk)
    inv = 1.0 / (_mm(p.astype(BF16), jnp.ones((2 * BLOCK, BLOCK), BF16)) + ps)
    return p * jnp.concatenate([inv, inv], axis=1), ps * inv


def _attn_fwd(z, zkv, cos_t, sin_t, sinks, S, tq, group, shards):
    T = z.shape[0]
    nt = S // tq
    nq = tq // BLOCK
    gathered = _group_shapes(group)

    def body(sink_ref, q_ref, kv_ref, hkv_ref, cos_ref, sin_ref, hcos_ref, hsin_ref, *rest):
        shard_refs, o_ref = rest[:len(group)], rest[len(group)]
        t = pl.program_id(1)
        step = pl.program_id(0) * nt + t
        finish_gather = _group_gather(group, shard_refs, rest[len(group) + 1:-4], rest[-4:], step, T // tq)
        cos = cos_ref[...]
        sin = sin_ref[...]
        pswap = _swap_matrix()
        kv = jnp.concatenate([hkv_ref[...], kv_ref[...]], axis=0)
        cos_k = jnp.concatenate([hcos_ref[...], cos], axis=0)
        sin_k = jnp.concatenate([hsin_ref[...], sin], axis=0)
        kx = _kv_variants(_rope(kv[:, :BLOCK], cos_k, sin_k, pswap))
        one = jnp.ones((tq + BLOCK, BLOCK), BF16)
        vx = [[jnp.concatenate([v, one], axis=1) for v in vg] for vg in _kv_variants(kv[:, BLOCK:].astype(F32))]
        band, sj = _band_mask(4)
        qs = [(_rope(q_ref[:, 128 * hp:128 * hp + 128], cos, sin, pswap) * 0.125).astype(BF16)
              for hp in range(8)]
        for n in range(nq):
            first = (t == 0) & (n == 0)
            valid = band & (jnp.logical_not(first) | (sj >= BLOCK))
            r0 = n * BLOCK
            for g in range(2):
                lhs = jnp.concatenate([qs[4 * g + j][r0:r0 + BLOCK] for j in range(4)], axis=0)
                acc = jnp.zeros((4 * BLOCK, BLOCK), F32)
                for e in range(2):
                    s = _mm_nt(lhs, kx[g][e][r0:r0 + 2 * BLOCK])
                    p, ps = _softmax_parts(s, valid, _sink_rep(sink_ref, g, e))
                    r = _mm(p.astype(BF16), vx[g][e][r0:r0 + 2 * BLOCK])
                    acc = acc + r[:, 0:BLOCK] * (1.0 / (r[:, BLOCK:2 * BLOCK] + ps))
                for j in range(4):
                    o_ref[r0:r0 + BLOCK, 128 * (4 * g + j):128 * (4 * g + j) + 128] = (
                        acc[j * BLOCK:(j + 1) * BLOCK].astype(BF16))
        pl.when(step == T // tq - 1)(finish_gather)

    def row(b, t):
        return b * nt + t

    def halo(b, t):
        return jnp.maximum(row(b, t) * nq - 1, 0)

    any_spec = pl.BlockSpec(memory_space=pl.ANY)
    return pl.pallas_call(
        body, name="attn_fwd", grid=(T // S, nt),
        in_specs=[pl.BlockSpec(memory_space=pltpu.SMEM),
                  pl.BlockSpec((tq, D), lambda b, t: (row(b, t), ZB_Q)),
                  pl.BlockSpec((tq, 2 * BLOCK), lambda b, t: (row(b, t), 0)),
                  pl.BlockSpec((BLOCK, 2 * BLOCK), lambda b, t: (halo(b, t), 0)),
                  pl.BlockSpec((tq, BLOCK), lambda b, t: (row(b, t), 0)),
                  pl.BlockSpec((tq, BLOCK), lambda b, t: (row(b, t), 0)),
                  pl.BlockSpec((BLOCK, BLOCK), lambda b, t: (halo(b, t), 0)),
                  pl.BlockSpec((BLOCK, BLOCK), lambda b, t: (halo(b, t), 0))] + [any_spec] * len(group),
        out_specs=(pl.BlockSpec((tq, D), lambda b, t: (row(b, t), 0)),) + (any_spec,) * len(gathered),
        out_shape=[jax.ShapeDtypeStruct((T, D), BF16)] + gathered,
        scratch_shapes=_group_scratch(group),
        compiler_params=_params(("arbitrary", "arbitrary")),
    )(sinks, z, zkv, zkv, cos_t, sin_t, cos_t, sin_t, *shards)


def _tail_a(x, tgt, p, o, ya, z, ln_post, wall_b, wppt, tm):
    T = x.shape[0]
    last = T // tm - 1

    def body(x_ref, tgt_ref, p_ref, o_ref, ya_ref, ag_ref, gc_ref, ga_ref, lnp_ref, wbra_ref, wout_ref,
             wpg_ref, wppt_ref, loss_ref, dx1_ref, dm_ref, yb_ref, glnp_ref, gpack_ref, gwpp_ref,
             acc_out, acc_pg, sem):
        i = pl.program_id(0)

        @pl.when(i == 0)
        def _():
            acc_out[...] = jnp.zeros_like(acc_out)
            acc_pg[...] = jnp.zeros_like(acc_pg)
            gwpp_ref[...] = jnp.zeros_like(gwpp_ref)
            glnp_ref[...] = jnp.zeros_like(glnp_ref)
            loss_ref[...] = jnp.zeros_like(loss_ref)

        ag = ag_ref[...].astype(F32)
        yb_in = (o_ref[...].astype(F32) * (ag * _sig(ag))).astype(BF16)
        yb = _mm(yb_in, wbra_ref[...])
        yb_ref[...] = yb.astype(BF16)
        m = (_sig(gc_ref[...].astype(F32)) * ya_ref[...].astype(F32)
             + _sig(ga_ref[...].astype(F32)) * yb).astype(BF16)
        mo = _mm(m, wout_ref[...])
        r2 = lax.rsqrt(jnp.mean(mo * mo, axis=-1, keepdims=True) + EPS)
        nrm = mo * r2
        g_post = lnp_ref[...]
        x1 = x_ref[...] + nrm * g_post
        x1b = x1.astype(BF16)
        gate = _sig(_mm(x1b, wpg_ref[...]))
        pb = p_ref[...].astype(BF16)
        pp = _mm_nt(pb, wppt_ref[...])
        err = x1 + gate * pp - tgt_ref[...]
        loss_ref[...] += 0.5 * jnp.sum(jnp.sum(err * err, axis=-1, keepdims=True) * (1.0 / D),
                                       axis=0, keepdims=True)
        dx2 = err * (1.0 / D)
        dgp = (dx2 * pp * gate * (1.0 - gate)).astype(BF16)
        dpp = (dx2 * gate).astype(BF16)
        dx1 = dx2 + _mm_nt(dgp, wpg_ref[...])
        dx1_ref[...] = dx1
        acc_pg[...] += _mm_tn(x1b, dgp)
        gwpp_ref[...] += _mm_tn(dpp, pb)
        glnp_ref[...] += jnp.sum(dx1 * nrm, axis=0, keepdims=True)
        a = dx1 * g_post
        dmo = (r2 * (a - nrm * jnp.mean(a * nrm, axis=-1, keepdims=True))).astype(BF16)
        dm_ref[...] = _mm_nt(dmo, wout_ref[...]).astype(BF16)
        acc_out[...] += _mm_tn(m, dmo)

        @pl.when(i == last)
        def _():
            _flush_all([(acc_out, 3 * D, sem.at[0]), (acc_pg, 4 * D, sem.at[1])], gpack_ref)

    tile = pl.BlockSpec((tm, D), lambda i: (i, 0))
    ztile = lambda j: pl.BlockSpec((tm, D), lambda i: (i, j))
    wsq = lambda k: pl.BlockSpec((D, D), lambda i: (k, 0))
    const = lambda shp: pl.BlockSpec(shp, lambda i: (0, 0))
    any_spec = pl.BlockSpec(memory_space=pl.ANY)
    return pl.pallas_call(
        body, name="tail_a", grid=(T // tm,),
        in_specs=[tile, tile, pl.BlockSpec((tm, PLE), lambda i: (i, 0)), tile, tile, ztile(ZB_AGATE),
                  ztile(ZB_GCONV), ztile(ZB_GATTN), const((1, D)), wsq(0), wsq(1), wsq(2), const((D, PLE))],
        out_specs=(const((1, 1)), tile, tile, tile, const((1, D)), any_spec, const((D, PLE))),
        out_shape=(jax.ShapeDtypeStruct((1, 1), F32), jax.ShapeDtypeStruct((T, D), F32),
                   jax.ShapeDtypeStruct((T, D), BF16), jax.ShapeDtypeStruct((T, D), BF16),
                   jax.ShapeDtypeStruct((1, D), F32), jax.ShapeDtypeStruct((N_SHARDS, SQ_PACK, D), F32),
                   jax.ShapeDtypeStruct((D, PLE), F32)),
        scratch_shapes=[pltpu.VMEM((D, D), F32), pltpu.VMEM((D, D), F32), pltpu.SemaphoreType.DMA((2,))],
        compiler_params=_params(("arbitrary",)),
    )(x, tgt, p, o, ya, z, z, z, ln_post, wall_b, wall_b, wall_b, wppt)


def _dsilu(v, sg):
    return sg * (1.0 + v * (1.0 - sg))


def _tail_b(dm, ya, yb, o, z, pw, y, rstd, ln_g, ln_b, wall_a, wall_b, gppt, gpack, tm):
    T = dm.shape[0]
    last = T // tm - 1

    def body(dm_ref, ya_ref, yb_ref, o_ref, ag_ref, gc_ref, ga_ref, cgate_ref, pw_ref, y_ref, rstd_ref,
             lng_ref, lnb_ref, wpw_ref, wbrc_ref, wbra_ref, gppt_ref, gpack_in, dg_ref, do_ref, dc_ref,
             gvec_ref, gpack_ref, acc_bra, acc_brc, acc_pw, sem):
        i = pl.program_id(0)

        @pl.when(i == 0)
        def _():
            acc_bra[...] = jnp.zeros_like(acc_bra)
            acc_brc[...] = jnp.zeros_like(acc_brc)
            acc_pw[...] = jnp.zeros_like(acc_pw)
            gvec_ref[...] = jnp.zeros_like(gvec_ref)

        g = lng_ref[...]

        def part(rs):
            dm_v = dm_ref[rs, :].astype(F32)
            sgc = _sig(gc_ref[rs, :].astype(F32))
            sga = _sig(ga_ref[rs, :].astype(F32))
            dya = (dm_v * sgc).astype(BF16)
            dyb = (dm_v * sga).astype(BF16)
            dg_ref[rs, D:2 * D] = (dm_v * ya_ref[rs, :].astype(F32) * sgc * (1.0 - sgc)).astype(BF16)
            dg_ref[rs, 2 * D:3 * D] = (dm_v * yb_ref[rs, :].astype(F32) * sga * (1.0 - sga)).astype(BF16)
            ag = ag_ref[rs, :].astype(F32)
            sag = _sig(ag)
            sa = ag * sag
            ov = o_ref[rs, :].astype(F32)
            dyb_in = _mm_nt(dyb, wbra_ref[...])
            do_ref[rs, :] = (dyb_in * sa).astype(BF16)
            dg_ref[rs, 0:D] = (dyb_in * ov * _dsilu(ag, sag)).astype(BF16)
            gt = cgate_ref[rs, :].astype(F32)
            sgt = _sig(gt)
            sgate = gt * sgt
            pw = pw_ref[rs, :].astype(F32)
            dya_in = _mm_nt(dya, wbrc_ref[...])
            dpw = (dya_in * sgate).astype(BF16)
            dg_ref[rs, 3 * D:4 * D] = (dya_in * pw * _dsilu(gt, sgt)).astype(BF16)
            yn = y_ref[rs, :].astype(F32)
            n = yn * g + lnb_ref[...]
            sn = _sig(n)
            dn = _mm_nt(dpw, wpw_ref[...]) * _dsilu(n, sn)
            dy = dn * g
            dc = rstd_ref[rs, :] * (dy - jnp.mean(dy, axis=-1, keepdims=True)
                                    - yn * jnp.mean(dy * yn, axis=-1, keepdims=True))
            dc_ref[rs, :] = dc.astype(BF16)
            sums = (jnp.sum(dn * yn, axis=0, keepdims=True), jnp.sum(dn, axis=0, keepdims=True),
                    jnp.sum(dc, axis=0, keepdims=True))
            return ((ov * sa).astype(BF16), dyb, (pw * sgate).astype(BF16), dya, (n * sn).astype(BF16), dpw,
                    sums)

        parts = [part(pl.ds(r * (tm // TAIL_PARTS), tm // TAIL_PARTS)) for r in range(TAIL_PARTS)]
        cat = lambda j: jnp.concatenate([pt[j] for pt in parts], axis=0)
        acc_bra[...] += _mm_tn(cat(0), cat(1))
        acc_brc[...] += _mm_tn(cat(2), cat(3))
        acc_pw[...] += _mm_tn(cat(4), cat(5))
        for j in range(3):
            gvec_ref[j:j + 1, :] += sum(pt[6][j] for pt in parts)

        @pl.when(i == last)
        def _():
            _flush_all([(acc_pw, 0, sem.at[0]), (acc_brc, D, sem.at[1]), (acc_bra, 2 * D, sem.at[2]),
                        (gppt_ref, WPP0, sem.at[3])], gpack_ref)

    tile = pl.BlockSpec((tm, D), lambda i: (i, 0))
    ztile = lambda j: pl.BlockSpec((tm, D), lambda i: (i, j))
    wsq = lambda k: pl.BlockSpec((D, D), lambda i: (k, 0))
    const = lambda shp: pl.BlockSpec(shp, lambda i: (0, 0))
    any_spec = pl.BlockSpec(memory_space=pl.ANY)
    return pl.pallas_call(
        body, name="tail_b", grid=(T // tm,),
        in_specs=[tile, tile, tile, tile, ztile(ZB_AGATE), ztile(ZB_GCONV), ztile(ZB_GATTN), ztile(ZB_CGATE),
                  tile, tile, pl.BlockSpec((tm, 1), lambda i: (i, 0)), const((1, D)), const((1, D)), wsq(0),
                  wsq(1), wsq(0), const((PLE, D)), any_spec],
        out_specs=(pl.BlockSpec((tm, 4 * D), lambda i: (i, 0)), tile, tile, const((8, D)), any_spec),
        out_shape=(jax.ShapeDtypeStruct((T, 7 * D), BF16), jax.ShapeDtypeStruct((T, D), BF16),
                   jax.ShapeDtypeStruct((T, D), BF16), jax.ShapeDtypeStruct((8, D), F32),
                   jax.ShapeDtypeStruct(gpack.shape, F32)),
        input_output_aliases={17: 4},
        scratch_shapes=[pltpu.VMEM((D, D), F32), pltpu.VMEM((D, D), F32), pltpu.VMEM((D, D), F32),
                        pltpu.SemaphoreType.DMA((4,))],
        compiler_params=_params(("arbitrary",)),
    )(dm, ya, yb, o, z, z, z, z, pw, y, rstd, ln_g, ln_b, wall_a, wall_a, wall_b, gppt, gpack)


def _conv_bwd(dc, z, wdw, dz, S, tm, copies, src, landing):
    T = dc.shape[0]
    nt = S // tm
    hb = tm // HALO
    nrows = T // HALO

    def body(dc_ref, hdc_ref, cv_ref, cg_ref, hcv_ref, hcg_ref, wdw_ref, dz_in, src_ref, dz_ref, gw_ref,
             land_ref, ubuf, dcbuf, dubuf, dwacc, shbuf, send_sems, recv_sems):
        b = pl.program_id(0)
        t = pl.program_id(1)

        @pl.when((b == 0) & (t == 0))
        def _():
            dwacc[...] = jnp.zeros_like(dwacc)
            for cp in copies(src_ref, land_ref, send_sems, recv_sems):
                cp.start()

        cv = cv_ref[...].astype(F32)
        sg = _sig(cg_ref[...].astype(F32))
        ubuf[HALO:HALO + tm, :] = cv * sg
        hu = hcv_ref[...].astype(F32) * _sig(hcg_ref[...].astype(F32))
        ubuf[0:HALO, :] = jnp.where(t > 0, hu, 0.0)
        ubuf[HALO + tm:HALO + tm + 8, :] = jnp.zeros((8, D), F32)
        dcbuf[0:tm, :] = dc_ref[...].astype(F32)
        dcbuf[tm:tm + HALO, :] = jnp.where(t < nt - 1, hdc_ref[...].astype(F32), 0.0)
        dcbuf[tm + HALO:tm + HALO + 8, :] = jnp.zeros((8, D), F32)

        def chunk(ci, carry):
            r0 = pl.multiple_of(ci * CONV_RC, CONV_RC)
            for lg in range(D // CONV_LC):
                l0 = lg * CONV_LC
                dubuf[pl.ds(r0, CONV_RC), pl.ds(l0, CONV_LC)] = _conv_taps(
                    wdw_ref, dcbuf, r0, l0, lambda k: CONV_K - 1 - k)
                dcc = dcbuf[pl.ds(r0, CONV_RC), pl.ds(l0, CONV_LC)]
                zero8 = jnp.zeros((8, CONV_LC), F32)
                dcz = jnp.concatenate([zero8, dcc, zero8], axis=0)
                for bb in range(8):
                    taps = [k for k in range(CONV_K) if (HALO - (CONV_K - 1) + k) % 8 == bb]
                    if not taps:
                        continue
                    rows = CONV_RC + (8 if bb else 0)
                    if bb:
                        shbuf[bb] = dcz[8 - bb:8 - bb + rows]
                    for k in taps:
                        a8 = HALO - (CONV_K - 1) + k - bb
                        dcs = shbuf[bb] if bb else dcc
                        prod = dcs * ubuf[pl.ds(r0 + a8, rows), pl.ds(l0, CONV_LC)]
                        part = prod[0:8]
                        for q in range(1, rows // 8):
                            part = part + prod[8 * q:8 * q + 8]
                        dwacc[8 * k:8 * k + 8, pl.ds(l0, CONV_LC)] += part
            return carry

        lax.fori_loop(0, tm // CONV_RC, chunk, 0)
        du = dubuf[...]
        dz_ref[:, 0:D] = (du * sg).astype(BF16)
        dz_ref[:, D:2 * D] = (du * cv * sg * (1.0 - sg)).astype(BF16)

        @pl.when((b == pl.num_programs(0) - 1) & (t == nt - 1))
        def _():
            for k in range(32):
                gw_ref[k:k + 1, :] = jnp.sum(dwacc[8 * k:8 * k + 8, :], axis=0, keepdims=True)
            cps = copies(src_ref, land_ref, send_sems, recv_sems)
            for cp in cps:
                cp.wait_recv()
            for cp in cps:
                cp.wait_send()

    def row(b, t):
        return b * nt + t

    def prev_halo(b, t):
        return jnp.maximum(row(b, t) * hb - 1, 0)

    def next_halo(b, t):
        return jnp.minimum((row(b, t) + 1) * hb, nrows - 1)

    return pl.pallas_call(
        body, name="conv_bwd", grid=(T // S, nt),
        in_specs=[pl.BlockSpec((tm, D), lambda b, t: (row(b, t), 0)),
                  pl.BlockSpec((HALO, D), lambda b, t: (next_halo(b, t), 0)),
                  pl.BlockSpec((tm, D), lambda b, t: (row(b, t), ZB_CVAL)),
                  pl.BlockSpec((tm, D), lambda b, t: (row(b, t), ZB_CGLU)),
                  pl.BlockSpec((HALO, D), lambda b, t: (prev_halo(b, t), ZB_CVAL)),
                  pl.BlockSpec((HALO, D), lambda b, t: (prev_halo(b, t), ZB_CGLU)),
                  pl.BlockSpec((32, D), lambda b, t: (0, 0)),
                  pl.BlockSpec(memory_space=pl.ANY), pl.BlockSpec(memory_space=pl.ANY)],
        out_specs=(pl.BlockSpec((tm, 2 * D), lambda b, t: (row(b, t), ZB_CVAL // 2)),
                   pl.BlockSpec((32, D), lambda b, t: (0, 0)), pl.BlockSpec(memory_space=pl.ANY)),
        out_shape=(jax.ShapeDtypeStruct(dz.shape, BF16), jax.ShapeDtypeStruct((32, D), F32), landing),
        input_output_aliases={7: 0},
        scratch_shapes=[pltpu.VMEM((tm + HALO + 8, D), F32), pltpu.VMEM((tm + HALO + 8, D), F32),
                        pltpu.VMEM((tm, D), F32), pltpu.VMEM((8 * 32, D), F32),
                        pltpu.VMEM((8, CONV_RC + 8, CONV_LC), F32), pltpu.SemaphoreType.DMA((3,)),
                        pltpu.SemaphoreType.DMA((3,))],
        compiler_params=_params(("arbitrary", "arbitrary")),
    )(dc, dc, z, z, z, z, wdw, dz, src)


def _attn_bwd(z, zkv, o, do, cos_t, sin_t, sinks, dz, S, tq, copies, src, landing):
    T = z.shape[0]
    nt = S // tq
    nq = tq // BLOCK

    def body(sink_ref, q_ref, kv_ref, hkv_ref, o_ref, do_ref, cos_ref, sin_ref, hcos_ref, hsin_ref, dz_in,
             src_ref, dq_ref, dkv_ref, gs_ref, land_ref, carry, dkacc, dvacc, send_sems, recv_sems):
        b = pl.program_id(0)
        tt = pl.program_id(1)
        t = nt - 1 - tt

        @pl.when((b == 0) & (tt == 0))
        def _():
            gs_ref[...] = jnp.zeros_like(gs_ref)
            for cp in copies(src_ref, land_ref, send_sems, recv_sems):
                cp.start()

        @pl.when(tt == 0)
        def _():
            carry[...] = jnp.zeros_like(carry)

        cos = cos_ref[...]
        sin = sin_ref[...]
        pswap = _swap_matrix()
        kv = jnp.concatenate([hkv_ref[...], kv_ref[...]], axis=0)
        cos_k = jnp.concatenate([hcos_ref[...], cos], axis=0)
        sin_k = jnp.concatenate([hsin_ref[...], sin], axis=0)
        kx = _kv_variants(_rope(kv[:, :BLOCK], cos_k, sin_k, pswap))
        vx = _kv_variants(kv[:, BLOCK:].astype(F32))
        band, sj = _band_mask(4)
        lo = lax.broadcasted_iota(jnp.int32, (4 * BLOCK, BLOCK), 1) < HEAD_DIM
        ones = jnp.ones((2 * BLOCK, 2 * BLOCK), BF16)
        qs = [(_rope(q_ref[:, 128 * hp:128 * hp + 128], cos, sin, pswap) * 0.125).astype(BF16)
              for hp in range(8)]
        dkacc[...] = jnp.zeros_like(dkacc)
        dvacc[...] = jnp.zeros_like(dvacc)
        gsum = jnp.zeros((1, BLOCK), F32)
        hlane = lax.broadcasted_iota(jnp.int32, (1, BLOCK), 1)
        for n in range(nq):
            first = (t == 0) & (n == 0)
            valid = band & (jnp.logical_not(first) | (sj >= BLOCK))
            r0 = n * BLOCK
            for g in range(2):
                cols = [slice(128 * (4 * g + j), 128 * (4 * g + j) + 128) for j in range(4)]
                lhs = jnp.concatenate([qs[4 * g + j][r0:r0 + BLOCK] for j in range(4)], axis=0)
                dov = jnp.concatenate([do_ref[r0:r0 + BLOCK, cs] for cs in cols], axis=0)
                prod = dov.astype(F32) * jnp.concatenate(
                    [o_ref[r0:r0 + BLOCK, cs] for cs in cols], axis=0).astype(F32)
                lhs_t = lhs.T
                dov_t = dov.T
                dq = jnp.zeros((4 * BLOCK, BLOCK), F32)
                dk_t = jnp.zeros((HEAD_DIM, 2 * BLOCK), F32)
                dv_t = jnp.zeros((HEAD_DIM, 2 * BLOCK), F32)
                for e in range(2):
                    kw = kx[g][e][r0:r0 + 2 * BLOCK]
                    vw = vx[g][e][r0:r0 + 2 * BLOCK]
                    s = _mm_nt(lhs, kw)
                    p, psink = _softmax_sink(s, valid, _sink_rep(sink_ref, g, e))
                    pe = jnp.where(lo if e == 0 else jnp.logical_not(lo), prod, 0.0)
                    pe_hi = pe.astype(BF16)
                    pe_lo = (pe - pe_hi.astype(F32)).astype(BF16)
                    delta = _mm(jnp.concatenate([pe_hi, pe_lo], axis=1), ones)
                    ds = (p * (_mm_nt(dov, vw) - delta)).astype(BF16)
                    dq = dq + _mm(ds, kw)
                    dims = slice(HEAD_DIM * e, HEAD_DIM * (e + 1))
                    dk_t = dk_t + _mm(lhs_t[dims], ds)
                    dv_t = dv_t + _mm(dov_t[dims], p.astype(BF16))
                    gs = -psink * delta[:, 0:BLOCK]
                    for j in range(4):
                        tot = jnp.sum(gs[j * BLOCK:(j + 1) * BLOCK], axis=0, keepdims=True)
                        gsum = gsum + jnp.where(hlane == 8 * g + 2 * j + e, tot, 0.0)
                dkacc[HEAD_DIM * g:HEAD_DIM * (g + 1), r0:r0 + 2 * BLOCK] += dk_t
                dvacc[HEAD_DIM * g:HEAD_DIM * (g + 1), r0:r0 + 2 * BLOCK] += dv_t
                for j in range(4):
                    dqj = _rope_f32(dq[j * BLOCK:(j + 1) * BLOCK] * 0.125, cos[r0:r0 + BLOCK],
                                    -sin[r0:r0 + BLOCK], pswap)
                    dq_ref[r0:r0 + BLOCK, cols[j]] = dqj.astype(BF16)
        gs_ref[0:1, :] += gsum
        dk_all = dkacc[...]
        dv_all = dvacc[...]
        dk_last = dk_all[:, tq:tq + BLOCK] + carry[0:BLOCK, :]
        dv_last = dv_all[:, tq:tq + BLOCK] + carry[BLOCK:2 * BLOCK, :]
        carry[0:BLOCK, :] = dk_all[:, 0:BLOCK]
        carry[BLOCK:2 * BLOCK, :] = dv_all[:, 0:BLOCK]
        if nq > 1:
            dk_tile = jnp.concatenate([dk_all[:, BLOCK:tq], dk_last], axis=1)
            dv_tile = jnp.concatenate([dv_all[:, BLOCK:tq], dv_last], axis=1)
        else:
            dk_tile, dv_tile = dk_last, dv_last
        dkv_ref[:, 0:BLOCK] = _rope_f32(dk_tile.T, cos, -sin, pswap).astype(BF16)
        dkv_ref[:, BLOCK:2 * BLOCK] = dv_tile.T.astype(BF16)

        @pl.when((b == pl.num_programs(0) - 1) & (tt == nt - 1))
        def _():
            cps = copies(src_ref, land_ref, send_sems, recv_sems)
            for cp in cps:
                cp.wait_recv()
            for cp in cps:
                cp.wait_send()

    def row(b, tt):
        return b * nt + (nt - 1 - tt)

    def halo(b, tt):
        return jnp.maximum(row(b, tt) * nq - 1, 0)

    tile = pl.BlockSpec((tq, D), lambda b, tt: (row(b, tt), 0))
    return pl.pallas_call(
        body, name="attn_bwd", grid=(T // S, nt),
        in_specs=[pl.BlockSpec(memory_space=pltpu.SMEM),
                  pl.BlockSpec((tq, D), lambda b, tt: (row(b, tt), ZB_Q)),
                  pl.BlockSpec((tq, 2 * BLOCK), lambda b, tt: (row(b, tt), 0)),
                  pl.BlockSpec((BLOCK, 2 * BLOCK), lambda b, tt: (halo(b, tt), 0)),
                  tile, tile,
                  pl.BlockSpec((tq, BLOCK), lambda b, tt: (row(b, tt), 0)),
                  pl.BlockSpec((tq, BLOCK), lambda b, tt: (row(b, tt), 0)),
                  pl.BlockSpec((BLOCK, BLOCK), lambda b, tt: (halo(b, tt), 0)),
                  pl.BlockSpec((BLOCK, BLOCK), lambda b, tt: (halo(b, tt), 0)),
                  pl.BlockSpec(memory_space=pl.ANY), pl.BlockSpec(memory_space=pl.ANY)],
        out_specs=(pl.BlockSpec((tq, D), lambda b, tt: (row(b, tt), ZB_Q)),
                   pl.BlockSpec((tq, 2 * BLOCK), lambda b, tt: (row(b, tt), 0)),
                   pl.BlockSpec((8, BLOCK), lambda b, tt: (0, 0)), pl.BlockSpec(memory_space=pl.ANY)),
        out_shape=(jax.ShapeDtypeStruct(dz.shape, BF16), jax.ShapeDtypeStruct((T, 2 * BLOCK), BF16),
                   jax.ShapeDtypeStruct((8, BLOCK), F32), landing),
        input_output_aliases={10: 0},
        scratch_shapes=[pltpu.VMEM((2 * BLOCK, BLOCK), F32), pltpu.VMEM((BLOCK, tq + BLOCK), F32),
                        pltpu.VMEM((BLOCK, tq + BLOCK), F32), pltpu.SemaphoreType.DMA((3,)),
                        pltpu.SemaphoreType.DMA((3,))],
        compiler_params=_params(("arbitrary", "arbitrary")),
    )(sinks, z, zkv, zkv, o, do, cos_t, sin_t, cos_t, sin_t, dz, src)


def _dh(dz, dz_kv, wall, x, dx1, ln_pre, tm, tile0, ntiles, gx_prev, name, copies, src, landing):
    T = x.shape[0]
    nsem = 3

    def body(*refs):
        dz_ref, kv_ref, wt_ref, x_ref, dx1_ref, g_ref, src_ref = refs[:7]
        gx_ref, glp_ref, land_ref, wbuf, send_sems, recv_sems, wsem = refs[-7:]
        i = pl.program_id(0)

        @pl.when(i == 0)
        def _():
            glp_ref[...] = jnp.zeros_like(glp_ref)
            for cp in copies(src_ref, land_ref, send_sems, recv_sems):
                cp.start()
            load = pltpu.make_async_copy(wt_ref, wbuf, wsem)
            load.start()
            load.wait()

        dh = _mm(dz_ref[...], wbuf[0:ZKV, :]) + _mm(kv_ref[...], wbuf[ZKV:IN_WIDTH, :])
        xv = x_ref[...]
        r = lax.rsqrt(jnp.mean(xv * xv, axis=-1, keepdims=True) + EPS)
        xr = xv * r
        glp_ref[...] += jnp.sum(dh * xr, axis=0, keepdims=True)
        a = dh * g_ref[...]
        gx_ref[...] = dx1_ref[...] + r * (a - xr * jnp.mean(a * xr, axis=-1, keepdims=True))

        @pl.when(i == ntiles - 1)
        def _():
            cps = copies(src_ref, land_ref, send_sems, recv_sems)
            for cp in cps:
                cp.wait_recv()
            for cp in cps:
                cp.wait_send()

    tile = pl.BlockSpec((tm, D), lambda i: (tile0 + i, 0))
    any_spec = pl.BlockSpec(memory_space=pl.ANY)
    operands = [dz, dz_kv, wall, x, dx1, ln_pre, src] + ([] if gx_prev is None else [gx_prev])
    return pl.pallas_call(
        body, name=name, grid=(ntiles,),
        in_specs=[pl.BlockSpec((tm, ZKV), lambda i: (tile0 + i, 0)),
                  pl.BlockSpec((tm, 2 * BLOCK), lambda i: (tile0 + i, 0)),
                  any_spec, tile, tile, pl.BlockSpec((1, D), lambda i: (0, 0)), any_spec]
        + ([] if gx_prev is None else [any_spec]),
        out_specs=(tile, pl.BlockSpec((1, D), lambda i: (0, 0)), any_spec),
        out_shape=(jax.ShapeDtypeStruct((T, D), F32), jax.ShapeDtypeStruct((1, D), F32), landing),
        input_output_aliases={} if gx_prev is None else {7: 0},
        scratch_shapes=[pltpu.VMEM((IN_WIDTH, D), BF16), pltpu.SemaphoreType.DMA((nsem,)),
                        pltpu.SemaphoreType.DMA((nsem,)), pltpu.SemaphoreType.DMA],
        compiler_params=_params(("arbitrary",)),
    )(*operands)


def _gwt(dz, dz_kv, h, tt):
    T = dz.shape[0]
    nt = T // tt
    last = nt - 1
    kv = 2 * BLOCK
    half = WIN_SHARD // 2

    def body(dz_ref, dzkv_ref, h_ref, gpack_ref, r1_ref, hbuf, acc, hsems, sems, send_sems, recv_sems):
        j = pl.program_id(0)
        t = pl.program_id(1)
        slot = j % 2
        rows = pl.ds(pl.multiple_of(t * tt, tt), tt)
        x, y, c = _coords()

        def exchange(jj):
            wall0, n_rows = (WT0 + jj * D, D) if jj < 7 else (WT0 + ZKV, kv)
            for _, n, s, pr in _wall_segments(wall0, n_rows):
                for hb in range(2):
                    lo, hi = max(pr, hb * half), min(pr + n, (hb + 1) * half)
                    if lo < hi:
                        cp = pltpu.make_async_remote_copy(
                            src_ref=gpack_ref.at[s, pl.ds(lo, hi - lo)],
                            dst_ref=r1_ref.at[s, pl.ds(lo - hb * half, hi - lo)], send_sem=send_sems.at[0],
                            recv_sem=recv_sems.at[0], device_id=(x, y, 1 - c), device_id_type=MESH)
                        pl.when(c == 1 - hb)(cp.start)

        def h_load(i):
            return pltpu.make_async_copy(h_ref.at[pl.ds(i * tt, tt)], hbuf.at[pl.ds(i * tt, tt)], hsems.at[i])

        @pl.when((j == 0) & (t == 0))
        def _():
            for i in range(nt):
                h_load(i).start()

        for i in range(nt):
            pl.when((j == 0) & (t == i))(h_load(i).wait)

        @pl.when((j < 7) & (t == 0))
        def _():
            acc[slot] = _mm_tn(dz_ref[...], hbuf[rows, :])

        @pl.when((j < 7) & (t > 0))
        def _():
            acc[slot] += _mm_tn(dz_ref[...], hbuf[rows, :])

        @pl.when((j == 7) & (t == 0))
        def _():
            acc[1, 0:kv, :] = _mm_tn(dzkv_ref[...], hbuf[rows, :])

        @pl.when((j == 7) & (t > 0))
        def _():
            acc[1, 0:kv, :] += _mm_tn(dzkv_ref[...], hbuf[rows, :])

        def block_total(sl):
            return pltpu.make_async_copy(acc.at[sl], gpack_ref.at[0, pl.ds(0, D)], sems.at[sl])

        for jj in range(8):
            @pl.when((t == last) & (j == jj))
            def _(jj=jj):
                if jj >= 1:
                    block_total((jj - 1) % 2).wait()
                    exchange(jj - 1)
                if jj == 7:
                    _flush_to_pack(acc.at[1, pl.ds(0, kv)], gpack_ref, WT0 + ZKV, sems.at[1])
                    exchange(7)
                    whole = _exchange_copies(gpack_ref, r1_ref, send_sems, recv_sems)[0]
                    whole.wait_recv()
                    whole.wait_send()
                else:
                    for cp in _pack_copies(acc.at[jj % 2], gpack_ref, WT0 + jj * D, sems.at[jj % 2]):
                        cp.start()

    any_spec = pl.BlockSpec(memory_space=pl.ANY)
    return pl.pallas_call(
        body, name="gwt", grid=(8, nt),
        in_specs=[pl.BlockSpec((tt, D), lambda j, t: (jnp.where(j == 7, last, t), jnp.minimum(j, 6))),
                  pl.BlockSpec((tt, kv), lambda j, t: (jnp.where(j == 7, t, 0), 0)), any_spec],
        out_specs=(any_spec, any_spec),
        out_shape=(jax.ShapeDtypeStruct((N_SHARDS, WIN_SHARD, D), F32),
                   jax.ShapeDtypeStruct((N_SHARDS, half, D), F32)),
        scratch_shapes=[pltpu.VMEM((T, D), BF16), pltpu.VMEM((2, D, D), F32), pltpu.SemaphoreType.DMA((nt,)),
                        pltpu.SemaphoreType.DMA((2,)), pltpu.SemaphoreType.DMA((1,)),
                        pltpu.SemaphoreType.DMA((1,))],
        compiler_params=_params(("arbitrary", "arbitrary")),
    )(dz, dz_kv, h)


_BC1 = 1.0 - ADAM_B1 ** ADAM_STEP
_BC2 = 1.0 - ADAM_B2 ** ADAM_STEP


def _adamw_math(w, g, m, v):
    m = ADAM_B1 * m + (1.0 - ADAM_B1) * g
    v = ADAM_B2 * v + (1.0 - ADAM_B2) * (g * g)
    delta = -ADAM_LR * ((m / _BC1) / (jnp.sqrt(v / _BC2) + ADAM_EPS) + ADAM_WD * w)
    return delta, m, v


def _adamw_rows(g, w, m, v, rows, name):
    R, C = w.shape

    def body(g_ref, w_ref, m_ref, v_ref, go_ref, d_ref, nm_ref, nv_ref):
        gv = g_ref[...]
        d, nm, nv = _adamw_math(w_ref[...], gv, m_ref[...], v_ref[...])
        go_ref[...] = gv
        d_ref[...] = d
        nm_ref[...] = nm
        nv_ref[...] = nv

    spec = pl.BlockSpec((rows, C), lambda i: (i, 0))
    shp = jax.ShapeDtypeStruct((R, C), F32)
    return pl.pallas_call(
        body, name=name, grid=(R // rows,), in_specs=[spec] * 4, out_specs=(spec,) * 4,
        out_shape=(shp,) * 4, compiler_params=_params(("arbitrary",)),
    )(g, w, m, v)


def _adamw_square(gfin, ws, ms, vs):
    rb = 64
    nb = SQ_SHARD // rb

    def body(*refs):
        g_refs = refs[0:5]
        w_refs, m_refs, v_refs = refs[5:10], refs[10:15], refs[15:20]
        outs = refs[20:]
        for k in range(5):
            gk = g_refs[k][...]
            d, nm, nv = _adamw_math(w_refs[k][...], gk, m_refs[k][...], v_refs[k][...])
            outs[4 * k][...] = gk
            outs[4 * k + 1][...] = d
            outs[4 * k + 2][...] = nm
            outs[4 * k + 3][...] = nv

    spec = pl.BlockSpec((rb, D), lambda i: (i, 0))
    gspecs = [pl.BlockSpec((rb, D), lambda i, k=k: (SQ_SHARD * k // rb + i, 0)) for k in range(5)]
    shp = jax.ShapeDtypeStruct((SQ_SHARD, D), F32)
    res = pl.pallas_call(
        body, name="adamw_square", grid=(nb,), in_specs=gspecs + [spec] * 15, out_specs=(spec,) * 20,
        out_shape=(shp,) * 20, compiler_params=_params(("arbitrary",)),
    )(*([gfin] * 5), *ws, *ms, *vs)
    return [tuple(res[4 * k:4 * k + 4]) for k in range(5)]


def _adamw_small(gs, ws, ms, vs):
    n = len(gs)

    def body(*refs):
        outs = refs[4 * n:]
        for k in range(n):
            d, nm, nv = _adamw_math(refs[n + k][...], refs[k][...], refs[2 * n + k][...],
                                    refs[3 * n + k][...])
            outs[3 * k][...] = d
            outs[3 * k + 1][...] = nm
            outs[3 * k + 2][...] = nv

    vm = pl.BlockSpec(memory_space=pltpu.VMEM)
    shapes = []
    for w in ws:
        shapes += [jax.ShapeDtypeStruct(w.shape, F32)] * 3
    res = pl.pallas_call(
        body, name="adamw_small", in_specs=[vm] * (4 * n), out_specs=(vm,) * (3 * n),
        out_shape=tuple(shapes),
    )(*gs, *ws, *ms, *vs)
    return [tuple(res[3 * k:3 * k + 3]) for k in range(n)]


def _rope_constants():
    half = ROPE_DIM // 2
    inv = jnp.power(ROPE_THETA, -jnp.arange(0, ROPE_DIM, 2, dtype=F32) / ROPE_DIM)
    freq = jnp.concatenate([inv, jnp.zeros((ROPE_ROWS - half,), F32)]).reshape(ROPE_ROWS, 1)
    spread = np.zeros((3, ROPE_ROWS, BLOCK), np.float32)
    for lane in range(BLOCK):
        d = lane % HEAD_DIM
        if d < ROPE_DIM:
            spread[0, d % half, lane] = 1.0
            spread[1, d % half, lane] = -1.0 if d < half else 1.0
        else:
            spread[2, 0, lane] = 1.0
    return freq, jnp.asarray(spread, BF16)


def kernel(x, p, positions, w_in, ln_pre, ln_post, w_dw, b_dw, conv_ln_g, conv_ln_b, w_pw, sinks, w_br_conv, w_br_attn, w_out, w_ple_gate, w_ple_proj, loss_target, m_w_in, m_ln_pre, m_ln_post, m_w_dw, m_b_dw, m_conv_ln_g, m_conv_ln_b, m_w_pw, m_sinks, m_w_br_conv, m_w_br_attn, m_w_out, m_w_ple_gate, m_w_ple_proj, v_w_in, v_ln_pre, v_ln_post, v_w_dw, v_b_dw, v_conv_ln_g, v_conv_ln_b, v_w_pw, v_sinks, v_w_br_conv, v_w_br_attn, v_w_out, v_w_ple_gate, v_w_ple_proj):
    nb, S, _ = x.shape
    T = nb * S
    xc = lax.axis_index("x")
    yc = lax.axis_index("y")
    cc = lax.axis_index("c")
    shard = 2 * xc + yc

    sq_w = (w_pw, w_br_conv, w_br_attn, w_out, w_ple_gate)
    wdw_shard = jnp.pad(w_dw[0], ((0, 1), (0, 0)))
    x2 = x.reshape(T, D)
    tm_res = min(TILE_RESIDENT, T // 2)

    tgt = loss_target.reshape(T, D)
    p2 = p.reshape(T, PLE)
    sinks1 = sinks.reshape(N_HEADS)

    tm = min(TILE_TOKEN, S)
    tc = min(TILE_CONV, S)
    tq = min(TILE_ATTN, S)

    z, zkv, wt, wdw_all, h, cos_t, sin_t = _inproj(
        x2, ln_pre, positions.astype(F32).reshape(1, T), *_rope_constants(), w_in[0].T.astype(BF16), wdw_shard,
        min(TILE_PROJ, T // 2))
    wdw = jnp.concatenate([wdw_all[s] for s in range(N_SHARDS)], axis=1)
    sq_shards = [w[0].astype(BF16) for w in sq_w] + [w_ple_proj[0].T.reshape(WPP_SHARD, D).astype(BF16)]
    o, wall_a = _attn_fwd(z, zkv, cos_t, sin_t, sinks1, S, tq, GROUP_CONV, sq_shards[0:2])
    ya, y, rstd, pw, wall_b, wppf = _conv_fwd(z, wdw, b_dw, conv_ln_g, conv_ln_b, wall_a, S, tc, GROUP_TAIL,
                                              sq_shards[2:])
    wppt = wppf.reshape(D, PLE)
    loss_p, dx1, dm, yb, g_ln_post, gsq, gw_ppt = _tail_a(x2, tgt, p2, o, ya, z, ln_post, wall_b, wppt, tm)

    cidx = jnp.reshape(cc, (1,)).astype(jnp.int32)
    scidx = jnp.stack([shard, cc]).astype(jnp.int32)

    def landing(pack, n, dtype):
        return jax.ShapeDtypeStruct((n, pack.shape[1] // 2, D), dtype)

    dz, do, dc, gvec, gsq = _tail_b(dm, ya, yb, o, z, pw, y, rstd, conv_ln_g, conv_ln_b, wall_a, wall_b,
                                    gw_ppt.reshape(PLE, D), gsq, tm)
    dz, g_wdw, r1_sq = _conv_bwd(dc, z, wdw, dz, S, tc, _exchange_copies, gsq, landing(gsq, N_SHARDS, F32))
    cs_sq = _chip_sum(cidx, gsq, r1_sq, "chip_sum_sq")
    dz, dkv, g_sinks, r2_sq = _attn_bwd(z, zkv, o, do, cos_t, sin_t, sinks1, dz, S, tq, _chip_sum_copies, cs_sq,
                                        landing(gsq, 3, BF16))
    gwt_pack, r1_wt = _gwt(dz, dkv, h, min(2 * TILE_PROJ, T))
    cs_wt = _chip_sum(cidx, gwt_pack, r1_wt, "chip_sum_wt")
    gx, g_ln_pre, r2_wt = _dh(dz, dkv, wt, x2, dx1, ln_pre, tm_res, 0, T // tm_res, None, "dh", _chip_sum_copies,
                              cs_wt, landing(gwt_pack, 3, BF16))
    row37 = jnp.concatenate([g_sinks[0:1, 0:N_HEADS], loss_p, jnp.zeros((1, D - N_HEADS - 1), F32)], axis=1)
    vec = jnp.concatenate([g_wdw, g_ln_pre, g_ln_post, gvec[2:3], gvec[0:1], gvec[1:2], row37,
                           jnp.zeros((VEC_ROWS - 38, D), F32)], axis=0)
    gfin_wt, gfin_sq, tot = _finish_reduce(_final_half(scidx, gwt_pack, r1_wt, r2_wt, "final_half_wt"),
                                           _final_half(scidx, gsq, r1_sq, r2_sq, "final_half_sq"), vec)

    g_w_in, d_w_in, nm_w_in, nv_w_in = [a.T for a in _adamw_rows(
        gfin_wt, w_in[0].T, m_w_in[0].T, v_w_in[0].T, WIN_SHARD // 8, "adamw_w_in")]
    g_w_in = g_w_in[None]
    sq_m = (m_w_pw, m_w_br_conv, m_w_br_attn, m_w_out, m_w_ple_gate)
    sq_v = (v_w_pw, v_w_br_conv, v_w_br_attn, v_w_out, v_w_ple_gate)
    sq_res = _adamw_square(gfin_sq, [w[0] for w in sq_w], [m[0] for m in sq_m], [v[0] for v in sq_v])
    g_wpp = gfin_sq[5 * SQ_SHARD:SQ_PACK].reshape(PLE, PLE).T
    g_dw_all = tot[0:CONV_K]
    g_dw = lax.dynamic_slice_in_dim(g_dw_all, shard * PLE, PLE, axis=1)
    small_g = [g_wpp, g_dw, tot[32:33], tot[33:34], tot[34:35], tot[35:36], tot[36:37],
               tot[37:38, 0:N_HEADS]]
    small_w = [w_ple_proj[0], w_dw[0], ln_pre, ln_post, b_dw, conv_ln_g, conv_ln_b, sinks]
    small_m = [m_w_ple_proj[0], m_w_dw[0], m_ln_pre, m_ln_post, m_b_dw, m_conv_ln_g, m_conv_ln_b, m_sinks]
    small_v = [v_w_ple_proj[0], v_w_dw[0], v_ln_pre, v_ln_post, v_b_dw, v_conv_ln_g, v_conv_ln_b, v_sinks]
    small = _adamw_small(small_g, small_w, small_m, small_v)

    loss = tot[37, N_HEADS]
    grads = [g_w_in, small_g[2], small_g[3], g_dw[None], small_g[4], small_g[5], small_g[6],
             sq_res[0][0][None], small_g[7], sq_res[1][0][None], sq_res[2][0][None], sq_res[3][0][None],
             sq_res[4][0][None], g_wpp[None]]

    def triple(i):
        w_in_t = (d_w_in[None], nm_w_in[None], nv_w_in[None])
        sq = lambda k: tuple(a[None] for a in sq_res[k][1:4])
        sm = lambda k, lead: tuple(a[None] if lead else a for a in small[k])
        return [w_in_t[i], sm(2, False)[i], sm(3, False)[i], sm(1, True)[i], sm(4, False)[i],
                sm(5, False)[i], sm(6, False)[i], sq(0)[i], sm(7, False)[i], sq(1)[i], sq(2)[i], sq(3)[i],
                sq(4)[i], sm(0, True)[i]]

    return (loss, gx.reshape(nb, S, D), *grads, *triple(0), *triple(1), *triple(2))
```

```python
import functools

import jax
import jax.numpy as jnp
import numpy as np
from jax import lax
from jax.experimental import pallas as pl
from jax.experimental.pallas import tpu as pltpu

F32 = jnp.float32
BF16 = jnp.bfloat16

D = 1024
PLE = 256
N_HEADS = 16
HEAD_DIM = 64
BLOCK = 128
CONV_K = 31
ROPE_DIM = 16
ROPE_THETA = 500000.0
EPS = 1e-6
IN_WIDTH = 7424
N_SHARDS = 4

ADAM_LR = 0.001
ADAM_B1 = 0.9
ADAM_B2 = 0.999
ADAM_EPS = 1e-08
ADAM_WD = 0.01
ADAM_STEP = 10

WT0 = 5 * D
WPP0 = WT0 + IN_WIDTH
WIN_SHARD = IN_WIDTH // N_SHARDS
SQ_SHARD = D // N_SHARDS
WPP_SHARD = PLE * PLE // D
PACK_ROWS = WIN_SHARD + 5 * SQ_SHARD + WPP_SHARD
VMEM_LIMIT = 56 * 1024 * 1024
MESH = pl.DeviceIdType.MESH
TILE_RESIDENT = 512
TILE_PROJ = 1024
TILE_TOKEN = 256
TILE_CONV = 512
TILE_ATTN = 512


ZB_AGATE, ZB_GCONV, ZB_GATTN, ZB_CGATE, ZB_CVAL, ZB_CGLU, ZB_Q = range(7)
ZKV = 7 * D
_SEGMENTS = ((0, D, ZB_CVAL * D), (D, D, ZB_CGLU * D), (2 * D, D, ZB_CGATE * D), (3 * D, D, ZB_Q * D),
             (4 * D, 2 * BLOCK, ZKV), (4 * D + 2 * BLOCK, D, ZB_AGATE * D),
             (5 * D + 2 * BLOCK, D, ZB_GCONV * D), (6 * D + 2 * BLOCK, D, ZB_GATTN * D))
_WT_CUTS = (0, 192, 640, 1216, WIN_SHARD)


def _zp_row(o):
    for a, w, zp in _SEGMENTS:
        if a <= o < a + w:
            return zp + o - a
    raise ValueError(o)


def _pieces(s):
    out = []
    for a, b in zip(_WT_CUTS[:-1], _WT_CUTS[1:]):
        first = _zp_row(WIN_SHARD * s + a)
        assert _zp_row(WIN_SHARD * s + b - 1) == first + b - a - 1
        out.append((a, b - a, WT0 + first))
    for k in range(5):
        out.append((WIN_SHARD + SQ_SHARD * k, SQ_SHARD, D * k + SQ_SHARD * s))
    out.append((WIN_SHARD + 5 * SQ_SHARD, WPP_SHARD, WPP0 + WPP_SHARD * s))
    return out


N_PIECES = len(_pieces(0))


def _wall_segments(wall0, rows):
    out = []
    for s in range(N_SHARDS):
        for pr, n, wr in _pieces(s):
            lo, hi = max(wr, wall0), min(wr + n, wall0 + rows)
            if lo < hi:
                out.append((lo - wall0, hi - lo, s, pr + lo - wr))
    assert sum(n for _, n, _, _ in out) == rows
    return out


def _sel(s, vals):
    r = jnp.int32(vals[0])
    for i in range(1, len(vals)):
        r = jnp.where(s == i, jnp.int32(vals[i]), r)
    return r


def _sig(x):
    return 1.0 / (1.0 + jnp.exp(-x))


def _mm(a, b):
    return lax.dot_general(a, b, (((1,), (0,)), ((), ())), preferred_element_type=F32)


def _mm_nt(a, b):
    return lax.dot_general(a, b, (((1,), (1,)), ((), ())), preferred_element_type=F32)


def _mm_tn(a, b):
    return lax.dot_general(a, b, (((0,), (0,)), ((), ())), preferred_element_type=F32)


def _params(sem=None):
    return pltpu.CompilerParams(dimension_semantics=sem, vmem_limit_bytes=VMEM_LIMIT)


def _flush_to_pack(acc_ref, gpack_ref, wall0, sem):
    for cp in _pack_copies(acc_ref, gpack_ref, wall0, sem):
        cp.start()
        cp.wait()


def _flush_all(items, gpack_ref):
    for acc_ref, wall0, sem in items:
        for cp in _pack_copies(acc_ref, gpack_ref, wall0, sem):
            cp.start()
    for acc_ref, _, sem in items:
        pltpu.make_async_copy(acc_ref, gpack_ref.at[0, pl.ds(0, acc_ref.shape[0])], sem).wait()


def _pack_copies(acc_ref, gpack_ref, wall0, sem):
    base = 0 if gpack_ref.shape[1] == WIN_SHARD else WIN_SHARD
    out = []
    for r, n, s, pr in _wall_segments(wall0, acc_ref.shape[0]):
        assert 0 <= pr - base and pr - base + n <= gpack_ref.shape[1]
        out.append(pltpu.make_async_copy(acc_ref.at[pl.ds(r, n)], gpack_ref.at[s, pl.ds(pr - base, n)], sem))
    return out


def _coords():
    return lax.axis_index("x"), lax.axis_index("y"), lax.axis_index("c")


def _chip_peers(x, y):
    return [(1 - x, y), (x, 1 - y), (1 - x, 1 - y)]


WIN_PIECES = tuple(range(len(_WT_CUTS) - 1))
SQ_PIECES = tuple(range(len(WIN_PIECES), N_PIECES))


def _gather_ops(group, src, landing, bytes_ref, stage, send_sems, recv_sems, loc_sem):
    sizes = [_pieces(0)[p][1] for p in group]
    half_rows = sum(n // 2 for n in sizes)
    starts = [sum(sizes[:i]) for i in range(len(sizes))]

    def rcopy(a, b, k, dev):
        return pltpu.make_async_remote_copy(src_ref=a, dst_ref=b, send_sem=send_sems.at[k],
                                            recv_sem=recv_sems.at[k], device_id=dev, device_id_type=MESH)

    def total(k):
        x, y, c = _coords()
        rows = bytes_ref.at[pl.ds(0, half_rows)]
        return rcopy(rows, rows, k, (x, y, c))

    def own_total():
        rows = stage.at[pl.ds(0, sum(sizes))]
        return pltpu.make_async_copy(rows, rows, loc_sem)

    def send():
        x, y, c = _coords()
        s_me = 2 * x + y
        for k, (px, py) in enumerate(_chip_peers(x, y)):
            for p, n in zip(group, sizes):
                h = n // 2
                rcopy(src(p, c * h, h), landing(p, s_me, c * h, h), k, (px, py, c)).start()
        for p, n, r in zip(group, sizes, starts):
            pltpu.make_async_copy(src(p, 0, n), stage.at[pl.ds(r, n)], loc_sem).start()

    def forward():
        x, y, c = _coords()
        own_total().wait()
        for p, n, r in zip(group, sizes, starts):
            pltpu.make_async_copy(stage.at[pl.ds(r, n)], landing(p, 2 * x + y, 0, n), loc_sem).start()
        for k, (px, py) in enumerate(_chip_peers(x, y)):
            total(k).wait_recv()
            for p, n in zip(group, sizes):
                rows = landing(p, 2 * px + py, c * (n // 2), n // 2)
                rcopy(rows, rows, 3 + k, (x, y, 1 - c)).start()

    def finish():
        own_total().wait()
        for k in range(3):
            total(3 + k).wait_recv()
        for k in range(6):
            total(k).wait_send()

    return send, forward, finish


def _piece_rows(ref, start, off, n):
    first = start + off
    return ref.at[pl.ds(first if isinstance(first, int) else pl.multiple_of(first, 32), n)]


GROUP_CONV = SQ_PIECES[0:2]
GROUP_TAIL = SQ_PIECES[2:]


def _group_shapes(group):
    n_sq = sum(1 for q in group if q != N_PIECES - 1)
    return [jax.ShapeDtypeStruct((n_sq * D, D), BF16)] + (
        [jax.ShapeDtypeStruct((PLE, D), BF16)] if N_PIECES - 1 in group else [])


def _group_scratch(group):
    return [pltpu.VMEM((sum(_pieces(0)[q][1] for q in group), D), BF16), pltpu.SemaphoreType.DMA((6,)),
            pltpu.SemaphoreType.DMA((6,)), pltpu.SemaphoreType.DMA]


def _group_gather(group, shard_refs, out_refs, scratch, step, n_steps):
    wall_ref = out_refs[0]
    stage, send_sems, recv_sems, loc_sem = scratch

    def src(q, off, n):
        return _piece_rows(shard_refs[group.index(q)], 0, off, n)

    def landing(q, s, off, n):
        if q == N_PIECES - 1:
            return _piece_rows(out_refs[1], WPP_SHARD * s, off, n)
        return _piece_rows(wall_ref, D * group.index(q) + SQ_SHARD * s, off, n)

    send, forward, finish = _gather_ops(group, src, landing, wall_ref, stage, send_sems, recv_sems, loc_sem)
    pl.when(step == 0)(send)
    pl.when(step == n_steps // 2)(forward)
    return finish


ROPE_ROWS = 16


def _rope_tables(pos, freq, spread_ref):
    def to_lanes(v, e):
        out = None
        for _ in range(3):
            part = v.astype(BF16)
            term = _mm_tn(part, e)
            out = term if out is None else out + term
            v = v - part.astype(F32)
        return out

    ang = freq * pos
    return (to_lanes(jnp.cos(ang), spread_ref[0]) + spread_ref[2, 0:1, :].astype(F32),
            to_lanes(jnp.sin(ang), spread_ref[1]))


SQ_PACK = PACK_ROWS - WIN_SHARD


def _row_tile(half):
    return max(t for t in range(8, 321, 8) if half % t == 0)


def _exchange_copies(g_ref, r1_ref, send_sems, recv_sems):
    x, y, c = _coords()
    half = g_ref.shape[1] // 2
    return [pltpu.make_async_remote_copy(
        src_ref=g_ref.at[:, pl.ds(pl.multiple_of((1 - c) * half, 32), half), :], dst_ref=r1_ref,
        send_sem=send_sems.at[0], recv_sem=recv_sems.at[0], device_id=(x, y, 1 - c), device_id_type=MESH)]


def _chip_sum_copies(cs_ref, r2_ref, send_sems, recv_sems):
    x, y, c = _coords()
    return [pltpu.make_async_remote_copy(
        src_ref=cs_ref.at[2 * px + py], dst_ref=r2_ref.at[k], send_sem=send_sems.at[k],
        recv_sem=recv_sems.at[k], device_id=(px, py, c), device_id_type=MESH)
        for k, (px, py) in enumerate(_chip_peers(x, y))]


def _chip_sum(cidx, gpack, r1, name):
    half = gpack.shape[1] // 2
    rt = _row_tile(half)

    def body(c_ref, g_ref, r_ref, o_ref):
        o_ref[...] = (g_ref[...] + r_ref[...]).astype(BF16)

    nt = half // rt
    return pl.pallas_call(
        body, name=name,
        grid_spec=pltpu.PrefetchScalarGridSpec(
            num_scalar_prefetch=1, grid=(N_SHARDS, nt),
            in_specs=[pl.BlockSpec((1, rt, D), lambda s, t, c: (s, c[0] * nt + t, 0)),
                      pl.BlockSpec((1, rt, D), lambda s, t, c: (s, t, 0))],
            out_specs=pl.BlockSpec((1, rt, D), lambda s, t, c: (s, t, 0))),
        out_shape=jax.ShapeDtypeStruct((N_SHARDS, half, D), BF16),
        compiler_params=_params(("arbitrary", "arbitrary")),
    )(cidx, gpack, r1)


def _final_half(sc, gpack, r1, r2, name):
    rows = gpack.shape[1]
    half = rows // 2
    rt = _row_tile(half)

    def body(sc_ref, g_ref, r_ref, p_ref, o_ref):
        acc = g_ref[0] + r_ref[0]
        for k in range(3):
            acc = acc + p_ref[k].astype(F32)
        o_ref[...] = acc

    nt = half // rt
    return pl.pallas_call(
        body, name=name,
        grid_spec=pltpu.PrefetchScalarGridSpec(
            num_scalar_prefetch=1, grid=(nt,),
            in_specs=[pl.BlockSpec((1, rt, D), lambda t, sc: (sc[0], sc[1] * nt + t, 0)),
                      pl.BlockSpec((1, rt, D), lambda t, sc: (sc[0], t, 0)),
                      pl.BlockSpec((3, rt, D), lambda t, sc: (0, t, 0))],
            out_specs=pl.BlockSpec((rt, D), lambda t, sc: (sc[1] * nt + t, 0))),
        out_shape=jax.ShapeDtypeStruct((rows, D), F32),
        compiler_params=_params(("arbitrary",)),
    )(sc, gpack, r1, r2)


VEC_ROWS = 40


def _finish_reduce(fwt, fsq, vec):
    def body(fwt_ref, fsq_ref, v_ref, owt_ref, osq_ref, tot_ref, buf, send_sems, recv_sems):
        x, y, c = _coords()
        swaps = []
        for k, (f_ref, o_ref) in enumerate(((fwt_ref, owt_ref), (fsq_ref, osq_ref))):
            half = f_ref.shape[0] // 2
            rows = pl.ds(pl.multiple_of(c * half, 32), half)
            swaps.append(pltpu.make_async_remote_copy(
                src_ref=f_ref.at[rows], dst_ref=o_ref.at[rows], send_sem=send_sems.at[7 + k],
                recv_sem=recv_sems.at[7 + k], device_id=(x, y, 1 - c), device_id_type=MESH))
        for cp in swaps:
            cp.start()
        me = 4 * x + 2 * y + c
        buf[me] = v_ref[...]
        cps = []
        for r in range(1, 8):
            dx, dy, dc = (r >> 2) & 1, (r >> 1) & 1, r & 1
            peer = (1 - x if dx else x, 1 - y if dy else y, 1 - c if dc else c)
            cp = pltpu.make_async_remote_copy(
                src_ref=v_ref, dst_ref=buf.at[me], send_sem=send_sems.at[r - 1],
                recv_sem=recv_sems.at[r - 1], device_id=peer, device_id_type=MESH)
            cp.start()
            cps.append(cp)
        for cp in cps:
            cp.wait_recv()
        for cp in cps:
            cp.wait_send()
        acc = buf[0]
        for d in range(1, 8):
            acc = acc + buf[d]
        tot_ref[...] = acc
        for cp in swaps:
            cp.wait()

    any_spec = pl.BlockSpec(memory_space=pl.ANY)
    vm = pl.BlockSpec(memory_space=pltpu.VMEM)
    return pl.pallas_call(
        body, name="finish_reduce",
        out_shape=(jax.ShapeDtypeStruct(fwt.shape, F32), jax.ShapeDtypeStruct(fsq.shape, F32),
                   jax.ShapeDtypeStruct((VEC_ROWS, D), F32)),
        in_specs=[any_spec, any_spec, vm], out_specs=(any_spec, any_spec, vm),
        input_output_aliases={0: 0, 1: 1},
        scratch_shapes=[pltpu.VMEM((8, VEC_ROWS, D), F32), pltpu.SemaphoreType.DMA((9,)),
                        pltpu.SemaphoreType.DMA((9,))],
    )(fwt, fsq, vec)


SOLO_ROWS = WIN_SHARD - BLOCK // 2


def _solo_first(s):
    return 0 if s % 2 == 0 else BLOCK // 2


def _solo_segments(s):
    lo = _solo_first(s)
    out = []
    for a, n, wr in _pieces(s)[:len(WIN_PIECES)]:
        b0, b1 = max(a, lo), min(a + n, lo + SOLO_ROWS)
        if b0 >= b1:
            continue
        z0 = wr - WT0 + b0 - a
        if out and out[-1][0] + out[-1][1] == b0 - lo and out[-1][2] + out[-1][1] == z0:
            out[-1] = (out[-1][0], out[-1][1] + b1 - b0, out[-1][2])
        else:
            out.append((b0 - lo, b1 - b0, z0))
    out = [r for o, n, z0 in out for r in
           (((o, ZKV - z0, z0), (o + ZKV - z0, z0 + n - ZKV, ZKV)) if z0 < ZKV < z0 + n else ((o, n, z0),))]
    assert all(v % BLOCK == 0 for seg in out for v in seg) and sum(n for _, n, _ in out) == SOLO_ROWS
    return out


def _shared_tile(pair):
    z0 = _zp_row(WIN_SHARD * (2 * pair) + SOLO_ROWS)
    assert z0 % BLOCK == 0 and _zp_row(WIN_SHARD * (2 * pair + 1)) == z0 + BLOCK // 2
    return z0


def _inproj(x, ln_pre, pos, freq, spread, win_t, wdw_shard, tm):
    T = x.shape[0]
    n_t = T // tm
    assert n_t >= 2 and n_t % 2 == 0
    tables = [[_pieces(s)[p][2] - WT0 for s in range(N_SHARDS)] for p in WIN_PIECES]
    sizes = [_pieces(0)[p][1] for p in WIN_PIECES]
    half_rows = sum(n // 2 for n in sizes)
    relation_of_pass = {1: 1, 2: 0, 3: 2}

    def body(x_ref, g_ref, pos_ref, f_ref, e_ref, win_ref, wdw_ref, z_ref, zkv_ref, wt_ref, wdwall_ref, h_ref,
             cos_ref, sin_ref, wbuf, stage, stage_sh, hbuf, wsend, wrecv, loc_sems, out_sems, sh_sems, h_sems):
        p = pl.program_id(0)
        t = pl.program_id(1)
        x, y, c = _coords()
        s_me = 2 * x + y
        peers = _chip_peers(x, y)
        shard = jnp.bitwise_xor(s_me, p)
        first, last = t == 0, t == n_t - 1

        def rcopy(a, b, k, dev):
            return pltpu.make_async_remote_copy(src_ref=a, dst_ref=b, send_sem=wsend.at[k], recv_sem=wrecv.at[k],
                                                device_id=dev, device_id_type=MESH)

        def total(k):
            rows = wt_ref.at[pl.ds(0, half_rows)]
            return rcopy(rows, rows, k, (x, y, c))

        def in_hbm(q, s, off, n):
            return _piece_rows(wt_ref, _sel(s, tables[q]), off, n)

        def in_vmem(q, s):
            return _piece_rows(wbuf, WIN_SHARD * s + _WT_CUTS[q], 0, sizes[q])

        def send_to(k):
            px, py = peers[k]
            for q, n in zip(WIN_PIECES, sizes):
                rcopy(_piece_rows(win_ref, _WT_CUTS[q], c * (n // 2), n // 2), in_hbm(q, s_me, c * (n // 2), n // 2),
                      k, (px, py, c)).start()

        def forward_from(k):
            px, py = peers[k]
            total(k).wait_recv()
            for q, n in zip(WIN_PIECES, sizes):
                rows = in_hbm(q, 2 * px + py, c * (n // 2), n // 2)
                rcopy(rows, rows, 3 + k, (x, y, 1 - c)).start()

        def shard_total(a, b, sem):
            return pltpu.make_async_copy(a.at[pl.ds(0, WIN_SHARD)], b.at[pl.ds(0, WIN_SHARD)], sem)

        def wdw_copies():
            return [pltpu.make_async_remote_copy(
                src_ref=wdw_ref, dst_ref=wdwall_ref.at[s_me], send_sem=wsend.at[6 + k], recv_sem=wrecv.at[6 + k],
                device_id=(px, py, c), device_id_type=MESH) for k, (px, py) in enumerate(peers)]

        def own_wdw():
            return pltpu.make_async_copy(wdw_ref, wdwall_ref.at[s_me], loc_sems.at[2])

        @pl.when((p == 0) & first)
        def _():
            send_to(0)
            send_to(1)
            own_wdw().start()
            for cp in wdw_copies():
                cp.start()
            for q in WIN_PIECES:
                pltpu.make_async_copy(_piece_rows(win_ref, _WT_CUTS[q], 0, sizes[q]), in_vmem(q, s_me),
                                      loc_sems.at[0]).start()

        for pp, k in relation_of_pass.items():
            pl.when((p == pp - 1) & (t == n_t - 2))(functools.partial(forward_from, k))

            @pl.when((p == pp - 1) & last)
            def _(k=k):
                total(3 + k).wait_recv()
                px, py = peers[k]
                for q in WIN_PIECES:
                    pltpu.make_async_copy(in_hbm(q, 2 * px + py, 0, sizes[q]), in_vmem(q, 2 * px + py),
                                          loc_sems.at[0]).start()

            @pl.when((p == pp) & first)
            def _(pp=pp):
                shard_total(wt_ref, wbuf, loc_sems.at[0]).wait()
                if pp == 1:
                    total(0).wait_send()
                    total(1).wait_send()
                    send_to(2)

        step = p * n_t + t
        slot = step % 2
        rows = pl.ds(pl.multiple_of(t * tm, tm), tm)

        def out_total(sl):
            return pltpu.make_async_copy(stage.at[sl], stage.at[sl], out_sems.at[sl])

        def sh_copy(sl, z0):
            return pltpu.make_async_copy(stage_sh.at[sl], z_ref.at[rows, pl.ds(z0, BLOCK)], sh_sems.at[sl])

        @pl.when(step >= 2)
        def _():
            out_total(slot).wait()

        @pl.when((step >= 2) & (((step - 2) // n_t) % 2 == 1))
        def _():
            sh_copy(slot, 0).wait()

        def h_out(sl):
            return pltpu.make_async_copy(hbuf.at[sl], h_ref.at[rows], h_sems.at[sl])

        def h_in(sl, tile):
            return pltpu.make_async_copy(h_ref.at[pl.ds(pl.multiple_of(tile * tm, tm), tm)], hbuf.at[sl],
                                         h_sems.at[sl])

        @pl.when((p == 0) & (t >= 2))
        def _():
            h_out(slot).wait()

        @pl.when(p == 0)
        def _():
            xv = x_ref[...]
            r = lax.rsqrt(jnp.mean(xv * xv, axis=-1, keepdims=True) + EPS)
            hbuf[slot] = (xv * r * g_ref[...]).astype(BF16)
            h_out(slot).start()
            cos, sin = _rope_tables(pos_ref[...], f_ref[...], e_ref)
            cos_ref[...] = cos
            sin_ref[...] = sin

        @pl.when((p == 1) & first)
        def _():
            h_out(0).wait()
            h_out(1).wait()
            h_in(0, 0).start()

        @pl.when(p >= 1)
        def _():
            h_in(slot, t).wait()

        @pl.when((p >= 1) & (step < N_SHARDS * n_t - 1))
        def _():
            h_in(1 - slot, jnp.where(last, 0, t + 1)).start()

        @pl.when((p == 0) & first)
        def _():
            shard_total(win_ref, wbuf, loc_sems.at[0]).wait()
            for q in WIN_PIECES:
                pltpu.make_async_copy(in_vmem(q, s_me), in_hbm(q, s_me, 0, sizes[q]), loc_sems.at[1]).start()

        solo0 = pl.multiple_of(WIN_SHARD * shard + (BLOCK // 2) * (shard % 2), BLOCK // 2)
        stage[slot] = _mm_nt(hbuf[slot], wbuf[pl.ds(solo0, SOLO_ROWS), :]).astype(BF16)
        for s in range(N_SHARDS):
            @pl.when(shard == s)
            def _(s=s):
                for off, n, z0 in _solo_segments(s):
                    dst = zkv_ref.at[rows] if z0 == ZKV else z_ref.at[rows, pl.ds(z0, n)]
                    pltpu.make_async_copy(stage.at[slot, :, pl.ds(off, n)], dst, out_sems.at[slot]).start()

        @pl.when(p % 2 == 1)
        def _():
            pair = shard // 2
            w0 = pl.multiple_of(2 * WIN_SHARD * pair + SOLO_ROWS, BLOCK // 2)
            z0 = pl.multiple_of(jnp.where(pair == 0, _shared_tile(0), _shared_tile(1)), BLOCK)
            stage_sh[slot] = _mm_nt(hbuf[slot], wbuf[pl.ds(w0, BLOCK), :]).astype(BF16)
            sh_copy(slot, z0).start()

        @pl.when((p == 3) & last)
        def _():
            for k in (2, 3, 4, 5):
                total(k).wait_send()
            shard_total(wbuf, wt_ref, loc_sems.at[1]).wait()
            cps = wdw_copies()
            for cp in cps:
                cp.wait_recv()
            for cp in cps:
                cp.wait_send()
            own_wdw().wait()
            for sl in range(2):
                out_total(sl).wait()
                sh_copy(sl, 0).wait()

    def in_pass0(p, t):
        return jnp.where(p == 0, t, n_t - 1)

    any_spec = pl.BlockSpec(memory_space=pl.ANY)
    return pl.pallas_call(
        body, name="inproj", grid=(N_SHARDS, n_t),
        in_specs=[pl.BlockSpec((tm, D), lambda p, t: (in_pass0(p, t), 0)), pl.BlockSpec((1, D), lambda p, t: (0, 0)),
                  pl.BlockSpec((1, tm), lambda p, t: (0, in_pass0(p, t))),
                  pl.BlockSpec((ROPE_ROWS, 1), lambda p, t: (0, 0)),
                  pl.BlockSpec((3, ROPE_ROWS, BLOCK), lambda p, t: (0, 0, 0)), any_spec, any_spec],
        out_specs=(any_spec,) * 5 + (pl.BlockSpec((tm, BLOCK), lambda p, t: (in_pass0(p, t), 0)),) * 2,
        out_shape=(jax.ShapeDtypeStruct((T, ZKV), BF16), jax.ShapeDtypeStruct((T, 2 * BLOCK), BF16),
                   jax.ShapeDtypeStruct((IN_WIDTH, D), BF16), jax.ShapeDtypeStruct((N_SHARDS, 32, PLE), F32),
                   jax.ShapeDtypeStruct((T, D), BF16), jax.ShapeDtypeStruct((T, BLOCK), F32),
                   jax.ShapeDtypeStruct((T, BLOCK), F32)),
        scratch_shapes=[pltpu.VMEM((IN_WIDTH, D), BF16), pltpu.VMEM((2, tm, SOLO_ROWS), BF16),
                        pltpu.VMEM((2, tm, BLOCK), BF16), pltpu.VMEM((2, tm, D), BF16),
                        pltpu.SemaphoreType.DMA((9,)), pltpu.SemaphoreType.DMA((9,)),
                        pltpu.SemaphoreType.DMA((3,)), pltpu.SemaphoreType.DMA((2,)),
                        pltpu.SemaphoreType.DMA((2,)), pltpu.SemaphoreType.DMA((2,))],
        compiler_params=_params(("arbitrary", "arbitrary")),
    )(x, ln_pre, pos, freq, spread, win_t, wdw_shard)


HALO = 32
CONV_RC = 64
CONV_LC = 256


def _conv_taps(w_ref, src, r0, lane0, offset_of_tap):
    lanes = pl.ds(lane0, CONV_LC)
    out = None
    for b in range(8):
        taps = [k for k in range(CONV_K) if offset_of_tap(k) % 8 == b]
        if not taps:
            continue
        rows = CONV_RC + (8 if b else 0)
        vb = None
        for k in taps:
            term = w_ref[k:k + 1, lanes] * src[pl.ds(r0 + (offset_of_tap(k) - b), rows), lanes]
            vb = term if vb is None else vb + term
        vb = vb[b:b + CONV_RC] if b else vb
        out = vb if out is None else out + vb
    return out


def _conv_fwd(z, wdw, b_dw, ln_g, ln_b, wall, S, tm, group, shards):
    T = z.shape[0]
    nt = S // tm
    hb = tm // HALO
    gathered = _group_shapes(group)

    def body(cv_ref, cg_ref, cgate_ref, hcv_ref, hcg_ref, wdw_ref, bdw_ref, lng_ref, lnb_ref, wpw_ref,
             wbrc_ref, *rest):
        shard_refs, rest = rest[:len(group)], rest[len(group):]
        ya_ref, y_ref, rstd_ref, pw_ref = rest[:4]
        gather_refs, (ubuf, cbuf), gather_scratch = rest[4:4 + len(gathered)], rest[-6:-4], rest[-4:]
        t = pl.program_id(1)
        step = pl.program_id(0) * nt + t
        finish_gather = _group_gather(group, shard_refs, gather_refs, gather_scratch, step, T // tm)
        ubuf[HALO:HALO + tm, :] = cv_ref[...].astype(F32) * _sig(cg_ref[...].astype(F32))
        hu = hcv_ref[...].astype(F32) * _sig(hcg_ref[...].astype(F32))
        ubuf[0:HALO, :] = jnp.where(t > 0, hu, 0.0)
        ubuf[HALO + tm:HALO + tm + 8, :] = jnp.zeros((8, D), F32)

        def chunk(ci, carry):
            r0 = pl.multiple_of(ci * CONV_RC, CONV_RC)
            for lg in range(D // CONV_LC):
                acc = _conv_taps(wdw_ref, ubuf, r0, lg * CONV_LC, lambda k: HALO - (CONV_K - 1) + k)
                cbuf[pl.ds(r0, CONV_RC), pl.ds(lg * CONV_LC, CONV_LC)] = acc
            return carry

        lax.fori_loop(0, tm // CONV_RC, chunk, 0)
        cc = cbuf[...] + bdw_ref[...]
        mu = jnp.mean(cc, axis=-1, keepdims=True)
        dd = cc - mu
        rstd = lax.rsqrt(jnp.mean(dd * dd, axis=-1, keepdims=True) + EPS)
        yn = dd * rstd
        y_ref[...] = yn.astype(BF16)
        rstd_ref[...] = rstd
        n = yn * lng_ref[...] + lnb_ref[...]
        s = n * _sig(n)
        pw = _mm(s.astype(BF16), wpw_ref[...])
        pw_ref[...] = pw.astype(BF16)
        gt = cgate_ref[...].astype(F32)
        ya_in = pw * (gt * _sig(gt))
        ya_ref[...] = _mm(ya_in.astype(BF16), wbrc_ref[...]).astype(BF16)
        pl.when(step == T // tm - 1)(finish_gather)

    def row(b, t):
        return b * nt + t

    def halo(b, t):
        return jnp.maximum(row(b, t) * hb - 1, 0)

    vec = pl.BlockSpec((1, D), lambda b, t: (0, 0))
    tile = lambda j: pl.BlockSpec((tm, D), lambda b, t: (row(b, t), j))
    out_tile = pl.BlockSpec((tm, D), lambda b, t: (row(b, t), 0))
    any_spec = pl.BlockSpec(memory_space=pl.ANY)
    return pl.pallas_call(
        body, name="conv_fwd", grid=(T // S, nt),
        in_specs=[tile(ZB_CVAL), tile(ZB_CGLU), tile(ZB_CGATE),
                  pl.BlockSpec((HALO, D), lambda b, t: (halo(b, t), ZB_CVAL)),
                  pl.BlockSpec((HALO, D), lambda b, t: (halo(b, t), ZB_CGLU)),
                  pl.BlockSpec((32, D), lambda b, t: (0, 0)), vec, vec, vec,
                  pl.BlockSpec((D, D), lambda b, t: (0, 0)),
                  pl.BlockSpec((D, D), lambda b, t: (1, 0))] + [any_spec] * len(group),
        out_specs=(out_tile, out_tile, pl.BlockSpec((tm, 1), lambda b, t: (row(b, t), 0)), out_tile)
        + (any_spec,) * len(gathered),
        out_shape=[jax.ShapeDtypeStruct((T, D), BF16), jax.ShapeDtypeStruct((T, D), BF16),
                   jax.ShapeDtypeStruct((T, 1), F32), jax.ShapeDtypeStruct((T, D), BF16)] + gathered,
        scratch_shapes=[pltpu.VMEM((tm + HALO + 8, D), F32), pltpu.VMEM((tm, D), F32)] + _group_scratch(group),
        compiler_params=_params(("arbitrary", "arbitrary")),
    )(z, z, z, z, z, wdw, b_dw, ln_g, ln_b, wall, wall, *shards)


def _swap_matrix():
    r = lax.broadcasted_iota(jnp.int32, (BLOCK, BLOCK), 0)
    l = lax.broadcasted_iota(jnp.int32, (BLOCK, BLOCK), 1)
    lh = l & (HEAD_DIM - 1)
    half = ROPE_DIM // 2
    hit = ((lh < half) & (r == l + half)) | ((lh >= half) & (lh < ROPE_DIM) & (r == l - half))
    return jnp.where(hit, 1.0, 0.0).astype(BF16)


def _rope(tb, cos, sin, pswap):
    return tb.astype(F32) * cos + _mm(tb, pswap) * sin


def _rope_f32(tv, cos, sin, pswap):
    hi = tv.astype(BF16)
    lo = (tv - hi.astype(F32)).astype(BF16)
    return tv * cos + (_mm(hi, pswap) + _mm(lo, pswap)) * sin


def _kv_variants(kv):
    lane = lax.broadcasted_iota(jnp.int32, kv.shape, 1)
    lo = lane < HEAD_DIM
    sw = pltpu.roll(kv, HEAD_DIM, 1)
    z = jnp.zeros_like(kv)
    g0 = (jnp.where(lo, kv, z).astype(BF16), jnp.where(lo, z, sw).astype(BF16))
    g1 = (jnp.where(lo, sw, z).astype(BF16), jnp.where(lo, z, kv).astype(BF16))
    return (g0, g1)


def _band_mask(nq):
    qi = lax.broadcasted_iota(jnp.int32, (nq * BLOCK, 2 * BLOCK), 0) & (BLOCK - 1)
    sj = lax.broadcasted_iota(jnp.int32, (nq * BLOCK, 2 * BLOCK), 1)
    return (sj <= qi + BLOCK) & (sj > qi), sj


def _sink_rep(sink_ref, g, e):
    return jnp.concatenate(
        [jnp.full((BLOCK, BLOCK), sink_ref[8 * g + 2 * j + e], F32) for j in range(4)], axis=0)


def _softmax_parts(s, valid, sk):
    rows = s.shape[0]
    s = jnp.where(valid, s, -1e30)
    m = jnp.maximum(jnp.broadcast_to(jnp.max(s, axis=-1, keepdims=True), (rows, BLOCK)), sk)
    return jnp.exp(s - jnp.concatenate([m, m], axis=1)), jnp.exp(sk - m)


def _softmax_sink(s, valid, sk):
    p, ps = _softmax_parts(s, valid, sk)
    inv = 1.0 / (_mm(p.astype(BF16), jnp.ones((2 * BLOCK, BLOCK), BF16)) + ps)
    return p * jnp.concatenate([inv, inv], axis=1), ps * inv


def _attn_fwd(z, zkv, cos_t, sin_t, sinks, S, tq, group, shards):
    T = z.shape[0]
    nt = S // tq
    nq = tq // BLOCK
    gathered = _group_shapes(group)

    def body(sink_ref, q_ref, kv_ref, hkv_ref, cos_ref, sin_ref, hcos_ref, hsin_ref, *rest):
        shard_refs, o_ref = rest[:len(group)], rest[len(group)]
        t = pl.program_id(1)
        step = pl.program_id(0) * nt + t
        finish_gather = _group_gather(group, shard_refs, rest[len(group) + 1:-4], rest[-4:], step, T // tq)
        cos = cos_ref[...]
        sin = sin_ref[...]
        pswap = _swap_matrix()
        kv = jnp.concatenate([hkv_ref[...], kv_ref[...]], axis=0)
        cos_k = jnp.concatenate([hcos_ref[...], cos], axis=0)
        sin_k = jnp.concatenate([hsin_ref[...], sin], axis=0)
        kx = _kv_variants(_rope(kv[:, :BLOCK], cos_k, sin_k, pswap))
        one = jnp.ones((tq + BLOCK, BLOCK), BF16)
        vx = [[jnp.concatenate([v, one], axis=1) for v in vg] for vg in _kv_variants(kv[:, BLOCK:].astype(F32))]
        band, sj = _band_mask(4)
        qs = [(_rope(q_ref[:, 128 * hp:128 * hp + 128], cos, sin, pswap) * 0.125).astype(BF16)
              for hp in range(8)]
        for n in range(nq):
            first = (t == 0) & (n == 0)
            valid = band & (jnp.logical_not(first) | (sj >= BLOCK))
            r0 = n * BLOCK
            for g in range(2):
                lhs = jnp.concatenate([qs[4 * g + j][r0:r0 + BLOCK] for j in range(4)], axis=0)
                acc = jnp.zeros((4 * BLOCK, BLOCK), F32)
                for e in range(2):
                    s = _mm_nt(lhs, kx[g][e][r0:r0 + 2 * BLOCK])
                    p, ps = _softmax_parts(s, valid, _sink_rep(sink_ref, g, e))
                    r = _mm(p.astype(BF16), vx[g][e][r0:r0 + 2 * BLOCK])
                    acc = acc + r[:, 0:BLOCK] * (1.0 / (r[:, BLOCK:2 * BLOCK] + ps))
                for j in range(4):
                    o_ref[r0:r0 + BLOCK, 128 * (4 * g + j):128 * (4 * g + j) + 128] = (
                        acc[j * BLOCK:(j + 1) * BLOCK].astype(BF16))
        pl.when(step == T // tq - 1)(finish_gather)

    def row(b, t):
        return b * nt + t

    def halo(b, t):
        return jnp.maximum(row(b, t) * nq - 1, 0)

    any_spec = pl.BlockSpec(memory_space=pl.ANY)
    return pl.pallas_call(
        body, name="attn_fwd", grid=(T // S, nt),
        in_specs=[pl.BlockSpec(memory_space=pltpu.SMEM),
                  pl.BlockSpec((tq, D), lambda b, t: (row(b, t), ZB_Q)),
                  pl.BlockSpec((tq, 2 * BLOCK), lambda b, t: (row(b, t), 0)),
                  pl.BlockSpec((BLOCK, 2 * BLOCK), lambda b, t: (halo(b, t), 0)),
                  pl.BlockSpec((tq, BLOCK), lambda b, t: (row(b, t), 0)),
                  pl.BlockSpec((tq, BLOCK), lambda b, t: (row(b, t), 0)),
                  pl.BlockSpec((BLOCK, BLOCK), lambda b, t: (halo(b, t), 0)),
                  pl.BlockSpec((BLOCK, BLOCK), lambda b, t: (halo(b, t), 0))] + [any_spec] * len(group),
        out_specs=(pl.BlockSpec((tq, D), lambda b, t: (row(b, t), 0)),) + (any_spec,) * len(gathered),
        out_shape=[jax.ShapeDtypeStruct((T, D), BF16)] + gathered,
        scratch_shapes=_group_scratch(group),
        compiler_params=_params(("arbitrary", "arbitrary")),
    )(sinks, z, zkv, zkv, cos_t, sin_t, cos_t, sin_t, *shards)


def _tail_a(x, tgt, p, o, ya, z, ln_post, wall_b, wppt, tm):
    T = x.shape[0]
    last = T // tm - 1

    def body(x_ref, tgt_ref, p_ref, o_ref, ya_ref, ag_ref, gc_ref, ga_ref, lnp_ref, wbra_ref, wout_ref,
             wpg_ref, wppt_ref, loss_ref, dx1_ref, dm_ref, yb_ref, glnp_ref, gpack_ref, gwpp_ref,
             acc_out, acc_pg, sem):
        i = pl.program_id(0)

        @pl.when(i == 0)
        def _():
            acc_out[...] = jnp.zeros_like(acc_out)
            acc_pg[...] = jnp.zeros_like(acc_pg)
            gwpp_ref[...] = jnp.zeros_like(gwpp_ref)
            glnp_ref[...] = jnp.zeros_like(glnp_ref)
            loss_ref[...] = jnp.zeros_like(loss_ref)

        ag = ag_ref[...].astype(F32)
        yb_in = (o_ref[...].astype(F32) * (ag * _sig(ag))).astype(BF16)
        yb = _mm(yb_in, wbra_ref[...])
        yb_ref[...] = yb.astype(BF16)
        m = (_sig(gc_ref[...].astype(F32)) * ya_ref[...].astype(F32)
             + _sig(ga_ref[...].astype(F32)) * yb).astype(BF16)
        mo = _mm(m, wout_ref[...])
        r2 = lax.rsqrt(jnp.mean(mo * mo, axis=-1, keepdims=True) + EPS)
        nrm = mo * r2
        g_post = lnp_ref[...]
        x1 = x_ref[...] + nrm * g_post
        x1b = x1.astype(BF16)
        gate = _sig(_mm(x1b, wpg_ref[...]))
        pb = p_ref[...].astype(BF16)
        pp = _mm_nt(pb, wppt_ref[...])
        err = x1 + gate * pp - tgt_ref[...]
        loss_ref[...] += 0.5 * jnp.sum(jnp.sum(err * err, axis=-1, keepdims=True) * (1.0 / D),
                                       axis=0, keepdims=True)
        dx2 = err * (1.0 / D)
        dgp = (dx2 * pp * gate * (1.0 - gate)).astype(BF16)
        dpp = (dx2 * gate).astype(BF16)
        dx1 = dx2 + _mm_nt(dgp, wpg_ref[...])
        dx1_ref[...] = dx1
        acc_pg[...] += _mm_tn(x1b, dgp)
        gwpp_ref[...] += _mm_tn(dpp, pb)
        glnp_ref[...] += jnp.sum(dx1 * nrm, axis=0, keepdims=True)
        a = dx1 * g_post
        dmo = (r2 * (a - nrm * jnp.mean(a * nrm, axis=-1, keepdims=True))).astype(BF16)
        dm_ref[...] = _mm_nt(dmo, wout_ref[...]).astype(BF16)
        acc_out[...] += _mm_tn(m, dmo)

        @pl.when(i == last)
        def _():
            _flush_all([(acc_out, 3 * D, sem.at[0]), (acc_pg, 4 * D, sem.at[1])], gpack_ref)

    tile = pl.BlockSpec((tm, D), lambda i: (i, 0))
    ztile = lambda j: pl.BlockSpec((tm, D), lambda i: (i, j))
    wsq = lambda k: pl.BlockSpec((D, D), lambda i: (k, 0))
    const = lambda shp: pl.BlockSpec(shp, lambda i: (0, 0))
    any_spec = pl.BlockSpec(memory_space=pl.ANY)
    return pl.pallas_call(
        body, name="tail_a", grid=(T // tm,),
        in_specs=[tile, tile, pl.BlockSpec((tm, PLE), lambda i: (i, 0)), tile, tile, ztile(ZB_AGATE),
                  ztile(ZB_GCONV), ztile(ZB_GATTN), const((1, D)), wsq(0), wsq(1), wsq(2), const((D, PLE))],
        out_specs=(const((1, 1)), tile, tile, tile, const((1, D)), any_spec, const((D, PLE))),
        out_shape=(jax.ShapeDtypeStruct((1, 1), F32), jax.ShapeDtypeStruct((T, D), F32),
                   jax.ShapeDtypeStruct((T, D), BF16), jax.ShapeDtypeStruct((T, D), BF16),
                   jax.ShapeDtypeStruct((1, D), F32), jax.ShapeDtypeStruct((N_SHARDS, SQ_PACK, D), F32),
                   jax.ShapeDtypeStruct((D, PLE), F32)),
        scratch_shapes=[pltpu.VMEM((D, D), F32), pltpu.VMEM((D, D), F32), pltpu.SemaphoreType.DMA((2,))],
        compiler_params=_params(("arbitrary",)),
    )(x, tgt, p, o, ya, z, z, z, ln_post, wall_b, wall_b, wall_b, wppt)


def _dsilu(v, sg):
    return sg * (1.0 + v * (1.0 - sg))


def _tail_b(dm, ya, yb, o, z, pw, y, rstd, ln_g, ln_b, wall_a, wall_b, gppt, gpack, tm):
    T = dm.shape[0]
    last = T // tm - 1

    def body(dm_ref, ya_ref, yb_ref, o_ref, ag_ref, gc_ref, ga_ref, cgate_ref, pw_ref, y_ref, rstd_ref,
             lng_ref, lnb_ref, wpw_ref, wbrc_ref, wbra_ref, gppt_ref, gpack_in, dg_ref, do_ref, dc_ref,
             gvec_ref, gpack_ref, acc_bra, acc_brc, acc_pw, sem):
        i = pl.program_id(0)

        @pl.when(i == 0)
        def _():
            acc_bra[...] = jnp.zeros_like(acc_bra)
            acc_brc[...] = jnp.zeros_like(acc_brc)
            acc_pw[...] = jnp.zeros_like(acc_pw)
            gvec_ref[...] = jnp.zeros_like(gvec_ref)

        g = lng_ref[...]

        def part(rs):
            dm_v = dm_ref[rs, :].astype(F32)
            sgc = _sig(gc_ref[rs, :].astype(F32))
            sga = _sig(ga_ref[rs, :].astype(F32))
            dya = (dm_v * sgc).astype(BF16)
            dyb = (dm_v * sga).astype(BF16)
            dg_ref[rs, D:2 * D] = (dm_v * ya_ref[rs, :].astype(F32) * sgc * (1.0 - sgc)).astype(BF16)
            dg_ref[rs, 2 * D:3 * D] = (dm_v * yb_ref[rs, :].astype(F32) * sga * (1.0 - sga)).astype(BF16)
            ag = ag_ref[rs, :].astype(F32)
            sag = _sig(ag)
            sa = ag * sag
            ov = o_ref[rs, :].astype(F32)
            dyb_in = _mm_nt(dyb, wbra_ref[...])
            do_ref[rs, :] = (dyb_in * sa).astype(BF16)
            dg_ref[rs, 0:D] = (dyb_in * ov * _dsilu(ag, sag)).astype(BF16)
            gt = cgate_ref[rs, :].astype(F32)
            sgt = _sig(gt)
            sgate = gt * sgt
            pw = pw_ref[rs, :].astype(F32)
            dya_in = _mm_nt(dya, wbrc_ref[...])
            dpw = (dya_in * sgate).astype(BF16)
            dg_ref[rs, 3 * D:4 * D] = (dya_in * pw * _dsilu(gt, sgt)).astype(BF16)
            yn = y_ref[rs, :].astype(F32)
            n = yn * g + lnb_ref[...]
            sn = _sig(n)
            dn = _mm_nt(dpw, wpw_ref[...]) * _dsilu(n, sn)
            dy = dn * g
            dc = rstd_ref[rs, :] * (dy - jnp.mean(dy, axis=-1, keepdims=True)
                                    - yn * jnp.mean(dy * yn, axis=-1, keepdims=True))
            dc_ref[rs, :] = dc.astype(BF16)
            sums = (jnp.sum(dn * yn, axis=0, keepdims=True), jnp.sum(dn, axis=0, keepdims=True),
                    jnp.sum(dc, axis=0, keepdims=True))
            return ((ov * sa).astype(BF16), dyb, (pw * sgate).astype(BF16), dya, (n * sn).astype(BF16), dpw,
                    sums)

        pt = part(pl.ds(0, tm))
        acc_bra[...] += _mm_tn(pt[0], pt[1])
        acc_brc[...] += _mm_tn(pt[2], pt[3])
        acc_pw[...] += _mm_tn(pt[4], pt[5])
        for j in range(3):
            gvec_ref[j:j + 1, :] += pt[6][j]

        @pl.when(i == last)
        def _():
            _flush_all([(acc_pw, 0, sem.at[0]), (acc_brc, D, sem.at[1]), (acc_bra, 2 * D, sem.at[2]),
                        (gppt_ref, WPP0, sem.at[3])], gpack_ref)

    tile = pl.BlockSpec((tm, D), lambda i: (i, 0))
    ztile = lambda j: pl.BlockSpec((tm, D), lambda i: (i, j))
    wsq = lambda k: pl.BlockSpec((D, D), lambda i: (k, 0))
    const = lambda shp: pl.BlockSpec(shp, lambda i: (0, 0))
    any_spec = pl.BlockSpec(memory_space=pl.ANY)
    return pl.pallas_call(
        body, name="tail_b", grid=(T // tm,),
        in_specs=[tile, tile, tile, tile, ztile(ZB_AGATE), ztile(ZB_GCONV), ztile(ZB_GATTN), ztile(ZB_CGATE),
                  tile, tile, pl.BlockSpec((tm, 1), lambda i: (i, 0)), const((1, D)), const((1, D)), wsq(0),
                  wsq(1), wsq(0), const((PLE, D)), any_spec],
        out_specs=(pl.BlockSpec((tm, 4 * D), lambda i: (i, 0)), tile, tile, const((8, D)), any_spec),
        out_shape=(jax.ShapeDtypeStruct((T, 7 * D), BF16), jax.ShapeDtypeStruct((T, D), BF16),
                   jax.ShapeDtypeStruct((T, D), BF16), jax.ShapeDtypeStruct((8, D), F32),
                   jax.ShapeDtypeStruct(gpack.shape, F32)),
        input_output_aliases={17: 4},
        scratch_shapes=[pltpu.VMEM((D, D), F32), pltpu.VMEM((D, D), F32), pltpu.VMEM((D, D), F32),
                        pltpu.SemaphoreType.DMA((4,))],
        compiler_params=_params(("arbitrary",)),
    )(dm, ya, yb, o, z, z, z, z, pw, y, rstd, ln_g, ln_b, wall_a, wall_a, wall_b, gppt, gpack)


def _conv_bwd(dc, z, wdw, dz, S, tm, copies, src, landing):
    T = dc.shape[0]
    nt = S // tm
    hb = tm // HALO
    nrows = T // HALO

    def body(dc_ref, hdc_ref, cv_ref, cg_ref, hcv_ref, hcg_ref, wdw_ref, dz_in, src_ref, dz_ref, gw_ref,
             land_ref, ubuf, dcbuf, dubuf, dwacc, shbuf, send_sems, recv_sems):
        b = pl.program_id(0)
        t = pl.program_id(1)

        @pl.when((b == 0) & (t == 0))
        def _():
            dwacc[...] = jnp.zeros_like(dwacc)
            for cp in copies(src_ref, land_ref, send_sems, recv_sems):
                cp.start()

        cv = cv_ref[...].astype(F32)
        sg = _sig(cg_ref[...].astype(F32))
        ubuf[HALO:HALO + tm, :] = cv * sg
        hu = hcv_ref[...].astype(F32) * _sig(hcg_ref[...].astype(F32))
        ubuf[0:HALO, :] = jnp.where(t > 0, hu, 0.0)
        ubuf[HALO + tm:HALO + tm + 8, :] = jnp.zeros((8, D), F32)
        dcbuf[0:tm, :] = dc_ref[...].astype(F32)
        dcbuf[tm:tm + HALO, :] = jnp.where(t < nt - 1, hdc_ref[...].astype(F32), 0.0)
        dcbuf[tm + HALO:tm + HALO + 8, :] = jnp.zeros((8, D), F32)

        def chunk(ci, carry):
            r0 = pl.multiple_of(ci * CONV_RC, CONV_RC)
            for lg in range(D // CONV_LC):
                l0 = lg * CONV_LC
                dubuf[pl.ds(r0, CONV_RC), pl.ds(l0, CONV_LC)] = _conv_taps(
                    wdw_ref, dcbuf, r0, l0, lambda k: CONV_K - 1 - k)
                dcc = dcbuf[pl.ds(r0, CONV_RC), pl.ds(l0, CONV_LC)]
                zero8 = jnp.zeros((8, CONV_LC), F32)
                dcz = jnp.concatenate([zero8, dcc, zero8], axis=0)
                for bb in range(8):
                    taps = [k for k in range(CONV_K) if (HALO - (CONV_K - 1) + k) % 8 == bb]
                    if not taps:
                        continue
                    rows = CONV_RC + (8 if bb else 0)
                    if bb:
                        shbuf[bb] = dcz[8 - bb:8 - bb + rows]
                    for k in taps:
                        a8 = HALO - (CONV_K - 1) + k - bb
                        dcs = shbuf[bb] if bb else dcc
                        prod = dcs * ubuf[pl.ds(r0 + a8, rows), pl.ds(l0, CONV_LC)]
                        part = prod[0:8]
                        for q in range(1, rows // 8):
                            part = part + prod[8 * q:8 * q + 8]
                        dwacc[8 * k:8 * k + 8, pl.ds(l0, CONV_LC)] += part
            return carry

        lax.fori_loop(0, tm // CONV_RC, chunk, 0)
        du = dubuf[...]
        dz_ref[:, 0:D] = (du * sg).astype(BF16)
        dz_ref[:, D:2 * D] = (du * cv * sg * (1.0 - sg)).astype(BF16)

        @pl.when((b == pl.num_programs(0) - 1) & (t == nt - 1))
        def _():
            for k in range(32):
                gw_ref[k:k + 1, :] = jnp.sum(dwacc[8 * k:8 * k + 8, :], axis=0, keepdims=True)
            cps = copies(src_ref, land_ref, send_sems, recv_sems)
            for cp in cps:
                cp.wait_recv()
            for cp in cps:
                cp.wait_send()

    def row(b, t):
        return b * nt + t

    def prev_halo(b, t):
        return jnp.maximum(row(b, t) * hb - 1, 0)

    def next_halo(b, t):
        return jnp.minimum((row(b, t) + 1) * hb, nrows - 1)

    return pl.pallas_call(
        body, name="conv_bwd", grid=(T // S, nt),
        in_specs=[pl.BlockSpec((tm, D), lambda b, t: (row(b, t), 0)),
                  pl.BlockSpec((HALO, D), lambda b, t: (next_halo(b, t), 0)),
                  pl.BlockSpec((tm, D), lambda b, t: (row(b, t), ZB_CVAL)),
                  pl.BlockSpec((tm, D), lambda b, t: (row(b, t), ZB_CGLU)),
                  pl.BlockSpec((HALO, D), lambda b, t: (prev_halo(b, t), ZB_CVAL)),
                  pl.BlockSpec((HALO, D), lambda b, t: (prev_halo(b, t), ZB_CGLU)),
                  pl.BlockSpec((32, D), lambda b, t: (0, 0)),
                  pl.BlockSpec(memory_space=pl.ANY), pl.BlockSpec(memory_space=pl.ANY)],
        out_specs=(pl.BlockSpec((tm, 2 * D), lambda b, t: (row(b, t), ZB_CVAL // 2)),
                   pl.BlockSpec((32, D), lambda b, t: (0, 0)), pl.BlockSpec(memory_space=pl.ANY)),
        out_shape=(jax.ShapeDtypeStruct(dz.shape, BF16), jax.ShapeDtypeStruct((32, D), F32), landing),
        input_output_aliases={7: 0},
        scratch_shapes=[pltpu.VMEM((tm + HALO + 8, D), F32), pltpu.VMEM((tm + HALO + 8, D), F32),
                        pltpu.VMEM((tm, D), F32), pltpu.VMEM((8 * 32, D), F32),
                        pltpu.VMEM((8, CONV_RC + 8, CONV_LC), F32), pltpu.SemaphoreType.DMA((3,)),
                        pltpu.SemaphoreType.DMA((3,))],
        compiler_params=_params(("arbitrary", "arbitrary")),
    )(dc, dc, z, z, z, z, wdw, dz, src)


def _attn_bwd(z, zkv, o, do, cos_t, sin_t, sinks, dz, S, tq, copies, src, landing):
    T = z.shape[0]
    nt = S // tq
    nq = tq // BLOCK

    def body(sink_ref, q_ref, kv_ref, hkv_ref, o_ref, do_ref, cos_ref, sin_ref, hcos_ref, hsin_ref, dz_in,
             src_ref, dq_ref, dkv_ref, gs_ref, land_ref, carry, dkacc, dvacc, send_sems, recv_sems):
        b = pl.program_id(0)
        tt = pl.program_id(1)
        t = nt - 1 - tt

        @pl.when((b == 0) & (tt == 0))
        def _():
            gs_ref[...] = jnp.zeros_like(gs_ref)
            for cp in copies(src_ref, land_ref, send_sems, recv_sems):
                cp.start()

        @pl.when(tt == 0)
        def _():
            carry[...] = jnp.zeros_like(carry)

        cos = cos_ref[...]
        sin = sin_ref[...]
        pswap = _swap_matrix()
        kv = jnp.concatenate([hkv_ref[...], kv_ref[...]], axis=0)
        cos_k = jnp.concatenate([hcos_ref[...], cos], axis=0)
        sin_k = jnp.concatenate([hsin_ref[...], sin], axis=0)
        kx = _kv_variants(_rope(kv[:, :BLOCK], cos_k, sin_k, pswap))
        vx = _kv_variants(kv[:, BLOCK:].astype(F32))
        band, sj = _band_mask(4)
        lo = lax.broadcasted_iota(jnp.int32, (4 * BLOCK, BLOCK), 1) < HEAD_DIM
        ones = jnp.ones((2 * BLOCK, 2 * BLOCK), BF16)
        qs = [(_rope(q_ref[:, 128 * hp:128 * hp + 128], cos, sin, pswap) * 0.125).astype(BF16)
              for hp in range(8)]
        dkacc[...] = jnp.zeros_like(dkacc)
        dvacc[...] = jnp.zeros_like(dvacc)
        gsum = jnp.zeros((1, BLOCK), F32)
        hlane = lax.broadcasted_iota(jnp.int32, (1, BLOCK), 1)
        for n in range(nq):
            first = (t == 0) & (n == 0)
            valid = band & (jnp.logical_not(first) | (sj >= BLOCK))
            r0 = n * BLOCK
            for g in range(2):
                cols = [slice(128 * (4 * g + j), 128 * (4 * g + j) + 128) for j in range(4)]
                lhs = jnp.concatenate([qs[4 * g + j][r0:r0 + BLOCK] for j in range(4)], axis=0)
                dov = jnp.concatenate([do_ref[r0:r0 + BLOCK, cs] for cs in cols], axis=0)
                prod = dov.astype(F32) * jnp.concatenate(
                    [o_ref[r0:r0 + BLOCK, cs] for cs in cols], axis=0).astype(F32)
                lhs_t = lhs.T
                dov_t = dov.T
                dq = jnp.zeros((4 * BLOCK, BLOCK), F32)
                dk_t = jnp.zeros((HEAD_DIM, 2 * BLOCK), F32)
                dv_t = jnp.zeros((HEAD_DIM, 2 * BLOCK), F32)
                for e in range(2):
                    kw = kx[g][e][r0:r0 + 2 * BLOCK]
                    vw = vx[g][e][r0:r0 + 2 * BLOCK]
                    s = _mm_nt(lhs, kw)
                    p, psink = _softmax_sink(s, valid, _sink_rep(sink_ref, g, e))
                    pe = jnp.where(lo if e == 0 else jnp.logical_not(lo), prod, 0.0)
                    pe_hi = pe.astype(BF16)
                    pe_lo = (pe - pe_hi.astype(F32)).astype(BF16)
                    delta = _mm(jnp.concatenate([pe_hi, pe_lo], axis=1), ones)
                    ds = (p * (_mm_nt(dov, vw) - delta)).astype(BF16)
                    dq = dq + _mm(ds, kw)
                    dims = slice(HEAD_DIM * e, HEAD_DIM * (e + 1))
                    dk_t = dk_t + _mm(lhs_t[dims], ds)
                    dv_t = dv_t + _mm(dov_t[dims], p.astype(BF16))
                    gs = -psink * delta[:, 0:BLOCK]
                    for j in range(4):
                        tot = jnp.sum(gs[j * BLOCK:(j + 1) * BLOCK], axis=0, keepdims=True)
                        gsum = gsum + jnp.where(hlane == 8 * g + 2 * j + e, tot, 0.0)
                dkacc[HEAD_DIM * g:HEAD_DIM * (g + 1), r0:r0 + 2 * BLOCK] += dk_t
                dvacc[HEAD_DIM * g:HEAD_DIM * (g + 1), r0:r0 + 2 * BLOCK] += dv_t
                for j in range(4):
                    dqj = _rope_f32(dq[j * BLOCK:(j + 1) * BLOCK] * 0.125, cos[r0:r0 + BLOCK],
                                    -sin[r0:r0 + BLOCK], pswap)
                    dq_ref[r0:r0 + BLOCK, cols[j]] = dqj.astype(BF16)
        gs_ref[0:1, :] += gsum
        dk_all = dkacc[...]
        dv_all = dvacc[...]
        dk_last = dk_all[:, tq:tq + BLOCK] + carry[0:BLOCK, :]
        dv_last = dv_all[:, tq:tq + BLOCK] + carry[BLOCK:2 * BLOCK, :]
        carry[0:BLOCK, :] = dk_all[:, 0:BLOCK]
        carry[BLOCK:2 * BLOCK, :] = dv_all[:, 0:BLOCK]
        if nq > 1:
            dk_tile = jnp.concatenate([dk_all[:, BLOCK:tq], dk_last], axis=1)
            dv_tile = jnp.concatenate([dv_all[:, BLOCK:tq], dv_last], axis=1)
        else:
            dk_tile, dv_tile = dk_last, dv_last
        dkv_ref[:, 0:BLOCK] = _rope_f32(dk_tile.T, cos, -sin, pswap).astype(BF16)
        dkv_ref[:, BLOCK:2 * BLOCK] = dv_tile.T.astype(BF16)

        @pl.when((b == pl.num_programs(0) - 1) & (tt == nt - 1))
        def _():
            cps = copies(src_ref, land_ref, send_sems, recv_sems)
            for cp in cps:
                cp.wait_recv()
            for cp in cps:
                cp.wait_send()

    def row(b, tt):
        return b * nt + (nt - 1 - tt)

    def halo(b, tt):
        return jnp.maximum(row(b, tt) * nq - 1, 0)

    tile = pl.BlockSpec((tq, D), lambda b, tt: (row(b, tt), 0))
    return pl.pallas_call(
        body, name="attn_bwd", grid=(T // S, nt),
        in_specs=[pl.BlockSpec(memory_space=pltpu.SMEM),
                  pl.BlockSpec((tq, D), lambda b, tt: (row(b, tt), ZB_Q)),
                  pl.BlockSpec((tq, 2 * BLOCK), lambda b, tt: (row(b, tt), 0)),
                  pl.BlockSpec((BLOCK, 2 * BLOCK), lambda b, tt: (halo(b, tt), 0)),
                  tile, tile,
                  pl.BlockSpec((tq, BLOCK), lambda b, tt: (row(b, tt), 0)),
                  pl.BlockSpec((tq, BLOCK), lambda b, tt: (row(b, tt), 0)),
                  pl.BlockSpec((BLOCK, BLOCK), lambda b, tt: (halo(b, tt), 0)),
                  pl.BlockSpec((BLOCK, BLOCK), lambda b, tt: (halo(b, tt), 0)),
                  pl.BlockSpec(memory_space=pl.ANY), pl.BlockSpec(memory_space=pl.ANY)],
        out_specs=(pl.BlockSpec((tq, D), lambda b, tt: (row(b, tt), ZB_Q)),
                   pl.BlockSpec((tq, 2 * BLOCK), lambda b, tt: (row(b, tt), 0)),
                   pl.BlockSpec((8, BLOCK), lambda b, tt: (0, 0)), pl.BlockSpec(memory_space=pl.ANY)),
        out_shape=(jax.ShapeDtypeStruct(dz.shape, BF16), jax.ShapeDtypeStruct((T, 2 * BLOCK), BF16),
                   jax.ShapeDtypeStruct((8, BLOCK), F32), landing),
        input_output_aliases={10: 0},
        scratch_shapes=[pltpu.VMEM((2 * BLOCK, BLOCK), F32), pltpu.VMEM((BLOCK, tq + BLOCK), F32),
                        pltpu.VMEM((BLOCK, tq + BLOCK), F32), pltpu.SemaphoreType.DMA((3,)),
                        pltpu.SemaphoreType.DMA((3,))],
        compiler_params=_params(("arbitrary", "arbitrary")),
    )(sinks, z, zkv, zkv, o, do, cos_t, sin_t, cos_t, sin_t, dz, src)


def _dh(dz, dz_kv, wt, x, dx1, ln_pre, tm, copies, src, landing):
    T = x.shape[0]
    ntiles = T // tm
    nsem = 3

    def body(dz_ref, kv_ref, wt_ref, x_ref, dx1_ref, g_ref, src_ref, gx_ref, glp_ref, land_ref, wbuf, send_sems,
             recv_sems, wsem):
        i = pl.program_id(0)

        @pl.when(i == 0)
        def _():
            glp_ref[...] = jnp.zeros_like(glp_ref)
            for cp in copies(src_ref, land_ref, send_sems, recv_sems):
                cp.start()
            load = pltpu.make_async_copy(wt_ref, wbuf, wsem)
            load.start()
            load.wait()

        dh = _mm(dz_ref[...], wbuf[0:ZKV, :]) + _mm(kv_ref[...], wbuf[ZKV:IN_WIDTH, :])
        xv = x_ref[...]
        r = lax.rsqrt(jnp.mean(xv * xv, axis=-1, keepdims=True) + EPS)
        xr = xv * r
        glp_ref[...] += jnp.sum(dh * xr, axis=0, keepdims=True)
        a = dh * g_ref[...]
        gx_ref[...] = dx1_ref[...] + r * (a - xr * jnp.mean(a * xr, axis=-1, keepdims=True))

        @pl.when(i == ntiles - 1)
        def _():
            cps = copies(src_ref, land_ref, send_sems, recv_sems)
            for cp in cps:
                cp.wait_recv()
            for cp in cps:
                cp.wait_send()

    tile = pl.BlockSpec((tm, D), lambda i: (i, 0))
    any_spec = pl.BlockSpec(memory_space=pl.ANY)
    return pl.pallas_call(
        body, name="dh", grid=(ntiles,),
        in_specs=[pl.BlockSpec((tm, ZKV), lambda i: (i, 0)), pl.BlockSpec((tm, 2 * BLOCK), lambda i: (i, 0)),
                  any_spec, tile, tile, pl.BlockSpec((1, D), lambda i: (0, 0)), any_spec],
        out_specs=(tile, pl.BlockSpec((1, D), lambda i: (0, 0)), any_spec),
        out_shape=(jax.ShapeDtypeStruct((T, D), F32), jax.ShapeDtypeStruct((1, D), F32), landing),
        scratch_shapes=[pltpu.VMEM((IN_WIDTH, D), BF16), pltpu.SemaphoreType.DMA((nsem,)),
                        pltpu.SemaphoreType.DMA((nsem,)), pltpu.SemaphoreType.DMA],
        compiler_params=_params(("arbitrary",)),
    )(dz, dz_kv, wt, x, dx1, ln_pre, src)


def _gwt(dz, dz_kv, h, tt):
    T = dz.shape[0]
    nt = T // tt
    last = nt - 1
    kv = 2 * BLOCK
    half = WIN_SHARD // 2

    def body(dz_ref, dzkv_ref, h_ref, gpack_ref, r1_ref, hbuf, acc, hsems, sems, send_sems, recv_sems):
        j = pl.program_id(0)
        t = pl.program_id(1)
        slot = j % 2
        rows = pl.ds(pl.multiple_of(t * tt, tt), tt)
        x, y, c = _coords()

        def exchange(jj):
            wall0, n_rows = (WT0 + jj * D, D) if jj < 7 else (WT0 + ZKV, kv)
            for _, n, s, pr in _wall_segments(wall0, n_rows):
                for hb in range(2):
                    lo, hi = max(pr, hb * half), min(pr + n, (hb + 1) * half)
                    if lo < hi:
                        cp = pltpu.make_async_remote_copy(
                            src_ref=gpack_ref.at[s, pl.ds(lo, hi - lo)],
                            dst_ref=r1_ref.at[s, pl.ds(lo - hb * half, hi - lo)], send_sem=send_sems.at[0],
                            recv_sem=recv_sems.at[0], device_id=(x, y, 1 - c), device_id_type=MESH)
                        pl.when(c == 1 - hb)(cp.start)

        def h_load(i):
            return pltpu.make_async_copy(h_ref.at[pl.ds(i * tt, tt)], hbuf.at[pl.ds(i * tt, tt)], hsems.at[i])

        @pl.when((j == 0) & (t == 0))
        def _():
            for i in range(nt):
                h_load(i).start()

        for i in range(nt):
            pl.when((j == 0) & (t == i))(h_load(i).wait)

        @pl.when((j < 7) & (t == 0))
        def _():
            acc[slot] = _mm_tn(dz_ref[...], hbuf[rows, :])

        @pl.when((j < 7) & (t > 0))
        def _():
            acc[slot] += _mm_tn(dz_ref[...], hbuf[rows, :])

        @pl.when((j == 7) & (t == 0))
        def _():
            acc[1, 0:kv, :] = _mm_tn(dzkv_ref[...], hbuf[rows, :])

        @pl.when((j == 7) & (t > 0))
        def _():
            acc[1, 0:kv, :] += _mm_tn(dzkv_ref[...], hbuf[rows, :])

        def block_total(sl):
            return pltpu.make_async_copy(acc.at[sl], gpack_ref.at[0, pl.ds(0, D)], sems.at[sl])

        for jj in range(8):
            @pl.when((t == last) & (j == jj))
            def _(jj=jj):
                if jj >= 1:
                    block_total((jj - 1) % 2).wait()
                    exchange(jj - 1)
                if jj == 7:
                    _flush_to_pack(acc.at[1, pl.ds(0, kv)], gpack_ref, WT0 + ZKV, sems.at[1])
                    exchange(7)
                    whole = _exchange_copies(gpack_ref, r1_ref, send_sems, recv_sems)[0]
                    whole.wait_recv()
                    whole.wait_send()
                else:
                    for cp in _pack_copies(acc.at[jj % 2], gpack_ref, WT0 + jj * D, sems.at[jj % 2]):
                        cp.start()

    any_spec = pl.BlockSpec(memory_space=pl.ANY)
    return pl.pallas_call(
        body, name="gwt", grid=(8, nt),
        in_specs=[pl.BlockSpec((tt, D), lambda j, t: (jnp.where(j == 7, last, t), jnp.minimum(j, 6))),
                  pl.BlockSpec((tt, kv), lambda j, t: (jnp.where(j == 7, t, 0), 0)), any_spec],
        out_specs=(any_spec, any_spec),
        out_shape=(jax.ShapeDtypeStruct((N_SHARDS, WIN_SHARD, D), F32),
                   jax.ShapeDtypeStruct((N_SHARDS, half, D), F32)),
        scratch_shapes=[pltpu.VMEM((T, D), BF16), pltpu.VMEM((2, D, D), F32), pltpu.SemaphoreType.DMA((nt,)),
                        pltpu.SemaphoreType.DMA((2,)), pltpu.SemaphoreType.DMA((1,)),
                        pltpu.SemaphoreType.DMA((1,))],
        compiler_params=_params(("arbitrary", "arbitrary")),
    )(dz, dz_kv, h)


_BC1 = 1.0 - ADAM_B1 ** ADAM_STEP
_BC2 = 1.0 - ADAM_B2 ** ADAM_STEP


def _adamw_math(w, g, m, v):
    m = ADAM_B1 * m + (1.0 - ADAM_B1) * g
    v = ADAM_B2 * v + (1.0 - ADAM_B2) * (g * g)
    delta = -ADAM_LR * ((m / _BC1) / (jnp.sqrt(v / _BC2) + ADAM_EPS) + ADAM_WD * w)
    return delta, m, v


def _adamw_rows(g, w, m, v, rows, name):
    R, C = w.shape

    def body(g_ref, w_ref, m_ref, v_ref, go_ref, d_ref, nm_ref, nv_ref):
        gv = g_ref[...]
        d, nm, nv = _adamw_math(w_ref[...], gv, m_ref[...], v_ref[...])
        go_ref[...] = gv
        d_ref[...] = d
        nm_ref[...] = nm
        nv_ref[...] = nv

    spec = pl.BlockSpec((rows, C), lambda i: (i, 0))
    shp = jax.ShapeDtypeStruct((R, C), F32)
    return pl.pallas_call(
        body, name=name, grid=(R // rows,), in_specs=[spec] * 4, out_specs=(spec,) * 4,
        out_shape=(shp,) * 4, compiler_params=_params(("arbitrary",)),
    )(g, w, m, v)


def _adamw_square(gfin, ws, ms, vs):
    rb = 64
    nb = SQ_SHARD // rb

    def body(*refs):
        g_refs = refs[0:5]
        w_refs, m_refs, v_refs = refs[5:10], refs[10:15], refs[15:20]
        outs = refs[20:]
        for k in range(5):
            gk = g_refs[k][...]
            d, nm, nv = _adamw_math(w_refs[k][...], gk, m_refs[k][...], v_refs[k][...])
            outs[4 * k][...] = gk
            outs[4 * k + 1][...] = d
            outs[4 * k + 2][...] = nm
            outs[4 * k + 3][...] = nv

    spec = pl.BlockSpec((rb, D), lambda i: (i, 0))
    gspecs = [pl.BlockSpec((rb, D), lambda i, k=k: (SQ_SHARD * k // rb + i, 0)) for k in range(5)]
    shp = jax.ShapeDtypeStruct((SQ_SHARD, D), F32)
    res = pl.pallas_call(
        body, name="adamw_square", grid=(nb,), in_specs=gspecs + [spec] * 15, out_specs=(spec,) * 20,
        out_shape=(shp,) * 20, compiler_params=_params(("arbitrary",)),
    )(*([gfin] * 5), *ws, *ms, *vs)
    return [tuple(res[4 * k:4 * k + 4]) for k in range(5)]


def _adamw_small(gs, ws, ms, vs):
    n = len(gs)

    def body(*refs):
        outs = refs[4 * n:]
        for k in range(n):
            d, nm, nv = _adamw_math(refs[n + k][...], refs[k][...], refs[2 * n + k][...],
                                    refs[3 * n + k][...])
            outs[3 * k][...] = d
            outs[3 * k + 1][...] = nm
            outs[3 * k + 2][...] = nv

    vm = pl.BlockSpec(memory_space=pltpu.VMEM)
    shapes = []
    for w in ws:
        shapes += [jax.ShapeDtypeStruct(w.shape, F32)] * 3
    res = pl.pallas_call(
        body, name="adamw_small", in_specs=[vm] * (4 * n), out_specs=(vm,) * (3 * n),
        out_shape=tuple(shapes),
    )(*gs, *ws, *ms, *vs)
    return [tuple(res[3 * k:3 * k + 3]) for k in range(n)]


def _rope_constants():
    half = ROPE_DIM // 2
    inv = jnp.power(ROPE_THETA, -jnp.arange(0, ROPE_DIM, 2, dtype=F32) / ROPE_DIM)
    freq = jnp.concatenate([inv, jnp.zeros((ROPE_ROWS - half,), F32)]).reshape(ROPE_ROWS, 1)
    spread = np.zeros((3, ROPE_ROWS, BLOCK), np.float32)
    for lane in range(BLOCK):
        d = lane % HEAD_DIM
        if d < ROPE_DIM:
            spread[0, d % half, lane] = 1.0
            spread[1, d % half, lane] = -1.0 if d < half else 1.0
        else:
            spread[2, 0, lane] = 1.0
    return freq, jnp.asarray(spread, BF16)


def kernel(x, p, positions, w_in, ln_pre, ln_post, w_dw, b_dw, conv_ln_g, conv_ln_b, w_pw, sinks, w_br_conv, w_br_attn, w_out, w_ple_gate, w_ple_proj, loss_target, m_w_in, m_ln_pre, m_ln_post, m_w_dw, m_b_dw, m_conv_ln_g, m_conv_ln_b, m_w_pw, m_sinks, m_w_br_conv, m_w_br_attn, m_w_out, m_w_ple_gate, m_w_ple_proj, v_w_in, v_ln_pre, v_ln_post, v_w_dw, v_b_dw, v_conv_ln_g, v_conv_ln_b, v_w_pw, v_sinks, v_w_br_conv, v_w_br_attn, v_w_out, v_w_ple_gate, v_w_ple_proj):
    nb, S, _ = x.shape
    T = nb * S
    xc = lax.axis_index("x")
    yc = lax.axis_index("y")
    cc = lax.axis_index("c")
    shard = 2 * xc + yc

    sq_w = (w_pw, w_br_conv, w_br_attn, w_out, w_ple_gate)
    wdw_shard = jnp.pad(w_dw[0], ((0, 1), (0, 0)))
    x2 = x.reshape(T, D)

    tgt = loss_target.reshape(T, D)
    p2 = p.reshape(T, PLE)
    sinks1 = sinks.reshape(N_HEADS)

    tm = min(TILE_TOKEN, S)
    tc = min(TILE_CONV, S)
    tq = min(TILE_ATTN, S)

    z, zkv, wt, wdw_all, h, cos_t, sin_t = _inproj(
        x2, ln_pre, positions.astype(F32).reshape(1, T), *_rope_constants(), w_in[0].T.astype(BF16), wdw_shard,
        min(TILE_PROJ, T // 2))
    wdw = jnp.concatenate([wdw_all[s] for s in range(N_SHARDS)], axis=1)
    sq_shards = [w[0].astype(BF16) for w in sq_w] + [w_ple_proj[0].T.reshape(WPP_SHARD, D).astype(BF16)]
    o, wall_a = _attn_fwd(z, zkv, cos_t, sin_t, sinks1, S, tq, GROUP_CONV, sq_shards[0:2])
    ya, y, rstd, pw, wall_b, wppf = _conv_fwd(z, wdw, b_dw, conv_ln_g, conv_ln_b, wall_a, S, tc, GROUP_TAIL,
                                              sq_shards[2:])
    wppt = wppf.reshape(D, PLE)
    loss_p, dx1, dm, yb, g_ln_post, gsq, gw_ppt = _tail_a(x2, tgt, p2, o, ya, z, ln_post, wall_b, wppt, tm)

    cidx = jnp.reshape(cc, (1,)).astype(jnp.int32)
    scidx = jnp.stack([shard, cc]).astype(jnp.int32)

    def landing(pack, n, dtype):
        return jax.ShapeDtypeStruct((n, pack.shape[1] // 2, D), dtype)

    dz, do, dc, gvec, gsq = _tail_b(dm, ya, yb, o, z, pw, y, rstd, conv_ln_g, conv_ln_b, wall_a, wall_b,
                                    gw_ppt.reshape(PLE, D), gsq, tm)
    dz, g_wdw, r1_sq = _conv_bwd(dc, z, wdw, dz, S, tc, _exchange_copies, gsq, landing(gsq, N_SHARDS, F32))
    cs_sq = _chip_sum(cidx, gsq, r1_sq, "chip_sum_sq")
    dz, dkv, g_sinks, r2_sq = _attn_bwd(z, zkv, o, do, cos_t, sin_t, sinks1, dz, S, tq, _chip_sum_copies, cs_sq,
                                        landing(gsq, 3, BF16))
    gwt_pack, r1_wt = _gwt(dz, dkv, h, min(2 * TILE_PROJ, T))
    cs_wt = _chip_sum(cidx, gwt_pack, r1_wt, "chip_sum_wt")
    gx, g_ln_pre, r2_wt = _dh(dz, dkv, wt, x2, dx1, ln_pre, min(TILE_RESIDENT, T // 2), _chip_sum_copies, cs_wt,
                              landing(gwt_pack, 3, BF16))
    row37 = jnp.concatenate([g_sinks[0:1, 0:N_HEADS], loss_p, jnp.zeros((1, D - N_HEADS - 1), F32)], axis=1)
    vec = jnp.concatenate([g_wdw, g_ln_pre, g_ln_post, gvec[2:3], gvec[0:1], gvec[1:2], row37,
                           jnp.zeros((VEC_ROWS - 38, D), F32)], axis=0)
    gfin_wt, gfin_sq, tot = _finish_reduce(_final_half(scidx, gwt_pack, r1_wt, r2_wt, "final_half_wt"),
                                           _final_half(scidx, gsq, r1_sq, r2_sq, "final_half_sq"), vec)

    g_w_in, d_w_in, nm_w_in, nv_w_in = [a.T for a in _adamw_rows(
        gfin_wt, w_in[0].T, m_w_in[0].T, v_w_in[0].T, WIN_SHARD // 8, "adamw_w_in")]
    g_w_in = g_w_in[None]
    sq_m = (m_w_pw, m_w_br_conv, m_w_br_attn, m_w_out, m_w_ple_gate)
    sq_v = (v_w_pw, v_w_br_conv, v_w_br_attn, v_w_out, v_w_ple_gate)
    sq_res = _adamw_square(gfin_sq, [w[0] for w in sq_w], [m[0] for m in sq_m], [v[0] for v in sq_v])
    g_wpp = gfin_sq[5 * SQ_SHARD:SQ_PACK].reshape(PLE, PLE).T
    g_dw_all = tot[0:CONV_K]
    g_dw = lax.dynamic_slice_in_dim(g_dw_all, shard * PLE, PLE, axis=1)
    small_g = [g_wpp, g_dw, tot[32:33], tot[33:34], tot[34:35], tot[35:36], tot[36:37],
               tot[37:38, 0:N_HEADS]]
    small_w = [w_ple_proj[0], w_dw[0], ln_pre, ln_post, b_dw, conv_ln_g, conv_ln_b, sinks]
    small_m = [m_w_ple_proj[0], m_w_dw[0], m_ln_pre, m_ln_post, m_b_dw, m_conv_ln_g, m_conv_ln_b, m_sinks]
    small_v = [v_w_ple_proj[0], v_w_dw[0], v_ln_pre, v_ln_post, v_b_dw, v_conv_ln_g, v_conv_ln_b, v_sinks]
    small = _adamw_small(small_g, small_w, small_m, small_v)

    loss = tot[37, N_HEADS]
    grads = [g_w_in, small_g[2], small_g[3], g_dw[None], small_g[4], small_g[5], small_g[6],
             sq_res[0][0][None], small_g[7], sq_res[1][0][None], sq_res[2][0][None], sq_res[3][0][None],
             sq_res[4][0][None], g_wpp[None]]

    def triple(i):
        w_in_t = (d_w_in[None], nm_w_in[None], nv_w_in[None])
        sq = lambda k: tuple(a[None] for a in sq_res[k][1:4])
        sm = lambda k, lead: tuple(a[None] if lead else a for a in small[k])
        return [w_in_t[i], sm(2, False)[i], sm(3, False)[i], sm(1, True)[i], sm(4, False)[i],
                sm(5, False)[i], sm(6, False)[i], sq(0)[i], sm(7, False)[i], sq(1)[i], sq(2)[i], sq(3)[i],
                sq(4)[i], sm(0, True)[i]]

    return (loss, gx.reshape(nb, S, D), *grads, *triple(0), *triple(1), *triple(2))
```

```python
import functools

import jax
import jax.numpy as jnp
import numpy as np
from jax import lax
from jax.experimental import pallas as pl
from jax.experimental.pallas import tpu as pltpu

F32 = jnp.float32
BF16 = jnp.bfloat16

D = 1024
PLE = 256
N_HEADS = 16
HEAD_DIM = 64
BLOCK = 128
CONV_K = 31
ROPE_DIM = 16
ROPE_THETA = 500000.0
EPS = 1e-6
IN_WIDTH = 7424
N_SHARDS = 4

ADAM_LR = 0.001
ADAM_B1 = 0.9
ADAM_B2 = 0.999
ADAM_EPS = 1e-08
ADAM_WD = 0.01
ADAM_STEP = 10

WT0 = 5 * D
WPP0 = WT0 + IN_WIDTH
WIN_SHARD = IN_WIDTH // N_SHARDS
SQ_SHARD = D // N_SHARDS
WPP_SHARD = PLE * PLE // D
PACK_ROWS = WIN_SHARD + 5 * SQ_SHARD + WPP_SHARD
VMEM_LIMIT = 56 * 1024 * 1024
MESH = pl.DeviceIdType.MESH
TILE_RESIDENT = 512
TILE_PROJ = 1024
TILE_TOKEN = 256
TILE_CONV = 512
TILE_ATTN = 1024


ZB_AGATE, ZB_GCONV, ZB_GATTN, ZB_CGATE, ZB_CVAL, ZB_CGLU, ZB_Q = range(7)
ZKV = 7 * D
_SEGMENTS = ((0, D, ZB_CVAL * D), (D, D, ZB_CGLU * D), (2 * D, D, ZB_CGATE * D), (3 * D, D, ZB_Q * D),
             (4 * D, 2 * BLOCK, ZKV), (4 * D + 2 * BLOCK, D, ZB_AGATE * D),
             (5 * D + 2 * BLOCK, D, ZB_GCONV * D), (6 * D + 2 * BLOCK, D, ZB_GATTN * D))
_WT_CUTS = (0, 192, 640, 1216, WIN_SHARD)


def _zp_row(o):
    for a, w, zp in _SEGMENTS:
        if a <= o < a + w:
            return zp + o - a
    raise ValueError(o)


def _pieces(s):
    out = []
    for a, b in zip(_WT_CUTS[:-1], _WT_CUTS[1:]):
        first = _zp_row(WIN_SHARD * s + a)
        assert _zp_row(WIN_SHARD * s + b - 1) == first + b - a - 1
        out.append((a, b - a, WT0 + first))
    for k in range(5):
        out.append((WIN_SHARD + SQ_SHARD * k, SQ_SHARD, D * k + SQ_SHARD * s))
    out.append((WIN_SHARD + 5 * SQ_SHARD, WPP_SHARD, WPP0 + WPP_SHARD * s))
    return out


N_PIECES = len(_pieces(0))


def _wall_segments(wall0, rows):
    out = []
    for s in range(N_SHARDS):
        for pr, n, wr in _pieces(s):
            lo, hi = max(wr, wall0), min(wr + n, wall0 + rows)
            if lo < hi:
                out.append((lo - wall0, hi - lo, s, pr + lo - wr))
    assert sum(n for _, n, _, _ in out) == rows
    return out


def _sel(s, vals):
    r = jnp.int32(vals[0])
    for i in range(1, len(vals)):
        r = jnp.where(s == i, jnp.int32(vals[i]), r)
    return r


def _sig(x):
    return 1.0 / (1.0 + jnp.exp(-x))


def _mm(a, b):
    return lax.dot_general(a, b, (((1,), (0,)), ((), ())), preferred_element_type=F32)


def _mm_nt(a, b):
    return lax.dot_general(a, b, (((1,), (1,)), ((), ())), preferred_element_type=F32)


def _mm_tn(a, b):
    return lax.dot_general(a, b, (((0,), (0,)), ((), ())), preferred_element_type=F32)


def _params(sem=None):
    return pltpu.CompilerParams(dimension_semantics=sem, vmem_limit_bytes=VMEM_LIMIT)


def _flush_to_pack(acc_ref, gpack_ref, wall0, sem):
    for cp in _pack_copies(acc_ref, gpack_ref, wall0, sem):
        cp.start()
        cp.wait()


def _flush_all(items, gpack_ref):
    for acc_ref, wall0, sem in items:
        for cp in _pack_copies(acc_ref, gpack_ref, wall0, sem):
            cp.start()
    for acc_ref, _, sem in items:
        pltpu.make_async_copy(acc_ref, gpack_ref.at[0, pl.ds(0, acc_ref.shape[0])], sem).wait()


def _pack_copies(acc_ref, gpack_ref, wall0, sem):
    base = 0 if gpack_ref.shape[1] == WIN_SHARD else WIN_SHARD
    out = []
    for r, n, s, pr in _wall_segments(wall0, acc_ref.shape[0]):
        assert 0 <= pr - base and pr - base + n <= gpack_ref.shape[1]
        out.append(pltpu.make_async_copy(acc_ref.at[pl.ds(r, n)], gpack_ref.at[s, pl.ds(pr - base, n)], sem))
    return out


def _coords():
    return lax.axis_index("x"), lax.axis_index("y"), lax.axis_index("c")


def _chip_peers(x, y):
    return [(1 - x, y), (x, 1 - y), (1 - x, 1 - y)]


WIN_PIECES = tuple(range(len(_WT_CUTS) - 1))
SQ_PIECES = tuple(range(len(WIN_PIECES), N_PIECES))


def _gather_ops(group, src, landing, bytes_ref, stage, send_sems, recv_sems, loc_sem):
    sizes = [_pieces(0)[p][1] for p in group]
    half_rows = sum(n // 2 for n in sizes)
    starts = [sum(sizes[:i]) for i in range(len(sizes))]

    def rcopy(a, b, k, dev):
        return pltpu.make_async_remote_copy(src_ref=a, dst_ref=b, send_sem=send_sems.at[k],
                                            recv_sem=recv_sems.at[k], device_id=dev, device_id_type=MESH)

    def total(k):
        x, y, c = _coords()
        rows = bytes_ref.at[pl.ds(0, half_rows)]
        return rcopy(rows, rows, k, (x, y, c))

    def own_total():
        rows = stage.at[pl.ds(0, sum(sizes))]
        return pltpu.make_async_copy(rows, rows, loc_sem)

    def send():
        x, y, c = _coords()
        s_me = 2 * x + y
        for k, (px, py) in enumerate(_chip_peers(x, y)):
            for p, n in zip(group, sizes):
                h = n // 2
                rcopy(src(p, c * h, h), landing(p, s_me, c * h, h), k, (px, py, c)).start()
        for p, n, r in zip(group, sizes, starts):
            pltpu.make_async_copy(src(p, 0, n), stage.at[pl.ds(r, n)], loc_sem).start()

    def forward():
        x, y, c = _coords()
        own_total().wait()
        for p, n, r in zip(group, sizes, starts):
            pltpu.make_async_copy(stage.at[pl.ds(r, n)], landing(p, 2 * x + y, 0, n), loc_sem).start()
        for k, (px, py) in enumerate(_chip_peers(x, y)):
            total(k).wait_recv()
            for p, n in zip(group, sizes):
                rows = landing(p, 2 * px + py, c * (n // 2), n // 2)
                rcopy(rows, rows, 3 + k, (x, y, 1 - c)).start()

    def finish():
        own_total().wait()
        for k in range(3):
            total(3 + k).wait_recv()
        for k in range(6):
            total(k).wait_send()

    return send, forward, finish


def _piece_rows(ref, start, off, n):
    first = start + off
    return ref.at[pl.ds(first if isinstance(first, int) else pl.multiple_of(first, 32), n)]


GROUP_CONV = SQ_PIECES[0:2]
GROUP_TAIL = SQ_PIECES[2:]


def _group_shapes(group):
    n_sq = sum(1 for q in group if q != N_PIECES - 1)
    return [jax.ShapeDtypeStruct((n_sq * D, D), BF16)] + (
        [jax.ShapeDtypeStruct((PLE, D), BF16)] if N_PIECES - 1 in group else [])


def _group_scratch(group):
    return [pltpu.VMEM((sum(_pieces(0)[q][1] for q in group), D), BF16), pltpu.SemaphoreType.DMA((6,)),
            pltpu.SemaphoreType.DMA((6,)), pltpu.SemaphoreType.DMA]


def _group_gather(group, shard_refs, out_refs, scratch, step, n_steps):
    wall_ref = out_refs[0]
    stage, send_sems, recv_sems, loc_sem = scratch

    def src(q, off, n):
        return _piece_rows(shard_refs[group.index(q)], 0, off, n)

    def landing(q, s, off, n):
        if q == N_PIECES - 1:
            return _piece_rows(out_refs[1], WPP_SHARD * s, off, n)
        return _piece_rows(wall_ref, D * group.index(q) + SQ_SHARD * s, off, n)

    send, forward, finish = _gather_ops(group, src, landing, wall_ref, stage, send_sems, recv_sems, loc_sem)
    pl.when(step == 0)(send)
    pl.when(step == n_steps // 2)(forward)
    return finish


ROPE_ROWS = 16


def _rope_tables(pos, freq, spread_ref):
    def to_lanes(v, e):
        out = None
        for _ in range(3):
            part = v.astype(BF16)
            term = _mm_tn(part, e)
            out = term if out is None else out + term
            v = v - part.astype(F32)
        return out

    ang = freq * pos
    return (to_lanes(jnp.cos(ang), spread_ref[0]) + spread_ref[2, 0:1, :].astype(F32),
            to_lanes(jnp.sin(ang), spread_ref[1]))


SQ_PACK = PACK_ROWS - WIN_SHARD


def _row_tile(half):
    return max(t for t in range(8, 321, 8) if half % t == 0)


def _exchange_copies(g_ref, r1_ref, send_sems, recv_sems):
    x, y, c = _coords()
    half = g_ref.shape[1] // 2
    return [pltpu.make_async_remote_copy(
        src_ref=g_ref.at[:, pl.ds(pl.multiple_of((1 - c) * half, 32), half), :], dst_ref=r1_ref,
        send_sem=send_sems.at[0], recv_sem=recv_sems.at[0], device_id=(x, y, 1 - c), device_id_type=MESH)]


def _chip_sum_copies(cs_ref, r2_ref, send_sems, recv_sems):
    x, y, c = _coords()
    return [pltpu.make_async_remote_copy(
        src_ref=cs_ref.at[2 * px + py], dst_ref=r2_ref.at[k], send_sem=send_sems.at[k],
        recv_sem=recv_sems.at[k], device_id=(px, py, c), device_id_type=MESH)
        for k, (px, py) in enumerate(_chip_peers(x, y))]


def _chip_sum(cidx, gpack, r1, name):
    half = gpack.shape[1] // 2
    rt = _row_tile(half)

    def body(c_ref, g_ref, r_ref, o_ref):
        o_ref[...] = (g_ref[...] + r_ref[...]).astype(BF16)

    nt = half // rt
    return pl.pallas_call(
        body, name=name,
        grid_spec=pltpu.PrefetchScalarGridSpec(
            num_scalar_prefetch=1, grid=(N_SHARDS, nt),
            in_specs=[pl.BlockSpec((1, rt, D), lambda s, t, c: (s, c[0] * nt + t, 0)),
                      pl.BlockSpec((1, rt, D), lambda s, t, c: (s, t, 0))],
            out_specs=pl.BlockSpec((1, rt, D), lambda s, t, c: (s, t, 0))),
        out_shape=jax.ShapeDtypeStruct((N_SHARDS, half, D), BF16),
        compiler_params=_params(("arbitrary", "arbitrary")),
    )(cidx, gpack, r1)


def _final_half(sc, gpack, r1, r2, name):
    rows = gpack.shape[1]
    half = rows // 2
    rt = _row_tile(half)

    def body(sc_ref, g_ref, r_ref, p_ref, o_ref):
        acc = g_ref[0] + r_ref[0]
        for k in range(3):
            acc = acc + p_ref[k].astype(F32)
        o_ref[...] = acc

    nt = half // rt
    return pl.pallas_call(
        body, name=name,
        grid_spec=pltpu.PrefetchScalarGridSpec(
            num_scalar_prefetch=1, grid=(nt,),
            in_specs=[pl.BlockSpec((1, rt, D), lambda t, sc: (sc[0], sc[1] * nt + t, 0)),
                      pl.BlockSpec((1, rt, D), lambda t, sc: (sc[0], t, 0)),
                      pl.BlockSpec((3, rt, D), lambda t, sc: (0, t, 0))],
            out_specs=pl.BlockSpec((rt, D), lambda t, sc: (sc[1] * nt + t, 0))),
        out_shape=jax.ShapeDtypeStruct((rows, D), F32),
        compiler_params=_params(("arbitrary",)),
    )(sc, gpack, r1, r2)


VEC_ROWS = 40


def _finish_reduce(fwt, fsq, vec):
    def body(fwt_ref, fsq_ref, v_ref, owt_ref, osq_ref, tot_ref, buf, send_sems, recv_sems):
        x, y, c = _coords()
        swaps = []
        for k, (f_ref, o_ref) in enumerate(((fwt_ref, owt_ref), (fsq_ref, osq_ref))):
            half = f_ref.shape[0] // 2
            rows = pl.ds(pl.multiple_of(c * half, 32), half)
            swaps.append(pltpu.make_async_remote_copy(
                src_ref=f_ref.at[rows], dst_ref=o_ref.at[rows], send_sem=send_sems.at[7 + k],
                recv_sem=recv_sems.at[7 + k], device_id=(x, y, 1 - c), device_id_type=MESH))
        for cp in swaps:
            cp.start()
        me = 4 * x + 2 * y + c
        buf[me] = v_ref[...]
        cps = []
        for r in range(1, 8):
            dx, dy, dc = (r >> 2) & 1, (r >> 1) & 1, r & 1
            peer = (1 - x if dx else x, 1 - y if dy else y, 1 - c if dc else c)
            cp = pltpu.make_async_remote_copy(
                src_ref=v_ref, dst_ref=buf.at[me], send_sem=send_sems.at[r - 1],
                recv_sem=recv_sems.at[r - 1], device_id=peer, device_id_type=MESH)
            cp.start()
            cps.append(cp)
        for cp in cps:
            cp.wait_recv()
        for cp in cps:
            cp.wait_send()
        acc = buf[0]
        for d in range(1, 8):
            acc = acc + buf[d]
        tot_ref[...] = acc
        for cp in swaps:
            cp.wait()

    any_spec = pl.BlockSpec(memory_space=pl.ANY)
    vm = pl.BlockSpec(memory_space=pltpu.VMEM)
    return pl.pallas_call(
        body, name="finish_reduce",
        out_shape=(jax.ShapeDtypeStruct(fwt.shape, F32), jax.ShapeDtypeStruct(fsq.shape, F32),
                   jax.ShapeDtypeStruct((VEC_ROWS, D), F32)),
        in_specs=[any_spec, any_spec, vm], out_specs=(any_spec, any_spec, vm),
        input_output_aliases={0: 0, 1: 1},
        scratch_shapes=[pltpu.VMEM((8, VEC_ROWS, D), F32), pltpu.SemaphoreType.DMA((9,)),
                        pltpu.SemaphoreType.DMA((9,))],
    )(fwt, fsq, vec)


SOLO_ROWS = WIN_SHARD - BLOCK // 2


def _solo_first(s):
    return 0 if s % 2 == 0 else BLOCK // 2


def _solo_segments(s):
    lo = _solo_first(s)
    out = []
    for a, n, wr in _pieces(s)[:len(WIN_PIECES)]:
        b0, b1 = max(a, lo), min(a + n, lo + SOLO_ROWS)
        if b0 >= b1:
            continue
        z0 = wr - WT0 + b0 - a
        if out and out[-1][0] + out[-1][1] == b0 - lo and out[-1][2] + out[-1][1] == z0:
            out[-1] = (out[-1][0], out[-1][1] + b1 - b0, out[-1][2])
        else:
            out.append((b0 - lo, b1 - b0, z0))
    out = [r for o, n, z0 in out for r in
           (((o, ZKV - z0, z0), (o + ZKV - z0, z0 + n - ZKV, ZKV)) if z0 < ZKV < z0 + n else ((o, n, z0),))]
    assert all(v % BLOCK == 0 for seg in out for v in seg) and sum(n for _, n, _ in out) == SOLO_ROWS
    return out


def _shared_tile(pair):
    z0 = _zp_row(WIN_SHARD * (2 * pair) + SOLO_ROWS)
    assert z0 % BLOCK == 0 and _zp_row(WIN_SHARD * (2 * pair + 1)) == z0 + BLOCK // 2
    return z0


def _inproj(x, ln_pre, pos, freq, spread, win_t, wdw_shard, tm):
    T = x.shape[0]
    n_t = T // tm
    assert n_t >= 2 and n_t % 2 == 0
    tables = [[_pieces(s)[p][2] - WT0 for s in range(N_SHARDS)] for p in WIN_PIECES]
    sizes = [_pieces(0)[p][1] for p in WIN_PIECES]
    half_rows = sum(n // 2 for n in sizes)
    relation_of_pass = {1: 1, 2: 0, 3: 2}

    def body(x_ref, g_ref, pos_ref, f_ref, e_ref, win_ref, wdw_ref, z_ref, zkv_ref, wt_ref, wdwall_ref, h_ref,
             cos_ref, sin_ref, wbuf, stage, stage_sh, hbuf, wsend, wrecv, loc_sems, out_sems, sh_sems, h_sems):
        p = pl.program_id(0)
        t = pl.program_id(1)
        x, y, c = _coords()
        s_me = 2 * x + y
        peers = _chip_peers(x, y)
        shard = jnp.bitwise_xor(s_me, p)
        first, last = t == 0, t == n_t - 1

        def rcopy(a, b, k, dev):
            return pltpu.make_async_remote_copy(src_ref=a, dst_ref=b, send_sem=wsend.at[k], recv_sem=wrecv.at[k],
                                                device_id=dev, device_id_type=MESH)

        def total(k):
            rows = wt_ref.at[pl.ds(0, half_rows)]
            return rcopy(rows, rows, k, (x, y, c))

        def in_hbm(q, s, off, n):
            return _piece_rows(wt_ref, _sel(s, tables[q]), off, n)

        def in_vmem(q, s):
            return _piece_rows(wbuf, WIN_SHARD * s + _WT_CUTS[q], 0, sizes[q])

        def send_to(k):
            px, py = peers[k]
            for q, n in zip(WIN_PIECES, sizes):
                rcopy(_piece_rows(win_ref, _WT_CUTS[q], c * (n // 2), n // 2), in_hbm(q, s_me, c * (n // 2), n // 2),
                      k, (px, py, c)).start()

        def forward_from(k):
            px, py = peers[k]
            total(k).wait_recv()
            for q, n in zip(WIN_PIECES, sizes):
                rows = in_hbm(q, 2 * px + py, c * (n // 2), n // 2)
                rcopy(rows, rows, 3 + k, (x, y, 1 - c)).start()

        def shard_total(a, b, sem):
            return pltpu.make_async_copy(a.at[pl.ds(0, WIN_SHARD)], b.at[pl.ds(0, WIN_SHARD)], sem)

        def wdw_copies():
            return [pltpu.make_async_remote_copy(
                src_ref=wdw_ref, dst_ref=wdwall_ref.at[s_me], send_sem=wsend.at[6 + k], recv_sem=wrecv.at[6 + k],
                device_id=(px, py, c), device_id_type=MESH) for k, (px, py) in enumerate(peers)]

        def own_wdw():
            return pltpu.make_async_copy(wdw_ref, wdwall_ref.at[s_me], loc_sems.at[2])

        @pl.when((p == 0) & first)
        def _():
            send_to(0)
            send_to(1)
            own_wdw().start()
            for cp in wdw_copies():
                cp.start()
            for q in WIN_PIECES:
                pltpu.make_async_copy(_piece_rows(win_ref, _WT_CUTS[q], 0, sizes[q]), in_vmem(q, s_me),
                                      loc_sems.at[0]).start()

        for pp, k in relation_of_pass.items():
            pl.when((p == pp - 1) & (t == n_t - 2))(functools.partial(forward_from, k))

            @pl.when((p == pp - 1) & last)
            def _(k=k):
                total(3 + k).wait_recv()
                px, py = peers[k]
                for q in WIN_PIECES:
                    pltpu.make_async_copy(in_hbm(q, 2 * px + py, 0, sizes[q]), in_vmem(q, 2 * px + py),
                                          loc_sems.at[0]).start()

            @pl.when((p == pp) & first)
            def _(pp=pp):
                shard_total(wt_ref, wbuf, loc_sems.at[0]).wait()
                if pp == 1:
                    total(0).wait_send()
                    total(1).wait_send()
                    send_to(2)

        step = p * n_t + t
        slot = step % 2
        rows = pl.ds(pl.multiple_of(t * tm, tm), tm)

        def out_total(sl):
            return pltpu.make_async_copy(stage.at[sl], stage.at[sl], out_sems.at[sl])

        def sh_copy(sl, z0):
            return pltpu.make_async_copy(stage_sh.at[sl], z_ref.at[rows, pl.ds(z0, BLOCK)], sh_sems.at[sl])

        @pl.when(step >= 2)
        def _():
            out_total(slot).wait()

        @pl.when((step >= 2) & (((step - 2) // n_t) % 2 == 1))
        def _():
            sh_copy(slot, 0).wait()

        def h_out(sl):
            return pltpu.make_async_copy(hbuf.at[sl], h_ref.at[rows], h_sems.at[sl])

        def h_in(sl, tile):
            return pltpu.make_async_copy(h_ref.at[pl.ds(pl.multiple_of(tile * tm, tm), tm)], hbuf.at[sl],
                                         h_sems.at[sl])

        @pl.when((p == 0) & (t >= 2))
        def _():
            h_out(slot).wait()

        @pl.when(p == 0)
        def _():
            xv = x_ref[...]
            r = lax.rsqrt(jnp.mean(xv * xv, axis=-1, keepdims=True) + EPS)
            hbuf[slot] = (xv * r * g_ref[...]).astype(BF16)
            h_out(slot).start()
            cos, sin = _rope_tables(pos_ref[...], f_ref[...], e_ref)
            cos_ref[...] = cos
            sin_ref[...] = sin

        @pl.when((p == 1) & first)
        def _():
            h_out(0).wait()
            h_out(1).wait()
            h_in(0, 0).start()

        @pl.when(p >= 1)
        def _():
            h_in(slot, t).wait()

        @pl.when((p >= 1) & (step < N_SHARDS * n_t - 1))
        def _():
            h_in(1 - slot, jnp.where(last, 0, t + 1)).start()

        @pl.when((p == 0) & first)
        def _():
            shard_total(win_ref, wbuf, loc_sems.at[0]).wait()
            for q in WIN_PIECES:
                pltpu.make_async_copy(in_vmem(q, s_me), in_hbm(q, s_me, 0, sizes[q]), loc_sems.at[1]).start()

        solo0 = pl.multiple_of(WIN_SHARD * shard + (BLOCK // 2) * (shard % 2), BLOCK // 2)
        stage[slot] = _mm_nt(hbuf[slot], wbuf[pl.ds(solo0, SOLO_ROWS), :]).astype(BF16)
        for s in range(N_SHARDS):
            @pl.when(shard == s)
            def _(s=s):
                for off, n, z0 in _solo_segments(s):
                    dst = zkv_ref.at[rows] if z0 == ZKV else z_ref.at[rows, pl.ds(z0, n)]
                    pltpu.make_async_copy(stage.at[slot, :, pl.ds(off, n)], dst, out_sems.at[slot]).start()

        @pl.when(p % 2 == 1)
        def _():
            pair = shard // 2
            w0 = pl.multiple_of(2 * WIN_SHARD * pair + SOLO_ROWS, BLOCK // 2)
            z0 = pl.multiple_of(jnp.where(pair == 0, _shared_tile(0), _shared_tile(1)), BLOCK)
            stage_sh[slot] = _mm_nt(hbuf[slot], wbuf[pl.ds(w0, BLOCK), :]).astype(BF16)
            sh_copy(slot, z0).start()

        @pl.when((p == 3) & last)
        def _():
            for k in (2, 3, 4, 5):
                total(k).wait_send()
            shard_total(wbuf, wt_ref, loc_sems.at[1]).wait()
            cps = wdw_copies()
            for cp in cps:
                cp.wait_recv()
            for cp in cps:
                cp.wait_send()
            own_wdw().wait()
            for sl in range(2):
                out_total(sl).wait()
                sh_copy(sl, 0).wait()

    def in_pass0(p, t):
        return jnp.where(p == 0, t, n_t - 1)

    any_spec = pl.BlockSpec(memory_space=pl.ANY)
    return pl.pallas_call(
        body, name="inproj", grid=(N_SHARDS, n_t),
        in_specs=[pl.BlockSpec((tm, D), lambda p, t: (in_pass0(p, t), 0)), pl.BlockSpec((1, D), lambda p, t: (0, 0)),
                  pl.BlockSpec((1, tm), lambda p, t: (0, in_pass0(p, t))),
                  pl.BlockSpec((ROPE_ROWS, 1), lambda p, t: (0, 0)),
                  pl.BlockSpec((3, ROPE_ROWS, BLOCK), lambda p, t: (0, 0, 0)), any_spec, any_spec],
        out_specs=(any_spec,) * 5 + (pl.BlockSpec((tm, BLOCK), lambda p, t: (in_pass0(p, t), 0)),) * 2,
        out_shape=(jax.ShapeDtypeStruct((T, ZKV), BF16), jax.ShapeDtypeStruct((T, 2 * BLOCK), BF16),
                   jax.ShapeDtypeStruct((IN_WIDTH, D), BF16), jax.ShapeDtypeStruct((N_SHARDS, 32, PLE), F32),
                   jax.ShapeDtypeStruct((T, D), BF16), jax.ShapeDtypeStruct((T, BLOCK), F32),
                   jax.ShapeDtypeStruct((T, BLOCK), F32)),
        scratch_shapes=[pltpu.VMEM((IN_WIDTH, D), BF16), pltpu.VMEM((2, tm, SOLO_ROWS), BF16),
                        pltpu.VMEM((2, tm, BLOCK), BF16), pltpu.VMEM((2, tm, D), BF16),
                        pltpu.SemaphoreType.DMA((9,)), pltpu.SemaphoreType.DMA((9,)),
                        pltpu.SemaphoreType.DMA((3,)), pltpu.SemaphoreType.DMA((2,)),
                        pltpu.SemaphoreType.DMA((2,)), pltpu.SemaphoreType.DMA((2,))],
        compiler_params=_params(("arbitrary", "arbitrary")),
    )(x, ln_pre, pos, freq, spread, win_t, wdw_shard)


HALO = 32
CONV_RC = 64
CONV_LC = 256


def _conv_taps(w_ref, src, r0, lane0, offset_of_tap):
    lanes = pl.ds(lane0, CONV_LC)
    out = None
    for b in range(8):
        taps = [k for k in range(CONV_K) if offset_of_tap(k) % 8 == b]
        if not taps:
            continue
        rows = CONV_RC + (8 if b else 0)
        vb = None
        for k in taps:
            term = w_ref[k:k + 1, lanes] * src[pl.ds(r0 + (offset_of_tap(k) - b), rows), lanes]
            vb = term if vb is None else vb + term
        vb = vb[b:b + CONV_RC] if b else vb
        out = vb if out is None else out + vb
    return out


def _conv_fwd(z, wdw, b_dw, ln_g, ln_b, wall, S, tm, group, shards):
    T = z.shape[0]
    nt = S // tm
    hb = tm // HALO
    gathered = _group_shapes(group)

    def body(cv_ref, cg_ref, cgate_ref, hcv_ref, hcg_ref, wdw_ref, bdw_ref, lng_ref, lnb_ref, wpw_ref,
             wbrc_ref, *rest):
        shard_refs, rest = rest[:len(group)], rest[len(group):]
        ya_ref, y_ref, rstd_ref, pw_ref = rest[:4]
        gather_refs, (ubuf, cbuf), gather_scratch = rest[4:4 + len(gathered)], rest[-6:-4], rest[-4:]
        t = pl.program_id(1)
        step = pl.program_id(0) * nt + t
        finish_gather = _group_gather(group, shard_refs, gather_refs, gather_scratch, step, T // tm)
        ubuf[HALO:HALO + tm, :] = cv_ref[...].astype(F32) * _sig(cg_ref[...].astype(F32))
        hu = hcv_ref[...].astype(F32) * _sig(hcg_ref[...].astype(F32))
        ubuf[0:HALO, :] = jnp.where(t > 0, hu, 0.0)
        ubuf[HALO + tm:HALO + tm + 8, :] = jnp.zeros((8, D), F32)

        def chunk(ci, carry):
            r0 = pl.multiple_of(ci * CONV_RC, CONV_RC)
            for lg in range(D // CONV_LC):
                acc = _conv_taps(wdw_ref, ubuf, r0, lg * CONV_LC, lambda k: HALO - (CONV_K - 1) + k)
                cbuf[pl.ds(r0, CONV_RC), pl.ds(lg * CONV_LC, CONV_LC)] = acc
            return carry

        lax.fori_loop(0, tm // CONV_RC, chunk, 0)
        cc = cbuf[...] + bdw_ref[...]
        mu = jnp.mean(cc, axis=-1, keepdims=True)
        dd = cc - mu
        rstd = lax.rsqrt(jnp.mean(dd * dd, axis=-1, keepdims=True) + EPS)
        yn = dd * rstd
        y_ref[...] = yn.astype(BF16)
        rstd_ref[...] = rstd
        n = yn * lng_ref[...] + lnb_ref[...]
        s = n * _sig(n)
        pw = _mm(s.astype(BF16), wpw_ref[...])
        pw_ref[...] = pw.astype(BF16)
        gt = cgate_ref[...].astype(F32)
        ya_in = pw * (gt * _sig(gt))
        ya_ref[...] = _mm(ya_in.astype(BF16), wbrc_ref[...]).astype(BF16)
        pl.when(step == T // tm - 1)(finish_gather)

    def row(b, t):
        return b * nt + t

    def halo(b, t):
        return jnp.maximum(row(b, t) * hb - 1, 0)

    vec = pl.BlockSpec((1, D), lambda b, t: (0, 0))
    tile = lambda j: pl.BlockSpec((tm, D), lambda b, t: (row(b, t), j))
    out_tile = pl.BlockSpec((tm, D), lambda b, t: (row(b, t), 0))
    any_spec = pl.BlockSpec(memory_space=pl.ANY)
    return pl.pallas_call(
        body, name="conv_fwd", grid=(T // S, nt),
        in_specs=[tile(ZB_CVAL), tile(ZB_CGLU), tile(ZB_CGATE),
                  pl.BlockSpec((HALO, D), lambda b, t: (halo(b, t), ZB_CVAL)),
                  pl.BlockSpec((HALO, D), lambda b, t: (halo(b, t), ZB_CGLU)),
                  pl.BlockSpec((32, D), lambda b, t: (0, 0)), vec, vec, vec,
                  pl.BlockSpec((D, D), lambda b, t: (0, 0)),
                  pl.BlockSpec((D, D), lambda b, t: (1, 0))] + [any_spec] * len(group),
        out_specs=(out_tile, out_tile, pl.BlockSpec((tm, 1), lambda b, t: (row(b, t), 0)), out_tile)
        + (any_spec,) * len(gathered),
        out_shape=[jax.ShapeDtypeStruct((T, D), BF16), jax.ShapeDtypeStruct((T, D), BF16),
                   jax.ShapeDtypeStruct((T, 1), F32), jax.ShapeDtypeStruct((T, D), BF16)] + gathered,
        scratch_shapes=[pltpu.VMEM((tm + HALO + 8, D), F32), pltpu.VMEM((tm, D), F32)] + _group_scratch(group),
        compiler_params=_params(("arbitrary", "arbitrary")),
    )(z, z, z, z, z, wdw, b_dw, ln_g, ln_b, wall, wall, *shards)


def _swap_matrix():
    r = lax.broadcasted_iota(jnp.int32, (BLOCK, BLOCK), 0)
    l = lax.broadcasted_iota(jnp.int32, (BLOCK, BLOCK), 1)
    lh = l & (HEAD_DIM - 1)
    half = ROPE_DIM // 2
    hit = ((lh < half) & (r == l + half)) | ((lh >= half) & (lh < ROPE_DIM) & (r == l - half))
    return jnp.where(hit, 1.0, 0.0).astype(BF16)


def _rope(tb, cos, sin, pswap):
    return tb.astype(F32) * cos + _mm(tb, pswap) * sin


def _rope_f32(tv, cos, sin, pswap):
    hi = tv.astype(BF16)
    lo = (tv - hi.astype(F32)).astype(BF16)
    return tv * cos + (_mm(hi, pswap) + _mm(lo, pswap)) * sin


def _kv_variants(kv):
    lane = lax.broadcasted_iota(jnp.int32, kv.shape, 1)
    lo = lane < HEAD_DIM
    sw = pltpu.roll(kv, HEAD_DIM, 1)
    z = jnp.zeros_like(kv)
    g0 = (jnp.where(lo, kv, z).astype(BF16), jnp.where(lo, z, sw).astype(BF16))
    g1 = (jnp.where(lo, sw, z).astype(BF16), jnp.where(lo, z, kv).astype(BF16))
    return (g0, g1)


def _band_mask(nq):
    qi = lax.broadcasted_iota(jnp.int32, (nq * BLOCK, 2 * BLOCK), 0) & (BLOCK - 1)
    sj = lax.broadcasted_iota(jnp.int32, (nq * BLOCK, 2 * BLOCK), 1)
    return (sj <= qi + BLOCK) & (sj > qi), sj


def _sink_rep(sink_ref, g, e):
    return jnp.concatenate(
        [jnp.full((BLOCK, BLOCK), sink_ref[8 * g + 2 * j + e], F32) for j in range(4)], axis=0)


def _softmax_parts(s, valid, sk):
    rows = s.shape[0]
    s = jnp.where(valid, s, -1e30)
    m = jnp.maximum(jnp.broadcast_to(jnp.max(s, axis=-1, keepdims=True), (rows, BLOCK)), sk)
    return jnp.exp(s - jnp.concatenate([m, m], axis=1)), jnp.exp(sk - m)


def _softmax_sink(s, valid, sk):
    p, ps = _softmax_parts(s, valid, sk)
    inv = 1.0 / (_mm(p.astype(BF16), jnp.ones((2 * BLOCK, BLOCK), BF16)) + ps)
    return p * jnp.concatenate([inv, inv], axis=1), ps * inv


def _attn_fwd(z, zkv, cos_t, sin_t, sinks, S, tq, group, shards):
    T = z.shape[0]
    nt = S // tq
    nq = tq // BLOCK
    gathered = _group_shapes(group)

    def body(sink_ref, q_ref, kv_ref, hkv_ref, cos_ref, sin_ref, hcos_ref, hsin_ref, *rest):
        shard_refs, o_ref = rest[:len(group)], rest[len(group)]
        t = pl.program_id(1)
        step = pl.program_id(0) * nt + t
        finish_gather = _group_gather(group, shard_refs, rest[len(group) + 1:-4], rest[-4:], step, T // tq)
        cos = cos_ref[...]
        sin = sin_ref[...]
        pswap = _swap_matrix()
        kv = jnp.concatenate([hkv_ref[...], kv_ref[...]], axis=0)
        cos_k = jnp.concatenate([hcos_ref[...], cos], axis=0)
        sin_k = jnp.concatenate([hsin_ref[...], sin], axis=0)
        kx = _kv_variants(_rope(kv[:, :BLOCK], cos_k, sin_k, pswap))
        one = jnp.ones((tq + BLOCK, BLOCK), BF16)
        vx = [[jnp.concatenate([v, one], axis=1) for v in vg] for vg in _kv_variants(kv[:, BLOCK:].astype(F32))]
        band, sj = _band_mask(4)
        qs = [(_rope(q_ref[:, 128 * hp:128 * hp + 128], cos, sin, pswap) * 0.125).astype(BF16)
              for hp in range(8)]
        for n in range(nq):
            first = (t == 0) & (n == 0)
            valid = band & (jnp.logical_not(first) | (sj >= BLOCK))
            r0 = n * BLOCK
            for g in range(2):
                lhs = jnp.concatenate([qs[4 * g + j][r0:r0 + BLOCK] for j in range(4)], axis=0)
                acc = jnp.zeros((4 * BLOCK, BLOCK), F32)
                for e in range(2):
                    s = _mm_nt(lhs, kx[g][e][r0:r0 + 2 * BLOCK])
                    p, ps = _softmax_parts(s, valid, _sink_rep(sink_ref, g, e))
                    r = _mm(p.astype(BF16), vx[g][e][r0:r0 + 2 * BLOCK])
                    acc = acc + r[:, 0:BLOCK] * (1.0 / (r[:, BLOCK:2 * BLOCK] + ps))
                for j in range(4):
                    o_ref[r0:r0 + BLOCK, 128 * (4 * g + j):128 * (4 * g + j) + 128] = (
                        acc[j * BLOCK:(j + 1) * BLOCK].astype(BF16))
        pl.when(step == T // tq - 1)(finish_gather)

    def row(b, t):
        return b * nt + t

    def halo(b, t):
        return jnp.maximum(row(b, t) * nq - 1, 0)

    any_spec = pl.BlockSpec(memory_space=pl.ANY)
    return pl.pallas_call(
        body, name="attn_fwd", grid=(T // S, nt),
        in_specs=[pl.BlockSpec(memory_space=pltpu.SMEM),
                  pl.BlockSpec((tq, D), lambda b, t: (row(b, t), ZB_Q)),
                  pl.BlockSpec((tq, 2 * BLOCK), lambda b, t: (row(b, t), 0)),
                  pl.BlockSpec((BLOCK, 2 * BLOCK), lambda b, t: (halo(b, t), 0)),
                  pl.BlockSpec((tq, BLOCK), lambda b, t: (row(b, t), 0)),
                  pl.BlockSpec((tq, BLOCK), lambda b, t: (row(b, t), 0)),
                  pl.BlockSpec((BLOCK, BLOCK), lambda b, t: (halo(b, t), 0)),
                  pl.BlockSpec((BLOCK, BLOCK), lambda b, t: (halo(b, t), 0))] + [any_spec] * len(group),
        out_specs=(pl.BlockSpec((tq, D), lambda b, t: (row(b, t), 0)),) + (any_spec,) * len(gathered),
        out_shape=[jax.ShapeDtypeStruct((T, D), BF16)] + gathered,
        scratch_shapes=_group_scratch(group),
        compiler_params=_params(("arbitrary", "arbitrary")),
    )(sinks, z, zkv, zkv, cos_t, sin_t, cos_t, sin_t, *shards)


def _tail_a(x, tgt, p, o, ya, z, ln_post, wall_b, wppt, tm):
    T = x.shape[0]
    last = T // tm - 1

    def body(x_ref, tgt_ref, p_ref, o_ref, ya_ref, ag_ref, gc_ref, ga_ref, lnp_ref, wbra_ref, wout_ref,
             wpg_ref, wppt_ref, loss_ref, dx1_ref, dm_ref, yb_ref, glnp_ref, gpack_ref, gwpp_ref,
             acc_out, acc_pg, sem):
        i = pl.program_id(0)

        @pl.when(i == 0)
        def _():
            acc_out[...] = jnp.zeros_like(acc_out)
            acc_pg[...] = jnp.zeros_like(acc_pg)
            gwpp_ref[...] = jnp.zeros_like(gwpp_ref)
            glnp_ref[...] = jnp.zeros_like(glnp_ref)
            loss_ref[...] = jnp.zeros_like(loss_ref)

        ag = ag_ref[...].astype(F32)
        yb_in = (o_ref[...].astype(F32) * (ag * _sig(ag))).astype(BF16)
        yb = _mm(yb_in, wbra_ref[...])
        yb_ref[...] = yb.astype(BF16)
        m = (_sig(gc_ref[...].astype(F32)) * ya_ref[...].astype(F32)
             + _sig(ga_ref[...].astype(F32)) * yb).astype(BF16)
        mo = _mm(m, wout_ref[...])
        r2 = lax.rsqrt(jnp.mean(mo * mo, axis=-1, keepdims=True) + EPS)
        nrm = mo * r2
        g_post = lnp_ref[...]
        x1 = x_ref[...] + nrm * g_post
        x1b = x1.astype(BF16)
        gate = _sig(_mm(x1b, wpg_ref[...]))
        pb = p_ref[...].astype(BF16)
        pp = _mm_nt(pb, wppt_ref[...])
        err = x1 + gate * pp - tgt_ref[...]
        loss_ref[...] += 0.5 * jnp.sum(jnp.sum(err * err, axis=-1, keepdims=True) * (1.0 / D),
                                       axis=0, keepdims=True)
        dx2 = err * (1.0 / D)
        dgp = (dx2 * pp * gate * (1.0 - gate)).astype(BF16)
        dpp = (dx2 * gate).astype(BF16)
        dx1 = dx2 + _mm_nt(dgp, wpg_ref[...])
        dx1_ref[...] = dx1
        acc_pg[...] += _mm_tn(x1b, dgp)
        gwpp_ref[...] += _mm_tn(dpp, pb)
        glnp_ref[...] += jnp.sum(dx1 * nrm, axis=0, keepdims=True)
        a = dx1 * g_post
        dmo = (r2 * (a - nrm * jnp.mean(a * nrm, axis=-1, keepdims=True))).astype(BF16)
        dm_ref[...] = _mm_nt(dmo, wout_ref[...]).astype(BF16)
        acc_out[...] += _mm_tn(m, dmo)

        @pl.when(i == last)
        def _():
            _flush_all([(acc_out, 3 * D, sem.at[0]), (acc_pg, 4 * D, sem.at[1])], gpack_ref)

    tile = pl.BlockSpec((tm, D), lambda i: (i, 0))
    ztile = lambda j: pl.BlockSpec((tm, D), lambda i: (i, j))
    wsq = lambda k: pl.BlockSpec((D, D), lambda i: (k, 0))
    const = lambda shp: pl.BlockSpec(shp, lambda i: (0, 0))
    any_spec = pl.BlockSpec(memory_space=pl.ANY)
    return pl.pallas_call(
        body, name="tail_a", grid=(T // tm,),
        in_specs=[tile, tile, pl.BlockSpec((tm, PLE), lambda i: (i, 0)), tile, tile, ztile(ZB_AGATE),
                  ztile(ZB_GCONV), ztile(ZB_GATTN), const((1, D)), wsq(0), wsq(1), wsq(2), const((D, PLE))],
        out_specs=(const((1, 1)), tile, tile, tile, const((1, D)), any_spec, const((D, PLE))),
        out_shape=(jax.ShapeDtypeStruct((1, 1), F32), jax.ShapeDtypeStruct((T, D), F32),
                   jax.ShapeDtypeStruct((T, D), BF16), jax.ShapeDtypeStruct((T, D), BF16),
                   jax.ShapeDtypeStruct((1, D), F32), jax.ShapeDtypeStruct((N_SHARDS, SQ_PACK, D), F32),
                   jax.ShapeDtypeStruct((D, PLE), F32)),
        scratch_shapes=[pltpu.VMEM((D, D), F32), pltpu.VMEM((D, D), F32), pltpu.SemaphoreType.DMA((2,))],
        compiler_params=_params(("arbitrary",)),
    )(x, tgt, p, o, ya, z, z, z, ln_post, wall_b, wall_b, wall_b, wppt)


def _dsilu(v, sg):
    return sg * (1.0 + v * (1.0 - sg))


def _tail_b(dm, ya, yb, o, z, pw, y, rstd, ln_g, ln_b, wall_a, wall_b, gppt, gpack, tm):
    T = dm.shape[0]
    last = T // tm - 1

    def body(dm_ref, ya_ref, yb_ref, o_ref, ag_ref, gc_ref, ga_ref, cgate_ref, pw_ref, y_ref, rstd_ref,
             lng_ref, lnb_ref, wpw_ref, wbrc_ref, wbra_ref, gppt_ref, gpack_in, dg_ref, do_ref, dc_ref,
             gvec_ref, gpack_ref, acc_bra, acc_brc, acc_pw, sem):
        i = pl.program_id(0)

        @pl.when(i == 0)
        def _():
            acc_bra[...] = jnp.zeros_like(acc_bra)
            acc_brc[...] = jnp.zeros_like(acc_brc)
            acc_pw[...] = jnp.zeros_like(acc_pw)
            gvec_ref[...] = jnp.zeros_like(gvec_ref)

        g = lng_ref[...]

        def part(rs):
            dm_v = dm_ref[rs, :].astype(F32)
            sgc = _sig(gc_ref[rs, :].astype(F32))
            sga = _sig(ga_ref[rs, :].astype(F32))
            dya = (dm_v * sgc).astype(BF16)
            dyb = (dm_v * sga).astype(BF16)
            dg_ref[rs, D:2 * D] = (dm_v * ya_ref[rs, :].astype(F32) * sgc * (1.0 - sgc)).astype(BF16)
            dg_ref[rs, 2 * D:3 * D] = (dm_v * yb_ref[rs, :].astype(F32) * sga * (1.0 - sga)).astype(BF16)
            ag = ag_ref[rs, :].astype(F32)
            sag = _sig(ag)
            sa = ag * sag
            ov = o_ref[rs, :].astype(F32)
            dyb_in = _mm_nt(dyb, wbra_ref[...])
            do_ref[rs, :] = (dyb_in * sa).astype(BF16)
            dg_ref[rs, 0:D] = (dyb_in * ov * _dsilu(ag, sag)).astype(BF16)
            gt = cgate_ref[rs, :].astype(F32)
            sgt = _sig(gt)
            sgate = gt * sgt
            pw = pw_ref[rs, :].astype(F32)
            dya_in = _mm_nt(dya, wbrc_ref[...])
            dpw = (dya_in * sgate).astype(BF16)
            dg_ref[rs, 3 * D:4 * D] = (dya_in * pw * _dsilu(gt, sgt)).astype(BF16)
            yn = y_ref[rs, :].astype(F32)
            n = yn * g + lnb_ref[...]
            sn = _sig(n)
            dn = _mm_nt(dpw, wpw_ref[...]) * _dsilu(n, sn)
            dy = dn * g
            dc = rstd_ref[rs, :] * (dy - jnp.mean(dy, axis=-1, keepdims=True)
                                    - yn * jnp.mean(dy * yn, axis=-1, keepdims=True))
            dc_ref[rs, :] = dc.astype(BF16)
            sums = (jnp.sum(dn * yn, axis=0, keepdims=True), jnp.sum(dn, axis=0, keepdims=True),
                    jnp.sum(dc, axis=0, keepdims=True))
            return ((ov * sa).astype(BF16), dyb, (pw * sgate).astype(BF16), dya, (n * sn).astype(BF16), dpw,
                    sums)

        pt = part(pl.ds(0, tm))
        acc_bra[...] += _mm_tn(pt[0], pt[1])
        acc_brc[...] += _mm_tn(pt[2], pt[3])
        acc_pw[...] += _mm_tn(pt[4], pt[5])
        for j in range(3):
            gvec_ref[j:j + 1, :] += pt[6][j]

        @pl.when(i == last)
        def _():
            _flush_all([(acc_pw, 0, sem.at[0]), (acc_brc, D, sem.at[1]), (acc_bra, 2 * D, sem.at[2]),
                        (gppt_ref, WPP0, sem.at[3])], gpack_ref)

    tile = pl.BlockSpec((tm, D), lambda i: (i, 0))
    ztile = lambda j: pl.BlockSpec((tm, D), lambda i: (i, j))
    wsq = lambda k: pl.BlockSpec((D, D), lambda i: (k, 0))
    const = lambda shp: pl.BlockSpec(shp, lambda i: (0, 0))
    any_spec = pl.BlockSpec(memory_space=pl.ANY)
    return pl.pallas_call(
        body, name="tail_b", grid=(T // tm,),
        in_specs=[tile, tile, tile, tile, ztile(ZB_AGATE), ztile(ZB_GCONV), ztile(ZB_GATTN), ztile(ZB_CGATE),
                  tile, tile, pl.BlockSpec((tm, 1), lambda i: (i, 0)), const((1, D)), const((1, D)), wsq(0),
                  wsq(1), wsq(0), const((PLE, D)), any_spec],
        out_specs=(pl.BlockSpec((tm, 4 * D), lambda i: (i, 0)), tile, tile, const((8, D)), any_spec),
        out_shape=(jax.ShapeDtypeStruct((T, 7 * D), BF16), jax.ShapeDtypeStruct((T, D), BF16),
                   jax.ShapeDtypeStruct((T, D), BF16), jax.ShapeDtypeStruct((8, D), F32),
                   jax.ShapeDtypeStruct(gpack.shape, F32)),
        input_output_aliases={17: 4},
        scratch_shapes=[pltpu.VMEM((D, D), F32), pltpu.VMEM((D, D), F32), pltpu.VMEM((D, D), F32),
                        pltpu.SemaphoreType.DMA((4,))],
        compiler_params=_params(("arbitrary",)),
    )(dm, ya, yb, o, z, z, z, z, pw, y, rstd, ln_g, ln_b, wall_a, wall_a, wall_b, gppt, gpack)


def _conv_bwd(dc, z, wdw, dz, S, tm, copies, src, landing):
    T = dc.shape[0]
    nt = S // tm
    hb = tm // HALO
    nrows = T // HALO

    def body(dc_ref, hdc_ref, cv_ref, cg_ref, hcv_ref, hcg_ref, wdw_ref, dz_in, src_ref, dz_ref, gw_ref,
             land_ref, ubuf, dcbuf, dubuf, dwacc, shbuf, send_sems, recv_sems):
        b = pl.program_id(0)
        t = pl.program_id(1)

        @pl.when((b == 0) & (t == 0))
        def _():
            dwacc[...] = jnp.zeros_like(dwacc)
            for cp in copies(src_ref, land_ref, send_sems, recv_sems):
                cp.start()

        cv = cv_ref[...].astype(F32)
        sg = _sig(cg_ref[...].astype(F32))
        ubuf[HALO:HALO + tm, :] = cv * sg
        hu = hcv_ref[...].astype(F32) * _sig(hcg_ref[...].astype(F32))
        ubuf[0:HALO, :] = jnp.where(t > 0, hu, 0.0)
        ubuf[HALO + tm:HALO + tm + 8, :] = jnp.zeros((8, D), F32)
        dcbuf[0:tm, :] = dc_ref[...].astype(F32)
        dcbuf[tm:tm + HALO, :] = jnp.where(t < nt - 1, hdc_ref[...].astype(F32), 0.0)
        dcbuf[tm + HALO:tm + HALO + 8, :] = jnp.zeros((8, D), F32)

        def chunk(ci, carry):
            r0 = pl.multiple_of(ci * CONV_RC, CONV_RC)
            for lg in range(D // CONV_LC):
                l0 = lg * CONV_LC
                dubuf[pl.ds(r0, CONV_RC), pl.ds(l0, CONV_LC)] = _conv_taps(
                    wdw_ref, dcbuf, r0, l0, lambda k: CONV_K - 1 - k)
                dcc = dcbuf[pl.ds(r0, CONV_RC), pl.ds(l0, CONV_LC)]
                zero8 = jnp.zeros((8, CONV_LC), F32)
                dcz = jnp.concatenate([zero8, dcc, zero8], axis=0)
                for bb in range(8):
                    taps = [k for k in range(CONV_K) if (HALO - (CONV_K - 1) + k) % 8 == bb]
                    if not taps:
                        continue
                    rows = CONV_RC + (8 if bb else 0)
                    if bb:
                        shbuf[bb] = dcz[8 - bb:8 - bb + rows]
                    for k in taps:
                        a8 = HALO - (CONV_K - 1) + k - bb
                        dcs = shbuf[bb] if bb else dcc
                        prod = dcs * ubuf[pl.ds(r0 + a8, rows), pl.ds(l0, CONV_LC)]
                        part = prod[0:8]
                        for q in range(1, rows // 8):
                            part = part + prod[8 * q:8 * q + 8]
                        dwacc[8 * k:8 * k + 8, pl.ds(l0, CONV_LC)] += part
            return carry

        lax.fori_loop(0, tm // CONV_RC, chunk, 0)
        du = dubuf[...]
        dz_ref[:, 0:D] = (du * sg).astype(BF16)
        dz_ref[:, D:2 * D] = (du * cv * sg * (1.0 - sg)).astype(BF16)

        @pl.when((b == pl.num_programs(0) - 1) & (t == nt - 1))
        def _():
            for k in range(32):
                gw_ref[k:k + 1, :] = jnp.sum(dwacc[8 * k:8 * k + 8, :], axis=0, keepdims=True)
            cps = copies(src_ref, land_ref, send_sems, recv_sems)
            for cp in cps:
                cp.wait_recv()
            for cp in cps:
                cp.wait_send()

    def row(b, t):
        return b * nt + t

    def prev_halo(b, t):
        return jnp.maximum(row(b, t) * hb - 1, 0)

    def next_halo(b, t):
        return jnp.minimum((row(b, t) + 1) * hb, nrows - 1)

    return pl.pallas_call(
        body, name="conv_bwd", grid=(T // S, nt),
        in_specs=[pl.BlockSpec((tm, D), lambda b, t: (row(b, t), 0)),
                  pl.BlockSpec((HALO, D), lambda b, t: (next_halo(b, t), 0)),
                  pl.BlockSpec((tm, D), lambda b, t: (row(b, t), ZB_CVAL)),
                  pl.BlockSpec((tm, D), lambda b, t: (row(b, t), ZB_CGLU)),
                  pl.BlockSpec((HALO, D), lambda b, t: (prev_halo(b, t), ZB_CVAL)),
                  pl.BlockSpec((HALO, D), lambda b, t: (prev_halo(b, t), ZB_CGLU)),
                  pl.BlockSpec((32, D), lambda b, t: (0, 0)),
                  pl.BlockSpec(memory_space=pl.ANY), pl.BlockSpec(memory_space=pl.ANY)],
        out_specs=(pl.BlockSpec((tm, 2 * D), lambda b, t: (row(b, t), ZB_CVAL // 2)),
                   pl.BlockSpec((32, D), lambda b, t: (0, 0)), pl.BlockSpec(memory_space=pl.ANY)),
        out_shape=(jax.ShapeDtypeStruct(dz.shape, BF16), jax.ShapeDtypeStruct((32, D), F32), landing),
        input_output_aliases={7: 0},
        scratch_shapes=[pltpu.VMEM((tm + HALO + 8, D), F32), pltpu.VMEM((tm + HALO + 8, D), F32),
                        pltpu.VMEM((tm, D), F32), pltpu.VMEM((8 * 32, D), F32),
                        pltpu.VMEM((8, CONV_RC + 8, CONV_LC), F32), pltpu.SemaphoreType.DMA((3,)),
                        pltpu.SemaphoreType.DMA((3,))],
        compiler_params=_params(("arbitrary", "arbitrary")),
    )(dc, dc, z, z, z, z, wdw, dz, src)


def _attn_bwd(z, zkv, o, do, cos_t, sin_t, sinks, dz, S, tq, gpack, r1, landing):
    T = z.shape[0]
    nt = S // tq
    nq = tq // BLOCK
    n_steps = (T // S) * nt
    assert n_steps >= N_SHARDS
    half = gpack.shape[1] // 2

    def body(sink_ref, q_ref, kv_ref, hkv_ref, o_ref, do_ref, cos_ref, sin_ref, hcos_ref, hsin_ref, dz_in,
             g_ref, r1_ref, dq_ref, dkv_ref, gs_ref, land_ref, carry, dkacc, dvacc, gbuf, rbuf, csbuf, ld_sems,
             send_sems, recv_sems):
        b = pl.program_id(0)
        tt = pl.program_id(1)
        t = nt - 1 - tt
        step = b * nt + tt
        own = pl.ds(pl.multiple_of(lax.axis_index("c") * half, 32), half)

        def loads(s):
            return [pltpu.make_async_copy(g_ref.at[s, own], gbuf.at[s % 2], ld_sems.at[0, s % 2]),
                    pltpu.make_async_copy(r1_ref.at[s], rbuf.at[s % 2], ld_sems.at[1, s % 2])]

        for s in range(N_SHARDS):
            @pl.when(step == s)
            def _(s=s):
                if s == 0:
                    gs_ref[...] = jnp.zeros_like(gs_ref)
                for cp in (loads(0) if s == 0 else []) + (loads(s + 1) if s + 1 < N_SHARDS else []):
                    cp.start()
                for cp in loads(s):
                    cp.wait()
                csbuf[s] = (gbuf[s % 2] + rbuf[s % 2]).astype(BF16)
                if s == N_SHARDS - 1:
                    for cp in _chip_sum_copies(csbuf, land_ref, send_sems, recv_sems):
                        cp.start()

        @pl.when(tt == 0)
        def _():
            carry[...] = jnp.zeros_like(carry)

        cos = cos_ref[...]
        sin = sin_ref[...]
        pswap = _swap_matrix()
        kv = jnp.concatenate([hkv_ref[...], kv_ref[...]], axis=0)
        cos_k = jnp.concatenate([hcos_ref[...], cos], axis=0)
        sin_k = jnp.concatenate([hsin_ref[...], sin], axis=0)
        kx = _kv_variants(_rope(kv[:, :BLOCK], cos_k, sin_k, pswap))
        vx = _kv_variants(kv[:, BLOCK:].astype(F32))
        band, sj = _band_mask(4)
        lo = lax.broadcasted_iota(jnp.int32, (4 * BLOCK, BLOCK), 1) < HEAD_DIM
        ones = jnp.ones((2 * BLOCK, 2 * BLOCK), BF16)
        qs = [(_rope(q_ref[:, 128 * hp:128 * hp + 128], cos, sin, pswap) * 0.125).astype(BF16)
              for hp in range(8)]
        dkacc[...] = jnp.zeros_like(dkacc)
        dvacc[...] = jnp.zeros_like(dvacc)
        gsum = jnp.zeros((1, BLOCK), F32)
        hlane = lax.broadcasted_iota(jnp.int32, (1, BLOCK), 1)
        for n in range(nq):
            first = (t == 0) & (n == 0)
            valid = band & (jnp.logical_not(first) | (sj >= BLOCK))
            r0 = n * BLOCK
            for g in range(2):
                cols = [slice(128 * (4 * g + j), 128 * (4 * g + j) + 128) for j in range(4)]
                lhs = jnp.concatenate([qs[4 * g + j][r0:r0 + BLOCK] for j in range(4)], axis=0)
                dov = jnp.concatenate([do_ref[r0:r0 + BLOCK, cs] for cs in cols], axis=0)
                prod = dov.astype(F32) * jnp.concatenate(
                    [o_ref[r0:r0 + BLOCK, cs] for cs in cols], axis=0).astype(F32)
                lhs_t = lhs.T
                dov_t = dov.T
                dq = jnp.zeros((4 * BLOCK, BLOCK), F32)
                dk_t = jnp.zeros((HEAD_DIM, 2 * BLOCK), F32)
                dv_t = jnp.zeros((HEAD_DIM, 2 * BLOCK), F32)
                for e in range(2):
                    kw = kx[g][e][r0:r0 + 2 * BLOCK]
                    vw = vx[g][e][r0:r0 + 2 * BLOCK]
                    s = _mm_nt(lhs, kw)
                    p, psink = _softmax_sink(s, valid, _sink_rep(sink_ref, g, e))
                    pe = jnp.where(lo if e == 0 else jnp.logical_not(lo), prod, 0.0)
                    pe_hi = pe.astype(BF16)
                    pe_lo = (pe - pe_hi.astype(F32)).astype(BF16)
                    delta = _mm(jnp.concatenate([pe_hi, pe_lo], axis=1), ones)
                    ds = (p * (_mm_nt(dov, vw) - delta)).astype(BF16)
                    dq = dq + _mm(ds, kw)
                    dims = slice(HEAD_DIM * e, HEAD_DIM * (e + 1))
                    dk_t = dk_t + _mm(lhs_t[dims], ds)
                    dv_t = dv_t + _mm(dov_t[dims], p.astype(BF16))
                    gs = -psink * delta[:, 0:BLOCK]
                    for j in range(4):
                        tot = jnp.sum(gs[j * BLOCK:(j + 1) * BLOCK], axis=0, keepdims=True)
                        gsum = gsum + jnp.where(hlane == 8 * g + 2 * j + e, tot, 0.0)
                dkacc[HEAD_DIM * g:HEAD_DIM * (g + 1), r0:r0 + 2 * BLOCK] += dk_t
                dvacc[HEAD_DIM * g:HEAD_DIM * (g + 1), r0:r0 + 2 * BLOCK] += dv_t
                for j in range(4):
                    dqj = _rope_f32(dq[j * BLOCK:(j + 1) * BLOCK] * 0.125, cos[r0:r0 + BLOCK],
                                    -sin[r0:r0 + BLOCK], pswap)
                    dq_ref[r0:r0 + BLOCK, cols[j]] = dqj.astype(BF16)
        gs_ref[0:1, :] += gsum
        dk_all = dkacc[...]
        dv_all = dvacc[...]
        dk_last = dk_all[:, tq:tq + BLOCK] + carry[0:BLOCK, :]
        dv_last = dv_all[:, tq:tq + BLOCK] + carry[BLOCK:2 * BLOCK, :]
        carry[0:BLOCK, :] = dk_all[:, 0:BLOCK]
        carry[BLOCK:2 * BLOCK, :] = dv_all[:, 0:BLOCK]
        if nq > 1:
            dk_tile = jnp.concatenate([dk_all[:, BLOCK:tq], dk_last], axis=1)
            dv_tile = jnp.concatenate([dv_all[:, BLOCK:tq], dv_last], axis=1)
        else:
            dk_tile, dv_tile = dk_last, dv_last
        dkv_ref[:, 0:BLOCK] = _rope_f32(dk_tile.T, cos, -sin, pswap).astype(BF16)
        dkv_ref[:, BLOCK:2 * BLOCK] = dv_tile.T.astype(BF16)

        @pl.when(step == n_steps - 1)
        def _():
            cps = _chip_sum_copies(csbuf, land_ref, send_sems, recv_sems)
            for cp in cps:
                cp.wait_recv()
            for cp in cps:
                cp.wait_send()

    def row(b, tt):
        return b * nt + (nt - 1 - tt)

    def halo(b, tt):
        return jnp.maximum(row(b, tt) * nq - 1, 0)

    tile = pl.BlockSpec((tq, D), lambda b, tt: (row(b, tt), 0))
    return pl.pallas_call(
        body, name="attn_bwd", grid=(T // S, nt),
        in_specs=[pl.BlockSpec(memory_space=pltpu.SMEM),
                  pl.BlockSpec((tq, D), lambda b, tt: (row(b, tt), ZB_Q)),
                  pl.BlockSpec((tq, 2 * BLOCK), lambda b, tt: (row(b, tt), 0)),
                  pl.BlockSpec((BLOCK, 2 * BLOCK), lambda b, tt: (halo(b, tt), 0)),
                  tile, tile,
                  pl.BlockSpec((tq, BLOCK), lambda b, tt: (row(b, tt), 0)),
                  pl.BlockSpec((tq, BLOCK), lambda b, tt: (row(b, tt), 0)),
                  pl.BlockSpec((BLOCK, BLOCK), lambda b, tt: (halo(b, tt), 0)),
                  pl.BlockSpec((BLOCK, BLOCK), lambda b, tt: (halo(b, tt), 0)),
                  pl.BlockSpec(memory_space=pl.ANY), pl.BlockSpec(memory_space=pl.ANY),
                  pl.BlockSpec(memory_space=pl.ANY)],
        out_specs=(pl.BlockSpec((tq, D), lambda b, tt: (row(b, tt), ZB_Q)),
                   pl.BlockSpec((tq, 2 * BLOCK), lambda b, tt: (row(b, tt), 0)),
                   pl.BlockSpec((8, BLOCK), lambda b, tt: (0, 0)), pl.BlockSpec(memory_space=pl.ANY)),
        out_shape=(jax.ShapeDtypeStruct(dz.shape, BF16), jax.ShapeDtypeStruct((T, 2 * BLOCK), BF16),
                   jax.ShapeDtypeStruct((8, BLOCK), F32), landing),
        input_output_aliases={10: 0},
        scratch_shapes=[pltpu.VMEM((2 * BLOCK, BLOCK), F32), pltpu.VMEM((BLOCK, tq + BLOCK), F32),
                        pltpu.VMEM((BLOCK, tq + BLOCK), F32), pltpu.VMEM((2, half, D), F32),
                        pltpu.VMEM((2, half, D), F32), pltpu.VMEM((N_SHARDS, half, D), BF16),
                        pltpu.SemaphoreType.DMA((2, 2)), pltpu.SemaphoreType.DMA((3,)),
                        pltpu.SemaphoreType.DMA((3,))],
        compiler_params=_params(("arbitrary", "arbitrary")),
    )(sinks, z, zkv, zkv, o, do, cos_t, sin_t, cos_t, sin_t, dz, gpack, r1)


def _dh(dz, dz_kv, wt, x, dx1, ln_pre, tm, copies, src, landing):
    T = x.shape[0]
    ntiles = T // tm
    nsem = 3

    def body(dz_ref, kv_ref, wt_ref, x_ref, dx1_ref, g_ref, src_ref, gx_ref, glp_ref, land_ref, wbuf, send_sems,
             recv_sems, wsem):
        i = pl.program_id(0)

        @pl.when(i == 0)
        def _():
            glp_ref[...] = jnp.zeros_like(glp_ref)
            for cp in copies(src_ref, land_ref, send_sems, recv_sems):
                cp.start()
            load = pltpu.make_async_copy(wt_ref, wbuf, wsem)
            load.start()
            load.wait()

        dh = _mm(dz_ref[...], wbuf[0:ZKV, :]) + _mm(kv_ref[...], wbuf[ZKV:IN_WIDTH, :])
        xv = x_ref[...]
        r = lax.rsqrt(jnp.mean(xv * xv, axis=-1, keepdims=True) + EPS)
        xr = xv * r
        glp_ref[...] += jnp.sum(dh * xr, axis=0, keepdims=True)
        a = dh * g_ref[...]
        gx_ref[...] = dx1_ref[...] + r * (a - xr * jnp.mean(a * xr, axis=-1, keepdims=True))

        @pl.when(i == ntiles - 1)
        def _():
            cps = copies(src_ref, land_ref, send_sems, recv_sems)
            for cp in cps:
                cp.wait_recv()
            for cp in cps:
                cp.wait_send()

    tile = pl.BlockSpec((tm, D), lambda i: (i, 0))
    any_spec = pl.BlockSpec(memory_space=pl.ANY)
    return pl.pallas_call(
        body, name="dh", grid=(ntiles,),
        in_specs=[pl.BlockSpec((tm, ZKV), lambda i: (i, 0)), pl.BlockSpec((tm, 2 * BLOCK), lambda i: (i, 0)),
                  any_spec, tile, tile, pl.BlockSpec((1, D), lambda i: (0, 0)), any_spec],
        out_specs=(tile, pl.BlockSpec((1, D), lambda i: (0, 0)), any_spec),
        out_shape=(jax.ShapeDtypeStruct((T, D), F32), jax.ShapeDtypeStruct((1, D), F32), landing),
        scratch_shapes=[pltpu.VMEM((IN_WIDTH, D), BF16), pltpu.SemaphoreType.DMA((nsem,)),
                        pltpu.SemaphoreType.DMA((nsem,)), pltpu.SemaphoreType.DMA],
        compiler_params=_params(("arbitrary",)),
    )(dz, dz_kv, wt, x, dx1, ln_pre, src)


def _gwt(dz, dz_kv, h, tt):
    T = dz.shape[0]
    nt = T // tt
    last = nt - 1
    kv = 2 * BLOCK
    half = WIN_SHARD // 2

    def body(dz_ref, dzkv_ref, h_ref, gpack_ref, r1_ref, hbuf, acc, hsems, sems, send_sems, recv_sems):
        j = pl.program_id(0)
        t = pl.program_id(1)
        slot = j % 2
        rows = pl.ds(pl.multiple_of(t * tt, tt), tt)
        x, y, c = _coords()

        def exchange(jj):
            wall0, n_rows = (WT0 + jj * D, D) if jj < 7 else (WT0 + ZKV, kv)
            for _, n, s, pr in _wall_segments(wall0, n_rows):
                for hb in range(2):
                    lo, hi = max(pr, hb * half), min(pr + n, (hb + 1) * half)
                    if lo < hi:
                        cp = pltpu.make_async_remote_copy(
                            src_ref=gpack_ref.at[s, pl.ds(lo, hi - lo)],
                            dst_ref=r1_ref.at[s, pl.ds(lo - hb * half, hi - lo)], send_sem=send_sems.at[0],
                            recv_sem=recv_sems.at[0], device_id=(x, y, 1 - c), device_id_type=MESH)
                        pl.when(c == 1 - hb)(cp.start)

        def h_load(i):
            return pltpu.make_async_copy(h_ref.at[pl.ds(i * tt, tt)], hbuf.at[pl.ds(i * tt, tt)], hsems.at[i])

        @pl.when((j == 0) & (t == 0))
        def _():
            for i in range(nt):
                h_load(i).start()

        for i in range(nt):
            pl.when((j == 0) & (t == i))(h_load(i).wait)

        @pl.when((j < 7) & (t == 0))
        def _():
            acc[slot] = _mm_tn(dz_ref[...], hbuf[rows, :])

        @pl.when((j < 7) & (t > 0))
        def _():
            acc[slot] += _mm_tn(dz_ref[...], hbuf[rows, :])

        @pl.when((j == 7) & (t == 0))
        def _():
            acc[1, 0:kv, :] = _mm_tn(dzkv_ref[...], hbuf[rows, :])

        @pl.when((j == 7) & (t > 0))
        def _():
            acc[1, 0:kv, :] += _mm_tn(dzkv_ref[...], hbuf[rows, :])

        def block_total(sl):
            return pltpu.make_async_copy(acc.at[sl], gpack_ref.at[0, pl.ds(0, D)], sems.at[sl])

        for jj in range(8):
            @pl.when((t == last) & (j == jj))
            def _(jj=jj):
                if jj >= 1:
                    block_total((jj - 1) % 2).wait()
                    exchange(jj - 1)
                if jj == 7:
                    _flush_to_pack(acc.at[1, pl.ds(0, kv)], gpack_ref, WT0 + ZKV, sems.at[1])
                    exchange(7)
                    whole = _exchange_copies(gpack_ref, r1_ref, send_sems, recv_sems)[0]
                    whole.wait_recv()
                    whole.wait_send()
                else:
                    for cp in _pack_copies(acc.at[jj % 2], gpack_ref, WT0 + jj * D, sems.at[jj % 2]):
                        cp.start()

    any_spec = pl.BlockSpec(memory_space=pl.ANY)
    return pl.pallas_call(
        body, name="gwt", grid=(8, nt),
        in_specs=[pl.BlockSpec((tt, D), lambda j, t: (jnp.where(j == 7, last, t), jnp.minimum(j, 6))),
                  pl.BlockSpec((tt, kv), lambda j, t: (jnp.where(j == 7, t, 0), 0)), any_spec],
        out_specs=(any_spec, any_spec),
        out_shape=(jax.ShapeDtypeStruct((N_SHARDS, WIN_SHARD, D), F32),
                   jax.ShapeDtypeStruct((N_SHARDS, half, D), F32)),
        scratch_shapes=[pltpu.VMEM((T, D), BF16), pltpu.VMEM((2, D, D), F32), pltpu.SemaphoreType.DMA((nt,)),
                        pltpu.SemaphoreType.DMA((2,)), pltpu.SemaphoreType.DMA((1,)),
                        pltpu.SemaphoreType.DMA((1,))],
        compiler_params=_params(("arbitrary", "arbitrary")),
    )(dz, dz_kv, h)


_BC1 = 1.0 - ADAM_B1 ** ADAM_STEP
_BC2 = 1.0 - ADAM_B2 ** ADAM_STEP


def _adamw_math(w, g, m, v):
    m = ADAM_B1 * m + (1.0 - ADAM_B1) * g
    v = ADAM_B2 * v + (1.0 - ADAM_B2) * (g * g)
    delta = -ADAM_LR * ((m / _BC1) / (jnp.sqrt(v / _BC2) + ADAM_EPS) + ADAM_WD * w)
    return delta, m, v


def _adamw_rows(g, w, m, v, rows, name):
    R, C = w.shape

    def body(g_ref, w_ref, m_ref, v_ref, go_ref, d_ref, nm_ref, nv_ref):
        gv = g_ref[...]
        d, nm, nv = _adamw_math(w_ref[...], gv, m_ref[...], v_ref[...])
        go_ref[...] = gv
        d_ref[...] = d
        nm_ref[...] = nm
        nv_ref[...] = nv

    spec = pl.BlockSpec((rows, C), lambda i: (i, 0))
    shp = jax.ShapeDtypeStruct((R, C), F32)
    return pl.pallas_call(
        body, name=name, grid=(R // rows,), in_specs=[spec] * 4, out_specs=(spec,) * 4,
        out_shape=(shp,) * 4, compiler_params=_params(("arbitrary",)),
    )(g, w, m, v)


def _adamw_square(gfin, ws, ms, vs):
    rb = 64
    nb = SQ_SHARD // rb

    def body(*refs):
        g_refs = refs[0:5]
        w_refs, m_refs, v_refs = refs[5:10], refs[10:15], refs[15:20]
        outs = refs[20:]
        for k in range(5):
            gk = g_refs[k][...]
            d, nm, nv = _adamw_math(w_refs[k][...], gk, m_refs[k][...], v_refs[k][...])
            outs[4 * k][...] = gk
            outs[4 * k + 1][...] = d
            outs[4 * k + 2][...] = nm
            outs[4 * k + 3][...] = nv

    spec = pl.BlockSpec((rb, D), lambda i: (i, 0))
    gspecs = [pl.BlockSpec((rb, D), lambda i, k=k: (SQ_SHARD * k // rb + i, 0)) for k in range(5)]
    shp = jax.ShapeDtypeStruct((SQ_SHARD, D), F32)
    res = pl.pallas_call(
        body, name="adamw_square", grid=(nb,), in_specs=gspecs + [spec] * 15, out_specs=(spec,) * 20,
        out_shape=(shp,) * 20, compiler_params=_params(("arbitrary",)),
    )(*([gfin] * 5), *ws, *ms, *vs)
    return [tuple(res[4 * k:4 * k + 4]) for k in range(5)]


def _adamw_small(gs, ws, ms, vs):
    n = len(gs)

    def body(*refs):
        outs = refs[4 * n:]
        for k in range(n):
            d, nm, nv = _adamw_math(refs[n + k][...], refs[k][...], refs[2 * n + k][...],
                                    refs[3 * n + k][...])
            outs[3 * k][...] = d
            outs[3 * k + 1][...] = nm
            outs[3 * k + 2][...] = nv

    vm = pl.BlockSpec(memory_space=pltpu.VMEM)
    shapes = []
    for w in ws:
        shapes += [jax.ShapeDtypeStruct(w.shape, F32)] * 3
    res = pl.pallas_call(
        body, name="adamw_small", in_specs=[vm] * (4 * n), out_specs=(vm,) * (3 * n),
        out_shape=tuple(shapes),
    )(*gs, *ws, *ms, *vs)
    return [tuple(res[3 * k:3 * k + 3]) for k in range(n)]


def _rope_constants():
    half = ROPE_DIM // 2
    inv = jnp.power(ROPE_THETA, -jnp.arange(0, ROPE_DIM, 2, dtype=F32) / ROPE_DIM)
    freq = jnp.concatenate([inv, jnp.zeros((ROPE_ROWS - half,), F32)]).reshape(ROPE_ROWS, 1)
    spread = np.zeros((3, ROPE_ROWS, BLOCK), np.float32)
    for lane in range(BLOCK):
        d = lane % HEAD_DIM
        if d < ROPE_DIM:
            spread[0, d % half, lane] = 1.0
            spread[1, d % half, lane] = -1.0 if d < half else 1.0
        else:
            spread[2, 0, lane] = 1.0
    return freq, jnp.asarray(spread, BF16)


def kernel(x, p, positions, w_in, ln_pre, ln_post, w_dw, b_dw, conv_ln_g, conv_ln_b, w_pw, sinks, w_br_conv, w_br_attn, w_out, w_ple_gate, w_ple_proj, loss_target, m_w_in, m_ln_pre, m_ln_post, m_w_dw, m_b_dw, m_conv_ln_g, m_conv_ln_b, m_w_pw, m_sinks, m_w_br_conv, m_w_br_attn, m_w_out, m_w_ple_gate, m_w_ple_proj, v_w_in, v_ln_pre, v_ln_post, v_w_dw, v_b_dw, v_conv_ln_g, v_conv_ln_b, v_w_pw, v_sinks, v_w_br_conv, v_w_br_attn, v_w_out, v_w_ple_gate, v_w_ple_proj):
    nb, S, _ = x.shape
    T = nb * S
    xc = lax.axis_index("x")
    yc = lax.axis_index("y")
    cc = lax.axis_index("c")
    shard = 2 * xc + yc

    sq_w = (w_pw, w_br_conv, w_br_attn, w_out, w_ple_gate)
    wdw_shard = jnp.pad(w_dw[0], ((0, 1), (0, 0)))
    x2 = x.reshape(T, D)

    tgt = loss_target.reshape(T, D)
    p2 = p.reshape(T, PLE)
    sinks1 = sinks.reshape(N_HEADS)

    tm = min(TILE_TOKEN, S)
    tc = min(TILE_CONV, S)
    tq = min(TILE_ATTN, S)

    z, zkv, wt, wdw_all, h, cos_t, sin_t = _inproj(
        x2, ln_pre, positions.astype(F32).reshape(1, T), *_rope_constants(), w_in[0].T.astype(BF16), wdw_shard,
        min(TILE_PROJ, T // 2))
    wdw = jnp.concatenate([wdw_all[s] for s in range(N_SHARDS)], axis=1)
    sq_shards = [w[0].astype(BF16) for w in sq_w] + [w_ple_proj[0].T.reshape(WPP_SHARD, D).astype(BF16)]
    o, wall_a = _attn_fwd(z, zkv, cos_t, sin_t, sinks1, S, tq, GROUP_CONV, sq_shards[0:2])
    ya, y, rstd, pw, wall_b, wppf = _conv_fwd(z, wdw, b_dw, conv_ln_g, conv_ln_b, wall_a, S, tc, GROUP_TAIL,
                                              sq_shards[2:])
    wppt = wppf.reshape(D, PLE)
    loss_p, dx1, dm, yb, g_ln_post, gsq, gw_ppt = _tail_a(x2, tgt, p2, o, ya, z, ln_post, wall_b, wppt, tm)

    cidx = jnp.reshape(cc, (1,)).astype(jnp.int32)
    scidx = jnp.stack([shard, cc]).astype(jnp.int32)

    def landing(pack, n, dtype):
        return jax.ShapeDtypeStruct((n, pack.shape[1] // 2, D), dtype)

    dz, do, dc, gvec, gsq = _tail_b(dm, ya, yb, o, z, pw, y, rstd, conv_ln_g, conv_ln_b, wall_a, wall_b,
                                    gw_ppt.reshape(PLE, D), gsq, tm)
    dz, g_wdw, r1_sq = _conv_bwd(dc, z, wdw, dz, S, tc, _exchange_copies, gsq, landing(gsq, N_SHARDS, F32))
    dz, dkv, g_sinks, r2_sq = _attn_bwd(z, zkv, o, do, cos_t, sin_t, sinks1, dz, S, tq, gsq, r1_sq,
                                        landing(gsq, 3, BF16))
    gwt_pack, r1_wt = _gwt(dz, dkv, h, min(2 * TILE_PROJ, T))
    cs_wt = _chip_sum(cidx, gwt_pack, r1_wt, "chip_sum_wt")
    gx, g_ln_pre, r2_wt = _dh(dz, dkv, wt, x2, dx1, ln_pre, min(TILE_RESIDENT, T // 2), _chip_sum_copies, cs_wt,
                              landing(gwt_pack, 3, BF16))
    row37 = jnp.concatenate([g_sinks[0:1, 0:N_HEADS], loss_p, jnp.zeros((1, D - N_HEADS - 1), F32)], axis=1)
    vec = jnp.concatenate([g_wdw, g_ln_pre, g_ln_post, gvec[2:3], gvec[0:1], gvec[1:2], row37,
                           jnp.zeros((VEC_ROWS - 38, D), F32)], axis=0)
    gfin_wt, gfin_sq, tot = _finish_reduce(_final_half(scidx, gwt_pack, r1_wt, r2_wt, "final_half_wt"),
                                           _final_half(scidx, gsq, r1_sq, r2_sq, "final_half_sq"), vec)

    g_w_in, d_w_in, nm_w_in, nv_w_in = [a.T for a in _adamw_rows(
        gfin_wt, w_in[0].T, m_w_in[0].T, v_w_in[0].T, WIN_SHARD // 8, "adamw_w_in")]
    g_w_in = g_w_in[None]
    sq_m = (m_w_pw, m_w_br_conv, m_w_br_attn, m_w_out, m_w_ple_gate)
    sq_v = (v_w_pw, v_w_br_conv, v_w_br_attn, v_w_out, v_w_ple_gate)
    sq_res = _adamw_square(gfin_sq, [w[0] for w in sq_w], [m[0] for m in sq_m], [v[0] for v in sq_v])
    g_wpp = gfin_sq[5 * SQ_SHARD:SQ_PACK].reshape(PLE, PLE).T
    g_dw_all = tot[0:CONV_K]
    g_dw = lax.dynamic_slice_in_dim(g_dw_all, shard * PLE, PLE, axis=1)
    small_g = [g_wpp, g_dw, tot[32:33], tot[33:34], tot[34:35], tot[35:36], tot[36:37],
               tot[37:38, 0:N_HEADS]]
    small_w = [w_ple_proj[0], w_dw[0], ln_pre, ln_post, b_dw, conv_ln_g, conv_ln_b, sinks]
    small_m = [m_w_ple_proj[0], m_w_dw[0], m_ln_pre, m_ln_post, m_b_dw, m_conv_ln_g, m_conv_ln_b, m_sinks]
    small_v = [v_w_ple_proj[0], v_w_dw[0], v_ln_pre, v_ln_post, v_b_dw, v_conv_ln_g, v_conv_ln_b, v_sinks]
    small = _adamw_small(small_g, small_w, small_m, small_v)

    loss = tot[37, N_HEADS]
    grads = [g_w_in, small_g[2], small_g[3], g_dw[None], small_g[4], small_g[5], small_g[6],
             sq_res[0][0][None], small_g[7], sq_res[1][0][None], sq_res[2][0][None], sq_res[3][0][None],
             sq_res[4][0][None], g_wpp[None]]

    def triple(i):
        w_in_t = (d_w_in[None], nm_w_in[None], nv_w_in[None])
        sq = lambda k: tuple(a[None] for a in sq_res[k][1:4])
        sm = lambda k, lead: tuple(a[None] if lead else a for a in small[k])
        return [w_in_t[i], sm(2, False)[i], sm(3, False)[i], sm(1, True)[i], sm(4, False)[i],
                sm(5, False)[i], sm(6, False)[i], sq(0)[i], sm(7, False)[i], sq(1)[i], sq(2)[i], sq(3)[i],
                sq(4)[i], sm(0, True)[i]]

    return (loss, gx.reshape(nb, S, D), *grads, *triple(0), *triple(1), *triple(2))
```

```python
import functools

import jax
import jax.numpy as jnp
import numpy as np
from jax import lax
from jax.experimental import pallas as pl
from jax.experimental.pallas import tpu as pltpu

F32 = jnp.float32
BF16 = jnp.bfloat16

D = 1024
PLE = 256
N_HEADS = 16
HEAD_DIM = 64
BLOCK = 128
CONV_K = 31
ROPE_DIM = 16
ROPE_THETA = 500000.0
EPS = 1e-6
IN_WIDTH = 7424
N_SHARDS = 4

ADAM_LR = 0.001
ADAM_B1 = 0.9
ADAM_B2 = 0.999
ADAM_EPS = 1e-08
ADAM_WD = 0.01
ADAM_STEP = 10

WT0 = 5 * D
WPP0 = WT0 + IN_WIDTH
WIN_SHARD = IN_WIDTH // N_SHARDS
SQ_SHARD = D // N_SHARDS
WPP_SHARD = PLE * PLE // D
PACK_ROWS = WIN_SHARD + 5 * SQ_SHARD + WPP_SHARD
VMEM_LIMIT = 56 * 1024 * 1024
MESH = pl.DeviceIdType.MESH
TILE_RESIDENT = 512
TILE_PROJ = 1024
TILE_TOKEN = 256
TILE_CONV = 1024
TILE_ATTN = 1024


ZB_AGATE, ZB_GCONV, ZB_GATTN, ZB_CGATE, ZB_CVAL, ZB_CGLU, ZB_Q = range(7)
ZKV = 7 * D
_SEGMENTS = ((0, D, ZB_CVAL * D), (D, D, ZB_CGLU * D), (2 * D, D, ZB_CGATE * D), (3 * D, D, ZB_Q * D),
             (4 * D, 2 * BLOCK, ZKV), (4 * D + 2 * BLOCK, D, ZB_AGATE * D),
             (5 * D + 2 * BLOCK, D, ZB_GCONV * D), (6 * D + 2 * BLOCK, D, ZB_GATTN * D))
_WT_CUTS = (0, 192, 640, 1216, WIN_SHARD)


def _zp_row(o):
    for a, w, zp in _SEGMENTS:
        if a <= o < a + w:
            return zp + o - a
    raise ValueError(o)


def _pieces(s):
    out = []
    for a, b in zip(_WT_CUTS[:-1], _WT_CUTS[1:]):
        first = _zp_row(WIN_SHARD * s + a)
        assert _zp_row(WIN_SHARD * s + b - 1) == first + b - a - 1
        out.append((a, b - a, WT0 + first))
    for k in range(5):
        out.append((WIN_SHARD + SQ_SHARD * k, SQ_SHARD, D * k + SQ_SHARD * s))
    out.append((WIN_SHARD + 5 * SQ_SHARD, WPP_SHARD, WPP0 + WPP_SHARD * s))
    return out


N_PIECES = len(_pieces(0))


def _wall_segments(wall0, rows):
    out = []
    for s in range(N_SHARDS):
        for pr, n, wr in _pieces(s):
            lo, hi = max(wr, wall0), min(wr + n, wall0 + rows)
            if lo < hi:
                out.append((lo - wall0, hi - lo, s, pr + lo - wr))
    assert sum(n for _, n, _, _ in out) == rows
    return out


def _sel(s, vals):
    r = jnp.int32(vals[0])
    for i in range(1, len(vals)):
        r = jnp.where(s == i, jnp.int32(vals[i]), r)
    return r


def _sig(x):
    return 1.0 / (1.0 + jnp.exp(-x))


def _mm(a, b):
    return lax.dot_general(a, b, (((1,), (0,)), ((), ())), preferred_element_type=F32)


def _mm_nt(a, b):
    return lax.dot_general(a, b, (((1,), (1,)), ((), ())), preferred_element_type=F32)


def _mm_tn(a, b):
    return lax.dot_general(a, b, (((0,), (0,)), ((), ())), preferred_element_type=F32)


def _params(sem=None):
    return pltpu.CompilerParams(dimension_semantics=sem, vmem_limit_bytes=VMEM_LIMIT)


def _flush_to_pack(acc_ref, gpack_ref, wall0, sem):
    for cp in _pack_copies(acc_ref, gpack_ref, wall0, sem):
        cp.start()
        cp.wait()


def _flush_all(items, gpack_ref):
    for acc_ref, wall0, sem in items:
        for cp in _pack_copies(acc_ref, gpack_ref, wall0, sem):
            cp.start()
    for acc_ref, _, sem in items:
        pltpu.make_async_copy(acc_ref, gpack_ref.at[0, pl.ds(0, acc_ref.shape[0])], sem).wait()


def _pack_copies(acc_ref, gpack_ref, wall0, sem):
    base = 0 if gpack_ref.shape[1] == WIN_SHARD else WIN_SHARD
    out = []
    for r, n, s, pr in _wall_segments(wall0, acc_ref.shape[0]):
        assert 0 <= pr - base and pr - base + n <= gpack_ref.shape[1]
        out.append(pltpu.make_async_copy(acc_ref.at[pl.ds(r, n)], gpack_ref.at[s, pl.ds(pr - base, n)], sem))
    return out


def _coords():
    return lax.axis_index("x"), lax.axis_index("y"), lax.axis_index("c")


def _chip_peers(x, y):
    return [(1 - x, y), (x, 1 - y), (1 - x, 1 - y)]


WIN_PIECES = tuple(range(len(_WT_CUTS) - 1))
SQ_PIECES = tuple(range(len(WIN_PIECES), N_PIECES))


def _gather_ops(group, src, landing, bytes_ref, stage, send_sems, recv_sems, loc_sem):
    sizes = [_pieces(0)[p][1] for p in group]
    half_rows = sum(n // 2 for n in sizes)
    starts = [sum(sizes[:i]) for i in range(len(sizes))]

    def rcopy(a, b, k, dev):
        return pltpu.make_async_remote_copy(src_ref=a, dst_ref=b, send_sem=send_sems.at[k],
                                            recv_sem=recv_sems.at[k], device_id=dev, device_id_type=MESH)

    def total(k):
        x, y, c = _coords()
        rows = bytes_ref.at[pl.ds(0, half_rows)]
        return rcopy(rows, rows, k, (x, y, c))

    def own_total():
        rows = stage.at[pl.ds(0, sum(sizes))]
        return pltpu.make_async_copy(rows, rows, loc_sem)

    def send():
        x, y, c = _coords()
        s_me = 2 * x + y
        for k, (px, py) in enumerate(_chip_peers(x, y)):
            for p, n in zip(group, sizes):
                h = n // 2
                rcopy(src(p, c * h, h), landing(p, s_me, c * h, h), k, (px, py, c)).start()
        for p, n, r in zip(group, sizes, starts):
            pltpu.make_async_copy(src(p, 0, n), stage.at[pl.ds(r, n)], loc_sem).start()

    def forward():
        x, y, c = _coords()
        own_total().wait()
        for p, n, r in zip(group, sizes, starts):
            pltpu.make_async_copy(stage.at[pl.ds(r, n)], landing(p, 2 * x + y, 0, n), loc_sem).start()
        for k, (px, py) in enumerate(_chip_peers(x, y)):
            total(k).wait_recv()
            for p, n in zip(group, sizes):
                rows = landing(p, 2 * px + py, c * (n // 2), n // 2)
                rcopy(rows, rows, 3 + k, (x, y, 1 - c)).start()

    def finish():
        own_total().wait()
        for k in range(3):
            total(3 + k).wait_recv()
        for k in range(6):
            total(k).wait_send()

    return send, forward, finish


def _piece_rows(ref, start, off, n):
    first = start + off
    return ref.at[pl.ds(first if isinstance(first, int) else pl.multiple_of(first, 32), n)]


GROUP_CONV = SQ_PIECES[0:2]
GROUP_TAIL = SQ_PIECES[2:]


def _group_shapes(group):
    n_sq = sum(1 for q in group if q != N_PIECES - 1)
    return [jax.ShapeDtypeStruct((n_sq * D, D), BF16)] + (
        [jax.ShapeDtypeStruct((PLE, D), BF16)] if N_PIECES - 1 in group else [])


def _group_scratch(group):
    return [pltpu.VMEM((sum(_pieces(0)[q][1] for q in group), D), BF16), pltpu.SemaphoreType.DMA((6,)),
            pltpu.SemaphoreType.DMA((6,)), pltpu.SemaphoreType.DMA]


def _group_gather(group, shard_refs, out_refs, scratch, step, n_steps):
    wall_ref = out_refs[0]
    stage, send_sems, recv_sems, loc_sem = scratch

    def src(q, off, n):
        return _piece_rows(shard_refs[group.index(q)], 0, off, n)

    def landing(q, s, off, n):
        if q == N_PIECES - 1:
            return _piece_rows(out_refs[1], WPP_SHARD * s, off, n)
        return _piece_rows(wall_ref, D * group.index(q) + SQ_SHARD * s, off, n)

    send, forward, finish = _gather_ops(group, src, landing, wall_ref, stage, send_sems, recv_sems, loc_sem)
    pl.when(step == 0)(send)
    pl.when(step == n_steps // 2)(forward)
    return finish


ROPE_ROWS = 16


def _rope_tables(pos, freq, spread_ref):
    def to_lanes(v, e):
        out = None
        for _ in range(3):
            part = v.astype(BF16)
            term = _mm_tn(part, e)
            out = term if out is None else out + term
            v = v - part.astype(F32)
        return out

    ang = freq * pos
    return (to_lanes(jnp.cos(ang), spread_ref[0]) + spread_ref[2, 0:1, :].astype(F32),
            to_lanes(jnp.sin(ang), spread_ref[1]))


SQ_PACK = PACK_ROWS - WIN_SHARD


def _row_tile(half):
    return max(t for t in range(8, 321, 8) if half % t == 0)


def _exchange_copies(g_ref, r1_ref, send_sems, recv_sems):
    x, y, c = _coords()
    half = g_ref.shape[1] // 2
    return [pltpu.make_async_remote_copy(
        src_ref=g_ref.at[:, pl.ds(pl.multiple_of((1 - c) * half, 32), half), :], dst_ref=r1_ref,
        send_sem=send_sems.at[0], recv_sem=recv_sems.at[0], device_id=(x, y, 1 - c), device_id_type=MESH)]


def _chip_sum_copies(cs_ref, r2_ref, send_sems, recv_sems):
    x, y, c = _coords()
    return [pltpu.make_async_remote_copy(
        src_ref=cs_ref.at[2 * px + py], dst_ref=r2_ref.at[k], send_sem=send_sems.at[k],
        recv_sem=recv_sems.at[k], device_id=(px, py, c), device_id_type=MESH)
        for k, (px, py) in enumerate(_chip_peers(x, y))]


def _chip_sum(cidx, gpack, r1, name):
    half = gpack.shape[1] // 2
    rt = _row_tile(half)

    def body(c_ref, g_ref, r_ref, o_ref):
        o_ref[...] = (g_ref[...] + r_ref[...]).astype(BF16)

    nt = half // rt
    return pl.pallas_call(
        body, name=name,
        grid_spec=pltpu.PrefetchScalarGridSpec(
            num_scalar_prefetch=1, grid=(N_SHARDS, nt),
            in_specs=[pl.BlockSpec((1, rt, D), lambda s, t, c: (s, c[0] * nt + t, 0)),
                      pl.BlockSpec((1, rt, D), lambda s, t, c: (s, t, 0))],
            out_specs=pl.BlockSpec((1, rt, D), lambda s, t, c: (s, t, 0))),
        out_shape=jax.ShapeDtypeStruct((N_SHARDS, half, D), BF16),
        compiler_params=_params(("arbitrary", "arbitrary")),
    )(cidx, gpack, r1)


def _final_half(sc, gpack, r1, r2, name):
    rows = gpack.shape[1]
    half = rows // 2
    rt = _row_tile(half)

    def body(sc_ref, g_ref, r_ref, p_ref, o_ref):
        acc = g_ref[0] + r_ref[0]
        for k in range(3):
            acc = acc + p_ref[k].astype(F32)
        o_ref[...] = acc

    nt = half // rt
    return pl.pallas_call(
        body, name=name,
        grid_spec=pltpu.PrefetchScalarGridSpec(
            num_scalar_prefetch=1, grid=(nt,),
            in_specs=[pl.BlockSpec((1, rt, D), lambda t, sc: (sc[0], sc[1] * nt + t, 0)),
                      pl.BlockSpec((1, rt, D), lambda t, sc: (sc[0], t, 0)),
                      pl.BlockSpec((3, rt, D), lambda t, sc: (0, t, 0))],
            out_specs=pl.BlockSpec((rt, D), lambda t, sc: (sc[1] * nt + t, 0))),
        out_shape=jax.ShapeDtypeStruct((rows, D), F32),
        compiler_params=_params(("arbitrary",)),
    )(sc, gpack, r1, r2)


VEC_ROWS = 40


def _finish_reduce(fwt, fsq, vec):
    def body(fwt_ref, fsq_ref, v_ref, owt_ref, osq_ref, tot_ref, buf, send_sems, recv_sems):
        x, y, c = _coords()
        swaps = []
        for k, (f_ref, o_ref) in enumerate(((fwt_ref, owt_ref), (fsq_ref, osq_ref))):
            half = f_ref.shape[0] // 2
            rows = pl.ds(pl.multiple_of(c * half, 32), half)
            swaps.append(pltpu.make_async_remote_copy(
                src_ref=f_ref.at[rows], dst_ref=o_ref.at[rows], send_sem=send_sems.at[7 + k],
                recv_sem=recv_sems.at[7 + k], device_id=(x, y, 1 - c), device_id_type=MESH))
        for cp in swaps:
            cp.start()
        me = 4 * x + 2 * y + c
        buf[me] = v_ref[...]
        cps = []
        for r in range(1, 8):
            dx, dy, dc = (r >> 2) & 1, (r >> 1) & 1, r & 1
            peer = (1 - x if dx else x, 1 - y if dy else y, 1 - c if dc else c)
            cp = pltpu.make_async_remote_copy(
                src_ref=v_ref, dst_ref=buf.at[me], send_sem=send_sems.at[r - 1],
                recv_sem=recv_sems.at[r - 1], device_id=peer, device_id_type=MESH)
            cp.start()
            cps.append(cp)
        for cp in cps:
            cp.wait_recv()
        for cp in cps:
            cp.wait_send()
        acc = buf[0]
        for d in range(1, 8):
            acc = acc + buf[d]
        tot_ref[...] = acc
        for cp in swaps:
            cp.wait()

    any_spec = pl.BlockSpec(memory_space=pl.ANY)
    vm = pl.BlockSpec(memory_space=pltpu.VMEM)
    return pl.pallas_call(
        body, name="finish_reduce",
        out_shape=(jax.ShapeDtypeStruct(fwt.shape, F32), jax.ShapeDtypeStruct(fsq.shape, F32),
                   jax.ShapeDtypeStruct((VEC_ROWS, D), F32)),
        in_specs=[any_spec, any_spec, vm], out_specs=(any_spec, any_spec, vm),
        input_output_aliases={0: 0, 1: 1},
        scratch_shapes=[pltpu.VMEM((8, VEC_ROWS, D), F32), pltpu.SemaphoreType.DMA((9,)),
                        pltpu.SemaphoreType.DMA((9,))],
    )(fwt, fsq, vec)


SOLO_ROWS = WIN_SHARD - BLOCK // 2


def _solo_first(s):
    return 0 if s % 2 == 0 else BLOCK // 2


def _solo_segments(s):
    lo = _solo_first(s)
    out = []
    for a, n, wr in _pieces(s)[:len(WIN_PIECES)]:
        b0, b1 = max(a, lo), min(a + n, lo + SOLO_ROWS)
        if b0 >= b1:
            continue
        z0 = wr - WT0 + b0 - a
        if out and out[-1][0] + out[-1][1] == b0 - lo and out[-1][2] + out[-1][1] == z0:
            out[-1] = (out[-1][0], out[-1][1] + b1 - b0, out[-1][2])
        else:
            out.append((b0 - lo, b1 - b0, z0))
    out = [r for o, n, z0 in out for r in
           (((o, ZKV - z0, z0), (o + ZKV - z0, z0 + n - ZKV, ZKV)) if z0 < ZKV < z0 + n else ((o, n, z0),))]
    assert all(v % BLOCK == 0 for seg in out for v in seg) and sum(n for _, n, _ in out) == SOLO_ROWS
    return out


def _shared_tile(pair):
    z0 = _zp_row(WIN_SHARD * (2 * pair) + SOLO_ROWS)
    assert z0 % BLOCK == 0 and _zp_row(WIN_SHARD * (2 * pair + 1)) == z0 + BLOCK // 2
    return z0


def _inproj(x, ln_pre, pos, freq, spread, win_t, wdw_shard, tm):
    T = x.shape[0]
    n_t = T // tm
    assert n_t >= 2 and n_t % 2 == 0
    tables = [[_pieces(s)[p][2] - WT0 for s in range(N_SHARDS)] for p in WIN_PIECES]
    sizes = [_pieces(0)[p][1] for p in WIN_PIECES]
    half_rows = sum(n // 2 for n in sizes)
    relation_of_pass = {1: 1, 2: 0, 3: 2}

    def body(x_ref, g_ref, pos_ref, f_ref, e_ref, win_ref, wdw_ref, z_ref, zkv_ref, wt_ref, wdwall_ref, h_ref,
             cos_ref, sin_ref, wbuf, stage, stage_sh, hbuf, wsend, wrecv, loc_sems, out_sems, sh_sems, h_sems):
        p = pl.program_id(0)
        t = pl.program_id(1)
        x, y, c = _coords()
        s_me = 2 * x + y
        peers = _chip_peers(x, y)
        shard = jnp.bitwise_xor(s_me, p)
        first, last = t == 0, t == n_t - 1

        def rcopy(a, b, k, dev):
            return pltpu.make_async_remote_copy(src_ref=a, dst_ref=b, send_sem=wsend.at[k], recv_sem=wrecv.at[k],
                                                device_id=dev, device_id_type=MESH)

        def total(k):
            rows = wt_ref.at[pl.ds(0, half_rows)]
            return rcopy(rows, rows, k, (x, y, c))

        def in_hbm(q, s, off, n):
            return _piece_rows(wt_ref, _sel(s, tables[q]), off, n)

        def in_vmem(q, s):
            return _piece_rows(wbuf, WIN_SHARD * s + _WT_CUTS[q], 0, sizes[q])

        def send_to(k):
            px, py = peers[k]
            for q, n in zip(WIN_PIECES, sizes):
                rcopy(_piece_rows(win_ref, _WT_CUTS[q], c * (n // 2), n // 2), in_hbm(q, s_me, c * (n // 2), n // 2),
                      k, (px, py, c)).start()

        def forward_from(k):
            px, py = peers[k]
            total(k).wait_recv()
            for q, n in zip(WIN_PIECES, sizes):
                rows = in_hbm(q, 2 * px + py, c * (n // 2), n // 2)
                rcopy(rows, rows, 3 + k, (x, y, 1 - c)).start()

        def shard_total(a, b, sem):
            return pltpu.make_async_copy(a.at[pl.ds(0, WIN_SHARD)], b.at[pl.ds(0, WIN_SHARD)], sem)

        def wdw_copies():
            return [pltpu.make_async_remote_copy(
                src_ref=wdw_ref, dst_ref=wdwall_ref.at[s_me], send_sem=wsend.at[6 + k], recv_sem=wrecv.at[6 + k],
                device_id=(px, py, c), device_id_type=MESH) for k, (px, py) in enumerate(peers)]

        def own_wdw():
            return pltpu.make_async_copy(wdw_ref, wdwall_ref.at[s_me], loc_sems.at[2])

        @pl.when((p == 0) & first)
        def _():
            send_to(0)
            send_to(1)
            own_wdw().start()
            for cp in wdw_copies():
                cp.start()
            for q in WIN_PIECES:
                pltpu.make_async_copy(_piece_rows(win_ref, _WT_CUTS[q], 0, sizes[q]), in_vmem(q, s_me),
                                      loc_sems.at[0]).start()

        for pp, k in relation_of_pass.items():
            pl.when((p == pp - 1) & (t == n_t - 2))(functools.partial(forward_from, k))

            @pl.when((p == pp - 1) & last)
            def _(k=k):
                total(3 + k).wait_recv()
                px, py = peers[k]
                for q in WIN_PIECES:
                    pltpu.make_async_copy(in_hbm(q, 2 * px + py, 0, sizes[q]), in_vmem(q, 2 * px + py),
                                          loc_sems.at[0]).start()

            @pl.when((p == pp) & first)
            def _(pp=pp):
                shard_total(wt_ref, wbuf, loc_sems.at[0]).wait()
                if pp == 1:
                    total(0).wait_send()
                    total(1).wait_send()
                    send_to(2)

        step = p * n_t + t
        slot = step % 2
        rows = pl.ds(pl.multiple_of(t * tm, tm), tm)

        def out_total(sl):
            return pltpu.make_async_copy(stage.at[sl], stage.at[sl], out_sems.at[sl])

        def sh_copy(sl, z0):
            return pltpu.make_async_copy(stage_sh.at[sl], z_ref.at[rows, pl.ds(z0, BLOCK)], sh_sems.at[sl])

        @pl.when(step >= 2)
        def _():
            out_total(slot).wait()

        @pl.when((step >= 2) & (((step - 2) // n_t) % 2 == 1))
        def _():
            sh_copy(slot, 0).wait()

        def h_out(sl):
            return pltpu.make_async_copy(hbuf.at[sl], h_ref.at[rows], h_sems.at[sl])

        def h_in(sl, tile):
            return pltpu.make_async_copy(h_ref.at[pl.ds(pl.multiple_of(tile * tm, tm), tm)], hbuf.at[sl],
                                         h_sems.at[sl])

        @pl.when((p == 0) & (t >= 2))
        def _():
            h_out(slot).wait()

        @pl.when(p == 0)
        def _():
            xv = x_ref[...]
            r = lax.rsqrt(jnp.mean(xv * xv, axis=-1, keepdims=True) + EPS)
            hbuf[slot] = (xv * r * g_ref[...]).astype(BF16)
            h_out(slot).start()
            cos, sin = _rope_tables(pos_ref[...], f_ref[...], e_ref)
            cos_ref[...] = cos
            sin_ref[...] = sin

        @pl.when((p == 1) & first)
        def _():
            h_out(0).wait()
            h_out(1).wait()
            h_in(0, 0).start()

        @pl.when(p >= 1)
        def _():
            h_in(slot, t).wait()

        @pl.when((p >= 1) & (step < N_SHARDS * n_t - 1))
        def _():
            h_in(1 - slot, jnp.where(last, 0, t + 1)).start()

        @pl.when((p == 0) & first)
        def _():
            shard_total(win_ref, wbuf, loc_sems.at[0]).wait()
            for q in WIN_PIECES:
                pltpu.make_async_copy(in_vmem(q, s_me), in_hbm(q, s_me, 0, sizes[q]), loc_sems.at[1]).start()

        solo0 = pl.multiple_of(WIN_SHARD * shard + (BLOCK // 2) * (shard % 2), BLOCK // 2)
        stage[slot] = _mm_nt(hbuf[slot], wbuf[pl.ds(solo0, SOLO_ROWS), :]).astype(BF16)
        for s in range(N_SHARDS):
            @pl.when(shard == s)
            def _(s=s):
                for off, n, z0 in _solo_segments(s):
                    dst = zkv_ref.at[rows] if z0 == ZKV else z_ref.at[rows, pl.ds(z0, n)]
                    pltpu.make_async_copy(stage.at[slot, :, pl.ds(off, n)], dst, out_sems.at[slot]).start()

        @pl.when(p % 2 == 1)
        def _():
            pair = shard // 2
            w0 = pl.multiple_of(2 * WIN_SHARD * pair + SOLO_ROWS, BLOCK // 2)
            z0 = pl.multiple_of(jnp.where(pair == 0, _shared_tile(0), _shared_tile(1)), BLOCK)
            stage_sh[slot] = _mm_nt(hbuf[slot], wbuf[pl.ds(w0, BLOCK), :]).astype(BF16)
            sh_copy(slot, z0).start()

        @pl.when((p == 3) & last)
        def _():
            for k in (2, 3, 4, 5):
                total(k).wait_send()
            shard_total(wbuf, wt_ref, loc_sems.at[1]).wait()
            cps = wdw_copies()
            for cp in cps:
                cp.wait_recv()
            for cp in cps:
                cp.wait_send()
            own_wdw().wait()
            for sl in range(2):
                out_total(sl).wait()
                sh_copy(sl, 0).wait()

    def in_pass0(p, t):
        return jnp.where(p == 0, t, n_t - 1)

    any_spec = pl.BlockSpec(memory_space=pl.ANY)
    return pl.pallas_call(
        body, name="inproj", grid=(N_SHARDS, n_t),
        in_specs=[pl.BlockSpec((tm, D), lambda p, t: (in_pass0(p, t), 0)), pl.BlockSpec((1, D), lambda p, t: (0, 0)),
                  pl.BlockSpec((1, tm), lambda p, t: (0, in_pass0(p, t))),
                  pl.BlockSpec((ROPE_ROWS, 1), lambda p, t: (0, 0)),
                  pl.BlockSpec((3, ROPE_ROWS, BLOCK), lambda p, t: (0, 0, 0)), any_spec, any_spec],
        out_specs=(any_spec,) * 5 + (pl.BlockSpec((tm, BLOCK), lambda p, t: (in_pass0(p, t), 0)),) * 2,
        out_shape=(jax.ShapeDtypeStruct((T, ZKV), BF16), jax.ShapeDtypeStruct((T, 2 * BLOCK), BF16),
                   jax.ShapeDtypeStruct((IN_WIDTH, D), BF16), jax.ShapeDtypeStruct((N_SHARDS, 32, PLE), F32),
                   jax.ShapeDtypeStruct((T, D), BF16), jax.ShapeDtypeStruct((T, BLOCK), F32),
                   jax.ShapeDtypeStruct((T, BLOCK), F32)),
        scratch_shapes=[pltpu.VMEM((IN_WIDTH, D), BF16), pltpu.VMEM((2, tm, SOLO_ROWS), BF16),
                        pltpu.VMEM((2, tm, BLOCK), BF16), pltpu.VMEM((2, tm, D), BF16),
                        pltpu.SemaphoreType.DMA((9,)), pltpu.SemaphoreType.DMA((9,)),
                        pltpu.SemaphoreType.DMA((3,)), pltpu.SemaphoreType.DMA((2,)),
                        pltpu.SemaphoreType.DMA((2,)), pltpu.SemaphoreType.DMA((2,))],
        compiler_params=_params(("arbitrary", "arbitrary")),
    )(x, ln_pre, pos, freq, spread, win_t, wdw_shard)


HALO = 32
CONV_RC = 64
CONV_LC = 256


def _conv_taps(w_ref, src, r0, lane0, offset_of_tap):
    lanes = pl.ds(lane0, CONV_LC)
    out = None
    for b in range(8):
        taps = [k for k in range(CONV_K) if offset_of_tap(k) % 8 == b]
        if not taps:
            continue
        rows = CONV_RC + (8 if b else 0)
        vb = None
        for k in taps:
            term = w_ref[k:k + 1, lanes] * src[pl.ds(r0 + (offset_of_tap(k) - b), rows), lanes]
            vb = term if vb is None else vb + term
        vb = vb[b:b + CONV_RC] if b else vb
        out = vb if out is None else out + vb
    return out


def _conv_fwd(z, wdw, b_dw, ln_g, ln_b, wall, S, tm, group, shards):
    T = z.shape[0]
    nt = S // tm
    hb = tm // HALO
    gathered = _group_shapes(group)

    def body(cv_ref, cg_ref, cgate_ref, hcv_ref, hcg_ref, wdw_ref, bdw_ref, lng_ref, lnb_ref, wpw_ref,
             wbrc_ref, *rest):
        shard_refs, rest = rest[:len(group)], rest[len(group):]
        ya_ref, y_ref, rstd_ref, pw_ref = rest[:4]
        gather_refs, (ubuf, cbuf), gather_scratch = rest[4:4 + len(gathered)], rest[-6:-4], rest[-4:]
        t = pl.program_id(1)
        step = pl.program_id(0) * nt + t
        finish_gather = _group_gather(group, shard_refs, gather_refs, gather_scratch, step, T // tm)
        ubuf[HALO:HALO + tm, :] = cv_ref[...].astype(F32) * _sig(cg_ref[...].astype(F32))
        hu = hcv_ref[...].astype(F32) * _sig(hcg_ref[...].astype(F32))
        ubuf[0:HALO, :] = jnp.where(t > 0, hu, 0.0)
        ubuf[HALO + tm:HALO + tm + 8, :] = jnp.zeros((8, D), F32)

        def chunk(ci, carry):
            r0 = pl.multiple_of(ci * CONV_RC, CONV_RC)
            for lg in range(D // CONV_LC):
                acc = _conv_taps(wdw_ref, ubuf, r0, lg * CONV_LC, lambda k: HALO - (CONV_K - 1) + k)
                cbuf[pl.ds(r0, CONV_RC), pl.ds(lg * CONV_LC, CONV_LC)] = acc
            return carry

        lax.fori_loop(0, tm // CONV_RC, chunk, 0)
        cc = cbuf[...] + bdw_ref[...]
        mu = jnp.mean(cc, axis=-1, keepdims=True)
        dd = cc - mu
        rstd = lax.rsqrt(jnp.mean(dd * dd, axis=-1, keepdims=True) + EPS)
        yn = dd * rstd
        y_ref[...] = yn.astype(BF16)
        rstd_ref[...] = rstd
        n = yn * lng_ref[...] + lnb_ref[...]
        s = n * _sig(n)
        pw = _mm(s.astype(BF16), wpw_ref[...])
        pw_ref[...] = pw.astype(BF16)
        gt = cgate_ref[...].astype(F32)
        ya_in = pw * (gt * _sig(gt))
        ya_ref[...] = _mm(ya_in.astype(BF16), wbrc_ref[...]).astype(BF16)
        pl.when(step == T // tm - 1)(finish_gather)

    def row(b, t):
        return b * nt + t

    def halo(b, t):
        return jnp.maximum(row(b, t) * hb - 1, 0)

    vec = pl.BlockSpec((1, D), lambda b, t: (0, 0))
    tile = lambda j: pl.BlockSpec((tm, D), lambda b, t: (row(b, t), j))
    out_tile = pl.BlockSpec((tm, D), lambda b, t: (row(b, t), 0))
    any_spec = pl.BlockSpec(memory_space=pl.ANY)
    return pl.pallas_call(
        body, name="conv_fwd", grid=(T // S, nt),
        in_specs=[tile(ZB_CVAL), tile(ZB_CGLU), tile(ZB_CGATE),
                  pl.BlockSpec((HALO, D), lambda b, t: (halo(b, t), ZB_CVAL)),
                  pl.BlockSpec((HALO, D), lambda b, t: (halo(b, t), ZB_CGLU)),
                  pl.BlockSpec((32, D), lambda b, t: (0, 0)), vec, vec, vec,
                  pl.BlockSpec((D, D), lambda b, t: (0, 0)),
                  pl.BlockSpec((D, D), lambda b, t: (1, 0))] + [any_spec] * len(group),
        out_specs=(out_tile, out_tile, pl.BlockSpec((tm, 1), lambda b, t: (row(b, t), 0)), out_tile)
        + (any_spec,) * len(gathered),
        out_shape=[jax.ShapeDtypeStruct((T, D), BF16), jax.ShapeDtypeStruct((T, D), BF16),
                   jax.ShapeDtypeStruct((T, 1), F32), jax.ShapeDtypeStruct((T, D), BF16)] + gathered,
        scratch_shapes=[pltpu.VMEM((tm + HALO + 8, D), F32), pltpu.VMEM((tm, D), F32)] + _group_scratch(group),
        compiler_params=_params(("arbitrary", "arbitrary")),
    )(z, z, z, z, z, wdw, b_dw, ln_g, ln_b, wall, wall, *shards)


def _swap_matrix():
    r = lax.broadcasted_iota(jnp.int32, (BLOCK, BLOCK), 0)
    l = lax.broadcasted_iota(jnp.int32, (BLOCK, BLOCK), 1)
    lh = l & (HEAD_DIM - 1)
    half = ROPE_DIM // 2
    hit = ((lh < half) & (r == l + half)) | ((lh >= half) & (lh < ROPE_DIM) & (r == l - half))
    return jnp.where(hit, 1.0, 0.0).astype(BF16)


def _rope(tb, cos, sin, pswap):
    return tb.astype(F32) * cos + _mm(tb, pswap) * sin


def _rope_f32(tv, cos, sin, pswap):
    hi = tv.astype(BF16)
    lo = (tv - hi.astype(F32)).astype(BF16)
    return tv * cos + (_mm(hi, pswap) + _mm(lo, pswap)) * sin


def _kv_variants(kv):
    lane = lax.broadcasted_iota(jnp.int32, kv.shape, 1)
    lo = lane < HEAD_DIM
    sw = pltpu.roll(kv, HEAD_DIM, 1)
    z = jnp.zeros_like(kv)
    g0 = (jnp.where(lo, kv, z).astype(BF16), jnp.where(lo, z, sw).astype(BF16))
    g1 = (jnp.where(lo, sw, z).astype(BF16), jnp.where(lo, z, kv).astype(BF16))
    return (g0, g1)


def _band_mask(nq):
    qi = lax.broadcasted_iota(jnp.int32, (nq * BLOCK, 2 * BLOCK), 0) & (BLOCK - 1)
    sj = lax.broadcasted_iota(jnp.int32, (nq * BLOCK, 2 * BLOCK), 1)
    return (sj <= qi + BLOCK) & (sj > qi), sj


def _sink_rep(sink_ref, g, e):
    return jnp.concatenate(
        [jnp.full((BLOCK, BLOCK), sink_ref[8 * g + 2 * j + e], F32) for j in range(4)], axis=0)


def _softmax_parts(s, valid, sk):
    rows = s.shape[0]
    s = jnp.where(valid, s, -1e30)
    m = jnp.maximum(jnp.broadcast_to(jnp.max(s, axis=-1, keepdims=True), (rows, BLOCK)), sk)
    return jnp.exp(s - jnp.concatenate([m, m], axis=1)), jnp.exp(sk - m)


def _softmax_sink(s, valid, sk):
    p, ps = _softmax_parts(s, valid, sk)
    inv = 1.0 / (_mm(p.astype(BF16), jnp.ones((2 * BLOCK, BLOCK), BF16)) + ps)
    return p * jnp.concatenate([inv, inv], axis=1), ps * inv


def _attn_fwd(z, zkv, cos_t, sin_t, sinks, S, tq, group, shards):
    T = z.shape[0]
    nt = S // tq
    nq = tq // BLOCK
    gathered = _group_shapes(group)

    def body(sink_ref, q_ref, kv_ref, hkv_ref, cos_ref, sin_ref, hcos_ref, hsin_ref, *rest):
        shard_refs, o_ref = rest[:len(group)], rest[len(group)]
        t = pl.program_id(1)
        step = pl.program_id(0) * nt + t
        finish_gather = _group_gather(group, shard_refs, rest[len(group) + 1:-4], rest[-4:], step, T // tq)
        cos = cos_ref[...]
        sin = sin_ref[...]
        pswap = _swap_matrix()
        kv = jnp.concatenate([hkv_ref[...], kv_ref[...]], axis=0)
        cos_k = jnp.concatenate([hcos_ref[...], cos], axis=0)
        sin_k = jnp.concatenate([hsin_ref[...], sin], axis=0)
        kx = _kv_variants(_rope(kv[:, :BLOCK], cos_k, sin_k, pswap))
        one = jnp.ones((tq + BLOCK, BLOCK), BF16)
        vx = [[jnp.concatenate([v, one], axis=1) for v in vg] for vg in _kv_variants(kv[:, BLOCK:].astype(F32))]
        band, sj = _band_mask(4)
        qs = [(_rope(q_ref[:, 128 * hp:128 * hp + 128], cos, sin, pswap) * 0.125).astype(BF16)
              for hp in range(8)]
        for n in range(nq):
            first = (t == 0) & (n == 0)
            valid = band & (jnp.logical_not(first) | (sj >= BLOCK))
            r0 = n * BLOCK
            for g in range(2):
                lhs = jnp.concatenate([qs[4 * g + j][r0:r0 + BLOCK] for j in range(4)], axis=0)
                acc = jnp.zeros((4 * BLOCK, BLOCK), F32)
                for e in range(2):
                    s = _mm_nt(lhs, kx[g][e][r0:r0 + 2 * BLOCK])
                    p, ps = _softmax_parts(s, valid, _sink_rep(sink_ref, g, e))
                    r = _mm(p.astype(BF16), vx[g][e][r0:r0 + 2 * BLOCK])
                    acc = acc + r[:, 0:BLOCK] * (1.0 / (r[:, BLOCK:2 * BLOCK] + ps))
                for j in range(4):
                    o_ref[r0:r0 + BLOCK, 128 * (4 * g + j):128 * (4 * g + j) + 128] = (
                        acc[j * BLOCK:(j + 1) * BLOCK].astype(BF16))
        pl.when(step == T // tq - 1)(finish_gather)

    def row(b, t):
        return b * nt + t

    def halo(b, t):
        return jnp.maximum(row(b, t) * nq - 1, 0)

    any_spec = pl.BlockSpec(memory_space=pl.ANY)
    return pl.pallas_call(
        body, name="attn_fwd", grid=(T // S, nt),
        in_specs=[pl.BlockSpec(memory_space=pltpu.SMEM),
                  pl.BlockSpec((tq, D), lambda b, t: (row(b, t), ZB_Q)),
                  pl.BlockSpec((tq, 2 * BLOCK), lambda b, t: (row(b, t), 0)),
                  pl.BlockSpec((BLOCK, 2 * BLOCK), lambda b, t: (halo(b, t), 0)),
                  pl.BlockSpec((tq, BLOCK), lambda b, t: (row(b, t), 0)),
                  pl.BlockSpec((tq, BLOCK), lambda b, t: (row(b, t), 0)),
                  pl.BlockSpec((BLOCK, BLOCK), lambda b, t: (halo(b, t), 0)),
                  pl.BlockSpec((BLOCK, BLOCK), lambda b, t: (halo(b, t), 0))] + [any_spec] * len(group),
        out_specs=(pl.BlockSpec((tq, D), lambda b, t: (row(b, t), 0)),) + (any_spec,) * len(gathered),
        out_shape=[jax.ShapeDtypeStruct((T, D), BF16)] + gathered,
        scratch_shapes=_group_scratch(group),
        compiler_params=_params(("arbitrary", "arbitrary")),
    )(sinks, z, zkv, zkv, cos_t, sin_t, cos_t, sin_t, *shards)


def _tail_a(x, tgt, p, o, ya, z, ln_post, wall_b, wppt, tm):
    T = x.shape[0]
    last = T // tm - 1

    def body(x_ref, tgt_ref, p_ref, o_ref, ya_ref, ag_ref, gc_ref, ga_ref, lnp_ref, wbra_ref, wout_ref,
             wpg_ref, wppt_ref, loss_ref, dx1_ref, dm_ref, yb_ref, glnp_ref, gpack_ref, gwpp_ref,
             acc_out, acc_pg, sem):
        i = pl.program_id(0)

        @pl.when(i == 0)
        def _():
            acc_out[...] = jnp.zeros_like(acc_out)
            acc_pg[...] = jnp.zeros_like(acc_pg)
            gwpp_ref[...] = jnp.zeros_like(gwpp_ref)
            glnp_ref[...] = jnp.zeros_like(glnp_ref)
            loss_ref[...] = jnp.zeros_like(loss_ref)

        ag = ag_ref[...].astype(F32)
        yb_in = (o_ref[...].astype(F32) * (ag * _sig(ag))).astype(BF16)
        yb = _mm(yb_in, wbra_ref[...])
        yb_ref[...] = yb.astype(BF16)
        m = (_sig(gc_ref[...].astype(F32)) * ya_ref[...].astype(F32)
             + _sig(ga_ref[...].astype(F32)) * yb).astype(BF16)
        mo = _mm(m, wout_ref[...])
        r2 = lax.rsqrt(jnp.mean(mo * mo, axis=-1, keepdims=True) + EPS)
        nrm = mo * r2
        g_post = lnp_ref[...]
        x1 = x_ref[...] + nrm * g_post
        x1b = x1.astype(BF16)
        gate = _sig(_mm(x1b, wpg_ref[...]))
        pb = p_ref[...].astype(BF16)
        pp = _mm_nt(pb, wppt_ref[...])
        err = x1 + gate * pp - tgt_ref[...]
        loss_ref[...] += 0.5 * jnp.sum(jnp.sum(err * err, axis=-1, keepdims=True) * (1.0 / D),
                                       axis=0, keepdims=True)
        dx2 = err * (1.0 / D)
        dgp = (dx2 * pp * gate * (1.0 - gate)).astype(BF16)
        dpp = (dx2 * gate).astype(BF16)
        dx1 = dx2 + _mm_nt(dgp, wpg_ref[...])
        dx1_ref[...] = dx1
        acc_pg[...] += _mm_tn(x1b, dgp)
        gwpp_ref[...] += _mm_tn(dpp, pb)
        glnp_ref[...] += jnp.sum(dx1 * nrm, axis=0, keepdims=True)
        a = dx1 * g_post
        dmo = (r2 * (a - nrm * jnp.mean(a * nrm, axis=-1, keepdims=True))).astype(BF16)
        dm_ref[...] = _mm_nt(dmo, wout_ref[...]).astype(BF16)
        acc_out[...] += _mm_tn(m, dmo)

        @pl.when(i == last)
        def _():
            _flush_all([(acc_out, 3 * D, sem.at[0]), (acc_pg, 4 * D, sem.at[1])], gpack_ref)

    tile = pl.BlockSpec((tm, D), lambda i: (i, 0))
    ztile = lambda j: pl.BlockSpec((tm, D), lambda i: (i, j))
    wsq = lambda k: pl.BlockSpec((D, D), lambda i: (k, 0))
    const = lambda shp: pl.BlockSpec(shp, lambda i: (0, 0))
    any_spec = pl.BlockSpec(memory_space=pl.ANY)
    return pl.pallas_call(
        body, name="tail_a", grid=(T // tm,),
        in_specs=[tile, tile, pl.BlockSpec((tm, PLE), lambda i: (i, 0)), tile, tile, ztile(ZB_AGATE),
                  ztile(ZB_GCONV), ztile(ZB_GATTN), const((1, D)), wsq(0), wsq(1), wsq(2), const((D, PLE))],
        out_specs=(const((1, 1)), tile, tile, tile, const((1, D)), any_spec, const((D, PLE))),
        out_shape=(jax.ShapeDtypeStruct((1, 1), F32), jax.ShapeDtypeStruct((T, D), F32),
                   jax.ShapeDtypeStruct((T, D), BF16), jax.ShapeDtypeStruct((T, D), BF16),
                   jax.ShapeDtypeStruct((1, D), F32), jax.ShapeDtypeStruct((N_SHARDS, SQ_PACK, D), F32),
                   jax.ShapeDtypeStruct((D, PLE), F32)),
        scratch_shapes=[pltpu.VMEM((D, D), F32), pltpu.VMEM((D, D), F32), pltpu.SemaphoreType.DMA((2,))],
        compiler_params=_params(("arbitrary",)),
    )(x, tgt, p, o, ya, z, z, z, ln_post, wall_b, wall_b, wall_b, wppt)


def _dsilu(v, sg):
    return sg * (1.0 + v * (1.0 - sg))


def _tail_b(dm, ya, yb, o, z, pw, y, rstd, ln_g, ln_b, wall_a, wall_b, gppt, gpack, tm):
    T = dm.shape[0]
    last = T // tm - 1

    def body(dm_ref, ya_ref, yb_ref, o_ref, ag_ref, gc_ref, ga_ref, cgate_ref, pw_ref, y_ref, rstd_ref,
             lng_ref, lnb_ref, wpw_ref, wbrc_ref, wbra_ref, gppt_ref, gpack_in, dg_ref, do_ref, dc_ref,
             gvec_ref, gpack_ref, acc_bra, acc_brc, acc_pw, sem):
        i = pl.program_id(0)

        @pl.when(i == 0)
        def _():
            acc_bra[...] = jnp.zeros_like(acc_bra)
            acc_brc[...] = jnp.zeros_like(acc_brc)
            acc_pw[...] = jnp.zeros_like(acc_pw)
            gvec_ref[...] = jnp.zeros_like(gvec_ref)

        g = lng_ref[...]

        def part(rs):
            dm_v = dm_ref[rs, :].astype(F32)
            sgc = _sig(gc_ref[rs, :].astype(F32))
            sga = _sig(ga_ref[rs, :].astype(F32))
            dya = (dm_v * sgc).astype(BF16)
            dyb = (dm_v * sga).astype(BF16)
            dg_ref[rs, D:2 * D] = (dm_v * ya_ref[rs, :].astype(F32) * sgc * (1.0 - sgc)).astype(BF16)
            dg_ref[rs, 2 * D:3 * D] = (dm_v * yb_ref[rs, :].astype(F32) * sga * (1.0 - sga)).astype(BF16)
            ag = ag_ref[rs, :].astype(F32)
            sag = _sig(ag)
            sa = ag * sag
            ov = o_ref[rs, :].astype(F32)
            dyb_in = _mm_nt(dyb, wbra_ref[...])
            do_ref[rs, :] = (dyb_in * sa).astype(BF16)
            dg_ref[rs, 0:D] = (dyb_in * ov * _dsilu(ag, sag)).astype(BF16)
            gt = cgate_ref[rs, :].astype(F32)
            sgt = _sig(gt)
            sgate = gt * sgt
            pw = pw_ref[rs, :].astype(F32)
            dya_in = _mm_nt(dya, wbrc_ref[...])
            dpw = (dya_in * sgate).astype(BF16)
            dg_ref[rs, 3 * D:4 * D] = (dya_in * pw * _dsilu(gt, sgt)).astype(BF16)
            yn = y_ref[rs, :].astype(F32)
            n = yn * g + lnb_ref[...]
            sn = _sig(n)
            dn = _mm_nt(dpw, wpw_ref[...]) * _dsilu(n, sn)
            dy = dn * g
            dc = rstd_ref[rs, :] * (dy - jnp.mean(dy, axis=-1, keepdims=True)
                                    - yn * jnp.mean(dy * yn, axis=-1, keepdims=True))
            dc_ref[rs, :] = dc.astype(BF16)
            sums = (jnp.sum(dn * yn, axis=0, keepdims=True), jnp.sum(dn, axis=0, keepdims=True),
                    jnp.sum(dc, axis=0, keepdims=True))
            return ((ov * sa).astype(BF16), dyb, (pw * sgate).astype(BF16), dya, (n * sn).astype(BF16), dpw,
                    sums)

        pt = part(pl.ds(0, tm))
        acc_bra[...] += _mm_tn(pt[0], pt[1])
        acc_brc[...] += _mm_tn(pt[2], pt[3])
        acc_pw[...] += _mm_tn(pt[4], pt[5])
        for j in range(3):
            gvec_ref[j:j + 1, :] += pt[6][j]

        @pl.when(i == last)
        def _():
            _flush_all([(acc_pw, 0, sem.at[0]), (acc_brc, D, sem.at[1]), (acc_bra, 2 * D, sem.at[2]),
                        (gppt_ref, WPP0, sem.at[3])], gpack_ref)

    tile = pl.BlockSpec((tm, D), lambda i: (i, 0))
    ztile = lambda j: pl.BlockSpec((tm, D), lambda i: (i, j))
    wsq = lambda k: pl.BlockSpec((D, D), lambda i: (k, 0))
    const = lambda shp: pl.BlockSpec(shp, lambda i: (0, 0))
    any_spec = pl.BlockSpec(memory_space=pl.ANY)
    return pl.pallas_call(
        body, name="tail_b", grid=(T // tm,),
        in_specs=[tile, tile, tile, tile, ztile(ZB_AGATE), ztile(ZB_GCONV), ztile(ZB_GATTN), ztile(ZB_CGATE),
                  tile, tile, pl.BlockSpec((tm, 1), lambda i: (i, 0)), const((1, D)), const((1, D)), wsq(0),
                  wsq(1), wsq(0), const((PLE, D)), any_spec],
        out_specs=(pl.BlockSpec((tm, 4 * D), lambda i: (i, 0)), tile, tile, const((8, D)), any_spec),
        out_shape=(jax.ShapeDtypeStruct((T, 7 * D), BF16), jax.ShapeDtypeStruct((T, D), BF16),
                   jax.ShapeDtypeStruct((T, D), BF16), jax.ShapeDtypeStruct((8, D), F32),
                   jax.ShapeDtypeStruct(gpack.shape, F32)),
        input_output_aliases={17: 4},
        scratch_shapes=[pltpu.VMEM((D, D), F32), pltpu.VMEM((D, D), F32), pltpu.VMEM((D, D), F32),
                        pltpu.SemaphoreType.DMA((4,))],
        compiler_params=_params(("arbitrary",)),
    )(dm, ya, yb, o, z, z, z, z, pw, y, rstd, ln_g, ln_b, wall_a, wall_a, wall_b, gppt, gpack)


def _conv_bwd(dc, z, wdw, dz, S, tm, copies, src, landing):
    T = dc.shape[0]
    nt = S // tm
    hb = tm // HALO
    nrows = T // HALO

    def body(dc_ref, hdc_ref, cv_ref, cg_ref, hcv_ref, hcg_ref, wdw_ref, dz_in, src_ref, dz_ref, gw_ref,
             land_ref, ubuf, dcbuf, dubuf, dwacc, shbuf, send_sems, recv_sems):
        b = pl.program_id(0)
        t = pl.program_id(1)

        @pl.when((b == 0) & (t == 0))
        def _():
            dwacc[...] = jnp.zeros_like(dwacc)
            for cp in copies(src_ref, land_ref, send_sems, recv_sems):
                cp.start()

        cv = cv_ref[...].astype(F32)
        sg = _sig(cg_ref[...].astype(F32))
        ubuf[HALO:HALO + tm, :] = cv * sg
        hu = hcv_ref[...].astype(F32) * _sig(hcg_ref[...].astype(F32))
        ubuf[0:HALO, :] = jnp.where(t > 0, hu, 0.0)
        ubuf[HALO + tm:HALO + tm + 8, :] = jnp.zeros((8, D), F32)
        dcbuf[0:tm, :] = dc_ref[...].astype(F32)
        dcbuf[tm:tm + HALO, :] = jnp.where(t < nt - 1, hdc_ref[...].astype(F32), 0.0)
        dcbuf[tm + HALO:tm + HALO + 8, :] = jnp.zeros((8, D), F32)

        def chunk(ci, carry):
            r0 = pl.multiple_of(ci * CONV_RC, CONV_RC)
            for lg in range(D // CONV_LC):
                l0 = lg * CONV_LC
                dubuf[pl.ds(r0, CONV_RC), pl.ds(l0, CONV_LC)] = _conv_taps(
                    wdw_ref, dcbuf, r0, l0, lambda k: CONV_K - 1 - k)
                dcc = dcbuf[pl.ds(r0, CONV_RC), pl.ds(l0, CONV_LC)]
                zero8 = jnp.zeros((8, CONV_LC), F32)
                dcz = jnp.concatenate([zero8, dcc, zero8], axis=0)
                for bb in range(8):
                    taps = [k for k in range(CONV_K) if (HALO - (CONV_K - 1) + k) % 8 == bb]
                    if not taps:
                        continue
                    rows = CONV_RC + (8 if bb else 0)
                    if bb:
                        shbuf[bb] = dcz[8 - bb:8 - bb + rows]
                    for k in taps:
                        a8 = HALO - (CONV_K - 1) + k - bb
                        dcs = shbuf[bb] if bb else dcc
                        prod = dcs * ubuf[pl.ds(r0 + a8, rows), pl.ds(l0, CONV_LC)]
                        part = prod[0:8]
                        for q in range(1, rows // 8):
                            part = part + prod[8 * q:8 * q + 8]
                        dwacc[8 * k:8 * k + 8, pl.ds(l0, CONV_LC)] += part
            return carry

        lax.fori_loop(0, tm // CONV_RC, chunk, 0)
        du = dubuf[...]
        dz_ref[:, 0:D] = (du * sg).astype(BF16)
        dz_ref[:, D:2 * D] = (du * cv * sg * (1.0 - sg)).astype(BF16)

        @pl.when((b == pl.num_programs(0) - 1) & (t == nt - 1))
        def _():
            for k in range(32):
                gw_ref[k:k + 1, :] = jnp.sum(dwacc[8 * k:8 * k + 8, :], axis=0, keepdims=True)
            cps = copies(src_ref, land_ref, send_sems, recv_sems)
            for cp in cps:
                cp.wait_recv()
            for cp in cps:
                cp.wait_send()

    def row(b, t):
        return b * nt + t

    def prev_halo(b, t):
        return jnp.maximum(row(b, t) * hb - 1, 0)

    def next_halo(b, t):
        return jnp.minimum((row(b, t) + 1) * hb, nrows - 1)

    return pl.pallas_call(
        body, name="conv_bwd", grid=(T // S, nt),
        in_specs=[pl.BlockSpec((tm, D), lambda b, t: (row(b, t), 0)),
                  pl.BlockSpec((HALO, D), lambda b, t: (next_halo(b, t), 0)),
                  pl.BlockSpec((tm, D), lambda b, t: (row(b, t), ZB_CVAL)),
                  pl.BlockSpec((tm, D), lambda b, t: (row(b, t), ZB_CGLU)),
                  pl.BlockSpec((HALO, D), lambda b, t: (prev_halo(b, t), ZB_CVAL)),
                  pl.BlockSpec((HALO, D), lambda b, t: (prev_halo(b, t), ZB_CGLU)),
                  pl.BlockSpec((32, D), lambda b, t: (0, 0)),
                  pl.BlockSpec(memory_space=pl.ANY), pl.BlockSpec(memory_space=pl.ANY)],
        out_specs=(pl.BlockSpec((tm, 2 * D), lambda b, t: (row(b, t), ZB_CVAL // 2)),
                   pl.BlockSpec((32, D), lambda b, t: (0, 0)), pl.BlockSpec(memory_space=pl.ANY)),
        out_shape=(jax.ShapeDtypeStruct(dz.shape, BF16), jax.ShapeDtypeStruct((32, D), F32), landing),
        input_output_aliases={7: 0},
        scratch_shapes=[pltpu.VMEM((tm + HALO + 8, D), F32), pltpu.VMEM((tm + HALO + 8, D), F32),
                        pltpu.VMEM((tm, D), F32), pltpu.VMEM((8 * 32, D), F32),
                        pltpu.VMEM((8, CONV_RC + 8, CONV_LC), F32), pltpu.SemaphoreType.DMA((3,)),
                        pltpu.SemaphoreType.DMA((3,))],
        compiler_params=_params(("arbitrary", "arbitrary")),
    )(dc, dc, z, z, z, z, wdw, dz, src)


def _attn_bwd(z, zkv, o, do, cos_t, sin_t, sinks, dz, S, tq, gpack, r1, landing):
    T = z.shape[0]
    nt = S // tq
    nq = tq // BLOCK
    n_steps = (T // S) * nt
    assert n_steps >= N_SHARDS
    half = gpack.shape[1] // 2

    def body(sink_ref, q_ref, kv_ref, hkv_ref, o_ref, do_ref, cos_ref, sin_ref, hcos_ref, hsin_ref, dz_in,
             g_ref, r1_ref, dq_ref, dkv_ref, gs_ref, land_ref, carry, dkacc, dvacc, gbuf, rbuf, csbuf, ld_sems,
             send_sems, recv_sems):
        b = pl.program_id(0)
        tt = pl.program_id(1)
        t = nt - 1 - tt
        step = b * nt + tt
        own = pl.ds(pl.multiple_of(lax.axis_index("c") * half, 32), half)

        def loads(s):
            return [pltpu.make_async_copy(g_ref.at[s, own], gbuf.at[s % 2], ld_sems.at[0, s % 2]),
                    pltpu.make_async_copy(r1_ref.at[s], rbuf.at[s % 2], ld_sems.at[1, s % 2])]

        for s in range(N_SHARDS):
            @pl.when(step == s)
            def _(s=s):
                if s == 0:
                    gs_ref[...] = jnp.zeros_like(gs_ref)
                for cp in (loads(0) if s == 0 else []) + (loads(s + 1) if s + 1 < N_SHARDS else []):
                    cp.start()
                for cp in loads(s):
                    cp.wait()
                csbuf[s] = (gbuf[s % 2] + rbuf[s % 2]).astype(BF16)
                if s == N_SHARDS - 1:
                    for cp in _chip_sum_copies(csbuf, land_ref, send_sems, recv_sems):
                        cp.start()

        @pl.when(tt == 0)
        def _():
            carry[...] = jnp.zeros_like(carry)

        cos = cos_ref[...]
        sin = sin_ref[...]
        pswap = _swap_matrix()
        kv = jnp.concatenate([hkv_ref[...], kv_ref[...]], axis=0)
        cos_k = jnp.concatenate([hcos_ref[...], cos], axis=0)
        sin_k = jnp.concatenate([hsin_ref[...], sin], axis=0)
        kx = _kv_variants(_rope(kv[:, :BLOCK], cos_k, sin_k, pswap))
        vx = _kv_variants(kv[:, BLOCK:].astype(F32))
        band, sj = _band_mask(4)
        lo = lax.broadcasted_iota(jnp.int32, (4 * BLOCK, BLOCK), 1) < HEAD_DIM
        ones = jnp.ones((2 * BLOCK, 2 * BLOCK), BF16)
        qs = [(_rope(q_ref[:, 128 * hp:128 * hp + 128], cos, sin, pswap) * 0.125).astype(BF16)
              for hp in range(8)]
        dkacc[...] = jnp.zeros_like(dkacc)
        dvacc[...] = jnp.zeros_like(dvacc)
        gsum = jnp.zeros((1, BLOCK), F32)
        hlane = lax.broadcasted_iota(jnp.int32, (1, BLOCK), 1)
        for n in range(nq):
            first = (t == 0) & (n == 0)
            valid = band & (jnp.logical_not(first) | (sj >= BLOCK))
            r0 = n * BLOCK
            for g in range(2):
                cols = [slice(128 * (4 * g + j), 128 * (4 * g + j) + 128) for j in range(4)]
                lhs = jnp.concatenate([qs[4 * g + j][r0:r0 + BLOCK] for j in range(4)], axis=0)
                dov = jnp.concatenate([do_ref[r0:r0 + BLOCK, cs] for cs in cols], axis=0)
                prod = dov.astype(F32) * jnp.concatenate(
                    [o_ref[r0:r0 + BLOCK, cs] for cs in cols], axis=0).astype(F32)
                lhs_t = lhs.T
                dov_t = dov.T
                dq = jnp.zeros((4 * BLOCK, BLOCK), F32)
                dk_t = jnp.zeros((HEAD_DIM, 2 * BLOCK), F32)
                dv_t = jnp.zeros((HEAD_DIM, 2 * BLOCK), F32)
                for e in range(2):
                    kw = kx[g][e][r0:r0 + 2 * BLOCK]
                    vw = vx[g][e][r0:r0 + 2 * BLOCK]
                    s = _mm_nt(lhs, kw)
                    p, psink = _softmax_sink(s, valid, _sink_rep(sink_ref, g, e))
                    pe = jnp.where(lo if e == 0 else jnp.logical_not(lo), prod, 0.0)
                    pe_hi = pe.astype(BF16)
                    pe_lo = (pe - pe_hi.astype(F32)).astype(BF16)
                    delta = _mm(jnp.concatenate([pe_hi, pe_lo], axis=1), ones)
                    ds = (p * (_mm_nt(dov, vw) - delta)).astype(BF16)
                    dq = dq + _mm(ds, kw)
                    dims = slice(HEAD_DIM * e, HEAD_DIM * (e + 1))
                    dk_t = dk_t + _mm(lhs_t[dims], ds)
                    dv_t = dv_t + _mm(dov_t[dims], p.astype(BF16))
                    gs = -psink * delta[:, 0:BLOCK]
                    for j in range(4):
                        tot = jnp.sum(gs[j * BLOCK:(j + 1) * BLOCK], axis=0, keepdims=True)
                        gsum = gsum + jnp.where(hlane == 8 * g + 2 * j + e, tot, 0.0)
                dkacc[HEAD_DIM * g:HEAD_DIM * (g + 1), r0:r0 + 2 * BLOCK] += dk_t
                dvacc[HEAD_DIM * g:HEAD_DIM * (g + 1), r0:r0 + 2 * BLOCK] += dv_t
                for j in range(4):
                    dqj = _rope_f32(dq[j * BLOCK:(j + 1) * BLOCK] * 0.125, cos[r0:r0 + BLOCK],
                                    -sin[r0:r0 + BLOCK], pswap)
                    dq_ref[r0:r0 + BLOCK, cols[j]] = dqj.astype(BF16)
        gs_ref[0:1, :] += gsum
        dk_all = dkacc[...]
        dv_all = dvacc[...]
        dk_last = dk_all[:, tq:tq + BLOCK] + carry[0:BLOCK, :]
        dv_last = dv_all[:, tq:tq + BLOCK] + carry[BLOCK:2 * BLOCK, :]
        carry[0:BLOCK, :] = dk_all[:, 0:BLOCK]
        carry[BLOCK:2 * BLOCK, :] = dv_all[:, 0:BLOCK]
        if nq > 1:
            dk_tile = jnp.concatenate([dk_all[:, BLOCK:tq], dk_last], axis=1)
            dv_tile = jnp.concatenate([dv_all[:, BLOCK:tq], dv_last], axis=1)
        else:
            dk_tile, dv_tile = dk_last, dv_last
        dkv_ref[:, 0:BLOCK] = _rope_f32(dk_tile.T, cos, -sin, pswap).astype(BF16)
        dkv_ref[:, BLOCK:2 * BLOCK] = dv_tile.T.astype(BF16)

        @pl.when(step == n_steps - 1)
        def _():
            cps = _chip_sum_copies(csbuf, land_ref, send_sems, recv_sems)
            for cp in cps:
                cp.wait_recv()
            for cp in cps:
                cp.wait_send()

    def row(b, tt):
        return b * nt + (nt - 1 - tt)

    def halo(b, tt):
        return jnp.maximum(row(b, tt) * nq - 1, 0)

    tile = pl.BlockSpec((tq, D), lambda b, tt: (row(b, tt), 0))
    return pl.pallas_call(
        body, name="attn_bwd", grid=(T // S, nt),
        in_specs=[pl.BlockSpec(memory_space=pltpu.SMEM),
                  pl.BlockSpec((tq, D), lambda b, tt: (row(b, tt), ZB_Q)),
                  pl.BlockSpec((tq, 2 * BLOCK), lambda b, tt: (row(b, tt), 0)),
                  pl.BlockSpec((BLOCK, 2 * BLOCK), lambda b, tt: (halo(b, tt), 0)),
                  tile, tile,
                  pl.BlockSpec((tq, BLOCK), lambda b, tt: (row(b, tt), 0)),
                  pl.BlockSpec((tq, BLOCK), lambda b, tt: (row(b, tt), 0)),
                  pl.BlockSpec((BLOCK, BLOCK), lambda b, tt: (halo(b, tt), 0)),
                  pl.BlockSpec((BLOCK, BLOCK), lambda b, tt: (halo(b, tt), 0)),
                  pl.BlockSpec(memory_space=pl.ANY), pl.BlockSpec(memory_space=pl.ANY),
                  pl.BlockSpec(memory_space=pl.ANY)],
        out_specs=(pl.BlockSpec((tq, D), lambda b, tt: (row(b, tt), ZB_Q)),
                   pl.BlockSpec((tq, 2 * BLOCK), lambda b, tt: (row(b, tt), 0)),
                   pl.BlockSpec((8, BLOCK), lambda b, tt: (0, 0)), pl.BlockSpec(memory_space=pl.ANY)),
        out_shape=(jax.ShapeDtypeStruct(dz.shape, BF16), jax.ShapeDtypeStruct((T, 2 * BLOCK), BF16),
                   jax.ShapeDtypeStruct((8, BLOCK), F32), landing),
        input_output_aliases={10: 0},
        scratch_shapes=[pltpu.VMEM((2 * BLOCK, BLOCK), F32), pltpu.VMEM((BLOCK, tq + BLOCK), F32),
                        pltpu.VMEM((BLOCK, tq + BLOCK), F32), pltpu.VMEM((2, half, D), F32),
                        pltpu.VMEM((2, half, D), F32), pltpu.VMEM((N_SHARDS, half, D), BF16),
                        pltpu.SemaphoreType.DMA((2, 2)), pltpu.SemaphoreType.DMA((3,)),
                        pltpu.SemaphoreType.DMA((3,))],
        compiler_params=_params(("arbitrary", "arbitrary")),
    )(sinks, z, zkv, zkv, o, do, cos_t, sin_t, cos_t, sin_t, dz, gpack, r1)


def _dh(dz, dz_kv, wt, x, dx1, ln_pre, tm, copies, src, landing):
    T = x.shape[0]
    ntiles = T // tm
    nsem = 3

    def body(dz_ref, kv_ref, wt_ref, x_ref, dx1_ref, g_ref, src_ref, gx_ref, glp_ref, land_ref, wbuf, send_sems,
             recv_sems, wsem):
        i = pl.program_id(0)

        @pl.when(i == 0)
        def _():
            glp_ref[...] = jnp.zeros_like(glp_ref)
            for cp in copies(src_ref, land_ref, send_sems, recv_sems):
                cp.start()
            load = pltpu.make_async_copy(wt_ref, wbuf, wsem)
            load.start()
            load.wait()

        dh = _mm(dz_ref[...], wbuf[0:ZKV, :]) + _mm(kv_ref[...], wbuf[ZKV:IN_WIDTH, :])
        xv = x_ref[...]
        r = lax.rsqrt(jnp.mean(xv * xv, axis=-1, keepdims=True) + EPS)
        xr = xv * r
        glp_ref[...] += jnp.sum(dh * xr, axis=0, keepdims=True)
        a = dh * g_ref[...]
        gx_ref[...] = dx1_ref[...] + r * (a - xr * jnp.mean(a * xr, axis=-1, keepdims=True))

        @pl.when(i == ntiles - 1)
        def _():
            cps = copies(src_ref, land_ref, send_sems, recv_sems)
            for cp in cps:
                cp.wait_recv()
            for cp in cps:
                cp.wait_send()

    tile = pl.BlockSpec((tm, D), lambda i: (i, 0))
    any_spec = pl.BlockSpec(memory_space=pl.ANY)
    return pl.pallas_call(
        body, name="dh", grid=(ntiles,),
        in_specs=[pl.BlockSpec((tm, ZKV), lambda i: (i, 0)), pl.BlockSpec((tm, 2 * BLOCK), lambda i: (i, 0)),
                  any_spec, tile, tile, pl.BlockSpec((1, D), lambda i: (0, 0)), any_spec],
        out_specs=(tile, pl.BlockSpec((1, D), lambda i: (0, 0)), any_spec),
        out_shape=(jax.ShapeDtypeStruct((T, D), F32), jax.ShapeDtypeStruct((1, D), F32), landing),
        scratch_shapes=[pltpu.VMEM((IN_WIDTH, D), BF16), pltpu.SemaphoreType.DMA((nsem,)),
                        pltpu.SemaphoreType.DMA((nsem,)), pltpu.SemaphoreType.DMA],
        compiler_params=_params(("arbitrary",)),
    )(dz, dz_kv, wt, x, dx1, ln_pre, src)


def _gwt(dz, dz_kv, h, tt):
    T = dz.shape[0]
    nt = T // tt
    last = nt - 1
    kv = 2 * BLOCK
    half = WIN_SHARD // 2

    def body(dz_ref, dzkv_ref, h_ref, gpack_ref, r1_ref, hbuf, acc, hsems, sems, send_sems, recv_sems):
        j = pl.program_id(0)
        t = pl.program_id(1)
        slot = j % 2
        rows = pl.ds(pl.multiple_of(t * tt, tt), tt)
        x, y, c = _coords()

        def exchange(jj):
            wall0, n_rows = (WT0 + jj * D, D) if jj < 7 else (WT0 + ZKV, kv)
            for _, n, s, pr in _wall_segments(wall0, n_rows):
                for hb in range(2):
                    lo, hi = max(pr, hb * half), min(pr + n, (hb + 1) * half)
                    if lo < hi:
                        cp = pltpu.make_async_remote_copy(
                            src_ref=gpack_ref.at[s, pl.ds(lo, hi - lo)],
                            dst_ref=r1_ref.at[s, pl.ds(lo - hb * half, hi - lo)], send_sem=send_sems.at[0],
                            recv_sem=recv_sems.at[0], device_id=(x, y, 1 - c), device_id_type=MESH)
                        pl.when(c == 1 - hb)(cp.start)

        def h_load(i):
            return pltpu.make_async_copy(h_ref.at[pl.ds(i * tt, tt)], hbuf.at[pl.ds(i * tt, tt)], hsems.at[i])

        @pl.when((j == 0) & (t == 0))
        def _():
            for i in range(nt):
                h_load(i).start()

        for i in range(nt):
            pl.when((j == 0) & (t == i))(h_load(i).wait)

        @pl.when((j < 7) & (t == 0))
        def _():
            acc[slot] = _mm_tn(dz_ref[...], hbuf[rows, :])

        @pl.when((j < 7) & (t > 0))
        def _():
            acc[slot] += _mm_tn(dz_ref[...], hbuf[rows, :])

        @pl.when((j == 7) & (t == 0))
        def _():
            acc[1, 0:kv, :] = _mm_tn(dzkv_ref[...], hbuf[rows, :])

        @pl.when((j == 7) & (t > 0))
        def _():
            acc[1, 0:kv, :] += _mm_tn(dzkv_ref[...], hbuf[rows, :])

        def block_total(sl):
            return pltpu.make_async_copy(acc.at[sl], gpack_ref.at[0, pl.ds(0, D)], sems.at[sl])

        for jj in range(8):
            @pl.when((t == last) & (j == jj))
            def _(jj=jj):
                if jj >= 1:
                    block_total((jj - 1) % 2).wait()
                    exchange(jj - 1)
                if jj == 7:
                    _flush_to_pack(acc.at[1, pl.ds(0, kv)], gpack_ref, WT0 + ZKV, sems.at[1])
                    exchange(7)
                    whole = _exchange_copies(gpack_ref, r1_ref, send_sems, recv_sems)[0]
                    whole.wait_recv()
                    whole.wait_send()
                else:
                    for cp in _pack_copies(acc.at[jj % 2], gpack_ref, WT0 + jj * D, sems.at[jj % 2]):
                        cp.start()

    any_spec = pl.BlockSpec(memory_space=pl.ANY)
    return pl.pallas_call(
        body, name="gwt", grid=(8, nt),
        in_specs=[pl.BlockSpec((tt, D), lambda j, t: (jnp.where(j == 7, last, t), jnp.minimum(j, 6))),
                  pl.BlockSpec((tt, kv), lambda j, t: (jnp.where(j == 7, t, 0), 0)), any_spec],
        out_specs=(any_spec, any_spec),
        out_shape=(jax.ShapeDtypeStruct((N_SHARDS, WIN_SHARD, D), F32),
                   jax.ShapeDtypeStruct((N_SHARDS, half, D), F32)),
        scratch_shapes=[pltpu.VMEM((T, D), BF16), pltpu.VMEM((2, D, D), F32), pltpu.SemaphoreType.DMA((nt,)),
                        pltpu.SemaphoreType.DMA((2,)), pltpu.SemaphoreType.DMA((1,)),
                        pltpu.SemaphoreType.DMA((1,))],
        compiler_params=_params(("arbitrary", "arbitrary")),
    )(dz, dz_kv, h)


_BC1 = 1.0 - ADAM_B1 ** ADAM_STEP
_BC2 = 1.0 - ADAM_B2 ** ADAM_STEP


def _adamw_math(w, g, m, v):
    m = ADAM_B1 * m + (1.0 - ADAM_B1) * g
    v = ADAM_B2 * v + (1.0 - ADAM_B2) * (g * g)
    delta = -ADAM_LR * ((m / _BC1) / (jnp.sqrt(v / _BC2) + ADAM_EPS) + ADAM_WD * w)
    return delta, m, v


def _adamw_rows(g, w, m, v, rows, name):
    R, C = w.shape

    def body(g_ref, w_ref, m_ref, v_ref, go_ref, d_ref, nm_ref, nv_ref):
        gv = g_ref[...]
        d, nm, nv = _adamw_math(w_ref[...], gv, m_ref[...], v_ref[...])
        go_ref[...] = gv
        d_ref[...] = d
        nm_ref[...] = nm
        nv_ref[...] = nv

    spec = pl.BlockSpec((rows, C), lambda i: (i, 0))
    shp = jax.ShapeDtypeStruct((R, C), F32)
    return pl.pallas_call(
        body, name=name, grid=(R // rows,), in_specs=[spec] * 4, out_specs=(spec,) * 4,
        out_shape=(shp,) * 4, compiler_params=_params(("arbitrary",)),
    )(g, w, m, v)


def _adamw_square(gfin, ws, ms, vs):
    rb = 64
    nb = SQ_SHARD // rb

    def body(*refs):
        g_refs = refs[0:5]
        w_refs, m_refs, v_refs = refs[5:10], refs[10:15], refs[15:20]
        outs = refs[20:]
        for k in range(5):
            gk = g_refs[k][...]
            d, nm, nv = _adamw_math(w_refs[k][...], gk, m_refs[k][...], v_refs[k][...])
            outs[4 * k][...] = gk
            outs[4 * k + 1][...] = d
            outs[4 * k + 2][...] = nm
            outs[4 * k + 3][...] = nv

    spec = pl.BlockSpec((rb, D), lambda i: (i, 0))
    gspecs = [pl.BlockSpec((rb, D), lambda i, k=k: (SQ_SHARD * k // rb + i, 0)) for k in range(5)]
    shp = jax.ShapeDtypeStruct((SQ_SHARD, D), F32)
    res = pl.pallas_call(
        body, name="adamw_square", grid=(nb,), in_specs=gspecs + [spec] * 15, out_specs=(spec,) * 20,
        out_shape=(shp,) * 20, compiler_params=_params(("arbitrary",)),
    )(*([gfin] * 5), *ws, *ms, *vs)
    return [tuple(res[4 * k:4 * k + 4]) for k in range(5)]


def _adamw_small(gs, ws, ms, vs):
    n = len(gs)

    def body(*refs):
        outs = refs[4 * n:]
        for k in range(n):
            d, nm, nv = _adamw_math(refs[n + k][...], refs[k][...], refs[2 * n + k][...],
                                    refs[3 * n + k][...])
            outs[3 * k][...] = d
            outs[3 * k + 1][...] = nm
            outs[3 * k + 2][...] = nv

    vm = pl.BlockSpec(memory_space=pltpu.VMEM)
    shapes = []
    for w in ws:
        shapes += [jax.ShapeDtypeStruct(w.shape, F32)] * 3
    res = pl.pallas_call(
        body, name="adamw_small", in_specs=[vm] * (4 * n), out_specs=(vm,) * (3 * n),
        out_shape=tuple(shapes),
    )(*gs, *ws, *ms, *vs)
    return [tuple(res[3 * k:3 * k + 3]) for k in range(n)]


def _rope_constants():
    half = ROPE_DIM // 2
    inv = jnp.power(ROPE_THETA, -jnp.arange(0, ROPE_DIM, 2, dtype=F32) / ROPE_DIM)
    freq = jnp.concatenate([inv, jnp.zeros((ROPE_ROWS - half,), F32)]).reshape(ROPE_ROWS, 1)
    spread = np.zeros((3, ROPE_ROWS, BLOCK), np.float32)
    for lane in range(BLOCK):
        d = lane % HEAD_DIM
        if d < ROPE_DIM:
            spread[0, d % half, lane] = 1.0
            spread[1, d % half, lane] = -1.0 if d < half else 1.0
        else:
            spread[2, 0, lane] = 1.0
    return freq, jnp.asarray(spread, BF16)


def kernel(x, p, positions, w_in, ln_pre, ln_post, w_dw, b_dw, conv_ln_g, conv_ln_b, w_pw, sinks, w_br_conv, w_br_attn, w_out, w_ple_gate, w_ple_proj, loss_target, m_w_in, m_ln_pre, m_ln_post, m_w_dw, m_b_dw, m_conv_ln_g, m_conv_ln_b, m_w_pw, m_sinks, m_w_br_conv, m_w_br_attn, m_w_out, m_w_ple_gate, m_w_ple_proj, v_w_in, v_ln_pre, v_ln_post, v_w_dw, v_b_dw, v_conv_ln_g, v_conv_ln_b, v_w_pw, v_sinks, v_w_br_conv, v_w_br_attn, v_w_out, v_w_ple_gate, v_w_ple_proj):
    nb, S, _ = x.shape
    T = nb * S
    xc = lax.axis_index("x")
    yc = lax.axis_index("y")
    cc = lax.axis_index("c")
    shard = 2 * xc + yc

    sq_w = (w_pw, w_br_conv, w_br_attn, w_out, w_ple_gate)
    wdw_shard = jnp.pad(w_dw[0], ((0, 1), (0, 0)))
    x2 = x.reshape(T, D)

    tgt = loss_target.reshape(T, D)
    p2 = p.reshape(T, PLE)
    sinks1 = sinks.reshape(N_HEADS)

    tm = min(TILE_TOKEN, S)
    tc = min(TILE_CONV, S)
    tq = min(TILE_ATTN, S)

    z, zkv, wt, wdw_all, h, cos_t, sin_t = _inproj(
        x2, ln_pre, positions.astype(F32).reshape(1, T), *_rope_constants(), w_in[0].T.astype(BF16), wdw_shard,
        min(TILE_PROJ, T // 2))
    wdw = jnp.concatenate([wdw_all[s] for s in range(N_SHARDS)], axis=1)
    sq_shards = [w[0].astype(BF16) for w in sq_w] + [w_ple_proj[0].T.reshape(WPP_SHARD, D).astype(BF16)]
    o, wall_a = _attn_fwd(z, zkv, cos_t, sin_t, sinks1, S, tq, GROUP_CONV, sq_shards[0:2])
    ya, y, rstd, pw, wall_b, wppf = _conv_fwd(z, wdw, b_dw, conv_ln_g, conv_ln_b, wall_a, S, tc, GROUP_TAIL,
                                              sq_shards[2:])
    wppt = wppf.reshape(D, PLE)
    loss_p, dx1, dm, yb, g_ln_post, gsq, gw_ppt = _tail_a(x2, tgt, p2, o, ya, z, ln_post, wall_b, wppt, tm)

    cidx = jnp.reshape(cc, (1,)).astype(jnp.int32)
    scidx = jnp.stack([shard, cc]).astype(jnp.int32)

    def landing(pack, n, dtype):
        return jax.ShapeDtypeStruct((n, pack.shape[1] // 2, D), dtype)

    dz, do, dc, gvec, gsq = _tail_b(dm, ya, yb, o, z, pw, y, rstd, conv_ln_g, conv_ln_b, wall_a, wall_b,
                                    gw_ppt.reshape(PLE, D), gsq, tm)
    dz, g_wdw, r1_sq = _conv_bwd(dc, z, wdw, dz, S, tc, _exchange_copies, gsq, landing(gsq, N_SHARDS, F32))
    dz, dkv, g_sinks, r2_sq = _attn_bwd(z, zkv, o, do, cos_t, sin_t, sinks1, dz, S, tq, gsq, r1_sq,
                                        landing(gsq, 3, BF16))
    gwt_pack, r1_wt = _gwt(dz, dkv, h, min(2 * TILE_PROJ, T))
    cs_wt = _chip_sum(cidx, gwt_pack, r1_wt, "chip_sum_wt")
    gx, g_ln_pre, r2_wt = _dh(dz, dkv, wt, x2, dx1, ln_pre, min(TILE_RESIDENT, T // 2), _chip_sum_copies, cs_wt,
                              landing(gwt_pack, 3, BF16))
    row37 = jnp.concatenate([g_sinks[0:1, 0:N_HEADS], loss_p, jnp.zeros((1, D - N_HEADS - 1), F32)], axis=1)
    vec = jnp.concatenate([g_wdw, g_ln_pre, g_ln_post, gvec[2:3], gvec[0:1], gvec[1:2], row37,
                           jnp.zeros((VEC_ROWS - 38, D), F32)], axis=0)
    gfin_wt, gfin_sq, tot = _finish_reduce(_final_half(scidx, gwt_pack, r1_wt, r2_wt, "final_half_wt"),
                                           _final_half(scidx, gsq, r1_sq, r2_sq, "final_half_sq"), vec)

    g_w_in, d_w_in, nm_w_in, nv_w_in = [a.T for a in _adamw_rows(
        gfin_wt, w_in[0].T, m_w_in[0].T, v_w_in[0].T, WIN_SHARD // 8, "adamw_w_in")]
    g_w_in = g_w_in[None]
    sq_m = (m_w_pw, m_w_br_conv, m_w_br_attn, m_w_out, m_w_ple_gate)
    sq_v = (v_w_pw, v_w_br_conv, v_w_br_attn, v_w_out, v_w_ple_gate)
    sq_res = _adamw_square(gfin_sq, [w[0] for w in sq_w], [m[0] for m in sq_m], [v[0] for v in sq_v])
    g_wpp = gfin_sq[5 * SQ_SHARD:SQ_PACK].reshape(PLE, PLE).T
    g_dw_all = tot[0:CONV_K]
    g_dw = lax.dynamic_slice_in_dim(g_dw_all, shard * PLE, PLE, axis=1)
    small_g = [g_wpp, g_dw, tot[32:33], tot[33:34], tot[34:35], tot[35:36], tot[36:37],
               tot[37:38, 0:N_HEADS]]
    small_w = [w_ple_proj[0], w_dw[0], ln_pre, ln_post, b_dw, conv_ln_g, conv_ln_b, sinks]
    small_m = [m_w_ple_proj[0], m_w_dw[0], m_ln_pre, m_ln_post, m_b_dw, m_conv_ln_g, m_conv_ln_b, m_sinks]
    small_v = [v_w_ple_proj[0], v_w_dw[0], v_ln_pre, v_ln_post, v_b_dw, v_conv_ln_g, v_conv_ln_b, v_sinks]
    small = _adamw_small(small_g, small_w, small_m, small_v)

    loss = tot[37, N_HEADS]
    grads = [g_w_in, small_g[2], small_g[3], g_dw[None], small_g[4], small_g[5], small_g[6],
             sq_res[0][0][None], small_g[7], sq_res[1][0][None], sq_res[2][0][None], sq_res[3][0][None],
             sq_res[4][0][None], g_wpp[None]]

    def triple(i):
        w_in_t = (d_w_in[None], nm_w_in[None], nv_w_in[None])
        sq = lambda k: tuple(a[None] for a in sq_res[k][1:4])
        sm = lambda k, lead: tuple(a[None] if lead else a for a in small[k])
        return [w_in_t[i], sm(2, False)[i], sm(3, False)[i], sm(1, True)[i], sm(4, False)[i],
                sm(5, False)[i], sm(6, False)[i], sq(0)[i], sm(7, False)[i], sq(1)[i], sq(2)[i], sq(3)[i],
                sq(4)[i], sm(0, True)[i]]

    return (loss, gx.reshape(nb, S, D), *grads, *triple(0), *triple(1), *triple(2))
```
